```python
import jax, jax.numpy as jnp
from jax import lax
import numpy as np

D_MODEL = 1024
BATCH = 16
SEQ = 2048
DEPTH = 1

D_MIX = D_MODEL
ATTN_HEADS = 8
HEAD_DIM = 64
D_ATTN = ATTN_HEADS * HEAD_DIM
D_CONV = D_MIX - D_ATTN
CONV_GROUPS = 8
CONV_K = 31
DILATED_PATTERNS = ((128, 1), (512, 4), (2048, 16))
BLK = 128
D_FF = 2816
D_IN = 3 * D_ATTN + 2 * D_CONV
EPS = 1e-6

kernel_name = "hybrid_dilated_attn_conformer_conv_macaron"


def _rms(x, g):
    xf = x.astype(jnp.float32)
    y = xf * lax.rsqrt(jnp.mean(xf * xf, axis=-1, keepdims=True) + EPS)
    return (y * g.astype(jnp.float32)).astype(x.dtype)


def _layernorm(x, g, b):
    xf = x.astype(jnp.float32)
    mu = jnp.mean(xf, axis=-1, keepdims=True)
    var = jnp.mean(jnp.square(xf - mu), axis=-1, keepdims=True)
    y = (xf - mu) * lax.rsqrt(var + EPS)
    return (y * g.astype(jnp.float32) + b.astype(jnp.float32)).astype(x.dtype)


def _swiglu(x, w_gate, w_up, w_down):
    return (jax.nn.silu(x @ w_gate) * (x @ w_up)) @ w_down


def _banded_causal_attn(q, k, v, steps):
    Bp, L, H, Dh = q.shape
    N = L // BLK
    qb = q.reshape(Bp, N, BLK, H, Dh)
    kb = k.reshape(Bp, N, BLK, H, Dh)
    vb = v.reshape(Bp, N, BLK, H, Dh)

    def with_prev(a):
        prev = jnp.concatenate([jnp.zeros_like(a[:, :1]), a[:, :-1]], axis=1)
        return jnp.concatenate([prev, a], axis=2)

    kk, vv = with_prev(kb), with_prev(vb)
    s = jnp.einsum('bnqhd,bnkhd->bnhqk', qb, kk,
                   preferred_element_type=jnp.float32) * (Dh ** -0.5)
    qi = jnp.arange(BLK)[:, None]
    ci = jnp.arange(2 * BLK)[None, :]
    dist = BLK + qi - ci
    band = (dist >= 0) & (dist <= steps)
    first = (jnp.arange(N) == 0)[:, None, None] & (ci < BLK)[None]
    mask = band[None] & jnp.logical_not(first)
    s = jnp.where(mask[None, :, None], s, jnp.float32(-1e30))
    m = jnp.max(s, axis=-1, keepdims=True)
    p = jnp.exp(s - m)
    l = jnp.sum(p, axis=-1, keepdims=True)
    o = jnp.einsum('bnhqk,bnkhd->bnqhd', (p / l).astype(v.dtype), vv,
                   preferred_element_type=jnp.float32)
    lse = (m + jnp.log(l))[..., 0]
    return o.reshape(Bp, L, H, Dh), lse.transpose(0, 1, 3, 2).reshape(Bp, L, H)


def _dilated_causal_attn(q, k, v, window, dilation):
    B, S, H, Dh = q.shape
    span = dilation * BLK
    S_pad = -(-S // span) * span
    L = S_pad // dilation

    def strided(a):
        a = jnp.pad(a, ((0, 0), (0, S_pad - S), (0, 0), (0, 0)))
        return a.reshape(B, L, dilation, H, Dh).transpose(0, 2, 1, 3, 4).reshape(B * dilation, L, H, Dh)

    o, lse = _banded_causal_attn(strided(q), strided(k), strided(v), window // dilation)
    o = o.reshape(B, dilation, L, H, Dh).transpose(0, 2, 1, 3, 4).reshape(B, S_pad, H, Dh)[:, :S]
    lse = lse.reshape(B, dilation, L, H).transpose(0, 2, 1, 3).reshape(B, S_pad, H)[:, :S]
    return o, lse


def _conformer_conv(a, gate, conv_w, conv_b, ln_g, ln_b):
    glu = a * jax.nn.sigmoid(gate)
    y = lax.conv_general_dilated(
        glu, conv_w[:, None, :], window_strides=(1,), padding=[(CONV_K - 1, 0)],
        dimension_numbers=('NWC', 'WIO', 'NWC'), feature_group_count=D_CONV)
    y = y + conv_b
    return jax.nn.silu(_layernorm(y, ln_g, ln_b))


def _fwd_setup_inputs(seed: int = 0) -> dict:
    key = jax.random.key(seed)
    ks = jax.random.split(key, 20)
    L = DEPTH

    def nrm(k, shape, scale):
        return jax.random.normal(k, shape, jnp.float32) * scale

    def gain(k, shape):
        return 1.0 + 0.02 * jax.random.normal(k, shape, jnp.float32)

    return {
        "x": jax.random.normal(ks[0], (BATCH, SEQ, D_MODEL), jnp.float32),
        "ffn1_norm": gain(ks[1], (L, D_MODEL)),
        "ffn1_w_gate": nrm(ks[2], (L, D_MODEL, D_FF), D_MODEL ** -0.5),
        "ffn1_w_up": nrm(ks[3], (L, D_MODEL, D_FF), D_MODEL ** -0.5),
        "ffn1_w_down": nrm(ks[4], (L, D_FF, D_MODEL), D_FF ** -0.5),
        "mix_norm": gain(ks[5], (L, D_MODEL)),
        "w_in": nrm(ks[6], (L, D_MODEL, D_IN), D_MODEL ** -0.5),
        "q_norm": gain(ks[7], (L, HEAD_DIM)),
        "k_norm": gain(ks[8], (L, HEAD_DIM)),
        "conv_w": nrm(ks[9], (L, CONV_K, D_CONV), CONV_K ** -0.5),
        "conv_b": nrm(ks[10], (L, D_CONV), 0.02),
        "conv_ln_g": gain(ks[11], (L, D_CONV)),
        "conv_ln_b": nrm(ks[12], (L, D_CONV), 0.02),
        "w_out": nrm(ks[13], (L, D_MIX, D_MODEL), D_MIX ** -0.5),
        "ffn2_norm": gain(ks[14], (L, D_MODEL)),
        "ffn2_w_gate": nrm(ks[15], (L, D_MODEL, D_FF), D_MODEL ** -0.5),
        "ffn2_w_up": nrm(ks[16], (L, D_MODEL, D_FF), D_MODEL ** -0.5),
        "ffn2_w_down": nrm(ks[17], (L, D_FF, D_MODEL), D_FF ** -0.5),
    }


def _fwd_reference(x, ffn1_norm, ffn1_w_gate, ffn1_w_up, ffn1_w_down, mix_norm, w_in, q_norm, k_norm,
              conv_w, conv_b, conv_ln_g, conv_ln_b, w_out, ffn2_norm, ffn2_w_gate, ffn2_w_up,
              ffn2_w_down):
    B, S, _ = x.shape
    h = x
    for l in range(DEPTH):
        h = h + 0.5 * _swiglu(_rms(h, ffn1_norm[l]), ffn1_w_gate[l], ffn1_w_up[l], ffn1_w_down[l])

        u = _rms(h, mix_norm[l]) @ w_in[l]
        q, k, v, ca, cg = jnp.split(
            u, np.cumsum([D_ATTN, D_ATTN, D_ATTN, D_CONV]).tolist(), axis=-1)

        q = _rms(q.reshape(B, S, ATTN_HEADS, HEAD_DIM), q_norm[l])
        k = _rms(k.reshape(B, S, ATTN_HEADS, HEAD_DIM), k_norm[l])
        v = v.reshape(B, S, ATTN_HEADS, HEAD_DIM)
        outs, lses = [], []
        for window, dilation in DILATED_PATTERNS:
            o, lse = _dilated_causal_attn(q, k, v, window, dilation)
            outs.append(o)
            lses.append(lse)
        wts = jax.nn.softmax(jnp.stack(lses, axis=0), axis=0)
        attn = jnp.sum(wts[..., None] * jnp.stack(outs, axis=0), axis=0)
        attn = attn.reshape(B, S, D_ATTN).astype(h.dtype)

        conv = _conformer_conv(ca, cg, conv_w[l], conv_b[l], conv_ln_g[l], conv_ln_b[l])

        h = h + jnp.concatenate([attn, conv], axis=-1) @ w_out[l]

        h = h + 0.5 * _swiglu(_rms(h, ffn2_norm[l]), ffn2_w_gate[l], ffn2_w_up[l], ffn2_w_down[l])
    return h


import jax as _jax
import jax.numpy as _jnp

TWIN_FORMAT = 'train_step'
FWD_PARAMS = ['x', 'ffn1_norm', 'ffn1_w_gate', 'ffn1_w_up', 'ffn1_w_down', 'mix_norm', 'w_in', 'q_norm', 'k_norm', 'conv_w', 'conv_b', 'conv_ln_g', 'conv_ln_b', 'w_out', 'ffn2_norm', 'ffn2_w_gate', 'ffn2_w_up', 'ffn2_w_down']
TWIN_WEIGHTS = ['ffn1_norm', 'ffn1_w_gate', 'ffn1_w_up', 'ffn1_w_down', 'mix_norm', 'w_in', 'q_norm', 'k_norm', 'conv_w', 'conv_b', 'conv_ln_g', 'conv_ln_b', 'w_out', 'ffn2_norm', 'ffn2_w_gate', 'ffn2_w_up', 'ffn2_w_down']
TWIN_DIFF_INPUT = 'x'
TWIN_INPUTS = ['x', 'ffn1_norm', 'ffn1_w_gate', 'ffn1_w_up', 'ffn1_w_down', 'mix_norm', 'w_in', 'q_norm', 'k_norm', 'conv_w', 'conv_b', 'conv_ln_g', 'conv_ln_b', 'w_out', 'ffn2_norm', 'ffn2_w_gate', 'ffn2_w_up', 'ffn2_w_down', 'loss_target', 'm_ffn1_norm', 'm_ffn1_w_gate', 'm_ffn1_w_up', 'm_ffn1_w_down', 'm_mix_norm', 'm_w_in', 'm_q_norm', 'm_k_norm', 'm_conv_w', 'm_conv_b', 'm_conv_ln_g', 'm_conv_ln_b', 'm_w_out', 'm_ffn2_norm', 'm_ffn2_w_gate', 'm_ffn2_w_up', 'm_ffn2_w_down', 'v_ffn1_norm', 'v_ffn1_w_gate', 'v_ffn1_w_up', 'v_ffn1_w_down', 'v_mix_norm', 'v_w_in', 'v_q_norm', 'v_k_norm', 'v_conv_w', 'v_conv_b', 'v_conv_ln_g', 'v_conv_ln_b', 'v_w_out', 'v_ffn2_norm', 'v_ffn2_w_gate', 'v_ffn2_w_up', 'v_ffn2_w_down']
TWIN_OUTPUTS = ['loss', 'grad_x', 'grad_ffn1_norm', 'grad_ffn1_w_gate', 'grad_ffn1_w_up', 'grad_ffn1_w_down', 'grad_mix_norm', 'grad_w_in', 'grad_q_norm', 'grad_k_norm', 'grad_conv_w', 'grad_conv_b', 'grad_conv_ln_g', 'grad_conv_ln_b', 'grad_w_out', 'grad_ffn2_norm', 'grad_ffn2_w_gate', 'grad_ffn2_w_up', 'grad_ffn2_w_down', 'delta_ffn1_norm', 'delta_ffn1_w_gate', 'delta_ffn1_w_up', 'delta_ffn1_w_down', 'delta_mix_norm', 'delta_w_in', 'delta_q_norm', 'delta_k_norm', 'delta_conv_w', 'delta_conv_b', 'delta_conv_ln_g', 'delta_conv_ln_b', 'delta_w_out', 'delta_ffn2_norm', 'delta_ffn2_w_gate', 'delta_ffn2_w_up', 'delta_ffn2_w_down', 'new_m_ffn1_norm', 'new_m_ffn1_w_gate', 'new_m_ffn1_w_up', 'new_m_ffn1_w_down', 'new_m_mix_norm', 'new_m_w_in', 'new_m_q_norm', 'new_m_k_norm', 'new_m_conv_w', 'new_m_conv_b', 'new_m_conv_ln_g', 'new_m_conv_ln_b', 'new_m_w_out', 'new_m_ffn2_norm', 'new_m_ffn2_w_gate', 'new_m_ffn2_w_up', 'new_m_ffn2_w_down', 'new_v_ffn1_norm', 'new_v_ffn1_w_gate', 'new_v_ffn1_w_up', 'new_v_ffn1_w_down', 'new_v_mix_norm', 'new_v_w_in', 'new_v_q_norm', 'new_v_k_norm', 'new_v_conv_w', 'new_v_conv_b', 'new_v_conv_ln_g', 'new_v_conv_ln_b', 'new_v_w_out', 'new_v_ffn2_norm', 'new_v_ffn2_w_gate', 'new_v_ffn2_w_up', 'new_v_ffn2_w_down']
TWIN_LEAF_KINDS = {'loss': 'loss', 'grad_x': 'grad_x', 'grad_ffn1_norm': 'grad_w', 'grad_ffn1_w_gate': 'grad_w', 'grad_ffn1_w_up': 'grad_w', 'grad_ffn1_w_down': 'grad_w', 'grad_mix_norm': 'grad_w', 'grad_w_in': 'grad_w', 'grad_q_norm': 'grad_w', 'grad_k_norm': 'grad_w', 'grad_conv_w': 'grad_w', 'grad_conv_b': 'grad_w', 'grad_conv_ln_g': 'grad_w', 'grad_conv_ln_b': 'grad_w', 'grad_w_out': 'grad_w', 'grad_ffn2_norm': 'grad_w', 'grad_ffn2_w_gate': 'grad_w', 'grad_ffn2_w_up': 'grad_w', 'grad_ffn2_w_down': 'grad_w', 'delta_ffn1_norm': 'delta_w', 'delta_ffn1_w_gate': 'delta_w', 'delta_ffn1_w_up': 'delta_w', 'delta_ffn1_w_down': 'delta_w', 'delta_mix_norm': 'delta_w', 'delta_w_in': 'delta_w', 'delta_q_norm': 'delta_w', 'delta_k_norm': 'delta_w', 'delta_conv_w': 'delta_w', 'delta_conv_b': 'delta_w', 'delta_conv_ln_g': 'delta_w', 'delta_conv_ln_b': 'delta_w', 'delta_w_out': 'delta_w', 'delta_ffn2_norm': 'delta_w', 'delta_ffn2_w_gate': 'delta_w', 'delta_ffn2_w_up': 'delta_w', 'delta_ffn2_w_down': 'delta_w', 'new_m_ffn1_norm': 'new_m', 'new_m_ffn1_w_gate': 'new_m', 'new_m_ffn1_w_up': 'new_m', 'new_m_ffn1_w_down': 'new_m', 'new_m_mix_norm': 'new_m', 'new_m_w_in': 'new_m', 'new_m_q_norm': 'new_m', 'new_m_k_norm': 'new_m', 'new_m_conv_w': 'new_m', 'new_m_conv_b': 'new_m', 'new_m_conv_ln_g': 'new_m', 'new_m_conv_ln_b': 'new_m', 'new_m_w_out': 'new_m', 'new_m_ffn2_norm': 'new_m', 'new_m_ffn2_w_gate': 'new_m', 'new_m_ffn2_w_up': 'new_m', 'new_m_ffn2_w_down': 'new_m', 'new_v_ffn1_norm': 'new_v', 'new_v_ffn1_w_gate': 'new_v', 'new_v_ffn1_w_up': 'new_v', 'new_v_ffn1_w_down': 'new_v', 'new_v_mix_norm': 'new_v', 'new_v_w_in': 'new_v', 'new_v_q_norm': 'new_v', 'new_v_k_norm': 'new_v', 'new_v_conv_w': 'new_v', 'new_v_conv_b': 'new_v', 'new_v_conv_ln_g': 'new_v', 'new_v_conv_ln_b': 'new_v', 'new_v_w_out': 'new_v', 'new_v_ffn2_norm': 'new_v', 'new_v_ffn2_w_gate': 'new_v', 'new_v_ffn2_w_up': 'new_v', 'new_v_ffn2_w_down': 'new_v'}


def _forward(args):
    return _fwd_reference(*[args[k] for k in FWD_PARAMS])


def _output_shape():
    out = _jax.eval_shape(lambda: _forward(_fwd_setup_inputs(0)))
    return out.shape, out.dtype

N_MICROBATCH = 1
ADAM_LR = 0.001
ADAM_B1 = 0.9
ADAM_B2 = 0.999
ADAM_EPS = 1e-08
ADAM_WD = 0.01
ADAM_STEP = 10
PER_EXAMPLE_BATCH_AXIS = {'x': 0, 'loss_target': 0}
SHARED_INPUTS = []
_WEIGHT_DTYPES = {'ffn1_norm': _jnp.float32, 'ffn1_w_gate': _jnp.float32, 'ffn1_w_up': _jnp.float32, 'ffn1_w_down': _jnp.float32, 'mix_norm': _jnp.float32, 'w_in': _jnp.float32, 'q_norm': _jnp.float32, 'k_norm': _jnp.float32, 'conv_w': _jnp.float32, 'conv_b': _jnp.float32, 'conv_ln_g': _jnp.float32, 'conv_ln_b': _jnp.float32, 'w_out': _jnp.float32, 'ffn2_norm': _jnp.float32, 'ffn2_w_gate': _jnp.float32, 'ffn2_w_up': _jnp.float32, 'ffn2_w_down': _jnp.float32}
MOMENT_SCALE = {'ffn1_norm': 6.144196e+00, 'ffn1_w_gate': 6.643349e-02, 'ffn1_w_up': 7.201469e-02, 'ffn1_w_down': 1.170226e-01, 'mix_norm': 2.371596e-01, 'w_in': 1.114223e-01, 'q_norm': 1.690596e+00, 'k_norm': 1.689805e+00, 'conv_w': 2.893530e-01, 'conv_b': 4.654965e+00, 'conv_ln_g': 1.408755e+01, 'conv_ln_b': 9.329029e+00, 'w_out': 6.131084e-01, 'ffn2_norm': 6.213876e+00, 'ffn2_w_gate': 9.151037e-02, 'ffn2_w_up': 7.763490e-02, 'ffn2_w_down': 1.240510e-01}


def _to_microbatches(a, axis):
    t = _jnp.moveaxis(a, axis, 0)
    t = t.reshape((N_MICROBATCH, t.shape[0] // N_MICROBATCH) + t.shape[1:])
    return _jnp.moveaxis(t, 1, axis + 1)


def setup_inputs(seed: int = 0) -> dict:
    inp = _fwd_setup_inputs(seed)
    key = _jax.random.fold_in(_jax.random.key(seed), 7919)
    shape, _ = _output_shape()
    out = dict(inp)
    out["loss_target"] = _jax.random.normal(_jax.random.fold_in(key, 0), shape, _jnp.float32)
    for i, name in enumerate(TWIN_WEIGHTS):
        w = inp[name].astype(_jnp.float32)
        if MOMENT_SCALE is None:
            s = _jnp.sqrt(_jnp.mean(_jnp.square(w)) + 1e-30)
        else:
            s = MOMENT_SCALE[name]
        km, kv = _jax.random.split(_jax.random.fold_in(key, i + 1))
        out[name] = w
        out["m_" + name] = s * _jax.random.normal(km, w.shape, _jnp.float32)
        out["v_" + name] = (s * s) * _jax.random.uniform(kv, w.shape, _jnp.float32, 0.5, 1.5)
    if N_MICROBATCH > 1:
        for name, axis in PER_EXAMPLE_BATCH_AXIS.items():
            out[name] = _to_microbatches(out[name], axis)
    return {'x': out['x'], 'ffn1_norm': out['ffn1_norm'], 'ffn1_w_gate': out['ffn1_w_gate'], 'ffn1_w_up': out['ffn1_w_up'], 'ffn1_w_down': out['ffn1_w_down'], 'mix_norm': out['mix_norm'], 'w_in': out['w_in'], 'q_norm': out['q_norm'], 'k_norm': out['k_norm'], 'conv_w': out['conv_w'], 'conv_b': out['conv_b'], 'conv_ln_g': out['conv_ln_g'], 'conv_ln_b': out['conv_ln_b'], 'w_out': out['w_out'], 'ffn2_norm': out['ffn2_norm'], 'ffn2_w_gate': out['ffn2_w_gate'], 'ffn2_w_up': out['ffn2_w_up'], 'ffn2_w_down': out['ffn2_w_down'], 'loss_target': out['loss_target'], 'm_ffn1_norm': out['m_ffn1_norm'], 'm_ffn1_w_gate': out['m_ffn1_w_gate'], 'm_ffn1_w_up': out['m_ffn1_w_up'], 'm_ffn1_w_down': out['m_ffn1_w_down'], 'm_mix_norm': out['m_mix_norm'], 'm_w_in': out['m_w_in'], 'm_q_norm': out['m_q_norm'], 'm_k_norm': out['m_k_norm'], 'm_conv_w': out['m_conv_w'], 'm_conv_b': out['m_conv_b'], 'm_conv_ln_g': out['m_conv_ln_g'], 'm_conv_ln_b': out['m_conv_ln_b'], 'm_w_out': out['m_w_out'], 'm_ffn2_norm': out['m_ffn2_norm'], 'm_ffn2_w_gate': out['m_ffn2_w_gate'], 'm_ffn2_w_up': out['m_ffn2_w_up'], 'm_ffn2_w_down': out['m_ffn2_w_down'], 'v_ffn1_norm': out['v_ffn1_norm'], 'v_ffn1_w_gate': out['v_ffn1_w_gate'], 'v_ffn1_w_up': out['v_ffn1_w_up'], 'v_ffn1_w_down': out['v_ffn1_w_down'], 'v_mix_norm': out['v_mix_norm'], 'v_w_in': out['v_w_in'], 'v_q_norm': out['v_q_norm'], 'v_k_norm': out['v_k_norm'], 'v_conv_w': out['v_conv_w'], 'v_conv_b': out['v_conv_b'], 'v_conv_ln_g': out['v_conv_ln_g'], 'v_conv_ln_b': out['v_conv_ln_b'], 'v_w_out': out['v_w_out'], 'v_ffn2_norm': out['v_ffn2_norm'], 'v_ffn2_w_gate': out['v_ffn2_w_gate'], 'v_ffn2_w_up': out['v_ffn2_w_up'], 'v_ffn2_w_down': out['v_ffn2_w_down']}


def _loss(weights, diff, rest, loss_target):
    with _jax.named_scope("forward"):
        args = {**rest, TWIN_DIFF_INPUT: diff, **{k: w.astype(_WEIGHT_DTYPES[k]) for k, w in weights.items()}}
        y = _forward(args)
    with _jax.named_scope("loss_head"):
        err = _jnp.square(y.astype(_jnp.float32) - loss_target)
        return 0.5 * _jnp.sum(_jnp.mean(err, axis=-1)) if err.ndim else 0.5 * err


def _adamw(w, g, m, v):
    m = ADAM_B1 * m + (1.0 - ADAM_B1) * g
    v = ADAM_B2 * v + (1.0 - ADAM_B2) * _jnp.square(g)
    m_hat = m / (1.0 - ADAM_B1 ** ADAM_STEP)
    v_hat = v / (1.0 - ADAM_B2 ** ADAM_STEP)
    delta = -ADAM_LR * (m_hat / (_jnp.sqrt(v_hat) + ADAM_EPS) + ADAM_WD * w)
    return delta, m, v


def reference(x, ffn1_norm, ffn1_w_gate, ffn1_w_up, ffn1_w_down, mix_norm, w_in, q_norm, k_norm, conv_w, conv_b, conv_ln_g, conv_ln_b, w_out, ffn2_norm, ffn2_w_gate, ffn2_w_up, ffn2_w_down, loss_target, m_ffn1_norm, m_ffn1_w_gate, m_ffn1_w_up, m_ffn1_w_down, m_mix_norm, m_w_in, m_q_norm, m_k_norm, m_conv_w, m_conv_b, m_conv_ln_g, m_conv_ln_b, m_w_out, m_ffn2_norm, m_ffn2_w_gate, m_ffn2_w_up, m_ffn2_w_down, v_ffn1_norm, v_ffn1_w_gate, v_ffn1_w_up, v_ffn1_w_down, v_mix_norm, v_w_in, v_q_norm, v_k_norm, v_conv_w, v_conv_b, v_conv_ln_g, v_conv_ln_b, v_w_out, v_ffn2_norm, v_ffn2_w_gate, v_ffn2_w_up, v_ffn2_w_down):
    given = dict(x=x, ffn1_norm=ffn1_norm, ffn1_w_gate=ffn1_w_gate, ffn1_w_up=ffn1_w_up, ffn1_w_down=ffn1_w_down, mix_norm=mix_norm, w_in=w_in, q_norm=q_norm, k_norm=k_norm, conv_w=conv_w, conv_b=conv_b, conv_ln_g=conv_ln_g, conv_ln_b=conv_ln_b, w_out=w_out, ffn2_norm=ffn2_norm, ffn2_w_gate=ffn2_w_gate, ffn2_w_up=ffn2_w_up, ffn2_w_down=ffn2_w_down, loss_target=loss_target, m_ffn1_norm=m_ffn1_norm, m_ffn1_w_gate=m_ffn1_w_gate, m_ffn1_w_up=m_ffn1_w_up, m_ffn1_w_down=m_ffn1_w_down, m_mix_norm=m_mix_norm, m_w_in=m_w_in, m_q_norm=m_q_norm, m_k_norm=m_k_norm, m_conv_w=m_conv_w, m_conv_b=m_conv_b, m_conv_ln_g=m_conv_ln_g, m_conv_ln_b=m_conv_ln_b, m_w_out=m_w_out, m_ffn2_norm=m_ffn2_norm, m_ffn2_w_gate=m_ffn2_w_gate, m_ffn2_w_up=m_ffn2_w_up, m_ffn2_w_down=m_ffn2_w_down, v_ffn1_norm=v_ffn1_norm, v_ffn1_w_gate=v_ffn1_w_gate, v_ffn1_w_up=v_ffn1_w_up, v_ffn1_w_down=v_ffn1_w_down, v_mix_norm=v_mix_norm, v_w_in=v_w_in, v_q_norm=v_q_norm, v_k_norm=v_k_norm, v_conv_w=v_conv_w, v_conv_b=v_conv_b, v_conv_ln_g=v_conv_ln_g, v_conv_ln_b=v_conv_ln_b, v_w_out=v_w_out, v_ffn2_norm=v_ffn2_norm, v_ffn2_w_gate=v_ffn2_w_gate, v_ffn2_w_up=v_ffn2_w_up, v_ffn2_w_down=v_ffn2_w_down)
    weights = {n: given[n] for n in TWIN_WEIGHTS}
    shared = {n: given[n] for n in SHARED_INPUTS}
    per_example = {n: given[n] for n in ['x']}
    grad_fn = _jax.value_and_grad(_loss, argnums=(0, 1))

    def one_microbatch(ex, loss_target):
        ex = dict(ex)
        diff = ex.pop(TWIN_DIFF_INPUT)
        return grad_fn(weights, diff, {**shared, **ex}, loss_target)

    if N_MICROBATCH == 1:
        loss, (grad_w, grad_x) = one_microbatch(per_example, given["loss_target"])
    else:
        def body(carry, xs):
            loss_sum, grad_sum = carry
            l_k, (gw_k, gx_k) = one_microbatch(xs[0], xs[1])
            with _jax.named_scope("update"):
                return (loss_sum + l_k, _jax.tree.map(_jnp.add, grad_sum, gw_k)), gx_k

        init = (_jnp.zeros((), _jnp.float32), _jax.tree.map(_jnp.zeros_like, weights))
        (loss, grad_w), grad_x = _jax.lax.scan(body, init, (per_example, given["loss_target"]))
    with _jax.named_scope("update"):
        delta_w, new_m, new_v = {}, {}, {}
        for n in TWIN_WEIGHTS:
            delta_w[n], new_m[n], new_v[n] = _adamw(weights[n], grad_w[n], given["m_" + n], given["v_" + n])
    return (loss, grad_x, *[grad_w[n] for n in TWIN_WEIGHTS], *[delta_w[n] for n in TWIN_WEIGHTS],
            *[new_m[n] for n in TWIN_WEIGHTS], *[new_v[n] for n in TWIN_WEIGHTS])
```

```python
import functools

import jax
import jax.numpy as jnp
from jax import lax
from jax.experimental import pallas as pl
from jax.experimental.pallas import tpu as pltpu

F32 = jnp.float32
BF16 = jnp.bfloat16

EPS = 1e-6
HEADS = 8
HEAD_DIM = 64
D_ATTN = HEADS * HEAD_DIM
D_CONV = 512
CONV_K = 31
QBLK = 128
N_PATTERNS = 3
DILATIONS = (1, 4, 16)
LANES = 128
NEG = -1e30

ADAM_LR = 0.001
ADAM_B1 = 0.9
ADAM_B2 = 0.999
ADAM_EPS = 1e-08
ADAM_WD = 0.01
ADAM_STEP = 10

VMEM_LIMIT = 56 * 1024 * 1024
MESH = pl.DeviceIdType.MESH

NT_DIMS = (((1,), (1,)), ((), ()))
TN_DIMS = (((0,), (0,)), ((), ()))


def _params(*sem):
    return pltpu.CompilerParams(dimension_semantics=sem, vmem_limit_bytes=VMEM_LIMIT)


def _dot(a, b):
    return jnp.dot(a, b, preferred_element_type=F32)


def _dot_nt(a, b):
    return lax.dot_general(a, b, NT_DIMS, preferred_element_type=F32)


def _dot_tn(a, b):
    return lax.dot_general(a, b, TN_DIMS, preferred_element_type=F32)


def _sigmoid(x):
    return 1.0 / (1.0 + jnp.exp(-x))


def _seg_mean(v, e_ref, width):
    hi = v.astype(BF16)
    lo = (v - hi.astype(F32)).astype(BF16)
    e = e_ref[...]
    return (_dot(hi, e) + _dot(lo, e)) * (1.0 / width)


def _seg_matrix(n):
    i = jnp.arange(n)
    return (i[:, None] // HEAD_DIM == i[None, :] // HEAD_DIM).astype(BF16)


def _ffn_fwd(x, gain, wg, wu, wd, tgt, *, tm, name):
    T, D = x.shape
    NS, _, Fs = wg.shape
    with_loss = tgt is not None

    def body(*refs):
        if with_loss:
            x_ref, g_ref, wg_ref, wu_ref, wd_ref, t_ref, h_ref, n_ref, G_ref, U_ref, loss_ref, acc_ref = refs
        else:
            x_ref, g_ref, wg_ref, wu_ref, wd_ref, h_ref, n_ref, G_ref, U_ref, acc_ref = refs
        i = pl.program_id(0)
        j = pl.program_id(1)

        @pl.when(j == 0)
        def _():
            xv = x_ref[...]
            r = lax.rsqrt(jnp.mean(xv * xv, axis=-1, keepdims=True) + EPS)
            n_ref[...] = (xv * r * g_ref[...]).astype(BF16)
            acc_ref[...] = jnp.zeros_like(acc_ref)

        n = n_ref[...]
        G = _dot(n, wg_ref[...])
        U = _dot(n, wu_ref[...])
        G_ref[...] = G.astype(BF16)
        U_ref[...] = U.astype(BF16)
        A = (G * _sigmoid(G) * U).astype(BF16)
        acc_ref[...] += _dot(A, wd_ref[...])

        @pl.when(j == NS - 1)
        def _():
            h = x_ref[...] + 0.5 * acc_ref[...]
            if with_loss:
                e = h - t_ref[...]
                h_ref[...] = e * (1.0 / D)

                @pl.when(i == 0)
                def _():
                    loss_ref[...] = jnp.zeros_like(loss_ref)

                loss_ref[...] += jnp.sum(e * e) * (0.5 / D)
            else:
                h_ref[...] = h

    tok = pl.BlockSpec((tm, D), lambda i, j: (i, 0))
    in_specs = [tok, pl.BlockSpec((1, D), lambda i, j: (0, 0)),
                pl.BlockSpec((None, D, Fs), lambda i, j: (j, 0, 0)),
                pl.BlockSpec((None, D, Fs), lambda i, j: (j, 0, 0)),
                pl.BlockSpec((None, Fs, D), lambda i, j: (j, 0, 0))]
    args = [x, gain, wg, wu, wd]
    act = pl.BlockSpec((None, tm, Fs), lambda i, j: (j, i, 0))
    out_shape = [jax.ShapeDtypeStruct((T, D), F32), jax.ShapeDtypeStruct((T, D), BF16),
                 jax.ShapeDtypeStruct((NS, T, Fs), BF16), jax.ShapeDtypeStruct((NS, T, Fs), BF16)]
    out_specs = [tok, tok, act, act]
    if with_loss:
        in_specs.append(tok)
        args.append(tgt)
        out_shape.append(jax.ShapeDtypeStruct((8, LANES), F32))
        out_specs.append(pl.BlockSpec((8, LANES), lambda i, j: (0, 0)))
    return pl.pallas_call(
        body, grid=(T // tm, NS), in_specs=in_specs, out_specs=out_specs, out_shape=out_shape,
        scratch_shapes=[pltpu.VMEM((tm, D), F32)],
        compiler_params=_params("arbitrary", "arbitrary"), name=name)(*args)


def _rms_bwd(xv, gain, dn):
    r = lax.rsqrt(jnp.mean(xv * xv, axis=-1, keepdims=True) + EPS)
    xhat = xv * r
    dxh = dn * gain
    dx = r * (dxh - xhat * jnp.mean(dxh * xhat, axis=-1, keepdims=True))
    dg = jnp.sum(dn * xhat, axis=0, keepdims=True)
    return dx, dg


def _ffn_bwd_act(dh, x, gain, G, U, wg, wu, wd, *, tm, name):
    T, D = x.shape
    NS, _, Fs = wg.shape

    def body(dh_ref, x_ref, g_ref, G_ref, U_ref, wg_ref, wu_ref, wd_ref,
             dG_ref, dU_ref, A_ref, dy_ref, dx_ref, dg_ref, acc_ref):
        i = pl.program_id(0)
        j = pl.program_id(1)

        @pl.when(j == 0)
        def _():
            dy_ref[...] = (0.5 * dh_ref[...]).astype(BF16)
            acc_ref[...] = jnp.zeros_like(acc_ref)

        @pl.when((i == 0) & (j == 0))
        def _():
            dg_ref[...] = jnp.zeros_like(dg_ref)

        Gv = G_ref[...].astype(F32)
        Uv = U_ref[...].astype(F32)
        sig = _sigmoid(Gv)
        s = Gv * sig
        dA = _dot_nt(dy_ref[...], wd_ref[...])
        dG = (dA * Uv * (sig * (1.0 + Gv * (1.0 - sig)))).astype(BF16)
        dU = (dA * s).astype(BF16)
        dG_ref[...] = dG
        dU_ref[...] = dU
        A_ref[...] = (s * Uv).astype(BF16)
        acc_ref[...] += _dot_nt(dG, wg_ref[...]) + _dot_nt(dU, wu_ref[...])

        @pl.when(j == NS - 1)
        def _():
            dx, dg = _rms_bwd(x_ref[...], g_ref[...], acc_ref[...])
            dx_ref[...] = dh_ref[...] + dx
            dg_ref[...] += dg

    tok = pl.BlockSpec((tm, D), lambda i, j: (i, 0))
    act = pl.BlockSpec((None, tm, Fs), lambda i, j: (j, i, 0))
    vec = pl.BlockSpec((1, D), lambda i, j: (0, 0))
    return pl.pallas_call(
        body, grid=(T // tm, NS),
        in_specs=[tok, tok, vec, act, act,
                  pl.BlockSpec((None, D, Fs), lambda i, j: (j, 0, 0)),
                  pl.BlockSpec((None, D, Fs), lambda i, j: (j, 0, 0)),
                  pl.BlockSpec((None, Fs, D), lambda i, j: (j, 0, 0))],
        out_specs=[act, act, act, tok, tok, vec],
        out_shape=[jax.ShapeDtypeStruct((NS, T, Fs), BF16)] * 3
        + [jax.ShapeDtypeStruct((T, D), BF16), jax.ShapeDtypeStruct((T, D), F32),
           jax.ShapeDtypeStruct((1, D), F32)],
        scratch_shapes=[pltpu.VMEM((tm, D), F32)],
        compiler_params=_params("arbitrary", "arbitrary"), name=name)(dh, x, gain, G, U, wg, wu, wd)


def _ffn_bwd_w(n, dy, dG, dU, A, *, tk, name):
    T, D = n.shape
    NS, _, Fs = dG.shape

    def body(n_ref, dy_ref, dG_ref, dU_ref, A_ref, wg_ref, wu_ref, wd_ref):
        @pl.when(pl.program_id(1) == 0)
        def _():
            wg_ref[...] = jnp.zeros_like(wg_ref)
            wu_ref[...] = jnp.zeros_like(wu_ref)
            wd_ref[...] = jnp.zeros_like(wd_ref)

        nv = n_ref[...]
        wg_ref[...] += _dot_tn(nv, dG_ref[...])
        wu_ref[...] += _dot_tn(nv, dU_ref[...])
        wd_ref[...] += _dot_tn(A_ref[...], dy_ref[...])

    tok = pl.BlockSpec((tk, D), lambda j, k: (k, 0))
    act = pl.BlockSpec((None, tk, Fs), lambda j, k: (j, k, 0))
    return pl.pallas_call(
        body, grid=(NS, T // tk), in_specs=[tok, tok, act, act, act],
        out_specs=[pl.BlockSpec((None, D, Fs), lambda j, k: (j, 0, 0)),
                   pl.BlockSpec((None, D, Fs), lambda j, k: (j, 0, 0)),
                   pl.BlockSpec((None, Fs, D), lambda j, k: (j, 0, 0))],
        out_shape=[jax.ShapeDtypeStruct((NS, D, Fs), F32), jax.ShapeDtypeStruct((NS, D, Fs), F32),
                   jax.ShapeDtypeStruct((NS, Fs, D), F32)],
        compiler_params=_params("arbitrary", "arbitrary"), name=name)(n, dy, dG, dU, A)


def _inproj_fwd(h, gain, win, *, tm, name):
    T, D = h.shape
    NS, _, Cs = win.shape

    def body(h_ref, g_ref, w_ref, u_ref, n_ref):
        @pl.when(pl.program_id(1) == 0)
        def _():
            xv = h_ref[...]
            r = lax.rsqrt(jnp.mean(xv * xv, axis=-1, keepdims=True) + EPS)
            n_ref[...] = (xv * r * g_ref[...]).astype(BF16)

        u_ref[...] = _dot(n_ref[...], w_ref[...])

    tok = pl.BlockSpec((tm, D), lambda i, j: (i, 0))
    return pl.pallas_call(
        body, grid=(T // tm, NS),
        in_specs=[tok, pl.BlockSpec((1, D), lambda i, j: (0, 0)),
                  pl.BlockSpec((None, D, Cs), lambda i, j: (j, 0, 0))],
        out_specs=[pl.BlockSpec((tm, Cs), lambda i, j: (i, j)), tok],
        out_shape=[jax.ShapeDtypeStruct((T, NS * Cs), F32), jax.ShapeDtypeStruct((T, D), BF16)],
        compiler_params=_params("arbitrary", "arbitrary"), name=name)(h, gain, win)


def _inproj_bwd_act(du, dh, h, gain, win, *, tm, name):
    T, D = h.shape
    NS, _, Cs = win.shape

    def body(du_ref, dh_ref, h_ref, g_ref, w_ref, dx_ref, dg_ref, acc_ref):
        i = pl.program_id(0)
        j = pl.program_id(1)

        @pl.when(j == 0)
        def _():
            acc_ref[...] = jnp.zeros_like(acc_ref)

        @pl.when((i == 0) & (j == 0))
        def _():
            dg_ref[...] = jnp.zeros_like(dg_ref)

        acc_ref[...] += _dot_nt(du_ref[...], w_ref[...])

        @pl.when(j == NS - 1)
        def _():
            dx, dg = _rms_bwd(h_ref[...], g_ref[...], acc_ref[...])
            dx_ref[...] = dh_ref[...] + dx
            dg_ref[...] += dg

    tok = pl.BlockSpec((tm, D), lambda i, j: (i, 0))
    vec = pl.BlockSpec((1, D), lambda i, j: (0, 0))
    return pl.pallas_call(
        body, grid=(T // tm, NS),
        in_specs=[pl.BlockSpec((tm, Cs), lambda i, j: (i, j)), tok, tok, vec,
                  pl.BlockSpec((None, D, Cs), lambda i, j: (j, 0, 0))],
        out_specs=[tok, vec],
        out_shape=[jax.ShapeDtypeStruct((T, D), F32), jax.ShapeDtypeStruct((1, D), F32)],
        scratch_shapes=[pltpu.VMEM((tm, D), F32)],
        compiler_params=_params("arbitrary", "arbitrary"), name=name)(du, dh, h, gain, win)


def _inproj_bwd_w(n, du, ns, *, tk, name):
    T, D = n.shape
    Cs = du.shape[1] // ns

    def body(n_ref, du_ref, w_ref):
        @pl.when(pl.program_id(1) == 0)
        def _():
            w_ref[...] = jnp.zeros_like(w_ref)

        w_ref[...] += _dot_tn(n_ref[...], du_ref[...])

    return pl.pallas_call(
        body, grid=(ns, T // tk),
        in_specs=[pl.BlockSpec((tk, D), lambda j, k: (k, 0)), pl.BlockSpec((tk, Cs), lambda j, k: (k, j))],
        out_specs=pl.BlockSpec((None, D, Cs), lambda j, k: (j, 0, 0)),
        out_shape=jax.ShapeDtypeStruct((ns, D, Cs), F32),
        compiler_params=_params("arbitrary", "arbitrary"), name=name)(n, du)


def _permute_out(src_ref, out_ref, cast):
    S = src_ref.shape[1]
    for p, d in enumerate(DILATIONS):
        L = S // d
        for cc in range(4):
            cols = slice(cc * LANES, (cc + 1) * LANES)
            if d == 1:
                out_ref[p, :, cols] = src_ref[cc].astype(cast)
            else:
                for r in range(d):
                    out_ref[p, r * L:(r + 1) * L, cols] = src_ref[cc, pl.ds(r, L, stride=d), :].astype(cast)


def _unpermute_in(get_block, dst_ref, p, S):
    d = DILATIONS[p]
    L = S // d
    if d == 1:
        dst_ref[...] = get_block(0, S)
    else:
        for r in range(d):
            dst_ref[pl.ds(r, L, stride=d), :] = get_block(r * L, L)


def _qkv_prep(u, gains, B, S, *, name):
    emat = _seg_matrix(D_ATTN)

    def body(u_ref, g_ref, e_ref, out_ref, scr_ref):
        c = pl.program_id(1)
        xv = u_ref[...]
        ms = _seg_mean(xv * xv, e_ref, HEAD_DIM)
        r = jnp.where(c < 2, lax.rsqrt(ms + EPS), 1.0)
        yv = xv * r * g_ref[...]
        for cc in range(4):
            scr_ref[cc] = yv[:, cc * LANES:(cc + 1) * LANES]
        _permute_out(scr_ref, out_ref, BF16)

    return pl.pallas_call(
        body, grid=(B, 3),
        in_specs=[pl.BlockSpec((S, D_ATTN), lambda b, c: (b, c)),
                  pl.BlockSpec((None, 1, D_ATTN), lambda b, c: (c, 0, 0)),
                  pl.BlockSpec((D_ATTN, D_ATTN), lambda b, c: (0, 0))],
        out_specs=pl.BlockSpec((None, N_PATTERNS, None, S, D_ATTN), lambda b, c: (c, 0, b, 0, 0)),
        out_shape=jax.ShapeDtypeStruct((3, N_PATTERNS, B, S, D_ATTN), BF16),
        scratch_shapes=[pltpu.VMEM((4, S, LANES), F32)],
        compiler_params=_params("arbitrary", "arbitrary"), name=name)(u, gains, emat)


def _band_mask(p, b):
    nblk = jnp.right_shift(16, 2 * p)
    has_prev = jnp.bitwise_and(b, nblk - 1) != 0
    qi = lax.broadcasted_iota(jnp.int32, (QBLK, 2 * QBLK), 0)
    ci = lax.broadcasted_iota(jnp.int32, (QBLK, 2 * QBLK), 1)
    dist = QBLK + qi - ci
    return (dist >= 0) & (dist <= QBLK) & (has_prev | (ci >= QBLK))


def _attn_specs(nb):
    blk = (None, None, None, QBLK, D_ATTN)
    q_spec = pl.BlockSpec(blk, lambda p, b: (0, p, b, 0, 0))
    kp_spec = pl.BlockSpec(blk, lambda p, b: (1, p, jnp.maximum(b - 1, 0), 0, 0))
    kc_spec = pl.BlockSpec(blk, lambda p, b: (1, p, b, 0, 0))
    vp_spec = pl.BlockSpec(blk, lambda p, b: (2, p, jnp.maximum(b - 1, 0), 0, 0))
    vc_spec = pl.BlockSpec(blk, lambda p, b: (2, p, b, 0, 0))
    return [q_spec, kp_spec, kc_spec, vp_spec, vc_spec]


def _attn_fwd(qkv, *, name):
    nb = qkv.shape[2]
    scale = HEAD_DIM ** -0.5

    def body(q_ref, kp_ref, kc_ref, vp_ref, vc_ref, o_ref, lse_ref):
        mask = _band_mask(pl.program_id(0), pl.program_id(1))
        q = q_ref[...]
        kk = jnp.concatenate([kp_ref[...], kc_ref[...]], axis=0)
        vv = jnp.concatenate([vp_ref[...], vc_ref[...]], axis=0)
        for h in range(HEADS):
            cols = slice(h * HEAD_DIM, (h + 1) * HEAD_DIM)
            s = _dot_nt(q[:, cols], kk[:, cols]) * scale
            s = jnp.where(mask, s, NEG)
            m = jnp.max(s, axis=-1, keepdims=True)
            e = jnp.exp(s - m)
            l = jnp.sum(e, axis=-1, keepdims=True)
            o = _dot(e.astype(BF16), vv[:, cols]) / l
            o_ref[:, cols] = o
            lse_ref[:, cols] = jnp.broadcast_to(m + jnp.log(l), (QBLK, HEAD_DIM))

    out = pl.BlockSpec((None, None, QBLK, D_ATTN), lambda p, b: (p, b, 0, 0))
    return pl.pallas_call(
        body, grid=(N_PATTERNS, nb), in_specs=_attn_specs(nb), out_specs=[out, out],
        out_shape=[jax.ShapeDtypeStruct((N_PATTERNS, nb, QBLK, D_ATTN), F32)] * 2,
        compiler_params=_params("arbitrary", "arbitrary"), name=name)(qkv, qkv, qkv, qkv, qkv)


def _attn_combine(o3, lse3, B, S, *, name):
    def body(o_ref, l_ref, a_ref, lt_ref, so_ref, sl_ref):
        for p in range(N_PATTERNS):
            _unpermute_in(lambda r0, n, p=p: o_ref[p, pl.ds(r0, n), :], so_ref.at[p], p, S)
            _unpermute_in(lambda r0, n, p=p: l_ref[p, pl.ds(r0, n), :], sl_ref.at[p], p, S)
        l0, l1, l2 = sl_ref[0], sl_ref[1], sl_ref[2]
        m = jnp.maximum(jnp.maximum(l0, l1), l2)
        w0, w1, w2 = jnp.exp(l0 - m), jnp.exp(l1 - m), jnp.exp(l2 - m)
        tot = w0 + w1 + w2
        a_ref[...] = (w0 * so_ref[0] + w1 * so_ref[1] + w2 * so_ref[2]) / tot
        lt_ref[...] = m + jnp.log(tot)

    o3 = o3.reshape(N_PATTERNS, B, S, D_ATTN)
    lse3 = lse3.reshape(N_PATTERNS, B, S, D_ATTN)
    inp = pl.BlockSpec((N_PATTERNS, None, S, LANES), lambda b, c: (0, b, 0, c))
    out = pl.BlockSpec((S, LANES), lambda b, c: (b, c))
    return pl.pallas_call(
        body, grid=(B, D_ATTN // LANES), in_specs=[inp, inp], out_specs=[out, out],
        out_shape=[jax.ShapeDtypeStruct((B * S, D_ATTN), F32)] * 2,
        scratch_shapes=[pltpu.VMEM((N_PATTERNS, S, LANES), F32)] * 2,
        compiler_params=_params("arbitrary", "arbitrary"), name=name)(o3, lse3)


def _attn_bwd_prep(dattn, attn, lse, B, S, *, name):
    emat = _seg_matrix(LANES)

    def body(da_ref, a_ref, l_ref, e_ref, do_ref, lo_ref, dd_ref, scr_ref):
        da = da_ref[...]
        dsum = _seg_mean(da * a_ref[...], e_ref, 1.0)
        for k, (val, out_ref, cast) in enumerate(((da, do_ref, BF16), (l_ref[...], lo_ref, F32), (dsum, dd_ref, F32))):
            scr_ref[...] = val
            for p, d in enumerate(DILATIONS):
                L = S // d
                if d == 1:
                    out_ref[p] = val.astype(cast)
                else:
                    for r in range(d):
                        out_ref[p, r * L:(r + 1) * L, :] = scr_ref[pl.ds(r, L, stride=d), :].astype(cast)

    inp = pl.BlockSpec((S, LANES), lambda b, c: (b, c))
    out = pl.BlockSpec((N_PATTERNS, None, S, LANES), lambda b, c: (0, b, 0, c))
    shp = (N_PATTERNS, B, S, D_ATTN)
    return pl.pallas_call(
        body, grid=(B, D_ATTN // LANES),
        in_specs=[inp, inp, inp, pl.BlockSpec((LANES, LANES), lambda b, c: (0, 0))],
        out_specs=[out, out, out],
        out_shape=[jax.ShapeDtypeStruct(shp, BF16), jax.ShapeDtypeStruct(shp, F32), jax.ShapeDtypeStruct(shp, F32)],
        scratch_shapes=[pltpu.VMEM((S, LANES), F32)],
        compiler_params=_params("arbitrary", "arbitrary"), name=name)(dattn, attn, lse, emat)


def _attn_bwd(qkv, do3, lse3, dd3, *, name):
    nb = qkv.shape[2]
    scale = HEAD_DIM ** -0.5

    def body(q_ref, kp_ref, kc_ref, vp_ref, vc_ref, do_ref, l_ref, d_ref, cur_ref, prev_ref):
        mask = _band_mask(pl.program_id(0), pl.program_id(1))
        q = q_ref[...]
        kk = jnp.concatenate([kp_ref[...], kc_ref[...]], axis=0)
        vv = jnp.concatenate([vp_ref[...], vc_ref[...]], axis=0)
        do = do_ref[...]
        for h in range(HEADS):
            cols = slice(h * HEAD_DIM, (h + 1) * HEAD_DIM)
            qh, kh, vh, doh = q[:, cols], kk[:, cols], vv[:, cols], do[:, cols]
            s = _dot_nt(qh, kh) * scale
            lse = l_ref[:, h * HEAD_DIM:h * HEAD_DIM + 1]
            dsum = d_ref[:, h * HEAD_DIM:h * HEAD_DIM + 1]
            pr = jnp.where(mask, jnp.exp(s - lse), 0.0)
            dp = _dot_nt(doh, vh)
            ds = (pr * (dp - dsum) * scale).astype(BF16)
            prb = pr.astype(BF16)
            cur_ref[0, :, cols] = _dot(ds, kh)
            dk = _dot_tn(ds, qh)
            dv = _dot_tn(prb, doh)
            prev_ref[0, :, cols] = dk[:QBLK]
            cur_ref[1, :, cols] = dk[QBLK:]
            prev_ref[1, :, cols] = dv[:QBLK]
            cur_ref[2, :, cols] = dv[QBLK:]

    aux = pl.BlockSpec((None, None, QBLK, D_ATTN), lambda p, b: (p, b, 0, 0))
    return pl.pallas_call(
        body, grid=(N_PATTERNS, nb), in_specs=_attn_specs(nb) + [aux, aux, aux],
        out_specs=[pl.BlockSpec((3, None, None, QBLK, D_ATTN), lambda p, b: (0, p, b, 0, 0)),
                   pl.BlockSpec((2, None, None, QBLK, D_ATTN), lambda p, b: (0, p, b, 0, 0))],
        out_shape=[jax.ShapeDtypeStruct((3, N_PATTERNS, nb, QBLK, D_ATTN), F32),
                   jax.ShapeDtypeStruct((2, N_PATTERNS, nb, QBLK, D_ATTN), F32)],
        compiler_params=_params("arbitrary", "arbitrary"), name=name)(qkv, qkv, qkv, qkv, qkv, do3, lse3, dd3)


def _attn_grad_combine(cur, prev, u, gains, B, S, *, name):
    emat = _seg_matrix(LANES)
    nblk = S // QBLK

    def body(cur_ref, prev_ref, u_ref, g_ref, e_ref, du_ref, dg_ref, scr_ref):
        c = pl.program_id(0)
        b = pl.program_id(2)
        use_prev = c > 0
        total = None
        for p, d in enumerate(DILATIONS):
            per_seq = nblk // d

            def get_block(r0, n, p=p, per_seq=per_seq):
                parts = []
                for blk in range(r0 // QBLK, (r0 + n) // QBLK):
                    v = cur_ref[p, pl.ds(blk * QBLK, QBLK), :]
                    if blk % per_seq != per_seq - 1:
                        nxt = prev_ref[p, pl.ds((blk + 1) * QBLK, QBLK), :]
                        v = v + jnp.where(use_prev, nxt, 0.0)
                    parts.append(v)
                return parts[0] if len(parts) == 1 else jnp.concatenate(parts, axis=0)

            _unpermute_in(get_block, scr_ref.at[p], p, S)
        dy = scr_ref[0] + scr_ref[1] + scr_ref[2]
        xv = u_ref[...]
        gain = g_ref[...]
        ms = _seg_mean(xv * xv, e_ref, HEAD_DIM)
        r = lax.rsqrt(ms + EPS)
        xhat = xv * r
        dxh = dy * gain
        dx = r * (dxh - xhat * _seg_mean(dxh * xhat, e_ref, HEAD_DIM))
        du_ref[...] = jnp.where(c < 2, dx, dy).astype(BF16)

        @pl.when((b == 0))
        def _():
            dg_ref[...] = jnp.zeros_like(dg_ref)

        dg_ref[...] += jnp.sum(dy * xhat, axis=0, keepdims=True)

    cur = cur.reshape(3, N_PATTERNS, B, S, D_ATTN)
    prev = prev.reshape(2, N_PATTERNS, B, S, D_ATTN)
    ncc = D_ATTN // LANES
    return pl.pallas_call(
        body, grid=(3, ncc, B),
        in_specs=[pl.BlockSpec((None, N_PATTERNS, None, S, LANES), lambda c, cc, b: (c, 0, b, 0, cc)),
                  pl.BlockSpec((None, N_PATTERNS, None, S, LANES), lambda c, cc, b: (jnp.maximum(c - 1, 0), 0, b, 0, cc)),
                  pl.BlockSpec((S, LANES), lambda c, cc, b: (b, c * ncc + cc)),
                  pl.BlockSpec((None, 1, LANES), lambda c, cc, b: (c, 0, cc)),
                  pl.BlockSpec((LANES, LANES), lambda c, cc, b: (0, 0))],
        out_specs=[pl.BlockSpec((S, LANES), lambda c, cc, b: (b, c * ncc + cc)),
                   pl.BlockSpec((None, 1, LANES), lambda c, cc, b: (c, 0, cc))],
        out_shape=[jax.ShapeDtypeStruct((B * S, 3 * D_ATTN), BF16), jax.ShapeDtypeStruct((3, 1, D_ATTN), F32)],
        scratch_shapes=[pltpu.VMEM((N_PATTERNS, S, LANES), F32)],
        compiler_params=_params("arbitrary", "arbitrary", "arbitrary"), name=name)(cur, prev, u, gains, emat)


HALO = 32
SUB = 64


def _conv_fwd(u, cw, cb, lg, lb, B, S, *, tc, name):
    nchunk = S // tc
    hb = tc // HALO

    def body(ca_ref, cap_ref, cg_ref, cgp_ref, w_ref, cb_ref, lg_ref, lb_ref, cv_ref, glu_ref, y_ref, pad_ref):
        i = pl.program_id(1)
        glu = ca_ref[...] * _sigmoid(cg_ref[...])
        glu_ref[...] = glu
        prev = cap_ref[...] * _sigmoid(cgp_ref[...])
        pad_ref[0:HALO, :] = jnp.where(i > 0, prev, 0.0)
        pad_ref[HALO:, :] = glu
        for sub in range(tc // SUB):
            acc = jnp.zeros((SUB, D_CONV), F32) + cb_ref[...]
            for k in range(CONV_K):
                acc = acc + pad_ref[pl.ds(sub * SUB + HALO - (CONV_K - 1) + k, SUB), :] * w_ref[pl.ds(k, 1), :]
            y_ref[sub * SUB:(sub + 1) * SUB, :] = acc
        y = y_ref[...]
        mu = jnp.mean(y, axis=-1, keepdims=True)
        yc = y - mu
        var = jnp.mean(yc * yc, axis=-1, keepdims=True)
        z = yc * lax.rsqrt(var + EPS) * lg_ref[...] + lb_ref[...]
        cv_ref[...] = (z * _sigmoid(z)).astype(BF16)

    def cur(col):
        return pl.BlockSpec((tc, D_CONV), lambda b, i: (b * nchunk + i, col))

    def halo(col):
        return pl.BlockSpec((HALO, D_CONV), lambda b, i: (jnp.maximum((b * nchunk + i) * hb - 1, 0), col))

    vec = pl.BlockSpec((1, D_CONV), lambda b, i: (0, 0))
    out = pl.BlockSpec((tc, D_CONV), lambda b, i: (b * nchunk + i, 0))
    return pl.pallas_call(
        body, grid=(B, nchunk),
        in_specs=[cur(3), halo(3), cur(4), halo(4), pl.BlockSpec((CONV_K, D_CONV), lambda b, i: (0, 0)), vec, vec, vec],
        out_specs=[out, out, out],
        out_shape=[jax.ShapeDtypeStruct((B * S, D_CONV), BF16), jax.ShapeDtypeStruct((B * S, D_CONV), F32),
                   jax.ShapeDtypeStruct((B * S, D_CONV), F32)],
        scratch_shapes=[pltpu.VMEM((tc + HALO, D_CONV), F32)],
        compiler_params=_params("arbitrary", "arbitrary"), name=name)(u, u, u, u, cw, cb, lg, lb)


def _conv_bwd_norm(dcv, y, lg, lb, *, tc, name):
    T = y.shape[0]

    def body(dcv_ref, y_ref, lg_ref, lb_ref, dy_ref, part_ref):
        yv = y_ref[...]
        mu = jnp.mean(yv, axis=-1, keepdims=True)
        yc = yv - mu
        var = jnp.mean(yc * yc, axis=-1, keepdims=True)
        rstd = lax.rsqrt(var + EPS)
        xhat = yc * rstd
        z = xhat * lg_ref[...] + lb_ref[...]
        sig = _sigmoid(z)
        dz = dcv_ref[...] * (sig * (1.0 + z * (1.0 - sig)))
        dxh = dz * lg_ref[...]
        dy = rstd * (dxh - jnp.mean(dxh, axis=-1, keepdims=True)
                     - xhat * jnp.mean(dxh * xhat, axis=-1, keepdims=True))
        dy_ref[...] = dy

        @pl.when(pl.program_id(0) == 0)
        def _():
            part_ref[...] = jnp.zeros_like(part_ref)

        part_ref[0:1, :] += jnp.sum(dz * xhat, axis=0, keepdims=True)
        part_ref[1:2, :] += jnp.sum(dz, axis=0, keepdims=True)
        part_ref[2:3, :] += jnp.sum(dy, axis=0, keepdims=True)

    tok = pl.BlockSpec((tc, D_CONV), lambda i: (i, 0))
    vec = pl.BlockSpec((1, D_CONV), lambda i: (0, 0))
    return pl.pallas_call(
        body, grid=(T // tc,), in_specs=[tok, tok, vec, vec],
        out_specs=[tok, pl.BlockSpec((8, D_CONV), lambda i: (0, 0))],
        out_shape=[jax.ShapeDtypeStruct((T, D_CONV), F32), jax.ShapeDtypeStruct((8, D_CONV), F32)],
        compiler_params=_params("arbitrary"), name=name)(dcv, y, lg, lb)


def _conv_bwd_taps(dy, glu, u, cw, B, S, *, tc, name):
    nchunk = S // tc
    hb = tc // HALO
    last_hb = B * S // HALO - 1

    def body(dy_ref, dyn_ref, glu_ref, glup_ref, ca_ref, cg_ref, w_ref, dca_ref, dcg_ref, dw_ref, dyp_ref, glp_ref, acc_ref):
        b = pl.program_id(0)
        i = pl.program_id(1)
        dy = dy_ref[...]
        dyp_ref[0:tc, :] = dy
        dyp_ref[tc:, :] = jnp.where(i < nchunk - 1, dyn_ref[...], 0.0)
        glp_ref[0:HALO, :] = jnp.where(i > 0, glup_ref[...], 0.0)
        glp_ref[HALO:, :] = glu_ref[...]

        @pl.when((b == 0) & (i == 0))
        def _():
            dw_ref[...] = jnp.zeros_like(dw_ref)

        for sub in range(tc // SUB):
            acc = jnp.zeros((SUB, D_CONV), F32)
            for k in range(CONV_K):
                acc = acc + dyp_ref[pl.ds(sub * SUB + (CONV_K - 1) - k, SUB), :] * w_ref[pl.ds(k, 1), :]
            acc_ref[sub * SUB:(sub + 1) * SUB, :] = acc
        for k in range(CONV_K):
            dw_ref[k:k + 1, :] += jnp.sum(dy * glp_ref[pl.ds(HALO - (CONV_K - 1) + k, tc), :], axis=0, keepdims=True)
        dglu = acc_ref[...]
        ca = ca_ref[...]
        sig = _sigmoid(cg_ref[...])
        dca_ref[...] = (dglu * sig).astype(BF16)
        dcg_ref[...] = (dglu * ca * sig * (1.0 - sig)).astype(BF16)

    tok = pl.BlockSpec((tc, D_CONV), lambda b, i: (b * nchunk + i, 0))
    nxt = pl.BlockSpec((HALO, D_CONV), lambda b, i: (jnp.minimum((b * nchunk + i + 1) * hb, last_hb), 0))
    prv = pl.BlockSpec((HALO, D_CONV), lambda b, i: (jnp.maximum((b * nchunk + i) * hb - 1, 0), 0))
    return pl.pallas_call(
        body, grid=(B, nchunk),
        in_specs=[tok, nxt, tok, prv,
                  pl.BlockSpec((tc, D_CONV), lambda b, i: (b * nchunk + i, 3)),
                  pl.BlockSpec((tc, D_CONV), lambda b, i: (b * nchunk + i, 4)),
                  pl.BlockSpec((CONV_K, D_CONV), lambda b, i: (0, 0))],
        out_specs=[tok, tok, pl.BlockSpec((32, D_CONV), lambda b, i: (0, 0))],
        out_shape=[jax.ShapeDtypeStruct((B * S, D_CONV), BF16), jax.ShapeDtypeStruct((B * S, D_CONV), BF16),
                   jax.ShapeDtypeStruct((32, D_CONV), F32)],
        scratch_shapes=[pltpu.VMEM((tc + HALO, D_CONV), F32), pltpu.VMEM((tc + HALO, D_CONV), F32),
                        pltpu.VMEM((tc, D_CONV), F32)],
        compiler_params=_params("arbitrary", "arbitrary"), name=name)(dy, dy, glu, glu, u, u, cw)


def _outproj_fwd(h, attn, cv, wout, *, tm, name):
    T, D = h.shape

    def body(h_ref, a_ref, c_ref, w_ref, o_ref):
        o_ref[...] = (h_ref[...] + _dot(a_ref[...].astype(BF16), w_ref[0:D_ATTN, :])
                      + _dot(c_ref[...], w_ref[D_ATTN:, :]))

    tok = pl.BlockSpec((tm, D), lambda i: (i, 0))
    half = pl.BlockSpec((tm, D_ATTN), lambda i: (i, 0))
    return pl.pallas_call(
        body, grid=(T // tm,), in_specs=[tok, half, half, pl.BlockSpec(wout.shape, lambda i: (0, 0))],
        out_specs=tok, out_shape=jax.ShapeDtypeStruct((T, D), F32),
        compiler_params=_params("arbitrary"), name=name)(h, attn, cv, wout)


def _outproj_bwd(dh, attn, cv, wout, *, tm, name):
    T, D = dh.shape

    def body(dh_ref, a_ref, c_ref, w_ref, da_ref, dc_ref, dw_ref):
        @pl.when(pl.program_id(0) == 0)
        def _():
            dw_ref[...] = jnp.zeros_like(dw_ref)

        dhb = dh_ref[...].astype(BF16)
        da_ref[...] = _dot_nt(dhb, w_ref[0:D_ATTN, :])
        dc_ref[...] = _dot_nt(dhb, w_ref[D_ATTN:, :])
        dw_ref[0:D_ATTN, :] += _dot_tn(a_ref[...].astype(BF16), dhb)
        dw_ref[D_ATTN:, :] += _dot_tn(c_ref[...], dhb)

    tok = pl.BlockSpec((tm, D), lambda i: (i, 0))
    half = pl.BlockSpec((tm, D_ATTN), lambda i: (i, 0))
    wspec = pl.BlockSpec(wout.shape, lambda i: (0, 0))
    return pl.pallas_call(
        body, grid=(T // tm,), in_specs=[tok, half, half, wspec], out_specs=[half, half, wspec],
        out_shape=[jax.ShapeDtypeStruct((T, D_ATTN), F32), jax.ShapeDtypeStruct((T, D_ATTN), F32),
                   jax.ShapeDtypeStruct(wout.shape, F32)],
        compiler_params=_params("arbitrary"), name=name)(dh, attn, cv, wout)


def _adamw(w, g, m, v, *, name):
    R, C = w.shape
    tr = R
    for cand in (512, 256, 128, 64, 32, 16, 8):
        if R % cand == 0 and R > cand:
            tr = cand
            break
    c1 = 1.0 - ADAM_B1 ** ADAM_STEP
    c2 = 1.0 - ADAM_B2 ** ADAM_STEP

    def body(w_ref, g_ref, m_ref, v_ref, d_ref, nm_ref, nv_ref):
        gv = g_ref[...]
        nm = ADAM_B1 * m_ref[...] + (1.0 - ADAM_B1) * gv
        nv = ADAM_B2 * v_ref[...] + (1.0 - ADAM_B2) * (gv * gv)
        d_ref[...] = -ADAM_LR * ((nm / c1) / (jnp.sqrt(nv / c2) + ADAM_EPS) + ADAM_WD * w_ref[...])
        nm_ref[...] = nm
        nv_ref[...] = nv

    blk = pl.BlockSpec((tr, C), lambda i: (i, 0))
    return pl.pallas_call(
        body, grid=(R // tr,), in_specs=[blk] * 4, out_specs=[blk] * 3,
        out_shape=[jax.ShapeDtypeStruct((R, C), F32)] * 3,
        compiler_params=_params("arbitrary"), name=name)(w, g, m, v)


TM = 512
TK = 1024
TC = 256


def _local_step(x, tgt, w):
    B, S, D = x.shape
    T = B * S
    x2 = x.reshape(T, D)
    t2 = tgt.reshape(T, D)
    ones = jnp.ones((1, D_ATTN), F32)
    gains = jnp.stack([jnp.tile(w["q_norm"], (1, HEADS)), jnp.tile(w["k_norm"], (1, HEADS)), ones])

    h1, n1, G1, U1 = _ffn_fwd(x2, w["ffn1_norm"], w["wg1"], w["wu1"], w["wd1"], None, tm=TM, name="ffn1_fwd")
    u, n2 = _inproj_fwd(h1, w["mix_norm"], w["win"], tm=TM, name="inproj_fwd")
    qkv = _qkv_prep(u, gains, B, S, name="qkv_prep")
    qkv = qkv.reshape(3, N_PATTERNS, T // QBLK, QBLK, D_ATTN)
    o3, lse3 = _attn_fwd(qkv, name="attn_fwd")
    attn, lse = _attn_combine(o3, lse3, B, S, name="attn_combine")
    cv, glu, yconv = _conv_fwd(u, w["conv_w"], w["conv_b"], w["conv_ln_g"], w["conv_ln_b"], B, S, tc=TC, name="conv_fwd")
    h2 = _outproj_fwd(h1, attn, cv, w["wout"], tm=TM, name="outproj_fwd")
    dh3, n3, G2, U2, loss = _ffn_fwd(h2, w["ffn2_norm"], w["wg2"], w["wu2"], w["wd2"], t2, tm=TM, name="ffn2_fwd")

    g = {}
    dG, dU, A, dy, dh2, g["ffn2_norm"] = _ffn_bwd_act(dh3, h2, w["ffn2_norm"], G2, U2, w["wg2"], w["wu2"], w["wd2"],
                                                    tm=TM, name="ffn2_bwd_act")
    g["wg2"], g["wu2"], g["wd2"] = _ffn_bwd_w(n3, dy, dG, dU, A, tk=TK, name="ffn2_bwd_w")
    dattn, dcv, g["wout"] = _outproj_bwd(dh2, attn, cv, w["wout"], tm=TM, name="outproj_bwd")
    dyc, cpart = _conv_bwd_norm(dcv, yconv, w["conv_ln_g"], w["conv_ln_b"], tc=TC, name="conv_bwd_norm")
    dca, dcg, dcw = _conv_bwd_taps(dyc, glu, u, w["conv_w"], B, S, tc=TC, name="conv_bwd_taps")
    do3, lseb3, dd3 = _attn_bwd_prep(dattn, attn, lse, B, S, name="attn_bwd_prep")
    nb = T // QBLK
    shp = (N_PATTERNS, nb, QBLK, D_ATTN)
    cur, prev = _attn_bwd(qkv, do3.reshape(shp), lseb3.reshape(shp), dd3.reshape(shp), name="attn_bwd")
    du_qkv, dgains = _attn_grad_combine(cur, prev, u, gains, B, S, name="attn_grad_combine")
    du = jnp.concatenate([du_qkv, dca, dcg], axis=1)
    dh1, g["mix_norm"] = _inproj_bwd_act(du, dh2, h1, w["mix_norm"], w["win"], tm=TM, name="inproj_bwd_act")
    g["win"] = _inproj_bwd_w(n2, du, w["win"].shape[0], tk=TK, name="inproj_bwd_w")
    dG, dU, A, dy, dx, g["ffn1_norm"] = _ffn_bwd_act(dh1, x2, w["ffn1_norm"], G1, U1, w["wg1"], w["wu1"], w["wd1"],
                                                   tm=TM, name="ffn1_bwd_act")
    g["wg1"], g["wu1"], g["wd1"] = _ffn_bwd_w(n1, dy, dG, dU, A, tk=TK, name="ffn1_bwd_w")

    g["q_norm"] = dgains[0].reshape(HEADS, HEAD_DIM).sum(axis=0, keepdims=True)
    g["k_norm"] = dgains[1].reshape(HEADS, HEAD_DIM).sum(axis=0, keepdims=True)
    g["conv_ln_g"] = cpart[0:1]
    g["conv_ln_b"] = cpart[1:2]
    g["conv_b"] = cpart[2:3]
    g["conv_w"] = dcw[:CONV_K]
    return loss, dx.reshape(B, S, D), g


N_CHIPS = 4
N_DEV = 8
ANY = pl.BlockSpec(memory_space=pl.ANY)
VMEM_SPEC = pl.BlockSpec(memory_space=pltpu.VMEM)


def _place():
    x, y, c = lax.axis_index("x"), lax.axis_index("y"), lax.axis_index("c")
    chips = [(1 - x, y), (x, 1 - y), (1 - x, 1 - y)]
    return x, y, c, 2 * x + y, chips, [2 * px + py for px, py in chips]


def _remote(src, dst, send_sem, recv_sem, device):
    return pltpu.make_async_remote_copy(src_ref=src, dst_ref=dst, send_sem=send_sem, recv_sem=recv_sem,
                                        device_id=device, device_id_type=MESH)


def _gather_weights(shards, dtypes, *, name):
    n = len(shards)
    halves = [s.reshape(2, s.shape[0] // 2, s.shape[1]) for s in shards]

    def body(*refs):
        ins, outs, vms = refs[:n], refs[n:2 * n], refs[2 * n:3 * n]
        send_sems, recv_sems, loc_sems = refs[3 * n:]
        x, y, c, me, chips, cidx = _place()
        sibling = (x, y, 1 - c)
        for a in range(n):
            vms[a][...] = ins[a][...].astype(dtypes[a])
        local, first, passed = [], [], []
        for a in range(n):
            lc = pltpu.make_async_copy(vms[a], outs[a].at[me], loc_sems.at[a])
            lc.start()
            local.append(lc)
            for j, chip in enumerate(chips):
                cp = _remote(vms[a].at[c], outs[a].at[me, c], send_sems.at[6 * a + j], recv_sems.at[6 * a + j], (*chip, c))
                cp.start()
                first.append(cp)
        for a in range(n):
            for j, chip in enumerate(chips):
                land = outs[a].at[cidx[j], c]
                _remote(land, land, send_sems.at[6 * a + j], recv_sems.at[6 * a + j], (*chip, c)).wait_recv()
                fw = _remote(land, land, send_sems.at[6 * a + 3 + j], recv_sems.at[6 * a + 3 + j], sibling)
                fw.start()
                passed.append(fw)
        for a in range(n):
            for j in range(3):
                land = outs[a].at[cidx[j], 1 - c]
                _remote(land, land, send_sems.at[6 * a + 3 + j], recv_sems.at[6 * a + 3 + j], sibling).wait_recv()
        for cp in first + passed:
            cp.wait_send()
        for lc in local:
            lc.wait()

    outs = pl.pallas_call(
        body, in_specs=[VMEM_SPEC] * n, out_specs=[ANY] * n,
        out_shape=[jax.ShapeDtypeStruct((N_CHIPS,) + h.shape, dt) for h, dt in zip(halves, dtypes)],
        scratch_shapes=[pltpu.VMEM(h.shape, dt) for h, dt in zip(halves, dtypes)]
        + [pltpu.SemaphoreType.DMA((6 * n,)), pltpu.SemaphoreType.DMA((6 * n,)), pltpu.SemaphoreType.DMA((n,))],
        compiler_params=pltpu.CompilerParams(vmem_limit_bytes=VMEM_LIMIT), name=name)(*halves)
    return [o.reshape((N_CHIPS,) + s.shape) for o, s in zip(outs, shards)]


def _exchange_halves(grads, *, name):
    n = len(grads)

    def body(*refs):
        ins, outs = refs[:n], refs[n:2 * n]
        send_sems, recv_sems = refs[2 * n:]
        x, y, c, me, chips, cidx = _place()
        cps = []
        for a in range(n):
            src = ins[a].at[pl.ds(0, ins[a].shape[0]), 1 - c]
            cp = _remote(src, outs[a], send_sems.at[a], recv_sems.at[a], (x, y, 1 - c))
            cp.start()
            cps.append(cp)
        for cp in cps:
            cp.wait()

    return pl.pallas_call(
        body, in_specs=[ANY] * n, out_specs=[ANY] * n,
        out_shape=[jax.ShapeDtypeStruct((g.shape[0],) + g.shape[2:], F32) for g in grads],
        scratch_shapes=[pltpu.SemaphoreType.DMA((n,)), pltpu.SemaphoreType.DMA((n,))], name=name)(*grads)


def _row_block(rows):
    for cand in (256, 176, 128, 64, 32, 16, 8):
        if rows % cand == 0:
            return cand
    return rows


def _add_own_half(g, r, core, *, name):
    ns, _, rh, cdim = g.shape
    tr = _row_block(rh)

    def body(c_ref, g_ref, r_ref, o_ref):
        o_ref[...] = g_ref[...] + r_ref[...]

    return pl.pallas_call(
        body,
        grid_spec=pltpu.PrefetchScalarGridSpec(
            num_scalar_prefetch=1, grid=(ns, rh // tr),
            in_specs=[pl.BlockSpec((None, None, tr, cdim), lambda j, i, c_ref: (j, c_ref[0], i, 0)),
                      pl.BlockSpec((None, tr, cdim), lambda j, i, c_ref: (j, i, 0))],
            out_specs=pl.BlockSpec((None, tr, cdim), lambda j, i, c_ref: (j, i, 0))),
        out_shape=jax.ShapeDtypeStruct((ns, rh, cdim), F32),
        compiler_params=_params("arbitrary", "arbitrary"), name=name)(core, g, r)


def _scatter_to_chips(parts, *, name):
    n = len(parts)

    def body(*refs):
        ins, outs = refs[:n], refs[n:2 * n]
        send_sems, recv_sems, loc_sems = refs[2 * n:]
        x, y, c, me, chips, cidx = _place()
        cps = []
        for a in range(n):
            lc = pltpu.make_async_copy(ins[a].at[me], outs[a].at[me], loc_sems.at[a])
            lc.start()
            cps.append(lc)
            for j, chip in enumerate(chips):
                cp = _remote(ins[a].at[cidx[j]], outs[a].at[me], send_sems.at[3 * a + j], recv_sems.at[3 * a + j], (*chip, c))
                cp.start()
        for a in range(n):
            for j, chip in enumerate(chips):
                land = outs[a].at[cidx[j]]
                cp = _remote(land, land, send_sems.at[3 * a + j], recv_sems.at[3 * a + j], (*chip, c))
                cp.wait_recv()
                cp.wait_send()
        for lc in cps:
            lc.wait()

    return pl.pallas_call(
        body, in_specs=[ANY] * n, out_specs=[ANY] * n,
        out_shape=[jax.ShapeDtypeStruct(p.shape, F32) for p in parts],
        scratch_shapes=[pltpu.SemaphoreType.DMA((3 * n,)), pltpu.SemaphoreType.DMA((3 * n,)),
                        pltpu.SemaphoreType.DMA((n,))], name=name)(*parts)


def _sum_slots(q, *, name):
    ns, rh, cdim = q.shape
    tr = _row_block(rh)

    def body(q_ref, o_ref):
        o_ref[...] = ((q_ref[0] + q_ref[1]) + q_ref[2]) + q_ref[3]

    return pl.pallas_call(
        body, grid=(rh // tr,), in_specs=[pl.BlockSpec((ns, tr, cdim), lambda i: (0, i, 0))],
        out_specs=pl.BlockSpec((tr, cdim), lambda i: (i, 0)), out_shape=jax.ShapeDtypeStruct((rh, cdim), F32),
        compiler_params=_params("arbitrary"), name=name)(q)


def _join_halves(halves, *, name):
    n = len(halves)

    def body(*refs):
        ins, outs = refs[:n], refs[n:2 * n]
        send_sems, recv_sems, loc_sems = refs[2 * n:]
        x, y, c, me, chips, cidx = _place()
        cps, lcs = [], []
        for a in range(n):
            lc = pltpu.make_async_copy(ins[a], outs[a].at[c], loc_sems.at[a])
            lc.start()
            lcs.append(lc)
            cp = _remote(ins[a], outs[a].at[c], send_sems.at[a], recv_sems.at[a], (x, y, 1 - c))
            cp.start()
            cps.append(cp)
        for a in range(n):
            land = outs[a].at[1 - c]
            _remote(land, land, send_sems.at[a], recv_sems.at[a], (x, y, 1 - c)).wait_recv()
        for cp in cps:
            cp.wait_send()
        for lc in lcs:
            lc.wait()

    return pl.pallas_call(
        body, in_specs=[ANY] * n, out_specs=[ANY] * n,
        out_shape=[jax.ShapeDtypeStruct((2,) + h.shape, F32) for h in halves],
        scratch_shapes=[pltpu.SemaphoreType.DMA((n,)), pltpu.SemaphoreType.DMA((n,)), pltpu.SemaphoreType.DMA((n,))],
        name=name)(*halves)


def _reduce_scatter(grads, core):
    views = [g.reshape(N_CHIPS, 2, g.shape[1] // 2, g.shape[2]) for g in grads]
    got = _exchange_halves(views, name="rs_exchange_halves")
    parts = [_add_own_half(v, r, core, name=f"rs_add_half_{a}") for a, (v, r) in enumerate(zip(views, got))]
    slots = _scatter_to_chips(parts, name="rs_scatter_to_chips")
    halves = [_sum_slots(q, name=f"rs_sum_slots_{a}") for a, q in enumerate(slots)]
    full = _join_halves(halves, name="rs_join_halves")
    return [f.reshape(g.shape[1], g.shape[2]) for f, g in zip(full, grads)]


def _allreduce_small(pack, *, name):
    rows = pack.shape[0]

    def body(p_ref, o_ref, buf_ref, send_sems, recv_sems):
        x, y, c = lax.axis_index("x"), lax.axis_index("y"), lax.axis_index("c")
        me = 4 * x + 2 * y + c
        buf_ref[me] = p_ref[...]
        cps = []
        for k in range(1, N_DEV):
            peer = tuple(1 - v if (k >> s) & 1 else v for v, s in ((x, 2), (y, 1), (c, 0)))
            cp = _remote(p_ref, buf_ref.at[me], send_sems.at[k - 1], recv_sems.at[k - 1], peer)
            cp.start()
            cps.append(cp)
        for k in range(1, N_DEV):
            src = 4 * (x ^ ((k >> 2) & 1)) + 2 * (y ^ ((k >> 1) & 1)) + (c ^ (k & 1))
            land = buf_ref.at[src]
            _remote(land, land, send_sems.at[k - 1], recv_sems.at[k - 1], (x, y, c)).wait_recv()
        acc = buf_ref[0]
        for d in range(1, N_DEV):
            acc = acc + buf_ref[d]
        o_ref[...] = acc
        for cp in cps:
            cp.wait_send()

    return pl.pallas_call(
        body, in_specs=[VMEM_SPEC], out_specs=VMEM_SPEC, out_shape=jax.ShapeDtypeStruct(pack.shape, F32),
        scratch_shapes=[pltpu.VMEM((N_DEV, rows, LANES), F32), pltpu.SemaphoreType.DMA((N_DEV - 1,)),
                        pltpu.SemaphoreType.DMA((N_DEV - 1,))], name=name)(pack)


SMALL = ("ffn1_norm", "mix_norm", "q_norm", "k_norm", "conv_b", "conv_ln_g", "conv_ln_b", "ffn2_norm", "conv_w")
BIG = ("ffn1_w_gate", "ffn1_w_up", "ffn1_w_down", "w_in", "w_out", "ffn2_w_gate", "ffn2_w_up", "ffn2_w_down")
WEIGHTS = ("ffn1_norm", "ffn1_w_gate", "ffn1_w_up", "ffn1_w_down", "mix_norm", "w_in", "q_norm", "k_norm",
           "conv_w", "conv_b", "conv_ln_g", "conv_ln_b", "w_out", "ffn2_norm", "ffn2_w_gate", "ffn2_w_up",
           "ffn2_w_down")
SUBLANES = 8


def _pack(parts):
    rows = []
    for p in parts:
        flat = p.reshape(-1)
        tile = SUBLANES * LANES
        padded = -(-flat.shape[0] // tile) * tile
        rows.append(jnp.pad(flat, (0, padded - flat.shape[0])).reshape(-1, LANES))
    return jnp.concatenate(rows, axis=0)


def _unpack(pack, shapes):
    out, row = [], 0
    for shp in shapes:
        size = shp[0] * shp[1]
        tile = SUBLANES * LANES
        nrows = -(-size // tile) * SUBLANES
        out.append(pack[row:row + nrows].reshape(-1)[:size].reshape(shp))
        row += nrows
    return out


def kernel(x, ffn1_norm, ffn1_w_gate, ffn1_w_up, ffn1_w_down, mix_norm, w_in, q_norm, k_norm, conv_w, conv_b, conv_ln_g, conv_ln_b, w_out, ffn2_norm, ffn2_w_gate, ffn2_w_up, ffn2_w_down, loss_target, m_ffn1_norm, m_ffn1_w_gate, m_ffn1_w_up, m_ffn1_w_down, m_mix_norm, m_w_in, m_q_norm, m_k_norm, m_conv_w, m_conv_b, m_conv_ln_g, m_conv_ln_b, m_w_out, m_ffn2_norm, m_ffn2_w_gate, m_ffn2_w_up, m_ffn2_w_down, v_ffn1_norm, v_ffn1_w_gate, v_ffn1_w_up, v_ffn1_w_down, v_mix_norm, v_w_in, v_q_norm, v_k_norm, v_conv_w, v_conv_b, v_conv_ln_g, v_conv_ln_b, v_w_out, v_ffn2_norm, v_ffn2_w_gate, v_ffn2_w_up, v_ffn2_w_down):
    wts = dict(ffn1_norm=ffn1_norm, ffn1_w_gate=ffn1_w_gate[0], ffn1_w_up=ffn1_w_up[0], ffn1_w_down=ffn1_w_down[0],
               mix_norm=mix_norm, w_in=w_in[0], q_norm=q_norm, k_norm=k_norm, conv_w=conv_w[0], conv_b=conv_b,
               conv_ln_g=conv_ln_g, conv_ln_b=conv_ln_b, w_out=w_out[0], ffn2_norm=ffn2_norm,
               ffn2_w_gate=ffn2_w_gate[0], ffn2_w_up=ffn2_w_up[0], ffn2_w_down=ffn2_w_down[0])
    mom = dict(ffn1_norm=m_ffn1_norm, ffn1_w_gate=m_ffn1_w_gate[0], ffn1_w_up=m_ffn1_w_up[0], ffn1_w_down=m_ffn1_w_down[0],
               mix_norm=m_mix_norm, w_in=m_w_in[0], q_norm=m_q_norm, k_norm=m_k_norm, conv_w=m_conv_w[0], conv_b=m_conv_b,
               conv_ln_g=m_conv_ln_g, conv_ln_b=m_conv_ln_b, w_out=m_w_out[0], ffn2_norm=m_ffn2_norm,
               ffn2_w_gate=m_ffn2_w_gate[0], ffn2_w_up=m_ffn2_w_up[0], ffn2_w_down=m_ffn2_w_down[0])
    var = dict(ffn1_norm=v_ffn1_norm, ffn1_w_gate=v_ffn1_w_gate[0], ffn1_w_up=v_ffn1_w_up[0], ffn1_w_down=v_ffn1_w_down[0],
               mix_norm=v_mix_norm, w_in=v_w_in[0], q_norm=v_q_norm, k_norm=v_k_norm, conv_w=v_conv_w[0], conv_b=v_conv_b,
               conv_ln_g=v_conv_ln_g, conv_ln_b=v_conv_ln_b, w_out=v_w_out[0], ffn2_norm=v_ffn2_norm,
               ffn2_w_gate=v_ffn2_w_gate[0], ffn2_w_up=v_ffn2_w_up[0], ffn2_w_down=v_ffn2_w_down[0])
    chip = 2 * lax.axis_index("x") + lax.axis_index("y")
    core = lax.axis_index("c").astype(jnp.int32).reshape(1)

    wg1, wu1, wd1 = _gather_weights([wts["ffn1_w_gate"], wts["ffn1_w_up"], wts["ffn1_w_down"]], [BF16] * 3,
                                    name="gather_ffn1")
    taps = jnp.pad(wts["conv_w"], ((0, 1), (0, 0)))
    win, wout, taps4 = _gather_weights([wts["w_in"], wts["w_out"], taps], [BF16, BF16, F32], name="gather_mix")
    wg2, wu2, wd2 = _gather_weights([wts["ffn2_w_gate"], wts["ffn2_w_up"], wts["ffn2_w_down"]], [BF16] * 3,
                                    name="gather_ffn2")
    conv_full = taps4.transpose(1, 0, 2).reshape(CONV_K + 1, D_CONV)[:CONV_K]
    w = dict(ffn1_norm=ffn1_norm, mix_norm=mix_norm, ffn2_norm=ffn2_norm, q_norm=q_norm, k_norm=k_norm,
             conv_w=conv_full, conv_b=conv_b, conv_ln_g=conv_ln_g, conv_ln_b=conv_ln_b,
             wg1=wg1, wu1=wu1, wd1=wd1, wg2=wg2, wu2=wu2, wd2=wd2, win=win,
             wout=wout.reshape(N_CHIPS * wout.shape[1], wout.shape[2]))

    loss_part, grad_x, g = _local_step(x, loss_target, w)

    big_parts = [g["wg1"], g["wu1"], g["wd1"], g["win"], g["wout"].reshape(wout.shape), g["wg2"], g["wu2"], g["wd2"]]
    big_grads = dict(zip(BIG, _reduce_scatter(big_parts, core)))

    small_shapes = [g[n].shape for n in SMALL] + [(SUBLANES, LANES)]
    red = _allreduce_small(_pack([g[n] for n in SMALL] + [loss_part]), name="allreduce_small")
    small = dict(zip(SMALL + ("loss",), _unpack(red, small_shapes)))
    loss = small["loss"][0, 0]
    small["conv_w"] = lax.dynamic_slice_in_dim(small["conv_w"], chip * LANES, LANES, axis=1)

    grads, delta, new_m, new_v = {}, {}, {}, {}
    for n in BIG:
        grads[n] = big_grads[n]
        delta[n], new_m[n], new_v[n] = _adamw(wts[n], grads[n], mom[n], var[n], name=f"adamw_{n}")
    shapes = [wts[n].shape for n in SMALL]
    packs = [_pack([src[n] for n in SMALL]) for src in (wts, small, mom, var)]
    outs = _adamw(*packs, name="adamw_small")
    for dst, pk in zip((delta, new_m, new_v), outs):
        dst.update(zip(SMALL, _unpack(pk, shapes)))
    for n in SMALL:
        grads[n] = small[n]

    def shaped(d, n):
        return d[n].reshape((1,) + d[n].shape) if n in BIG or n == "conv_w" else d[n]

    return (loss, grad_x, *[shaped(grads, n) for n in WEIGHTS], *[shaped(delta, n) for n in WEIGHTS],
            *[shaped(new_m, n) for n in WEIGHTS], *[shaped(new_v, n) for n in WEIGHTS])
```

```python
import functools

import jax
import jax.numpy as jnp
from jax import lax
from jax.experimental import pallas as pl
from jax.experimental.pallas import tpu as pltpu

F32 = jnp.float32
BF16 = jnp.bfloat16

EPS = 1e-6
HEADS = 8
HEAD_DIM = 64
D_ATTN = HEADS * HEAD_DIM
D_CONV = 512
CONV_K = 31
QBLK = 128
N_PATTERNS = 3
DILATIONS = (1, 4, 16)
LANES = 128
NEG = -1e30

ADAM_LR = 0.001
ADAM_B1 = 0.9
ADAM_B2 = 0.999
ADAM_EPS = 1e-08
ADAM_WD = 0.01
ADAM_STEP = 10

VMEM_LIMIT = 56 * 1024 * 1024
MESH = pl.DeviceIdType.MESH

NT_DIMS = (((1,), (1,)), ((), ()))
TN_DIMS = (((0,), (0,)), ((), ()))


def _params(*sem):
    return pltpu.CompilerParams(dimension_semantics=sem, vmem_limit_bytes=VMEM_LIMIT)


def _dot(a, b):
    return jnp.dot(a, b, preferred_element_type=F32)


def _dot_nt(a, b):
    return lax.dot_general(a, b, NT_DIMS, preferred_element_type=F32)


def _dot_tn(a, b):
    return lax.dot_general(a, b, TN_DIMS, preferred_element_type=F32)


def _sigmoid(x):
    return 1.0 / (1.0 + jnp.exp(-x))


def _seg_mean(v, e_ref, width):
    hi = v.astype(BF16)
    lo = (v - hi.astype(F32)).astype(BF16)
    e = e_ref[...]
    return (_dot(hi, e) + _dot(lo, e)) * (1.0 / width)


def _seg_matrix(n):
    i = jnp.arange(n)
    return (i[:, None] // HEAD_DIM == i[None, :] // HEAD_DIM).astype(BF16)


def _ffn_fwd(x, gain, wg, wu, wd, tgt, *, tm, name):
    T, D = x.shape
    NS, _, Fs = wg.shape
    with_loss = tgt is not None

    def body(*refs):
        if with_loss:
            x_ref, g_ref, wg_ref, wu_ref, wd_ref, t_ref, h_ref, n_ref, G_ref, U_ref, loss_ref, acc_ref = refs
        else:
            x_ref, g_ref, wg_ref, wu_ref, wd_ref, h_ref, n_ref, G_ref, U_ref, acc_ref = refs
        i = pl.program_id(0)
        j = pl.program_id(1)

        @pl.when(j == 0)
        def _():
            xv = x_ref[...]
            r = lax.rsqrt(jnp.mean(xv * xv, axis=-1, keepdims=True) + EPS)
            n_ref[...] = (xv * r * g_ref[...]).astype(BF16)
            acc_ref[...] = jnp.zeros_like(acc_ref)

        n = n_ref[...]
        G = _dot(n, wg_ref[...])
        U = _dot(n, wu_ref[...])
        G_ref[...] = G.astype(BF16)
        U_ref[...] = U.astype(BF16)
        A = (G * _sigmoid(G) * U).astype(BF16)
        acc_ref[...] += _dot(A, wd_ref[...])

        @pl.when(j == NS - 1)
        def _():
            h = x_ref[...] + 0.5 * acc_ref[...]
            if with_loss:
                e = h - t_ref[...]
                h_ref[...] = e * (1.0 / D)

                @pl.when(i == 0)
                def _():
                    loss_ref[...] = jnp.zeros_like(loss_ref)

                loss_ref[...] += jnp.sum(e * e) * (0.5 / D)
            else:
                h_ref[...] = h

    tok = pl.BlockSpec((tm, D), lambda i, j: (i, 0))
    in_specs = [tok, pl.BlockSpec((1, D), lambda i, j: (0, 0)),
                pl.BlockSpec((None, D, Fs), lambda i, j: (j, 0, 0)),
                pl.BlockSpec((None, D, Fs), lambda i, j: (j, 0, 0)),
                pl.BlockSpec((None, Fs, D), lambda i, j: (j, 0, 0))]
    args = [x, gain, wg, wu, wd]
    act = pl.BlockSpec((None, tm, Fs), lambda i, j: (j, i, 0))
    out_shape = [jax.ShapeDtypeStruct((T, D), F32), jax.ShapeDtypeStruct((T, D), BF16),
                 jax.ShapeDtypeStruct((NS, T, Fs), BF16), jax.ShapeDtypeStruct((NS, T, Fs), BF16)]
    out_specs = [tok, tok, act, act]
    if with_loss:
        in_specs.append(tok)
        args.append(tgt)
        out_shape.append(jax.ShapeDtypeStruct((8, LANES), F32))
        out_specs.append(pl.BlockSpec((8, LANES), lambda i, j: (0, 0)))
    return pl.pallas_call(
        body, grid=(T // tm, NS), in_specs=in_specs, out_specs=out_specs, out_shape=out_shape,
        scratch_shapes=[pltpu.VMEM((tm, D), F32)],
        compiler_params=_params("arbitrary", "arbitrary"), name=name)(*args)


def _rms_bwd(xv, gain, dn):
    r = lax.rsqrt(jnp.mean(xv * xv, axis=-1, keepdims=True) + EPS)
    xhat = xv * r
    dxh = dn * gain
    dx = r * (dxh - xhat * jnp.mean(dxh * xhat, axis=-1, keepdims=True))
    dg = jnp.sum(dn * xhat, axis=0, keepdims=True)
    return dx, dg


def _ffn_bwd_act(dh, x, gain, G, U, wg, wu, wd, *, tm, name):
    T, D = x.shape
    NS, _, Fs = wg.shape

    def body(dh_ref, x_ref, g_ref, G_ref, U_ref, wg_ref, wu_ref, wd_ref,
             dG_ref, dU_ref, A_ref, dy_ref, dx_ref, dg_ref, acc_ref):
        i = pl.program_id(0)
        j = pl.program_id(1)

        @pl.when(j == 0)
        def _():
            dy_ref[...] = (0.5 * dh_ref[...]).astype(BF16)
            acc_ref[...] = jnp.zeros_like(acc_ref)

        @pl.when((i == 0) & (j == 0))
        def _():
            dg_ref[...] = jnp.zeros_like(dg_ref)

        Gv = G_ref[...].astype(F32)
        Uv = U_ref[...].astype(F32)
        sig = _sigmoid(Gv)
        s = Gv * sig
        dA = _dot_nt(dy_ref[...], wd_ref[...])
        dG = (dA * Uv * (sig * (1.0 + Gv * (1.0 - sig)))).astype(BF16)
        dU = (dA * s).astype(BF16)
        dG_ref[...] = dG
        dU_ref[...] = dU
        A_ref[...] = (s * Uv).astype(BF16)
        acc_ref[...] += _dot_nt(dG, wg_ref[...]) + _dot_nt(dU, wu_ref[...])

        @pl.when(j == NS - 1)
        def _():
            dx, dg = _rms_bwd(x_ref[...], g_ref[...], acc_ref[...])
            dx_ref[...] = dh_ref[...] + dx
            dg_ref[...] += dg

    tok = pl.BlockSpec((tm, D), lambda i, j: (i, 0))
    act = pl.BlockSpec((None, tm, Fs), lambda i, j: (j, i, 0))
    vec = pl.BlockSpec((1, D), lambda i, j: (0, 0))
    return pl.pallas_call(
        body, grid=(T // tm, NS),
        in_specs=[tok, tok, vec, act, act,
                  pl.BlockSpec((None, D, Fs), lambda i, j: (j, 0, 0)),
                  pl.BlockSpec((None, D, Fs), lambda i, j: (j, 0, 0)),
                  pl.BlockSpec((None, Fs, D), lambda i, j: (j, 0, 0))],
        out_specs=[act, act, act, tok, tok, vec],
        out_shape=[jax.ShapeDtypeStruct((NS, T, Fs), BF16)] * 3
        + [jax.ShapeDtypeStruct((T, D), BF16), jax.ShapeDtypeStruct((T, D), F32),
           jax.ShapeDtypeStruct((1, D), F32)],
        scratch_shapes=[pltpu.VMEM((tm, D), F32)],
        compiler_params=_params("arbitrary", "arbitrary"), name=name)(dh, x, gain, G, U, wg, wu, wd)


def _ffn_bwd_w(n, dy, dG, dU, A, *, tk, name):
    T, D = n.shape
    NS, _, Fs = dG.shape

    def body(n_ref, dy_ref, dG_ref, dU_ref, A_ref, wg_ref, wu_ref, wd_ref):
        @pl.when(pl.program_id(1) == 0)
        def _():
            wg_ref[...] = jnp.zeros_like(wg_ref)
            wu_ref[...] = jnp.zeros_like(wu_ref)
            wd_ref[...] = jnp.zeros_like(wd_ref)

        nv = n_ref[...]
        wg_ref[...] += _dot_tn(nv, dG_ref[...])
        wu_ref[...] += _dot_tn(nv, dU_ref[...])
        wd_ref[...] += _dot_tn(A_ref[...], dy_ref[...])

    tok = pl.BlockSpec((tk, D), lambda j, k: (k, 0))
    act = pl.BlockSpec((None, tk, Fs), lambda j, k: (j, k, 0))
    return pl.pallas_call(
        body, grid=(NS, T // tk), in_specs=[tok, tok, act, act, act],
        out_specs=[pl.BlockSpec((None, D, Fs), lambda j, k: (j, 0, 0)),
                   pl.BlockSpec((None, D, Fs), lambda j, k: (j, 0, 0)),
                   pl.BlockSpec((None, Fs, D), lambda j, k: (j, 0, 0))],
        out_shape=[jax.ShapeDtypeStruct((NS, D, Fs), F32), jax.ShapeDtypeStruct((NS, D, Fs), F32),
                   jax.ShapeDtypeStruct((NS, Fs, D), F32)],
        compiler_params=_params("arbitrary", "arbitrary"), name=name)(n, dy, dG, dU, A)


def _inproj_fwd(h, gain, win, *, tm, name):
    T, D = h.shape
    NS, _, Cs = win.shape

    def body(h_ref, g_ref, w_ref, u_ref, n_ref):
        @pl.when(pl.program_id(1) == 0)
        def _():
            xv = h_ref[...]
            r = lax.rsqrt(jnp.mean(xv * xv, axis=-1, keepdims=True) + EPS)
            n_ref[...] = (xv * r * g_ref[...]).astype(BF16)

        u_ref[...] = _dot(n_ref[...], w_ref[...])

    tok = pl.BlockSpec((tm, D), lambda i, j: (i, 0))
    return pl.pallas_call(
        body, grid=(T // tm, NS),
        in_specs=[tok, pl.BlockSpec((1, D), lambda i, j: (0, 0)),
                  pl.BlockSpec((None, D, Cs), lambda i, j: (j, 0, 0))],
        out_specs=[pl.BlockSpec((tm, Cs), lambda i, j: (i, j)), tok],
        out_shape=[jax.ShapeDtypeStruct((T, NS * Cs), F32), jax.ShapeDtypeStruct((T, D), BF16)],
        compiler_params=_params("arbitrary", "arbitrary"), name=name)(h, gain, win)


def _inproj_bwd_act(du, dh, h, gain, win, *, tm, name):
    T, D = h.shape
    NS, _, Cs = win.shape

    def body(du_ref, dh_ref, h_ref, g_ref, w_ref, dx_ref, dg_ref, acc_ref):
        i = pl.program_id(0)
        j = pl.program_id(1)

        @pl.when(j == 0)
        def _():
            acc_ref[...] = jnp.zeros_like(acc_ref)

        @pl.when((i == 0) & (j == 0))
        def _():
            dg_ref[...] = jnp.zeros_like(dg_ref)

        acc_ref[...] += _dot_nt(du_ref[...], w_ref[...])

        @pl.when(j == NS - 1)
        def _():
            dx, dg = _rms_bwd(h_ref[...], g_ref[...], acc_ref[...])
            dx_ref[...] = dh_ref[...] + dx
            dg_ref[...] += dg

    tok = pl.BlockSpec((tm, D), lambda i, j: (i, 0))
    vec = pl.BlockSpec((1, D), lambda i, j: (0, 0))
    return pl.pallas_call(
        body, grid=(T // tm, NS),
        in_specs=[pl.BlockSpec((tm, Cs), lambda i, j: (i, j)), tok, tok, vec,
                  pl.BlockSpec((None, D, Cs), lambda i, j: (j, 0, 0))],
        out_specs=[tok, vec],
        out_shape=[jax.ShapeDtypeStruct((T, D), F32), jax.ShapeDtypeStruct((1, D), F32)],
        scratch_shapes=[pltpu.VMEM((tm, D), F32)],
        compiler_params=_params("arbitrary", "arbitrary"), name=name)(du, dh, h, gain, win)


def _inproj_bwd_w(n, du, ns, *, tk, name):
    T, D = n.shape
    Cs = du.shape[1] // ns

    def body(n_ref, du_ref, w_ref):
        @pl.when(pl.program_id(1) == 0)
        def _():
            w_ref[...] = jnp.zeros_like(w_ref)

        w_ref[...] += _dot_tn(n_ref[...], du_ref[...])

    return pl.pallas_call(
        body, grid=(ns, T // tk),
        in_specs=[pl.BlockSpec((tk, D), lambda j, k: (k, 0)), pl.BlockSpec((tk, Cs), lambda j, k: (k, j))],
        out_specs=pl.BlockSpec((None, D, Cs), lambda j, k: (j, 0, 0)),
        out_shape=jax.ShapeDtypeStruct((ns, D, Cs), F32),
        compiler_params=_params("arbitrary", "arbitrary"), name=name)(n, du)


def _permute_out(src_ref, out_ref, cast):
    S = src_ref.shape[1]
    for p, d in enumerate(DILATIONS):
        L = S // d
        for cc in range(4):
            cols = slice(cc * LANES, (cc + 1) * LANES)
            if d == 1:
                out_ref[p, :, cols] = src_ref[cc].astype(cast)
            else:
                for r in range(d):
                    out_ref[p, r * L:(r + 1) * L, cols] = src_ref[cc, pl.ds(r, L, stride=d), :].astype(cast)


def _unpermute_in(get_block, dst_ref, p, S):
    d = DILATIONS[p]
    L = S // d
    if d == 1:
        dst_ref[...] = get_block(0, S)
    else:
        for r in range(d):
            dst_ref[pl.ds(r, L, stride=d), :] = get_block(r * L, L)


def _qkv_prep(u, gains, B, S, *, name):
    emat = _seg_matrix(D_ATTN)

    def body(u_ref, g_ref, e_ref, out_ref, scr_ref):
        c = pl.program_id(1)
        xv = u_ref[...]
        ms = _seg_mean(xv * xv, e_ref, HEAD_DIM)
        r = jnp.where(c < 2, lax.rsqrt(ms + EPS), 1.0)
        yv = xv * r * g_ref[...]
        for cc in range(4):
            scr_ref[cc] = yv[:, cc * LANES:(cc + 1) * LANES]
        _permute_out(scr_ref, out_ref, BF16)

    return pl.pallas_call(
        body, grid=(B, 3),
        in_specs=[pl.BlockSpec((S, D_ATTN), lambda b, c: (b, c)),
                  pl.BlockSpec((None, 1, D_ATTN), lambda b, c: (c, 0, 0)),
                  pl.BlockSpec((D_ATTN, D_ATTN), lambda b, c: (0, 0))],
        out_specs=pl.BlockSpec((None, N_PATTERNS, None, S, D_ATTN), lambda b, c: (c, 0, b, 0, 0)),
        out_shape=jax.ShapeDtypeStruct((3, N_PATTERNS, B, S, D_ATTN), BF16),
        scratch_shapes=[pltpu.VMEM((4, S, LANES), F32)],
        compiler_params=_params("arbitrary", "arbitrary"), name=name)(u, gains, emat)


def _band_mask(p, b):
    nblk = jnp.right_shift(16, 2 * p)
    has_prev = jnp.bitwise_and(b, nblk - 1) != 0
    qi = lax.broadcasted_iota(jnp.int32, (QBLK, 2 * QBLK), 0)
    ci = lax.broadcasted_iota(jnp.int32, (QBLK, 2 * QBLK), 1)
    dist = QBLK + qi - ci
    return (dist >= 0) & (dist <= QBLK) & (has_prev | (ci >= QBLK))


def _attn_specs(nb):
    blk = (None, None, None, QBLK, D_ATTN)
    q_spec = pl.BlockSpec(blk, lambda p, b: (0, p, b, 0, 0))
    kp_spec = pl.BlockSpec(blk, lambda p, b: (1, p, jnp.maximum(b - 1, 0), 0, 0))
    kc_spec = pl.BlockSpec(blk, lambda p, b: (1, p, b, 0, 0))
    vp_spec = pl.BlockSpec(blk, lambda p, b: (2, p, jnp.maximum(b - 1, 0), 0, 0))
    vc_spec = pl.BlockSpec(blk, lambda p, b: (2, p, b, 0, 0))
    return [q_spec, kp_spec, kc_spec, vp_spec, vc_spec]


def _attn_fwd(qkv, *, name):
    nb = qkv.shape[2]
    scale = HEAD_DIM ** -0.5

    def body(q_ref, kp_ref, kc_ref, vp_ref, vc_ref, o_ref, lse_ref):
        mask = _band_mask(pl.program_id(0), pl.program_id(1))
        q = q_ref[...]
        kk = jnp.concatenate([kp_ref[...], kc_ref[...]], axis=0)
        vv = jnp.concatenate([vp_ref[...], vc_ref[...]], axis=0)
        for h in range(HEADS):
            cols = slice(h * HEAD_DIM, (h + 1) * HEAD_DIM)
            s = _dot_nt(q[:, cols], kk[:, cols]) * scale
            s = jnp.where(mask, s, NEG)
            m = jnp.max(s, axis=-1, keepdims=True)
            e = jnp.exp(s - m)
            l = jnp.sum(e, axis=-1, keepdims=True)
            o = _dot(e.astype(BF16), vv[:, cols]) / l
            o_ref[:, cols] = o
            lse_ref[:, cols] = jnp.broadcast_to(m + jnp.log(l), (QBLK, HEAD_DIM))

    out = pl.BlockSpec((None, None, QBLK, D_ATTN), lambda p, b: (p, b, 0, 0))
    return pl.pallas_call(
        body, grid=(N_PATTERNS, nb), in_specs=_attn_specs(nb), out_specs=[out, out],
        out_shape=[jax.ShapeDtypeStruct((N_PATTERNS, nb, QBLK, D_ATTN), F32)] * 2,
        compiler_params=_params("arbitrary", "arbitrary"), name=name)(qkv, qkv, qkv, qkv, qkv)


def _attn_combine(o3, lse3, B, S, *, name):
    def body(o_ref, l_ref, a_ref, lt_ref, so_ref, sl_ref):
        for p in range(N_PATTERNS):
            _unpermute_in(lambda r0, n, p=p: o_ref[p, pl.ds(r0, n), :], so_ref.at[p], p, S)
            _unpermute_in(lambda r0, n, p=p: l_ref[p, pl.ds(r0, n), :], sl_ref.at[p], p, S)
        l0, l1, l2 = sl_ref[0], sl_ref[1], sl_ref[2]
        m = jnp.maximum(jnp.maximum(l0, l1), l2)
        w0, w1, w2 = jnp.exp(l0 - m), jnp.exp(l1 - m), jnp.exp(l2 - m)
        tot = w0 + w1 + w2
        a_ref[...] = (w0 * so_ref[0] + w1 * so_ref[1] + w2 * so_ref[2]) / tot
        lt_ref[...] = m + jnp.log(tot)

    o3 = o3.reshape(N_PATTERNS, B, S, D_ATTN)
    lse3 = lse3.reshape(N_PATTERNS, B, S, D_ATTN)
    inp = pl.BlockSpec((N_PATTERNS, None, S, LANES), lambda b, c: (0, b, 0, c))
    out = pl.BlockSpec((S, LANES), lambda b, c: (b, c))
    return pl.pallas_call(
        body, grid=(B, D_ATTN // LANES), in_specs=[inp, inp], out_specs=[out, out],
        out_shape=[jax.ShapeDtypeStruct((B * S, D_ATTN), F32)] * 2,
        scratch_shapes=[pltpu.VMEM((N_PATTERNS, S, LANES), F32)] * 2,
        compiler_params=_params("arbitrary", "arbitrary"), name=name)(o3, lse3)


def _attn_bwd_prep(dattn, attn, lse, B, S, *, name):
    emat = _seg_matrix(LANES)

    def body(da_ref, a_ref, l_ref, e_ref, do_ref, lo_ref, dd_ref, scr_ref):
        da = da_ref[...]
        dsum = _seg_mean(da * a_ref[...], e_ref, 1.0)
        for k, (val, out_ref, cast) in enumerate(((da, do_ref, BF16), (l_ref[...], lo_ref, F32), (dsum, dd_ref, F32))):
            scr_ref[...] = val
            for p, d in enumerate(DILATIONS):
                L = S // d
                if d == 1:
                    out_ref[p] = val.astype(cast)
                else:
                    for r in range(d):
                        out_ref[p, r * L:(r + 1) * L, :] = scr_ref[pl.ds(r, L, stride=d), :].astype(cast)

    inp = pl.BlockSpec((S, LANES), lambda b, c: (b, c))
    out = pl.BlockSpec((N_PATTERNS, None, S, LANES), lambda b, c: (0, b, 0, c))
    shp = (N_PATTERNS, B, S, D_ATTN)
    return pl.pallas_call(
        body, grid=(B, D_ATTN // LANES),
        in_specs=[inp, inp, inp, pl.BlockSpec((LANES, LANES), lambda b, c: (0, 0))],
        out_specs=[out, out, out],
        out_shape=[jax.ShapeDtypeStruct(shp, BF16), jax.ShapeDtypeStruct(shp, F32), jax.ShapeDtypeStruct(shp, F32)],
        scratch_shapes=[pltpu.VMEM((S, LANES), F32)],
        compiler_params=_params("arbitrary", "arbitrary"), name=name)(dattn, attn, lse, emat)


def _attn_bwd(qkv, do3, lse3, dd3, *, name):
    nb = qkv.shape[2]
    scale = HEAD_DIM ** -0.5

    def body(q_ref, kp_ref, kc_ref, vp_ref, vc_ref, do_ref, l_ref, d_ref, cur_ref, prev_ref):
        mask = _band_mask(pl.program_id(0), pl.program_id(1))
        q = q_ref[...]
        kk = jnp.concatenate([kp_ref[...], kc_ref[...]], axis=0)
        vv = jnp.concatenate([vp_ref[...], vc_ref[...]], axis=0)
        do = do_ref[...]
        for h in range(HEADS):
            cols = slice(h * HEAD_DIM, (h + 1) * HEAD_DIM)
            qh, kh, vh, doh = q[:, cols], kk[:, cols], vv[:, cols], do[:, cols]
            s = _dot_nt(qh, kh) * scale
            lse = l_ref[:, h * HEAD_DIM:h * HEAD_DIM + 1]
            dsum = d_ref[:, h * HEAD_DIM:h * HEAD_DIM + 1]
            pr = jnp.where(mask, jnp.exp(s - lse), 0.0)
            dp = _dot_nt(doh, vh)
            ds = (pr * (dp - dsum) * scale).astype(BF16)
            prb = pr.astype(BF16)
            cur_ref[0, :, cols] = _dot(ds, kh)
            dk = _dot_tn(ds, qh)
            dv = _dot_tn(prb, doh)
            prev_ref[0, :, cols] = dk[:QBLK]
            cur_ref[1, :, cols] = dk[QBLK:]
            prev_ref[1, :, cols] = dv[:QBLK]
            cur_ref[2, :, cols] = dv[QBLK:]

    aux = pl.BlockSpec((None, None, QBLK, D_ATTN), lambda p, b: (p, b, 0, 0))
    return pl.pallas_call(
        body, grid=(N_PATTERNS, nb), in_specs=_attn_specs(nb) + [aux, aux, aux],
        out_specs=[pl.BlockSpec((3, None, None, QBLK, D_ATTN), lambda p, b: (0, p, b, 0, 0)),
                   pl.BlockSpec((2, None, None, QBLK, D_ATTN), lambda p, b: (0, p, b, 0, 0))],
        out_shape=[jax.ShapeDtypeStruct((3, N_PATTERNS, nb, QBLK, D_ATTN), F32),
                   jax.ShapeDtypeStruct((2, N_PATTERNS, nb, QBLK, D_ATTN), F32)],
        compiler_params=_params("arbitrary", "arbitrary"), name=name)(qkv, qkv, qkv, qkv, qkv, do3, lse3, dd3)


def _attn_grad_combine(cur, prev, u, gains, B, S, *, name):
    emat = _seg_matrix(LANES)
    nblk = S // QBLK

    def body(cur_ref, prev_ref, u_ref, g_ref, e_ref, du_ref, dg_ref, scr_ref):
        c = pl.program_id(0)
        b = pl.program_id(2)
        use_prev = c > 0
        total = None
        for p, d in enumerate(DILATIONS):
            per_seq = nblk // d

            def get_block(r0, n, p=p, per_seq=per_seq):
                parts = []
                for blk in range(r0 // QBLK, (r0 + n) // QBLK):
                    v = cur_ref[p, pl.ds(blk * QBLK, QBLK), :]
                    if blk % per_seq != per_seq - 1:
                        nxt = prev_ref[p, pl.ds((blk + 1) * QBLK, QBLK), :]
                        v = v + jnp.where(use_prev, nxt, 0.0)
                    parts.append(v)
                return parts[0] if len(parts) == 1 else jnp.concatenate(parts, axis=0)

            _unpermute_in(get_block, scr_ref.at[p], p, S)
        dy = scr_ref[0] + scr_ref[1] + scr_ref[2]
        xv = u_ref[...]
        gain = g_ref[...]
        ms = _seg_mean(xv * xv, e_ref, HEAD_DIM)
        r = lax.rsqrt(ms + EPS)
        xhat = xv * r
        dxh = dy * gain
        dx = r * (dxh - xhat * _seg_mean(dxh * xhat, e_ref, HEAD_DIM))
        du_ref[...] = jnp.where(c < 2, dx, dy).astype(BF16)

        @pl.when((b == 0))
        def _():
            dg_ref[...] = jnp.zeros_like(dg_ref)

        dg_ref[...] += jnp.sum(dy * xhat, axis=0, keepdims=True)

    cur = cur.reshape(3, N_PATTERNS, B, S, D_ATTN)
    prev = prev.reshape(2, N_PATTERNS, B, S, D_ATTN)
    ncc = D_ATTN // LANES
    return pl.pallas_call(
        body, grid=(3, ncc, B),
        in_specs=[pl.BlockSpec((None, N_PATTERNS, None, S, LANES), lambda c, cc, b: (c, 0, b, 0, cc)),
                  pl.BlockSpec((None, N_PATTERNS, None, S, LANES), lambda c, cc, b: (jnp.maximum(c - 1, 0), 0, b, 0, cc)),
                  pl.BlockSpec((S, LANES), lambda c, cc, b: (b, c * ncc + cc)),
                  pl.BlockSpec((None, 1, LANES), lambda c, cc, b: (c, 0, cc)),
                  pl.BlockSpec((LANES, LANES), lambda c, cc, b: (0, 0))],
        out_specs=[pl.BlockSpec((S, LANES), lambda c, cc, b: (b, c * ncc + cc)),
                   pl.BlockSpec((None, 1, LANES), lambda c, cc, b: (c, 0, cc))],
        out_shape=[jax.ShapeDtypeStruct((B * S, 3 * D_ATTN), BF16), jax.ShapeDtypeStruct((3, 1, D_ATTN), F32)],
        scratch_shapes=[pltpu.VMEM((N_PATTERNS, S, LANES), F32)],
        compiler_params=_params("arbitrary", "arbitrary", "arbitrary"), name=name)(cur, prev, u, gains, emat)


HALO = 32
SUB = 64


def _conv_fwd(u, cw, cb, lg, lb, B, S, *, tc, name):
    nchunk = S // tc
    hb = tc // HALO

    def body(ca_ref, cap_ref, cg_ref, cgp_ref, w_ref, cb_ref, lg_ref, lb_ref, cv_ref, glu_ref, y_ref, pad_ref):
        i = pl.program_id(1)
        glu = ca_ref[...] * _sigmoid(cg_ref[...])
        glu_ref[...] = glu
        prev = cap_ref[...] * _sigmoid(cgp_ref[...])
        pad_ref[0:HALO, :] = jnp.where(i > 0, prev, 0.0)
        pad_ref[HALO:, :] = glu
        for sub in range(tc // SUB):
            acc = jnp.zeros((SUB, D_CONV), F32) + cb_ref[...]
            for k in range(CONV_K):
                acc = acc + pad_ref[pl.ds(sub * SUB + HALO - (CONV_K - 1) + k, SUB), :] * w_ref[pl.ds(k, 1), :]
            y_ref[sub * SUB:(sub + 1) * SUB, :] = acc
        y = y_ref[...]
        mu = jnp.mean(y, axis=-1, keepdims=True)
        yc = y - mu
        var = jnp.mean(yc * yc, axis=-1, keepdims=True)
        z = yc * lax.rsqrt(var + EPS) * lg_ref[...] + lb_ref[...]
        cv_ref[...] = (z * _sigmoid(z)).astype(BF16)

    def cur(col):
        return pl.BlockSpec((tc, D_CONV), lambda b, i: (b * nchunk + i, col))

    def halo(col):
        return pl.BlockSpec((HALO, D_CONV), lambda b, i: (jnp.maximum((b * nchunk + i) * hb - 1, 0), col))

    vec = pl.BlockSpec((1, D_CONV), lambda b, i: (0, 0))
    out = pl.BlockSpec((tc, D_CONV), lambda b, i: (b * nchunk + i, 0))
    return pl.pallas_call(
        body, grid=(B, nchunk),
        in_specs=[cur(3), halo(3), cur(4), halo(4), pl.BlockSpec((CONV_K, D_CONV), lambda b, i: (0, 0)), vec, vec, vec],
        out_specs=[out, out, out],
        out_shape=[jax.ShapeDtypeStruct((B * S, D_CONV), BF16), jax.ShapeDtypeStruct((B * S, D_CONV), F32),
                   jax.ShapeDtypeStruct((B * S, D_CONV), F32)],
        scratch_shapes=[pltpu.VMEM((tc + HALO, D_CONV), F32)],
        compiler_params=_params("arbitrary", "arbitrary"), name=name)(u, u, u, u, cw, cb, lg, lb)


def _conv_bwd_norm(dcv, y, lg, lb, *, tc, name):
    T = y.shape[0]

    def body(dcv_ref, y_ref, lg_ref, lb_ref, dy_ref, part_ref):
        yv = y_ref[...]
        mu = jnp.mean(yv, axis=-1, keepdims=True)
        yc = yv - mu
        var = jnp.mean(yc * yc, axis=-1, keepdims=True)
        rstd = lax.rsqrt(var + EPS)
        xhat = yc * rstd
        z = xhat * lg_ref[...] + lb_ref[...]
        sig = _sigmoid(z)
        dz = dcv_ref[...] * (sig * (1.0 + z * (1.0 - sig)))
        dxh = dz * lg_ref[...]
        dy = rstd * (dxh - jnp.mean(dxh, axis=-1, keepdims=True)
                     - xhat * jnp.mean(dxh * xhat, axis=-1, keepdims=True))
        dy_ref[...] = dy

        @pl.when(pl.program_id(0) == 0)
        def _():
            part_ref[...] = jnp.zeros_like(part_ref)

        part_ref[0:1, :] += jnp.sum(dz * xhat, axis=0, keepdims=True)
        part_ref[1:2, :] += jnp.sum(dz, axis=0, keepdims=True)
        part_ref[2:3, :] += jnp.sum(dy, axis=0, keepdims=True)

    tok = pl.BlockSpec((tc, D_CONV), lambda i: (i, 0))
    vec = pl.BlockSpec((1, D_CONV), lambda i: (0, 0))
    return pl.pallas_call(
        body, grid=(T // tc,), in_specs=[tok, tok, vec, vec],
        out_specs=[tok, pl.BlockSpec((8, D_CONV), lambda i: (0, 0))],
        out_shape=[jax.ShapeDtypeStruct((T, D_CONV), F32), jax.ShapeDtypeStruct((8, D_CONV), F32)],
        compiler_params=_params("arbitrary"), name=name)(dcv, y, lg, lb)


def _conv_bwd_taps(dy, glu, u, cw, B, S, *, tc, name):
    nchunk = S // tc
    hb = tc // HALO
    last_hb = B * S // HALO - 1

    def body(dy_ref, dyn_ref, glu_ref, glup_ref, ca_ref, cg_ref, w_ref, dca_ref, dcg_ref, dw_ref, dyp_ref, glp_ref, acc_ref):
        b = pl.program_id(0)
        i = pl.program_id(1)
        dy = dy_ref[...]
        dyp_ref[0:tc, :] = dy
        dyp_ref[tc:, :] = jnp.where(i < nchunk - 1, dyn_ref[...], 0.0)
        glp_ref[0:HALO, :] = jnp.where(i > 0, glup_ref[...], 0.0)
        glp_ref[HALO:, :] = glu_ref[...]

        @pl.when((b == 0) & (i == 0))
        def _():
            dw_ref[...] = jnp.zeros_like(dw_ref)

        for sub in range(tc // SUB):
            acc = jnp.zeros((SUB, D_CONV), F32)
            for k in range(CONV_K):
                acc = acc + dyp_ref[pl.ds(sub * SUB + (CONV_K - 1) - k, SUB), :] * w_ref[pl.ds(k, 1), :]
            acc_ref[sub * SUB:(sub + 1) * SUB, :] = acc
        for k in range(CONV_K):
            dw_ref[k:k + 1, :] += jnp.sum(dy * glp_ref[pl.ds(HALO - (CONV_K - 1) + k, tc), :], axis=0, keepdims=True)
        dglu = acc_ref[...]
        ca = ca_ref[...]
        sig = _sigmoid(cg_ref[...])
        dca_ref[...] = (dglu * sig).astype(BF16)
        dcg_ref[...] = (dglu * ca * sig * (1.0 - sig)).astype(BF16)

    tok = pl.BlockSpec((tc, D_CONV), lambda b, i: (b * nchunk + i, 0))
    nxt = pl.BlockSpec((HALO, D_CONV), lambda b, i: (jnp.minimum((b * nchunk + i + 1) * hb, last_hb), 0))
    prv = pl.BlockSpec((HALO, D_CONV), lambda b, i: (jnp.maximum((b * nchunk + i) * hb - 1, 0), 0))
    return pl.pallas_call(
        body, grid=(B, nchunk),
        in_specs=[tok, nxt, tok, prv,
                  pl.BlockSpec((tc, D_CONV), lambda b, i: (b * nchunk + i, 3)),
                  pl.BlockSpec((tc, D_CONV), lambda b, i: (b * nchunk + i, 4)),
                  pl.BlockSpec((CONV_K, D_CONV), lambda b, i: (0, 0))],
        out_specs=[tok, tok, pl.BlockSpec((32, D_CONV), lambda b, i: (0, 0))],
        out_shape=[jax.ShapeDtypeStruct((B * S, D_CONV), BF16), jax.ShapeDtypeStruct((B * S, D_CONV), BF16),
                   jax.ShapeDtypeStruct((32, D_CONV), F32)],
        scratch_shapes=[pltpu.VMEM((tc + HALO, D_CONV), F32), pltpu.VMEM((tc + HALO, D_CONV), F32),
                        pltpu.VMEM((tc, D_CONV), F32)],
        compiler_params=_params("arbitrary", "arbitrary"), name=name)(dy, dy, glu, glu, u, u, cw)


def _outproj_fwd(h, attn, cv, wout, *, tm, name):
    T, D = h.shape

    def body(h_ref, a_ref, c_ref, w_ref, o_ref):
        o_ref[...] = (h_ref[...] + _dot(a_ref[...].astype(BF16), w_ref[0:D_ATTN, :])
                      + _dot(c_ref[...], w_ref[D_ATTN:, :]))

    tok = pl.BlockSpec((tm, D), lambda i: (i, 0))
    half = pl.BlockSpec((tm, D_ATTN), lambda i: (i, 0))
    return pl.pallas_call(
        body, grid=(T // tm,), in_specs=[tok, half, half, pl.BlockSpec(wout.shape, lambda i: (0, 0))],
        out_specs=tok, out_shape=jax.ShapeDtypeStruct((T, D), F32),
        compiler_params=_params("arbitrary"), name=name)(h, attn, cv, wout)


def _outproj_bwd(dh, attn, cv, wout, *, tm, name):
    T, D = dh.shape

    def body(dh_ref, a_ref, c_ref, w_ref, da_ref, dc_ref, dw_ref):
        @pl.when(pl.program_id(0) == 0)
        def _():
            dw_ref[...] = jnp.zeros_like(dw_ref)

        dhb = dh_ref[...].astype(BF16)
        da_ref[...] = _dot_nt(dhb, w_ref[0:D_ATTN, :])
        dc_ref[...] = _dot_nt(dhb, w_ref[D_ATTN:, :])
        dw_ref[0:D_ATTN, :] += _dot_tn(a_ref[...].astype(BF16), dhb)
        dw_ref[D_ATTN:, :] += _dot_tn(c_ref[...], dhb)

    tok = pl.BlockSpec((tm, D), lambda i: (i, 0))
    half = pl.BlockSpec((tm, D_ATTN), lambda i: (i, 0))
    wspec = pl.BlockSpec(wout.shape, lambda i: (0, 0))
    return pl.pallas_call(
        body, grid=(T // tm,), in_specs=[tok, half, half, wspec], out_specs=[half, half, wspec],
        out_shape=[jax.ShapeDtypeStruct((T, D_ATTN), F32), jax.ShapeDtypeStruct((T, D_ATTN), F32),
                   jax.ShapeDtypeStruct(wout.shape, F32)],
        compiler_params=_params("arbitrary"), name=name)(dh, attn, cv, wout)


def _adamw(w, g, m, v, *, name):
    R, C = w.shape
    tr = R
    for cand in (512, 256, 128, 64, 32, 16, 8):
        if R % cand == 0 and R > cand:
            tr = cand
            break
    c1 = 1.0 - ADAM_B1 ** ADAM_STEP
    c2 = 1.0 - ADAM_B2 ** ADAM_STEP

    def body(w_ref, g_ref, m_ref, v_ref, d_ref, nm_ref, nv_ref):
        gv = g_ref[...]
        nm = ADAM_B1 * m_ref[...] + (1.0 - ADAM_B1) * gv
        nv = ADAM_B2 * v_ref[...] + (1.0 - ADAM_B2) * (gv * gv)
        d_ref[...] = -ADAM_LR * ((nm / c1) / (jnp.sqrt(nv / c2) + ADAM_EPS) + ADAM_WD * w_ref[...])
        nm_ref[...] = nm
        nv_ref[...] = nv

    blk = pl.BlockSpec((tr, C), lambda i: (i, 0))
    return pl.pallas_call(
        body, grid=(R // tr,), in_specs=[blk] * 4, out_specs=[blk] * 3,
        out_shape=[jax.ShapeDtypeStruct((R, C), F32)] * 3,
        compiler_params=_params("arbitrary"), name=name)(w, g, m, v)


TM = 512
TK = 1024
TC = 256


def _local_step(x, tgt, w):
    B, S, D = x.shape
    T = B * S
    x2 = x.reshape(T, D)
    t2 = tgt.reshape(T, D)
    ones = jnp.ones((1, D_ATTN), F32)
    gains = jnp.stack([jnp.tile(w["q_norm"], (1, HEADS)), jnp.tile(w["k_norm"], (1, HEADS)), ones])

    h1, n1, G1, U1 = _ffn_fwd(x2, w["ffn1_norm"], w["wg1"], w["wu1"], w["wd1"], None, tm=TM, name="ffn1_fwd")
    u, n2 = _inproj_fwd(h1, w["mix_norm"], w["win"], tm=TM, name="inproj_fwd")
    qkv = _qkv_prep(u, gains, B, S, name="qkv_prep")
    qkv = qkv.reshape(3, N_PATTERNS, T // QBLK, QBLK, D_ATTN)
    o3, lse3 = _attn_fwd(qkv, name="attn_fwd")
    attn, lse = _attn_combine(o3, lse3, B, S, name="attn_combine")
    cv, glu, yconv = _conv_fwd(u, w["conv_w"], w["conv_b"], w["conv_ln_g"], w["conv_ln_b"], B, S, tc=TC, name="conv_fwd")
    h2 = _outproj_fwd(h1, attn, cv, w["wout"], tm=TM, name="outproj_fwd")
    dh3, n3, G2, U2, loss = _ffn_fwd(h2, w["ffn2_norm"], w["wg2"], w["wu2"], w["wd2"], t2, tm=TM, name="ffn2_fwd")

    g = {}
    dG, dU, A, dy, dh2, g["ffn2_norm"] = _ffn_bwd_act(dh3, h2, w["ffn2_norm"], G2, U2, w["wg2"], w["wu2"], w["wd2"],
                                                    tm=TM, name="ffn2_bwd_act")
    g["wg2"], g["wu2"], g["wd2"] = _ffn_bwd_w(n3, dy, dG, dU, A, tk=TK, name="ffn2_bwd_w")
    dattn, dcv, g["wout"] = _outproj_bwd(dh2, attn, cv, w["wout"], tm=TM, name="outproj_bwd")
    dyc, cpart = _conv_bwd_norm(dcv, yconv, w["conv_ln_g"], w["conv_ln_b"], tc=TC, name="conv_bwd_norm")
    dca, dcg, dcw = _conv_bwd_taps(dyc, glu, u, w["conv_w"], B, S, tc=TC, name="conv_bwd_taps")
    do3, lseb3, dd3 = _attn_bwd_prep(dattn, attn, lse, B, S, name="attn_bwd_prep")
    nb = T // QBLK
    shp = (N_PATTERNS, nb, QBLK, D_ATTN)
    cur, prev = _attn_bwd(qkv, do3.reshape(shp), lseb3.reshape(shp), dd3.reshape(shp), name="attn_bwd")
    du_qkv, dgains = _attn_grad_combine(cur, prev, u, gains, B, S, name="attn_grad_combine")
    du = jnp.concatenate([du_qkv, dca, dcg], axis=1)
    dh1, g["mix_norm"] = _inproj_bwd_act(du, dh2, h1, w["mix_norm"], w["win"], tm=TM, name="inproj_bwd_act")
    g["win"] = _inproj_bwd_w(n2, du, w["win"].shape[0], tk=TK, name="inproj_bwd_w")
    dG, dU, A, dy, dx, g["ffn1_norm"] = _ffn_bwd_act(dh1, x2, w["ffn1_norm"], G1, U1, w["wg1"], w["wu1"], w["wd1"],
                                                   tm=TM, name="ffn1_bwd_act")
    g["wg1"], g["wu1"], g["wd1"] = _ffn_bwd_w(n1, dy, dG, dU, A, tk=TK, name="ffn1_bwd_w")

    g["q_norm"] = dgains[0].reshape(HEADS, HEAD_DIM).sum(axis=0, keepdims=True)
    g["k_norm"] = dgains[1].reshape(HEADS, HEAD_DIM).sum(axis=0, keepdims=True)
    g["conv_ln_g"] = cpart[0:1]
    g["conv_ln_b"] = cpart[1:2]
    g["conv_b"] = cpart[2:3]
    g["conv_w"] = dcw[:CONV_K]
    return loss, dx.reshape(B, S, D), g


N_CHIPS = 4
N_DEV = 8
ANY = pl.BlockSpec(memory_space=pl.ANY)
VMEM_SPEC = pl.BlockSpec(memory_space=pltpu.VMEM)


def _place():
    x, y, c = lax.axis_index("x"), lax.axis_index("y"), lax.axis_index("c")
    chips = [(1 - x, y), (x, 1 - y), (1 - x, 1 - y)]
    return x, y, c, 2 * x + y, chips, [2 * px + py for px, py in chips]


def _remote(src, dst, send_sem, recv_sem, device):
    return pltpu.make_async_remote_copy(src_ref=src, dst_ref=dst, send_sem=send_sem, recv_sem=recv_sem,
                                        device_id=device, device_id_type=MESH)


def _gather_weights(shards, dtypes, *, name):
    n = len(shards)
    halves = [s.reshape(2, s.shape[0] // 2, s.shape[1]) for s in shards]

    def body(*refs):
        ins, outs, vms = refs[:n], refs[n:2 * n], refs[2 * n:3 * n]
        send_sems, recv_sems, loc_sems = refs[3 * n:]
        x, y, c, me, chips, cidx = _place()
        sibling = (x, y, 1 - c)
        for a in range(n):
            vms[a][...] = ins[a][...].astype(dtypes[a])
        local, first, passed = [], [], []
        for a in range(n):
            lc = pltpu.make_async_copy(vms[a], outs[a].at[me], loc_sems.at[a])
            lc.start()
            local.append(lc)
            for j, chip in enumerate(chips):
                cp = _remote(vms[a].at[c], outs[a].at[me, c], send_sems.at[6 * a + j], recv_sems.at[6 * a + j], (*chip, c))
                cp.start()
                first.append(cp)
        for a in range(n):
            for j, chip in enumerate(chips):
                land = outs[a].at[cidx[j], c]
                _remote(land, land, send_sems.at[6 * a + j], recv_sems.at[6 * a + j], (*chip, c)).wait_recv()
                fw = _remote(land, land, send_sems.at[6 * a + 3 + j], recv_sems.at[6 * a + 3 + j], sibling)
                fw.start()
                passed.append(fw)
        for a in range(n):
            for j in range(3):
                land = outs[a].at[cidx[j], 1 - c]
                _remote(land, land, send_sems.at[6 * a + 3 + j], recv_sems.at[6 * a + 3 + j], sibling).wait_recv()
        for cp in first + passed:
            cp.wait_send()
        for lc in local:
            lc.wait()

    outs = pl.pallas_call(
        body, in_specs=[VMEM_SPEC] * n, out_specs=[ANY] * n,
        out_shape=[jax.ShapeDtypeStruct((N_CHIPS,) + h.shape, dt) for h, dt in zip(halves, dtypes)],
        scratch_shapes=[pltpu.VMEM(h.shape, dt) for h, dt in zip(halves, dtypes)]
        + [pltpu.SemaphoreType.DMA((6 * n,)), pltpu.SemaphoreType.DMA((6 * n,)), pltpu.SemaphoreType.DMA((n,))],
        compiler_params=pltpu.CompilerParams(vmem_limit_bytes=VMEM_LIMIT), name=name)(*halves)
    return [o.reshape((N_CHIPS,) + s.shape) for o, s in zip(outs, shards)]


def _exchange_halves(grads, *, name):
    n = len(grads)

    def body(*refs):
        ins, outs = refs[:n], refs[n:2 * n]
        send_sems, recv_sems = refs[2 * n:]
        x, y, c, me, chips, cidx = _place()
        cps = []
        for a in range(n):
            src = ins[a].at[pl.ds(0, ins[a].shape[0]), 1 - c]
            cp = _remote(src, outs[a], send_sems.at[a], recv_sems.at[a], (x, y, 1 - c))
            cp.start()
            cps.append(cp)
        for cp in cps:
            cp.wait()

    return pl.pallas_call(
        body, in_specs=[ANY] * n, out_specs=[ANY] * n,
        out_shape=[jax.ShapeDtypeStruct((g.shape[0],) + g.shape[2:], F32) for g in grads],
        scratch_shapes=[pltpu.SemaphoreType.DMA((n,)), pltpu.SemaphoreType.DMA((n,))], name=name)(*grads)


def _row_block(rows):
    for cand in (256, 176, 128, 64, 32, 16, 8):
        if rows % cand == 0:
            return cand
    return rows


def _add_own_half(g, r, sel, *, name):
    ns, _, rh, cdim = g.shape
    tr = _row_block(rh)

    def body(s_ref, gk_ref, rk_ref, gs_ref, rs_ref, keep_ref, send_ref):
        keep_ref[...] = gk_ref[...] + rk_ref[...]
        send_ref[...] = (gs_ref[...] + rs_ref[...]).astype(BF16)

    def g_spec(off):
        return pl.BlockSpec((None, None, tr, cdim), lambda k, i, s: (s[1 + off + k], s[0], i, 0))

    def r_spec(off):
        return pl.BlockSpec((None, tr, cdim), lambda k, i, s: (s[1 + off + k], i, 0))

    out = pl.BlockSpec((None, tr, cdim), lambda k, i, s: (k, i, 0))
    return pl.pallas_call(
        body,
        grid_spec=pltpu.PrefetchScalarGridSpec(
            num_scalar_prefetch=1, grid=(2, rh // tr),
            in_specs=[g_spec(0), r_spec(0), g_spec(2), r_spec(2)], out_specs=[out, out]),
        out_shape=[jax.ShapeDtypeStruct((2, rh, cdim), F32), jax.ShapeDtypeStruct((2, rh, cdim), BF16)],
        compiler_params=_params("arbitrary", "arbitrary"), name=name)(sel, g, r, g, r)


def _swap(arrays, stage, *, name):
    n = len(arrays)

    def body(*refs):
        ins, outs = refs[:n], refs[n:2 * n]
        send_sems, recv_sems = refs[2 * n:]
        x, y, c = lax.axis_index("x"), lax.axis_index("y"), lax.axis_index("c")
        along_x = (1 - c) if stage == 1 else c
        peer = (x + along_x * (1 - 2 * x), y + (1 - along_x) * (1 - 2 * y), c)
        cps = []
        for a in range(n):
            cp = _remote(ins[a], outs[a], send_sems.at[a], recv_sems.at[a], peer)
            cp.start()
            cps.append(cp)
        for cp in cps:
            cp.wait()

    return pl.pallas_call(
        body, in_specs=[ANY] * n, out_specs=[ANY] * n,
        out_shape=[jax.ShapeDtypeStruct(v.shape, v.dtype) for v in arrays],
        scratch_shapes=[pltpu.SemaphoreType.DMA((n,)), pltpu.SemaphoreType.DMA((n,))], name=name)(*arrays)


def _add_stage1(keep, got, *, name):
    _, rh, cdim = keep.shape
    tr = _row_block(rh)

    def body(k_ref, g_ref, keep_ref, send_ref):
        keep_ref[...] = k_ref[0] + g_ref[0].astype(F32)
        send_ref[...] = (k_ref[1] + g_ref[1].astype(F32)).astype(BF16)

    blk2 = pl.BlockSpec((2, tr, cdim), lambda i: (0, i, 0))
    blk = pl.BlockSpec((tr, cdim), lambda i: (i, 0))
    return pl.pallas_call(
        body, grid=(rh // tr,), in_specs=[blk2, blk2], out_specs=[blk, blk],
        out_shape=[jax.ShapeDtypeStruct((rh, cdim), F32), jax.ShapeDtypeStruct((rh, cdim), BF16)],
        compiler_params=_params("arbitrary"), name=name)(keep, got)


def _add_stage2(keep, got, *, name):
    rh, cdim = keep.shape
    tr = _row_block(rh)

    def body(k_ref, g_ref, o_ref):
        o_ref[...] = k_ref[...] + g_ref[...].astype(F32)

    blk = pl.BlockSpec((tr, cdim), lambda i: (i, 0))
    return pl.pallas_call(
        body, grid=(rh // tr,), in_specs=[blk, blk], out_specs=blk,
        out_shape=jax.ShapeDtypeStruct((rh, cdim), F32),
        compiler_params=_params("arbitrary"), name=name)(keep, got)


def _join_halves(halves, *, name):
    n = len(halves)

    def body(*refs):
        ins, outs = refs[:n], refs[n:2 * n]
        send_sems, recv_sems, loc_sems = refs[2 * n:]
        x, y, c, me, chips, cidx = _place()
        cps, lcs = [], []
        for a in range(n):
            lc = pltpu.make_async_copy(ins[a], outs[a].at[c], loc_sems.at[a])
            lc.start()
            lcs.append(lc)
            cp = _remote(ins[a], outs[a].at[c], send_sems.at[a], recv_sems.at[a], (x, y, 1 - c))
            cp.start()
            cps.append(cp)
        for a in range(n):
            land = outs[a].at[1 - c]
            _remote(land, land, send_sems.at[a], recv_sems.at[a], (x, y, 1 - c)).wait_recv()
        for cp in cps:
            cp.wait_send()
        for lc in lcs:
            lc.wait()

    return pl.pallas_call(
        body, in_specs=[ANY] * n, out_specs=[ANY] * n,
        out_shape=[jax.ShapeDtypeStruct((2,) + h.shape, F32) for h in halves],
        scratch_shapes=[pltpu.SemaphoreType.DMA((n,)), pltpu.SemaphoreType.DMA((n,)), pltpu.SemaphoreType.DMA((n,))],
        name=name)(*halves)


def _slot_order():
    x, y, c = lax.axis_index("x"), lax.axis_index("y"), lax.axis_index("c")
    own, flip_x, flip_y, both = 2 * x + y, 2 * (1 - x) + y, 2 * x + 1 - y, 2 * (1 - x) + 1 - y
    first = jnp.where(c == 0, flip_x, flip_y)
    second = jnp.where(c == 0, flip_y, flip_x)
    return jnp.stack([c, own, second, first, both]).astype(jnp.int32)


def _reduce_scatter(grads):
    sel = _slot_order()
    views = [g.reshape(N_CHIPS, 2, g.shape[1] // 2, g.shape[2]) for g in grads]
    got = _exchange_halves(views, name="rs_exchange_halves")
    keep, send = zip(*[_add_own_half(v, r, sel, name=f"rs_add_half_{a}") for a, (v, r) in enumerate(zip(views, got))])
    got = _swap(list(send), 1, name="rs_swap_first_axis")
    keep, send = zip(*[_add_stage1(k, r, name=f"rs_add_first_{a}") for a, (k, r) in enumerate(zip(keep, got))])
    got = _swap(list(send), 2, name="rs_swap_second_axis")
    halves = [_add_stage2(k, r, name=f"rs_add_second_{a}") for a, (k, r) in enumerate(zip(keep, got))]
    full = _join_halves(halves, name="rs_join_halves")
    return [f.reshape(g.shape[1], g.shape[2]) for f, g in zip(full, grads)]


def _allreduce_small(pack, *, name):
    rows = pack.shape[0]

    def body(p_ref, o_ref, buf_ref, send_sems, recv_sems):
        x, y, c = lax.axis_index("x"), lax.axis_index("y"), lax.axis_index("c")
        me = 4 * x + 2 * y + c
        buf_ref[me] = p_ref[...]
        cps = []
        for k in range(1, N_DEV):
            peer = tuple(1 - v if (k >> s) & 1 else v for v, s in ((x, 2), (y, 1), (c, 0)))
            cp = _remote(p_ref, buf_ref.at[me], send_sems.at[k - 1], recv_sems.at[k - 1], peer)
            cp.start()
            cps.append(cp)
        for k in range(1, N_DEV):
            src = 4 * (x ^ ((k >> 2) & 1)) + 2 * (y ^ ((k >> 1) & 1)) + (c ^ (k & 1))
            land = buf_ref.at[src]
            _remote(land, land, send_sems.at[k - 1], recv_sems.at[k - 1], (x, y, c)).wait_recv()
        acc = buf_ref[0]
        for d in range(1, N_DEV):
            acc = acc + buf_ref[d]
        o_ref[...] = acc
        for cp in cps:
            cp.wait_send()

    return pl.pallas_call(
        body, in_specs=[VMEM_SPEC], out_specs=VMEM_SPEC, out_shape=jax.ShapeDtypeStruct(pack.shape, F32),
        scratch_shapes=[pltpu.VMEM((N_DEV, rows, LANES), F32), pltpu.SemaphoreType.DMA((N_DEV - 1,)),
                        pltpu.SemaphoreType.DMA((N_DEV - 1,))], name=name)(pack)


SMALL = ("ffn1_norm", "mix_norm", "q_norm", "k_norm", "conv_b", "conv_ln_g", "conv_ln_b", "ffn2_norm", "conv_w")
BIG = ("ffn1_w_gate", "ffn1_w_up", "ffn1_w_down", "w_in", "w_out", "ffn2_w_gate", "ffn2_w_up", "ffn2_w_down")
WEIGHTS = ("ffn1_norm", "ffn1_w_gate", "ffn1_w_up", "ffn1_w_down", "mix_norm", "w_in", "q_norm", "k_norm",
           "conv_w", "conv_b", "conv_ln_g", "conv_ln_b", "w_out", "ffn2_norm", "ffn2_w_gate", "ffn2_w_up",
           "ffn2_w_down")
SUBLANES = 8


def _pack(parts):
    rows = []
    for p in parts:
        flat = p.reshape(-1)
        tile = SUBLANES * LANES
        padded = -(-flat.shape[0] // tile) * tile
        rows.append(jnp.pad(flat, (0, padded - flat.shape[0])).reshape(-1, LANES))
    return jnp.concatenate(rows, axis=0)


def _unpack(pack, shapes):
    out, row = [], 0
    for shp in shapes:
        size = shp[0] * shp[1]
        tile = SUBLANES * LANES
        nrows = -(-size // tile) * SUBLANES
        out.append(pack[row:row + nrows].reshape(-1)[:size].reshape(shp))
        row += nrows
    return out


def kernel(x, ffn1_norm, ffn1_w_gate, ffn1_w_up, ffn1_w_down, mix_norm, w_in, q_norm, k_norm, conv_w, conv_b, conv_ln_g, conv_ln_b, w_out, ffn2_norm, ffn2_w_gate, ffn2_w_up, ffn2_w_down, loss_target, m_ffn1_norm, m_ffn1_w_gate, m_ffn1_w_up, m_ffn1_w_down, m_mix_norm, m_w_in, m_q_norm, m_k_norm, m_conv_w, m_conv_b, m_conv_ln_g, m_conv_ln_b, m_w_out, m_ffn2_norm, m_ffn2_w_gate, m_ffn2_w_up, m_ffn2_w_down, v_ffn1_norm, v_ffn1_w_gate, v_ffn1_w_up, v_ffn1_w_down, v_mix_norm, v_w_in, v_q_norm, v_k_norm, v_conv_w, v_conv_b, v_conv_ln_g, v_conv_ln_b, v_w_out, v_ffn2_norm, v_ffn2_w_gate, v_ffn2_w_up, v_ffn2_w_down):
    wts = dict(ffn1_norm=ffn1_norm, ffn1_w_gate=ffn1_w_gate[0], ffn1_w_up=ffn1_w_up[0], ffn1_w_down=ffn1_w_down[0],
               mix_norm=mix_norm, w_in=w_in[0], q_norm=q_norm, k_norm=k_norm, conv_w=conv_w[0], conv_b=conv_b,
               conv_ln_g=conv_ln_g, conv_ln_b=conv_ln_b, w_out=w_out[0], ffn2_norm=ffn2_norm,
               ffn2_w_gate=ffn2_w_gate[0], ffn2_w_up=ffn2_w_up[0], ffn2_w_down=ffn2_w_down[0])
    mom = dict(ffn1_norm=m_ffn1_norm, ffn1_w_gate=m_ffn1_w_gate[0], ffn1_w_up=m_ffn1_w_up[0], ffn1_w_down=m_ffn1_w_down[0],
               mix_norm=m_mix_norm, w_in=m_w_in[0], q_norm=m_q_norm, k_norm=m_k_norm, conv_w=m_conv_w[0], conv_b=m_conv_b,
               conv_ln_g=m_conv_ln_g, conv_ln_b=m_conv_ln_b, w_out=m_w_out[0], ffn2_norm=m_ffn2_norm,
               ffn2_w_gate=m_ffn2_w_gate[0], ffn2_w_up=m_ffn2_w_up[0], ffn2_w_down=m_ffn2_w_down[0])
    var = dict(ffn1_norm=v_ffn1_norm, ffn1_w_gate=v_ffn1_w_gate[0], ffn1_w_up=v_ffn1_w_up[0], ffn1_w_down=v_ffn1_w_down[0],
               mix_norm=v_mix_norm, w_in=v_w_in[0], q_norm=v_q_norm, k_norm=v_k_norm, conv_w=v_conv_w[0], conv_b=v_conv_b,
               conv_ln_g=v_conv_ln_g, conv_ln_b=v_conv_ln_b, w_out=v_w_out[0], ffn2_norm=v_ffn2_norm,
               ffn2_w_gate=v_ffn2_w_gate[0], ffn2_w_up=v_ffn2_w_up[0], ffn2_w_down=v_ffn2_w_down[0])
    chip = 2 * lax.axis_index("x") + lax.axis_index("y")

    wg1, wu1, wd1 = _gather_weights([wts["ffn1_w_gate"], wts["ffn1_w_up"], wts["ffn1_w_down"]], [BF16] * 3,
                                    name="gather_ffn1")
    taps = jnp.pad(wts["conv_w"], ((0, 1), (0, 0)))
    win, wout, taps4 = _gather_weights([wts["w_in"], wts["w_out"], taps], [BF16, BF16, F32], name="gather_mix")
    wg2, wu2, wd2 = _gather_weights([wts["ffn2_w_gate"], wts["ffn2_w_up"], wts["ffn2_w_down"]], [BF16] * 3,
                                    name="gather_ffn2")
    conv_full = taps4.transpose(1, 0, 2).reshape(CONV_K + 1, D_CONV)[:CONV_K]
    w = dict(ffn1_norm=ffn1_norm, mix_norm=mix_norm, ffn2_norm=ffn2_norm, q_norm=q_norm, k_norm=k_norm,
             conv_w=conv_full, conv_b=conv_b, conv_ln_g=conv_ln_g, conv_ln_b=conv_ln_b,
             wg1=wg1, wu1=wu1, wd1=wd1, wg2=wg2, wu2=wu2, wd2=wd2, win=win,
             wout=wout.reshape(N_CHIPS * wout.shape[1], wout.shape[2]))

    loss_part, grad_x, g = _local_step(x, loss_target, w)

    big_parts = [g["wg1"], g["wu1"], g["wd1"], g["win"], g["wout"].reshape(wout.shape), g["wg2"], g["wu2"], g["wd2"]]
    big_grads = dict(zip(BIG, _reduce_scatter(big_parts)))

    small_shapes = [g[n].shape for n in SMALL] + [(SUBLANES, LANES)]
    red = _allreduce_small(_pack([g[n] for n in SMALL] + [loss_part]), name="allreduce_small")
    small = dict(zip(SMALL + ("loss",), _unpack(red, small_shapes)))
    loss = small["loss"][0, 0]
    small["conv_w"] = lax.dynamic_slice_in_dim(small["conv_w"], chip * LANES, LANES, axis=1)

    grads, delta, new_m, new_v = {}, {}, {}, {}
    for n in BIG:
        grads[n] = big_grads[n]
        delta[n], new_m[n], new_v[n] = _adamw(wts[n], grads[n], mom[n], var[n], name=f"adamw_{n}")
    shapes = [wts[n].shape for n in SMALL]
    packs = [_pack([src[n] for n in SMALL]) for src in (wts, small, mom, var)]
    outs = _adamw(*packs, name="adamw_small")
    for dst, pk in zip((delta, new_m, new_v), outs):
        dst.update(zip(SMALL, _unpack(pk, shapes)))
    for n in SMALL:
        grads[n] = small[n]

    def shaped(d, n):
        return d[n].reshape((1,) + d[n].shape) if n in BIG or n == "conv_w" else d[n]

    return (loss, grad_x, *[shaped(grads, n) for n in WEIGHTS], *[shaped(delta, n) for n in WEIGHTS],
            *[shaped(new_m, n) for n in WEIGHTS], *[shaped(new_v, n) for n in WEIGHTS])
```

```python
import functools

import jax
import jax.numpy as jnp
from jax import lax
from jax.experimental import pallas as pl
from jax.experimental.pallas import tpu as pltpu

F32 = jnp.float32
BF16 = jnp.bfloat16

EPS = 1e-6
HEADS = 8
HEAD_DIM = 64
D_ATTN = HEADS * HEAD_DIM
D_CONV = 512
CONV_K = 31
QBLK = 128
N_PATTERNS = 3
DILATIONS = (1, 4, 16)
LANES = 128
NEG = -1e30

ADAM_LR = 0.001
ADAM_B1 = 0.9
ADAM_B2 = 0.999
ADAM_EPS = 1e-08
ADAM_WD = 0.01
ADAM_STEP = 10

VMEM_LIMIT = 56 * 1024 * 1024
MESH = pl.DeviceIdType.MESH

NT_DIMS = (((1,), (1,)), ((), ()))
TN_DIMS = (((0,), (0,)), ((), ()))


def _params(*sem):
    return pltpu.CompilerParams(dimension_semantics=sem, vmem_limit_bytes=VMEM_LIMIT)


def _dot(a, b):
    return jnp.dot(a, b, preferred_element_type=F32)


def _dot_nt(a, b):
    return lax.dot_general(a, b, NT_DIMS, preferred_element_type=F32)


def _dot_tn(a, b):
    return lax.dot_general(a, b, TN_DIMS, preferred_element_type=F32)


def _sigmoid(x):
    return 1.0 / (1.0 + jnp.exp(-x))


def _seg_mean(v, e_ref, width):
    hi = v.astype(BF16)
    lo = (v - hi.astype(F32)).astype(BF16)
    e = e_ref[...]
    return (_dot(hi, e) + _dot(lo, e)) * (1.0 / width)


def _seg_matrix(n):
    i = jnp.arange(n)
    return (i[:, None] // HEAD_DIM == i[None, :] // HEAD_DIM).astype(BF16)


def _ffn_fwd(x, gain, wg, wu, wd, tgt, *, tm, name):
    T, D = x.shape
    NS, Fs, _ = wg.shape
    with_loss = tgt is not None

    def body(*refs):
        if with_loss:
            x_ref, g_ref, wg_ref, wu_ref, wd_ref, t_ref, h_ref, n_ref, G_ref, U_ref, loss_ref, acc_ref = refs
        else:
            x_ref, g_ref, wg_ref, wu_ref, wd_ref, h_ref, n_ref, G_ref, U_ref, acc_ref = refs
        i = pl.program_id(0)
        j = pl.program_id(1)

        @pl.when(j == 0)
        def _():
            xv = x_ref[...]
            r = lax.rsqrt(jnp.mean(xv * xv, axis=-1, keepdims=True) + EPS)
            n_ref[...] = (xv * r * g_ref[...]).astype(BF16)
            acc_ref[...] = jnp.zeros_like(acc_ref)

        n = n_ref[...]
        G = _dot_nt(n, wg_ref[...])
        U = _dot_nt(n, wu_ref[...])
        G_ref[...] = G.astype(BF16)
        U_ref[...] = U.astype(BF16)
        A = (G * _sigmoid(G) * U).astype(BF16)
        acc_ref[...] += _dot(A, wd_ref[...])

        @pl.when(j == NS - 1)
        def _():
            h = x_ref[...] + 0.5 * acc_ref[...]
            if with_loss:
                e = h - t_ref[...]
                h_ref[...] = e * (1.0 / D)

                @pl.when(i == 0)
                def _():
                    loss_ref[...] = jnp.zeros_like(loss_ref)

                loss_ref[...] += jnp.sum(e * e) * (0.5 / D)
            else:
                h_ref[...] = h

    tok = pl.BlockSpec((tm, D), lambda i, j: (i, 0))
    in_specs = [tok, pl.BlockSpec((1, D), lambda i, j: (0, 0)),
                pl.BlockSpec((None, Fs, D), lambda i, j: (j, 0, 0)),
                pl.BlockSpec((None, Fs, D), lambda i, j: (j, 0, 0)),
                pl.BlockSpec((None, Fs, D), lambda i, j: (j, 0, 0))]
    args = [x, gain, wg, wu, wd]
    act = pl.BlockSpec((None, tm, Fs), lambda i, j: (j, i, 0))
    out_shape = [jax.ShapeDtypeStruct((T, D), F32), jax.ShapeDtypeStruct((T, D), BF16),
                 jax.ShapeDtypeStruct((NS, T, Fs), BF16), jax.ShapeDtypeStruct((NS, T, Fs), BF16)]
    out_specs = [tok, tok, act, act]
    if with_loss:
        in_specs.append(tok)
        args.append(tgt)
        out_shape.append(jax.ShapeDtypeStruct((8, LANES), F32))
        out_specs.append(pl.BlockSpec((8, LANES), lambda i, j: (0, 0)))
    return pl.pallas_call(
        body, grid=(T // tm, NS), in_specs=in_specs, out_specs=out_specs, out_shape=out_shape,
        scratch_shapes=[pltpu.VMEM((tm, D), F32)],
        compiler_params=_params("arbitrary", "arbitrary"), name=name)(*args)


def _rms_bwd(xv, gain, dn):
    r = lax.rsqrt(jnp.mean(xv * xv, axis=-1, keepdims=True) + EPS)
    xhat = xv * r
    dxh = dn * gain
    dx = r * (dxh - xhat * jnp.mean(dxh * xhat, axis=-1, keepdims=True))
    dg = jnp.sum(dn * xhat, axis=0, keepdims=True)
    return dx, dg


def _ffn_bwd_act(dh, x, gain, G, U, wg, wu, wd, *, tm, name):
    T, D = x.shape
    NS, Fs, _ = wg.shape

    def body(dh_ref, x_ref, g_ref, G_ref, U_ref, wg_ref, wu_ref, wd_ref,
             dG_ref, dU_ref, A_ref, dy_ref, dx_ref, dg_ref, acc_ref):
        i = pl.program_id(0)
        j = pl.program_id(1)

        @pl.when(j == 0)
        def _():
            dy_ref[...] = (0.5 * dh_ref[...]).astype(BF16)
            acc_ref[...] = jnp.zeros_like(acc_ref)

        @pl.when((i == 0) & (j == 0))
        def _():
            dg_ref[...] = jnp.zeros_like(dg_ref)

        Gv = G_ref[...].astype(F32)
        Uv = U_ref[...].astype(F32)
        sig = _sigmoid(Gv)
        s = Gv * sig
        dA = _dot_nt(dy_ref[...], wd_ref[...])
        dG = (dA * Uv * (sig * (1.0 + Gv * (1.0 - sig)))).astype(BF16)
        dU = (dA * s).astype(BF16)
        dG_ref[...] = dG
        dU_ref[...] = dU
        A_ref[...] = (s * Uv).astype(BF16)
        acc_ref[...] += _dot(dG, wg_ref[...]) + _dot(dU, wu_ref[...])

        @pl.when(j == NS - 1)
        def _():
            dx, dg = _rms_bwd(x_ref[...], g_ref[...], acc_ref[...])
            dx_ref[...] = dh_ref[...] + dx
            dg_ref[...] += dg

    tok = pl.BlockSpec((tm, D), lambda i, j: (i, 0))
    act = pl.BlockSpec((None, tm, Fs), lambda i, j: (j, i, 0))
    vec = pl.BlockSpec((1, D), lambda i, j: (0, 0))
    return pl.pallas_call(
        body, grid=(T // tm, NS),
        in_specs=[tok, tok, vec, act, act,
                  pl.BlockSpec((None, Fs, D), lambda i, j: (j, 0, 0)),
                  pl.BlockSpec((None, Fs, D), lambda i, j: (j, 0, 0)),
                  pl.BlockSpec((None, Fs, D), lambda i, j: (j, 0, 0))],
        out_specs=[act, act, act, tok, tok, vec],
        out_shape=[jax.ShapeDtypeStruct((NS, T, Fs), BF16)] * 3
        + [jax.ShapeDtypeStruct((T, D), BF16), jax.ShapeDtypeStruct((T, D), F32),
           jax.ShapeDtypeStruct((1, D), F32)],
        scratch_shapes=[pltpu.VMEM((tm, D), F32)],
        compiler_params=_params("arbitrary", "arbitrary"), name=name)(dh, x, gain, G, U, wg, wu, wd)


def _ffn_bwd_w(n, dy, dG, dU, A, *, tk, name):
    T, D = n.shape
    NS, _, Fs = dG.shape

    def body(n_ref, dy_ref, dG_ref, dU_ref, A_ref, wg_ref, wu_ref, wd_ref):
        @pl.when(pl.program_id(1) == 0)
        def _():
            wg_ref[...] = jnp.zeros_like(wg_ref)
            wu_ref[...] = jnp.zeros_like(wu_ref)
            wd_ref[...] = jnp.zeros_like(wd_ref)

        nv = n_ref[...]
        wg_ref[...] += _dot_tn(dG_ref[...], nv)
        wu_ref[...] += _dot_tn(dU_ref[...], nv)
        wd_ref[...] += _dot_tn(A_ref[...], dy_ref[...])

    tok = pl.BlockSpec((tk, D), lambda j, k: (k, 0))
    act = pl.BlockSpec((None, tk, Fs), lambda j, k: (j, k, 0))
    return pl.pallas_call(
        body, grid=(NS, T // tk), in_specs=[tok, tok, act, act, act],
        out_specs=[pl.BlockSpec((None, Fs, D), lambda j, k: (j, 0, 0))] * 3,
        out_shape=[jax.ShapeDtypeStruct((NS, Fs, D), F32)] * 3,
        compiler_params=_params("arbitrary", "arbitrary"), name=name)(n, dy, dG, dU, A)


def _inproj_fwd(h, gain, win, *, tm, name):
    T, D = h.shape
    NS, _, Cs = win.shape

    def body(h_ref, g_ref, w_ref, u_ref, n_ref):
        @pl.when(pl.program_id(1) == 0)
        def _():
            xv = h_ref[...]
            r = lax.rsqrt(jnp.mean(xv * xv, axis=-1, keepdims=True) + EPS)
            n_ref[...] = (xv * r * g_ref[...]).astype(BF16)

        u_ref[...] = _dot(n_ref[...], w_ref[...])

    tok = pl.BlockSpec((tm, D), lambda i, j: (i, 0))
    return pl.pallas_call(
        body, grid=(T // tm, NS),
        in_specs=[tok, pl.BlockSpec((1, D), lambda i, j: (0, 0)),
                  pl.BlockSpec((None, D, Cs), lambda i, j: (j, 0, 0))],
        out_specs=[pl.BlockSpec((tm, Cs), lambda i, j: (i, j)), tok],
        out_shape=[jax.ShapeDtypeStruct((T, NS * Cs), F32), jax.ShapeDtypeStruct((T, D), BF16)],
        compiler_params=_params("arbitrary", "arbitrary"), name=name)(h, gain, win)


def _inproj_bwd_act(du, dh, h, gain, win, *, tm, name):
    T, D = h.shape
    NS, _, Cs = win.shape

    def body(du_ref, dh_ref, h_ref, g_ref, w_ref, dx_ref, dg_ref, acc_ref):
        i = pl.program_id(0)
        j = pl.program_id(1)

        @pl.when(j == 0)
        def _():
            acc_ref[...] = jnp.zeros_like(acc_ref)

        @pl.when((i == 0) & (j == 0))
        def _():
            dg_ref[...] = jnp.zeros_like(dg_ref)

        acc_ref[...] += _dot_nt(du_ref[...], w_ref[...])

        @pl.when(j == NS - 1)
        def _():
            dx, dg = _rms_bwd(h_ref[...], g_ref[...], acc_ref[...])
            dx_ref[...] = dh_ref[...] + dx
            dg_ref[...] += dg

    tok = pl.BlockSpec((tm, D), lambda i, j: (i, 0))
    vec = pl.BlockSpec((1, D), lambda i, j: (0, 0))
    return pl.pallas_call(
        body, grid=(T // tm, NS),
        in_specs=[pl.BlockSpec((tm, Cs), lambda i, j: (i, j)), tok, tok, vec,
                  pl.BlockSpec((None, D, Cs), lambda i, j: (j, 0, 0))],
        out_specs=[tok, vec],
        out_shape=[jax.ShapeDtypeStruct((T, D), F32), jax.ShapeDtypeStruct((1, D), F32)],
        scratch_shapes=[pltpu.VMEM((tm, D), F32)],
        compiler_params=_params("arbitrary", "arbitrary"), name=name)(du, dh, h, gain, win)


def _inproj_bwd_w(n, du, ns, *, tk, name):
    T, D = n.shape
    Cs = du.shape[1] // ns

    def body(n_ref, du_ref, w_ref):
        @pl.when(pl.program_id(1) == 0)
        def _():
            w_ref[...] = jnp.zeros_like(w_ref)

        w_ref[...] += _dot_tn(n_ref[...], du_ref[...])

    return pl.pallas_call(
        body, grid=(ns, T // tk),
        in_specs=[pl.BlockSpec((tk, D), lambda j, k: (k, 0)), pl.BlockSpec((tk, Cs), lambda j, k: (k, j))],
        out_specs=pl.BlockSpec((None, D, Cs), lambda j, k: (j, 0, 0)),
        out_shape=jax.ShapeDtypeStruct((ns, D, Cs), F32),
        compiler_params=_params("arbitrary", "arbitrary"), name=name)(n, du)


def _permute_out(src_ref, out_ref, cast):
    S = src_ref.shape[1]
    for p, d in enumerate(DILATIONS):
        L = S // d
        for cc in range(4):
            cols = slice(cc * LANES, (cc + 1) * LANES)
            if d == 1:
                out_ref[p, :, cols] = src_ref[cc].astype(cast)
            else:
                for r in range(d):
                    out_ref[p, r * L:(r + 1) * L, cols] = src_ref[cc, pl.ds(r, L, stride=d), :].astype(cast)


def _unpermute_in(get_block, dst_ref, p, S):
    d = DILATIONS[p]
    L = S // d
    if d == 1:
        dst_ref[...] = get_block(0, S)
    else:
        for r in range(d):
            dst_ref[pl.ds(r, L, stride=d), :] = get_block(r * L, L)


def _qkv_prep(u, gains, B, S, *, name):
    emat = _seg_matrix(D_ATTN)

    def body(u_ref, g_ref, e_ref, out_ref, scr_ref):
        c = pl.program_id(1)
        xv = u_ref[...]
        ms = _seg_mean(xv * xv, e_ref, HEAD_DIM)
        r = jnp.where(c < 2, lax.rsqrt(ms + EPS), 1.0)
        yv = xv * r * g_ref[...]
        for cc in range(4):
            scr_ref[cc] = yv[:, cc * LANES:(cc + 1) * LANES]
        _permute_out(scr_ref, out_ref, BF16)

    return pl.pallas_call(
        body, grid=(B, 3),
        in_specs=[pl.BlockSpec((S, D_ATTN), lambda b, c: (b, c)),
                  pl.BlockSpec((None, 1, D_ATTN), lambda b, c: (c, 0, 0)),
                  pl.BlockSpec((D_ATTN, D_ATTN), lambda b, c: (0, 0))],
        out_specs=pl.BlockSpec((None, N_PATTERNS, None, S, D_ATTN), lambda b, c: (c, 0, b, 0, 0)),
        out_shape=jax.ShapeDtypeStruct((3, N_PATTERNS, B, S, D_ATTN), BF16),
        scratch_shapes=[pltpu.VMEM((4, S, LANES), F32)],
        compiler_params=_params("arbitrary", "arbitrary"), name=name)(u, gains, emat)


def _band_mask(p, b):
    nblk = jnp.right_shift(16, 2 * p)
    has_prev = jnp.bitwise_and(b, nblk - 1) != 0
    qi = lax.broadcasted_iota(jnp.int32, (QBLK, 2 * QBLK), 0)
    ci = lax.broadcasted_iota(jnp.int32, (QBLK, 2 * QBLK), 1)
    dist = QBLK + qi - ci
    return (dist >= 0) & (dist <= QBLK) & (has_prev | (ci >= QBLK))


def _attn_specs(nb):
    blk = (None, None, None, QBLK, D_ATTN)
    q_spec = pl.BlockSpec(blk, lambda p, b: (0, p, b, 0, 0))
    kp_spec = pl.BlockSpec(blk, lambda p, b: (1, p, jnp.maximum(b - 1, 0), 0, 0))
    kc_spec = pl.BlockSpec(blk, lambda p, b: (1, p, b, 0, 0))
    vp_spec = pl.BlockSpec(blk, lambda p, b: (2, p, jnp.maximum(b - 1, 0), 0, 0))
    vc_spec = pl.BlockSpec(blk, lambda p, b: (2, p, b, 0, 0))
    return [q_spec, kp_spec, kc_spec, vp_spec, vc_spec]


def _attn_fwd(qkv, *, name):
    nb = qkv.shape[2]

    def body(q_ref, kp_ref, kc_ref, vp_ref, vc_ref, o_ref, lse_ref):
        mask = _band_mask(pl.program_id(0), pl.program_id(1))
        q = q_ref[...]
        kk = jnp.concatenate([kp_ref[...], kc_ref[...]], axis=0)
        vv = jnp.concatenate([vp_ref[...], vc_ref[...]], axis=0)
        for h in range(HEADS):
            cols = slice(h * HEAD_DIM, (h + 1) * HEAD_DIM)
            s = _dot_nt(q[:, cols], kk[:, cols])
            s = jnp.where(mask, s, NEG)
            m = jnp.max(s, axis=-1, keepdims=True)
            e = jnp.exp(s - m)
            l = jnp.sum(e, axis=-1, keepdims=True)
            o = _dot(e.astype(BF16), vv[:, cols]) / l
            o_ref[:, cols] = o
            lse_ref[:, cols] = jnp.broadcast_to(m + jnp.log(l), (QBLK, HEAD_DIM))

    out = pl.BlockSpec((None, None, QBLK, D_ATTN), lambda p, b: (p, b, 0, 0))
    return pl.pallas_call(
        body, grid=(N_PATTERNS, nb), in_specs=_attn_specs(nb), out_specs=[out, out],
        out_shape=[jax.ShapeDtypeStruct((N_PATTERNS, nb, QBLK, D_ATTN), F32)] * 2,
        compiler_params=_params("arbitrary", "arbitrary"), name=name)(qkv, qkv, qkv, qkv, qkv)


def _attn_combine(o3, lse3, B, S, *, name):
    def body(o_ref, l_ref, a_ref, lt_ref, so_ref, sl_ref):
        for p in range(N_PATTERNS):
            _unpermute_in(lambda r0, n, p=p: o_ref[p, pl.ds(r0, n), :], so_ref.at[p], p, S)
            _unpermute_in(lambda r0, n, p=p: l_ref[p, pl.ds(r0, n), :], sl_ref.at[p], p, S)
        l0, l1, l2 = sl_ref[0], sl_ref[1], sl_ref[2]
        m = jnp.maximum(jnp.maximum(l0, l1), l2)
        w0, w1, w2 = jnp.exp(l0 - m), jnp.exp(l1 - m), jnp.exp(l2 - m)
        tot = w0 + w1 + w2
        a_ref[...] = (w0 * so_ref[0] + w1 * so_ref[1] + w2 * so_ref[2]) / tot
        lt_ref[...] = m + jnp.log(tot)

    o3 = o3.reshape(N_PATTERNS, B, S, D_ATTN)
    lse3 = lse3.reshape(N_PATTERNS, B, S, D_ATTN)
    inp = pl.BlockSpec((N_PATTERNS, None, S, LANES), lambda b, c: (0, b, 0, c))
    out = pl.BlockSpec((S, LANES), lambda b, c: (b, c))
    return pl.pallas_call(
        body, grid=(B, D_ATTN // LANES), in_specs=[inp, inp], out_specs=[out, out],
        out_shape=[jax.ShapeDtypeStruct((B * S, D_ATTN), F32)] * 2,
        scratch_shapes=[pltpu.VMEM((N_PATTERNS, S, LANES), F32)] * 2,
        compiler_params=_params("arbitrary", "arbitrary"), name=name)(o3, lse3)


def _attn_bwd_prep(dattn, attn, lse, B, S, *, name):
    emat = _seg_matrix(LANES)

    def body(da_ref, a_ref, l_ref, e_ref, do_ref, lo_ref, dd_ref, scr_ref):
        da = da_ref[...]
        dsum = _seg_mean(da * a_ref[...], e_ref, 1.0)
        for k, (val, out_ref, cast) in enumerate(((da, do_ref, BF16), (l_ref[...], lo_ref, F32), (dsum, dd_ref, F32))):
            scr_ref[...] = val
            for p, d in enumerate(DILATIONS):
                L = S // d
                if d == 1:
                    out_ref[p] = val.astype(cast)
                else:
                    for r in range(d):
                        out_ref[p, r * L:(r + 1) * L, :] = scr_ref[pl.ds(r, L, stride=d), :].astype(cast)

    inp = pl.BlockSpec((S, LANES), lambda b, c: (b, c))
    out = pl.BlockSpec((N_PATTERNS, None, S, LANES), lambda b, c: (0, b, 0, c))
    shp = (N_PATTERNS, B, S, D_ATTN)
    return pl.pallas_call(
        body, grid=(B, D_ATTN // LANES),
        in_specs=[inp, inp, inp, pl.BlockSpec((LANES, LANES), lambda b, c: (0, 0))],
        out_specs=[out, out, out],
        out_shape=[jax.ShapeDtypeStruct(shp, BF16), jax.ShapeDtypeStruct(shp, F32), jax.ShapeDtypeStruct(shp, F32)],
        scratch_shapes=[pltpu.VMEM((S, LANES), F32)],
        compiler_params=_params("arbitrary", "arbitrary"), name=name)(dattn, attn, lse, emat)


def _attn_bwd(qkv, do3, lse3, dd3, *, name):
    nb = qkv.shape[2]

    def body(q_ref, kp_ref, kc_ref, vp_ref, vc_ref, do_ref, l_ref, d_ref, cur_ref, prev_ref):
        mask = _band_mask(pl.program_id(0), pl.program_id(1))
        q = q_ref[...]
        kk = jnp.concatenate([kp_ref[...], kc_ref[...]], axis=0)
        vv = jnp.concatenate([vp_ref[...], vc_ref[...]], axis=0)
        do = do_ref[...]
        for h in range(HEADS):
            cols = slice(h * HEAD_DIM, (h + 1) * HEAD_DIM)
            qh, kh, vh, doh = q[:, cols], kk[:, cols], vv[:, cols], do[:, cols]
            s = _dot_nt(qh, kh)
            lse = l_ref[:, h * HEAD_DIM:h * HEAD_DIM + 1]
            dsum = d_ref[:, h * HEAD_DIM:h * HEAD_DIM + 1]
            pr = jnp.where(mask, jnp.exp(s - lse), 0.0)
            dp = _dot_nt(doh, vh)
            ds = (pr * (dp - dsum)).astype(BF16)
            prb = pr.astype(BF16)
            cur_ref[0, :, cols] = _dot(ds, kh)
            dk = _dot_tn(ds, qh)
            dv = _dot_tn(prb, doh)
            prev_ref[0, :, cols] = dk[:QBLK]
            cur_ref[1, :, cols] = dk[QBLK:]
            prev_ref[1, :, cols] = dv[:QBLK]
            cur_ref[2, :, cols] = dv[QBLK:]

    aux = pl.BlockSpec((None, None, QBLK, D_ATTN), lambda p, b: (p, b, 0, 0))
    return pl.pallas_call(
        body, grid=(N_PATTERNS, nb), in_specs=_attn_specs(nb) + [aux, aux, aux],
        out_specs=[pl.BlockSpec((3, None, None, QBLK, D_ATTN), lambda p, b: (0, p, b, 0, 0)),
                   pl.BlockSpec((2, None, None, QBLK, D_ATTN), lambda p, b: (0, p, b, 0, 0))],
        out_shape=[jax.ShapeDtypeStruct((3, N_PATTERNS, nb, QBLK, D_ATTN), F32),
                   jax.ShapeDtypeStruct((2, N_PATTERNS, nb, QBLK, D_ATTN), F32)],
        compiler_params=_params("arbitrary", "arbitrary"), name=name)(qkv, qkv, qkv, qkv, qkv, do3, lse3, dd3)


def _attn_grad_combine(cur, prev, u, gains, B, S, *, name):
    emat = _seg_matrix(LANES)
    nblk = S // QBLK

    def body(cur_ref, prev_ref, u_ref, g_ref, e_ref, du_ref, dg_ref, scr_ref):
        c = pl.program_id(0)
        b = pl.program_id(2)
        use_prev = c > 0
        total = None
        for p, d in enumerate(DILATIONS):
            per_seq = nblk // d

            def get_block(r0, n, p=p, per_seq=per_seq):
                parts = []
                for blk in range(r0 // QBLK, (r0 + n) // QBLK):
                    v = cur_ref[p, pl.ds(blk * QBLK, QBLK), :]
                    if blk % per_seq != per_seq - 1:
                        nxt = prev_ref[p, pl.ds((blk + 1) * QBLK, QBLK), :]
                        v = v + jnp.where(use_prev, nxt, 0.0)
                    parts.append(v)
                return parts[0] if len(parts) == 1 else jnp.concatenate(parts, axis=0)

            _unpermute_in(get_block, scr_ref.at[p], p, S)
        dy = scr_ref[0] + scr_ref[1] + scr_ref[2]
        xv = u_ref[...]
        gain = g_ref[...]
        ms = _seg_mean(xv * xv, e_ref, HEAD_DIM)
        r = lax.rsqrt(ms + EPS)
        xhat = xv * r
        dxh = dy * gain
        dx = r * (dxh - xhat * _seg_mean(dxh * xhat, e_ref, HEAD_DIM))
        du_ref[...] = jnp.where(c < 2, dx, dy).astype(BF16)

        @pl.when((b == 0))
        def _():
            dg_ref[...] = jnp.zeros_like(dg_ref)

        dg_ref[...] += jnp.sum(dy * xhat, axis=0, keepdims=True)

    cur = cur.reshape(3, N_PATTERNS, B, S, D_ATTN)
    prev = prev.reshape(2, N_PATTERNS, B, S, D_ATTN)
    ncc = D_ATTN // LANES
    return pl.pallas_call(
        body, grid=(3, ncc, B),
        in_specs=[pl.BlockSpec((None, N_PATTERNS, None, S, LANES), lambda c, cc, b: (c, 0, b, 0, cc)),
                  pl.BlockSpec((None, N_PATTERNS, None, S, LANES), lambda c, cc, b: (jnp.maximum(c - 1, 0), 0, b, 0, cc)),
                  pl.BlockSpec((S, LANES), lambda c, cc, b: (b, c * ncc + cc)),
                  pl.BlockSpec((None, 1, LANES), lambda c, cc, b: (c, 0, cc)),
                  pl.BlockSpec((LANES, LANES), lambda c, cc, b: (0, 0))],
        out_specs=[pl.BlockSpec((S, LANES), lambda c, cc, b: (b, c * ncc + cc)),
                   pl.BlockSpec((None, 1, LANES), lambda c, cc, b: (c, 0, cc))],
        out_shape=[jax.ShapeDtypeStruct((B * S, 3 * D_ATTN), BF16), jax.ShapeDtypeStruct((3, 1, D_ATTN), F32)],
        scratch_shapes=[pltpu.VMEM((N_PATTERNS, S, LANES), F32)],
        compiler_params=_params("arbitrary", "arbitrary", "arbitrary"), name=name)(cur, prev, u, gains, emat)


HALO = 32
SUB = 64


def _conv_fwd(u, cw, cb, lg, lb, B, S, *, tc, name):
    nchunk = S // tc
    hb = tc // HALO

    def body(ca_ref, cap_ref, cg_ref, cgp_ref, w_ref, cb_ref, lg_ref, lb_ref, cv_ref, glu_ref, y_ref, pad_ref):
        i = pl.program_id(1)
        glu = ca_ref[...] * _sigmoid(cg_ref[...])
        glu_ref[...] = glu
        prev = cap_ref[...] * _sigmoid(cgp_ref[...])
        pad_ref[0:HALO, :] = jnp.where(i > 0, prev, 0.0)
        pad_ref[HALO:, :] = glu
        for sub in range(tc // SUB):
            acc = jnp.zeros((SUB, D_CONV), F32) + cb_ref[...]
            for k in range(CONV_K):
                acc = acc + pad_ref[pl.ds(sub * SUB + HALO - (CONV_K - 1) + k, SUB), :] * w_ref[pl.ds(k, 1), :]
            y_ref[sub * SUB:(sub + 1) * SUB, :] = acc
        y = y_ref[...]
        mu = jnp.mean(y, axis=-1, keepdims=True)
        yc = y - mu
        var = jnp.mean(yc * yc, axis=-1, keepdims=True)
        z = yc * lax.rsqrt(var + EPS) * lg_ref[...] + lb_ref[...]
        cv_ref[...] = (z * _sigmoid(z)).astype(BF16)

    def cur(col):
        return pl.BlockSpec((tc, D_CONV), lambda b, i: (b * nchunk + i, col))

    def halo(col):
        return pl.BlockSpec((HALO, D_CONV), lambda b, i: (jnp.maximum((b * nchunk + i) * hb - 1, 0), col))

    vec = pl.BlockSpec((1, D_CONV), lambda b, i: (0, 0))
    out = pl.BlockSpec((tc, D_CONV), lambda b, i: (b * nchunk + i, 0))
    return pl.pallas_call(
        body, grid=(B, nchunk),
        in_specs=[cur(3), halo(3), cur(4), halo(4), pl.BlockSpec((CONV_K, D_CONV), lambda b, i: (0, 0)), vec, vec, vec],
        out_specs=[out, out, out],
        out_shape=[jax.ShapeDtypeStruct((B * S, D_CONV), BF16), jax.ShapeDtypeStruct((B * S, D_CONV), F32),
                   jax.ShapeDtypeStruct((B * S, D_CONV), F32)],
        scratch_shapes=[pltpu.VMEM((tc + HALO, D_CONV), F32)],
        compiler_params=_params("arbitrary", "arbitrary"), name=name)(u, u, u, u, cw, cb, lg, lb)


def _conv_bwd_norm(dcv, y, lg, lb, *, tc, name):
    T = y.shape[0]

    def body(dcv_ref, y_ref, lg_ref, lb_ref, dy_ref, part_ref):
        yv = y_ref[...]
        mu = jnp.mean(yv, axis=-1, keepdims=True)
        yc = yv - mu
        var = jnp.mean(yc * yc, axis=-1, keepdims=True)
        rstd = lax.rsqrt(var + EPS)
        xhat = yc * rstd
        z = xhat * lg_ref[...] + lb_ref[...]
        sig = _sigmoid(z)
        dz = dcv_ref[...] * (sig * (1.0 + z * (1.0 - sig)))
        dxh = dz * lg_ref[...]
        dy = rstd * (dxh - jnp.mean(dxh, axis=-1, keepdims=True)
                     - xhat * jnp.mean(dxh * xhat, axis=-1, keepdims=True))
        dy_ref[...] = dy

        @pl.when(pl.program_id(0) == 0)
        def _():
            part_ref[...] = jnp.zeros_like(part_ref)

        part_ref[0:1, :] += jnp.sum(dz * xhat, axis=0, keepdims=True)
        part_ref[1:2, :] += jnp.sum(dz, axis=0, keepdims=True)
        part_ref[2:3, :] += jnp.sum(dy, axis=0, keepdims=True)

    tok = pl.BlockSpec((tc, D_CONV), lambda i: (i, 0))
    vec = pl.BlockSpec((1, D_CONV), lambda i: (0, 0))
    return pl.pallas_call(
        body, grid=(T // tc,), in_specs=[tok, tok, vec, vec],
        out_specs=[tok, pl.BlockSpec((8, D_CONV), lambda i: (0, 0))],
        out_shape=[jax.ShapeDtypeStruct((T, D_CONV), F32), jax.ShapeDtypeStruct((8, D_CONV), F32)],
        compiler_params=_params("arbitrary"), name=name)(dcv, y, lg, lb)


def _conv_bwd_taps(dy, glu, u, cw, B, S, *, tc, name):
    nchunk = S // tc
    hb = tc // HALO
    last_hb = B * S // HALO - 1

    def body(dy_ref, dyn_ref, glu_ref, glup_ref, ca_ref, cg_ref, w_ref, dca_ref, dcg_ref, dw_ref, dyp_ref, glp_ref, acc_ref):
        b = pl.program_id(0)
        i = pl.program_id(1)
        dy = dy_ref[...]
        dyp_ref[0:tc, :] = dy
        dyp_ref[tc:, :] = jnp.where(i < nchunk - 1, dyn_ref[...], 0.0)
        glp_ref[0:HALO, :] = jnp.where(i > 0, glup_ref[...], 0.0)
        glp_ref[HALO:, :] = glu_ref[...]

        @pl.when((b == 0) & (i == 0))
        def _():
            dw_ref[...] = jnp.zeros_like(dw_ref)

        for sub in range(tc // SUB):
            acc = jnp.zeros((SUB, D_CONV), F32)
            for k in range(CONV_K):
                acc = acc + dyp_ref[pl.ds(sub * SUB + (CONV_K - 1) - k, SUB), :] * w_ref[pl.ds(k, 1), :]
            acc_ref[sub * SUB:(sub + 1) * SUB, :] = acc
        for k in range(CONV_K):
            dw_ref[k:k + 1, :] += jnp.sum(dy * glp_ref[pl.ds(HALO - (CONV_K - 1) + k, tc), :], axis=0, keepdims=True)
        dglu = acc_ref[...]
        ca = ca_ref[...]
        sig = _sigmoid(cg_ref[...])
        dca_ref[...] = (dglu * sig).astype(BF16)
        dcg_ref[...] = (dglu * ca * sig * (1.0 - sig)).astype(BF16)

    tok = pl.BlockSpec((tc, D_CONV), lambda b, i: (b * nchunk + i, 0))
    nxt = pl.BlockSpec((HALO, D_CONV), lambda b, i: (jnp.minimum((b * nchunk + i + 1) * hb, last_hb), 0))
    prv = pl.BlockSpec((HALO, D_CONV), lambda b, i: (jnp.maximum((b * nchunk + i) * hb - 1, 0), 0))
    return pl.pallas_call(
        body, grid=(B, nchunk),
        in_specs=[tok, nxt, tok, prv,
                  pl.BlockSpec((tc, D_CONV), lambda b, i: (b * nchunk + i, 3)),
                  pl.BlockSpec((tc, D_CONV), lambda b, i: (b * nchunk + i, 4)),
                  pl.BlockSpec((CONV_K, D_CONV), lambda b, i: (0, 0))],
        out_specs=[tok, tok, pl.BlockSpec((32, D_CONV), lambda b, i: (0, 0))],
        out_shape=[jax.ShapeDtypeStruct((B * S, D_CONV), BF16), jax.ShapeDtypeStruct((B * S, D_CONV), BF16),
                   jax.ShapeDtypeStruct((32, D_CONV), F32)],
        scratch_shapes=[pltpu.VMEM((tc + HALO, D_CONV), F32), pltpu.VMEM((tc + HALO, D_CONV), F32),
                        pltpu.VMEM((tc, D_CONV), F32)],
        compiler_params=_params("arbitrary", "arbitrary"), name=name)(dy, dy, glu, glu, u, u, cw)


def _outproj_fwd(h, attn, cv, wout, *, tm, name):
    T, D = h.shape

    def body(h_ref, a_ref, c_ref, w_ref, o_ref):
        o_ref[...] = (h_ref[...] + _dot(a_ref[...].astype(BF16), w_ref[0:D_ATTN, :])
                      + _dot(c_ref[...], w_ref[D_ATTN:, :]))

    tok = pl.BlockSpec((tm, D), lambda i: (i, 0))
    half = pl.BlockSpec((tm, D_ATTN), lambda i: (i, 0))
    return pl.pallas_call(
        body, grid=(T // tm,), in_specs=[tok, half, half, pl.BlockSpec(wout.shape, lambda i: (0, 0))],
        out_specs=tok, out_shape=jax.ShapeDtypeStruct((T, D), F32),
        compiler_params=_params("arbitrary"), name=name)(h, attn, cv, wout)


def _outproj_bwd(dh, attn, cv, wout, *, tm, name):
    T, D = dh.shape

    def body(dh_ref, a_ref, c_ref, w_ref, da_ref, dc_ref, dw_ref):
        @pl.when(pl.program_id(0) == 0)
        def _():
            dw_ref[...] = jnp.zeros_like(dw_ref)

        dhb = dh_ref[...].astype(BF16)
        da_ref[...] = _dot_nt(dhb, w_ref[0:D_ATTN, :])
        dc_ref[...] = _dot_nt(dhb, w_ref[D_ATTN:, :])
        dw_ref[0:D_ATTN, :] += _dot_tn(a_ref[...].astype(BF16), dhb)
        dw_ref[D_ATTN:, :] += _dot_tn(c_ref[...], dhb)

    tok = pl.BlockSpec((tm, D), lambda i: (i, 0))
    half = pl.BlockSpec((tm, D_ATTN), lambda i: (i, 0))
    wspec = pl.BlockSpec(wout.shape, lambda i: (0, 0))
    return pl.pallas_call(
        body, grid=(T // tm,), in_specs=[tok, half, half, wspec], out_specs=[half, half, wspec],
        out_shape=[jax.ShapeDtypeStruct((T, D_ATTN), F32), jax.ShapeDtypeStruct((T, D_ATTN), F32),
                   jax.ShapeDtypeStruct(wout.shape, F32)],
        compiler_params=_params("arbitrary"), name=name)(dh, attn, cv, wout)


def _adamw(w, g, m, v, *, name):
    R, C = w.shape
    tr = R
    for cand in (512, 256, 128, 64, 32, 16, 8):
        if R % cand == 0 and R > cand:
            tr = cand
            break
    c1 = 1.0 - ADAM_B1 ** ADAM_STEP
    c2 = 1.0 - ADAM_B2 ** ADAM_STEP

    def body(w_ref, g_ref, m_ref, v_ref, d_ref, nm_ref, nv_ref):
        gv = g_ref[...]
        nm = ADAM_B1 * m_ref[...] + (1.0 - ADAM_B1) * gv
        nv = ADAM_B2 * v_ref[...] + (1.0 - ADAM_B2) * (gv * gv)
        d_ref[...] = -ADAM_LR * ((nm / c1) / (jnp.sqrt(nv / c2) + ADAM_EPS) + ADAM_WD * w_ref[...])
        nm_ref[...] = nm
        nv_ref[...] = nv

    blk = pl.BlockSpec((tr, C), lambda i: (i, 0))
    return pl.pallas_call(
        body, grid=(R // tr,), in_specs=[blk] * 4, out_specs=[blk] * 3,
        out_shape=[jax.ShapeDtypeStruct((R, C), F32)] * 3,
        compiler_params=_params("arbitrary"), name=name)(w, g, m, v)


TM = 512
TK = 1024
TC = 256


def _local_step(x, tgt, w):
    B, S, D = x.shape
    T = B * S
    x2 = x.reshape(T, D)
    t2 = tgt.reshape(T, D)
    ones = jnp.ones((1, D_ATTN), F32)
    scale = HEAD_DIM ** -0.5
    gains = jnp.stack([jnp.tile(w["q_norm"], (1, HEADS)) * scale, jnp.tile(w["k_norm"], (1, HEADS)), ones])

    h1, n1, G1, U1 = _ffn_fwd(x2, w["ffn1_norm"], w["wg1"], w["wu1"], w["wd1"], None, tm=TM, name="ffn1_fwd")
    u, n2 = _inproj_fwd(h1, w["mix_norm"], w["win"], tm=TM, name="inproj_fwd")
    qkv = _qkv_prep(u, gains, B, S, name="qkv_prep")
    qkv = qkv.reshape(3, N_PATTERNS, T // QBLK, QBLK, D_ATTN)
    o3, lse3 = _attn_fwd(qkv, name="attn_fwd")
    attn, lse = _attn_combine(o3, lse3, B, S, name="attn_combine")
    cv, glu, yconv = _conv_fwd(u, w["conv_w"], w["conv_b"], w["conv_ln_g"], w["conv_ln_b"], B, S, tc=TC, name="conv_fwd")
    h2 = _outproj_fwd(h1, attn, cv, w["wout"], tm=TM, name="outproj_fwd")
    dh3, n3, G2, U2, loss = _ffn_fwd(h2, w["ffn2_norm"], w["wg2"], w["wu2"], w["wd2"], t2, tm=TM, name="ffn2_fwd")

    g = {}
    dG, dU, A, dy, dh2, g["ffn2_norm"] = _ffn_bwd_act(dh3, h2, w["ffn2_norm"], G2, U2, w["wg2"], w["wu2"], w["wd2"],
                                                    tm=TM, name="ffn2_bwd_act")
    g["wg2"], g["wu2"], g["wd2"] = _ffn_bwd_w(n3, dy, dG, dU, A, tk=TK, name="ffn2_bwd_w")
    dattn, dcv, g["wout"] = _outproj_bwd(dh2, attn, cv, w["wout"], tm=TM, name="outproj_bwd")
    dyc, cpart = _conv_bwd_norm(dcv, yconv, w["conv_ln_g"], w["conv_ln_b"], tc=TC, name="conv_bwd_norm")
    dca, dcg, dcw = _conv_bwd_taps(dyc, glu, u, w["conv_w"], B, S, tc=TC, name="conv_bwd_taps")
    do3, lseb3, dd3 = _attn_bwd_prep(dattn, attn, lse, B, S, name="attn_bwd_prep")
    nb = T // QBLK
    shp = (N_PATTERNS, nb, QBLK, D_ATTN)
    cur, prev = _attn_bwd(qkv, do3.reshape(shp), lseb3.reshape(shp), dd3.reshape(shp), name="attn_bwd")
    du_qkv, dgains = _attn_grad_combine(cur, prev, u, gains, B, S, name="attn_grad_combine")
    du = jnp.concatenate([du_qkv, dca, dcg], axis=1)
    dh1, g["mix_norm"] = _inproj_bwd_act(du, dh2, h1, w["mix_norm"], w["win"], tm=TM, name="inproj_bwd_act")
    g["win"] = _inproj_bwd_w(n2, du, w["win"].shape[0], tk=TK, name="inproj_bwd_w")
    dG, dU, A, dy, dx, g["ffn1_norm"] = _ffn_bwd_act(dh1, x2, w["ffn1_norm"], G1, U1, w["wg1"], w["wu1"], w["wd1"],
                                                   tm=TM, name="ffn1_bwd_act")
    g["wg1"], g["wu1"], g["wd1"] = _ffn_bwd_w(n1, dy, dG, dU, A, tk=TK, name="ffn1_bwd_w")

    g["q_norm"] = dgains[0].reshape(HEADS, HEAD_DIM).sum(axis=0, keepdims=True) * scale
    g["k_norm"] = dgains[1].reshape(HEADS, HEAD_DIM).sum(axis=0, keepdims=True)
    g["conv_ln_g"] = cpart[0:1]
    g["conv_ln_b"] = cpart[1:2]
    g["conv_b"] = cpart[2:3]
    g["conv_w"] = dcw[:CONV_K]
    return loss, dx.reshape(B, S, D), g


N_CHIPS = 4
N_DEV = 8
ANY = pl.BlockSpec(memory_space=pl.ANY)
VMEM_SPEC = pl.BlockSpec(memory_space=pltpu.VMEM)


def _place():
    x, y, c = lax.axis_index("x"), lax.axis_index("y"), lax.axis_index("c")
    chips = [(1 - x, y), (x, 1 - y), (1 - x, 1 - y)]
    return x, y, c, 2 * x + y, chips, [2 * px + py for px, py in chips]


def _remote(src, dst, send_sem, recv_sem, device):
    return pltpu.make_async_remote_copy(src_ref=src, dst_ref=dst, send_sem=send_sem, recv_sem=recv_sem,
                                        device_id=device, device_id_type=MESH)


def _gather_weights(shards, dtypes, *, name):
    n = len(shards)
    halves = [s.reshape(2, s.shape[0] // 2, s.shape[1]) for s in shards]

    def body(*refs):
        ins, outs, vms = refs[:n], refs[n:2 * n], refs[2 * n:3 * n]
        send_sems, recv_sems, loc_sems = refs[3 * n:]
        x, y, c, me, chips, cidx = _place()
        sibling = (x, y, 1 - c)
        for a in range(n):
            vms[a][...] = ins[a][...].astype(dtypes[a])
        local, first, passed = [], [], []
        for a in range(n):
            lc = pltpu.make_async_copy(vms[a], outs[a].at[me], loc_sems.at[a])
            lc.start()
            local.append(lc)
            for j, chip in enumerate(chips):
                cp = _remote(vms[a].at[c], outs[a].at[me, c], send_sems.at[6 * a + j], recv_sems.at[6 * a + j], (*chip, c))
                cp.start()
                first.append(cp)
        for a in range(n):
            for j, chip in enumerate(chips):
                land = outs[a].at[cidx[j], c]
                _remote(land, land, send_sems.at[6 * a + j], recv_sems.at[6 * a + j], (*chip, c)).wait_recv()
                fw = _remote(land, land, send_sems.at[6 * a + 3 + j], recv_sems.at[6 * a + 3 + j], sibling)
                fw.start()
                passed.append(fw)
        for a in range(n):
            for j in range(3):
                land = outs[a].at[cidx[j], 1 - c]
                _remote(land, land, send_sems.at[6 * a + 3 + j], recv_sems.at[6 * a + 3 + j], sibling).wait_recv()
        for cp in first + passed:
            cp.wait_send()
        for lc in local:
            lc.wait()

    outs = pl.pallas_call(
        body, in_specs=[VMEM_SPEC] * n, out_specs=[ANY] * n,
        out_shape=[jax.ShapeDtypeStruct((N_CHIPS,) + h.shape, dt) for h, dt in zip(halves, dtypes)],
        scratch_shapes=[pltpu.VMEM(h.shape, dt) for h, dt in zip(halves, dtypes)]
        + [pltpu.SemaphoreType.DMA((6 * n,)), pltpu.SemaphoreType.DMA((6 * n,)), pltpu.SemaphoreType.DMA((n,))],
        compiler_params=pltpu.CompilerParams(vmem_limit_bytes=VMEM_LIMIT), name=name)(*halves)
    return [o.reshape((N_CHIPS,) + s.shape) for o, s in zip(outs, shards)]


def _exchange_halves(grads, *, name):
    n = len(grads)

    def body(*refs):
        ins, outs = refs[:n], refs[n:2 * n]
        send_sems, recv_sems = refs[2 * n:]
        x, y, c, me, chips, cidx = _place()
        cps = []
        for a in range(n):
            src = ins[a].at[pl.ds(0, ins[a].shape[0]), 1 - c]
            cp = _remote(src, outs[a], send_sems.at[a], recv_sems.at[a], (x, y, 1 - c))
            cp.start()
            cps.append(cp)
        for cp in cps:
            cp.wait()

    return pl.pallas_call(
        body, in_specs=[ANY] * n, out_specs=[ANY] * n,
        out_shape=[jax.ShapeDtypeStruct((g.shape[0],) + g.shape[2:], F32) for g in grads],
        scratch_shapes=[pltpu.SemaphoreType.DMA((n,)), pltpu.SemaphoreType.DMA((n,))], name=name)(*grads)


def _row_block(rows):
    for cand in (256, 176, 128, 64, 32, 16, 8):
        if rows % cand == 0:
            return cand
    return rows


def _add_own_half(g, r, sel, *, name):
    ns, _, rh, cdim = g.shape
    tr = _row_block(rh)

    def body(s_ref, gk_ref, rk_ref, gs_ref, rs_ref, keep_ref, send_ref):
        keep_ref[...] = gk_ref[...] + rk_ref[...]
        send_ref[...] = (gs_ref[...] + rs_ref[...]).astype(BF16)

    def g_spec(off):
        return pl.BlockSpec((None, None, tr, cdim), lambda k, i, s: (s[1 + off + k], s[0], i, 0))

    def r_spec(off):
        return pl.BlockSpec((None, tr, cdim), lambda k, i, s: (s[1 + off + k], i, 0))

    out = pl.BlockSpec((None, tr, cdim), lambda k, i, s: (k, i, 0))
    return pl.pallas_call(
        body,
        grid_spec=pltpu.PrefetchScalarGridSpec(
            num_scalar_prefetch=1, grid=(2, rh // tr),
            in_specs=[g_spec(0), r_spec(0), g_spec(2), r_spec(2)], out_specs=[out, out]),
        out_shape=[jax.ShapeDtypeStruct((2, rh, cdim), F32), jax.ShapeDtypeStruct((2, rh, cdim), BF16)],
        compiler_params=_params("arbitrary", "arbitrary"), name=name)(sel, g, r, g, r)


def _swap(arrays, stage, *, name):
    n = len(arrays)

    def body(*refs):
        ins, outs = refs[:n], refs[n:2 * n]
        send_sems, recv_sems = refs[2 * n:]
        x, y, c = lax.axis_index("x"), lax.axis_index("y"), lax.axis_index("c")
        along_x = (1 - c) if stage == 1 else c
        peer = (x + along_x * (1 - 2 * x), y + (1 - along_x) * (1 - 2 * y), c)
        cps = []
        for a in range(n):
            cp = _remote(ins[a], outs[a], send_sems.at[a], recv_sems.at[a], peer)
            cp.start()
            cps.append(cp)
        for cp in cps:
            cp.wait()

    return pl.pallas_call(
        body, in_specs=[ANY] * n, out_specs=[ANY] * n,
        out_shape=[jax.ShapeDtypeStruct(v.shape, v.dtype) for v in arrays],
        scratch_shapes=[pltpu.SemaphoreType.DMA((n,)), pltpu.SemaphoreType.DMA((n,))], name=name)(*arrays)


def _add_stage1(keep, got, *, name):
    _, rh, cdim = keep.shape
    tr = _row_block(rh)

    def body(k_ref, g_ref, keep_ref, send_ref):
        keep_ref[...] = k_ref[0] + g_ref[0].astype(F32)
        send_ref[...] = (k_ref[1] + g_ref[1].astype(F32)).astype(BF16)

    blk2 = pl.BlockSpec((2, tr, cdim), lambda i: (0, i, 0))
    blk = pl.BlockSpec((tr, cdim), lambda i: (i, 0))
    return pl.pallas_call(
        body, grid=(rh // tr,), in_specs=[blk2, blk2], out_specs=[blk, blk],
        out_shape=[jax.ShapeDtypeStruct((rh, cdim), F32), jax.ShapeDtypeStruct((rh, cdim), BF16)],
        compiler_params=_params("arbitrary"), name=name)(keep, got)


def _add_stage2(keep, got, sel, *, name):
    rh, cdim = keep.shape
    tr = _row_block(rh)

    def body(s_ref, k_ref, g_ref, o_ref):
        o_ref[...] = k_ref[...] + g_ref[...].astype(F32)

    blk = pl.BlockSpec((tr, cdim), lambda i, s: (i, 0))
    return pl.pallas_call(
        body,
        grid_spec=pltpu.PrefetchScalarGridSpec(
            num_scalar_prefetch=1, grid=(rh // tr,), in_specs=[blk, blk],
            out_specs=pl.BlockSpec((None, tr, cdim), lambda i, s: (s[0], i, 0))),
        out_shape=jax.ShapeDtypeStruct((2, rh, cdim), F32),
        compiler_params=_params("arbitrary"), name=name)(sel, keep, got)


def _join_halves(halves, *, name):
    n = len(halves)

    def body(*refs):
        outs = refs[n:2 * n]
        send_sems, recv_sems = refs[2 * n:]
        x, y, c = lax.axis_index("x"), lax.axis_index("y"), lax.axis_index("c")
        cps = []
        for a in range(n):
            mine = outs[a].at[c]
            cp = _remote(mine, mine, send_sems.at[a], recv_sems.at[a], (x, y, 1 - c))
            cp.start()
            cps.append(cp)
        for a in range(n):
            land = outs[a].at[1 - c]
            _remote(land, land, send_sems.at[a], recv_sems.at[a], (x, y, 1 - c)).wait_recv()
        for cp in cps:
            cp.wait_send()

    return pl.pallas_call(
        body, in_specs=[ANY] * n, out_specs=[ANY] * n,
        out_shape=[jax.ShapeDtypeStruct(h.shape, F32) for h in halves],
        input_output_aliases={a: a for a in range(n)},
        scratch_shapes=[pltpu.SemaphoreType.DMA((n,)), pltpu.SemaphoreType.DMA((n,))],
        name=name)(*halves)


def _slot_order():
    x, y, c = lax.axis_index("x"), lax.axis_index("y"), lax.axis_index("c")
    own, flip_x, flip_y, both = 2 * x + y, 2 * (1 - x) + y, 2 * x + 1 - y, 2 * (1 - x) + 1 - y
    first = jnp.where(c == 0, flip_x, flip_y)
    second = jnp.where(c == 0, flip_y, flip_x)
    return jnp.stack([c, own, second, first, both]).astype(jnp.int32)


def _reduce_scatter(grads):
    sel = _slot_order()
    views = [g.reshape(N_CHIPS, 2, g.shape[1] // 2, g.shape[2]) for g in grads]
    got = _exchange_halves(views, name="rs_exchange_halves")
    keep, send = zip(*[_add_own_half(v, r, sel, name=f"rs_add_half_{a}") for a, (v, r) in enumerate(zip(views, got))])
    got = _swap(list(send), 1, name="rs_swap_first_axis")
    keep, send = zip(*[_add_stage1(k, r, name=f"rs_add_first_{a}") for a, (k, r) in enumerate(zip(keep, got))])
    got = _swap(list(send), 2, name="rs_swap_second_axis")
    halves = [_add_stage2(k, r, sel, name=f"rs_add_second_{a}") for a, (k, r) in enumerate(zip(keep, got))]
    full = _join_halves(halves, name="rs_join_halves")
    return [f.reshape(g.shape[1], g.shape[2]) for f, g in zip(full, grads)]


def _allreduce_small(pack, *, name):
    rows = pack.shape[0]

    def body(p_ref, o_ref, buf_ref, send_sems, recv_sems):
        x, y, c = lax.axis_index("x"), lax.axis_index("y"), lax.axis_index("c")
        me = 4 * x + 2 * y + c
        buf_ref[me] = p_ref[...]
        cps = []
        for k in range(1, N_DEV):
            peer = tuple(1 - v if (k >> s) & 1 else v for v, s in ((x, 2), (y, 1), (c, 0)))
            cp = _remote(p_ref, buf_ref.at[me], send_sems.at[k - 1], recv_sems.at[k - 1], peer)
            cp.start()
            cps.append(cp)
        for k in range(1, N_DEV):
            src = 4 * (x ^ ((k >> 2) & 1)) + 2 * (y ^ ((k >> 1) & 1)) + (c ^ (k & 1))
            land = buf_ref.at[src]
            _remote(land, land, send_sems.at[k - 1], recv_sems.at[k - 1], (x, y, c)).wait_recv()
        acc = buf_ref[0]
        for d in range(1, N_DEV):
            acc = acc + buf_ref[d]
        o_ref[...] = acc
        for cp in cps:
            cp.wait_send()

    return pl.pallas_call(
        body, in_specs=[VMEM_SPEC], out_specs=VMEM_SPEC, out_shape=jax.ShapeDtypeStruct(pack.shape, F32),
        scratch_shapes=[pltpu.VMEM((N_DEV, rows, LANES), F32), pltpu.SemaphoreType.DMA((N_DEV - 1,)),
                        pltpu.SemaphoreType.DMA((N_DEV - 1,))], name=name)(pack)


SMALL = ("ffn1_norm", "mix_norm", "q_norm", "k_norm", "conv_b", "conv_ln_g", "conv_ln_b", "ffn2_norm", "conv_w")
BIG = ("ffn1_w_gate", "ffn1_w_up", "ffn1_w_down", "w_in", "w_out", "ffn2_w_gate", "ffn2_w_up", "ffn2_w_down")
TRANSPOSED = ("ffn1_w_gate", "ffn1_w_up", "ffn2_w_gate", "ffn2_w_up")
WEIGHTS = ("ffn1_norm", "ffn1_w_gate", "ffn1_w_up", "ffn1_w_down", "mix_norm", "w_in", "q_norm", "k_norm",
           "conv_w", "conv_b", "conv_ln_g", "conv_ln_b", "w_out", "ffn2_norm", "ffn2_w_gate", "ffn2_w_up",
           "ffn2_w_down")
SUBLANES = 8


def _pack(parts):
    rows = []
    for p in parts:
        flat = p.reshape(-1)
        tile = SUBLANES * LANES
        padded = -(-flat.shape[0] // tile) * tile
        rows.append(jnp.pad(flat, (0, padded - flat.shape[0])).reshape(-1, LANES))
    return jnp.concatenate(rows, axis=0)


def _unpack(pack, shapes):
    out, row = [], 0
    for shp in shapes:
        size = shp[0] * shp[1]
        tile = SUBLANES * LANES
        nrows = -(-size // tile) * SUBLANES
        out.append(pack[row:row + nrows].reshape(-1)[:size].reshape(shp))
        row += nrows
    return out


def kernel(x, ffn1_norm, ffn1_w_gate, ffn1_w_up, ffn1_w_down, mix_norm, w_in, q_norm, k_norm, conv_w, conv_b, conv_ln_g, conv_ln_b, w_out, ffn2_norm, ffn2_w_gate, ffn2_w_up, ffn2_w_down, loss_target, m_ffn1_norm, m_ffn1_w_gate, m_ffn1_w_up, m_ffn1_w_down, m_mix_norm, m_w_in, m_q_norm, m_k_norm, m_conv_w, m_conv_b, m_conv_ln_g, m_conv_ln_b, m_w_out, m_ffn2_norm, m_ffn2_w_gate, m_ffn2_w_up, m_ffn2_w_down, v_ffn1_norm, v_ffn1_w_gate, v_ffn1_w_up, v_ffn1_w_down, v_mix_norm, v_w_in, v_q_norm, v_k_norm, v_conv_w, v_conv_b, v_conv_ln_g, v_conv_ln_b, v_w_out, v_ffn2_norm, v_ffn2_w_gate, v_ffn2_w_up, v_ffn2_w_down):
    wts = dict(ffn1_norm=ffn1_norm, ffn1_w_gate=ffn1_w_gate[0], ffn1_w_up=ffn1_w_up[0], ffn1_w_down=ffn1_w_down[0],
               mix_norm=mix_norm, w_in=w_in[0], q_norm=q_norm, k_norm=k_norm, conv_w=conv_w[0], conv_b=conv_b,
               conv_ln_g=conv_ln_g, conv_ln_b=conv_ln_b, w_out=w_out[0], ffn2_norm=ffn2_norm,
               ffn2_w_gate=ffn2_w_gate[0], ffn2_w_up=ffn2_w_up[0], ffn2_w_down=ffn2_w_down[0])
    mom = dict(ffn1_norm=m_ffn1_norm, ffn1_w_gate=m_ffn1_w_gate[0], ffn1_w_up=m_ffn1_w_up[0], ffn1_w_down=m_ffn1_w_down[0],
               mix_norm=m_mix_norm, w_in=m_w_in[0], q_norm=m_q_norm, k_norm=m_k_norm, conv_w=m_conv_w[0], conv_b=m_conv_b,
               conv_ln_g=m_conv_ln_g, conv_ln_b=m_conv_ln_b, w_out=m_w_out[0], ffn2_norm=m_ffn2_norm,
               ffn2_w_gate=m_ffn2_w_gate[0], ffn2_w_up=m_ffn2_w_up[0], ffn2_w_down=m_ffn2_w_down[0])
    var = dict(ffn1_norm=v_ffn1_norm, ffn1_w_gate=v_ffn1_w_gate[0], ffn1_w_up=v_ffn1_w_up[0], ffn1_w_down=v_ffn1_w_down[0],
               mix_norm=v_mix_norm, w_in=v_w_in[0], q_norm=v_q_norm, k_norm=v_k_norm, conv_w=v_conv_w[0], conv_b=v_conv_b,
               conv_ln_g=v_conv_ln_g, conv_ln_b=v_conv_ln_b, w_out=v_w_out[0], ffn2_norm=v_ffn2_norm,
               ffn2_w_gate=v_ffn2_w_gate[0], ffn2_w_up=v_ffn2_w_up[0], ffn2_w_down=v_ffn2_w_down[0])
    chip = 2 * lax.axis_index("x") + lax.axis_index("y")
    for src in (wts, mom, var):
        for n in TRANSPOSED:
            src[n] = src[n].T

    wg1, wu1, wd1 = _gather_weights([wts["ffn1_w_gate"], wts["ffn1_w_up"], wts["ffn1_w_down"]], [BF16] * 3,
                                    name="gather_ffn1")
    taps = jnp.pad(wts["conv_w"], ((0, 1), (0, 0)))
    win, wout, taps4 = _gather_weights([wts["w_in"], wts["w_out"], taps], [BF16, BF16, F32], name="gather_mix")
    wg2, wu2, wd2 = _gather_weights([wts["ffn2_w_gate"], wts["ffn2_w_up"], wts["ffn2_w_down"]], [BF16] * 3,
                                    name="gather_ffn2")
    conv_full = taps4.transpose(1, 0, 2).reshape(CONV_K + 1, D_CONV)[:CONV_K]
    w = dict(ffn1_norm=ffn1_norm, mix_norm=mix_norm, ffn2_norm=ffn2_norm, q_norm=q_norm, k_norm=k_norm,
             conv_w=conv_full, conv_b=conv_b, conv_ln_g=conv_ln_g, conv_ln_b=conv_ln_b,
             wg1=wg1, wu1=wu1, wd1=wd1, wg2=wg2, wu2=wu2, wd2=wd2, win=win,
             wout=wout.reshape(N_CHIPS * wout.shape[1], wout.shape[2]))

    loss_part, grad_x, g = _local_step(x, loss_target, w)

    big_parts = [g["wg1"], g["wu1"], g["wd1"], g["win"], g["wout"].reshape(wout.shape), g["wg2"], g["wu2"], g["wd2"]]
    big_grads = dict(zip(BIG, _reduce_scatter(big_parts)))

    small_shapes = [g[n].shape for n in SMALL] + [(SUBLANES, LANES)]
    red = _allreduce_small(_pack([g[n] for n in SMALL] + [loss_part]), name="allreduce_small")
    small = dict(zip(SMALL + ("loss",), _unpack(red, small_shapes)))
    loss = small["loss"][0, 0]
    small["conv_w"] = lax.dynamic_slice_in_dim(small["conv_w"], chip * LANES, LANES, axis=1)

    grads, delta, new_m, new_v = {}, {}, {}, {}
    for n in BIG:
        grads[n] = big_grads[n]
        delta[n], new_m[n], new_v[n] = _adamw(wts[n], grads[n], mom[n], var[n], name=f"adamw_{n}")
    shapes = [wts[n].shape for n in SMALL]
    packs = [_pack([src[n] for n in SMALL]) for src in (wts, small, mom, var)]
    outs = _adamw(*packs, name="adamw_small")
    for dst, pk in zip((delta, new_m, new_v), outs):
        dst.update(zip(SMALL, _unpack(pk, shapes)))
    for n in SMALL:
        grads[n] = small[n]

    def shaped(d, n):
        v = d[n].T if n in TRANSPOSED else d[n]
        return v.reshape((1,) + v.shape) if n in BIG or n == "conv_w" else v

    return (loss, grad_x, *[shaped(grads, n) for n in WEIGHTS], *[shaped(delta, n) for n in WEIGHTS],
            *[shaped(new_m, n) for n in WEIGHTS], *[shaped(new_v, n) for n in WEIGHTS])
```

```python
import functools

import jax
import jax.numpy as jnp
from jax import lax
from jax.experimental import pallas as pl
from jax.experimental.pallas import tpu as pltpu

F32 = jnp.float32
BF16 = jnp.bfloat16

EPS = 1e-6
HEADS = 8
HEAD_DIM = 64
D_ATTN = HEADS * HEAD_DIM
D_CONV = 512
CONV_K = 31
QBLK = 128
N_PATTERNS = 3
DILATIONS = (1, 4, 16)
LANES = 128
NEG = -1e30

ADAM_LR = 0.001
ADAM_B1 = 0.9
ADAM_B2 = 0.999
ADAM_EPS = 1e-08
ADAM_WD = 0.01
ADAM_STEP = 10

VMEM_LIMIT = 56 * 1024 * 1024
MESH = pl.DeviceIdType.MESH

NT_DIMS = (((1,), (1,)), ((), ()))
TN_DIMS = (((0,), (0,)), ((), ()))


def _params(*sem):
    return pltpu.CompilerParams(dimension_semantics=sem, vmem_limit_bytes=VMEM_LIMIT)


def _dot(a, b):
    return jnp.dot(a, b, preferred_element_type=F32)


def _dot_nt(a, b):
    return lax.dot_general(a, b, NT_DIMS, preferred_element_type=F32)


def _dot_tn(a, b):
    return lax.dot_general(a, b, TN_DIMS, preferred_element_type=F32)


def _sigmoid(x):
    return 1.0 / (1.0 + jnp.exp(-x))


def _seg_mean(v, e_ref, width):
    hi = v.astype(BF16)
    lo = (v - hi.astype(F32)).astype(BF16)
    e = e_ref[...]
    return (_dot(hi, e) + _dot(lo, e)) * (1.0 / width)


def _seg_matrix(n):
    i = jnp.arange(n)
    return (i[:, None] // HEAD_DIM == i[None, :] // HEAD_DIM).astype(BF16)


def _ffn_fwd(x, gain, wg, wu, wd, tgt, *, tm, name):
    T, D = x.shape
    NS, Fs, _ = wg.shape
    with_loss = tgt is not None

    def body(*refs):
        if with_loss:
            x_ref, g_ref, wg_ref, wu_ref, wd_ref, t_ref, h_ref, n_ref, G_ref, U_ref, loss_ref, acc_ref = refs
        else:
            x_ref, g_ref, wg_ref, wu_ref, wd_ref, h_ref, n_ref, G_ref, U_ref, acc_ref = refs
        i = pl.program_id(0)
        j = pl.program_id(1)

        @pl.when(j == 0)
        def _():
            xv = x_ref[...]
            r = lax.rsqrt(jnp.mean(xv * xv, axis=-1, keepdims=True) + EPS)
            n_ref[...] = (xv * r * g_ref[...]).astype(BF16)
            acc_ref[...] = jnp.zeros_like(acc_ref)

        n = n_ref[...]
        G = _dot_nt(n, wg_ref[...])
        U = _dot_nt(n, wu_ref[...])
        G_ref[...] = G.astype(BF16)
        U_ref[...] = U.astype(BF16)
        A = (G * _sigmoid(G) * U).astype(BF16)
        acc_ref[...] += _dot(A, wd_ref[...])

        @pl.when(j == NS - 1)
        def _():
            h = x_ref[...] + 0.5 * acc_ref[...]
            if with_loss:
                e = h - t_ref[...]
                h_ref[...] = e * (1.0 / D)

                @pl.when(i == 0)
                def _():
                    loss_ref[...] = jnp.zeros_like(loss_ref)

                loss_ref[...] += jnp.sum(e * e) * (0.5 / D)
            else:
                h_ref[...] = h

    tok = pl.BlockSpec((tm, D), lambda i, j: (i, 0))
    in_specs = [tok, pl.BlockSpec((1, D), lambda i, j: (0, 0)),
                pl.BlockSpec((None, Fs, D), lambda i, j: (j, 0, 0)),
                pl.BlockSpec((None, Fs, D), lambda i, j: (j, 0, 0)),
                pl.BlockSpec((None, Fs, D), lambda i, j: (j, 0, 0))]
    args = [x, gain, wg, wu, wd]
    act = pl.BlockSpec((None, tm, Fs), lambda i, j: (j, i, 0))
    out_shape = [jax.ShapeDtypeStruct((T, D), F32), jax.ShapeDtypeStruct((T, D), BF16),
                 jax.ShapeDtypeStruct((NS, T, Fs), BF16), jax.ShapeDtypeStruct((NS, T, Fs), BF16)]
    out_specs = [tok, tok, act, act]
    if with_loss:
        in_specs.append(tok)
        args.append(tgt)
        out_shape.append(jax.ShapeDtypeStruct((8, LANES), F32))
        out_specs.append(pl.BlockSpec((8, LANES), lambda i, j: (0, 0)))
    return pl.pallas_call(
        body, grid=(T // tm, NS), in_specs=in_specs, out_specs=out_specs, out_shape=out_shape,
        scratch_shapes=[pltpu.VMEM((tm, D), F32)],
        compiler_params=_params("arbitrary", "arbitrary"), name=name)(*args)


def _rms_bwd(xv, gain, dn):
    r = lax.rsqrt(jnp.mean(xv * xv, axis=-1, keepdims=True) + EPS)
    xhat = xv * r
    dxh = dn * gain
    dx = r * (dxh - xhat * jnp.mean(dxh * xhat, axis=-1, keepdims=True))
    dg = jnp.sum(dn * xhat, axis=0, keepdims=True)
    return dx, dg


def _ffn_bwd_act(dh, x, gain, G, U, wg, wu, wd, *, tm, name):
    T, D = x.shape
    NS, Fs, _ = wg.shape

    def body(dh_ref, x_ref, g_ref, G_ref, U_ref, wg_ref, wu_ref, wd_ref,
             dG_ref, dU_ref, A_ref, dy_ref, dx_ref, dg_ref, acc_ref):
        i = pl.program_id(0)
        j = pl.program_id(1)

        @pl.when(j == 0)
        def _():
            dy_ref[...] = (0.5 * dh_ref[...]).astype(BF16)
            acc_ref[...] = jnp.zeros_like(acc_ref)

        @pl.when((i == 0) & (j == 0))
        def _():
            dg_ref[...] = jnp.zeros_like(dg_ref)

        Gv = G_ref[...].astype(F32)
        Uv = U_ref[...].astype(F32)
        sig = _sigmoid(Gv)
        s = Gv * sig
        dA = _dot_nt(dy_ref[...], wd_ref[...])
        dG = (dA * Uv * (sig * (1.0 + Gv * (1.0 - sig)))).astype(BF16)
        dU = (dA * s).astype(BF16)
        dG_ref[...] = dG
        dU_ref[...] = dU
        A_ref[...] = (s * Uv).astype(BF16)
        acc_ref[...] += _dot(dG, wg_ref[...]) + _dot(dU, wu_ref[...])

        @pl.when(j == NS - 1)
        def _():
            dx, dg = _rms_bwd(x_ref[...], g_ref[...], acc_ref[...])
            dx_ref[...] = dh_ref[...] + dx
            dg_ref[...] += dg

    tok = pl.BlockSpec((tm, D), lambda i, j: (i, 0))
    act = pl.BlockSpec((None, tm, Fs), lambda i, j: (j, i, 0))
    vec = pl.BlockSpec((1, D), lambda i, j: (0, 0))
    return pl.pallas_call(
        body, grid=(T // tm, NS),
        in_specs=[tok, tok, vec, act, act,
                  pl.BlockSpec((None, Fs, D), lambda i, j: (j, 0, 0)),
                  pl.BlockSpec((None, Fs, D), lambda i, j: (j, 0, 0)),
                  pl.BlockSpec((None, Fs, D), lambda i, j: (j, 0, 0))],
        out_specs=[act, act, act, tok, tok, vec],
        out_shape=[jax.ShapeDtypeStruct((NS, T, Fs), BF16)] * 3
        + [jax.ShapeDtypeStruct((T, D), BF16), jax.ShapeDtypeStruct((T, D), F32),
           jax.ShapeDtypeStruct((1, D), F32)],
        scratch_shapes=[pltpu.VMEM((tm, D), F32)],
        compiler_params=_params("arbitrary", "arbitrary"), name=name)(dh, x, gain, G, U, wg, wu, wd)


def _ffn_bwd_w(n, dy, dG, dU, A, *, tk, name):
    T, D = n.shape
    NS, _, Fs = dG.shape

    def body(n_ref, dy_ref, dG_ref, dU_ref, A_ref, wg_ref, wu_ref, wd_ref):
        @pl.when(pl.program_id(1) == 0)
        def _():
            wg_ref[...] = jnp.zeros_like(wg_ref)
            wu_ref[...] = jnp.zeros_like(wu_ref)
            wd_ref[...] = jnp.zeros_like(wd_ref)

        nv = n_ref[...]
        wg_ref[...] += _dot_tn(dG_ref[...], nv)
        wu_ref[...] += _dot_tn(dU_ref[...], nv)
        wd_ref[...] += _dot_tn(A_ref[...], dy_ref[...])

    tok = pl.BlockSpec((tk, D), lambda j, k: (k, 0))
    act = pl.BlockSpec((None, tk, Fs), lambda j, k: (j, k, 0))
    return pl.pallas_call(
        body, grid=(NS, T // tk), in_specs=[tok, tok, act, act, act],
        out_specs=[pl.BlockSpec((None, Fs, D), lambda j, k: (j, 0, 0))] * 3,
        out_shape=[jax.ShapeDtypeStruct((NS, Fs, D), F32)] * 3,
        compiler_params=_params("arbitrary", "arbitrary"), name=name)(n, dy, dG, dU, A)


def _inproj_fwd(h, gain, win, *, tm, name):
    T, D = h.shape
    NS, _, Cs = win.shape

    def body(h_ref, g_ref, w_ref, u_ref, n_ref):
        @pl.when(pl.program_id(1) == 0)
        def _():
            xv = h_ref[...]
            r = lax.rsqrt(jnp.mean(xv * xv, axis=-1, keepdims=True) + EPS)
            n_ref[...] = (xv * r * g_ref[...]).astype(BF16)

        u_ref[...] = _dot(n_ref[...], w_ref[...])

    tok = pl.BlockSpec((tm, D), lambda i, j: (i, 0))
    return pl.pallas_call(
        body, grid=(T // tm, NS),
        in_specs=[tok, pl.BlockSpec((1, D), lambda i, j: (0, 0)),
                  pl.BlockSpec((None, D, Cs), lambda i, j: (j, 0, 0))],
        out_specs=[pl.BlockSpec((tm, Cs), lambda i, j: (i, j)), tok],
        out_shape=[jax.ShapeDtypeStruct((T, NS * Cs), F32), jax.ShapeDtypeStruct((T, D), BF16)],
        compiler_params=_params("arbitrary", "arbitrary"), name=name)(h, gain, win)


def _inproj_bwd_act(du, dh, h, gain, win, *, tm, name):
    T, D = h.shape
    NS, _, Cs = win.shape

    def body(du_ref, dh_ref, h_ref, g_ref, w_ref, dx_ref, dg_ref, acc_ref):
        i = pl.program_id(0)
        j = pl.program_id(1)

        @pl.when(j == 0)
        def _():
            acc_ref[...] = jnp.zeros_like(acc_ref)

        @pl.when((i == 0) & (j == 0))
        def _():
            dg_ref[...] = jnp.zeros_like(dg_ref)

        acc_ref[...] += _dot_nt(du_ref[...], w_ref[...])

        @pl.when(j == NS - 1)
        def _():
            dx, dg = _rms_bwd(h_ref[...], g_ref[...], acc_ref[...])
            dx_ref[...] = dh_ref[...] + dx
            dg_ref[...] += dg

    tok = pl.BlockSpec((tm, D), lambda i, j: (i, 0))
    vec = pl.BlockSpec((1, D), lambda i, j: (0, 0))
    return pl.pallas_call(
        body, grid=(T // tm, NS),
        in_specs=[pl.BlockSpec((tm, Cs), lambda i, j: (i, j)), tok, tok, vec,
                  pl.BlockSpec((None, D, Cs), lambda i, j: (j, 0, 0))],
        out_specs=[tok, vec],
        out_shape=[jax.ShapeDtypeStruct((T, D), F32), jax.ShapeDtypeStruct((1, D), F32)],
        scratch_shapes=[pltpu.VMEM((tm, D), F32)],
        compiler_params=_params("arbitrary", "arbitrary"), name=name)(du, dh, h, gain, win)


def _inproj_bwd_w(n, du, ns, *, tk, name):
    T, D = n.shape
    Cs = du.shape[1] // ns

    def body(n_ref, du_ref, w_ref):
        @pl.when(pl.program_id(1) == 0)
        def _():
            w_ref[...] = jnp.zeros_like(w_ref)

        w_ref[...] += _dot_tn(n_ref[...], du_ref[...])

    return pl.pallas_call(
        body, grid=(ns, T // tk),
        in_specs=[pl.BlockSpec((tk, D), lambda j, k: (k, 0)), pl.BlockSpec((tk, Cs), lambda j, k: (k, j))],
        out_specs=pl.BlockSpec((None, D, Cs), lambda j, k: (j, 0, 0)),
        out_shape=jax.ShapeDtypeStruct((ns, D, Cs), F32),
        compiler_params=_params("arbitrary", "arbitrary"), name=name)(n, du)


def _permute_out(src_ref, out_ref, cast):
    S = src_ref.shape[1]
    for p, d in enumerate(DILATIONS):
        L = S // d
        for cc in range(4):
            cols = slice(cc * LANES, (cc + 1) * LANES)
            if d == 1:
                out_ref[p, :, cols] = src_ref[cc].astype(cast)
            else:
                for r in range(d):
                    out_ref[p, r * L:(r + 1) * L, cols] = src_ref[cc, pl.ds(r, L, stride=d), :].astype(cast)


def _unpermute_in(get_block, dst_ref, p, S):
    d = DILATIONS[p]
    L = S // d
    if d == 1:
        dst_ref[...] = get_block(0, S)
    else:
        for r in range(d):
            dst_ref[pl.ds(r, L, stride=d), :] = get_block(r * L, L)


def _qkv_prep(u, gains, B, S, *, name):
    emat = _seg_matrix(D_ATTN)

    def body(u_ref, g_ref, e_ref, out_ref, scr_ref):
        c = pl.program_id(1)
        xv = u_ref[...]
        ms = _seg_mean(xv * xv, e_ref, HEAD_DIM)
        r = jnp.where(c < 2, lax.rsqrt(ms + EPS), 1.0)
        yv = xv * r * g_ref[...]
        for cc in range(4):
            scr_ref[cc] = yv[:, cc * LANES:(cc + 1) * LANES]
        _permute_out(scr_ref, out_ref, BF16)

    return pl.pallas_call(
        body, grid=(B, 3),
        in_specs=[pl.BlockSpec((S, D_ATTN), lambda b, c: (b, c)),
                  pl.BlockSpec((None, 1, D_ATTN), lambda b, c: (c, 0, 0)),
                  pl.BlockSpec((D_ATTN, D_ATTN), lambda b, c: (0, 0))],
        out_specs=pl.BlockSpec((None, N_PATTERNS, None, S, D_ATTN), lambda b, c: (c, 0, b, 0, 0)),
        out_shape=jax.ShapeDtypeStruct((3, N_PATTERNS, B, S, D_ATTN), BF16),
        scratch_shapes=[pltpu.VMEM((4, S, LANES), F32)],
        compiler_params=_params("arbitrary", "arbitrary"), name=name)(u, gains, emat)


def _band_mask(p, b):
    nblk = jnp.right_shift(16, 2 * p)
    has_prev = jnp.bitwise_and(b, nblk - 1) != 0
    qi = lax.broadcasted_iota(jnp.int32, (QBLK, 2 * QBLK), 0)
    ci = lax.broadcasted_iota(jnp.int32, (QBLK, 2 * QBLK), 1)
    dist = QBLK + qi - ci
    return (dist >= 0) & (dist <= QBLK) & (has_prev | (ci >= QBLK))


def _first_head(rows):
    return lax.broadcasted_iota(jnp.int32, (rows, LANES), 1) < HEAD_DIM


def _split_heads(pair):
    first = _first_head(pair.shape[0])
    zero = jnp.zeros_like(pair)
    return jnp.concatenate([jnp.where(first, pair, zero), jnp.where(first, zero, pair)], axis=0)


def _merge_heads(col_a, col_b):
    rows = col_a.shape[0]
    return jnp.where(_first_head(rows), jnp.broadcast_to(col_a, (rows, LANES)), jnp.broadcast_to(col_b, (rows, LANES)))


def _attn_specs(nb):
    blk = (None, None, None, QBLK, D_ATTN)
    q_spec = pl.BlockSpec(blk, lambda p, b: (0, p, b, 0, 0))
    kp_spec = pl.BlockSpec(blk, lambda p, b: (1, p, jnp.maximum(b - 1, 0), 0, 0))
    kc_spec = pl.BlockSpec(blk, lambda p, b: (1, p, b, 0, 0))
    vp_spec = pl.BlockSpec(blk, lambda p, b: (2, p, jnp.maximum(b - 1, 0), 0, 0))
    vc_spec = pl.BlockSpec(blk, lambda p, b: (2, p, b, 0, 0))
    return [q_spec, kp_spec, kc_spec, vp_spec, vc_spec]


def _attn_fwd(qkv, *, name):
    nb = qkv.shape[2]

    def body(q_ref, kp_ref, kc_ref, vp_ref, vc_ref, o_ref, lse_ref):
        mask = _band_mask(pl.program_id(0), pl.program_id(1))
        mask2 = jnp.concatenate([mask, mask], axis=0)
        kk = jnp.concatenate([kp_ref[...], kc_ref[...]], axis=0)
        vv = jnp.concatenate([vp_ref[...], vc_ref[...]], axis=0)
        for hp in range(HEADS // 2):
            cols = slice(hp * LANES, (hp + 1) * LANES)
            s = _dot_nt(_split_heads(q_ref[:, cols]), kk[:, cols])
            s = jnp.where(mask2, s, NEG)
            m = jnp.max(s, axis=-1, keepdims=True)
            e = jnp.exp(s - m)
            l = jnp.sum(e, axis=-1, keepdims=True)
            pr = (e * (1.0 / l)).astype(BF16)
            o_ref[:, cols] = _dot(jnp.concatenate([pr[:QBLK], pr[QBLK:]], axis=1), _split_heads(vv[:, cols]))
            lse = m + jnp.log(l)
            lse_ref[:, cols] = _merge_heads(lse[:QBLK], lse[QBLK:])

    out = pl.BlockSpec((None, None, QBLK, D_ATTN), lambda p, b: (p, b, 0, 0))
    return pl.pallas_call(
        body, grid=(N_PATTERNS, nb), in_specs=_attn_specs(nb), out_specs=[out, out],
        out_shape=[jax.ShapeDtypeStruct((N_PATTERNS, nb, QBLK, D_ATTN), F32)] * 2,
        compiler_params=_params("arbitrary", "arbitrary"), name=name)(qkv, qkv, qkv, qkv, qkv)


def _attn_combine(o3, lse3, B, S, *, name):
    def body(o_ref, l_ref, a_ref, lt_ref, so_ref, sl_ref):
        for p in range(N_PATTERNS):
            _unpermute_in(lambda r0, n, p=p: o_ref[p, pl.ds(r0, n), :], so_ref.at[p], p, S)
            _unpermute_in(lambda r0, n, p=p: l_ref[p, pl.ds(r0, n), :], sl_ref.at[p], p, S)
        l0, l1, l2 = sl_ref[0], sl_ref[1], sl_ref[2]
        m = jnp.maximum(jnp.maximum(l0, l1), l2)
        w0, w1, w2 = jnp.exp(l0 - m), jnp.exp(l1 - m), jnp.exp(l2 - m)
        tot = w0 + w1 + w2
        a_ref[...] = (w0 * so_ref[0] + w1 * so_ref[1] + w2 * so_ref[2]) / tot
        lt_ref[...] = m + jnp.log(tot)

    o3 = o3.reshape(N_PATTERNS, B, S, D_ATTN)
    lse3 = lse3.reshape(N_PATTERNS, B, S, D_ATTN)
    inp = pl.BlockSpec((N_PATTERNS, None, S, LANES), lambda b, c: (0, b, 0, c))
    out = pl.BlockSpec((S, LANES), lambda b, c: (b, c))
    return pl.pallas_call(
        body, grid=(B, D_ATTN // LANES), in_specs=[inp, inp], out_specs=[out, out],
        out_shape=[jax.ShapeDtypeStruct((B * S, D_ATTN), F32)] * 2,
        scratch_shapes=[pltpu.VMEM((N_PATTERNS, S, LANES), F32)] * 2,
        compiler_params=_params("arbitrary", "arbitrary"), name=name)(o3, lse3)


def _attn_bwd_prep(dattn, attn, lse, B, S, *, name):
    emat = _seg_matrix(LANES)

    def body(da_ref, a_ref, l_ref, e_ref, do_ref, lo_ref, dd_ref, scr_ref):
        da = da_ref[...]
        dsum = _seg_mean(da * a_ref[...], e_ref, 1.0)
        for k, (val, out_ref, cast) in enumerate(((da, do_ref, BF16), (l_ref[...], lo_ref, F32), (dsum, dd_ref, F32))):
            scr_ref[...] = val
            for p, d in enumerate(DILATIONS):
                L = S // d
                if d == 1:
                    out_ref[p] = val.astype(cast)
                else:
                    for r in range(d):
                        out_ref[p, r * L:(r + 1) * L, :] = scr_ref[pl.ds(r, L, stride=d), :].astype(cast)

    inp = pl.BlockSpec((S, LANES), lambda b, c: (b, c))
    out = pl.BlockSpec((N_PATTERNS, None, S, LANES), lambda b, c: (0, b, 0, c))
    shp = (N_PATTERNS, B, S, D_ATTN)
    return pl.pallas_call(
        body, grid=(B, D_ATTN // LANES),
        in_specs=[inp, inp, inp, pl.BlockSpec((LANES, LANES), lambda b, c: (0, 0))],
        out_specs=[out, out, out],
        out_shape=[jax.ShapeDtypeStruct(shp, BF16), jax.ShapeDtypeStruct(shp, F32), jax.ShapeDtypeStruct(shp, F32)],
        scratch_shapes=[pltpu.VMEM((S, LANES), F32)],
        compiler_params=_params("arbitrary", "arbitrary"), name=name)(dattn, attn, lse, emat)


def _attn_bwd(qkv, do3, lse3, dd3, *, name):
    nb = qkv.shape[2]

    def body(q_ref, kp_ref, kc_ref, vp_ref, vc_ref, do_ref, l_ref, d_ref, cur_ref, prev_ref):
        mask = _band_mask(pl.program_id(0), pl.program_id(1))
        mask2 = jnp.concatenate([mask, mask], axis=0)
        kk = jnp.concatenate([kp_ref[...], kc_ref[...]], axis=0)
        vv = jnp.concatenate([vp_ref[...], vc_ref[...]], axis=0)
        for hp in range(HEADS // 2):
            cols = slice(hp * LANES, (hp + 1) * LANES)
            a, b = hp * LANES, hp * LANES + HEAD_DIM
            kh, vh = kk[:, cols], vv[:, cols]
            q2 = _split_heads(q_ref[:, cols])
            do2 = _split_heads(do_ref[:, cols])
            lse = jnp.concatenate([l_ref[:, a:a + 1], l_ref[:, b:b + 1]], axis=0)
            dsum = jnp.concatenate([d_ref[:, a:a + 1], d_ref[:, b:b + 1]], axis=0)
            s = _dot_nt(q2, kh)
            pr = jnp.where(mask2, jnp.exp(s - lse), 0.0)
            dp = _dot_nt(do2, vh)
            ds = (pr * (dp - dsum)).astype(BF16)
            prb = pr.astype(BF16)
            cur_ref[0, :, cols] = _dot(jnp.concatenate([ds[:QBLK], ds[QBLK:]], axis=1), _split_heads(kh))
            dk = _dot_tn(ds, q2)
            dv = _dot_tn(prb, do2)
            prev_ref[0, :, cols] = dk[:QBLK]
            cur_ref[1, :, cols] = dk[QBLK:]
            prev_ref[1, :, cols] = dv[:QBLK]
            cur_ref[2, :, cols] = dv[QBLK:]

    aux = pl.BlockSpec((None, None, QBLK, D_ATTN), lambda p, b: (p, b, 0, 0))
    return pl.pallas_call(
        body, grid=(N_PATTERNS, nb), in_specs=_attn_specs(nb) + [aux, aux, aux],
        out_specs=[pl.BlockSpec((3, None, None, QBLK, D_ATTN), lambda p, b: (0, p, b, 0, 0)),
                   pl.BlockSpec((2, None, None, QBLK, D_ATTN), lambda p, b: (0, p, b, 0, 0))],
        out_shape=[jax.ShapeDtypeStruct((3, N_PATTERNS, nb, QBLK, D_ATTN), F32),
                   jax.ShapeDtypeStruct((2, N_PATTERNS, nb, QBLK, D_ATTN), F32)],
        compiler_params=_params("arbitrary", "arbitrary"), name=name)(qkv, qkv, qkv, qkv, qkv, do3, lse3, dd3)


def _attn_grad_combine(cur, prev, u, gains, B, S, *, name):
    emat = _seg_matrix(LANES)
    nblk = S // QBLK

    def body(cur_ref, prev_ref, u_ref, g_ref, e_ref, du_ref, dg_ref, scr_ref):
        c = pl.program_id(0)
        b = pl.program_id(2)
        use_prev = c > 0
        total = None
        for p, d in enumerate(DILATIONS):
            per_seq = nblk // d

            def get_block(r0, n, p=p, per_seq=per_seq):
                parts = []
                for blk in range(r0 // QBLK, (r0 + n) // QBLK):
                    v = cur_ref[p, pl.ds(blk * QBLK, QBLK), :]
                    if blk % per_seq != per_seq - 1:
                        nxt = prev_ref[p, pl.ds((blk + 1) * QBLK, QBLK), :]
                        v = v + jnp.where(use_prev, nxt, 0.0)
                    parts.append(v)
                return parts[0] if len(parts) == 1 else jnp.concatenate(parts, axis=0)

            _unpermute_in(get_block, scr_ref.at[p], p, S)
        dy = scr_ref[0] + scr_ref[1] + scr_ref[2]
        xv = u_ref[...]
        gain = g_ref[...]
        ms = _seg_mean(xv * xv, e_ref, HEAD_DIM)
        r = lax.rsqrt(ms + EPS)
        xhat = xv * r
        dxh = dy * gain
        dx = r * (dxh - xhat * _seg_mean(dxh * xhat, e_ref, HEAD_DIM))
        du_ref[...] = jnp.where(c < 2, dx, dy).astype(BF16)

        @pl.when((b == 0))
        def _():
            dg_ref[...] = jnp.zeros_like(dg_ref)

        dg_ref[...] += jnp.sum(dy * xhat, axis=0, keepdims=True)

    cur = cur.reshape(3, N_PATTERNS, B, S, D_ATTN)
    prev = prev.reshape(2, N_PATTERNS, B, S, D_ATTN)
    ncc = D_ATTN // LANES
    return pl.pallas_call(
        body, grid=(3, ncc, B),
        in_specs=[pl.BlockSpec((None, N_PATTERNS, None, S, LANES), lambda c, cc, b: (c, 0, b, 0, cc)),
                  pl.BlockSpec((None, N_PATTERNS, None, S, LANES), lambda c, cc, b: (jnp.maximum(c - 1, 0), 0, b, 0, cc)),
                  pl.BlockSpec((S, LANES), lambda c, cc, b: (b, c * ncc + cc)),
                  pl.BlockSpec((None, 1, LANES), lambda c, cc, b: (c, 0, cc)),
                  pl.BlockSpec((LANES, LANES), lambda c, cc, b: (0, 0))],
        out_specs=[pl.BlockSpec((S, LANES), lambda c, cc, b: (b, c * ncc + cc)),
                   pl.BlockSpec((None, 1, LANES), lambda c, cc, b: (c, 0, cc))],
        out_shape=[jax.ShapeDtypeStruct((B * S, 3 * D_ATTN), BF16), jax.ShapeDtypeStruct((3, 1, D_ATTN), F32)],
        scratch_shapes=[pltpu.VMEM((N_PATTERNS, S, LANES), F32)],
        compiler_params=_params("arbitrary", "arbitrary", "arbitrary"), name=name)(cur, prev, u, gains, emat)


HALO = 32
SUB = 64


def _conv_fwd(u, cw, cb, lg, lb, B, S, *, tc, name):
    nchunk = S // tc
    hb = tc // HALO

    def body(ca_ref, cap_ref, cg_ref, cgp_ref, w_ref, cb_ref, lg_ref, lb_ref, cv_ref, glu_ref, y_ref, pad_ref):
        i = pl.program_id(1)
        glu = ca_ref[...] * _sigmoid(cg_ref[...])
        glu_ref[...] = glu
        prev = cap_ref[...] * _sigmoid(cgp_ref[...])
        pad_ref[0:HALO, :] = jnp.where(i > 0, prev, 0.0)
        pad_ref[HALO:, :] = glu
        for sub in range(tc // SUB):
            acc = jnp.zeros((SUB, D_CONV), F32) + cb_ref[...]
            for k in range(CONV_K):
                acc = acc + pad_ref[pl.ds(sub * SUB + HALO - (CONV_K - 1) + k, SUB), :] * w_ref[pl.ds(k, 1), :]
            y_ref[sub * SUB:(sub + 1) * SUB, :] = acc
        y = y_ref[...]
        mu = jnp.mean(y, axis=-1, keepdims=True)
        yc = y - mu
        var = jnp.mean(yc * yc, axis=-1, keepdims=True)
        z = yc * lax.rsqrt(var + EPS) * lg_ref[...] + lb_ref[...]
        cv_ref[...] = (z * _sigmoid(z)).astype(BF16)

    def cur(col):
        return pl.BlockSpec((tc, D_CONV), lambda b, i: (b * nchunk + i, col))

    def halo(col):
        return pl.BlockSpec((HALO, D_CONV), lambda b, i: (jnp.maximum((b * nchunk + i) * hb - 1, 0), col))

    vec = pl.BlockSpec((1, D_CONV), lambda b, i: (0, 0))
    out = pl.BlockSpec((tc, D_CONV), lambda b, i: (b * nchunk + i, 0))
    return pl.pallas_call(
        body, grid=(B, nchunk),
        in_specs=[cur(3), halo(3), cur(4), halo(4), pl.BlockSpec((CONV_K, D_CONV), lambda b, i: (0, 0)), vec, vec, vec],
        out_specs=[out, out, out],
        out_shape=[jax.ShapeDtypeStruct((B * S, D_CONV), BF16), jax.ShapeDtypeStruct((B * S, D_CONV), F32),
                   jax.ShapeDtypeStruct((B * S, D_CONV), F32)],
        scratch_shapes=[pltpu.VMEM((tc + HALO, D_CONV), F32)],
        compiler_params=_params("arbitrary", "arbitrary"), name=name)(u, u, u, u, cw, cb, lg, lb)


def _conv_bwd_norm(dcv, y, lg, lb, *, tc, name):
    T = y.shape[0]

    def body(dcv_ref, y_ref, lg_ref, lb_ref, dy_ref, part_ref):
        yv = y_ref[...]
        mu = jnp.mean(yv, axis=-1, keepdims=True)
        yc = yv - mu
        var = jnp.mean(yc * yc, axis=-1, keepdims=True)
        rstd = lax.rsqrt(var + EPS)
        xhat = yc * rstd
        z = xhat * lg_ref[...] + lb_ref[...]
        sig = _sigmoid(z)
        dz = dcv_ref[...] * (sig * (1.0 + z * (1.0 - sig)))
        dxh = dz * lg_ref[...]
        dy = rstd * (dxh - jnp.mean(dxh, axis=-1, keepdims=True)
                     - xhat * jnp.mean(dxh * xhat, axis=-1, keepdims=True))
        dy_ref[...] = dy

        @pl.when(pl.program_id(0) == 0)
        def _():
            part_ref[...] = jnp.zeros_like(part_ref)

        part_ref[0:1, :] += jnp.sum(dz * xhat, axis=0, keepdims=True)
        part_ref[1:2, :] += jnp.sum(dz, axis=0, keepdims=True)
        part_ref[2:3, :] += jnp.sum(dy, axis=0, keepdims=True)

    tok = pl.BlockSpec((tc, D_CONV), lambda i: (i, 0))
    vec = pl.BlockSpec((1, D_CONV), lambda i: (0, 0))
    return pl.pallas_call(
        body, grid=(T // tc,), in_specs=[tok, tok, vec, vec],
        out_specs=[tok, pl.BlockSpec((8, D_CONV), lambda i: (0, 0))],
        out_shape=[jax.ShapeDtypeStruct((T, D_CONV), F32), jax.ShapeDtypeStruct((8, D_CONV), F32)],
        compiler_params=_params("arbitrary"), name=name)(dcv, y, lg, lb)


def _conv_bwd_taps(dy, glu, u, cw, B, S, *, tc, name):
    nchunk = S // tc
    hb = tc // HALO
    last_hb = B * S // HALO - 1

    def body(dy_ref, dyn_ref, glu_ref, glup_ref, ca_ref, cg_ref, w_ref, dca_ref, dcg_ref, dw_ref, dyp_ref, glp_ref, acc_ref):
        b = pl.program_id(0)
        i = pl.program_id(1)
        dy = dy_ref[...]
        dyp_ref[0:tc, :] = dy
        dyp_ref[tc:, :] = jnp.where(i < nchunk - 1, dyn_ref[...], 0.0)
        glp_ref[0:HALO, :] = jnp.where(i > 0, glup_ref[...], 0.0)
        glp_ref[HALO:, :] = glu_ref[...]

        @pl.when((b == 0) & (i == 0))
        def _():
            dw_ref[...] = jnp.zeros_like(dw_ref)

        for sub in range(tc // SUB):
            acc = jnp.zeros((SUB, D_CONV), F32)
            for k in range(CONV_K):
                acc = acc + dyp_ref[pl.ds(sub * SUB + (CONV_K - 1) - k, SUB), :] * w_ref[pl.ds(k, 1), :]
            acc_ref[sub * SUB:(sub + 1) * SUB, :] = acc
        for k in range(CONV_K):
            dw_ref[k:k + 1, :] += jnp.sum(dy * glp_ref[pl.ds(HALO - (CONV_K - 1) + k, tc), :], axis=0, keepdims=True)
        dglu = acc_ref[...]
        ca = ca_ref[...]
        sig = _sigmoid(cg_ref[...])
        dca_ref[...] = (dglu * sig).astype(BF16)
        dcg_ref[...] = (dglu * ca * sig * (1.0 - sig)).astype(BF16)

    tok = pl.BlockSpec((tc, D_CONV), lambda b, i: (b * nchunk + i, 0))
    nxt = pl.BlockSpec((HALO, D_CONV), lambda b, i: (jnp.minimum((b * nchunk + i + 1) * hb, last_hb), 0))
    prv = pl.BlockSpec((HALO, D_CONV), lambda b, i: (jnp.maximum((b * nchunk + i) * hb - 1, 0), 0))
    return pl.pallas_call(
        body, grid=(B, nchunk),
        in_specs=[tok, nxt, tok, prv,
                  pl.BlockSpec((tc, D_CONV), lambda b, i: (b * nchunk + i, 3)),
                  pl.BlockSpec((tc, D_CONV), lambda b, i: (b * nchunk + i, 4)),
                  pl.BlockSpec((CONV_K, D_CONV), lambda b, i: (0, 0))],
        out_specs=[tok, tok, pl.BlockSpec((32, D_CONV), lambda b, i: (0, 0))],
        out_shape=[jax.ShapeDtypeStruct((B * S, D_CONV), BF16), jax.ShapeDtypeStruct((B * S, D_CONV), BF16),
                   jax.ShapeDtypeStruct((32, D_CONV), F32)],
        scratch_shapes=[pltpu.VMEM((tc + HALO, D_CONV), F32), pltpu.VMEM((tc + HALO, D_CONV), F32),
                        pltpu.VMEM((tc, D_CONV), F32)],
        compiler_params=_params("arbitrary", "arbitrary"), name=name)(dy, dy, glu, glu, u, u, cw)


def _outproj_fwd(h, attn, cv, wout, *, tm, name):
    T, D = h.shape

    def body(h_ref, a_ref, c_ref, w_ref, o_ref):
        o_ref[...] = (h_ref[...] + _dot(a_ref[...].astype(BF16), w_ref[0:D_ATTN, :])
                      + _dot(c_ref[...], w_ref[D_ATTN:, :]))

    tok = pl.BlockSpec((tm, D), lambda i: (i, 0))
    half = pl.BlockSpec((tm, D_ATTN), lambda i: (i, 0))
    return pl.pallas_call(
        body, grid=(T // tm,), in_specs=[tok, half, half, pl.BlockSpec(wout.shape, lambda i: (0, 0))],
        out_specs=tok, out_shape=jax.ShapeDtypeStruct((T, D), F32),
        compiler_params=_params("arbitrary"), name=name)(h, attn, cv, wout)


def _outproj_bwd(dh, attn, cv, wout, *, tm, name):
    T, D = dh.shape

    def body(dh_ref, a_ref, c_ref, w_ref, da_ref, dc_ref, dw_ref):
        @pl.when(pl.program_id(0) == 0)
        def _():
            dw_ref[...] = jnp.zeros_like(dw_ref)

        dhb = dh_ref[...].astype(BF16)
        da_ref[...] = _dot_nt(dhb, w_ref[0:D_ATTN, :])
        dc_ref[...] = _dot_nt(dhb, w_ref[D_ATTN:, :])
        dw_ref[0:D_ATTN, :] += _dot_tn(a_ref[...].astype(BF16), dhb)
        dw_ref[D_ATTN:, :] += _dot_tn(c_ref[...], dhb)

    tok = pl.BlockSpec((tm, D), lambda i: (i, 0))
    half = pl.BlockSpec((tm, D_ATTN), lambda i: (i, 0))
    wspec = pl.BlockSpec(wout.shape, lambda i: (0, 0))
    return pl.pallas_call(
        body, grid=(T // tm,), in_specs=[tok, half, half, wspec], out_specs=[half, half, wspec],
        out_shape=[jax.ShapeDtypeStruct((T, D_ATTN), F32), jax.ShapeDtypeStruct((T, D_ATTN), F32),
                   jax.ShapeDtypeStruct(wout.shape, F32)],
        compiler_params=_params("arbitrary"), name=name)(dh, attn, cv, wout)


def _adamw(w, g, m, v, *, name):
    R, C = w.shape
    tr = R
    for cand in (512, 256, 128, 64, 32, 16, 8):
        if R % cand == 0 and R > cand:
            tr = cand
            break
    c1 = 1.0 - ADAM_B1 ** ADAM_STEP
    c2 = 1.0 - ADAM_B2 ** ADAM_STEP

    def body(w_ref, g_ref, m_ref, v_ref, d_ref, nm_ref, nv_ref):
        gv = g_ref[...]
        nm = ADAM_B1 * m_ref[...] + (1.0 - ADAM_B1) * gv
        nv = ADAM_B2 * v_ref[...] + (1.0 - ADAM_B2) * (gv * gv)
        d_ref[...] = -ADAM_LR * ((nm / c1) / (jnp.sqrt(nv / c2) + ADAM_EPS) + ADAM_WD * w_ref[...])
        nm_ref[...] = nm
        nv_ref[...] = nv

    blk = pl.BlockSpec((tr, C), lambda i: (i, 0))
    return pl.pallas_call(
        body, grid=(R // tr,), in_specs=[blk] * 4, out_specs=[blk] * 3,
        out_shape=[jax.ShapeDtypeStruct((R, C), F32)] * 3,
        compiler_params=_params("arbitrary"), name=name)(w, g, m, v)


TM = 512
TM_FFN_FWD = 1024
TK = 1024
TC = 256


def _local_step(x, tgt, w):
    B, S, D = x.shape
    T = B * S
    x2 = x.reshape(T, D)
    t2 = tgt.reshape(T, D)
    ones = jnp.ones((1, D_ATTN), F32)
    scale = HEAD_DIM ** -0.5
    gains = jnp.stack([jnp.tile(w["q_norm"], (1, HEADS)) * scale, jnp.tile(w["k_norm"], (1, HEADS)), ones])

    h1, n1, G1, U1 = _ffn_fwd(x2, w["ffn1_norm"], w["wg1"], w["wu1"], w["wd1"], None, tm=TM_FFN_FWD, name="ffn1_fwd")
    u, n2 = _inproj_fwd(h1, w["mix_norm"], w["win"], tm=TM, name="inproj_fwd")
    qkv = _qkv_prep(u, gains, B, S, name="qkv_prep")
    qkv = qkv.reshape(3, N_PATTERNS, T // QBLK, QBLK, D_ATTN)
    o3, lse3 = _attn_fwd(qkv, name="attn_fwd")
    attn, lse = _attn_combine(o3, lse3, B, S, name="attn_combine")
    cv, glu, yconv = _conv_fwd(u, w["conv_w"], w["conv_b"], w["conv_ln_g"], w["conv_ln_b"], B, S, tc=TC, name="conv_fwd")
    h2 = _outproj_fwd(h1, attn, cv, w["wout"], tm=TM, name="outproj_fwd")
    dh3, n3, G2, U2, loss = _ffn_fwd(h2, w["ffn2_norm"], w["wg2"], w["wu2"], w["wd2"], t2, tm=TM_FFN_FWD, name="ffn2_fwd")

    g = {}
    dG, dU, A, dy, dh2, g["ffn2_norm"] = _ffn_bwd_act(dh3, h2, w["ffn2_norm"], G2, U2, w["wg2"], w["wu2"], w["wd2"],
                                                    tm=TM, name="ffn2_bwd_act")
    g["wg2"], g["wu2"], g["wd2"] = _ffn_bwd_w(n3, dy, dG, dU, A, tk=TK, name="ffn2_bwd_w")
    dattn, dcv, g["wout"] = _outproj_bwd(dh2, attn, cv, w["wout"], tm=TM, name="outproj_bwd")
    dyc, cpart = _conv_bwd_norm(dcv, yconv, w["conv_ln_g"], w["conv_ln_b"], tc=TC, name="conv_bwd_norm")
    dca, dcg, dcw = _conv_bwd_taps(dyc, glu, u, w["conv_w"], B, S, tc=TC, name="conv_bwd_taps")
    do3, lseb3, dd3 = _attn_bwd_prep(dattn, attn, lse, B, S, name="attn_bwd_prep")
    nb = T // QBLK
    shp = (N_PATTERNS, nb, QBLK, D_ATTN)
    cur, prev = _attn_bwd(qkv, do3.reshape(shp), lseb3.reshape(shp), dd3.reshape(shp), name="attn_bwd")
    du_qkv, dgains = _attn_grad_combine(cur, prev, u, gains, B, S, name="attn_grad_combine")
    du = jnp.concatenate([du_qkv, dca, dcg], axis=1)
    dh1, g["mix_norm"] = _inproj_bwd_act(du, dh2, h1, w["mix_norm"], w["win"], tm=TM, name="inproj_bwd_act")
    g["win"] = _inproj_bwd_w(n2, du, w["win"].shape[0], tk=TK, name="inproj_bwd_w")
    dG, dU, A, dy, dx, g["ffn1_norm"] = _ffn_bwd_act(dh1, x2, w["ffn1_norm"], G1, U1, w["wg1"], w["wu1"], w["wd1"],
                                                   tm=TM, name="ffn1_bwd_act")
    g["wg1"], g["wu1"], g["wd1"] = _ffn_bwd_w(n1, dy, dG, dU, A, tk=TK, name="ffn1_bwd_w")

    g["q_norm"] = dgains[0].reshape(HEADS, HEAD_DIM).sum(axis=0, keepdims=True) * scale
    g["k_norm"] = dgains[1].reshape(HEADS, HEAD_DIM).sum(axis=0, keepdims=True)
    g["conv_ln_g"] = cpart[0:1]
    g["conv_ln_b"] = cpart[1:2]
    g["conv_b"] = cpart[2:3]
    g["conv_w"] = dcw[:CONV_K]
    return loss, dx.reshape(B, S, D), g


N_CHIPS = 4
N_DEV = 8
ANY = pl.BlockSpec(memory_space=pl.ANY)
VMEM_SPEC = pl.BlockSpec(memory_space=pltpu.VMEM)


def _place():
    x, y, c = lax.axis_index("x"), lax.axis_index("y"), lax.axis_index("c")
    chips = [(1 - x, y), (x, 1 - y), (1 - x, 1 - y)]
    return x, y, c, 2 * x + y, chips, [2 * px + py for px, py in chips]


def _remote(src, dst, send_sem, recv_sem, device):
    return pltpu.make_async_remote_copy(src_ref=src, dst_ref=dst, send_sem=send_sem, recv_sem=recv_sem,
                                        device_id=device, device_id_type=MESH)


def _gather_weights(shards, dtypes, *, name):
    n = len(shards)
    halves = [s.reshape(2, s.shape[0] // 2, s.shape[1]) for s in shards]

    def body(*refs):
        ins, outs, vms = refs[:n], refs[n:2 * n], refs[2 * n:3 * n]
        send_sems, recv_sems, loc_sems = refs[3 * n:]
        x, y, c, me, chips, cidx = _place()
        sibling = (x, y, 1 - c)
        for a in range(n):
            vms[a][...] = ins[a][...].astype(dtypes[a])
        local, first, passed = [], [], []
        for a in range(n):
            lc = pltpu.make_async_copy(vms[a], outs[a].at[me], loc_sems.at[a])
            lc.start()
            local.append(lc)
            for j, chip in enumerate(chips):
                cp = _remote(vms[a].at[c], outs[a].at[me, c], send_sems.at[6 * a + j], recv_sems.at[6 * a + j], (*chip, c))
                cp.start()
                first.append(cp)
        for a in range(n):
            for j, chip in enumerate(chips):
                land = outs[a].at[cidx[j], c]
                _remote(land, land, send_sems.at[6 * a + j], recv_sems.at[6 * a + j], (*chip, c)).wait_recv()
                fw = _remote(land, land, send_sems.at[6 * a + 3 + j], recv_sems.at[6 * a + 3 + j], sibling)
                fw.start()
                passed.append(fw)
        for a in range(n):
            for j in range(3):
                land = outs[a].at[cidx[j], 1 - c]
                _remote(land, land, send_sems.at[6 * a + 3 + j], recv_sems.at[6 * a + 3 + j], sibling).wait_recv()
        for cp in first + passed:
            cp.wait_send()
        for lc in local:
            lc.wait()

    outs = pl.pallas_call(
        body, in_specs=[VMEM_SPEC] * n, out_specs=[ANY] * n,
        out_shape=[jax.ShapeDtypeStruct((N_CHIPS,) + h.shape, dt) for h, dt in zip(halves, dtypes)],
        scratch_shapes=[pltpu.VMEM(h.shape, dt) for h, dt in zip(halves, dtypes)]
        + [pltpu.SemaphoreType.DMA((6 * n,)), pltpu.SemaphoreType.DMA((6 * n,)), pltpu.SemaphoreType.DMA((n,))],
        compiler_params=pltpu.CompilerParams(vmem_limit_bytes=VMEM_LIMIT), name=name)(*halves)
    return [o.reshape((N_CHIPS,) + s.shape) for o, s in zip(outs, shards)]


def _exchange_halves(grads, *, name):
    n = len(grads)

    def body(*refs):
        ins, outs = refs[:n], refs[n:2 * n]
        send_sems, recv_sems = refs[2 * n:]
        x, y, c, me, chips, cidx = _place()
        cps = []
        for a in range(n):
            src = ins[a].at[pl.ds(0, ins[a].shape[0]), 1 - c]
            cp = _remote(src, outs[a], send_sems.at[a], recv_sems.at[a], (x, y, 1 - c))
            cp.start()
            cps.append(cp)
        for cp in cps:
            cp.wait()

    return pl.pallas_call(
        body, in_specs=[ANY] * n, out_specs=[ANY] * n,
        out_shape=[jax.ShapeDtypeStruct((g.shape[0],) + g.shape[2:], F32) for g in grads],
        scratch_shapes=[pltpu.SemaphoreType.DMA((n,)), pltpu.SemaphoreType.DMA((n,))], name=name)(*grads)


def _row_block(rows):
    for cand in (256, 176, 128, 64, 32, 16, 8):
        if rows % cand == 0:
            return cand
    return rows


def _add_own_half(g, r, sel, *, name):
    ns, _, rh, cdim = g.shape
    tr = _row_block(rh)

    def body(s_ref, gk_ref, rk_ref, gs_ref, rs_ref, keep_ref, send_ref):
        keep_ref[...] = gk_ref[...] + rk_ref[...]
        send_ref[...] = (gs_ref[...] + rs_ref[...]).astype(BF16)

    def g_spec(off):
        return pl.BlockSpec((None, None, tr, cdim), lambda k, i, s: (s[1 + off + k], s[0], i, 0))

    def r_spec(off):
        return pl.BlockSpec((None, tr, cdim), lambda k, i, s: (s[1 + off + k], i, 0))

    out = pl.BlockSpec((None, tr, cdim), lambda k, i, s: (k, i, 0))
    return pl.pallas_call(
        body,
        grid_spec=pltpu.PrefetchScalarGridSpec(
            num_scalar_prefetch=1, grid=(2, rh // tr),
            in_specs=[g_spec(0), r_spec(0), g_spec(2), r_spec(2)], out_specs=[out, out]),
        out_shape=[jax.ShapeDtypeStruct((2, rh, cdim), F32), jax.ShapeDtypeStruct((2, rh, cdim), BF16)],
        compiler_params=_params("arbitrary", "arbitrary"), name=name)(sel, g, r, g, r)


def _swap(arrays, stage, *, name):
    n = len(arrays)

    def body(*refs):
        ins, outs = refs[:n], refs[n:2 * n]
        send_sems, recv_sems = refs[2 * n:]
        x, y, c = lax.axis_index("x"), lax.axis_index("y"), lax.axis_index("c")
        along_x = (1 - c) if stage == 1 else c
        peer = (x + along_x * (1 - 2 * x), y + (1 - along_x) * (1 - 2 * y), c)
        cps = []
        for a in range(n):
            cp = _remote(ins[a], outs[a], send_sems.at[a], recv_sems.at[a], peer)
            cp.start()
            cps.append(cp)
        for cp in cps:
            cp.wait()

    return pl.pallas_call(
        body, in_specs=[ANY] * n, out_specs=[ANY] * n,
        out_shape=[jax.ShapeDtypeStruct(v.shape, v.dtype) for v in arrays],
        scratch_shapes=[pltpu.SemaphoreType.DMA((n,)), pltpu.SemaphoreType.DMA((n,))], name=name)(*arrays)


def _add_stage1(keep, got, *, name):
    _, rh, cdim = keep.shape
    tr = _row_block(rh)

    def body(k_ref, g_ref, keep_ref, send_ref):
        keep_ref[...] = k_ref[0] + g_ref[0].astype(F32)
        send_ref[...] = (k_ref[1] + g_ref[1].astype(F32)).astype(BF16)

    blk2 = pl.BlockSpec((2, tr, cdim), lambda i: (0, i, 0))
    blk = pl.BlockSpec((tr, cdim), lambda i: (i, 0))
    return pl.pallas_call(
        body, grid=(rh // tr,), in_specs=[blk2, blk2], out_specs=[blk, blk],
        out_shape=[jax.ShapeDtypeStruct((rh, cdim), F32), jax.ShapeDtypeStruct((rh, cdim), BF16)],
        compiler_params=_params("arbitrary"), name=name)(keep, got)


def _add_stage2(keep, got, sel, *, name):
    rh, cdim = keep.shape
    tr = _row_block(rh)

    def body(s_ref, k_ref, g_ref, o_ref):
        o_ref[...] = k_ref[...] + g_ref[...].astype(F32)

    blk = pl.BlockSpec((tr, cdim), lambda i, s: (i, 0))
    return pl.pallas_call(
        body,
        grid_spec=pltpu.PrefetchScalarGridSpec(
            num_scalar_prefetch=1, grid=(rh // tr,), in_specs=[blk, blk],
            out_specs=pl.BlockSpec((None, tr, cdim), lambda i, s: (s[0], i, 0))),
        out_shape=jax.ShapeDtypeStruct((2, rh, cdim), F32),
        compiler_params=_params("arbitrary"), name=name)(sel, keep, got)


def _join_halves(halves, *, name):
    n = len(halves)

    def body(*refs):
        outs = refs[n:2 * n]
        send_sems, recv_sems = refs[2 * n:]
        x, y, c = lax.axis_index("x"), lax.axis_index("y"), lax.axis_index("c")
        cps = []
        for a in range(n):
            mine = outs[a].at[c]
            cp = _remote(mine, mine, send_sems.at[a], recv_sems.at[a], (x, y, 1 - c))
            cp.start()
            cps.append(cp)
        for a in range(n):
            land = outs[a].at[1 - c]
            _remote(land, land, send_sems.at[a], recv_sems.at[a], (x, y, 1 - c)).wait_recv()
        for cp in cps:
            cp.wait_send()

    return pl.pallas_call(
        body, in_specs=[ANY] * n, out_specs=[ANY] * n,
        out_shape=[jax.ShapeDtypeStruct(h.shape, F32) for h in halves],
        input_output_aliases={a: a for a in range(n)},
        scratch_shapes=[pltpu.SemaphoreType.DMA((n,)), pltpu.SemaphoreType.DMA((n,))],
        name=name)(*halves)


def _slot_order():
    x, y, c = lax.axis_index("x"), lax.axis_index("y"), lax.axis_index("c")
    own, flip_x, flip_y, both = 2 * x + y, 2 * (1 - x) + y, 2 * x + 1 - y, 2 * (1 - x) + 1 - y
    first = jnp.where(c == 0, flip_x, flip_y)
    second = jnp.where(c == 0, flip_y, flip_x)
    return jnp.stack([c, own, second, first, both]).astype(jnp.int32)


def _reduce_scatter(grads):
    sel = _slot_order()
    views = [g.reshape(N_CHIPS, 2, g.shape[1] // 2, g.shape[2]) for g in grads]
    got = _exchange_halves(views, name="rs_exchange_halves")
    keep, send = zip(*[_add_own_half(v, r, sel, name=f"rs_add_half_{a}") for a, (v, r) in enumerate(zip(views, got))])
    got = _swap(list(send), 1, name="rs_swap_first_axis")
    keep, send = zip(*[_add_stage1(k, r, name=f"rs_add_first_{a}") for a, (k, r) in enumerate(zip(keep, got))])
    got = _swap(list(send), 2, name="rs_swap_second_axis")
    halves = [_add_stage2(k, r, sel, name=f"rs_add_second_{a}") for a, (k, r) in enumerate(zip(keep, got))]
    full = _join_halves(halves, name="rs_join_halves")
    return [f.reshape(g.shape[1], g.shape[2]) for f, g in zip(full, grads)]


def _allreduce_small(pack, *, name):
    rows = pack.shape[0]

    def body(p_ref, o_ref, buf_ref, send_sems, recv_sems):
        x, y, c = lax.axis_index("x"), lax.axis_index("y"), lax.axis_index("c")
        me = 4 * x + 2 * y + c
        buf_ref[me] = p_ref[...]
        cps = []
        for k in range(1, N_DEV):
            peer = tuple(1 - v if (k >> s) & 1 else v for v, s in ((x, 2), (y, 1), (c, 0)))
            cp = _remote(p_ref, buf_ref.at[me], send_sems.at[k - 1], recv_sems.at[k - 1], peer)
            cp.start()
            cps.append(cp)
        for k in range(1, N_DEV):
            src = 4 * (x ^ ((k >> 2) & 1)) + 2 * (y ^ ((k >> 1) & 1)) + (c ^ (k & 1))
            land = buf_ref.at[src]
            _remote(land, land, send_sems.at[k - 1], recv_sems.at[k - 1], (x, y, c)).wait_recv()
        acc = buf_ref[0]
        for d in range(1, N_DEV):
            acc = acc + buf_ref[d]
        o_ref[...] = acc
        for cp in cps:
            cp.wait_send()

    return pl.pallas_call(
        body, in_specs=[VMEM_SPEC], out_specs=VMEM_SPEC, out_shape=jax.ShapeDtypeStruct(pack.shape, F32),
        scratch_shapes=[pltpu.VMEM((N_DEV, rows, LANES), F32), pltpu.SemaphoreType.DMA((N_DEV - 1,)),
                        pltpu.SemaphoreType.DMA((N_DEV - 1,))], name=name)(pack)


SMALL = ("ffn1_norm", "mix_norm", "q_norm", "k_norm", "conv_b", "conv_ln_g", "conv_ln_b", "ffn2_norm", "conv_w")
BIG = ("ffn1_w_gate", "ffn1_w_up", "ffn1_w_down", "w_in", "w_out", "ffn2_w_gate", "ffn2_w_up", "ffn2_w_down")
TRANSPOSED = ("ffn1_w_gate", "ffn1_w_up", "ffn2_w_gate", "ffn2_w_up")
WEIGHTS = ("ffn1_norm", "ffn1_w_gate", "ffn1_w_up", "ffn1_w_down", "mix_norm", "w_in", "q_norm", "k_norm",
           "conv_w", "conv_b", "conv_ln_g", "conv_ln_b", "w_out", "ffn2_norm", "ffn2_w_gate", "ffn2_w_up",
           "ffn2_w_down")
SUBLANES = 8


def _pack(parts):
    rows = []
    for p in parts:
        flat = p.reshape(-1)
        tile = SUBLANES * LANES
        padded = -(-flat.shape[0] // tile) * tile
        rows.append(jnp.pad(flat, (0, padded - flat.shape[0])).reshape(-1, LANES))
    return jnp.concatenate(rows, axis=0)


def _unpack(pack, shapes):
    out, row = [], 0
    for shp in shapes:
        size = shp[0] * shp[1]
        tile = SUBLANES * LANES
        nrows = -(-size // tile) * SUBLANES
        out.append(pack[row:row + nrows].reshape(-1)[:size].reshape(shp))
        row += nrows
    return out


def kernel(x, ffn1_norm, ffn1_w_gate, ffn1_w_up, ffn1_w_down, mix_norm, w_in, q_norm, k_norm, conv_w, conv_b, conv_ln_g, conv_ln_b, w_out, ffn2_norm, ffn2_w_gate, ffn2_w_up, ffn2_w_down, loss_target, m_ffn1_norm, m_ffn1_w_gate, m_ffn1_w_up, m_ffn1_w_down, m_mix_norm, m_w_in, m_q_norm, m_k_norm, m_conv_w, m_conv_b, m_conv_ln_g, m_conv_ln_b, m_w_out, m_ffn2_norm, m_ffn2_w_gate, m_ffn2_w_up, m_ffn2_w_down, v_ffn1_norm, v_ffn1_w_gate, v_ffn1_w_up, v_ffn1_w_down, v_mix_norm, v_w_in, v_q_norm, v_k_norm, v_conv_w, v_conv_b, v_conv_ln_g, v_conv_ln_b, v_w_out, v_ffn2_norm, v_ffn2_w_gate, v_ffn2_w_up, v_ffn2_w_down):
    wts = dict(ffn1_norm=ffn1_norm, ffn1_w_gate=ffn1_w_gate[0], ffn1_w_up=ffn1_w_up[0], ffn1_w_down=ffn1_w_down[0],
               mix_norm=mix_norm, w_in=w_in[0], q_norm=q_norm, k_norm=k_norm, conv_w=conv_w[0], conv_b=conv_b,
               conv_ln_g=conv_ln_g, conv_ln_b=conv_ln_b, w_out=w_out[0], ffn2_norm=ffn2_norm,
               ffn2_w_gate=ffn2_w_gate[0], ffn2_w_up=ffn2_w_up[0], ffn2_w_down=ffn2_w_down[0])
    mom = dict(ffn1_norm=m_ffn1_norm, ffn1_w_gate=m_ffn1_w_gate[0], ffn1_w_up=m_ffn1_w_up[0], ffn1_w_down=m_ffn1_w_down[0],
               mix_norm=m_mix_norm, w_in=m_w_in[0], q_norm=m_q_norm, k_norm=m_k_norm, conv_w=m_conv_w[0], conv_b=m_conv_b,
               conv_ln_g=m_conv_ln_g, conv_ln_b=m_conv_ln_b, w_out=m_w_out[0], ffn2_norm=m_ffn2_norm,
               ffn2_w_gate=m_ffn2_w_gate[0], ffn2_w_up=m_ffn2_w_up[0], ffn2_w_down=m_ffn2_w_down[0])
    var = dict(ffn1_norm=v_ffn1_norm, ffn1_w_gate=v_ffn1_w_gate[0], ffn1_w_up=v_ffn1_w_up[0], ffn1_w_down=v_ffn1_w_down[0],
               mix_norm=v_mix_norm, w_in=v_w_in[0], q_norm=v_q_norm, k_norm=v_k_norm, conv_w=v_conv_w[0], conv_b=v_conv_b,
               conv_ln_g=v_conv_ln_g, conv_ln_b=v_conv_ln_b, w_out=v_w_out[0], ffn2_norm=v_ffn2_norm,
               ffn2_w_gate=v_ffn2_w_gate[0], ffn2_w_up=v_ffn2_w_up[0], ffn2_w_down=v_ffn2_w_down[0])
    chip = 2 * lax.axis_index("x") + lax.axis_index("y")
    for src in (wts, mom, var):
        for n in TRANSPOSED:
            src[n] = src[n].T

    wg1, wu1, wd1 = _gather_weights([wts["ffn1_w_gate"], wts["ffn1_w_up"], wts["ffn1_w_down"]], [BF16] * 3,
                                    name="gather_ffn1")
    taps = jnp.pad(wts["conv_w"], ((0, 1), (0, 0)))
    win, wout, taps4 = _gather_weights([wts["w_in"], wts["w_out"], taps], [BF16, BF16, F32], name="gather_mix")
    wg2, wu2, wd2 = _gather_weights([wts["ffn2_w_gate"], wts["ffn2_w_up"], wts["ffn2_w_down"]], [BF16] * 3,
                                    name="gather_ffn2")
    conv_full = taps4.transpose(1, 0, 2).reshape(CONV_K + 1, D_CONV)[:CONV_K]
    w = dict(ffn1_norm=ffn1_norm, mix_norm=mix_norm, ffn2_norm=ffn2_norm, q_norm=q_norm, k_norm=k_norm,
             conv_w=conv_full, conv_b=conv_b, conv_ln_g=conv_ln_g, conv_ln_b=conv_ln_b,
             wg1=wg1, wu1=wu1, wd1=wd1, wg2=wg2, wu2=wu2, wd2=wd2, win=win,
             wout=wout.reshape(N_CHIPS * wout.shape[1], wout.shape[2]))

    loss_part, grad_x, g = _local_step(x, loss_target, w)

    big_parts = [g["wg1"], g["wu1"], g["wd1"], g["win"], g["wout"].reshape(wout.shape), g["wg2"], g["wu2"], g["wd2"]]
    big_grads = dict(zip(BIG, _reduce_scatter(big_parts)))

    small_shapes = [g[n].shape for n in SMALL] + [(SUBLANES, LANES)]
    red = _allreduce_small(_pack([g[n] for n in SMALL] + [loss_part]), name="allreduce_small")
    small = dict(zip(SMALL + ("loss",), _unpack(red, small_shapes)))
    loss = small["loss"][0, 0]
    small["conv_w"] = lax.dynamic_slice_in_dim(small["conv_w"], chip * LANES, LANES, axis=1)

    grads, delta, new_m, new_v = {}, {}, {}, {}
    for n in BIG:
        grads[n] = big_grads[n]
        delta[n], new_m[n], new_v[n] = _adamw(wts[n], grads[n], mom[n], var[n], name=f"adamw_{n}")
    shapes = [wts[n].shape for n in SMALL]
    packs = [_pack([src[n] for n in SMALL]) for src in (wts, small, mom, var)]
    outs = _adamw(*packs, name="adamw_small")
    for dst, pk in zip((delta, new_m, new_v), outs):
        dst.update(zip(SMALL, _unpack(pk, shapes)))
    for n in SMALL:
        grads[n] = small[n]

    def shaped(d, n):
        v = d[n].T if n in TRANSPOSED else d[n]
        return v.reshape((1,) + v.shape) if n in BIG or n == "conv_w" else v

    return (loss, grad_x, *[shaped(grads, n) for n in WEIGHTS], *[shaped(delta, n) for n in WEIGHTS],
            *[shaped(new_m, n) for n in WEIGHTS], *[shaped(new_v, n) for n in WEIGHTS])
```

```python
import functools

import jax
import jax.numpy as jnp
from jax import lax
from jax.experimental import pallas as pl
from jax.experimental.pallas import tpu as pltpu

F32 = jnp.float32
BF16 = jnp.bfloat16

EPS = 1e-6
HEADS = 8
HEAD_DIM = 64
D_ATTN = HEADS * HEAD_DIM
D_CONV = 512
CONV_K = 31
QBLK = 128
N_PATTERNS = 3
DILATIONS = (1, 4, 16)
LANES = 128
NEG = -1e30

ADAM_LR = 0.001
ADAM_B1 = 0.9
ADAM_B2 = 0.999
ADAM_EPS = 1e-08
ADAM_WD = 0.01
ADAM_STEP = 10

VMEM_LIMIT = 56 * 1024 * 1024
MESH = pl.DeviceIdType.MESH

NT_DIMS = (((1,), (1,)), ((), ()))
TN_DIMS = (((0,), (0,)), ((), ()))


def _params(*sem):
    return pltpu.CompilerParams(dimension_semantics=sem, vmem_limit_bytes=VMEM_LIMIT)


def _dot(a, b):
    return jnp.dot(a, b, preferred_element_type=F32)


def _dot_nt(a, b):
    return lax.dot_general(a, b, NT_DIMS, preferred_element_type=F32)


def _dot_tn(a, b):
    return lax.dot_general(a, b, TN_DIMS, preferred_element_type=F32)


def _sigmoid(x):
    return 1.0 / (1.0 + jnp.exp(-x))


def _seg_mean(v, e_ref, width):
    hi = v.astype(BF16)
    lo = (v - hi.astype(F32)).astype(BF16)
    e = e_ref[...]
    return (_dot(hi, e) + _dot(lo, e)) * (1.0 / width)


def _seg_matrix(n):
    i = jnp.arange(n)
    return (i[:, None] // HEAD_DIM == i[None, :] // HEAD_DIM).astype(BF16)


ANY = pl.BlockSpec(memory_space=pl.ANY)
DMA_SEMS = pltpu.SemaphoreType.DMA


class _Phase:
    def __init__(self, ins, outs, aliases, nsem, copies):
        self.ins, self.outs, self.aliases, self.nsem, self.copies = list(ins), list(outs), dict(aliases), nsem, copies

    def start(self, in_refs, out_refs, send_sems, recv_sems):
        for cp in self.copies(in_refs, out_refs, send_sems, recv_sems)[0]:
            cp.start()

    def finish(self, in_refs, out_refs, send_sems, recv_sems):
        starts, arrivals = self.copies(in_refs, out_refs, send_sems, recv_sems)
        for cp in arrivals:
            cp.wait_recv()
        for cp in starts:
            cp.wait_send()


def _run_phase(phase, *, name):
    n_in, n_out = len(phase.ins), len(phase.outs)

    def body(*refs):
        ins, outs = refs[:n_in], refs[n_in:n_in + n_out]
        send_sems, recv_sems = refs[n_in + n_out:]
        phase.start(ins, outs, send_sems, recv_sems)
        phase.finish(ins, outs, send_sems, recv_sems)

    return pl.pallas_call(
        body, in_specs=[ANY] * n_in, out_specs=[ANY] * n_out, out_shape=phase.outs,
        input_output_aliases=phase.aliases,
        scratch_shapes=[DMA_SEMS((phase.nsem,)), DMA_SEMS((phase.nsem,))], name=name)(*phase.ins)


def _call(body, *, grid, in_specs, out_specs, out_shape, scratch_shapes=(), sem, name, args, phase=None):
    in_specs, out_specs, out_shape = list(in_specs), list(out_specs), list(out_shape)
    scratch_shapes = list(scratch_shapes)
    if phase is None:
        return pl.pallas_call(body, grid=grid, in_specs=in_specs, out_specs=out_specs, out_shape=out_shape,
                              scratch_shapes=scratch_shapes, compiler_params=_params(*sem), name=name)(*args)
    n_in, n_out, n_scr = len(in_specs), len(out_specs), len(scratch_shapes)
    p_in, p_out = len(phase.ins), len(phase.outs)

    def hosted(*refs):
        ins, pins = refs[:n_in], refs[n_in:n_in + p_in]
        o0 = n_in + p_in
        outs, pouts = refs[o0:o0 + n_out], refs[o0 + n_out:o0 + n_out + p_out]
        s0 = o0 + n_out + p_out
        scr = refs[s0:s0 + n_scr]
        send_sems, recv_sems = refs[s0 + n_scr:]
        ids = [pl.program_id(d) for d in range(len(grid))]
        first = functools.reduce(jnp.logical_and, [i == 0 for i in ids])
        last = functools.reduce(jnp.logical_and, [i == n - 1 for i, n in zip(ids, grid)])

        @pl.when(first)
        def _():
            phase.start(pins, pouts, send_sems, recv_sems)

        body(*ins, *outs, *scr)

        @pl.when(last)
        def _():
            phase.finish(pins, pouts, send_sems, recv_sems)

    res = pl.pallas_call(
        hosted, grid=grid, in_specs=in_specs + [ANY] * p_in, out_specs=out_specs + [ANY] * p_out,
        out_shape=out_shape + phase.outs,
        input_output_aliases={n_in + i: n_out + o for i, o in phase.aliases.items()},
        scratch_shapes=scratch_shapes + [DMA_SEMS((phase.nsem,)), DMA_SEMS((phase.nsem,))],
        compiler_params=_params(*sem), name=name)(*args, *phase.ins)
    return res[:n_out], res[n_out:]


def _ffn_fwd(x, gain, wg, wu, wd, tgt, *, tm, name, phase=None):
    T, D = x.shape
    NS, Fs, _ = wg.shape
    with_loss = tgt is not None

    def body(*refs):
        if with_loss:
            x_ref, g_ref, wg_ref, wu_ref, wd_ref, t_ref, h_ref, n_ref, G_ref, U_ref, loss_ref, acc_ref = refs
        else:
            x_ref, g_ref, wg_ref, wu_ref, wd_ref, h_ref, n_ref, G_ref, U_ref, acc_ref = refs
        i = pl.program_id(0)
        j = pl.program_id(1)

        @pl.when(j == 0)
        def _():
            xv = x_ref[...]
            r = lax.rsqrt(jnp.mean(xv * xv, axis=-1, keepdims=True) + EPS)
            n_ref[...] = (xv * r * g_ref[...]).astype(BF16)
            acc_ref[...] = jnp.zeros_like(acc_ref)

        n = n_ref[...]
        G = _dot_nt(n, wg_ref[...])
        U = _dot_nt(n, wu_ref[...])
        G_ref[...] = G.astype(BF16)
        U_ref[...] = U.astype(BF16)
        A = (G * _sigmoid(G) * U).astype(BF16)
        acc_ref[...] += _dot(A, wd_ref[...])

        @pl.when(j == NS - 1)
        def _():
            h = x_ref[...] + 0.5 * acc_ref[...]
            if with_loss:
                e = h - t_ref[...]
                h_ref[...] = e * (1.0 / D)

                @pl.when(i == 0)
                def _():
                    loss_ref[...] = jnp.zeros_like(loss_ref)

                loss_ref[...] += jnp.sum(e * e) * (0.5 / D)
            else:
                h_ref[...] = h

    tok = pl.BlockSpec((tm, D), lambda i, j: (i, 0))
    in_specs = [tok, pl.BlockSpec((1, D), lambda i, j: (0, 0)),
                pl.BlockSpec((None, Fs, D), lambda i, j: (j, 0, 0)),
                pl.BlockSpec((None, Fs, D), lambda i, j: (j, 0, 0)),
                pl.BlockSpec((None, Fs, D), lambda i, j: (j, 0, 0))]
    args = [x, gain, wg, wu, wd]
    act = pl.BlockSpec((None, tm, Fs), lambda i, j: (j, i, 0))
    out_shape = [jax.ShapeDtypeStruct((T, D), F32), jax.ShapeDtypeStruct((T, D), BF16),
                 jax.ShapeDtypeStruct((NS, T, Fs), BF16), jax.ShapeDtypeStruct((NS, T, Fs), BF16)]
    out_specs = [tok, tok, act, act]
    if with_loss:
        in_specs.append(tok)
        args.append(tgt)
        out_shape.append(jax.ShapeDtypeStruct((8, LANES), F32))
        out_specs.append(pl.BlockSpec((8, LANES), lambda i, j: (0, 0)))
    return _call(body, grid=(T // tm, NS), in_specs=in_specs, out_specs=out_specs, out_shape=out_shape,
                 scratch_shapes=[pltpu.VMEM((tm, D), F32)], sem=("arbitrary", "arbitrary"), name=name,
                 args=args, phase=phase)


def _rms_bwd(xv, gain, dn):
    r = lax.rsqrt(jnp.mean(xv * xv, axis=-1, keepdims=True) + EPS)
    xhat = xv * r
    dxh = dn * gain
    dx = r * (dxh - xhat * jnp.mean(dxh * xhat, axis=-1, keepdims=True))
    dg = jnp.sum(dn * xhat, axis=0, keepdims=True)
    return dx, dg


def _ffn_bwd_act(dh, x, gain, G, U, wg, wu, wd, *, tm, name):
    T, D = x.shape
    NS, Fs, _ = wg.shape

    def body(dh_ref, x_ref, g_ref, G_ref, U_ref, wg_ref, wu_ref, wd_ref,
             dG_ref, dU_ref, A_ref, dy_ref, dx_ref, dg_ref, acc_ref):
        i = pl.program_id(0)
        j = pl.program_id(1)

        @pl.when(j == 0)
        def _():
            dy_ref[...] = (0.5 * dh_ref[...]).astype(BF16)
            acc_ref[...] = jnp.zeros_like(acc_ref)

        @pl.when((i == 0) & (j == 0))
        def _():
            dg_ref[...] = jnp.zeros_like(dg_ref)

        Gv = G_ref[...].astype(F32)
        Uv = U_ref[...].astype(F32)
        sig = _sigmoid(Gv)
        s = Gv * sig
        dA = _dot_nt(dy_ref[...], wd_ref[...])
        dG = (dA * Uv * (sig * (1.0 + Gv * (1.0 - sig)))).astype(BF16)
        dU = (dA * s).astype(BF16)
        dG_ref[...] = dG
        dU_ref[...] = dU
        A_ref[...] = (s * Uv).astype(BF16)
        acc_ref[...] += _dot(dG, wg_ref[...]) + _dot(dU, wu_ref[...])

        @pl.when(j == NS - 1)
        def _():
            dx, dg = _rms_bwd(x_ref[...], g_ref[...], acc_ref[...])
            dx_ref[...] = dh_ref[...] + dx
            dg_ref[...] += dg

    tok = pl.BlockSpec((tm, D), lambda i, j: (i, 0))
    act = pl.BlockSpec((None, tm, Fs), lambda i, j: (j, i, 0))
    vec = pl.BlockSpec((1, D), lambda i, j: (0, 0))
    return pl.pallas_call(
        body, grid=(T // tm, NS),
        in_specs=[tok, tok, vec, act, act,
                  pl.BlockSpec((None, Fs, D), lambda i, j: (j, 0, 0)),
                  pl.BlockSpec((None, Fs, D), lambda i, j: (j, 0, 0)),
                  pl.BlockSpec((None, Fs, D), lambda i, j: (j, 0, 0))],
        out_specs=[act, act, act, tok, tok, vec],
        out_shape=[jax.ShapeDtypeStruct((NS, T, Fs), BF16)] * 3
        + [jax.ShapeDtypeStruct((T, D), BF16), jax.ShapeDtypeStruct((T, D), F32),
           jax.ShapeDtypeStruct((1, D), F32)],
        scratch_shapes=[pltpu.VMEM((tm, D), F32)],
        compiler_params=_params("arbitrary", "arbitrary"), name=name)(dh, x, gain, G, U, wg, wu, wd)


def _ffn_bwd_w(n, dy, dG, dU, A, *, tk, name):
    T, D = n.shape
    NS, _, Fs = dG.shape

    def body(n_ref, dy_ref, dG_ref, dU_ref, A_ref, wg_ref, wu_ref, wd_ref):
        @pl.when(pl.program_id(1) == 0)
        def _():
            wg_ref[...] = jnp.zeros_like(wg_ref)
            wu_ref[...] = jnp.zeros_like(wu_ref)
            wd_ref[...] = jnp.zeros_like(wd_ref)

        nv = n_ref[...]
        wg_ref[...] += _dot_tn(dG_ref[...], nv)
        wu_ref[...] += _dot_tn(dU_ref[...], nv)
        wd_ref[...] += _dot_tn(A_ref[...], dy_ref[...])

    tok = pl.BlockSpec((tk, D), lambda j, k: (k, 0))
    act = pl.BlockSpec((None, tk, Fs), lambda j, k: (j, k, 0))
    return pl.pallas_call(
        body, grid=(NS, T // tk), in_specs=[tok, tok, act, act, act],
        out_specs=[pl.BlockSpec((None, Fs, D), lambda j, k: (j, 0, 0))] * 3,
        out_shape=[jax.ShapeDtypeStruct((NS, Fs, D), F32)] * 3,
        compiler_params=_params("arbitrary", "arbitrary"), name=name)(n, dy, dG, dU, A)


def _inproj_fwd(h, gain, win, *, tm, name, phase=None):
    T, D = h.shape
    NS, _, Cs = win.shape

    def body(h_ref, g_ref, w_ref, u_ref, n_ref):
        @pl.when(pl.program_id(1) == 0)
        def _():
            xv = h_ref[...]
            r = lax.rsqrt(jnp.mean(xv * xv, axis=-1, keepdims=True) + EPS)
            n_ref[...] = (xv * r * g_ref[...]).astype(BF16)

        u_ref[...] = _dot(n_ref[...], w_ref[...])

    tok = pl.BlockSpec((tm, D), lambda i, j: (i, 0))
    return _call(
        body, grid=(T // tm, NS),
        in_specs=[tok, pl.BlockSpec((1, D), lambda i, j: (0, 0)),
                  pl.BlockSpec((None, D, Cs), lambda i, j: (j, 0, 0))],
        out_specs=[pl.BlockSpec((tm, Cs), lambda i, j: (i, j)), tok],
        out_shape=[jax.ShapeDtypeStruct((T, NS * Cs), F32), jax.ShapeDtypeStruct((T, D), BF16)],
        sem=("arbitrary", "arbitrary"), name=name, args=(h, gain, win), phase=phase)


def _inproj_bwd_act(du, dh, h, gain, win, *, tm, name, phase=None):
    T, D = h.shape
    NS, _, Cs = win.shape

    def body(du_ref, dh_ref, h_ref, g_ref, w_ref, dx_ref, dg_ref, acc_ref):
        i = pl.program_id(0)
        j = pl.program_id(1)

        @pl.when(j == 0)
        def _():
            acc_ref[...] = jnp.zeros_like(acc_ref)

        @pl.when((i == 0) & (j == 0))
        def _():
            dg_ref[...] = jnp.zeros_like(dg_ref)

        acc_ref[...] += _dot_nt(du_ref[...], w_ref[...])

        @pl.when(j == NS - 1)
        def _():
            dx, dg = _rms_bwd(h_ref[...], g_ref[...], acc_ref[...])
            dx_ref[...] = dh_ref[...] + dx
            dg_ref[...] += dg

    tok = pl.BlockSpec((tm, D), lambda i, j: (i, 0))
    vec = pl.BlockSpec((1, D), lambda i, j: (0, 0))
    return _call(
        body, grid=(T // tm, NS),
        in_specs=[pl.BlockSpec((tm, Cs), lambda i, j: (i, j)), tok, tok, vec,
                  pl.BlockSpec((None, D, Cs), lambda i, j: (j, 0, 0))],
        out_specs=[tok, vec],
        out_shape=[jax.ShapeDtypeStruct((T, D), F32), jax.ShapeDtypeStruct((1, D), F32)],
        scratch_shapes=[pltpu.VMEM((tm, D), F32)],
        sem=("arbitrary", "arbitrary"), name=name, args=(du, dh, h, gain, win), phase=phase)


def _inproj_bwd_w(n, du, ns, *, tk, name):
    T, D = n.shape
    Cs = du.shape[1] // ns

    def body(n_ref, du_ref, w_ref):
        @pl.when(pl.program_id(1) == 0)
        def _():
            w_ref[...] = jnp.zeros_like(w_ref)

        w_ref[...] += _dot_tn(n_ref[...], du_ref[...])

    return pl.pallas_call(
        body, grid=(ns, T // tk),
        in_specs=[pl.BlockSpec((tk, D), lambda j, k: (k, 0)), pl.BlockSpec((tk, Cs), lambda j, k: (k, j))],
        out_specs=pl.BlockSpec((None, D, Cs), lambda j, k: (j, 0, 0)),
        out_shape=jax.ShapeDtypeStruct((ns, D, Cs), F32),
        compiler_params=_params("arbitrary", "arbitrary"), name=name)(n, du)


def _permute_out(src_ref, out_ref, cast):
    S = src_ref.shape[1]
    for p, d in enumerate(DILATIONS):
        L = S // d
        for cc in range(4):
            cols = slice(cc * LANES, (cc + 1) * LANES)
            if d == 1:
                out_ref[p, :, cols] = src_ref[cc].astype(cast)
            else:
                for r in range(d):
                    out_ref[p, r * L:(r + 1) * L, cols] = src_ref[cc, pl.ds(r, L, stride=d), :].astype(cast)


def _unpermute_in(get_block, dst_ref, p, S):
    d = DILATIONS[p]
    L = S // d
    if d == 1:
        dst_ref[...] = get_block(0, S)
    else:
        for r in range(d):
            dst_ref[pl.ds(r, L, stride=d), :] = get_block(r * L, L)


def _qkv_prep(u, gains, B, S, *, name):
    emat = _seg_matrix(D_ATTN)

    def body(u_ref, g_ref, e_ref, out_ref, scr_ref):
        c = pl.program_id(1)
        xv = u_ref[...]
        ms = _seg_mean(xv * xv, e_ref, HEAD_DIM)
        r = jnp.where(c < 2, lax.rsqrt(ms + EPS), 1.0)
        yv = xv * r * g_ref[...]
        for cc in range(4):
            scr_ref[cc] = yv[:, cc * LANES:(cc + 1) * LANES]
        _permute_out(scr_ref, out_ref, BF16)

    return pl.pallas_call(
        body, grid=(B, 3),
        in_specs=[pl.BlockSpec((S, D_ATTN), lambda b, c: (b, c)),
                  pl.BlockSpec((None, 1, D_ATTN), lambda b, c: (c, 0, 0)),
                  pl.BlockSpec((D_ATTN, D_ATTN), lambda b, c: (0, 0))],
        out_specs=pl.BlockSpec((None, N_PATTERNS, None, S, D_ATTN), lambda b, c: (c, 0, b, 0, 0)),
        out_shape=jax.ShapeDtypeStruct((3, N_PATTERNS, B, S, D_ATTN), BF16),
        scratch_shapes=[pltpu.VMEM((4, S, LANES), F32)],
        compiler_params=_params("arbitrary", "arbitrary"), name=name)(u, gains, emat)


def _band_mask(p, b):
    nblk = jnp.right_shift(16, 2 * p)
    has_prev = jnp.bitwise_and(b, nblk - 1) != 0
    qi = lax.broadcasted_iota(jnp.int32, (QBLK, 2 * QBLK), 0)
    ci = lax.broadcasted_iota(jnp.int32, (QBLK, 2 * QBLK), 1)
    dist = QBLK + qi - ci
    return (dist >= 0) & (dist <= QBLK) & (has_prev | (ci >= QBLK))


def _first_head(rows):
    return lax.broadcasted_iota(jnp.int32, (rows, LANES), 1) < HEAD_DIM


def _split_heads(pair):
    first = _first_head(pair.shape[0])
    zero = jnp.zeros_like(pair)
    return jnp.concatenate([jnp.where(first, pair, zero), jnp.where(first, zero, pair)], axis=0)


def _merge_heads(col_a, col_b):
    rows = col_a.shape[0]
    return jnp.where(_first_head(rows), jnp.broadcast_to(col_a, (rows, LANES)), jnp.broadcast_to(col_b, (rows, LANES)))


def _attn_specs(nb):
    blk = (None, None, None, QBLK, D_ATTN)
    q_spec = pl.BlockSpec(blk, lambda p, b: (0, p, b, 0, 0))
    kp_spec = pl.BlockSpec(blk, lambda p, b: (1, p, jnp.maximum(b - 1, 0), 0, 0))
    kc_spec = pl.BlockSpec(blk, lambda p, b: (1, p, b, 0, 0))
    vp_spec = pl.BlockSpec(blk, lambda p, b: (2, p, jnp.maximum(b - 1, 0), 0, 0))
    vc_spec = pl.BlockSpec(blk, lambda p, b: (2, p, b, 0, 0))
    return [q_spec, kp_spec, kc_spec, vp_spec, vc_spec]


def _attn_fwd(qkv, *, name):
    nb = qkv.shape[2]

    def body(q_ref, kp_ref, kc_ref, vp_ref, vc_ref, o_ref, lse_ref):
        mask = _band_mask(pl.program_id(0), pl.program_id(1))
        mask2 = jnp.concatenate([mask, mask], axis=0)
        kk = jnp.concatenate([kp_ref[...], kc_ref[...]], axis=0)
        vv = jnp.concatenate([vp_ref[...], vc_ref[...]], axis=0)
        for hp in range(HEADS // 2):
            cols = slice(hp * LANES, (hp + 1) * LANES)
            s = _dot_nt(_split_heads(q_ref[:, cols]), kk[:, cols])
            s = jnp.where(mask2, s, NEG)
            m = jnp.max(s, axis=-1, keepdims=True)
            e = jnp.exp(s - m)
            l = jnp.sum(e, axis=-1, keepdims=True)
            pr = (e * (1.0 / l)).astype(BF16)
            o_ref[:, cols] = _dot(jnp.concatenate([pr[:QBLK], pr[QBLK:]], axis=1), _split_heads(vv[:, cols]))
            lse = m + jnp.log(l)
            lse_ref[:, cols] = _merge_heads(lse[:QBLK], lse[QBLK:])

    out = pl.BlockSpec((None, None, QBLK, D_ATTN), lambda p, b: (p, b, 0, 0))
    return pl.pallas_call(
        body, grid=(N_PATTERNS, nb), in_specs=_attn_specs(nb), out_specs=[out, out],
        out_shape=[jax.ShapeDtypeStruct((N_PATTERNS, nb, QBLK, D_ATTN), F32)] * 2,
        compiler_params=_params("arbitrary", "arbitrary"), name=name)(qkv, qkv, qkv, qkv, qkv)


def _attn_combine(o3, lse3, B, S, *, name):
    def body(o_ref, l_ref, a_ref, lt_ref, so_ref, sl_ref):
        for p in range(N_PATTERNS):
            _unpermute_in(lambda r0, n, p=p: o_ref[p, pl.ds(r0, n), :], so_ref.at[p], p, S)
            _unpermute_in(lambda r0, n, p=p: l_ref[p, pl.ds(r0, n), :], sl_ref.at[p], p, S)
        l0, l1, l2 = sl_ref[0], sl_ref[1], sl_ref[2]
        m = jnp.maximum(jnp.maximum(l0, l1), l2)
        w0, w1, w2 = jnp.exp(l0 - m), jnp.exp(l1 - m), jnp.exp(l2 - m)
        tot = w0 + w1 + w2
        a_ref[...] = (w0 * so_ref[0] + w1 * so_ref[1] + w2 * so_ref[2]) / tot
        lt_ref[...] = m + jnp.log(tot)

    o3 = o3.reshape(N_PATTERNS, B, S, D_ATTN)
    lse3 = lse3.reshape(N_PATTERNS, B, S, D_ATTN)
    inp = pl.BlockSpec((N_PATTERNS, None, S, LANES), lambda b, c: (0, b, 0, c))
    out = pl.BlockSpec((S, LANES), lambda b, c: (b, c))
    return pl.pallas_call(
        body, grid=(B, D_ATTN // LANES), in_specs=[inp, inp], out_specs=[out, out],
        out_shape=[jax.ShapeDtypeStruct((B * S, D_ATTN), F32)] * 2,
        scratch_shapes=[pltpu.VMEM((N_PATTERNS, S, LANES), F32)] * 2,
        compiler_params=_params("arbitrary", "arbitrary"), name=name)(o3, lse3)


def _attn_bwd_prep(dattn, attn, lse, B, S, *, name):
    emat = _seg_matrix(LANES)

    def body(da_ref, a_ref, l_ref, e_ref, do_ref, lo_ref, dd_ref, scr_ref):
        da = da_ref[...]
        dsum = _seg_mean(da * a_ref[...], e_ref, 1.0)
        for k, (val, out_ref, cast) in enumerate(((da, do_ref, BF16), (l_ref[...], lo_ref, F32), (dsum, dd_ref, F32))):
            scr_ref[...] = val
            for p, d in enumerate(DILATIONS):
                L = S // d
                if d == 1:
                    out_ref[p] = val.astype(cast)
                else:
                    for r in range(d):
                        out_ref[p, r * L:(r + 1) * L, :] = scr_ref[pl.ds(r, L, stride=d), :].astype(cast)

    inp = pl.BlockSpec((S, LANES), lambda b, c: (b, c))
    out = pl.BlockSpec((N_PATTERNS, None, S, LANES), lambda b, c: (0, b, 0, c))
    shp = (N_PATTERNS, B, S, D_ATTN)
    return pl.pallas_call(
        body, grid=(B, D_ATTN // LANES),
        in_specs=[inp, inp, inp, pl.BlockSpec((LANES, LANES), lambda b, c: (0, 0))],
        out_specs=[out, out, out],
        out_shape=[jax.ShapeDtypeStruct(shp, BF16), jax.ShapeDtypeStruct(shp, F32), jax.ShapeDtypeStruct(shp, F32)],
        scratch_shapes=[pltpu.VMEM((S, LANES), F32)],
        compiler_params=_params("arbitrary", "arbitrary"), name=name)(dattn, attn, lse, emat)


def _attn_bwd(qkv, do3, lse3, dd3, *, name, phase=None):
    nb = qkv.shape[2]

    def body(q_ref, kp_ref, kc_ref, vp_ref, vc_ref, do_ref, l_ref, d_ref, cur_ref, prev_ref):
        mask = _band_mask(pl.program_id(0), pl.program_id(1))
        mask2 = jnp.concatenate([mask, mask], axis=0)
        kk = jnp.concatenate([kp_ref[...], kc_ref[...]], axis=0)
        vv = jnp.concatenate([vp_ref[...], vc_ref[...]], axis=0)
        for hp in range(HEADS // 2):
            cols = slice(hp * LANES, (hp + 1) * LANES)
            a, b = hp * LANES, hp * LANES + HEAD_DIM
            kh, vh = kk[:, cols], vv[:, cols]
            q2 = _split_heads(q_ref[:, cols])
            do2 = _split_heads(do_ref[:, cols])
            lse = jnp.concatenate([l_ref[:, a:a + 1], l_ref[:, b:b + 1]], axis=0)
            dsum = jnp.concatenate([d_ref[:, a:a + 1], d_ref[:, b:b + 1]], axis=0)
            s = _dot_nt(q2, kh)
            pr = jnp.where(mask2, jnp.exp(s - lse), 0.0)
            dp = _dot_nt(do2, vh)
            ds = (pr * (dp - dsum)).astype(BF16)
            prb = pr.astype(BF16)
            cur_ref[0, :, cols] = _dot(jnp.concatenate([ds[:QBLK], ds[QBLK:]], axis=1), _split_heads(kh))
            dk = _dot_tn(ds, q2)
            dv = _dot_tn(prb, do2)
            prev_ref[0, :, cols] = dk[:QBLK]
            cur_ref[1, :, cols] = dk[QBLK:]
            prev_ref[1, :, cols] = dv[:QBLK]
            cur_ref[2, :, cols] = dv[QBLK:]

    aux = pl.BlockSpec((None, None, QBLK, D_ATTN), lambda p, b: (p, b, 0, 0))
    return _call(
        body, grid=(N_PATTERNS, nb), in_specs=_attn_specs(nb) + [aux, aux, aux],
        out_specs=[pl.BlockSpec((3, None, None, QBLK, D_ATTN), lambda p, b: (0, p, b, 0, 0)),
                   pl.BlockSpec((2, None, None, QBLK, D_ATTN), lambda p, b: (0, p, b, 0, 0))],
        out_shape=[jax.ShapeDtypeStruct((3, N_PATTERNS, nb, QBLK, D_ATTN), F32),
                   jax.ShapeDtypeStruct((2, N_PATTERNS, nb, QBLK, D_ATTN), F32)],
        sem=("arbitrary", "arbitrary"), name=name, args=(qkv, qkv, qkv, qkv, qkv, do3, lse3, dd3), phase=phase)


def _attn_grad_combine(cur, prev, u, gains, B, S, *, name, phase=None):
    emat = _seg_matrix(LANES)
    nblk = S // QBLK

    def body(cur_ref, prev_ref, u_ref, g_ref, e_ref, du_ref, dg_ref, scr_ref):
        c = pl.program_id(0)
        b = pl.program_id(2)
        use_prev = c > 0
        total = None
        for p, d in enumerate(DILATIONS):
            per_seq = nblk // d

            def get_block(r0, n, p=p, per_seq=per_seq):
                parts = []
                for blk in range(r0 // QBLK, (r0 + n) // QBLK):
                    v = cur_ref[p, pl.ds(blk * QBLK, QBLK), :]
                    if blk % per_seq != per_seq - 1:
                        nxt = prev_ref[p, pl.ds((blk + 1) * QBLK, QBLK), :]
                        v = v + jnp.where(use_prev, nxt, 0.0)
                    parts.append(v)
                return parts[0] if len(parts) == 1 else jnp.concatenate(parts, axis=0)

            _unpermute_in(get_block, scr_ref.at[p], p, S)
        dy = scr_ref[0] + scr_ref[1] + scr_ref[2]
        xv = u_ref[...]
        gain = g_ref[...]
        ms = _seg_mean(xv * xv, e_ref, HEAD_DIM)
        r = lax.rsqrt(ms + EPS)
        xhat = xv * r
        dxh = dy * gain
        dx = r * (dxh - xhat * _seg_mean(dxh * xhat, e_ref, HEAD_DIM))
        du_ref[...] = jnp.where(c < 2, dx, dy).astype(BF16)

        @pl.when((b == 0))
        def _():
            dg_ref[...] = jnp.zeros_like(dg_ref)

        dg_ref[...] += jnp.sum(dy * xhat, axis=0, keepdims=True)

    cur = cur.reshape(3, N_PATTERNS, B, S, D_ATTN)
    prev = prev.reshape(2, N_PATTERNS, B, S, D_ATTN)
    ncc = D_ATTN // LANES
    return _call(
        body, grid=(3, ncc, B),
        in_specs=[pl.BlockSpec((None, N_PATTERNS, None, S, LANES), lambda c, cc, b: (c, 0, b, 0, cc)),
                  pl.BlockSpec((None, N_PATTERNS, None, S, LANES), lambda c, cc, b: (jnp.maximum(c - 1, 0), 0, b, 0, cc)),
                  pl.BlockSpec((S, LANES), lambda c, cc, b: (b, c * ncc + cc)),
                  pl.BlockSpec((None, 1, LANES), lambda c, cc, b: (c, 0, cc)),
                  pl.BlockSpec((LANES, LANES), lambda c, cc, b: (0, 0))],
        out_specs=[pl.BlockSpec((S, LANES), lambda c, cc, b: (b, c * ncc + cc)),
                   pl.BlockSpec((None, 1, LANES), lambda c, cc, b: (c, 0, cc))],
        out_shape=[jax.ShapeDtypeStruct((B * S, 3 * D_ATTN), BF16), jax.ShapeDtypeStruct((3, 1, D_ATTN), F32)],
        scratch_shapes=[pltpu.VMEM((N_PATTERNS, S, LANES), F32)],
        sem=("arbitrary", "arbitrary", "arbitrary"), name=name, args=(cur, prev, u, gains, emat), phase=phase)


HALO = 32
SUB = 64


def _conv_fwd(u, cw, cb, lg, lb, B, S, *, tc, name):
    nchunk = S // tc
    hb = tc // HALO

    def body(ca_ref, cap_ref, cg_ref, cgp_ref, w_ref, cb_ref, lg_ref, lb_ref, cv_ref, glu_ref, y_ref, pad_ref):
        i = pl.program_id(1)
        glu = ca_ref[...] * _sigmoid(cg_ref[...])
        glu_ref[...] = glu
        prev = cap_ref[...] * _sigmoid(cgp_ref[...])
        pad_ref[0:HALO, :] = jnp.where(i > 0, prev, 0.0)
        pad_ref[HALO:, :] = glu
        for sub in range(tc // SUB):
            acc = jnp.zeros((SUB, D_CONV), F32) + cb_ref[...]
            for k in range(CONV_K):
                acc = acc + pad_ref[pl.ds(sub * SUB + HALO - (CONV_K - 1) + k, SUB), :] * w_ref[pl.ds(k, 1), :]
            y_ref[sub * SUB:(sub + 1) * SUB, :] = acc
        y = y_ref[...]
        mu = jnp.mean(y, axis=-1, keepdims=True)
        yc = y - mu
        var = jnp.mean(yc * yc, axis=-1, keepdims=True)
        z = yc * lax.rsqrt(var + EPS) * lg_ref[...] + lb_ref[...]
        cv_ref[...] = (z * _sigmoid(z)).astype(BF16)

    def cur(col):
        return pl.BlockSpec((tc, D_CONV), lambda b, i: (b * nchunk + i, col))

    def halo(col):
        return pl.BlockSpec((HALO, D_CONV), lambda b, i: (jnp.maximum((b * nchunk + i) * hb - 1, 0), col))

    vec = pl.BlockSpec((1, D_CONV), lambda b, i: (0, 0))
    out = pl.BlockSpec((tc, D_CONV), lambda b, i: (b * nchunk + i, 0))
    return pl.pallas_call(
        body, grid=(B, nchunk),
        in_specs=[cur(3), halo(3), cur(4), halo(4), pl.BlockSpec((CONV_K, D_CONV), lambda b, i: (0, 0)), vec, vec, vec],
        out_specs=[out, out, out],
        out_shape=[jax.ShapeDtypeStruct((B * S, D_CONV), BF16), jax.ShapeDtypeStruct((B * S, D_CONV), F32),
                   jax.ShapeDtypeStruct((B * S, D_CONV), F32)],
        scratch_shapes=[pltpu.VMEM((tc + HALO, D_CONV), F32)],
        compiler_params=_params("arbitrary", "arbitrary"), name=name)(u, u, u, u, cw, cb, lg, lb)


def _conv_bwd_norm(dcv, y, lg, lb, *, tc, name):
    T = y.shape[0]

    def body(dcv_ref, y_ref, lg_ref, lb_ref, dy_ref, part_ref):
        yv = y_ref[...]
        mu = jnp.mean(yv, axis=-1, keepdims=True)
        yc = yv - mu
        var = jnp.mean(yc * yc, axis=-1, keepdims=True)
        rstd = lax.rsqrt(var + EPS)
        xhat = yc * rstd
        z = xhat * lg_ref[...] + lb_ref[...]
        sig = _sigmoid(z)
        dz = dcv_ref[...] * (sig * (1.0 + z * (1.0 - sig)))
        dxh = dz * lg_ref[...]
        dy = rstd * (dxh - jnp.mean(dxh, axis=-1, keepdims=True)
                     - xhat * jnp.mean(dxh * xhat, axis=-1, keepdims=True))
        dy_ref[...] = dy

        @pl.when(pl.program_id(0) == 0)
        def _():
            part_ref[...] = jnp.zeros_like(part_ref)

        part_ref[0:1, :] += jnp.sum(dz * xhat, axis=0, keepdims=True)
        part_ref[1:2, :] += jnp.sum(dz, axis=0, keepdims=True)
        part_ref[2:3, :] += jnp.sum(dy, axis=0, keepdims=True)

    tok = pl.BlockSpec((tc, D_CONV), lambda i: (i, 0))
    vec = pl.BlockSpec((1, D_CONV), lambda i: (0, 0))
    return pl.pallas_call(
        body, grid=(T // tc,), in_specs=[tok, tok, vec, vec],
        out_specs=[tok, pl.BlockSpec((8, D_CONV), lambda i: (0, 0))],
        out_shape=[jax.ShapeDtypeStruct((T, D_CONV), F32), jax.ShapeDtypeStruct((8, D_CONV), F32)],
        compiler_params=_params("arbitrary"), name=name)(dcv, y, lg, lb)


def _conv_bwd_taps(dy, glu, u, cw, B, S, *, tc, name, phase=None):
    nchunk = S // tc
    hb = tc // HALO
    last_hb = B * S // HALO - 1

    def body(dy_ref, dyn_ref, glu_ref, glup_ref, ca_ref, cg_ref, w_ref, dca_ref, dcg_ref, dw_ref, dyp_ref, glp_ref, acc_ref):
        b = pl.program_id(0)
        i = pl.program_id(1)
        dy = dy_ref[...]
        dyp_ref[0:tc, :] = dy
        dyp_ref[tc:, :] = jnp.where(i < nchunk - 1, dyn_ref[...], 0.0)
        glp_ref[0:HALO, :] = jnp.where(i > 0, glup_ref[...], 0.0)
        glp_ref[HALO:, :] = glu_ref[...]

        @pl.when((b == 0) & (i == 0))
        def _():
            dw_ref[...] = jnp.zeros_like(dw_ref)

        for sub in range(tc // SUB):
            acc = jnp.zeros((SUB, D_CONV), F32)
            for k in range(CONV_K):
                acc = acc + dyp_ref[pl.ds(sub * SUB + (CONV_K - 1) - k, SUB), :] * w_ref[pl.ds(k, 1), :]
            acc_ref[sub * SUB:(sub + 1) * SUB, :] = acc
        for k in range(CONV_K):
            dw_ref[k:k + 1, :] += jnp.sum(dy * glp_ref[pl.ds(HALO - (CONV_K - 1) + k, tc), :], axis=0, keepdims=True)
        dglu = acc_ref[...]
        ca = ca_ref[...]
        sig = _sigmoid(cg_ref[...])
        dca_ref[...] = (dglu * sig).astype(BF16)
        dcg_ref[...] = (dglu * ca * sig * (1.0 - sig)).astype(BF16)

    tok = pl.BlockSpec((tc, D_CONV), lambda b, i: (b * nchunk + i, 0))
    nxt = pl.BlockSpec((HALO, D_CONV), lambda b, i: (jnp.minimum((b * nchunk + i + 1) * hb, last_hb), 0))
    prv = pl.BlockSpec((HALO, D_CONV), lambda b, i: (jnp.maximum((b * nchunk + i) * hb - 1, 0), 0))
    return _call(
        body, grid=(B, nchunk),
        in_specs=[tok, nxt, tok, prv,
                  pl.BlockSpec((tc, D_CONV), lambda b, i: (b * nchunk + i, 3)),
                  pl.BlockSpec((tc, D_CONV), lambda b, i: (b * nchunk + i, 4)),
                  pl.BlockSpec((CONV_K, D_CONV), lambda b, i: (0, 0))],
        out_specs=[tok, tok, pl.BlockSpec((32, D_CONV), lambda b, i: (0, 0))],
        out_shape=[jax.ShapeDtypeStruct((B * S, D_CONV), BF16), jax.ShapeDtypeStruct((B * S, D_CONV), BF16),
                   jax.ShapeDtypeStruct((32, D_CONV), F32)],
        scratch_shapes=[pltpu.VMEM((tc + HALO, D_CONV), F32), pltpu.VMEM((tc + HALO, D_CONV), F32),
                        pltpu.VMEM((tc, D_CONV), F32)],
        sem=("arbitrary", "arbitrary"), name=name, args=(dy, dy, glu, glu, u, u, cw), phase=phase)


def _outproj_fwd(h, attn, cv, wout, *, tm, name):
    T, D = h.shape

    def body(h_ref, a_ref, c_ref, w_ref, o_ref):
        o_ref[...] = (h_ref[...] + _dot(a_ref[...].astype(BF16), w_ref[0:D_ATTN, :])
                      + _dot(c_ref[...], w_ref[D_ATTN:, :]))

    tok = pl.BlockSpec((tm, D), lambda i: (i, 0))
    half = pl.BlockSpec((tm, D_ATTN), lambda i: (i, 0))
    return pl.pallas_call(
        body, grid=(T // tm,), in_specs=[tok, half, half, pl.BlockSpec(wout.shape, lambda i: (0, 0))],
        out_specs=tok, out_shape=jax.ShapeDtypeStruct((T, D), F32),
        compiler_params=_params("arbitrary"), name=name)(h, attn, cv, wout)


def _outproj_bwd(dh, attn, cv, wout, *, tm, name):
    T, D = dh.shape

    def body(dh_ref, a_ref, c_ref, w_ref, da_ref, dc_ref, dw_ref):
        @pl.when(pl.program_id(0) == 0)
        def _():
            dw_ref[...] = jnp.zeros_like(dw_ref)

        dhb = dh_ref[...].astype(BF16)
        da_ref[...] = _dot_nt(dhb, w_ref[0:D_ATTN, :])
        dc_ref[...] = _dot_nt(dhb, w_ref[D_ATTN:, :])
        dw_ref[0:D_ATTN, :] += _dot_tn(a_ref[...].astype(BF16), dhb)
        dw_ref[D_ATTN:, :] += _dot_tn(c_ref[...], dhb)

    tok = pl.BlockSpec((tm, D), lambda i: (i, 0))
    half = pl.BlockSpec((tm, D_ATTN), lambda i: (i, 0))
    wspec = pl.BlockSpec(wout.shape, lambda i: (0, 0))
    return pl.pallas_call(
        body, grid=(T // tm,), in_specs=[tok, half, half, wspec], out_specs=[half, half, wspec],
        out_shape=[jax.ShapeDtypeStruct((T, D_ATTN), F32), jax.ShapeDtypeStruct((T, D_ATTN), F32),
                   jax.ShapeDtypeStruct(wout.shape, F32)],
        compiler_params=_params("arbitrary"), name=name)(dh, attn, cv, wout)


def _adamw(w, g, m, v, *, name):
    R, C = w.shape
    tr = R
    for cand in (512, 256, 128, 64, 32, 16, 8):
        if R % cand == 0 and R > cand:
            tr = cand
            break
    c1 = 1.0 - ADAM_B1 ** ADAM_STEP
    c2 = 1.0 - ADAM_B2 ** ADAM_STEP

    def body(w_ref, g_ref, m_ref, v_ref, d_ref, nm_ref, nv_ref):
        gv = g_ref[...]
        nm = ADAM_B1 * m_ref[...] + (1.0 - ADAM_B1) * gv
        nv = ADAM_B2 * v_ref[...] + (1.0 - ADAM_B2) * (gv * gv)
        d_ref[...] = -ADAM_LR * ((nm / c1) / (jnp.sqrt(nv / c2) + ADAM_EPS) + ADAM_WD * w_ref[...])
        nm_ref[...] = nm
        nv_ref[...] = nv

    blk = pl.BlockSpec((tr, C), lambda i: (i, 0))
    return pl.pallas_call(
        body, grid=(R // tr,), in_specs=[blk] * 4, out_specs=[blk] * 3,
        out_shape=[jax.ShapeDtypeStruct((R, C), F32)] * 3,
        compiler_params=_params("arbitrary"), name=name)(w, g, m, v)


TM = 512
TM_FFN_FWD = 1024
TK = 1024
TC = 256


def _local_step(x, tgt, w, overlap=None):
    B, S, D = x.shape
    T = B * S
    x2 = x.reshape(T, D)
    t2 = tgt.reshape(T, D)
    ones = jnp.ones((1, D_ATTN), F32)
    scale = HEAD_DIM ** -0.5
    gains = jnp.stack([jnp.tile(w["q_norm"], (1, HEADS)) * scale, jnp.tile(w["k_norm"], (1, HEADS)), ones])
    g = {}

    def hosting(point, build):
        phase = overlap.phase(point, w, g) if overlap is not None else None
        if phase is None:
            return build(None)
        outs, extra = build(phase)
        overlap.done(point, extra, w, g)
        return outs

    h1, n1, G1, U1 = hosting("ffn1_fwd", lambda ph: _ffn_fwd(
        x2, w["ffn1_norm"], w["wg1"], w["wu1"], w["wd1"], None, tm=TM_FFN_FWD, name="ffn1_fwd", phase=ph))
    u, n2 = hosting("inproj_fwd", lambda ph: _inproj_fwd(h1, w["mix_norm"], w["win"], tm=TM, name="inproj_fwd", phase=ph))
    qkv = _qkv_prep(u, gains, B, S, name="qkv_prep")
    qkv = qkv.reshape(3, N_PATTERNS, T // QBLK, QBLK, D_ATTN)
    o3, lse3 = _attn_fwd(qkv, name="attn_fwd")
    attn, lse = _attn_combine(o3, lse3, B, S, name="attn_combine")
    cv, glu, yconv = _conv_fwd(u, w["conv_w"], w["conv_b"], w["conv_ln_g"], w["conv_ln_b"], B, S, tc=TC, name="conv_fwd")
    h2 = _outproj_fwd(h1, attn, cv, w["wout"], tm=TM, name="outproj_fwd")
    dh3, n3, G2, U2, loss = _ffn_fwd(h2, w["ffn2_norm"], w["wg2"], w["wu2"], w["wd2"], t2, tm=TM_FFN_FWD, name="ffn2_fwd")

    dG, dU, A, dy, dh2, g["ffn2_norm"] = _ffn_bwd_act(dh3, h2, w["ffn2_norm"], G2, U2, w["wg2"], w["wu2"], w["wd2"],
                                                    tm=TM, name="ffn2_bwd_act")
    g["wg2"], g["wu2"], g["wd2"] = _ffn_bwd_w(n3, dy, dG, dU, A, tk=TK, name="ffn2_bwd_w")
    dattn, dcv, g["wout"] = _outproj_bwd(dh2, attn, cv, w["wout"], tm=TM, name="outproj_bwd")
    dyc, cpart = _conv_bwd_norm(dcv, yconv, w["conv_ln_g"], w["conv_ln_b"], tc=TC, name="conv_bwd_norm")
    dca, dcg, dcw = hosting("conv_bwd_taps", lambda ph: _conv_bwd_taps(
        dyc, glu, u, w["conv_w"], B, S, tc=TC, name="conv_bwd_taps", phase=ph))
    do3, lseb3, dd3 = _attn_bwd_prep(dattn, attn, lse, B, S, name="attn_bwd_prep")
    nb = T // QBLK
    shp = (N_PATTERNS, nb, QBLK, D_ATTN)
    cur, prev = hosting("attn_bwd", lambda ph: _attn_bwd(
        qkv, do3.reshape(shp), lseb3.reshape(shp), dd3.reshape(shp), name="attn_bwd", phase=ph))
    du_qkv, dgains = hosting("attn_grad_combine", lambda ph: _attn_grad_combine(
        cur, prev, u, gains, B, S, name="attn_grad_combine", phase=ph))
    du = jnp.concatenate([du_qkv, dca, dcg], axis=1)
    dh1, g["mix_norm"] = hosting("inproj_bwd_act", lambda ph: _inproj_bwd_act(
        du, dh2, h1, w["mix_norm"], w["win"], tm=TM, name="inproj_bwd_act", phase=ph))
    g["win"] = _inproj_bwd_w(n2, du, w["win"].shape[0], tk=TK, name="inproj_bwd_w")
    dG, dU, A, dy, dx, g["ffn1_norm"] = _ffn_bwd_act(dh1, x2, w["ffn1_norm"], G1, U1, w["wg1"], w["wu1"], w["wd1"],
                                                   tm=TM, name="ffn1_bwd_act")
    g["wg1"], g["wu1"], g["wd1"] = _ffn_bwd_w(n1, dy, dG, dU, A, tk=TK, name="ffn1_bwd_w")

    g["q_norm"] = dgains[0].reshape(HEADS, HEAD_DIM).sum(axis=0, keepdims=True) * scale
    g["k_norm"] = dgains[1].reshape(HEADS, HEAD_DIM).sum(axis=0, keepdims=True)
    g["conv_ln_g"] = cpart[0:1]
    g["conv_ln_b"] = cpart[1:2]
    g["conv_b"] = cpart[2:3]
    g["conv_w"] = dcw[:CONV_K]
    return loss, dx.reshape(B, S, D), g


N_CHIPS = 4
N_DEV = 8
ANY = pl.BlockSpec(memory_space=pl.ANY)
VMEM_SPEC = pl.BlockSpec(memory_space=pltpu.VMEM)


def _place():
    x, y, c = lax.axis_index("x"), lax.axis_index("y"), lax.axis_index("c")
    chips = [(1 - x, y), (x, 1 - y), (1 - x, 1 - y)]
    return x, y, c, 2 * x + y, chips, [2 * px + py for px, py in chips]


def _remote(src, dst, send_sem, recv_sem, device):
    return pltpu.make_async_remote_copy(src_ref=src, dst_ref=dst, send_sem=send_sem, recv_sem=recv_sem,
                                        device_id=device, device_id_type=MESH)


def _gather_weights(shards, dtypes, *, name):
    n = len(shards)
    halves = [s.reshape(2, s.shape[0] // 2, s.shape[1]) for s in shards]

    def body(*refs):
        ins, outs, vms = refs[:n], refs[n:2 * n], refs[2 * n:3 * n]
        send_sems, recv_sems, loc_sems = refs[3 * n:]
        x, y, c, me, chips, cidx = _place()
        sibling = (x, y, 1 - c)
        for a in range(n):
            vms[a][...] = ins[a][...].astype(dtypes[a])
        local, first, passed = [], [], []
        for a in range(n):
            lc = pltpu.make_async_copy(vms[a], outs[a].at[me], loc_sems.at[a])
            lc.start()
            local.append(lc)
            for j, chip in enumerate(chips):
                cp = _remote(vms[a].at[c], outs[a].at[me, c], send_sems.at[6 * a + j], recv_sems.at[6 * a + j], (*chip, c))
                cp.start()
                first.append(cp)
        for a in range(n):
            for j, chip in enumerate(chips):
                land = outs[a].at[cidx[j], c]
                _remote(land, land, send_sems.at[6 * a + j], recv_sems.at[6 * a + j], (*chip, c)).wait_recv()
                fw = _remote(land, land, send_sems.at[6 * a + 3 + j], recv_sems.at[6 * a + 3 + j], sibling)
                fw.start()
                passed.append(fw)
        for a in range(n):
            for j in range(3):
                land = outs[a].at[cidx[j], 1 - c]
                _remote(land, land, send_sems.at[6 * a + 3 + j], recv_sems.at[6 * a + 3 + j], sibling).wait_recv()
        for cp in first + passed:
            cp.wait_send()
        for lc in local:
            lc.wait()

    outs = pl.pallas_call(
        body, in_specs=[VMEM_SPEC] * n, out_specs=[ANY] * n,
        out_shape=[jax.ShapeDtypeStruct((N_CHIPS,) + h.shape, dt) for h, dt in zip(halves, dtypes)],
        scratch_shapes=[pltpu.VMEM(h.shape, dt) for h, dt in zip(halves, dtypes)]
        + [pltpu.SemaphoreType.DMA((6 * n,)), pltpu.SemaphoreType.DMA((6 * n,)), pltpu.SemaphoreType.DMA((n,))],
        compiler_params=pltpu.CompilerParams(vmem_limit_bytes=VMEM_LIMIT), name=name)(*halves)
    return [o.reshape((N_CHIPS,) + s.shape) for o, s in zip(outs, shards)]


def _stage_shards(shards, dtypes, *, name):
    n = len(shards)
    halves = [s.reshape(2, s.shape[0] // 2, s.shape[1]) for s in shards]

    def body(*refs):
        ins, outs, vms, loc_sems = refs[:n], refs[n:2 * n], refs[2 * n:3 * n], refs[3 * n]
        me = 2 * lax.axis_index("x") + lax.axis_index("y")
        copies = []
        for a in range(n):
            vms[a][...] = ins[a][...].astype(dtypes[a])
            cp = pltpu.make_async_copy(vms[a], outs[a].at[me], loc_sems.at[a])
            cp.start()
            copies.append(cp)
        for cp in copies:
            cp.wait()

    return pl.pallas_call(
        body, in_specs=[VMEM_SPEC] * n, out_specs=[ANY] * n,
        out_shape=[jax.ShapeDtypeStruct((N_CHIPS,) + h.shape, dt) for h, dt in zip(halves, dtypes)],
        scratch_shapes=[pltpu.VMEM(h.shape, dt) for h, dt in zip(halves, dtypes)] + [DMA_SEMS((n,))],
        compiler_params=pltpu.CompilerParams(vmem_limit_bytes=VMEM_LIMIT), name=name)(*halves)


def _like(arrays):
    return [jax.ShapeDtypeStruct(a.shape, a.dtype) for a in arrays]


def _gather_ici_phase(bufs):
    n = len(bufs)

    def copies(ins, outs, send_sems, recv_sems):
        x, y, c, me, chips, cidx = _place()
        starts, arrivals = [], []
        for a in range(n):
            for j, chip in enumerate(chips):
                k = 3 * a + j
                mine, land = outs[a].at[me, c], outs[a].at[cidx[j], c]
                starts.append(_remote(mine, mine, send_sems.at[k], recv_sems.at[k], (*chip, c)))
                arrivals.append(_remote(land, land, send_sems.at[k], recv_sems.at[k], (*chip, c)))
        return starts, arrivals

    return _Phase(bufs, _like(bufs), {a: a for a in range(n)}, 3 * n, copies)


def _gather_d2d_phase(bufs):
    n = len(bufs)

    def copies(ins, outs, send_sems, recv_sems):
        x, y, c, me, chips, cidx = _place()
        starts, arrivals = [], []
        for a in range(n):
            for j in range(3):
                k = 3 * a + j
                got, land = outs[a].at[cidx[j], c], outs[a].at[cidx[j], 1 - c]
                starts.append(_remote(got, got, send_sems.at[k], recv_sems.at[k], (x, y, 1 - c)))
                arrivals.append(_remote(land, land, send_sems.at[k], recv_sems.at[k], (x, y, 1 - c)))
        return starts, arrivals

    return _Phase(bufs, _like(bufs), {a: a for a in range(n)}, 3 * n, copies)


def _exchange_phase(views):
    n = len(views)

    def copies(ins, outs, send_sems, recv_sems):
        x, y, c = lax.axis_index("x"), lax.axis_index("y"), lax.axis_index("c")
        starts = [_remote(ins[a].at[pl.ds(0, ins[a].shape[0]), 1 - c], outs[a], send_sems.at[a], recv_sems.at[a],
                          (x, y, 1 - c)) for a in range(n)]
        return starts, starts

    outs = [jax.ShapeDtypeStruct((v.shape[0],) + v.shape[2:], F32) for v in views]
    return _Phase(views, outs, {}, n, copies)


def _row_block(rows):
    for cand in (256, 176, 128, 64, 32, 16, 8):
        if rows % cand == 0:
            return cand
    return rows


def _add_own_half(g, r, sel, *, name):
    ns, _, rh, cdim = g.shape
    tr = _row_block(rh)

    def body(s_ref, gk_ref, rk_ref, gs_ref, rs_ref, keep_ref, send_ref):
        keep_ref[...] = gk_ref[...] + rk_ref[...]
        send_ref[...] = (gs_ref[...] + rs_ref[...]).astype(BF16)

    def g_spec(off):
        return pl.BlockSpec((None, None, tr, cdim), lambda k, i, s: (s[1 + off + k], s[0], i, 0))

    def r_spec(off):
        return pl.BlockSpec((None, tr, cdim), lambda k, i, s: (s[1 + off + k], i, 0))

    out = pl.BlockSpec((None, tr, cdim), lambda k, i, s: (k, i, 0))
    return pl.pallas_call(
        body,
        grid_spec=pltpu.PrefetchScalarGridSpec(
            num_scalar_prefetch=1, grid=(2, rh // tr),
            in_specs=[g_spec(0), r_spec(0), g_spec(2), r_spec(2)], out_specs=[out, out]),
        out_shape=[jax.ShapeDtypeStruct((2, rh, cdim), F32), jax.ShapeDtypeStruct((2, rh, cdim), BF16)],
        compiler_params=_params("arbitrary", "arbitrary"), name=name)(sel, g, r, g, r)


def _swap_phase(arrays, stage):
    n = len(arrays)

    def copies(ins, outs, send_sems, recv_sems):
        x, y, c = lax.axis_index("x"), lax.axis_index("y"), lax.axis_index("c")
        along_x = (1 - c) if stage == 1 else c
        peer = (x + along_x * (1 - 2 * x), y + (1 - along_x) * (1 - 2 * y), c)
        starts = [_remote(ins[a], outs[a], send_sems.at[a], recv_sems.at[a], peer) for a in range(n)]
        return starts, starts

    return _Phase(arrays, _like(arrays), {}, n, copies)


def _add_stage1(keep, got, *, name):
    _, rh, cdim = keep.shape
    tr = _row_block(rh)

    def body(k_ref, g_ref, keep_ref, send_ref):
        keep_ref[...] = k_ref[0] + g_ref[0].astype(F32)
        send_ref[...] = (k_ref[1] + g_ref[1].astype(F32)).astype(BF16)

    blk2 = pl.BlockSpec((2, tr, cdim), lambda i: (0, i, 0))
    blk = pl.BlockSpec((tr, cdim), lambda i: (i, 0))
    return pl.pallas_call(
        body, grid=(rh // tr,), in_specs=[blk2, blk2], out_specs=[blk, blk],
        out_shape=[jax.ShapeDtypeStruct((rh, cdim), F32), jax.ShapeDtypeStruct((rh, cdim), BF16)],
        compiler_params=_params("arbitrary"), name=name)(keep, got)


def _add_stage2(keep, got, sel, *, name):
    rh, cdim = keep.shape
    tr = _row_block(rh)

    def body(s_ref, k_ref, g_ref, o_ref):
        o_ref[...] = k_ref[...] + g_ref[...].astype(F32)

    blk = pl.BlockSpec((tr, cdim), lambda i, s: (i, 0))
    return pl.pallas_call(
        body,
        grid_spec=pltpu.PrefetchScalarGridSpec(
            num_scalar_prefetch=1, grid=(rh // tr,), in_specs=[blk, blk],
            out_specs=pl.BlockSpec((None, tr, cdim), lambda i, s: (s[0], i, 0))),
        out_shape=jax.ShapeDtypeStruct((2, rh, cdim), F32),
        compiler_params=_params("arbitrary"), name=name)(sel, keep, got)


def _join_phase(halves):
    n = len(halves)

    def copies(ins, outs, send_sems, recv_sems):
        x, y, c = lax.axis_index("x"), lax.axis_index("y"), lax.axis_index("c")
        starts, arrivals = [], []
        for a in range(n):
            mine, land = outs[a].at[c], outs[a].at[1 - c]
            starts.append(_remote(mine, mine, send_sems.at[a], recv_sems.at[a], (x, y, 1 - c)))
            arrivals.append(_remote(land, land, send_sems.at[a], recv_sems.at[a], (x, y, 1 - c)))
        return starts, arrivals

    return _Phase(halves, _like(halves), {a: a for a in range(n)}, n, copies)


def _slot_order():
    x, y, c = lax.axis_index("x"), lax.axis_index("y"), lax.axis_index("c")
    own, flip_x, flip_y, both = 2 * x + y, 2 * (1 - x) + y, 2 * x + 1 - y, 2 * (1 - x) + 1 - y
    first = jnp.where(c == 0, flip_x, flip_y)
    second = jnp.where(c == 0, flip_y, flip_x)
    return jnp.stack([c, own, second, first, both]).astype(jnp.int32)


def _reduce_scatter(grads):
    sel = _slot_order()
    views = [_half_view(g) for g in grads]
    got = _run_phase(_exchange_phase(views), name="rs_exchange_halves")
    keep, send = _add_halves(views, got, sel, "late")
    got = _run_phase(_swap_phase(send, 1), name="rs_swap_first_axis")
    keep, send = _add_first(keep, got, "late")
    got = _run_phase(_swap_phase(send, 2), name="rs_swap_second_axis")
    halves = _add_second(keep, got, sel, "late")
    full = _run_phase(_join_phase(halves), name="rs_join_halves")
    return [f.reshape(g.shape[1], g.shape[2]) for f, g in zip(full, grads)]


def _half_view(g):
    return g.reshape(N_CHIPS, 2, g.shape[1] // 2, g.shape[2])


def _add_halves(views, got, sel, tag):
    keep, send = zip(*[_add_own_half(v, r, sel, name=f"rs_add_half_{tag}{a}") for a, (v, r) in enumerate(zip(views, got))])
    return list(keep), list(send)


def _add_first(keep, got, tag):
    keep, send = zip(*[_add_stage1(k, r, name=f"rs_add_first_{tag}{a}") for a, (k, r) in enumerate(zip(keep, got))])
    return list(keep), list(send)


def _add_second(keep, got, sel, tag):
    return [_add_stage2(k, r, sel, name=f"rs_add_second_{tag}{a}") for a, (k, r) in enumerate(zip(keep, got))]


EARLY_GRADS = ("wg2", "wu2", "wd2", "wout")


class _Overlap:
    def __init__(self, staged):
        self.staged = staged
        self.sel = _slot_order()
        self.reduced = {}

    def phase(self, point, w, g):
        if point == "ffn1_fwd":
            return _gather_ici_phase(self.staged)
        if point == "inproj_fwd":
            return _gather_d2d_phase(self.ffn2)
        if point == "conv_bwd_taps":
            self.shapes = [g[k].shape for k in EARLY_GRADS]
            self.views = [_half_view(g[k].reshape(N_CHIPS, -1, g[k].shape[-1])) for k in EARLY_GRADS]
            return _exchange_phase(self.views)
        if point == "attn_bwd":
            return _swap_phase(self.send, 1)
        if point == "attn_grad_combine":
            return _swap_phase(self.send, 2)
        if point == "inproj_bwd_act":
            return _join_phase(self.halves)
        return None

    def done(self, point, outs, w, g):
        if point == "ffn1_fwd":
            win, wout, taps = [_whole(b) for b in _run_phase(_gather_d2d_phase(outs[:3]), name="gather_mix_d2d")]
            w["win"] = win
            w["wout"] = wout.reshape(-1, wout.shape[-1])
            w["conv_w"] = taps.transpose(1, 0, 2).reshape(CONV_K + 1, D_CONV)[:CONV_K]
            self.ffn2 = list(outs[3:])
        elif point == "inproj_fwd":
            w["wg2"], w["wu2"], w["wd2"] = [_whole(b) for b in outs]
        elif point == "conv_bwd_taps":
            self.keep, self.send = _add_halves(self.views, outs, self.sel, "early")
        elif point == "attn_bwd":
            self.keep, self.send = _add_first(self.keep, outs, "early")
        elif point == "attn_grad_combine":
            self.halves = _add_second(self.keep, outs, self.sel, "early")
        elif point == "inproj_bwd_act":
            for k, shp, f in zip(EARLY_GRADS, self.shapes, outs):
                self.reduced[k] = f.reshape(-1, shp[-1])


def _whole(buf):
    return buf.reshape(buf.shape[0], 2 * buf.shape[2], buf.shape[3])


def _allreduce_small(pack, *, name):
    rows = pack.shape[0]

    def body(p_ref, o_ref, buf_ref, send_sems, recv_sems):
        x, y, c = lax.axis_index("x"), lax.axis_index("y"), lax.axis_index("c")
        me = 4 * x + 2 * y + c
        buf_ref[me] = p_ref[...]
        cps = []
        for k in range(1, N_DEV):
            peer = tuple(1 - v if (k >> s) & 1 else v for v, s in ((x, 2), (y, 1), (c, 0)))
            cp = _remote(p_ref, buf_ref.at[me], send_sems.at[k - 1], recv_sems.at[k - 1], peer)
            cp.start()
            cps.append(cp)
        for k in range(1, N_DEV):
            src = 4 * (x ^ ((k >> 2) & 1)) + 2 * (y ^ ((k >> 1) & 1)) + (c ^ (k & 1))
            land = buf_ref.at[src]
            _remote(land, land, send_sems.at[k - 1], recv_sems.at[k - 1], (x, y, c)).wait_recv()
        acc = buf_ref[0]
        for d in range(1, N_DEV):
            acc = acc + buf_ref[d]
        o_ref[...] = acc
        for cp in cps:
            cp.wait_send()

    return pl.pallas_call(
        body, in_specs=[VMEM_SPEC], out_specs=VMEM_SPEC, out_shape=jax.ShapeDtypeStruct(pack.shape, F32),
        scratch_shapes=[pltpu.VMEM((N_DEV, rows, LANES), F32), pltpu.SemaphoreType.DMA((N_DEV - 1,)),
                        pltpu.SemaphoreType.DMA((N_DEV - 1,))], name=name)(pack)


SMALL = ("ffn1_norm", "mix_norm", "q_norm", "k_norm", "conv_b", "conv_ln_g", "conv_ln_b", "ffn2_norm", "conv_w")
BIG = ("ffn1_w_gate", "ffn1_w_up", "ffn1_w_down", "w_in", "w_out", "ffn2_w_gate", "ffn2_w_up", "ffn2_w_down")
TRANSPOSED = ("ffn1_w_gate", "ffn1_w_up", "ffn2_w_gate", "ffn2_w_up")
WEIGHTS = ("ffn1_norm", "ffn1_w_gate", "ffn1_w_up", "ffn1_w_down", "mix_norm", "w_in", "q_norm", "k_norm",
           "conv_w", "conv_b", "conv_ln_g", "conv_ln_b", "w_out", "ffn2_norm", "ffn2_w_gate", "ffn2_w_up",
           "ffn2_w_down")
SUBLANES = 8


def _pack(parts):
    rows = []
    for p in parts:
        flat = p.reshape(-1)
        tile = SUBLANES * LANES
        padded = -(-flat.shape[0] // tile) * tile
        rows.append(jnp.pad(flat, (0, padded - flat.shape[0])).reshape(-1, LANES))
    return jnp.concatenate(rows, axis=0)


def _unpack(pack, shapes):
    out, row = [], 0
    for shp in shapes:
        size = shp[0] * shp[1]
        tile = SUBLANES * LANES
        nrows = -(-size // tile) * SUBLANES
        out.append(pack[row:row + nrows].reshape(-1)[:size].reshape(shp))
        row += nrows
    return out


def kernel(x, ffn1_norm, ffn1_w_gate, ffn1_w_up, ffn1_w_down, mix_norm, w_in, q_norm, k_norm, conv_w, conv_b, conv_ln_g, conv_ln_b, w_out, ffn2_norm, ffn2_w_gate, ffn2_w_up, ffn2_w_down, loss_target, m_ffn1_norm, m_ffn1_w_gate, m_ffn1_w_up, m_ffn1_w_down, m_mix_norm, m_w_in, m_q_norm, m_k_norm, m_conv_w, m_conv_b, m_conv_ln_g, m_conv_ln_b, m_w_out, m_ffn2_norm, m_ffn2_w_gate, m_ffn2_w_up, m_ffn2_w_down, v_ffn1_norm, v_ffn1_w_gate, v_ffn1_w_up, v_ffn1_w_down, v_mix_norm, v_w_in, v_q_norm, v_k_norm, v_conv_w, v_conv_b, v_conv_ln_g, v_conv_ln_b, v_w_out, v_ffn2_norm, v_ffn2_w_gate, v_ffn2_w_up, v_ffn2_w_down):
    wts = dict(ffn1_norm=ffn1_norm, ffn1_w_gate=ffn1_w_gate[0], ffn1_w_up=ffn1_w_up[0], ffn1_w_down=ffn1_w_down[0],
               mix_norm=mix_norm, w_in=w_in[0], q_norm=q_norm, k_norm=k_norm, conv_w=conv_w[0], conv_b=conv_b,
               conv_ln_g=conv_ln_g, conv_ln_b=conv_ln_b, w_out=w_out[0], ffn2_norm=ffn2_norm,
               ffn2_w_gate=ffn2_w_gate[0], ffn2_w_up=ffn2_w_up[0], ffn2_w_down=ffn2_w_down[0])
    mom = dict(ffn1_norm=m_ffn1_norm, ffn1_w_gate=m_ffn1_w_gate[0], ffn1_w_up=m_ffn1_w_up[0], ffn1_w_down=m_ffn1_w_down[0],
               mix_norm=m_mix_norm, w_in=m_w_in[0], q_norm=m_q_norm, k_norm=m_k_norm, conv_w=m_conv_w[0], conv_b=m_conv_b,
               conv_ln_g=m_conv_ln_g, conv_ln_b=m_conv_ln_b, w_out=m_w_out[0], ffn2_norm=m_ffn2_norm,
               ffn2_w_gate=m_ffn2_w_gate[0], ffn2_w_up=m_ffn2_w_up[0], ffn2_w_down=m_ffn2_w_down[0])
    var = dict(ffn1_norm=v_ffn1_norm, ffn1_w_gate=v_ffn1_w_gate[0], ffn1_w_up=v_ffn1_w_up[0], ffn1_w_down=v_ffn1_w_down[0],
               mix_norm=v_mix_norm, w_in=v_w_in[0], q_norm=v_q_norm, k_norm=v_k_norm, conv_w=v_conv_w[0], conv_b=v_conv_b,
               conv_ln_g=v_conv_ln_g, conv_ln_b=v_conv_ln_b, w_out=v_w_out[0], ffn2_norm=v_ffn2_norm,
               ffn2_w_gate=v_ffn2_w_gate[0], ffn2_w_up=v_ffn2_w_up[0], ffn2_w_down=v_ffn2_w_down[0])
    chip = 2 * lax.axis_index("x") + lax.axis_index("y")
    for src in (wts, mom, var):
        for n in TRANSPOSED:
            src[n] = src[n].T

    wg1, wu1, wd1 = _gather_weights([wts["ffn1_w_gate"], wts["ffn1_w_up"], wts["ffn1_w_down"]], [BF16] * 3,
                                    name="gather_ffn1")
    taps = jnp.pad(wts["conv_w"], ((0, 1), (0, 0)))
    staged = _stage_shards([wts["w_in"], wts["w_out"], taps, wts["ffn2_w_gate"], wts["ffn2_w_up"], wts["ffn2_w_down"]],
                           [BF16, BF16, F32, BF16, BF16, BF16], name="stage_shards")
    w = dict(ffn1_norm=ffn1_norm, mix_norm=mix_norm, ffn2_norm=ffn2_norm, q_norm=q_norm, k_norm=k_norm,
             conv_b=conv_b, conv_ln_g=conv_ln_g, conv_ln_b=conv_ln_b, wg1=wg1, wu1=wu1, wd1=wd1)
    overlap = _Overlap(staged)
    loss_part, grad_x, g = _local_step(x, loss_target, w, overlap)

    late = _reduce_scatter([g["wg1"], g["wu1"], g["wd1"], g["win"]])
    big_grads = dict(zip(("ffn1_w_gate", "ffn1_w_up", "ffn1_w_down", "w_in"), late))
    big_grads.update(ffn2_w_gate=overlap.reduced["wg2"], ffn2_w_up=overlap.reduced["wu2"],
                     ffn2_w_down=overlap.reduced["wd2"], w_out=overlap.reduced["wout"])

    small_shapes = [g[n].shape for n in SMALL] + [(SUBLANES, LANES)]
    red = _allreduce_small(_pack([g[n] for n in SMALL] + [loss_part]), name="allreduce_small")
    small = dict(zip(SMALL + ("loss",), _unpack(red, small_shapes)))
    loss = small["loss"][0, 0]
    small["conv_w"] = lax.dynamic_slice_in_dim(small["conv_w"], chip * LANES, LANES, axis=1)

    grads, delta, new_m, new_v = {}, {}, {}, {}
    for n in BIG:
        grads[n] = big_grads[n]
        delta[n], new_m[n], new_v[n] = _adamw(wts[n], grads[n], mom[n], var[n], name=f"adamw_{n}")
    shapes = [wts[n].shape for n in SMALL]
    packs = [_pack([src[n] for n in SMALL]) for src in (wts, small, mom, var)]
    outs = _adamw(*packs, name="adamw_small")
    for dst, pk in zip((delta, new_m, new_v), outs):
        dst.update(zip(SMALL, _unpack(pk, shapes)))
    for n in SMALL:
        grads[n] = small[n]

    def shaped(d, n):
        v = d[n].T if n in TRANSPOSED else d[n]
        return v.reshape((1,) + v.shape) if n in BIG or n == "conv_w" else v

    return (loss, grad_x, *[shaped(grads, n) for n in WEIGHTS], *[shaped(delta, n) for n in WEIGHTS],
            *[shaped(new_m, n) for n in WEIGHTS], *[shaped(new_v, n) for n in WEIGHTS])
```

```python
import functools

import jax
import jax.numpy as jnp
from jax import lax
from jax.experimental import pallas as pl
from jax.experimental.pallas import tpu as pltpu

F32 = jnp.float32
BF16 = jnp.bfloat16

EPS = 1e-6
HEADS = 8
HEAD_DIM = 64
D_ATTN = HEADS * HEAD_DIM
D_CONV = 512
CONV_K = 31
QBLK = 128
N_PATTERNS = 3
DILATIONS = (1, 4, 16)
LANES = 128
NEG = -1e30

ADAM_LR = 0.001
ADAM_B1 = 0.9
ADAM_B2 = 0.999
ADAM_EPS = 1e-08
ADAM_WD = 0.01
ADAM_STEP = 10

VMEM_LIMIT = 56 * 1024 * 1024
MESH = pl.DeviceIdType.MESH

NT_DIMS = (((1,), (1,)), ((), ()))
TN_DIMS = (((0,), (0,)), ((), ()))


def _params(*sem):
    return pltpu.CompilerParams(dimension_semantics=sem, vmem_limit_bytes=VMEM_LIMIT)


def _dot(a, b):
    return jnp.dot(a, b, preferred_element_type=F32)


def _dot_nt(a, b):
    return lax.dot_general(a, b, NT_DIMS, preferred_element_type=F32)


def _dot_tn(a, b):
    return lax.dot_general(a, b, TN_DIMS, preferred_element_type=F32)


def _sigmoid(x):
    return 1.0 / (1.0 + jnp.exp(-x))


def _seg_mean(v, e_ref, width):
    hi = v.astype(BF16)
    lo = (v - hi.astype(F32)).astype(BF16)
    e = e_ref[...]
    return (_dot(hi, e) + _dot(lo, e)) * (1.0 / width)


def _seg_matrix(n):
    i = jnp.arange(n)
    return (i[:, None] // HEAD_DIM == i[None, :] // HEAD_DIM).astype(BF16)


ANY = pl.BlockSpec(memory_space=pl.ANY)
DMA_SEMS = pltpu.SemaphoreType.DMA


class _Phase:
    def __init__(self, ins, outs, aliases, nsem, copies):
        self.ins, self.outs, self.aliases = list(ins), list(outs), dict(aliases)
        self.stages = [(nsem, copies)]

    def then(self, other):
        self.stages = self.stages + other.stages
        return self

    @property
    def nsem(self):
        return sum(n for n, _ in self.stages)

    def _copies(self, k, in_refs, out_refs, send_sems, recv_sems):
        base = sum(n for n, _ in self.stages[:k])
        return self.stages[k][1](in_refs, out_refs, lambda i: (send_sems.at[base + i], recv_sems.at[base + i]))

    def start(self, k, *refs):
        for cp in self._copies(k, *refs)[0]:
            cp.start()

    def finish(self, k, *refs):
        starts, arrivals = self._copies(k, *refs)
        for cp in arrivals:
            cp.wait_recv()
        for cp in starts:
            cp.wait_send()


def _run_phase(phase, *, name):
    n_in, n_out = len(phase.ins), len(phase.outs)

    def body(*refs):
        ins, outs = refs[:n_in], refs[n_in:n_in + n_out]
        send_sems, recv_sems = refs[n_in + n_out:]
        for k in range(len(phase.stages)):
            phase.start(k, ins, outs, send_sems, recv_sems)
            phase.finish(k, ins, outs, send_sems, recv_sems)

    return pl.pallas_call(
        body, in_specs=[ANY] * n_in, out_specs=[ANY] * n_out, out_shape=phase.outs,
        input_output_aliases=phase.aliases,
        scratch_shapes=[DMA_SEMS((phase.nsem,)), DMA_SEMS((phase.nsem,))], name=name)(*phase.ins)


def _call(body, *, grid, in_specs, out_specs, out_shape, scratch_shapes=(), sem, name, args, phase=None):
    in_specs, out_specs, out_shape = list(in_specs), list(out_specs), list(out_shape)
    scratch_shapes = list(scratch_shapes)
    if phase is None:
        return pl.pallas_call(body, grid=grid, in_specs=in_specs, out_specs=out_specs, out_shape=out_shape,
                              scratch_shapes=scratch_shapes, compiler_params=_params(*sem), name=name)(*args)
    n_in, n_out, n_scr = len(in_specs), len(out_specs), len(scratch_shapes)
    p_in, p_out = len(phase.ins), len(phase.outs)

    def hosted(*refs):
        ins, pins = refs[:n_in], refs[n_in:n_in + p_in]
        o0 = n_in + p_in
        outs, pouts = refs[o0:o0 + n_out], refs[o0 + n_out:o0 + n_out + p_out]
        s0 = o0 + n_out + p_out
        scr = refs[s0:s0 + n_scr]
        send_sems, recv_sems = refs[s0 + n_scr:]
        step = 0
        for d, n in enumerate(grid):
            step = step * n + pl.program_id(d)
        nsteps = functools.reduce(lambda a, b: a * b, grid)
        nstages = len(phase.stages)
        comm_refs = (pins, pouts, send_sems, recv_sems)

        for k in range(nstages):
            @pl.when(step == (k * nsteps) // nstages)
            def _(k=k):
                if k > 0:
                    phase.finish(k - 1, *comm_refs)
                phase.start(k, *comm_refs)

        body(*ins, *outs, *scr)

        @pl.when(step == nsteps - 1)
        def _():
            phase.finish(nstages - 1, *comm_refs)

    res = pl.pallas_call(
        hosted, grid=grid, in_specs=in_specs + [ANY] * p_in, out_specs=out_specs + [ANY] * p_out,
        out_shape=out_shape + phase.outs,
        input_output_aliases={n_in + i: n_out + o for i, o in phase.aliases.items()},
        scratch_shapes=scratch_shapes + [DMA_SEMS((phase.nsem,)), DMA_SEMS((phase.nsem,))],
        compiler_params=_params(*sem), name=name)(*args, *phase.ins)
    return res[:n_out], res[n_out:]


def _ffn_fwd(x, gain, wg, wu, wd, tgt, *, tm, name, phase=None):
    T, D = x.shape
    NS, Fs, _ = wg.shape
    with_loss = tgt is not None

    def body(*refs):
        if with_loss:
            x_ref, g_ref, wg_ref, wu_ref, wd_ref, t_ref, h_ref, n_ref, G_ref, U_ref, loss_ref, acc_ref = refs
        else:
            x_ref, g_ref, wg_ref, wu_ref, wd_ref, h_ref, n_ref, G_ref, U_ref, acc_ref = refs
        i = pl.program_id(0)
        j = pl.program_id(1)

        @pl.when(j == 0)
        def _():
            xv = x_ref[...]
            r = lax.rsqrt(jnp.mean(xv * xv, axis=-1, keepdims=True) + EPS)
            n_ref[...] = (xv * r * g_ref[...]).astype(BF16)
            acc_ref[...] = jnp.zeros_like(acc_ref)

        n = n_ref[...]
        G = _dot_nt(n, wg_ref[...])
        U = _dot_nt(n, wu_ref[...])
        G_ref[...] = G.astype(BF16)
        U_ref[...] = U.astype(BF16)
        A = (G * _sigmoid(G) * U).astype(BF16)
        acc_ref[...] += _dot(A, wd_ref[...])

        @pl.when(j == NS - 1)
        def _():
            h = x_ref[...] + 0.5 * acc_ref[...]
            if with_loss:
                e = h - t_ref[...]
                h_ref[...] = e * (1.0 / D)

                @pl.when(i == 0)
                def _():
                    loss_ref[...] = jnp.zeros_like(loss_ref)

                loss_ref[...] += jnp.sum(e * e) * (0.5 / D)
            else:
                h_ref[...] = h

    tok = pl.BlockSpec((tm, D), lambda i, j: (i, 0))
    in_specs = [tok, pl.BlockSpec((1, D), lambda i, j: (0, 0)),
                pl.BlockSpec((None, Fs, D), lambda i, j: (j, 0, 0)),
                pl.BlockSpec((None, Fs, D), lambda i, j: (j, 0, 0)),
                pl.BlockSpec((None, Fs, D), lambda i, j: (j, 0, 0))]
    args = [x, gain, wg, wu, wd]
    act = pl.BlockSpec((None, tm, Fs), lambda i, j: (j, i, 0))
    out_shape = [jax.ShapeDtypeStruct((T, D), F32), jax.ShapeDtypeStruct((T, D), BF16),
                 jax.ShapeDtypeStruct((NS, T, Fs), BF16), jax.ShapeDtypeStruct((NS, T, Fs), BF16)]
    out_specs = [tok, tok, act, act]
    if with_loss:
        in_specs.append(tok)
        args.append(tgt)
        out_shape.append(jax.ShapeDtypeStruct((8, LANES), F32))
        out_specs.append(pl.BlockSpec((8, LANES), lambda i, j: (0, 0)))
    return _call(body, grid=(T // tm, NS), in_specs=in_specs, out_specs=out_specs, out_shape=out_shape,
                 scratch_shapes=[pltpu.VMEM((tm, D), F32)], sem=("arbitrary", "arbitrary"), name=name,
                 args=args, phase=phase)


def _rms_bwd(xv, gain, dn):
    r = lax.rsqrt(jnp.mean(xv * xv, axis=-1, keepdims=True) + EPS)
    xhat = xv * r
    dxh = dn * gain
    dx = r * (dxh - xhat * jnp.mean(dxh * xhat, axis=-1, keepdims=True))
    dg = jnp.sum(dn * xhat, axis=0, keepdims=True)
    return dx, dg


def _ffn_bwd_act(dh, x, gain, G, U, wg, wu, wd, *, tm, name):
    T, D = x.shape
    NS, Fs, _ = wg.shape

    def body(dh_ref, x_ref, g_ref, G_ref, U_ref, wg_ref, wu_ref, wd_ref,
             dG_ref, dU_ref, A_ref, dy_ref, dx_ref, dg_ref, acc_ref):
        i = pl.program_id(0)
        j = pl.program_id(1)

        @pl.when(j == 0)
        def _():
            dy_ref[...] = (0.5 * dh_ref[...]).astype(BF16)
            acc_ref[...] = jnp.zeros_like(acc_ref)

        @pl.when((i == 0) & (j == 0))
        def _():
            dg_ref[...] = jnp.zeros_like(dg_ref)

        Gv = G_ref[...].astype(F32)
        Uv = U_ref[...].astype(F32)
        sig = _sigmoid(Gv)
        s = Gv * sig
        dA = _dot_nt(dy_ref[...], wd_ref[...])
        dG = (dA * Uv * (sig * (1.0 + Gv * (1.0 - sig)))).astype(BF16)
        dU = (dA * s).astype(BF16)
        dG_ref[...] = dG
        dU_ref[...] = dU
        A_ref[...] = (s * Uv).astype(BF16)
        acc_ref[...] += _dot(dG, wg_ref[...]) + _dot(dU, wu_ref[...])

        @pl.when(j == NS - 1)
        def _():
            dx, dg = _rms_bwd(x_ref[...], g_ref[...], acc_ref[...])
            dx_ref[...] = dh_ref[...] + dx
            dg_ref[...] += dg

    tok = pl.BlockSpec((tm, D), lambda i, j: (i, 0))
    act = pl.BlockSpec((None, tm, Fs), lambda i, j: (j, i, 0))
    vec = pl.BlockSpec((1, D), lambda i, j: (0, 0))
    return pl.pallas_call(
        body, grid=(T // tm, NS),
        in_specs=[tok, tok, vec, act, act,
                  pl.BlockSpec((None, Fs, D), lambda i, j: (j, 0, 0)),
                  pl.BlockSpec((None, Fs, D), lambda i, j: (j, 0, 0)),
                  pl.BlockSpec((None, Fs, D), lambda i, j: (j, 0, 0))],
        out_specs=[act, act, act, tok, tok, vec],
        out_shape=[jax.ShapeDtypeStruct((NS, T, Fs), BF16)] * 3
        + [jax.ShapeDtypeStruct((T, D), BF16), jax.ShapeDtypeStruct((T, D), F32),
           jax.ShapeDtypeStruct((1, D), F32)],
        scratch_shapes=[pltpu.VMEM((tm, D), F32)],
        compiler_params=_params("arbitrary", "arbitrary"), name=name)(dh, x, gain, G, U, wg, wu, wd)


def _ffn_bwd_w(n, dy, dG, dU, A, *, tk, name):
    T, D = n.shape
    NS, _, Fs = dG.shape

    def body(n_ref, dy_ref, dG_ref, dU_ref, A_ref, wg_ref, wu_ref, wd_ref):
        @pl.when(pl.program_id(1) == 0)
        def _():
            wg_ref[...] = jnp.zeros_like(wg_ref)
            wu_ref[...] = jnp.zeros_like(wu_ref)
            wd_ref[...] = jnp.zeros_like(wd_ref)

        nv = n_ref[...]
        wg_ref[...] += _dot_tn(dG_ref[...], nv)
        wu_ref[...] += _dot_tn(dU_ref[...], nv)
        wd_ref[...] += _dot_tn(A_ref[...], dy_ref[...])

    tok = pl.BlockSpec((tk, D), lambda j, k: (k, 0))
    act = pl.BlockSpec((None, tk, Fs), lambda j, k: (j, k, 0))
    return pl.pallas_call(
        body, grid=(NS, T // tk), in_specs=[tok, tok, act, act, act],
        out_specs=[pl.BlockSpec((None, Fs, D), lambda j, k: (j, 0, 0))] * 3,
        out_shape=[jax.ShapeDtypeStruct((NS, Fs, D), F32)] * 3,
        compiler_params=_params("arbitrary", "arbitrary"), name=name)(n, dy, dG, dU, A)


def _inproj_fwd(h, gain, win, *, tm, name, phase=None):
    T, D = h.shape
    NS, _, Cs = win.shape

    def body(h_ref, g_ref, w_ref, u_ref, n_ref):
        @pl.when(pl.program_id(1) == 0)
        def _():
            xv = h_ref[...]
            r = lax.rsqrt(jnp.mean(xv * xv, axis=-1, keepdims=True) + EPS)
            n_ref[...] = (xv * r * g_ref[...]).astype(BF16)

        u_ref[...] = _dot(n_ref[...], w_ref[...])

    tok = pl.BlockSpec((tm, D), lambda i, j: (i, 0))
    return _call(
        body, grid=(T // tm, NS),
        in_specs=[tok, pl.BlockSpec((1, D), lambda i, j: (0, 0)),
                  pl.BlockSpec((None, D, Cs), lambda i, j: (j, 0, 0))],
        out_specs=[pl.BlockSpec((tm, Cs), lambda i, j: (i, j)), tok],
        out_shape=[jax.ShapeDtypeStruct((T, NS * Cs), F32), jax.ShapeDtypeStruct((T, D), BF16)],
        sem=("arbitrary", "arbitrary"), name=name, args=(h, gain, win), phase=phase)


def _inproj_bwd_act(du, dh, h, gain, win, *, tm, name, phase=None):
    T, D = h.shape
    NS, _, Cs = win.shape

    def body(du_ref, dh_ref, h_ref, g_ref, w_ref, dx_ref, dg_ref, acc_ref):
        i = pl.program_id(0)
        j = pl.program_id(1)

        @pl.when(j == 0)
        def _():
            acc_ref[...] = jnp.zeros_like(acc_ref)

        @pl.when((i == 0) & (j == 0))
        def _():
            dg_ref[...] = jnp.zeros_like(dg_ref)

        acc_ref[...] += _dot_nt(du_ref[...], w_ref[...])

        @pl.when(j == NS - 1)
        def _():
            dx, dg = _rms_bwd(h_ref[...], g_ref[...], acc_ref[...])
            dx_ref[...] = dh_ref[...] + dx
            dg_ref[...] += dg

    tok = pl.BlockSpec((tm, D), lambda i, j: (i, 0))
    vec = pl.BlockSpec((1, D), lambda i, j: (0, 0))
    return _call(
        body, grid=(T // tm, NS),
        in_specs=[pl.BlockSpec((tm, Cs), lambda i, j: (i, j)), tok, tok, vec,
                  pl.BlockSpec((None, D, Cs), lambda i, j: (j, 0, 0))],
        out_specs=[tok, vec],
        out_shape=[jax.ShapeDtypeStruct((T, D), F32), jax.ShapeDtypeStruct((1, D), F32)],
        scratch_shapes=[pltpu.VMEM((tm, D), F32)],
        sem=("arbitrary", "arbitrary"), name=name, args=(du, dh, h, gain, win), phase=phase)


def _inproj_bwd_w(n, du, ns, *, tk, name):
    T, D = n.shape
    Cs = du.shape[1] // ns

    def body(n_ref, du_ref, w_ref):
        @pl.when(pl.program_id(1) == 0)
        def _():
            w_ref[...] = jnp.zeros_like(w_ref)

        w_ref[...] += _dot_tn(n_ref[...], du_ref[...])

    return pl.pallas_call(
        body, grid=(ns, T // tk),
        in_specs=[pl.BlockSpec((tk, D), lambda j, k: (k, 0)), pl.BlockSpec((tk, Cs), lambda j, k: (k, j))],
        out_specs=pl.BlockSpec((None, D, Cs), lambda j, k: (j, 0, 0)),
        out_shape=jax.ShapeDtypeStruct((ns, D, Cs), F32),
        compiler_params=_params("arbitrary", "arbitrary"), name=name)(n, du)


def _permute_out(src_ref, out_ref, cast):
    S = src_ref.shape[1]
    for p, d in enumerate(DILATIONS):
        L = S // d
        for cc in range(4):
            cols = slice(cc * LANES, (cc + 1) * LANES)
            if d == 1:
                out_ref[p, :, cols] = src_ref[cc].astype(cast)
            else:
                for r in range(d):
                    out_ref[p, r * L:(r + 1) * L, cols] = src_ref[cc, pl.ds(r, L, stride=d), :].astype(cast)


def _unpermute_in(get_block, dst_ref, p, S):
    d = DILATIONS[p]
    L = S // d
    if d == 1:
        dst_ref[...] = get_block(0, S)
    else:
        for r in range(d):
            dst_ref[pl.ds(r, L, stride=d), :] = get_block(r * L, L)


def _qkv_prep(u, gains, B, S, *, name):
    emat = _seg_matrix(D_ATTN)

    def body(u_ref, g_ref, e_ref, out_ref, scr_ref):
        c = pl.program_id(1)
        xv = u_ref[...]
        ms = _seg_mean(xv * xv, e_ref, HEAD_DIM)
        r = jnp.where(c < 2, lax.rsqrt(ms + EPS), 1.0)
        yv = xv * r * g_ref[...]
        for cc in range(4):
            scr_ref[cc] = yv[:, cc * LANES:(cc + 1) * LANES]
        _permute_out(scr_ref, out_ref, BF16)

    return pl.pallas_call(
        body, grid=(B, 3),
        in_specs=[pl.BlockSpec((S, D_ATTN), lambda b, c: (b, c)),
                  pl.BlockSpec((None, 1, D_ATTN), lambda b, c: (c, 0, 0)),
                  pl.BlockSpec((D_ATTN, D_ATTN), lambda b, c: (0, 0))],
        out_specs=pl.BlockSpec((None, N_PATTERNS, None, S, D_ATTN), lambda b, c: (c, 0, b, 0, 0)),
        out_shape=jax.ShapeDtypeStruct((3, N_PATTERNS, B, S, D_ATTN), BF16),
        scratch_shapes=[pltpu.VMEM((4, S, LANES), F32)],
        compiler_params=_params("arbitrary", "arbitrary"), name=name)(u, gains, emat)


def _band_mask(p, b):
    nblk = jnp.right_shift(16, 2 * p)
    has_prev = jnp.bitwise_and(b, nblk - 1) != 0
    qi = lax.broadcasted_iota(jnp.int32, (QBLK, 2 * QBLK), 0)
    ci = lax.broadcasted_iota(jnp.int32, (QBLK, 2 * QBLK), 1)
    dist = QBLK + qi - ci
    return (dist >= 0) & (dist <= QBLK) & (has_prev | (ci >= QBLK))


def _first_head(rows):
    return lax.broadcasted_iota(jnp.int32, (rows, LANES), 1) < HEAD_DIM


def _split_heads(pair):
    first = _first_head(pair.shape[0])
    zero = jnp.zeros_like(pair)
    return jnp.concatenate([jnp.where(first, pair, zero), jnp.where(first, zero, pair)], axis=0)


def _merge_heads(col_a, col_b):
    rows = col_a.shape[0]
    return jnp.where(_first_head(rows), jnp.broadcast_to(col_a, (rows, LANES)), jnp.broadcast_to(col_b, (rows, LANES)))


def _attn_specs(nb):
    blk = (None, None, None, QBLK, D_ATTN)
    q_spec = pl.BlockSpec(blk, lambda p, b: (0, p, b, 0, 0))
    kp_spec = pl.BlockSpec(blk, lambda p, b: (1, p, jnp.maximum(b - 1, 0), 0, 0))
    kc_spec = pl.BlockSpec(blk, lambda p, b: (1, p, b, 0, 0))
    vp_spec = pl.BlockSpec(blk, lambda p, b: (2, p, jnp.maximum(b - 1, 0), 0, 0))
    vc_spec = pl.BlockSpec(blk, lambda p, b: (2, p, b, 0, 0))
    return [q_spec, kp_spec, kc_spec, vp_spec, vc_spec]


def _attn_fwd(qkv, *, name):
    nb = qkv.shape[2]

    def body(q_ref, kp_ref, kc_ref, vp_ref, vc_ref, o_ref, lse_ref):
        mask = _band_mask(pl.program_id(0), pl.program_id(1))
        mask2 = jnp.concatenate([mask, mask], axis=0)
        kk = jnp.concatenate([kp_ref[...], kc_ref[...]], axis=0)
        vv = jnp.concatenate([vp_ref[...], vc_ref[...]], axis=0)
        for hp in range(HEADS // 2):
            cols = slice(hp * LANES, (hp + 1) * LANES)
            s = _dot_nt(_split_heads(q_ref[:, cols]), kk[:, cols])
            s = jnp.where(mask2, s, NEG)
            m = jnp.max(s, axis=-1, keepdims=True)
            e = jnp.exp(s - m)
            l = jnp.sum(e, axis=-1, keepdims=True)
            pr = (e * (1.0 / l)).astype(BF16)
            o_ref[:, cols] = _dot(jnp.concatenate([pr[:QBLK], pr[QBLK:]], axis=1), _split_heads(vv[:, cols]))
            lse = m + jnp.log(l)
            lse_ref[:, cols] = _merge_heads(lse[:QBLK], lse[QBLK:])

    out = pl.BlockSpec((None, None, QBLK, D_ATTN), lambda p, b: (p, b, 0, 0))
    return pl.pallas_call(
        body, grid=(N_PATTERNS, nb), in_specs=_attn_specs(nb), out_specs=[out, out],
        out_shape=[jax.ShapeDtypeStruct((N_PATTERNS, nb, QBLK, D_ATTN), F32)] * 2,
        compiler_params=_params("arbitrary", "arbitrary"), name=name)(qkv, qkv, qkv, qkv, qkv)


def _attn_combine(o3, lse3, B, S, *, name):
    def body(o_ref, l_ref, a_ref, lt_ref, so_ref, sl_ref):
        for p in range(N_PATTERNS):
            _unpermute_in(lambda r0, n, p=p: o_ref[p, pl.ds(r0, n), :], so_ref.at[p], p, S)
            _unpermute_in(lambda r0, n, p=p: l_ref[p, pl.ds(r0, n), :], sl_ref.at[p], p, S)
        l0, l1, l2 = sl_ref[0], sl_ref[1], sl_ref[2]
        m = jnp.maximum(jnp.maximum(l0, l1), l2)
        w0, w1, w2 = jnp.exp(l0 - m), jnp.exp(l1 - m), jnp.exp(l2 - m)
        tot = w0 + w1 + w2
        a_ref[...] = (w0 * so_ref[0] + w1 * so_ref[1] + w2 * so_ref[2]) / tot
        lt_ref[...] = m + jnp.log(tot)

    o3 = o3.reshape(N_PATTERNS, B, S, D_ATTN)
    lse3 = lse3.reshape(N_PATTERNS, B, S, D_ATTN)
    inp = pl.BlockSpec((N_PATTERNS, None, S, LANES), lambda b, c: (0, b, 0, c))
    out = pl.BlockSpec((S, LANES), lambda b, c: (b, c))
    return pl.pallas_call(
        body, grid=(B, D_ATTN // LANES), in_specs=[inp, inp], out_specs=[out, out],
        out_shape=[jax.ShapeDtypeStruct((B * S, D_ATTN), F32)] * 2,
        scratch_shapes=[pltpu.VMEM((N_PATTERNS, S, LANES), F32)] * 2,
        compiler_params=_params("arbitrary", "arbitrary"), name=name)(o3, lse3)


def _attn_bwd_prep(dattn, attn, lse, B, S, *, name):
    emat = _seg_matrix(LANES)

    def body(da_ref, a_ref, l_ref, e_ref, do_ref, lo_ref, dd_ref, scr_ref):
        da = da_ref[...]
        dsum = _seg_mean(da * a_ref[...], e_ref, 1.0)
        for k, (val, out_ref, cast) in enumerate(((da, do_ref, BF16), (l_ref[...], lo_ref, F32), (dsum, dd_ref, F32))):
            scr_ref[...] = val
            for p, d in enumerate(DILATIONS):
                L = S // d
                if d == 1:
                    out_ref[p] = val.astype(cast)
                else:
                    for r in range(d):
                        out_ref[p, r * L:(r + 1) * L, :] = scr_ref[pl.ds(r, L, stride=d), :].astype(cast)

    inp = pl.BlockSpec((S, LANES), lambda b, c: (b, c))
    out = pl.BlockSpec((N_PATTERNS, None, S, LANES), lambda b, c: (0, b, 0, c))
    shp = (N_PATTERNS, B, S, D_ATTN)
    return pl.pallas_call(
        body, grid=(B, D_ATTN // LANES),
        in_specs=[inp, inp, inp, pl.BlockSpec((LANES, LANES), lambda b, c: (0, 0))],
        out_specs=[out, out, out],
        out_shape=[jax.ShapeDtypeStruct(shp, BF16), jax.ShapeDtypeStruct(shp, F32), jax.ShapeDtypeStruct(shp, F32)],
        scratch_shapes=[pltpu.VMEM((S, LANES), F32)],
        compiler_params=_params("arbitrary", "arbitrary"), name=name)(dattn, attn, lse, emat)


def _attn_bwd(qkv, do3, lse3, dd3, *, name, phase=None):
    nb = qkv.shape[2]

    def body(q_ref, kp_ref, kc_ref, vp_ref, vc_ref, do_ref, l_ref, d_ref, cur_ref, prev_ref):
        mask = _band_mask(pl.program_id(0), pl.program_id(1))
        mask2 = jnp.concatenate([mask, mask], axis=0)
        kk = jnp.concatenate([kp_ref[...], kc_ref[...]], axis=0)
        vv = jnp.concatenate([vp_ref[...], vc_ref[...]], axis=0)
        for hp in range(HEADS // 2):
            cols = slice(hp * LANES, (hp + 1) * LANES)
            a, b = hp * LANES, hp * LANES + HEAD_DIM
            kh, vh = kk[:, cols], vv[:, cols]
            q2 = _split_heads(q_ref[:, cols])
            do2 = _split_heads(do_ref[:, cols])
            lse = jnp.concatenate([l_ref[:, a:a + 1], l_ref[:, b:b + 1]], axis=0)
            dsum = jnp.concatenate([d_ref[:, a:a + 1], d_ref[:, b:b + 1]], axis=0)
            s = _dot_nt(q2, kh)
            pr = jnp.where(mask2, jnp.exp(s - lse), 0.0)
            dp = _dot_nt(do2, vh)
            ds = (pr * (dp - dsum)).astype(BF16)
            prb = pr.astype(BF16)
            cur_ref[0, :, cols] = _dot(jnp.concatenate([ds[:QBLK], ds[QBLK:]], axis=1), _split_heads(kh))
            dk = _dot_tn(ds, q2)
            dv = _dot_tn(prb, do2)
            prev_ref[0, :, cols] = dk[:QBLK]
            cur_ref[1, :, cols] = dk[QBLK:]
            prev_ref[1, :, cols] = dv[:QBLK]
            cur_ref[2, :, cols] = dv[QBLK:]

    aux = pl.BlockSpec((None, None, QBLK, D_ATTN), lambda p, b: (p, b, 0, 0))
    return _call(
        body, grid=(N_PATTERNS, nb), in_specs=_attn_specs(nb) + [aux, aux, aux],
        out_specs=[pl.BlockSpec((3, None, None, QBLK, D_ATTN), lambda p, b: (0, p, b, 0, 0)),
                   pl.BlockSpec((2, None, None, QBLK, D_ATTN), lambda p, b: (0, p, b, 0, 0))],
        out_shape=[jax.ShapeDtypeStruct((3, N_PATTERNS, nb, QBLK, D_ATTN), F32),
                   jax.ShapeDtypeStruct((2, N_PATTERNS, nb, QBLK, D_ATTN), F32)],
        sem=("arbitrary", "arbitrary"), name=name, args=(qkv, qkv, qkv, qkv, qkv, do3, lse3, dd3), phase=phase)


def _attn_grad_combine(cur, prev, u, gains, B, S, *, name, phase=None):
    emat = _seg_matrix(LANES)
    nblk = S // QBLK

    def body(cur_ref, prev_ref, u_ref, g_ref, e_ref, du_ref, dg_ref, scr_ref):
        c = pl.program_id(0)
        b = pl.program_id(2)
        use_prev = c > 0
        total = None
        for p, d in enumerate(DILATIONS):
            per_seq = nblk // d

            def get_block(r0, n, p=p, per_seq=per_seq):
                parts = []
                for blk in range(r0 // QBLK, (r0 + n) // QBLK):
                    v = cur_ref[p, pl.ds(blk * QBLK, QBLK), :]
                    if blk % per_seq != per_seq - 1:
                        nxt = prev_ref[p, pl.ds((blk + 1) * QBLK, QBLK), :]
                        v = v + jnp.where(use_prev, nxt, 0.0)
                    parts.append(v)
                return parts[0] if len(parts) == 1 else jnp.concatenate(parts, axis=0)

            _unpermute_in(get_block, scr_ref.at[p], p, S)
        dy = scr_ref[0] + scr_ref[1] + scr_ref[2]
        xv = u_ref[...]
        gain = g_ref[...]
        ms = _seg_mean(xv * xv, e_ref, HEAD_DIM)
        r = lax.rsqrt(ms + EPS)
        xhat = xv * r
        dxh = dy * gain
        dx = r * (dxh - xhat * _seg_mean(dxh * xhat, e_ref, HEAD_DIM))
        du_ref[...] = jnp.where(c < 2, dx, dy).astype(BF16)

        @pl.when((b == 0))
        def _():
            dg_ref[...] = jnp.zeros_like(dg_ref)

        dg_ref[...] += jnp.sum(dy * xhat, axis=0, keepdims=True)

    cur = cur.reshape(3, N_PATTERNS, B, S, D_ATTN)
    prev = prev.reshape(2, N_PATTERNS, B, S, D_ATTN)
    ncc = D_ATTN // LANES
    return _call(
        body, grid=(3, ncc, B),
        in_specs=[pl.BlockSpec((None, N_PATTERNS, None, S, LANES), lambda c, cc, b: (c, 0, b, 0, cc)),
                  pl.BlockSpec((None, N_PATTERNS, None, S, LANES), lambda c, cc, b: (jnp.maximum(c - 1, 0), 0, b, 0, cc)),
                  pl.BlockSpec((S, LANES), lambda c, cc, b: (b, c * ncc + cc)),
                  pl.BlockSpec((None, 1, LANES), lambda c, cc, b: (c, 0, cc)),
                  pl.BlockSpec((LANES, LANES), lambda c, cc, b: (0, 0))],
        out_specs=[pl.BlockSpec((S, LANES), lambda c, cc, b: (b, c * ncc + cc)),
                   pl.BlockSpec((None, 1, LANES), lambda c, cc, b: (c, 0, cc))],
        out_shape=[jax.ShapeDtypeStruct((B * S, 3 * D_ATTN), BF16), jax.ShapeDtypeStruct((3, 1, D_ATTN), F32)],
        scratch_shapes=[pltpu.VMEM((N_PATTERNS, S, LANES), F32)],
        sem=("arbitrary", "arbitrary", "arbitrary"), name=name, args=(cur, prev, u, gains, emat), phase=phase)


HALO = 32
SUB = 64


def _conv_fwd(u, cw, cb, lg, lb, B, S, *, tc, name):
    nchunk = S // tc
    hb = tc // HALO

    def body(ca_ref, cap_ref, cg_ref, cgp_ref, w_ref, cb_ref, lg_ref, lb_ref, cv_ref, glu_ref, y_ref, pad_ref):
        i = pl.program_id(1)
        glu = ca_ref[...] * _sigmoid(cg_ref[...])
        glu_ref[...] = glu
        prev = cap_ref[...] * _sigmoid(cgp_ref[...])
        pad_ref[0:HALO, :] = jnp.where(i > 0, prev, 0.0)
        pad_ref[HALO:, :] = glu
        for sub in range(tc // SUB):
            acc = jnp.zeros((SUB, D_CONV), F32) + cb_ref[...]
            for k in range(CONV_K):
                acc = acc + pad_ref[pl.ds(sub * SUB + HALO - (CONV_K - 1) + k, SUB), :] * w_ref[pl.ds(k, 1), :]
            y_ref[sub * SUB:(sub + 1) * SUB, :] = acc
        y = y_ref[...]
        mu = jnp.mean(y, axis=-1, keepdims=True)
        yc = y - mu
        var = jnp.mean(yc * yc, axis=-1, keepdims=True)
        z = yc * lax.rsqrt(var + EPS) * lg_ref[...] + lb_ref[...]
        cv_ref[...] = (z * _sigmoid(z)).astype(BF16)

    def cur(col):
        return pl.BlockSpec((tc, D_CONV), lambda b, i: (b * nchunk + i, col))

    def halo(col):
        return pl.BlockSpec((HALO, D_CONV), lambda b, i: (jnp.maximum((b * nchunk + i) * hb - 1, 0), col))

    vec = pl.BlockSpec((1, D_CONV), lambda b, i: (0, 0))
    out = pl.BlockSpec((tc, D_CONV), lambda b, i: (b * nchunk + i, 0))
    return pl.pallas_call(
        body, grid=(B, nchunk),
        in_specs=[cur(3), halo(3), cur(4), halo(4), pl.BlockSpec((CONV_K, D_CONV), lambda b, i: (0, 0)), vec, vec, vec],
        out_specs=[out, out, out],
        out_shape=[jax.ShapeDtypeStruct((B * S, D_CONV), BF16), jax.ShapeDtypeStruct((B * S, D_CONV), F32),
                   jax.ShapeDtypeStruct((B * S, D_CONV), F32)],
        scratch_shapes=[pltpu.VMEM((tc + HALO, D_CONV), F32)],
        compiler_params=_params("arbitrary", "arbitrary"), name=name)(u, u, u, u, cw, cb, lg, lb)


def _conv_bwd_norm(dcv, y, lg, lb, *, tc, name):
    T = y.shape[0]

    def body(dcv_ref, y_ref, lg_ref, lb_ref, dy_ref, part_ref):
        yv = y_ref[...]
        mu = jnp.mean(yv, axis=-1, keepdims=True)
        yc = yv - mu
        var = jnp.mean(yc * yc, axis=-1, keepdims=True)
        rstd = lax.rsqrt(var + EPS)
        xhat = yc * rstd
        z = xhat * lg_ref[...] + lb_ref[...]
        sig = _sigmoid(z)
        dz = dcv_ref[...] * (sig * (1.0 + z * (1.0 - sig)))
        dxh = dz * lg_ref[...]
        dy = rstd * (dxh - jnp.mean(dxh, axis=-1, keepdims=True)
                     - xhat * jnp.mean(dxh * xhat, axis=-1, keepdims=True))
        dy_ref[...] = dy

        @pl.when(pl.program_id(0) == 0)
        def _():
            part_ref[...] = jnp.zeros_like(part_ref)

        part_ref[0:1, :] += jnp.sum(dz * xhat, axis=0, keepdims=True)
        part_ref[1:2, :] += jnp.sum(dz, axis=0, keepdims=True)
        part_ref[2:3, :] += jnp.sum(dy, axis=0, keepdims=True)

    tok = pl.BlockSpec((tc, D_CONV), lambda i: (i, 0))
    vec = pl.BlockSpec((1, D_CONV), lambda i: (0, 0))
    return pl.pallas_call(
        body, grid=(T // tc,), in_specs=[tok, tok, vec, vec],
        out_specs=[tok, pl.BlockSpec((8, D_CONV), lambda i: (0, 0))],
        out_shape=[jax.ShapeDtypeStruct((T, D_CONV), F32), jax.ShapeDtypeStruct((8, D_CONV), F32)],
        compiler_params=_params("arbitrary"), name=name)(dcv, y, lg, lb)


def _conv_bwd_taps(dy, glu, u, cw, B, S, *, tc, name, phase=None):
    nchunk = S // tc
    hb = tc // HALO
    last_hb = B * S // HALO - 1

    def body(dy_ref, dyn_ref, glu_ref, glup_ref, ca_ref, cg_ref, w_ref, dca_ref, dcg_ref, dw_ref, dyp_ref, glp_ref, acc_ref):
        b = pl.program_id(0)
        i = pl.program_id(1)
        dy = dy_ref[...]
        dyp_ref[0:tc, :] = dy
        dyp_ref[tc:, :] = jnp.where(i < nchunk - 1, dyn_ref[...], 0.0)
        glp_ref[0:HALO, :] = jnp.where(i > 0, glup_ref[...], 0.0)
        glp_ref[HALO:, :] = glu_ref[...]

        @pl.when((b == 0) & (i == 0))
        def _():
            dw_ref[...] = jnp.zeros_like(dw_ref)

        for sub in range(tc // SUB):
            acc = jnp.zeros((SUB, D_CONV), F32)
            for k in range(CONV_K):
                acc = acc + dyp_ref[pl.ds(sub * SUB + (CONV_K - 1) - k, SUB), :] * w_ref[pl.ds(k, 1), :]
            acc_ref[sub * SUB:(sub + 1) * SUB, :] = acc
        for k in range(CONV_K):
            dw_ref[k:k + 1, :] += jnp.sum(dy * glp_ref[pl.ds(HALO - (CONV_K - 1) + k, tc), :], axis=0, keepdims=True)
        dglu = acc_ref[...]
        ca = ca_ref[...]
        sig = _sigmoid(cg_ref[...])
        dca_ref[...] = (dglu * sig).astype(BF16)
        dcg_ref[...] = (dglu * ca * sig * (1.0 - sig)).astype(BF16)

    tok = pl.BlockSpec((tc, D_CONV), lambda b, i: (b * nchunk + i, 0))
    nxt = pl.BlockSpec((HALO, D_CONV), lambda b, i: (jnp.minimum((b * nchunk + i + 1) * hb, last_hb), 0))
    prv = pl.BlockSpec((HALO, D_CONV), lambda b, i: (jnp.maximum((b * nchunk + i) * hb - 1, 0), 0))
    return _call(
        body, grid=(B, nchunk),
        in_specs=[tok, nxt, tok, prv,
                  pl.BlockSpec((tc, D_CONV), lambda b, i: (b * nchunk + i, 3)),
                  pl.BlockSpec((tc, D_CONV), lambda b, i: (b * nchunk + i, 4)),
                  pl.BlockSpec((CONV_K, D_CONV), lambda b, i: (0, 0))],
        out_specs=[tok, tok, pl.BlockSpec((32, D_CONV), lambda b, i: (0, 0))],
        out_shape=[jax.ShapeDtypeStruct((B * S, D_CONV), BF16), jax.ShapeDtypeStruct((B * S, D_CONV), BF16),
                   jax.ShapeDtypeStruct((32, D_CONV), F32)],
        scratch_shapes=[pltpu.VMEM((tc + HALO, D_CONV), F32), pltpu.VMEM((tc + HALO, D_CONV), F32),
                        pltpu.VMEM((tc, D_CONV), F32)],
        sem=("arbitrary", "arbitrary"), name=name, args=(dy, dy, glu, glu, u, u, cw), phase=phase)


def _outproj_fwd(h, attn, cv, wout, *, tm, name):
    T, D = h.shape

    def body(h_ref, a_ref, c_ref, w_ref, o_ref):
        o_ref[...] = (h_ref[...] + _dot(a_ref[...].astype(BF16), w_ref[0:D_ATTN, :])
                      + _dot(c_ref[...], w_ref[D_ATTN:, :]))

    tok = pl.BlockSpec((tm, D), lambda i: (i, 0))
    half = pl.BlockSpec((tm, D_ATTN), lambda i: (i, 0))
    return pl.pallas_call(
        body, grid=(T // tm,), in_specs=[tok, half, half, pl.BlockSpec(wout.shape, lambda i: (0, 0))],
        out_specs=tok, out_shape=jax.ShapeDtypeStruct((T, D), F32),
        compiler_params=_params("arbitrary"), name=name)(h, attn, cv, wout)


def _outproj_bwd(dh, attn, cv, wout, *, tm, name):
    T, D = dh.shape

    def body(dh_ref, a_ref, c_ref, w_ref, da_ref, dc_ref, dw_ref):
        @pl.when(pl.program_id(0) == 0)
        def _():
            dw_ref[...] = jnp.zeros_like(dw_ref)

        dhb = dh_ref[...].astype(BF16)
        da_ref[...] = _dot_nt(dhb, w_ref[0:D_ATTN, :])
        dc_ref[...] = _dot_nt(dhb, w_ref[D_ATTN:, :])
        dw_ref[0:D_ATTN, :] += _dot_tn(a_ref[...].astype(BF16), dhb)
        dw_ref[D_ATTN:, :] += _dot_tn(c_ref[...], dhb)

    tok = pl.BlockSpec((tm, D), lambda i: (i, 0))
    half = pl.BlockSpec((tm, D_ATTN), lambda i: (i, 0))
    wspec = pl.BlockSpec(wout.shape, lambda i: (0, 0))
    return pl.pallas_call(
        body, grid=(T // tm,), in_specs=[tok, half, half, wspec], out_specs=[half, half, wspec],
        out_shape=[jax.ShapeDtypeStruct((T, D_ATTN), F32), jax.ShapeDtypeStruct((T, D_ATTN), F32),
                   jax.ShapeDtypeStruct(wout.shape, F32)],
        compiler_params=_params("arbitrary"), name=name)(dh, attn, cv, wout)


ADAM_BLOCK_BYTES = 3 * 512 * 1024


def _adamw(w, g, m, v, *, name):
    R, C = w.shape
    tr = R
    for cand in (512, 352, 256, 176, 128, 64, 32, 16, 8):
        if R % cand == 0 and cand * C * 4 <= ADAM_BLOCK_BYTES:
            tr = cand
            break
    c1 = 1.0 - ADAM_B1 ** ADAM_STEP
    c2 = 1.0 - ADAM_B2 ** ADAM_STEP

    def body(w_ref, g_ref, m_ref, v_ref, d_ref, nm_ref, nv_ref):
        gv = g_ref[...]
        nm = ADAM_B1 * m_ref[...] + (1.0 - ADAM_B1) * gv
        nv = ADAM_B2 * v_ref[...] + (1.0 - ADAM_B2) * (gv * gv)
        d_ref[...] = -ADAM_LR * ((nm / c1) / (jnp.sqrt(nv / c2) + ADAM_EPS) + ADAM_WD * w_ref[...])
        nm_ref[...] = nm
        nv_ref[...] = nv

    blk = pl.BlockSpec((tr, C), lambda i: (i, 0))
    return pl.pallas_call(
        body, grid=(R // tr,), in_specs=[blk] * 4, out_specs=[blk] * 3,
        out_shape=[jax.ShapeDtypeStruct((R, C), F32)] * 3,
        compiler_params=_params("arbitrary"), name=name)(w, g, m, v)


TM = 512
TM_FFN_FWD = 1024
TK = 1024
TC = 256


def _local_step(x, tgt, w, overlap=None):
    B, S, D = x.shape
    T = B * S
    x2 = x.reshape(T, D)
    t2 = tgt.reshape(T, D)
    ones = jnp.ones((1, D_ATTN), F32)
    scale = HEAD_DIM ** -0.5
    gains = jnp.stack([jnp.tile(w["q_norm"], (1, HEADS)) * scale, jnp.tile(w["k_norm"], (1, HEADS)), ones])
    g = {}

    def hosting(point, build):
        phase = overlap.phase(point, w, g) if overlap is not None else None
        if phase is None:
            return build(None)
        outs, extra = build(phase)
        overlap.done(point, extra, w, g)
        return outs

    h1, n1, G1, U1 = hosting("ffn1_fwd", lambda ph: _ffn_fwd(
        x2, w["ffn1_norm"], w["wg1"], w["wu1"], w["wd1"], None, tm=TM_FFN_FWD, name="ffn1_fwd", phase=ph))
    u, n2 = hosting("inproj_fwd", lambda ph: _inproj_fwd(h1, w["mix_norm"], w["win"], tm=TM, name="inproj_fwd", phase=ph))
    qkv = _qkv_prep(u, gains, B, S, name="qkv_prep")
    qkv = qkv.reshape(3, N_PATTERNS, T // QBLK, QBLK, D_ATTN)
    o3, lse3 = _attn_fwd(qkv, name="attn_fwd")
    attn, lse = _attn_combine(o3, lse3, B, S, name="attn_combine")
    cv, glu, yconv = _conv_fwd(u, w["conv_w"], w["conv_b"], w["conv_ln_g"], w["conv_ln_b"], B, S, tc=TC, name="conv_fwd")
    h2 = _outproj_fwd(h1, attn, cv, w["wout"], tm=TM, name="outproj_fwd")
    dh3, n3, G2, U2, loss = _ffn_fwd(h2, w["ffn2_norm"], w["wg2"], w["wu2"], w["wd2"], t2, tm=TM_FFN_FWD, name="ffn2_fwd")

    dG, dU, A, dy, dh2, g["ffn2_norm"] = _ffn_bwd_act(dh3, h2, w["ffn2_norm"], G2, U2, w["wg2"], w["wu2"], w["wd2"],
                                                    tm=TM, name="ffn2_bwd_act")
    g["wg2"], g["wu2"], g["wd2"] = _ffn_bwd_w(n3, dy, dG, dU, A, tk=TK, name="ffn2_bwd_w")
    dattn, dcv, g["wout"] = _outproj_bwd(dh2, attn, cv, w["wout"], tm=TM, name="outproj_bwd")
    dyc, cpart = _conv_bwd_norm(dcv, yconv, w["conv_ln_g"], w["conv_ln_b"], tc=TC, name="conv_bwd_norm")
    dca, dcg, dcw = hosting("conv_bwd_taps", lambda ph: _conv_bwd_taps(
        dyc, glu, u, w["conv_w"], B, S, tc=TC, name="conv_bwd_taps", phase=ph))
    do3, lseb3, dd3 = _attn_bwd_prep(dattn, attn, lse, B, S, name="attn_bwd_prep")
    nb = T // QBLK
    shp = (N_PATTERNS, nb, QBLK, D_ATTN)
    cur, prev = hosting("attn_bwd", lambda ph: _attn_bwd(
        qkv, do3.reshape(shp), lseb3.reshape(shp), dd3.reshape(shp), name="attn_bwd", phase=ph))
    du_qkv, dgains = hosting("attn_grad_combine", lambda ph: _attn_grad_combine(
        cur, prev, u, gains, B, S, name="attn_grad_combine", phase=ph))
    du = jnp.concatenate([du_qkv, dca, dcg], axis=1)
    dh1, g["mix_norm"] = hosting("inproj_bwd_act", lambda ph: _inproj_bwd_act(
        du, dh2, h1, w["mix_norm"], w["win"], tm=TM, name="inproj_bwd_act", phase=ph))
    g["win"] = _inproj_bwd_w(n2, du, w["win"].shape[0], tk=TK, name="inproj_bwd_w")
    dG, dU, A, dy, dx, g["ffn1_norm"] = _ffn_bwd_act(dh1, x2, w["ffn1_norm"], G1, U1, w["wg1"], w["wu1"], w["wd1"],
                                                   tm=TM, name="ffn1_bwd_act")
    g["wg1"], g["wu1"], g["wd1"] = _ffn_bwd_w(n1, dy, dG, dU, A, tk=TK, name="ffn1_bwd_w")

    g["q_norm"] = dgains[0].reshape(HEADS, HEAD_DIM).sum(axis=0, keepdims=True) * scale
    g["k_norm"] = dgains[1].reshape(HEADS, HEAD_DIM).sum(axis=0, keepdims=True)
    g["conv_ln_g"] = cpart[0:1]
    g["conv_ln_b"] = cpart[1:2]
    g["conv_b"] = cpart[2:3]
    g["conv_w"] = dcw[:CONV_K]
    return loss, dx.reshape(B, S, D), g


N_CHIPS = 4
N_DEV = 8
VMEM_SPEC = pl.BlockSpec(memory_space=pltpu.VMEM)


def _remote(src, dst, send_sem, recv_sem, device):
    return pltpu.make_async_remote_copy(src_ref=src, dst_ref=dst, send_sem=send_sem, recv_sem=recv_sem,
                                        device_id=device, device_id_type=MESH)


def _stage_shards(shards, dtypes, *, name):
    n = len(shards)
    halves = [s.reshape(2, s.shape[0] // 2, s.shape[1]) for s in shards]

    def body(*refs):
        ins, outs, vms, loc_sems = refs[:n], refs[n:2 * n], refs[2 * n:3 * n], refs[3 * n]
        me = 2 * lax.axis_index("x") + lax.axis_index("y")
        copies = []
        for a in range(n):
            vms[a][...] = ins[a][...].astype(dtypes[a])
            cp = pltpu.make_async_copy(vms[a], outs[a].at[me], loc_sems.at[a])
            cp.start()
            copies.append(cp)
        for cp in copies:
            cp.wait()

    return pl.pallas_call(
        body, in_specs=[VMEM_SPEC] * n, out_specs=[ANY] * n,
        out_shape=[jax.ShapeDtypeStruct((N_CHIPS,) + h.shape, dt) for h, dt in zip(halves, dtypes)],
        scratch_shapes=[pltpu.VMEM(h.shape, dt) for h, dt in zip(halves, dtypes)] + [DMA_SEMS((n,))],
        compiler_params=pltpu.CompilerParams(vmem_limit_bytes=VMEM_LIMIT), name=name)(*halves)


def _like(arrays):
    return [jax.ShapeDtypeStruct(a.shape, a.dtype) for a in arrays]


def _axes():
    x, y, c = lax.axis_index("x"), lax.axis_index("y"), lax.axis_index("c")
    first = (x + (1 - c) * (1 - 2 * x), y + c * (1 - 2 * y))
    second = (x + c * (1 - 2 * x), y + (1 - c) * (1 - 2 * y))
    slots = tuple(2 * px + py for px, py in ((x, y), first, second, (1 - x, 1 - y)))
    return (x, y, c), (*first, c), (*second, c), slots


def _gather_ici_phase(bufs):
    n = len(bufs)

    def stage1(ins, outs, sems):
        (x, y, c), peer1, peer2, (own, s1, s2, both) = _axes()
        starts, arrivals = [], []
        for a in range(n):
            mine, land = outs[a].at[own, c], outs[a].at[s2, c]
            starts.append(_remote(mine, mine, *sems(a), peer2))
            arrivals.append(_remote(land, land, *sems(a), peer2))
        return starts, arrivals

    def stage2(ins, outs, sems):
        (x, y, c), peer1, peer2, (own, s1, s2, both) = _axes()
        starts, arrivals = [], []
        for a in range(n):
            for k, (src, dst) in enumerate(((own, s1), (s2, both))):
                mine, land = outs[a].at[src, c], outs[a].at[dst, c]
                starts.append(_remote(mine, mine, *sems(2 * a + k), peer1))
                arrivals.append(_remote(land, land, *sems(2 * a + k), peer1))
        return starts, arrivals

    same = {a: a for a in range(n)}
    return _Phase(bufs, _like(bufs), same, n, stage1).then(_Phase(bufs, _like(bufs), same, 2 * n, stage2))


def _gather_d2d_phase(bufs):
    n = len(bufs)

    def copies(ins, outs, sems):
        (x, y, c), peer1, peer2, (own, s1, s2, both) = _axes()
        starts, arrivals = [], []
        for a in range(n):
            for j, s in enumerate((s1, s2, both)):
                got, land = outs[a].at[s, c], outs[a].at[s, 1 - c]
                starts.append(_remote(got, got, *sems(3 * a + j), (x, y, 1 - c)))
                arrivals.append(_remote(land, land, *sems(3 * a + j), (x, y, 1 - c)))
        return starts, arrivals

    return _Phase(bufs, _like(bufs), {a: a for a in range(n)}, 3 * n, copies)


def _exchange_phase(views):
    n = len(views)

    def copies(ins, outs, sems):
        x, y, c = lax.axis_index("x"), lax.axis_index("y"), lax.axis_index("c")
        starts = [_remote(ins[a].at[pl.ds(0, ins[a].shape[0]), 1 - c], outs[a], *sems(a), (x, y, 1 - c))
                  for a in range(n)]
        return starts, starts

    outs = [jax.ShapeDtypeStruct((v.shape[0],) + v.shape[2:], F32) for v in views]
    return _Phase(views, outs, {}, n, copies)


def _row_block(rows):
    for cand in (256, 176, 128, 64, 32, 16, 8):
        if rows % cand == 0:
            return cand
    return rows


def _add_own_half(g, r, sel, *, name):
    ns, _, rh, cdim = g.shape
    tr = _row_block(rh)

    def body(s_ref, gk_ref, rk_ref, gs_ref, rs_ref, keep_ref, send_ref):
        keep_ref[...] = gk_ref[...] + rk_ref[...]
        send_ref[...] = (gs_ref[...] + rs_ref[...]).astype(BF16)

    def g_spec(off):
        return pl.BlockSpec((None, None, tr, cdim), lambda k, i, s: (s[1 + off + k], s[0], i, 0))

    def r_spec(off):
        return pl.BlockSpec((None, tr, cdim), lambda k, i, s: (s[1 + off + k], i, 0))

    out = pl.BlockSpec((None, tr, cdim), lambda k, i, s: (k, i, 0))
    return pl.pallas_call(
        body,
        grid_spec=pltpu.PrefetchScalarGridSpec(
            num_scalar_prefetch=1, grid=(2, rh // tr),
            in_specs=[g_spec(0), r_spec(0), g_spec(2), r_spec(2)], out_specs=[out, out]),
        out_shape=[jax.ShapeDtypeStruct((2, rh, cdim), F32), jax.ShapeDtypeStruct((2, rh, cdim), BF16)],
        compiler_params=_params("arbitrary", "arbitrary"), name=name)(sel, g, r, g, r)


def _swap_phase(arrays, stage):
    n = len(arrays)

    def copies(ins, outs, sems):
        peer = _axes()[stage]
        starts = [_remote(ins[a], outs[a], *sems(a), peer) for a in range(n)]
        return starts, starts

    return _Phase(arrays, _like(arrays), {}, n, copies)


def _add_stage1(keep, got, *, name):
    _, rh, cdim = keep.shape
    tr = _row_block(rh)

    def body(k_ref, g_ref, keep_ref, send_ref):
        keep_ref[...] = k_ref[0] + g_ref[0].astype(F32)
        send_ref[...] = (k_ref[1] + g_ref[1].astype(F32)).astype(BF16)

    blk2 = pl.BlockSpec((2, tr, cdim), lambda i: (0, i, 0))
    blk = pl.BlockSpec((tr, cdim), lambda i: (i, 0))
    return pl.pallas_call(
        body, grid=(rh // tr,), in_specs=[blk2, blk2], out_specs=[blk, blk],
        out_shape=[jax.ShapeDtypeStruct((rh, cdim), F32), jax.ShapeDtypeStruct((rh, cdim), BF16)],
        compiler_params=_params("arbitrary"), name=name)(keep, got)


def _add_stage2(keep, got, sel, *, name):
    rh, cdim = keep.shape
    tr = _row_block(rh)

    def body(s_ref, k_ref, g_ref, o_ref):
        o_ref[...] = k_ref[...] + g_ref[...].astype(F32)

    blk = pl.BlockSpec((tr, cdim), lambda i, s: (i, 0))
    return pl.pallas_call(
        body,
        grid_spec=pltpu.PrefetchScalarGridSpec(
            num_scalar_prefetch=1, grid=(rh // tr,), in_specs=[blk, blk],
            out_specs=pl.BlockSpec((None, tr, cdim), lambda i, s: (s[0], i, 0))),
        out_shape=jax.ShapeDtypeStruct((2, rh, cdim), F32),
        compiler_params=_params("arbitrary"), name=name)(sel, keep, got)


def _join_phase(halves):
    n = len(halves)

    def copies(ins, outs, sems):
        x, y, c = lax.axis_index("x"), lax.axis_index("y"), lax.axis_index("c")
        starts, arrivals = [], []
        for a in range(n):
            mine, land = outs[a].at[c], outs[a].at[1 - c]
            starts.append(_remote(mine, mine, *sems(a), (x, y, 1 - c)))
            arrivals.append(_remote(land, land, *sems(a), (x, y, 1 - c)))
        return starts, arrivals

    return _Phase(halves, _like(halves), {a: a for a in range(n)}, n, copies)


def _slot_order():
    x, y, c = lax.axis_index("x"), lax.axis_index("y"), lax.axis_index("c")
    own, flip_x, flip_y, both = 2 * x + y, 2 * (1 - x) + y, 2 * x + 1 - y, 2 * (1 - x) + 1 - y
    first = jnp.where(c == 0, flip_x, flip_y)
    second = jnp.where(c == 0, flip_y, flip_x)
    return jnp.stack([c, own, second, first, both]).astype(jnp.int32)


def _reduce_scatter(grads):
    sel = _slot_order()
    views = [_half_view(g) for g in grads]
    got = _run_phase(_exchange_phase(views), name="rs_exchange_halves")
    keep, send = _add_halves(views, got, sel, "late")
    got = _run_phase(_swap_phase(send, 1), name="rs_swap_first_axis")
    keep, send = _add_first(keep, got, "late")
    got = _run_phase(_swap_phase(send, 2), name="rs_swap_second_axis")
    halves = _add_second(keep, got, sel, "late")
    full = _run_phase(_join_phase(halves), name="rs_join_halves")
    return [f.reshape(g.shape[1], g.shape[2]) for f, g in zip(full, grads)]


def _half_view(g):
    return g.reshape(N_CHIPS, 2, g.shape[1] // 2, g.shape[2])


def _add_halves(views, got, sel, tag):
    keep, send = zip(*[_add_own_half(v, r, sel, name=f"rs_add_half_{tag}{a}") for a, (v, r) in enumerate(zip(views, got))])
    return list(keep), list(send)


def _add_first(keep, got, tag):
    keep, send = zip(*[_add_stage1(k, r, name=f"rs_add_first_{tag}{a}") for a, (k, r) in enumerate(zip(keep, got))])
    return list(keep), list(send)


def _add_second(keep, got, sel, tag):
    return [_add_stage2(k, r, sel, name=f"rs_add_second_{tag}{a}") for a, (k, r) in enumerate(zip(keep, got))]


EARLY_GRADS = ("wg2", "wu2", "wd2", "wout")


class _Overlap:
    def __init__(self, staged):
        self.staged = staged
        self.sel = _slot_order()
        self.reduced = {}

    def phase(self, point, w, g):
        if point == "ffn1_fwd":
            return _gather_ici_phase(self.staged)
        if point == "inproj_fwd":
            return _gather_d2d_phase(self.ffn2)
        if point == "conv_bwd_taps":
            self.shapes = [g[k].shape for k in EARLY_GRADS]
            self.views = [_half_view(g[k].reshape(N_CHIPS, -1, g[k].shape[-1])) for k in EARLY_GRADS]
            return _exchange_phase(self.views)
        if point == "attn_bwd":
            return _swap_phase(self.send, 1)
        if point == "attn_grad_combine":
            return _swap_phase(self.send, 2)
        if point == "inproj_bwd_act":
            return _join_phase(self.halves)
        return None

    def done(self, point, outs, w, g):
        if point == "ffn1_fwd":
            win, wout, taps = [_whole(b) for b in _run_phase(_gather_d2d_phase(outs[:3]), name="gather_mix_d2d")]
            w["win"] = win
            w["wout"] = wout.reshape(-1, wout.shape[-1])
            w["conv_w"] = taps.transpose(1, 0, 2).reshape(CONV_K + 1, D_CONV)[:CONV_K]
            self.ffn2 = list(outs[3:])
        elif point == "inproj_fwd":
            w["wg2"], w["wu2"], w["wd2"] = [_whole(b) for b in outs]
        elif point == "conv_bwd_taps":
            self.keep, self.send = _add_halves(self.views, outs, self.sel, "early")
        elif point == "attn_bwd":
            self.keep, self.send = _add_first(self.keep, outs, "early")
        elif point == "attn_grad_combine":
            self.halves = _add_second(self.keep, outs, self.sel, "early")
        elif point == "inproj_bwd_act":
            for k, shp, f in zip(EARLY_GRADS, self.shapes, outs):
                self.reduced[k] = f.reshape(-1, shp[-1])


def _whole(buf):
    return buf.reshape(buf.shape[0], 2 * buf.shape[2], buf.shape[3])


def _allreduce_small(pack, *, name):
    rows = pack.shape[0]

    def body(p_ref, o_ref, buf_ref, send_sems, recv_sems):
        x, y, c = lax.axis_index("x"), lax.axis_index("y"), lax.axis_index("c")
        me = 4 * x + 2 * y + c
        buf_ref[me] = p_ref[...]
        cps = []
        for k in range(1, N_DEV):
            peer = tuple(1 - v if (k >> s) & 1 else v for v, s in ((x, 2), (y, 1), (c, 0)))
            cp = _remote(p_ref, buf_ref.at[me], send_sems.at[k - 1], recv_sems.at[k - 1], peer)
            cp.start()
            cps.append(cp)
        for k in range(1, N_DEV):
            src = 4 * (x ^ ((k >> 2) & 1)) + 2 * (y ^ ((k >> 1) & 1)) + (c ^ (k & 1))
            land = buf_ref.at[src]
            _remote(land, land, send_sems.at[k - 1], recv_sems.at[k - 1], (x, y, c)).wait_recv()
        acc = buf_ref[0]
        for d in range(1, N_DEV):
            acc = acc + buf_ref[d]
        o_ref[...] = acc
        for cp in cps:
            cp.wait_send()

    return pl.pallas_call(
        body, in_specs=[VMEM_SPEC], out_specs=VMEM_SPEC, out_shape=jax.ShapeDtypeStruct(pack.shape, F32),
        scratch_shapes=[pltpu.VMEM((N_DEV, rows, LANES), F32), pltpu.SemaphoreType.DMA((N_DEV - 1,)),
                        pltpu.SemaphoreType.DMA((N_DEV - 1,))], name=name)(pack)


SMALL = ("ffn1_norm", "mix_norm", "q_norm", "k_norm", "conv_b", "conv_ln_g", "conv_ln_b", "ffn2_norm", "conv_w")
BIG = ("ffn1_w_gate", "ffn1_w_up", "ffn1_w_down", "w_in", "w_out", "ffn2_w_gate", "ffn2_w_up", "ffn2_w_down")
TRANSPOSED = ("ffn1_w_gate", "ffn1_w_up", "ffn2_w_gate", "ffn2_w_up")
WEIGHTS = ("ffn1_norm", "ffn1_w_gate", "ffn1_w_up", "ffn1_w_down", "mix_norm", "w_in", "q_norm", "k_norm",
           "conv_w", "conv_b", "conv_ln_g", "conv_ln_b", "w_out", "ffn2_norm", "ffn2_w_gate", "ffn2_w_up",
           "ffn2_w_down")
SUBLANES = 8


def _pack(parts):
    rows = []
    for p in parts:
        flat = p.reshape(-1)
        tile = SUBLANES * LANES
        padded = -(-flat.shape[0] // tile) * tile
        rows.append(jnp.pad(flat, (0, padded - flat.shape[0])).reshape(-1, LANES))
    return jnp.concatenate(rows, axis=0)


def _unpack(pack, shapes):
    out, row = [], 0
    for shp in shapes:
        size = shp[0] * shp[1]
        tile = SUBLANES * LANES
        nrows = -(-size // tile) * SUBLANES
        out.append(pack[row:row + nrows].reshape(-1)[:size].reshape(shp))
        row += nrows
    return out


def kernel(x, ffn1_norm, ffn1_w_gate, ffn1_w_up, ffn1_w_down, mix_norm, w_in, q_norm, k_norm, conv_w, conv_b, conv_ln_g, conv_ln_b, w_out, ffn2_norm, ffn2_w_gate, ffn2_w_up, ffn2_w_down, loss_target, m_ffn1_norm, m_ffn1_w_gate, m_ffn1_w_up, m_ffn1_w_down, m_mix_norm, m_w_in, m_q_norm, m_k_norm, m_conv_w, m_conv_b, m_conv_ln_g, m_conv_ln_b, m_w_out, m_ffn2_norm, m_ffn2_w_gate, m_ffn2_w_up, m_ffn2_w_down, v_ffn1_norm, v_ffn1_w_gate, v_ffn1_w_up, v_ffn1_w_down, v_mix_norm, v_w_in, v_q_norm, v_k_norm, v_conv_w, v_conv_b, v_conv_ln_g, v_conv_ln_b, v_w_out, v_ffn2_norm, v_ffn2_w_gate, v_ffn2_w_up, v_ffn2_w_down):
    wts = dict(ffn1_norm=ffn1_norm, ffn1_w_gate=ffn1_w_gate[0], ffn1_w_up=ffn1_w_up[0], ffn1_w_down=ffn1_w_down[0],
               mix_norm=mix_norm, w_in=w_in[0], q_norm=q_norm, k_norm=k_norm, conv_w=conv_w[0], conv_b=conv_b,
               conv_ln_g=conv_ln_g, conv_ln_b=conv_ln_b, w_out=w_out[0], ffn2_norm=ffn2_norm,
               ffn2_w_gate=ffn2_w_gate[0], ffn2_w_up=ffn2_w_up[0], ffn2_w_down=ffn2_w_down[0])
    mom = dict(ffn1_norm=m_ffn1_norm, ffn1_w_gate=m_ffn1_w_gate[0], ffn1_w_up=m_ffn1_w_up[0], ffn1_w_down=m_ffn1_w_down[0],
               mix_norm=m_mix_norm, w_in=m_w_in[0], q_norm=m_q_norm, k_norm=m_k_norm, conv_w=m_conv_w[0], conv_b=m_conv_b,
               conv_ln_g=m_conv_ln_g, conv_ln_b=m_conv_ln_b, w_out=m_w_out[0], ffn2_norm=m_ffn2_norm,
               ffn2_w_gate=m_ffn2_w_gate[0], ffn2_w_up=m_ffn2_w_up[0], ffn2_w_down=m_ffn2_w_down[0])
    var = dict(ffn1_norm=v_ffn1_norm, ffn1_w_gate=v_ffn1_w_gate[0], ffn1_w_up=v_ffn1_w_up[0], ffn1_w_down=v_ffn1_w_down[0],
               mix_norm=v_mix_norm, w_in=v_w_in[0], q_norm=v_q_norm, k_norm=v_k_norm, conv_w=v_conv_w[0], conv_b=v_conv_b,
               conv_ln_g=v_conv_ln_g, conv_ln_b=v_conv_ln_b, w_out=v_w_out[0], ffn2_norm=v_ffn2_norm,
               ffn2_w_gate=v_ffn2_w_gate[0], ffn2_w_up=v_ffn2_w_up[0], ffn2_w_down=v_ffn2_w_down[0])
    chip = 2 * lax.axis_index("x") + lax.axis_index("y")
    for src in (wts, mom, var):
        for n in TRANSPOSED:
            src[n] = src[n].T

    taps = jnp.pad(wts["conv_w"], ((0, 1), (0, 0)))
    staged = _stage_shards([wts["ffn1_w_gate"], wts["ffn1_w_up"], wts["ffn1_w_down"], wts["w_in"], wts["w_out"], taps,
                            wts["ffn2_w_gate"], wts["ffn2_w_up"], wts["ffn2_w_down"]],
                           [BF16, BF16, BF16, BF16, BF16, F32, BF16, BF16, BF16], name="stage_shards")
    first = _run_phase(_gather_ici_phase(staged[:3]).then(_gather_d2d_phase(staged[:3])), name="gather_ffn1")
    wg1, wu1, wd1 = [_whole(b) for b in first]
    w = dict(ffn1_norm=ffn1_norm, mix_norm=mix_norm, ffn2_norm=ffn2_norm, q_norm=q_norm, k_norm=k_norm,
             conv_b=conv_b, conv_ln_g=conv_ln_g, conv_ln_b=conv_ln_b, wg1=wg1, wu1=wu1, wd1=wd1)
    overlap = _Overlap(staged[3:])
    loss_part, grad_x, g = _local_step(x, loss_target, w, overlap)

    late = _reduce_scatter([g["wg1"], g["wu1"], g["wd1"], g["win"]])
    big_grads = dict(zip(("ffn1_w_gate", "ffn1_w_up", "ffn1_w_down", "w_in"), late))
    big_grads.update(ffn2_w_gate=overlap.reduced["wg2"], ffn2_w_up=overlap.reduced["wu2"],
                     ffn2_w_down=overlap.reduced["wd2"], w_out=overlap.reduced["wout"])

    small_shapes = [g[n].shape for n in SMALL] + [(SUBLANES, LANES)]
    red = _allreduce_small(_pack([g[n] for n in SMALL] + [loss_part]), name="allreduce_small")
    small = dict(zip(SMALL + ("loss",), _unpack(red, small_shapes)))
    loss = small["loss"][0, 0]
    small["conv_w"] = lax.dynamic_slice_in_dim(small["conv_w"], chip * LANES, LANES, axis=1)

    grads, delta, new_m, new_v = {}, {}, {}, {}
    for n in BIG:
        grads[n] = big_grads[n]
        delta[n], new_m[n], new_v[n] = _adamw(wts[n], grads[n], mom[n], var[n], name=f"adamw_{n}")
    shapes = [wts[n].shape for n in SMALL]
    packs = [_pack([src[n] for n in SMALL]) for src in (wts, small, mom, var)]
    outs = _adamw(*packs, name="adamw_small")
    for dst, pk in zip((delta, new_m, new_v), outs):
        dst.update(zip(SMALL, _unpack(pk, shapes)))
    for n in SMALL:
        grads[n] = small[n]

    def shaped(d, n):
        v = d[n].T if n in TRANSPOSED else d[n]
        return v.reshape((1,) + v.shape) if n in BIG or n == "conv_w" else v

    return (loss, grad_x, *[shaped(grads, n) for n in WEIGHTS], *[shaped(delta, n) for n in WEIGHTS],
            *[shaped(new_m, n) for n in WEIGHTS], *[shaped(new_v, n) for n in WEIGHTS])
```

```python
import functools

import jax
import jax.numpy as jnp
from jax import lax
from jax.experimental import pallas as pl
from jax.experimental.pallas import tpu as pltpu

F32 = jnp.float32
BF16 = jnp.bfloat16

EPS = 1e-6
HEADS = 8
HEAD_DIM = 64
D_ATTN = HEADS * HEAD_DIM
D_CONV = 512
CONV_K = 31
QBLK = 128
N_PATTERNS = 3
DILATIONS = (1, 4, 16)
LANES = 128
NEG = -1e30

ADAM_LR = 0.001
ADAM_B1 = 0.9
ADAM_B2 = 0.999
ADAM_EPS = 1e-08
ADAM_WD = 0.01
ADAM_STEP = 10

VMEM_LIMIT = 56 * 1024 * 1024
MESH = pl.DeviceIdType.MESH

NT_DIMS = (((1,), (1,)), ((), ()))
TN_DIMS = (((0,), (0,)), ((), ()))


def _params(*sem):
    return pltpu.CompilerParams(dimension_semantics=sem, vmem_limit_bytes=VMEM_LIMIT)


def _dot(a, b):
    return jnp.dot(a, b, preferred_element_type=F32)


def _dot_nt(a, b):
    return lax.dot_general(a, b, NT_DIMS, preferred_element_type=F32)


def _dot_tn(a, b):
    return lax.dot_general(a, b, TN_DIMS, preferred_element_type=F32)


def _sigmoid(x):
    return 1.0 / (1.0 + jnp.exp(-x))


def _seg_mean(v, e_ref, width):
    hi = v.astype(BF16)
    lo = (v - hi.astype(F32)).astype(BF16)
    e = e_ref[...]
    return (_dot(hi, e) + _dot(lo, e)) * (1.0 / width)


def _seg_matrix(n):
    i = jnp.arange(n)
    return (i[:, None] // HEAD_DIM == i[None, :] // HEAD_DIM).astype(BF16)


ANY = pl.BlockSpec(memory_space=pl.ANY)
DMA_SEMS = pltpu.SemaphoreType.DMA


class _Phase:
    def __init__(self, ins, outs, aliases, nsem, copies):
        self.ins, self.outs, self.aliases = list(ins), list(outs), dict(aliases)
        self.stages = [(nsem, copies)]

    def then(self, other):
        self.stages = self.stages + other.stages
        return self

    @property
    def nsem(self):
        return sum(n for n, _ in self.stages)

    def _copies(self, k, in_refs, out_refs, send_sems, recv_sems):
        base = sum(n for n, _ in self.stages[:k])
        return self.stages[k][1](in_refs, out_refs, lambda i: (send_sems.at[base + i], recv_sems.at[base + i]))

    def start(self, k, *refs):
        for cp in self._copies(k, *refs)[0]:
            cp.start()

    def finish(self, k, *refs):
        starts, arrivals = self._copies(k, *refs)
        for cp in arrivals:
            cp.wait_recv()
        for cp in starts:
            cp.wait_send()


def _run_phase(phase, *, name):
    n_in, n_out = len(phase.ins), len(phase.outs)

    def body(*refs):
        ins, outs = refs[:n_in], refs[n_in:n_in + n_out]
        send_sems, recv_sems = refs[n_in + n_out:]
        for k in range(len(phase.stages)):
            phase.start(k, ins, outs, send_sems, recv_sems)
            phase.finish(k, ins, outs, send_sems, recv_sems)

    return pl.pallas_call(
        body, in_specs=[ANY] * n_in, out_specs=[ANY] * n_out, out_shape=phase.outs,
        input_output_aliases=phase.aliases,
        scratch_shapes=[DMA_SEMS((phase.nsem,)), DMA_SEMS((phase.nsem,))], name=name)(*phase.ins)


def _call(body, *, grid, in_specs, out_specs, out_shape, scratch_shapes=(), sem, name, args, phase=None):
    in_specs, out_specs, out_shape = list(in_specs), list(out_specs), list(out_shape)
    scratch_shapes = list(scratch_shapes)
    if phase is None:
        return pl.pallas_call(body, grid=grid, in_specs=in_specs, out_specs=out_specs, out_shape=out_shape,
                              scratch_shapes=scratch_shapes, compiler_params=_params(*sem), name=name)(*args)
    n_in, n_out, n_scr = len(in_specs), len(out_specs), len(scratch_shapes)
    p_in, p_out = len(phase.ins), len(phase.outs)

    def hosted(*refs):
        ins, pins = refs[:n_in], refs[n_in:n_in + p_in]
        o0 = n_in + p_in
        outs, pouts = refs[o0:o0 + n_out], refs[o0 + n_out:o0 + n_out + p_out]
        s0 = o0 + n_out + p_out
        scr = refs[s0:s0 + n_scr]
        send_sems, recv_sems = refs[s0 + n_scr:]
        step = 0
        for d, n in enumerate(grid):
            step = step * n + pl.program_id(d)
        nsteps = functools.reduce(lambda a, b: a * b, grid)
        nstages = len(phase.stages)
        comm_refs = (pins, pouts, send_sems, recv_sems)

        for k in range(nstages):
            @pl.when(step == (k * nsteps) // nstages)
            def _(k=k):
                if k > 0:
                    phase.finish(k - 1, *comm_refs)
                phase.start(k, *comm_refs)

        body(*ins, *outs, *scr)

        @pl.when(step == nsteps - 1)
        def _():
            phase.finish(nstages - 1, *comm_refs)

    res = pl.pallas_call(
        hosted, grid=grid, in_specs=in_specs + [ANY] * p_in, out_specs=out_specs + [ANY] * p_out,
        out_shape=out_shape + phase.outs,
        input_output_aliases={n_in + i: n_out + o for i, o in phase.aliases.items()},
        scratch_shapes=scratch_shapes + [DMA_SEMS((phase.nsem,)), DMA_SEMS((phase.nsem,))],
        compiler_params=_params(*sem), name=name)(*args, *phase.ins)
    return res[:n_out], res[n_out:]


def _ffn_fwd(x, gain, wg, wu, wd, tgt, *, tm, name, phase=None):
    T, D = x.shape
    NS, Fs, _ = wg.shape
    with_loss = tgt is not None

    def body(*refs):
        if with_loss:
            x_ref, g_ref, wg_ref, wu_ref, wd_ref, t_ref, h_ref, n_ref, G_ref, U_ref, loss_ref, acc_ref = refs
        else:
            x_ref, g_ref, wg_ref, wu_ref, wd_ref, h_ref, n_ref, G_ref, U_ref, acc_ref = refs
        i = pl.program_id(0)
        j = pl.program_id(1)

        @pl.when(j == 0)
        def _():
            xv = x_ref[...]
            r = lax.rsqrt(jnp.mean(xv * xv, axis=-1, keepdims=True) + EPS)
            n_ref[...] = (xv * r * g_ref[...]).astype(BF16)
            acc_ref[...] = jnp.zeros_like(acc_ref)

        n = n_ref[...]
        G = _dot_nt(n, wg_ref[...])
        U = _dot_nt(n, wu_ref[...])
        G_ref[...] = G.astype(BF16)
        U_ref[...] = U.astype(BF16)
        A = (G * _sigmoid(G) * U).astype(BF16)
        acc_ref[...] += _dot(A, wd_ref[...])

        @pl.when(j == NS - 1)
        def _():
            h = x_ref[...] + 0.5 * acc_ref[...]
            if with_loss:
                e = h - t_ref[...]
                h_ref[...] = e * (1.0 / D)

                @pl.when(i == 0)
                def _():
                    loss_ref[...] = jnp.zeros_like(loss_ref)

                loss_ref[...] += jnp.sum(e * e) * (0.5 / D)
            else:
                h_ref[...] = h

    tok = pl.BlockSpec((tm, D), lambda i, j: (i, 0))
    in_specs = [tok, pl.BlockSpec((1, D), lambda i, j: (0, 0)),
                pl.BlockSpec((None, Fs, D), lambda i, j: (j, 0, 0)),
                pl.BlockSpec((None, Fs, D), lambda i, j: (j, 0, 0)),
                pl.BlockSpec((None, Fs, D), lambda i, j: (j, 0, 0))]
    args = [x, gain, wg, wu, wd]
    act = pl.BlockSpec((None, tm, Fs), lambda i, j: (j, i, 0))
    out_shape = [jax.ShapeDtypeStruct((T, D), F32), jax.ShapeDtypeStruct((T, D), BF16),
                 jax.ShapeDtypeStruct((NS, T, Fs), BF16), jax.ShapeDtypeStruct((NS, T, Fs), BF16)]
    out_specs = [tok, tok, act, act]
    if with_loss:
        in_specs.append(tok)
        args.append(tgt)
        out_shape.append(jax.ShapeDtypeStruct((8, LANES), F32))
        out_specs.append(pl.BlockSpec((8, LANES), lambda i, j: (0, 0)))
    return _call(body, grid=(T // tm, NS), in_specs=in_specs, out_specs=out_specs, out_shape=out_shape,
                 scratch_shapes=[pltpu.VMEM((tm, D), F32)], sem=("arbitrary", "arbitrary"), name=name,
                 args=args, phase=phase)


def _rms_bwd(xv, gain, dn):
    r = lax.rsqrt(jnp.mean(xv * xv, axis=-1, keepdims=True) + EPS)
    xhat = xv * r
    dxh = dn * gain
    dx = r * (dxh - xhat * jnp.mean(dxh * xhat, axis=-1, keepdims=True))
    dg = jnp.sum(dn * xhat, axis=0, keepdims=True)
    return dx, dg


def _ffn_bwd_act(dh, x, gain, G, U, wg, wu, wd, *, tm, name):
    T, D = x.shape
    NS, Fs, _ = wg.shape

    def body(dh_ref, x_ref, g_ref, G_ref, U_ref, wg_ref, wu_ref, wd_ref,
             dG_ref, dU_ref, A_ref, dy_ref, dx_ref, dg_ref, acc_ref):
        i = pl.program_id(0)
        j = pl.program_id(1)

        @pl.when(j == 0)
        def _():
            dy_ref[...] = (0.5 * dh_ref[...]).astype(BF16)
            acc_ref[...] = jnp.zeros_like(acc_ref)

        @pl.when((i == 0) & (j == 0))
        def _():
            dg_ref[...] = jnp.zeros_like(dg_ref)

        Gv = G_ref[...].astype(F32)
        Uv = U_ref[...].astype(F32)
        sig = _sigmoid(Gv)
        s = Gv * sig
        dA = _dot_nt(dy_ref[...], wd_ref[...])
        dG = (dA * Uv * (sig * (1.0 + Gv * (1.0 - sig)))).astype(BF16)
        dU = (dA * s).astype(BF16)
        dG_ref[...] = dG
        dU_ref[...] = dU
        A_ref[...] = (s * Uv).astype(BF16)
        acc_ref[...] += _dot(dG, wg_ref[...]) + _dot(dU, wu_ref[...])

        @pl.when(j == NS - 1)
        def _():
            dx, dg = _rms_bwd(x_ref[...], g_ref[...], acc_ref[...])
            dx_ref[...] = dh_ref[...] + dx
            dg_ref[...] += dg

    tok = pl.BlockSpec((tm, D), lambda i, j: (i, 0))
    act = pl.BlockSpec((None, tm, Fs), lambda i, j: (j, i, 0))
    vec = pl.BlockSpec((1, D), lambda i, j: (0, 0))
    return pl.pallas_call(
        body, grid=(T // tm, NS),
        in_specs=[tok, tok, vec, act, act,
                  pl.BlockSpec((None, Fs, D), lambda i, j: (j, 0, 0)),
                  pl.BlockSpec((None, Fs, D), lambda i, j: (j, 0, 0)),
                  pl.BlockSpec((None, Fs, D), lambda i, j: (j, 0, 0))],
        out_specs=[act, act, act, tok, tok, vec],
        out_shape=[jax.ShapeDtypeStruct((NS, T, Fs), BF16)] * 3
        + [jax.ShapeDtypeStruct((T, D), BF16), jax.ShapeDtypeStruct((T, D), F32),
           jax.ShapeDtypeStruct((1, D), F32)],
        scratch_shapes=[pltpu.VMEM((tm, D), F32)],
        compiler_params=_params("arbitrary", "arbitrary"), name=name)(dh, x, gain, G, U, wg, wu, wd)


def _ffn_bwd_w(n, dy, dG, dU, A, *, tk, name):
    T, D = n.shape
    NS, _, Fs = dG.shape

    def body(n_ref, dy_ref, dG_ref, dU_ref, A_ref, wg_ref, wu_ref, wd_ref):
        @pl.when(pl.program_id(1) == 0)
        def _():
            wg_ref[...] = jnp.zeros_like(wg_ref)
            wu_ref[...] = jnp.zeros_like(wu_ref)
            wd_ref[...] = jnp.zeros_like(wd_ref)

        nv = n_ref[...]
        wg_ref[...] += _dot_tn(dG_ref[...], nv)
        wu_ref[...] += _dot_tn(dU_ref[...], nv)
        wd_ref[...] += _dot_tn(A_ref[...], dy_ref[...])

    tok = pl.BlockSpec((tk, D), lambda j, k: (k, 0))
    act = pl.BlockSpec((None, tk, Fs), lambda j, k: (j, k, 0))
    return pl.pallas_call(
        body, grid=(NS, T // tk), in_specs=[tok, tok, act, act, act],
        out_specs=[pl.BlockSpec((None, Fs, D), lambda j, k: (j, 0, 0))] * 3,
        out_shape=[jax.ShapeDtypeStruct((NS, Fs, D), F32)] * 3,
        compiler_params=_params("arbitrary", "arbitrary"), name=name)(n, dy, dG, dU, A)


def _inproj_fwd(h, gain, win, *, tm, name, phase=None):
    T, D = h.shape
    NS, _, Cs = win.shape

    def body(h_ref, g_ref, w_ref, u_ref, n_ref):
        @pl.when(pl.program_id(1) == 0)
        def _():
            xv = h_ref[...]
            r = lax.rsqrt(jnp.mean(xv * xv, axis=-1, keepdims=True) + EPS)
            n_ref[...] = (xv * r * g_ref[...]).astype(BF16)

        u_ref[...] = _dot(n_ref[...], w_ref[...])

    tok = pl.BlockSpec((tm, D), lambda i, j: (i, 0))
    return _call(
        body, grid=(T // tm, NS),
        in_specs=[tok, pl.BlockSpec((1, D), lambda i, j: (0, 0)),
                  pl.BlockSpec((None, D, Cs), lambda i, j: (j, 0, 0))],
        out_specs=[pl.BlockSpec((tm, Cs), lambda i, j: (i, j)), tok],
        out_shape=[jax.ShapeDtypeStruct((T, NS * Cs), F32), jax.ShapeDtypeStruct((T, D), BF16)],
        sem=("arbitrary", "arbitrary"), name=name, args=(h, gain, win), phase=phase)


def _inproj_bwd_act(du, dh, h, gain, win, *, tm, name, phase=None):
    T, D = h.shape
    NS, _, Cs = win.shape

    def body(du_ref, dh_ref, h_ref, g_ref, w_ref, dx_ref, dg_ref, acc_ref):
        i = pl.program_id(0)
        j = pl.program_id(1)

        @pl.when(j == 0)
        def _():
            acc_ref[...] = jnp.zeros_like(acc_ref)

        @pl.when((i == 0) & (j == 0))
        def _():
            dg_ref[...] = jnp.zeros_like(dg_ref)

        acc_ref[...] += _dot_nt(du_ref[...], w_ref[...])

        @pl.when(j == NS - 1)
        def _():
            dx, dg = _rms_bwd(h_ref[...], g_ref[...], acc_ref[...])
            dx_ref[...] = dh_ref[...] + dx
            dg_ref[...] += dg

    tok = pl.BlockSpec((tm, D), lambda i, j: (i, 0))
    vec = pl.BlockSpec((1, D), lambda i, j: (0, 0))
    return _call(
        body, grid=(T // tm, NS),
        in_specs=[pl.BlockSpec((tm, Cs), lambda i, j: (i, j)), tok, tok, vec,
                  pl.BlockSpec((None, D, Cs), lambda i, j: (j, 0, 0))],
        out_specs=[tok, vec],
        out_shape=[jax.ShapeDtypeStruct((T, D), F32), jax.ShapeDtypeStruct((1, D), F32)],
        scratch_shapes=[pltpu.VMEM((tm, D), F32)],
        sem=("arbitrary", "arbitrary"), name=name, args=(du, dh, h, gain, win), phase=phase)


def _inproj_bwd_w(n, du, ns, *, tk, name):
    T, D = n.shape
    Cs = du.shape[1] // ns

    def body(n_ref, du_ref, w_ref):
        @pl.when(pl.program_id(1) == 0)
        def _():
            w_ref[...] = jnp.zeros_like(w_ref)

        w_ref[...] += _dot_tn(n_ref[...], du_ref[...])

    return pl.pallas_call(
        body, grid=(ns, T // tk),
        in_specs=[pl.BlockSpec((tk, D), lambda j, k: (k, 0)), pl.BlockSpec((tk, Cs), lambda j, k: (k, j))],
        out_specs=pl.BlockSpec((None, D, Cs), lambda j, k: (j, 0, 0)),
        out_shape=jax.ShapeDtypeStruct((ns, D, Cs), F32),
        compiler_params=_params("arbitrary", "arbitrary"), name=name)(n, du)


def _permute_out(src_ref, out_ref, cast):
    S = src_ref.shape[1]
    for p, d in enumerate(DILATIONS):
        L = S // d
        for cc in range(4):
            cols = slice(cc * LANES, (cc + 1) * LANES)
            if d == 1:
                out_ref[p, :, cols] = src_ref[cc].astype(cast)
            else:
                for r in range(d):
                    out_ref[p, r * L:(r + 1) * L, cols] = src_ref[cc, pl.ds(r, L, stride=d), :].astype(cast)


def _unpermute_in(get_block, dst_ref, p, S):
    d = DILATIONS[p]
    L = S // d
    if d == 1:
        dst_ref[...] = get_block(0, S)
    else:
        for r in range(d):
            dst_ref[pl.ds(r, L, stride=d), :] = get_block(r * L, L)


def _qkv_prep(u, gains, B, S, *, name):
    emat = _seg_matrix(D_ATTN)

    def body(u_ref, g_ref, e_ref, out_ref, scr_ref):
        c = pl.program_id(1)
        xv = u_ref[...]
        ms = _seg_mean(xv * xv, e_ref, HEAD_DIM)
        r = jnp.where(c < 2, lax.rsqrt(ms + EPS), 1.0)
        yv = xv * r * g_ref[...]
        for cc in range(4):
            scr_ref[cc] = yv[:, cc * LANES:(cc + 1) * LANES]
        _permute_out(scr_ref, out_ref, BF16)

    return pl.pallas_call(
        body, grid=(B, 3),
        in_specs=[pl.BlockSpec((S, D_ATTN), lambda b, c: (b, c)),
                  pl.BlockSpec((None, 1, D_ATTN), lambda b, c: (c, 0, 0)),
                  pl.BlockSpec((D_ATTN, D_ATTN), lambda b, c: (0, 0))],
        out_specs=pl.BlockSpec((None, N_PATTERNS, None, S, D_ATTN), lambda b, c: (c, 0, b, 0, 0)),
        out_shape=jax.ShapeDtypeStruct((3, N_PATTERNS, B, S, D_ATTN), BF16),
        scratch_shapes=[pltpu.VMEM((4, S, LANES), F32)],
        compiler_params=_params("arbitrary", "arbitrary"), name=name)(u, gains, emat)


def _band_mask(p, b):
    nblk = jnp.right_shift(16, 2 * p)
    has_prev = jnp.bitwise_and(b, nblk - 1) != 0
    qi = lax.broadcasted_iota(jnp.int32, (QBLK, 2 * QBLK), 0)
    ci = lax.broadcasted_iota(jnp.int32, (QBLK, 2 * QBLK), 1)
    dist = QBLK + qi - ci
    return (dist >= 0) & (dist <= QBLK) & (has_prev | (ci >= QBLK))


def _first_head(rows):
    return lax.broadcasted_iota(jnp.int32, (rows, LANES), 1) < HEAD_DIM


def _split_heads(pair):
    first = _first_head(pair.shape[0])
    zero = jnp.zeros_like(pair)
    return jnp.concatenate([jnp.where(first, pair, zero), jnp.where(first, zero, pair)], axis=0)


def _merge_heads(col_a, col_b):
    rows = col_a.shape[0]
    return jnp.where(_first_head(rows), jnp.broadcast_to(col_a, (rows, LANES)), jnp.broadcast_to(col_b, (rows, LANES)))


def _attn_specs(nb):
    blk = (None, None, None, QBLK, D_ATTN)
    q_spec = pl.BlockSpec(blk, lambda p, b: (0, p, b, 0, 0))
    kp_spec = pl.BlockSpec(blk, lambda p, b: (1, p, jnp.maximum(b - 1, 0), 0, 0))
    kc_spec = pl.BlockSpec(blk, lambda p, b: (1, p, b, 0, 0))
    vp_spec = pl.BlockSpec(blk, lambda p, b: (2, p, jnp.maximum(b - 1, 0), 0, 0))
    vc_spec = pl.BlockSpec(blk, lambda p, b: (2, p, b, 0, 0))
    return [q_spec, kp_spec, kc_spec, vp_spec, vc_spec]


def _attn_fwd(qkv, *, name):
    nb = qkv.shape[2]

    def body(q_ref, kp_ref, kc_ref, vp_ref, vc_ref, o_ref, lse_ref):
        mask = _band_mask(pl.program_id(0), pl.program_id(1))
        mask2 = jnp.concatenate([mask, mask], axis=0)
        kk = jnp.concatenate([kp_ref[...], kc_ref[...]], axis=0)
        vv = jnp.concatenate([vp_ref[...], vc_ref[...]], axis=0)
        for hp in range(HEADS // 2):
            cols = slice(hp * LANES, (hp + 1) * LANES)
            s = _dot_nt(_split_heads(q_ref[:, cols]), kk[:, cols])
            s = jnp.where(mask2, s, NEG)
            m = jnp.max(s, axis=-1, keepdims=True)
            e = jnp.exp(s - m)
            l = jnp.sum(e, axis=-1, keepdims=True)
            pr = (e * (1.0 / l)).astype(BF16)
            o_ref[:, cols] = _dot(jnp.concatenate([pr[:QBLK], pr[QBLK:]], axis=1), _split_heads(vv[:, cols]))
            lse = m + jnp.log(l)
            lse_ref[:, cols] = _merge_heads(lse[:QBLK], lse[QBLK:])

    out = pl.BlockSpec((None, None, QBLK, D_ATTN), lambda p, b: (p, b, 0, 0))
    return pl.pallas_call(
        body, grid=(N_PATTERNS, nb), in_specs=_attn_specs(nb), out_specs=[out, out],
        out_shape=[jax.ShapeDtypeStruct((N_PATTERNS, nb, QBLK, D_ATTN), F32)] * 2,
        compiler_params=_params("arbitrary", "arbitrary"), name=name)(qkv, qkv, qkv, qkv, qkv)


def _attn_combine(o3, lse3, B, S, *, name):
    def body(o_ref, l_ref, a_ref, lt_ref, so_ref, sl_ref):
        for p in range(N_PATTERNS):
            _unpermute_in(lambda r0, n, p=p: o_ref[p, pl.ds(r0, n), :], so_ref.at[p], p, S)
            _unpermute_in(lambda r0, n, p=p: l_ref[p, pl.ds(r0, n), :], sl_ref.at[p], p, S)
        l0, l1, l2 = sl_ref[0], sl_ref[1], sl_ref[2]
        m = jnp.maximum(jnp.maximum(l0, l1), l2)
        w0, w1, w2 = jnp.exp(l0 - m), jnp.exp(l1 - m), jnp.exp(l2 - m)
        tot = w0 + w1 + w2
        a_ref[...] = (w0 * so_ref[0] + w1 * so_ref[1] + w2 * so_ref[2]) / tot
        lt_ref[...] = m + jnp.log(tot)

    o3 = o3.reshape(N_PATTERNS, B, S, D_ATTN)
    lse3 = lse3.reshape(N_PATTERNS, B, S, D_ATTN)
    inp = pl.BlockSpec((N_PATTERNS, None, S, LANES), lambda b, c: (0, b, 0, c))
    out = pl.BlockSpec((S, LANES), lambda b, c: (b, c))
    return pl.pallas_call(
        body, grid=(B, D_ATTN // LANES), in_specs=[inp, inp], out_specs=[out, out],
        out_shape=[jax.ShapeDtypeStruct((B * S, D_ATTN), F32)] * 2,
        scratch_shapes=[pltpu.VMEM((N_PATTERNS, S, LANES), F32)] * 2,
        compiler_params=_params("arbitrary", "arbitrary"), name=name)(o3, lse3)


STAT_D = 8


def _attn_bwd_prep(dattn, attn, lse, B, S, *, name):
    emat = _seg_matrix(LANES)
    ncc = D_ATTN // LANES

    def body(da_ref, a_ref, l_ref, e_ref, do_ref, st_ref, scr_ref, nat_ref):
        cc = pl.program_id(1)
        da = da_ref[...]
        dsum = _seg_mean(da * a_ref[...], e_ref, 1.0)
        scr_ref[...] = da
        for p, d in enumerate(DILATIONS):
            L = S // d
            if d == 1:
                do_ref[p] = da.astype(BF16)
            else:
                for r in range(d):
                    do_ref[p, r * L:(r + 1) * L, :] = scr_ref[pl.ds(r, L, stride=d), :].astype(BF16)

        lane = lax.broadcasted_iota(jnp.int32, (S, LANES), 1)
        h0 = 2 * cc
        vals = ((h0, l_ref[:, 0:1]), (h0 + 1, l_ref[:, HEAD_DIM:HEAD_DIM + 1]),
                (STAT_D + h0, dsum[:, 0:1]), (STAT_D + h0 + 1, dsum[:, HEAD_DIM:HEAD_DIM + 1]))
        tile = jnp.where(cc == 0, 0.0, nat_ref[...])
        for at, col in vals:
            tile = jnp.where(lane == at, col, tile)
        nat_ref[...] = tile

        @pl.when(cc == ncc - 1)
        def _():
            for p, d in enumerate(DILATIONS):
                L = S // d
                if d == 1:
                    st_ref[p] = nat_ref[...]
                else:
                    for r in range(d):
                        st_ref[p, r * L:(r + 1) * L, :] = nat_ref[pl.ds(r, L, stride=d), :]

    inp = pl.BlockSpec((S, LANES), lambda b, c: (b, c))
    return pl.pallas_call(
        body, grid=(B, ncc),
        in_specs=[inp, inp, inp, pl.BlockSpec((LANES, LANES), lambda b, c: (0, 0))],
        out_specs=[pl.BlockSpec((N_PATTERNS, None, S, LANES), lambda b, c: (0, b, 0, c)),
                   pl.BlockSpec((N_PATTERNS, None, S, LANES), lambda b, c: (0, b, 0, 0))],
        out_shape=[jax.ShapeDtypeStruct((N_PATTERNS, B, S, D_ATTN), BF16),
                   jax.ShapeDtypeStruct((N_PATTERNS, B, S, LANES), F32)],
        scratch_shapes=[pltpu.VMEM((S, LANES), F32), pltpu.VMEM((S, LANES), F32)],
        compiler_params=_params("arbitrary", "arbitrary"), name=name)(dattn, attn, lse, emat)


def _attn_bwd(qkv, do3, st3, *, name, phase=None):
    nb = qkv.shape[2]

    def body(q_ref, kp_ref, kc_ref, vp_ref, vc_ref, do_ref, st_ref, out_ref, carry_ref):
        p = pl.program_id(0)
        b = pl.program_id(1)

        @pl.when((p == 0) & (b == 0))
        def _():
            carry_ref[...] = jnp.zeros_like(carry_ref)

        mask = _band_mask(p, b) & (b < nb)
        mask2 = jnp.concatenate([mask, mask], axis=0)
        kk = jnp.concatenate([kp_ref[...], kc_ref[...]], axis=0)
        vv = jnp.concatenate([vp_ref[...], vc_ref[...]], axis=0)
        for hp in range(HEADS // 2):
            cols = slice(hp * LANES, (hp + 1) * LANES)
            h0, h1 = 2 * hp, 2 * hp + 1
            kh, vh = kk[:, cols], vv[:, cols]
            q2 = _split_heads(q_ref[:, cols])
            do2 = _split_heads(do_ref[:, cols])
            lse = jnp.concatenate([st_ref[:, h0:h0 + 1], st_ref[:, h1:h1 + 1]], axis=0)
            dsum = jnp.concatenate([st_ref[:, STAT_D + h0:STAT_D + h0 + 1], st_ref[:, STAT_D + h1:STAT_D + h1 + 1]], axis=0)
            s = _dot_nt(q2, kh)
            pr = jnp.where(mask2, jnp.exp(s - lse), 0.0)
            dp = _dot_nt(do2, vh)
            ds = (pr * (dp - dsum)).astype(BF16)
            prb = pr.astype(BF16)
            dq = _dot(jnp.concatenate([ds[:QBLK], ds[QBLK:]], axis=1), _split_heads(kh))
            dk = _dot_tn(ds, q2)
            dv = _dot_tn(prb, do2)
            out_ref[0, :, cols] = carry_ref[0, :, cols].astype(BF16)
            out_ref[1, :, cols] = (carry_ref[1, :, cols] + dk[:QBLK]).astype(BF16)
            out_ref[2, :, cols] = (carry_ref[2, :, cols] + dv[:QBLK]).astype(BF16)
            carry_ref[0, :, cols] = dq
            carry_ref[1, :, cols] = dk[QBLK:]
            carry_ref[2, :, cols] = dv[QBLK:]

    def at(which, back):
        return lambda p, b: (which, p, jnp.maximum(jnp.minimum(b, nb - 1) - back, 0), 0, 0)

    blk = (None, None, None, QBLK, D_ATTN)
    aux = lambda lanes: pl.BlockSpec((None, None, QBLK, lanes), lambda p, b: (p, jnp.minimum(b, nb - 1), 0, 0))
    return _call(
        body, grid=(N_PATTERNS, nb + 1),
        in_specs=[pl.BlockSpec(blk, at(0, 0)), pl.BlockSpec(blk, at(1, 1)), pl.BlockSpec(blk, at(1, 0)),
                  pl.BlockSpec(blk, at(2, 1)), pl.BlockSpec(blk, at(2, 0)), aux(D_ATTN), aux(LANES)],
        out_specs=[pl.BlockSpec((3, None, None, QBLK, D_ATTN), lambda p, b: (0, p, jnp.maximum(b - 1, 0), 0, 0))],
        out_shape=[jax.ShapeDtypeStruct((3, N_PATTERNS, nb, QBLK, D_ATTN), BF16)],
        scratch_shapes=[pltpu.VMEM((3, QBLK, D_ATTN), F32)],
        sem=("arbitrary", "arbitrary"), name=name, args=(qkv, qkv, qkv, qkv, qkv, do3, st3), phase=phase)


def _attn_grad_combine(cur, u, gains, B, S, *, name, phase=None):
    emat = _seg_matrix(LANES)

    def body(cur_ref, u_ref, g_ref, e_ref, du_ref, dg_ref, scr_ref):
        c = pl.program_id(0)
        b = pl.program_id(2)
        for p in range(N_PATTERNS):
            _unpermute_in(lambda r0, n, p=p: cur_ref[p, pl.ds(r0, n), :].astype(F32), scr_ref.at[p], p, S)
        dy = scr_ref[0] + scr_ref[1] + scr_ref[2]
        xv = u_ref[...]
        gain = g_ref[...]
        ms = _seg_mean(xv * xv, e_ref, HEAD_DIM)
        r = lax.rsqrt(ms + EPS)
        xhat = xv * r
        dxh = dy * gain
        dx = r * (dxh - xhat * _seg_mean(dxh * xhat, e_ref, HEAD_DIM))
        du_ref[...] = jnp.where(c < 2, dx, dy).astype(BF16)

        @pl.when((b == 0))
        def _():
            dg_ref[...] = jnp.zeros_like(dg_ref)

        dg_ref[...] += jnp.sum(dy * xhat, axis=0, keepdims=True)

    cur = cur.reshape(3, N_PATTERNS, B, S, D_ATTN)
    ncc = D_ATTN // LANES
    return _call(
        body, grid=(3, ncc, B),
        in_specs=[pl.BlockSpec((None, N_PATTERNS, None, S, LANES), lambda c, cc, b: (c, 0, b, 0, cc)),
                  pl.BlockSpec((S, LANES), lambda c, cc, b: (b, c * ncc + cc)),
                  pl.BlockSpec((None, 1, LANES), lambda c, cc, b: (c, 0, cc)),
                  pl.BlockSpec((LANES, LANES), lambda c, cc, b: (0, 0))],
        out_specs=[pl.BlockSpec((S, LANES), lambda c, cc, b: (b, c * ncc + cc)),
                   pl.BlockSpec((None, 1, LANES), lambda c, cc, b: (c, 0, cc))],
        out_shape=[jax.ShapeDtypeStruct((B * S, 3 * D_ATTN), BF16), jax.ShapeDtypeStruct((3, 1, D_ATTN), F32)],
        scratch_shapes=[pltpu.VMEM((N_PATTERNS, S, LANES), F32)],
        sem=("arbitrary", "arbitrary", "arbitrary"), name=name, args=(cur, u, gains, emat), phase=phase)


HALO = 32
SUB = 64


def _conv_fwd(u, cw, cb, lg, lb, B, S, *, tc, name):
    nchunk = S // tc
    hb = tc // HALO

    def body(ca_ref, cap_ref, cg_ref, cgp_ref, w_ref, cb_ref, lg_ref, lb_ref, cv_ref, glu_ref, y_ref, pad_ref):
        i = pl.program_id(1)
        glu = ca_ref[...] * _sigmoid(cg_ref[...])
        glu_ref[...] = glu
        prev = cap_ref[...] * _sigmoid(cgp_ref[...])
        pad_ref[0:HALO, :] = jnp.where(i > 0, prev, 0.0)
        pad_ref[HALO:, :] = glu
        for sub in range(tc // SUB):
            acc = jnp.zeros((SUB, D_CONV), F32) + cb_ref[...]
            for k in range(CONV_K):
                acc = acc + pad_ref[pl.ds(sub * SUB + HALO - (CONV_K - 1) + k, SUB), :] * w_ref[pl.ds(k, 1), :]
            y_ref[sub * SUB:(sub + 1) * SUB, :] = acc
        y = y_ref[...]
        mu = jnp.mean(y, axis=-1, keepdims=True)
        yc = y - mu
        var = jnp.mean(yc * yc, axis=-1, keepdims=True)
        z = yc * lax.rsqrt(var + EPS) * lg_ref[...] + lb_ref[...]
        cv_ref[...] = (z * _sigmoid(z)).astype(BF16)

    def cur(col):
        return pl.BlockSpec((tc, D_CONV), lambda b, i: (b * nchunk + i, col))

    def halo(col):
        return pl.BlockSpec((HALO, D_CONV), lambda b, i: (jnp.maximum((b * nchunk + i) * hb - 1, 0), col))

    vec = pl.BlockSpec((1, D_CONV), lambda b, i: (0, 0))
    out = pl.BlockSpec((tc, D_CONV), lambda b, i: (b * nchunk + i, 0))
    return pl.pallas_call(
        body, grid=(B, nchunk),
        in_specs=[cur(3), halo(3), cur(4), halo(4), pl.BlockSpec((CONV_K, D_CONV), lambda b, i: (0, 0)), vec, vec, vec],
        out_specs=[out, out, out],
        out_shape=[jax.ShapeDtypeStruct((B * S, D_CONV), BF16), jax.ShapeDtypeStruct((B * S, D_CONV), F32),
                   jax.ShapeDtypeStruct((B * S, D_CONV), F32)],
        scratch_shapes=[pltpu.VMEM((tc + HALO, D_CONV), F32)],
        compiler_params=_params("arbitrary", "arbitrary"), name=name)(u, u, u, u, cw, cb, lg, lb)


def _conv_bwd_norm(dcv, y, lg, lb, *, tc, name):
    T = y.shape[0]

    def body(dcv_ref, y_ref, lg_ref, lb_ref, dy_ref, part_ref):
        yv = y_ref[...]
        mu = jnp.mean(yv, axis=-1, keepdims=True)
        yc = yv - mu
        var = jnp.mean(yc * yc, axis=-1, keepdims=True)
        rstd = lax.rsqrt(var + EPS)
        xhat = yc * rstd
        z = xhat * lg_ref[...] + lb_ref[...]
        sig = _sigmoid(z)
        dz = dcv_ref[...] * (sig * (1.0 + z * (1.0 - sig)))
        dxh = dz * lg_ref[...]
        dy = rstd * (dxh - jnp.mean(dxh, axis=-1, keepdims=True)
                     - xhat * jnp.mean(dxh * xhat, axis=-1, keepdims=True))
        dy_ref[...] = dy

        @pl.when(pl.program_id(0) == 0)
        def _():
            part_ref[...] = jnp.zeros_like(part_ref)

        part_ref[0:1, :] += jnp.sum(dz * xhat, axis=0, keepdims=True)
        part_ref[1:2, :] += jnp.sum(dz, axis=0, keepdims=True)
        part_ref[2:3, :] += jnp.sum(dy, axis=0, keepdims=True)

    tok = pl.BlockSpec((tc, D_CONV), lambda i: (i, 0))
    vec = pl.BlockSpec((1, D_CONV), lambda i: (0, 0))
    return pl.pallas_call(
        body, grid=(T // tc,), in_specs=[tok, tok, vec, vec],
        out_specs=[tok, pl.BlockSpec((8, D_CONV), lambda i: (0, 0))],
        out_shape=[jax.ShapeDtypeStruct((T, D_CONV), F32), jax.ShapeDtypeStruct((8, D_CONV), F32)],
        compiler_params=_params("arbitrary"), name=name)(dcv, y, lg, lb)


def _conv_bwd_taps(dy, glu, u, cw, B, S, *, tc, name, phase=None):
    nchunk = S // tc
    hb = tc // HALO
    last_hb = B * S // HALO - 1

    def body(dy_ref, dyn_ref, glu_ref, glup_ref, ca_ref, cg_ref, w_ref, dca_ref, dcg_ref, dw_ref, dyp_ref, glp_ref, acc_ref):
        b = pl.program_id(0)
        i = pl.program_id(1)
        dy = dy_ref[...]
        dyp_ref[0:tc, :] = dy
        dyp_ref[tc:, :] = jnp.where(i < nchunk - 1, dyn_ref[...], 0.0)
        glp_ref[0:HALO, :] = jnp.where(i > 0, glup_ref[...], 0.0)
        glp_ref[HALO:, :] = glu_ref[...]

        @pl.when((b == 0) & (i == 0))
        def _():
            dw_ref[...] = jnp.zeros_like(dw_ref)

        for sub in range(tc // SUB):
            acc = jnp.zeros((SUB, D_CONV), F32)
            for k in range(CONV_K):
                acc = acc + dyp_ref[pl.ds(sub * SUB + (CONV_K - 1) - k, SUB), :] * w_ref[pl.ds(k, 1), :]
            acc_ref[sub * SUB:(sub + 1) * SUB, :] = acc
        for k in range(CONV_K):
            dw_ref[k:k + 1, :] += jnp.sum(dy * glp_ref[pl.ds(HALO - (CONV_K - 1) + k, tc), :], axis=0, keepdims=True)
        dglu = acc_ref[...]
        ca = ca_ref[...]
        sig = _sigmoid(cg_ref[...])
        dca_ref[...] = (dglu * sig).astype(BF16)
        dcg_ref[...] = (dglu * ca * sig * (1.0 - sig)).astype(BF16)

    tok = pl.BlockSpec((tc, D_CONV), lambda b, i: (b * nchunk + i, 0))
    nxt = pl.BlockSpec((HALO, D_CONV), lambda b, i: (jnp.minimum((b * nchunk + i + 1) * hb, last_hb), 0))
    prv = pl.BlockSpec((HALO, D_CONV), lambda b, i: (jnp.maximum((b * nchunk + i) * hb - 1, 0), 0))
    return _call(
        body, grid=(B, nchunk),
        in_specs=[tok, nxt, tok, prv,
                  pl.BlockSpec((tc, D_CONV), lambda b, i: (b * nchunk + i, 3)),
                  pl.BlockSpec((tc, D_CONV), lambda b, i: (b * nchunk + i, 4)),
                  pl.BlockSpec((CONV_K, D_CONV), lambda b, i: (0, 0))],
        out_specs=[tok, tok, pl.BlockSpec((32, D_CONV), lambda b, i: (0, 0))],
        out_shape=[jax.ShapeDtypeStruct((B * S, D_CONV), BF16), jax.ShapeDtypeStruct((B * S, D_CONV), BF16),
                   jax.ShapeDtypeStruct((32, D_CONV), F32)],
        scratch_shapes=[pltpu.VMEM((tc + HALO, D_CONV), F32), pltpu.VMEM((tc + HALO, D_CONV), F32),
                        pltpu.VMEM((tc, D_CONV), F32)],
        sem=("arbitrary", "arbitrary"), name=name, args=(dy, dy, glu, glu, u, u, cw), phase=phase)


def _outproj_fwd(h, attn, cv, wout, *, tm, name):
    T, D = h.shape

    def body(h_ref, a_ref, c_ref, w_ref, o_ref):
        o_ref[...] = (h_ref[...] + _dot(a_ref[...].astype(BF16), w_ref[0:D_ATTN, :])
                      + _dot(c_ref[...], w_ref[D_ATTN:, :]))

    tok = pl.BlockSpec((tm, D), lambda i: (i, 0))
    half = pl.BlockSpec((tm, D_ATTN), lambda i: (i, 0))
    return pl.pallas_call(
        body, grid=(T // tm,), in_specs=[tok, half, half, pl.BlockSpec(wout.shape, lambda i: (0, 0))],
        out_specs=tok, out_shape=jax.ShapeDtypeStruct((T, D), F32),
        compiler_params=_params("arbitrary"), name=name)(h, attn, cv, wout)


def _outproj_bwd(dh, attn, cv, wout, *, tm, name):
    T, D = dh.shape

    def body(dh_ref, a_ref, c_ref, w_ref, da_ref, dc_ref, dw_ref):
        @pl.when(pl.program_id(0) == 0)
        def _():
            dw_ref[...] = jnp.zeros_like(dw_ref)

        dhb = dh_ref[...].astype(BF16)
        da_ref[...] = _dot_nt(dhb, w_ref[0:D_ATTN, :])
        dc_ref[...] = _dot_nt(dhb, w_ref[D_ATTN:, :])
        dw_ref[0:D_ATTN, :] += _dot_tn(a_ref[...].astype(BF16), dhb)
        dw_ref[D_ATTN:, :] += _dot_tn(c_ref[...], dhb)

    tok = pl.BlockSpec((tm, D), lambda i: (i, 0))
    half = pl.BlockSpec((tm, D_ATTN), lambda i: (i, 0))
    wspec = pl.BlockSpec(wout.shape, lambda i: (0, 0))
    return pl.pallas_call(
        body, grid=(T // tm,), in_specs=[tok, half, half, wspec], out_specs=[half, half, wspec],
        out_shape=[jax.ShapeDtypeStruct((T, D_ATTN), F32), jax.ShapeDtypeStruct((T, D_ATTN), F32),
                   jax.ShapeDtypeStruct(wout.shape, F32)],
        compiler_params=_params("arbitrary"), name=name)(dh, attn, cv, wout)


ADAM_BLOCK_BYTES = 3 * 512 * 1024


def _adamw(w, g, m, v, *, name):
    R, C = w.shape
    tr = R
    for cand in (512, 352, 256, 176, 128, 64, 32, 16, 8):
        if R % cand == 0 and cand * C * 4 <= ADAM_BLOCK_BYTES:
            tr = cand
            break
    c1 = 1.0 - ADAM_B1 ** ADAM_STEP
    c2 = 1.0 - ADAM_B2 ** ADAM_STEP

    def body(w_ref, g_ref, m_ref, v_ref, d_ref, nm_ref, nv_ref):
        gv = g_ref[...]
        nm = ADAM_B1 * m_ref[...] + (1.0 - ADAM_B1) * gv
        nv = ADAM_B2 * v_ref[...] + (1.0 - ADAM_B2) * (gv * gv)
        d_ref[...] = -ADAM_LR * ((nm / c1) / (jnp.sqrt(nv / c2) + ADAM_EPS) + ADAM_WD * w_ref[...])
        nm_ref[...] = nm
        nv_ref[...] = nv

    blk = pl.BlockSpec((tr, C), lambda i: (i, 0))
    return pl.pallas_call(
        body, grid=(R // tr,), in_specs=[blk] * 4, out_specs=[blk] * 3,
        out_shape=[jax.ShapeDtypeStruct((R, C), F32)] * 3,
        compiler_params=_params("arbitrary"), name=name)(w, g, m, v)


TM = 512
TM_FFN_FWD = 1024
TK = 1024
TC = 256


def _local_step(x, tgt, w, overlap=None):
    B, S, D = x.shape
    T = B * S
    x2 = x.reshape(T, D)
    t2 = tgt.reshape(T, D)
    ones = jnp.ones((1, D_ATTN), F32)
    scale = HEAD_DIM ** -0.5
    gains = jnp.stack([jnp.tile(w["q_norm"], (1, HEADS)) * scale, jnp.tile(w["k_norm"], (1, HEADS)), ones])
    g = {}

    def hosting(point, build):
        phase = overlap.phase(point, w, g) if overlap is not None else None
        if phase is None:
            return build(None)
        outs, extra = build(phase)
        overlap.done(point, extra, w, g)
        return outs

    h1, n1, G1, U1 = hosting("ffn1_fwd", lambda ph: _ffn_fwd(
        x2, w["ffn1_norm"], w["wg1"], w["wu1"], w["wd1"], None, tm=TM_FFN_FWD, name="ffn1_fwd", phase=ph))
    u, n2 = hosting("inproj_fwd", lambda ph: _inproj_fwd(h1, w["mix_norm"], w["win"], tm=TM, name="inproj_fwd", phase=ph))
    qkv = _qkv_prep(u, gains, B, S, name="qkv_prep")
    qkv = qkv.reshape(3, N_PATTERNS, T // QBLK, QBLK, D_ATTN)
    o3, lse3 = _attn_fwd(qkv, name="attn_fwd")
    attn, lse = _attn_combine(o3, lse3, B, S, name="attn_combine")
    cv, glu, yconv = _conv_fwd(u, w["conv_w"], w["conv_b"], w["conv_ln_g"], w["conv_ln_b"], B, S, tc=TC, name="conv_fwd")
    h2 = _outproj_fwd(h1, attn, cv, w["wout"], tm=TM, name="outproj_fwd")
    dh3, n3, G2, U2, loss = _ffn_fwd(h2, w["ffn2_norm"], w["wg2"], w["wu2"], w["wd2"], t2, tm=TM_FFN_FWD, name="ffn2_fwd")

    dG, dU, A, dy, dh2, g["ffn2_norm"] = _ffn_bwd_act(dh3, h2, w["ffn2_norm"], G2, U2, w["wg2"], w["wu2"], w["wd2"],
                                                    tm=TM, name="ffn2_bwd_act")
    g["wg2"], g["wu2"], g["wd2"] = _ffn_bwd_w(n3, dy, dG, dU, A, tk=TK, name="ffn2_bwd_w")
    dattn, dcv, g["wout"] = _outproj_bwd(dh2, attn, cv, w["wout"], tm=TM, name="outproj_bwd")
    dyc, cpart = _conv_bwd_norm(dcv, yconv, w["conv_ln_g"], w["conv_ln_b"], tc=TC, name="conv_bwd_norm")
    dca, dcg, dcw = hosting("conv_bwd_taps", lambda ph: _conv_bwd_taps(
        dyc, glu, u, w["conv_w"], B, S, tc=TC, name="conv_bwd_taps", phase=ph))
    do3, st3 = _attn_bwd_prep(dattn, attn, lse, B, S, name="attn_bwd_prep")
    nb = T // QBLK
    (cur,) = hosting("attn_bwd", lambda ph: _attn_bwd(
        qkv, do3.reshape(N_PATTERNS, nb, QBLK, D_ATTN), st3.reshape(N_PATTERNS, nb, QBLK, LANES),
        name="attn_bwd", phase=ph))
    du_qkv, dgains = hosting("attn_grad_combine", lambda ph: _attn_grad_combine(
        cur, u, gains, B, S, name="attn_grad_combine", phase=ph))
    du = jnp.concatenate([du_qkv, dca, dcg], axis=1)
    dh1, g["mix_norm"] = hosting("inproj_bwd_act", lambda ph: _inproj_bwd_act(
        du, dh2, h1, w["mix_norm"], w["win"], tm=TM, name="inproj_bwd_act", phase=ph))
    g["win"] = _inproj_bwd_w(n2, du, w["win"].shape[0], tk=TK, name="inproj_bwd_w")
    dG, dU, A, dy, dx, g["ffn1_norm"] = _ffn_bwd_act(dh1, x2, w["ffn1_norm"], G1, U1, w["wg1"], w["wu1"], w["wd1"],
                                                   tm=TM, name="ffn1_bwd_act")
    g["wg1"], g["wu1"], g["wd1"] = _ffn_bwd_w(n1, dy, dG, dU, A, tk=TK, name="ffn1_bwd_w")

    g["q_norm"] = dgains[0].reshape(HEADS, HEAD_DIM).sum(axis=0, keepdims=True) * scale
    g["k_norm"] = dgains[1].reshape(HEADS, HEAD_DIM).sum(axis=0, keepdims=True)
    g["conv_ln_g"] = cpart[0:1]
    g["conv_ln_b"] = cpart[1:2]
    g["conv_b"] = cpart[2:3]
    g["conv_w"] = dcw[:CONV_K]
    return loss, dx.reshape(B, S, D), g


N_CHIPS = 4
N_DEV = 8
VMEM_SPEC = pl.BlockSpec(memory_space=pltpu.VMEM)


def _remote(src, dst, send_sem, recv_sem, device):
    return pltpu.make_async_remote_copy(src_ref=src, dst_ref=dst, send_sem=send_sem, recv_sem=recv_sem,
                                        device_id=device, device_id_type=MESH)


def _stage_shards(shards, dtypes, *, name):
    n = len(shards)
    halves = [s.reshape(2, s.shape[0] // 2, s.shape[1]) for s in shards]

    def body(*refs):
        ins, outs, vms, loc_sems = refs[:n], refs[n:2 * n], refs[2 * n:3 * n], refs[3 * n]
        me = 2 * lax.axis_index("x") + lax.axis_index("y")
        copies = []
        for a in range(n):
            vms[a][...] = ins[a][...].astype(dtypes[a])
            cp = pltpu.make_async_copy(vms[a], outs[a].at[me], loc_sems.at[a])
            cp.start()
            copies.append(cp)
        for cp in copies:
            cp.wait()

    return pl.pallas_call(
        body, in_specs=[VMEM_SPEC] * n, out_specs=[ANY] * n,
        out_shape=[jax.ShapeDtypeStruct((N_CHIPS,) + h.shape, dt) for h, dt in zip(halves, dtypes)],
        scratch_shapes=[pltpu.VMEM(h.shape, dt) for h, dt in zip(halves, dtypes)] + [DMA_SEMS((n,))],
        compiler_params=pltpu.CompilerParams(vmem_limit_bytes=VMEM_LIMIT), name=name)(*halves)


def _like(arrays):
    return [jax.ShapeDtypeStruct(a.shape, a.dtype) for a in arrays]


def _axes():
    x, y, c = lax.axis_index("x"), lax.axis_index("y"), lax.axis_index("c")
    first = (x + (1 - c) * (1 - 2 * x), y + c * (1 - 2 * y))
    second = (x + c * (1 - 2 * x), y + (1 - c) * (1 - 2 * y))
    slots = tuple(2 * px + py for px, py in ((x, y), first, second, (1 - x, 1 - y)))
    return (x, y, c), (*first, c), (*second, c), slots


def _gather_ici_phase(bufs):
    n = len(bufs)

    def stage1(ins, outs, sems):
        (x, y, c), peer1, peer2, (own, s1, s2, both) = _axes()
        starts, arrivals = [], []
        for a in range(n):
            mine, land = outs[a].at[own, c], outs[a].at[s2, c]
            starts.append(_remote(mine, mine, *sems(a), peer2))
            arrivals.append(_remote(land, land, *sems(a), peer2))
        return starts, arrivals

    def stage2(ins, outs, sems):
        (x, y, c), peer1, peer2, (own, s1, s2, both) = _axes()
        starts, arrivals = [], []
        for a in range(n):
            for k, (src, dst) in enumerate(((own, s1), (s2, both))):
                mine, land = outs[a].at[src, c], outs[a].at[dst, c]
                starts.append(_remote(mine, mine, *sems(2 * a + k), peer1))
                arrivals.append(_remote(land, land, *sems(2 * a + k), peer1))
        return starts, arrivals

    same = {a: a for a in range(n)}
    return _Phase(bufs, _like(bufs), same, n, stage1).then(_Phase(bufs, _like(bufs), same, 2 * n, stage2))


def _gather_d2d_phase(bufs):
    n = len(bufs)

    def copies(ins, outs, sems):
        (x, y, c), peer1, peer2, (own, s1, s2, both) = _axes()
        starts, arrivals = [], []
        for a in range(n):
            for j, s in enumerate((s1, s2, both)):
                got, land = outs[a].at[s, c], outs[a].at[s, 1 - c]
                starts.append(_remote(got, got, *sems(3 * a + j), (x, y, 1 - c)))
                arrivals.append(_remote(land, land, *sems(3 * a + j), (x, y, 1 - c)))
        return starts, arrivals

    return _Phase(bufs, _like(bufs), {a: a for a in range(n)}, 3 * n, copies)


def _exchange_phase(views):
    n = len(views)

    def copies(ins, outs, sems):
        x, y, c = lax.axis_index("x"), lax.axis_index("y"), lax.axis_index("c")
        starts = [_remote(ins[a].at[pl.ds(0, ins[a].shape[0]), 1 - c], outs[a], *sems(a), (x, y, 1 - c))
                  for a in range(n)]
        return starts, starts

    outs = [jax.ShapeDtypeStruct((v.shape[0],) + v.shape[2:], F32) for v in views]
    return _Phase(views, outs, {}, n, copies)


def _row_block(rows):
    for cand in (256, 176, 128, 64, 32, 16, 8):
        if rows % cand == 0:
            return cand
    return rows


def _add_own_half(g, r, sel, *, name):
    ns, _, rh, cdim = g.shape
    tr = _row_block(rh)

    def body(s_ref, gk_ref, rk_ref, gs_ref, rs_ref, keep_ref, send_ref):
        keep_ref[...] = gk_ref[...] + rk_ref[...]
        send_ref[...] = (gs_ref[...] + rs_ref[...]).astype(BF16)

    def g_spec(off):
        return pl.BlockSpec((None, None, tr, cdim), lambda k, i, s: (s[1 + off + k], s[0], i, 0))

    def r_spec(off):
        return pl.BlockSpec((None, tr, cdim), lambda k, i, s: (s[1 + off + k], i, 0))

    out = pl.BlockSpec((None, tr, cdim), lambda k, i, s: (k, i, 0))
    return pl.pallas_call(
        body,
        grid_spec=pltpu.PrefetchScalarGridSpec(
            num_scalar_prefetch=1, grid=(2, rh // tr),
            in_specs=[g_spec(0), r_spec(0), g_spec(2), r_spec(2)], out_specs=[out, out]),
        out_shape=[jax.ShapeDtypeStruct((2, rh, cdim), F32), jax.ShapeDtypeStruct((2, rh, cdim), BF16)],
        compiler_params=_params("arbitrary", "arbitrary"), name=name)(sel, g, r, g, r)


def _swap_phase(arrays, stage):
    n = len(arrays)

    def copies(ins, outs, sems):
        peer = _axes()[stage]
        starts = [_remote(ins[a], outs[a], *sems(a), peer) for a in range(n)]
        return starts, starts

    return _Phase(arrays, _like(arrays), {}, n, copies)


def _add_stage1(keep, got, *, name):
    _, rh, cdim = keep.shape
    tr = _row_block(rh)

    def body(k_ref, g_ref, keep_ref, send_ref):
        keep_ref[...] = k_ref[0] + g_ref[0].astype(F32)
        send_ref[...] = (k_ref[1] + g_ref[1].astype(F32)).astype(BF16)

    blk2 = pl.BlockSpec((2, tr, cdim), lambda i: (0, i, 0))
    blk = pl.BlockSpec((tr, cdim), lambda i: (i, 0))
    return pl.pallas_call(
        body, grid=(rh // tr,), in_specs=[blk2, blk2], out_specs=[blk, blk],
        out_shape=[jax.ShapeDtypeStruct((rh, cdim), F32), jax.ShapeDtypeStruct((rh, cdim), BF16)],
        compiler_params=_params("arbitrary"), name=name)(keep, got)


def _add_stage2(keep, got, sel, *, name):
    rh, cdim = keep.shape
    tr = _row_block(rh)

    def body(s_ref, k_ref, g_ref, o_ref):
        o_ref[...] = k_ref[...] + g_ref[...].astype(F32)

    blk = pl.BlockSpec((tr, cdim), lambda i, s: (i, 0))
    return pl.pallas_call(
        body,
        grid_spec=pltpu.PrefetchScalarGridSpec(
            num_scalar_prefetch=1, grid=(rh // tr,), in_specs=[blk, blk],
            out_specs=pl.BlockSpec((None, tr, cdim), lambda i, s: (s[0], i, 0))),
        out_shape=jax.ShapeDtypeStruct((2, rh, cdim), F32),
        compiler_params=_params("arbitrary"), name=name)(sel, keep, got)


def _join_phase(halves):
    n = len(halves)

    def copies(ins, outs, sems):
        x, y, c = lax.axis_index("x"), lax.axis_index("y"), lax.axis_index("c")
        starts, arrivals = [], []
        for a in range(n):
            mine, land = outs[a].at[c], outs[a].at[1 - c]
            starts.append(_remote(mine, mine, *sems(a), (x, y, 1 - c)))
            arrivals.append(_remote(land, land, *sems(a), (x, y, 1 - c)))
        return starts, arrivals

    return _Phase(halves, _like(halves), {a: a for a in range(n)}, n, copies)


def _slot_order():
    x, y, c = lax.axis_index("x"), lax.axis_index("y"), lax.axis_index("c")
    own, flip_x, flip_y, both = 2 * x + y, 2 * (1 - x) + y, 2 * x + 1 - y, 2 * (1 - x) + 1 - y
    first = jnp.where(c == 0, flip_x, flip_y)
    second = jnp.where(c == 0, flip_y, flip_x)
    return jnp.stack([c, own, second, first, both]).astype(jnp.int32)


def _reduce_scatter(grads):
    sel = _slot_order()
    views = [_half_view(g) for g in grads]
    got = _run_phase(_exchange_phase(views), name="rs_exchange_halves")
    keep, send = _add_halves(views, got, sel, "late")
    got = _run_phase(_swap_phase(send, 1), name="rs_swap_first_axis")
    keep, send = _add_first(keep, got, "late")
    got = _run_phase(_swap_phase(send, 2), name="rs_swap_second_axis")
    halves = _add_second(keep, got, sel, "late")
    full = _run_phase(_join_phase(halves), name="rs_join_halves")
    return [f.reshape(g.shape[1], g.shape[2]) for f, g in zip(full, grads)]


def _half_view(g):
    return g.reshape(N_CHIPS, 2, g.shape[1] // 2, g.shape[2])


def _add_halves(views, got, sel, tag):
    keep, send = zip(*[_add_own_half(v, r, sel, name=f"rs_add_half_{tag}{a}") for a, (v, r) in enumerate(zip(views, got))])
    return list(keep), list(send)


def _add_first(keep, got, tag):
    keep, send = zip(*[_add_stage1(k, r, name=f"rs_add_first_{tag}{a}") for a, (k, r) in enumerate(zip(keep, got))])
    return list(keep), list(send)


def _add_second(keep, got, sel, tag):
    return [_add_stage2(k, r, sel, name=f"rs_add_second_{tag}{a}") for a, (k, r) in enumerate(zip(keep, got))]


EARLY_GRADS = ("wg2", "wu2", "wd2", "wout")


class _Overlap:
    def __init__(self, staged):
        self.staged = staged
        self.sel = _slot_order()
        self.reduced = {}

    def phase(self, point, w, g):
        if point == "ffn1_fwd":
            return _gather_ici_phase(self.staged)
        if point == "inproj_fwd":
            return _gather_d2d_phase(self.ffn2)
        if point == "conv_bwd_taps":
            self.shapes = [g[k].shape for k in EARLY_GRADS]
            self.views = [_half_view(g[k].reshape(N_CHIPS, -1, g[k].shape[-1])) for k in EARLY_GRADS]
            return _exchange_phase(self.views)
        if point == "attn_bwd":
            return _swap_phase(self.send, 1)
        if point == "attn_grad_combine":
            return _swap_phase(self.send, 2)
        if point == "inproj_bwd_act":
            return _join_phase(self.halves)
        return None

    def done(self, point, outs, w, g):
        if point == "ffn1_fwd":
            win, wout, taps = [_whole(b) for b in _run_phase(_gather_d2d_phase(outs[:3]), name="gather_mix_d2d")]
            w["win"] = win
            w["wout"] = wout.reshape(-1, wout.shape[-1])
            w["conv_w"] = taps.transpose(1, 0, 2).reshape(CONV_K + 1, D_CONV)[:CONV_K]
            self.ffn2 = list(outs[3:])
        elif point == "inproj_fwd":
            w["wg2"], w["wu2"], w["wd2"] = [_whole(b) for b in outs]
        elif point == "conv_bwd_taps":
            self.keep, self.send = _add_halves(self.views, outs, self.sel, "early")
        elif point == "attn_bwd":
            self.keep, self.send = _add_first(self.keep, outs, "early")
        elif point == "attn_grad_combine":
            self.halves = _add_second(self.keep, outs, self.sel, "early")
        elif point == "inproj_bwd_act":
            for k, shp, f in zip(EARLY_GRADS, self.shapes, outs):
                self.reduced[k] = f.reshape(-1, shp[-1])


def _whole(buf):
    return buf.reshape(buf.shape[0], 2 * buf.shape[2], buf.shape[3])


def _allreduce_small(pack, *, name):
    rows = pack.shape[0]

    def body(p_ref, o_ref, buf_ref, send_sems, recv_sems):
        x, y, c = lax.axis_index("x"), lax.axis_index("y"), lax.axis_index("c")
        me = 4 * x + 2 * y + c
        buf_ref[me] = p_ref[...]
        cps = []
        for k in range(1, N_DEV):
            peer = tuple(1 - v if (k >> s) & 1 else v for v, s in ((x, 2), (y, 1), (c, 0)))
            cp = _remote(p_ref, buf_ref.at[me], send_sems.at[k - 1], recv_sems.at[k - 1], peer)
            cp.start()
            cps.append(cp)
        for k in range(1, N_DEV):
            src = 4 * (x ^ ((k >> 2) & 1)) + 2 * (y ^ ((k >> 1) & 1)) + (c ^ (k & 1))
            land = buf_ref.at[src]
            _remote(land, land, send_sems.at[k - 1], recv_sems.at[k - 1], (x, y, c)).wait_recv()
        acc = buf_ref[0]
        for d in range(1, N_DEV):
            acc = acc + buf_ref[d]
        o_ref[...] = acc
        for cp in cps:
            cp.wait_send()

    return pl.pallas_call(
        body, in_specs=[VMEM_SPEC], out_specs=VMEM_SPEC, out_shape=jax.ShapeDtypeStruct(pack.shape, F32),
        scratch_shapes=[pltpu.VMEM((N_DEV, rows, LANES), F32), pltpu.SemaphoreType.DMA((N_DEV - 1,)),
                        pltpu.SemaphoreType.DMA((N_DEV - 1,))], name=name)(pack)


SMALL = ("ffn1_norm", "mix_norm", "q_norm", "k_norm", "conv_b", "conv_ln_g", "conv_ln_b", "ffn2_norm", "conv_w")
BIG = ("ffn1_w_gate", "ffn1_w_up", "ffn1_w_down", "w_in", "w_out", "ffn2_w_gate", "ffn2_w_up", "ffn2_w_down")
TRANSPOSED = ("ffn1_w_gate", "ffn1_w_up", "ffn2_w_gate", "ffn2_w_up")
WEIGHTS = ("ffn1_norm", "ffn1_w_gate", "ffn1_w_up", "ffn1_w_down", "mix_norm", "w_in", "q_norm", "k_norm",
           "conv_w", "conv_b", "conv_ln_g", "conv_ln_b", "w_out", "ffn2_norm", "ffn2_w_gate", "ffn2_w_up",
           "ffn2_w_down")
SUBLANES = 8


def _pack(parts):
    rows = []
    for p in parts:
        flat = p.reshape(-1)
        tile = SUBLANES * LANES
        padded = -(-flat.shape[0] // tile) * tile
        rows.append(jnp.pad(flat, (0, padded - flat.shape[0])).reshape(-1, LANES))
    return jnp.concatenate(rows, axis=0)


def _unpack(pack, shapes):
    out, row = [], 0
    for shp in shapes:
        size = shp[0] * shp[1]
        tile = SUBLANES * LANES
        nrows = -(-size // tile) * SUBLANES
        out.append(pack[row:row + nrows].reshape(-1)[:size].reshape(shp))
        row += nrows
    return out


def kernel(x, ffn1_norm, ffn1_w_gate, ffn1_w_up, ffn1_w_down, mix_norm, w_in, q_norm, k_norm, conv_w, conv_b, conv_ln_g, conv_ln_b, w_out, ffn2_norm, ffn2_w_gate, ffn2_w_up, ffn2_w_down, loss_target, m_ffn1_norm, m_ffn1_w_gate, m_ffn1_w_up, m_ffn1_w_down, m_mix_norm, m_w_in, m_q_norm, m_k_norm, m_conv_w, m_conv_b, m_conv_ln_g, m_conv_ln_b, m_w_out, m_ffn2_norm, m_ffn2_w_gate, m_ffn2_w_up, m_ffn2_w_down, v_ffn1_norm, v_ffn1_w_gate, v_ffn1_w_up, v_ffn1_w_down, v_mix_norm, v_w_in, v_q_norm, v_k_norm, v_conv_w, v_conv_b, v_conv_ln_g, v_conv_ln_b, v_w_out, v_ffn2_norm, v_ffn2_w_gate, v_ffn2_w_up, v_ffn2_w_down):
    wts = dict(ffn1_norm=ffn1_norm, ffn1_w_gate=ffn1_w_gate[0], ffn1_w_up=ffn1_w_up[0], ffn1_w_down=ffn1_w_down[0],
               mix_norm=mix_norm, w_in=w_in[0], q_norm=q_norm, k_norm=k_norm, conv_w=conv_w[0], conv_b=conv_b,
               conv_ln_g=conv_ln_g, conv_ln_b=conv_ln_b, w_out=w_out[0], ffn2_norm=ffn2_norm,
               ffn2_w_gate=ffn2_w_gate[0], ffn2_w_up=ffn2_w_up[0], ffn2_w_down=ffn2_w_down[0])
    mom = dict(ffn1_norm=m_ffn1_norm, ffn1_w_gate=m_ffn1_w_gate[0], ffn1_w_up=m_ffn1_w_up[0], ffn1_w_down=m_ffn1_w_down[0],
               mix_norm=m_mix_norm, w_in=m_w_in[0], q_norm=m_q_norm, k_norm=m_k_norm, conv_w=m_conv_w[0], conv_b=m_conv_b,
               conv_ln_g=m_conv_ln_g, conv_ln_b=m_conv_ln_b, w_out=m_w_out[0], ffn2_norm=m_ffn2_norm,
               ffn2_w_gate=m_ffn2_w_gate[0], ffn2_w_up=m_ffn2_w_up[0], ffn2_w_down=m_ffn2_w_down[0])
    var = dict(ffn1_norm=v_ffn1_norm, ffn1_w_gate=v_ffn1_w_gate[0], ffn1_w_up=v_ffn1_w_up[0], ffn1_w_down=v_ffn1_w_down[0],
               mix_norm=v_mix_norm, w_in=v_w_in[0], q_norm=v_q_norm, k_norm=v_k_norm, conv_w=v_conv_w[0], conv_b=v_conv_b,
               conv_ln_g=v_conv_ln_g, conv_ln_b=v_conv_ln_b, w_out=v_w_out[0], ffn2_norm=v_ffn2_norm,
               ffn2_w_gate=v_ffn2_w_gate[0], ffn2_w_up=v_ffn2_w_up[0], ffn2_w_down=v_ffn2_w_down[0])
    chip = 2 * lax.axis_index("x") + lax.axis_index("y")
    for src in (wts, mom, var):
        for n in TRANSPOSED:
            src[n] = src[n].T

    taps = jnp.pad(wts["conv_w"], ((0, 1), (0, 0)))
    staged = _stage_shards([wts["ffn1_w_gate"], wts["ffn1_w_up"], wts["ffn1_w_down"], wts["w_in"], wts["w_out"], taps,
                            wts["ffn2_w_gate"], wts["ffn2_w_up"], wts["ffn2_w_down"]],
                           [BF16, BF16, BF16, BF16, BF16, F32, BF16, BF16, BF16], name="stage_shards")
    first = _run_phase(_gather_ici_phase(staged[:3]).then(_gather_d2d_phase(staged[:3])), name="gather_ffn1")
    wg1, wu1, wd1 = [_whole(b) for b in first]
    w = dict(ffn1_norm=ffn1_norm, mix_norm=mix_norm, ffn2_norm=ffn2_norm, q_norm=q_norm, k_norm=k_norm,
             conv_b=conv_b, conv_ln_g=conv_ln_g, conv_ln_b=conv_ln_b, wg1=wg1, wu1=wu1, wd1=wd1)
    overlap = _Overlap(staged[3:])
    loss_part, grad_x, g = _local_step(x, loss_target, w, overlap)

    late = _reduce_scatter([g["wg1"], g["wu1"], g["wd1"], g["win"]])
    big_grads = dict(zip(("ffn1_w_gate", "ffn1_w_up", "ffn1_w_down", "w_in"), late))
    big_grads.update(ffn2_w_gate=overlap.reduced["wg2"], ffn2_w_up=overlap.reduced["wu2"],
                     ffn2_w_down=overlap.reduced["wd2"], w_out=overlap.reduced["wout"])

    small_shapes = [g[n].shape for n in SMALL] + [(SUBLANES, LANES)]
    red = _allreduce_small(_pack([g[n] for n in SMALL] + [loss_part]), name="allreduce_small")
    small = dict(zip(SMALL + ("loss",), _unpack(red, small_shapes)))
    loss = small["loss"][0, 0]
    small["conv_w"] = lax.dynamic_slice_in_dim(small["conv_w"], chip * LANES, LANES, axis=1)

    grads, delta, new_m, new_v = {}, {}, {}, {}
    for n in BIG:
        grads[n] = big_grads[n]
        delta[n], new_m[n], new_v[n] = _adamw(wts[n], grads[n], mom[n], var[n], name=f"adamw_{n}")
    shapes = [wts[n].shape for n in SMALL]
    packs = [_pack([src[n] for n in SMALL]) for src in (wts, small, mom, var)]
    outs = _adamw(*packs, name="adamw_small")
    for dst, pk in zip((delta, new_m, new_v), outs):
        dst.update(zip(SMALL, _unpack(pk, shapes)))
    for n in SMALL:
        grads[n] = small[n]

    def shaped(d, n):
        v = d[n].T if n in TRANSPOSED else d[n]
        return v.reshape((1,) + v.shape) if n in BIG or n == "conv_w" else v

    return (loss, grad_x, *[shaped(grads, n) for n in WEIGHTS], *[shaped(delta, n) for n in WEIGHTS],
            *[shaped(new_m, n) for n in WEIGHTS], *[shaped(new_v, n) for n in WEIGHTS])
```

```python
import functools

import jax
import jax.numpy as jnp
from jax import lax
from jax.experimental import pallas as pl
from jax.experimental.pallas import tpu as pltpu

F32 = jnp.float32
BF16 = jnp.bfloat16

EPS = 1e-6
HEADS = 8
HEAD_DIM = 64
D_ATTN = HEADS * HEAD_DIM
D_CONV = 512
CONV_K = 31
QBLK = 128
N_PATTERNS = 3
DILATIONS = (1, 4, 16)
LANES = 128
NEG = -1e30

ADAM_LR = 0.001
ADAM_B1 = 0.9
ADAM_B2 = 0.999
ADAM_EPS = 1e-08
ADAM_WD = 0.01
ADAM_STEP = 10

VMEM_LIMIT = 56 * 1024 * 1024
MESH = pl.DeviceIdType.MESH

NT_DIMS = (((1,), (1,)), ((), ()))
TN_DIMS = (((0,), (0,)), ((), ()))


def _params(*sem):
    return pltpu.CompilerParams(dimension_semantics=sem, vmem_limit_bytes=VMEM_LIMIT)


def _dot(a, b):
    return jnp.dot(a, b, preferred_element_type=F32)


def _dot_nt(a, b):
    return lax.dot_general(a, b, NT_DIMS, preferred_element_type=F32)


def _dot_tn(a, b):
    return lax.dot_general(a, b, TN_DIMS, preferred_element_type=F32)


def _sigmoid(x):
    return 1.0 / (1.0 + jnp.exp(-x))


def _seg_mean(v, e_ref, width):
    hi = v.astype(BF16)
    lo = (v - hi.astype(F32)).astype(BF16)
    e = e_ref[...]
    return (_dot(hi, e) + _dot(lo, e)) * (1.0 / width)


def _seg_matrix(n):
    i = jnp.arange(n)
    return (i[:, None] // HEAD_DIM == i[None, :] // HEAD_DIM).astype(BF16)


ANY = pl.BlockSpec(memory_space=pl.ANY)
DMA_SEMS = pltpu.SemaphoreType.DMA


class _Phase:
    def __init__(self, ins, outs, aliases, nsem, copies):
        self.ins, self.outs, self.aliases = list(ins), list(outs), dict(aliases)
        self.stages = [(nsem, copies)]

    def then(self, other):
        self.stages = self.stages + other.stages
        return self

    @property
    def nsem(self):
        return sum(n for n, _ in self.stages)

    def _copies(self, k, in_refs, out_refs, send_sems, recv_sems):
        base = sum(n for n, _ in self.stages[:k])
        return self.stages[k][1](in_refs, out_refs, lambda i: (send_sems.at[base + i], recv_sems.at[base + i]))

    def start(self, k, *refs):
        for cp in self._copies(k, *refs)[0]:
            cp.start()

    def finish(self, k, *refs):
        starts, arrivals = self._copies(k, *refs)
        for cp in arrivals:
            cp.wait_recv()
        for cp in starts:
            cp.wait_send()


def _run_phase(phase, *, name):
    n_in, n_out = len(phase.ins), len(phase.outs)

    def body(*refs):
        ins, outs = refs[:n_in], refs[n_in:n_in + n_out]
        send_sems, recv_sems = refs[n_in + n_out:]
        for k in range(len(phase.stages)):
            phase.start(k, ins, outs, send_sems, recv_sems)
            phase.finish(k, ins, outs, send_sems, recv_sems)

    return pl.pallas_call(
        body, in_specs=[ANY] * n_in, out_specs=[ANY] * n_out, out_shape=phase.outs,
        input_output_aliases=phase.aliases,
        scratch_shapes=[DMA_SEMS((phase.nsem,)), DMA_SEMS((phase.nsem,))], name=name)(*phase.ins)


def _call(body, *, grid, in_specs, out_specs, out_shape, scratch_shapes=(), sem, name, args, phase=None):
    in_specs, out_specs, out_shape = list(in_specs), list(out_specs), list(out_shape)
    scratch_shapes = list(scratch_shapes)
    if phase is None:
        return pl.pallas_call(body, grid=grid, in_specs=in_specs, out_specs=out_specs, out_shape=out_shape,
                              scratch_shapes=scratch_shapes, compiler_params=_params(*sem), name=name)(*args)
    n_in, n_out, n_scr = len(in_specs), len(out_specs), len(scratch_shapes)
    p_in, p_out = len(phase.ins), len(phase.outs)

    def hosted(*refs):
        ins, pins = refs[:n_in], refs[n_in:n_in + p_in]
        o0 = n_in + p_in
        outs, pouts = refs[o0:o0 + n_out], refs[o0 + n_out:o0 + n_out + p_out]
        s0 = o0 + n_out + p_out
        scr = refs[s0:s0 + n_scr]
        send_sems, recv_sems = refs[s0 + n_scr:]
        step = 0
        for d, n in enumerate(grid):
            step = step * n + pl.program_id(d)
        nsteps = functools.reduce(lambda a, b: a * b, grid)
        nstages = len(phase.stages)
        comm_refs = (pins, pouts, send_sems, recv_sems)

        for k in range(nstages):
            @pl.when(step == (k * nsteps) // nstages)
            def _(k=k):
                if k > 0:
                    phase.finish(k - 1, *comm_refs)
                phase.start(k, *comm_refs)

        body(*ins, *outs, *scr)

        @pl.when(step == nsteps - 1)
        def _():
            phase.finish(nstages - 1, *comm_refs)

    res = pl.pallas_call(
        hosted, grid=grid, in_specs=in_specs + [ANY] * p_in, out_specs=out_specs + [ANY] * p_out,
        out_shape=out_shape + phase.outs,
        input_output_aliases={n_in + i: n_out + o for i, o in phase.aliases.items()},
        scratch_shapes=scratch_shapes + [DMA_SEMS((phase.nsem,)), DMA_SEMS((phase.nsem,))],
        compiler_params=_params(*sem), name=name)(*args, *phase.ins)
    return res[:n_out], res[n_out:]


ROW_CHUNK = 256


def _ffn_fwd(x, gain, wg, wu, wd, tgt, *, tm, name, phase=None):
    T, D = x.shape
    NS, Fs, _ = wg.shape
    with_loss = tgt is not None

    def body(*refs):
        if with_loss:
            x_ref, g_ref, wg_ref, wu_ref, wd_ref, t_ref, h_ref, n_ref, G_ref, U_ref, loss_ref, acc_ref = refs
        else:
            x_ref, g_ref, wg_ref, wu_ref, wd_ref, h_ref, n_ref, G_ref, U_ref, acc_ref = refs
        i = pl.program_id(0)
        j = pl.program_id(1)

        @pl.when(j == 0)
        def _():
            xv = x_ref[...]
            r = lax.rsqrt(jnp.mean(xv * xv, axis=-1, keepdims=True) + EPS)
            n_ref[...] = (xv * r * g_ref[...]).astype(BF16)
            acc_ref[...] = jnp.zeros_like(acc_ref)

        n = n_ref[...]
        G = _dot_nt(n, wg_ref[...])
        U = _dot_nt(n, wu_ref[...])
        G_ref[...] = G.astype(BF16)
        U_ref[...] = U.astype(BF16)
        A = (G * _sigmoid(G) * U).astype(BF16)
        acc_ref[...] += _dot(A, wd_ref[...])

        @pl.when(j == NS - 1)
        def _():
            h = x_ref[...] + 0.5 * acc_ref[...]
            if with_loss:
                e = h - t_ref[...]
                h_ref[...] = e * (1.0 / D)

                @pl.when(i == 0)
                def _():
                    loss_ref[...] = jnp.zeros_like(loss_ref)

                loss_ref[...] += jnp.sum(e * e) * (0.5 / D)
            else:
                h_ref[...] = h

    tok = pl.BlockSpec((tm, D), lambda i, j: (i, 0))
    in_specs = [tok, pl.BlockSpec((1, D), lambda i, j: (0, 0)),
                pl.BlockSpec((None, Fs, D), lambda i, j: (j, 0, 0)),
                pl.BlockSpec((None, Fs, D), lambda i, j: (j, 0, 0)),
                pl.BlockSpec((None, Fs, D), lambda i, j: (j, 0, 0))]
    args = [x, gain, wg, wu, wd]
    act = pl.BlockSpec((None, tm, Fs), lambda i, j: (j, i, 0))
    out_shape = [jax.ShapeDtypeStruct((T, D), F32), jax.ShapeDtypeStruct((T, D), BF16),
                 jax.ShapeDtypeStruct((NS, T, Fs), BF16), jax.ShapeDtypeStruct((NS, T, Fs), BF16)]
    out_specs = [tok, tok, act, act]
    if with_loss:
        in_specs.append(tok)
        args.append(tgt)
        out_shape.append(jax.ShapeDtypeStruct((8, LANES), F32))
        out_specs.append(pl.BlockSpec((8, LANES), lambda i, j: (0, 0)))
    return _call(body, grid=(T // tm, NS), in_specs=in_specs, out_specs=out_specs, out_shape=out_shape,
                 scratch_shapes=[pltpu.VMEM((tm, D), F32)], sem=("arbitrary", "arbitrary"), name=name,
                 args=args, phase=phase)


def _rms_bwd(xv, gain, dn):
    r = lax.rsqrt(jnp.mean(xv * xv, axis=-1, keepdims=True) + EPS)
    xhat = xv * r
    dxh = dn * gain
    dx = r * (dxh - xhat * jnp.mean(dxh * xhat, axis=-1, keepdims=True))
    dg = jnp.sum(dn * xhat, axis=0, keepdims=True)
    return dx, dg


def _ffn_bwd_act(dh, x, gain, G, U, wg, wu, wd, *, tm, name):
    T, D = x.shape
    NS, Fs, _ = wg.shape

    def body(dh_ref, x_ref, g_ref, G_ref, U_ref, wg_ref, wu_ref, wd_ref,
             dG_ref, dU_ref, A_ref, dy_ref, dx_ref, dg_ref, acc_ref):
        i = pl.program_id(0)
        j = pl.program_id(1)

        @pl.when(j == 0)
        def _():
            dy_ref[...] = (0.5 * dh_ref[...]).astype(BF16)
            acc_ref[...] = jnp.zeros_like(acc_ref)

        @pl.when((i == 0) & (j == 0))
        def _():
            dg_ref[...] = jnp.zeros_like(dg_ref)

        for r0 in range(0, tm, ROW_CHUNK):
            rows = slice(r0, r0 + ROW_CHUNK)
            Gv = G_ref[rows, :].astype(F32)
            Uv = U_ref[rows, :].astype(F32)
            sig = _sigmoid(Gv)
            s = Gv * sig
            dA = _dot_nt(dy_ref[rows, :], wd_ref[...])
            dG = (dA * Uv * (sig * (1.0 + Gv * (1.0 - sig)))).astype(BF16)
            dU = (dA * s).astype(BF16)
            dG_ref[rows, :] = dG
            dU_ref[rows, :] = dU
            A_ref[rows, :] = (s * Uv).astype(BF16)
            acc_ref[rows, :] += _dot(dG, wg_ref[...]) + _dot(dU, wu_ref[...])

        @pl.when(j == NS - 1)
        def _():
            dx, dg = _rms_bwd(x_ref[...], g_ref[...], acc_ref[...])
            dx_ref[...] = dh_ref[...] + dx
            dg_ref[...] += dg

    tok = pl.BlockSpec((tm, D), lambda i, j: (i, 0))
    act = pl.BlockSpec((None, tm, Fs), lambda i, j: (j, i, 0))
    vec = pl.BlockSpec((1, D), lambda i, j: (0, 0))
    return pl.pallas_call(
        body, grid=(T // tm, NS),
        in_specs=[tok, tok, vec, act, act,
                  pl.BlockSpec((None, Fs, D), lambda i, j: (j, 0, 0)),
                  pl.BlockSpec((None, Fs, D), lambda i, j: (j, 0, 0)),
                  pl.BlockSpec((None, Fs, D), lambda i, j: (j, 0, 0))],
        out_specs=[act, act, act, tok, tok, vec],
        out_shape=[jax.ShapeDtypeStruct((NS, T, Fs), BF16)] * 3
        + [jax.ShapeDtypeStruct((T, D), BF16), jax.ShapeDtypeStruct((T, D), F32),
           jax.ShapeDtypeStruct((1, D), F32)],
        scratch_shapes=[pltpu.VMEM((tm, D), F32)],
        compiler_params=_params("arbitrary", "arbitrary"), name=name)(dh, x, gain, G, U, wg, wu, wd)


def _ffn_bwd_w(n, dy, dG, dU, A, *, tk, name):
    T, D = n.shape
    NS, _, Fs = dG.shape

    def body(n_ref, dy_ref, dG_ref, dU_ref, A_ref, wg_ref, wu_ref, wd_ref):
        @pl.when(pl.program_id(1) == 0)
        def _():
            wg_ref[...] = jnp.zeros_like(wg_ref)
            wu_ref[...] = jnp.zeros_like(wu_ref)
            wd_ref[...] = jnp.zeros_like(wd_ref)

        nv = n_ref[...]
        wg_ref[...] += _dot_tn(dG_ref[...], nv)
        wu_ref[...] += _dot_tn(dU_ref[...], nv)
        wd_ref[...] += _dot_tn(A_ref[...], dy_ref[...])

    tok = pl.BlockSpec((tk, D), lambda j, k: (k, 0))
    act = pl.BlockSpec((None, tk, Fs), lambda j, k: (j, k, 0))
    return pl.pallas_call(
        body, grid=(NS, T // tk), in_specs=[tok, tok, act, act, act],
        out_specs=[pl.BlockSpec((None, Fs, D), lambda j, k: (j, 0, 0))] * 3,
        out_shape=[jax.ShapeDtypeStruct((NS, Fs, D), F32)] * 3,
        compiler_params=_params("arbitrary", "arbitrary"), name=name)(n, dy, dG, dU, A)


def _inproj_fwd(h, gain, win, *, tm, name, phase=None):
    T, D = h.shape
    NS, _, Cs = win.shape

    def body(h_ref, g_ref, w_ref, u_ref, n_ref):
        @pl.when(pl.program_id(1) == 0)
        def _():
            xv = h_ref[...]
            r = lax.rsqrt(jnp.mean(xv * xv, axis=-1, keepdims=True) + EPS)
            n_ref[...] = (xv * r * g_ref[...]).astype(BF16)

        u_ref[...] = _dot(n_ref[...], w_ref[...])

    tok = pl.BlockSpec((tm, D), lambda i, j: (i, 0))
    return _call(
        body, grid=(T // tm, NS),
        in_specs=[tok, pl.BlockSpec((1, D), lambda i, j: (0, 0)),
                  pl.BlockSpec((None, D, Cs), lambda i, j: (j, 0, 0))],
        out_specs=[pl.BlockSpec((tm, Cs), lambda i, j: (i, j)), tok],
        out_shape=[jax.ShapeDtypeStruct((T, NS * Cs), F32), jax.ShapeDtypeStruct((T, D), BF16)],
        sem=("arbitrary", "arbitrary"), name=name, args=(h, gain, win), phase=phase)


def _inproj_bwd_act(du, dh, h, gain, win, *, tm, name, phase=None):
    T, D = h.shape
    NS, _, Cs = win.shape

    def body(du_ref, dh_ref, h_ref, g_ref, w_ref, dx_ref, dg_ref, acc_ref):
        i = pl.program_id(0)
        j = pl.program_id(1)

        @pl.when(j == 0)
        def _():
            acc_ref[...] = jnp.zeros_like(acc_ref)

        @pl.when((i == 0) & (j == 0))
        def _():
            dg_ref[...] = jnp.zeros_like(dg_ref)

        acc_ref[...] += _dot_nt(du_ref[...], w_ref[...])

        @pl.when(j == NS - 1)
        def _():
            dx, dg = _rms_bwd(h_ref[...], g_ref[...], acc_ref[...])
            dx_ref[...] = dh_ref[...] + dx
            dg_ref[...] += dg

    tok = pl.BlockSpec((tm, D), lambda i, j: (i, 0))
    vec = pl.BlockSpec((1, D), lambda i, j: (0, 0))
    return _call(
        body, grid=(T // tm, NS),
        in_specs=[pl.BlockSpec((tm, Cs), lambda i, j: (i, j)), tok, tok, vec,
                  pl.BlockSpec((None, D, Cs), lambda i, j: (j, 0, 0))],
        out_specs=[tok, vec],
        out_shape=[jax.ShapeDtypeStruct((T, D), F32), jax.ShapeDtypeStruct((1, D), F32)],
        scratch_shapes=[pltpu.VMEM((tm, D), F32)],
        sem=("arbitrary", "arbitrary"), name=name, args=(du, dh, h, gain, win), phase=phase)


def _inproj_bwd_w(n, du, ns, *, tk, name):
    T, D = n.shape
    Cs = du.shape[1] // ns

    def body(n_ref, du_ref, w_ref):
        @pl.when(pl.program_id(1) == 0)
        def _():
            w_ref[...] = jnp.zeros_like(w_ref)

        w_ref[...] += _dot_tn(n_ref[...], du_ref[...])

    return pl.pallas_call(
        body, grid=(ns, T // tk),
        in_specs=[pl.BlockSpec((tk, D), lambda j, k: (k, 0)), pl.BlockSpec((tk, Cs), lambda j, k: (k, j))],
        out_specs=pl.BlockSpec((None, D, Cs), lambda j, k: (j, 0, 0)),
        out_shape=jax.ShapeDtypeStruct((ns, D, Cs), F32),
        compiler_params=_params("arbitrary", "arbitrary"), name=name)(n, du)


def _permute_out(src_ref, out_ref, cast):
    S = src_ref.shape[1]
    for p, d in enumerate(DILATIONS):
        L = S // d
        for cc in range(4):
            cols = slice(cc * LANES, (cc + 1) * LANES)
            if d == 1:
                out_ref[p, :, cols] = src_ref[cc].astype(cast)
            else:
                for r in range(d):
                    out_ref[p, r * L:(r + 1) * L, cols] = src_ref[cc, pl.ds(r, L, stride=d), :].astype(cast)


def _unpermute_in(get_block, dst_ref, p, S):
    d = DILATIONS[p]
    L = S // d
    if d == 1:
        dst_ref[...] = get_block(0, S)
    else:
        for r in range(d):
            dst_ref[pl.ds(r, L, stride=d), :] = get_block(r * L, L)


def _qkv_prep(u, gains, B, S, *, name):
    emat = _seg_matrix(D_ATTN)

    def body(u_ref, g_ref, e_ref, out_ref, scr_ref):
        c = pl.program_id(1)
        xv = u_ref[...]
        ms = _seg_mean(xv * xv, e_ref, HEAD_DIM)
        r = jnp.where(c < 2, lax.rsqrt(ms + EPS), 1.0)
        yv = xv * r * g_ref[...]
        for cc in range(4):
            scr_ref[cc] = yv[:, cc * LANES:(cc + 1) * LANES]
        _permute_out(scr_ref, out_ref, BF16)

    return pl.pallas_call(
        body, grid=(B, 3),
        in_specs=[pl.BlockSpec((S, D_ATTN), lambda b, c: (b, c)),
                  pl.BlockSpec((None, 1, D_ATTN), lambda b, c: (c, 0, 0)),
                  pl.BlockSpec((D_ATTN, D_ATTN), lambda b, c: (0, 0))],
        out_specs=pl.BlockSpec((None, N_PATTERNS, None, S, D_ATTN), lambda b, c: (c, 0, b, 0, 0)),
        out_shape=jax.ShapeDtypeStruct((3, N_PATTERNS, B, S, D_ATTN), BF16),
        scratch_shapes=[pltpu.VMEM((4, S, LANES), F32)],
        compiler_params=_params("arbitrary", "arbitrary"), name=name)(u, gains, emat)


def _band_mask(p, b):
    nblk = jnp.right_shift(16, 2 * p)
    has_prev = jnp.bitwise_and(b, nblk - 1) != 0
    qi = lax.broadcasted_iota(jnp.int32, (QBLK, 2 * QBLK), 0)
    ci = lax.broadcasted_iota(jnp.int32, (QBLK, 2 * QBLK), 1)
    dist = QBLK + qi - ci
    return (dist >= 0) & (dist <= QBLK) & (has_prev | (ci >= QBLK))


def _first_head(rows):
    return lax.broadcasted_iota(jnp.int32, (rows, LANES), 1) < HEAD_DIM


def _split_heads(pair):
    first = _first_head(pair.shape[0])
    zero = jnp.zeros_like(pair)
    return jnp.concatenate([jnp.where(first, pair, zero), jnp.where(first, zero, pair)], axis=0)


def _merge_heads(col_a, col_b):
    rows = col_a.shape[0]
    return jnp.where(_first_head(rows), jnp.broadcast_to(col_a, (rows, LANES)), jnp.broadcast_to(col_b, (rows, LANES)))


def _attn_specs(nb):
    blk = (None, None, None, QBLK, D_ATTN)
    q_spec = pl.BlockSpec(blk, lambda p, b: (0, p, b, 0, 0))
    kp_spec = pl.BlockSpec(blk, lambda p, b: (1, p, jnp.maximum(b - 1, 0), 0, 0))
    kc_spec = pl.BlockSpec(blk, lambda p, b: (1, p, b, 0, 0))
    vp_spec = pl.BlockSpec(blk, lambda p, b: (2, p, jnp.maximum(b - 1, 0), 0, 0))
    vc_spec = pl.BlockSpec(blk, lambda p, b: (2, p, b, 0, 0))
    return [q_spec, kp_spec, kc_spec, vp_spec, vc_spec]


def _attn_fwd(qkv, *, name):
    nb = qkv.shape[2]

    def body(q_ref, kp_ref, kc_ref, vp_ref, vc_ref, o_ref, lse_ref):
        mask = _band_mask(pl.program_id(0), pl.program_id(1))
        mask2 = jnp.concatenate([mask, mask], axis=0)
        kk = jnp.concatenate([kp_ref[...], kc_ref[...]], axis=0)
        vv = jnp.concatenate([vp_ref[...], vc_ref[...]], axis=0)
        for hp in range(HEADS // 2):
            cols = slice(hp * LANES, (hp + 1) * LANES)
            s = _dot_nt(_split_heads(q_ref[:, cols]), kk[:, cols])
            s = jnp.where(mask2, s, NEG)
            m = jnp.max(s, axis=-1, keepdims=True)
            e = jnp.exp(s - m)
            l = jnp.sum(e, axis=-1, keepdims=True)
            pr = (e * (1.0 / l)).astype(BF16)
            o_ref[:, cols] = _dot(jnp.concatenate([pr[:QBLK], pr[QBLK:]], axis=1), _split_heads(vv[:, cols]))
            lse = m + jnp.log(l)
            lse_ref[:, cols] = _merge_heads(lse[:QBLK], lse[QBLK:])

    out = pl.BlockSpec((None, None, QBLK, D_ATTN), lambda p, b: (p, b, 0, 0))
    return pl.pallas_call(
        body, grid=(N_PATTERNS, nb), in_specs=_attn_specs(nb), out_specs=[out, out],
        out_shape=[jax.ShapeDtypeStruct((N_PATTERNS, nb, QBLK, D_ATTN), F32)] * 2,
        compiler_params=_params("arbitrary", "arbitrary"), name=name)(qkv, qkv, qkv, qkv, qkv)


def _attn_combine(o3, lse3, B, S, *, name):
    def body(o_ref, l_ref, a_ref, lt_ref, so_ref, sl_ref):
        for p in range(N_PATTERNS):
            _unpermute_in(lambda r0, n, p=p: o_ref[p, pl.ds(r0, n), :], so_ref.at[p], p, S)
            _unpermute_in(lambda r0, n, p=p: l_ref[p, pl.ds(r0, n), :], sl_ref.at[p], p, S)
        l0, l1, l2 = sl_ref[0], sl_ref[1], sl_ref[2]
        m = jnp.maximum(jnp.maximum(l0, l1), l2)
        w0, w1, w2 = jnp.exp(l0 - m), jnp.exp(l1 - m), jnp.exp(l2 - m)
        tot = w0 + w1 + w2
        a_ref[...] = (w0 * so_ref[0] + w1 * so_ref[1] + w2 * so_ref[2]) / tot
        lt_ref[...] = m + jnp.log(tot)

    o3 = o3.reshape(N_PATTERNS, B, S, D_ATTN)
    lse3 = lse3.reshape(N_PATTERNS, B, S, D_ATTN)
    inp = pl.BlockSpec((N_PATTERNS, None, S, LANES), lambda b, c: (0, b, 0, c))
    out = pl.BlockSpec((S, LANES), lambda b, c: (b, c))
    return pl.pallas_call(
        body, grid=(B, D_ATTN // LANES), in_specs=[inp, inp], out_specs=[out, out],
        out_shape=[jax.ShapeDtypeStruct((B * S, D_ATTN), F32)] * 2,
        scratch_shapes=[pltpu.VMEM((N_PATTERNS, S, LANES), F32)] * 2,
        compiler_params=_params("arbitrary", "arbitrary"), name=name)(o3, lse3)


STAT_D = 8


def _attn_bwd_prep(dattn, attn, lse, B, S, *, name):
    emat = _seg_matrix(LANES)
    ncc = D_ATTN // LANES

    def body(da_ref, a_ref, l_ref, e_ref, do_ref, st_ref, scr_ref, nat_ref):
        cc = pl.program_id(1)
        da = da_ref[...]
        dsum = _seg_mean(da * a_ref[...], e_ref, 1.0)
        scr_ref[...] = da
        for p, d in enumerate(DILATIONS):
            L = S // d
            if d == 1:
                do_ref[p] = da.astype(BF16)
            else:
                for r in range(d):
                    do_ref[p, r * L:(r + 1) * L, :] = scr_ref[pl.ds(r, L, stride=d), :].astype(BF16)

        lane = lax.broadcasted_iota(jnp.int32, (S, LANES), 1)
        h0 = 2 * cc
        vals = ((h0, l_ref[:, 0:1]), (h0 + 1, l_ref[:, HEAD_DIM:HEAD_DIM + 1]),
                (STAT_D + h0, dsum[:, 0:1]), (STAT_D + h0 + 1, dsum[:, HEAD_DIM:HEAD_DIM + 1]))
        tile = jnp.where(cc == 0, 0.0, nat_ref[...])
        for at, col in vals:
            tile = jnp.where(lane == at, col, tile)
        nat_ref[...] = tile

        @pl.when(cc == ncc - 1)
        def _():
            for p, d in enumerate(DILATIONS):
                L = S // d
                if d == 1:
                    st_ref[p] = nat_ref[...]
                else:
                    for r in range(d):
                        st_ref[p, r * L:(r + 1) * L, :] = nat_ref[pl.ds(r, L, stride=d), :]

    inp = pl.BlockSpec((S, LANES), lambda b, c: (b, c))
    return pl.pallas_call(
        body, grid=(B, ncc),
        in_specs=[inp, inp, inp, pl.BlockSpec((LANES, LANES), lambda b, c: (0, 0))],
        out_specs=[pl.BlockSpec((N_PATTERNS, None, S, LANES), lambda b, c: (0, b, 0, c)),
                   pl.BlockSpec((N_PATTERNS, None, S, LANES), lambda b, c: (0, b, 0, 0))],
        out_shape=[jax.ShapeDtypeStruct((N_PATTERNS, B, S, D_ATTN), BF16),
                   jax.ShapeDtypeStruct((N_PATTERNS, B, S, LANES), F32)],
        scratch_shapes=[pltpu.VMEM((S, LANES), F32), pltpu.VMEM((S, LANES), F32)],
        compiler_params=_params("arbitrary", "arbitrary"), name=name)(dattn, attn, lse, emat)


def _attn_bwd(qkv, do3, st3, *, name, phase=None):
    nb = qkv.shape[2]

    def body(q_ref, kp_ref, kc_ref, vp_ref, vc_ref, do_ref, st_ref, out_ref, carry_ref):
        p = pl.program_id(0)
        b = pl.program_id(1)

        @pl.when((p == 0) & (b == 0))
        def _():
            carry_ref[...] = jnp.zeros_like(carry_ref)

        mask = _band_mask(p, b) & (b < nb)
        mask2 = jnp.concatenate([mask, mask], axis=0)
        kk = jnp.concatenate([kp_ref[...], kc_ref[...]], axis=0)
        vv = jnp.concatenate([vp_ref[...], vc_ref[...]], axis=0)
        for hp in range(HEADS // 2):
            cols = slice(hp * LANES, (hp + 1) * LANES)
            h0, h1 = 2 * hp, 2 * hp + 1
            kh, vh = kk[:, cols], vv[:, cols]
            q2 = _split_heads(q_ref[:, cols])
            do2 = _split_heads(do_ref[:, cols])
            lse = jnp.concatenate([st_ref[:, h0:h0 + 1], st_ref[:, h1:h1 + 1]], axis=0)
            dsum = jnp.concatenate([st_ref[:, STAT_D + h0:STAT_D + h0 + 1], st_ref[:, STAT_D + h1:STAT_D + h1 + 1]], axis=0)
            s = _dot_nt(q2, kh)
            pr = jnp.where(mask2, jnp.exp(s - lse), 0.0)
            dp = _dot_nt(do2, vh)
            ds = (pr * (dp - dsum)).astype(BF16)
            prb = pr.astype(BF16)
            dq = _dot(jnp.concatenate([ds[:QBLK], ds[QBLK:]], axis=1), _split_heads(kh))
            dk = _dot_tn(ds, q2)
            dv = _dot_tn(prb, do2)
            out_ref[0, :, cols] = carry_ref[0, :, cols].astype(BF16)
            out_ref[1, :, cols] = (carry_ref[1, :, cols] + dk[:QBLK]).astype(BF16)
            out_ref[2, :, cols] = (carry_ref[2, :, cols] + dv[:QBLK]).astype(BF16)
            carry_ref[0, :, cols] = dq
            carry_ref[1, :, cols] = dk[QBLK:]
            carry_ref[2, :, cols] = dv[QBLK:]

    def at(which, back):
        return lambda p, b: (which, p, jnp.maximum(jnp.minimum(b, nb - 1) - back, 0), 0, 0)

    blk = (None, None, None, QBLK, D_ATTN)
    aux = lambda lanes: pl.BlockSpec((None, None, QBLK, lanes), lambda p, b: (p, jnp.minimum(b, nb - 1), 0, 0))
    return _call(
        body, grid=(N_PATTERNS, nb + 1),
        in_specs=[pl.BlockSpec(blk, at(0, 0)), pl.BlockSpec(blk, at(1, 1)), pl.BlockSpec(blk, at(1, 0)),
                  pl.BlockSpec(blk, at(2, 1)), pl.BlockSpec(blk, at(2, 0)), aux(D_ATTN), aux(LANES)],
        out_specs=[pl.BlockSpec((3, None, None, QBLK, D_ATTN), lambda p, b: (0, p, jnp.maximum(b - 1, 0), 0, 0))],
        out_shape=[jax.ShapeDtypeStruct((3, N_PATTERNS, nb, QBLK, D_ATTN), BF16)],
        scratch_shapes=[pltpu.VMEM((3, QBLK, D_ATTN), F32)],
        sem=("arbitrary", "arbitrary"), name=name, args=(qkv, qkv, qkv, qkv, qkv, do3, st3), phase=phase)


def _attn_grad_combine(cur, u, gains, B, S, *, name, phase=None):
    emat = _seg_matrix(LANES)

    def body(cur_ref, u_ref, g_ref, e_ref, du_ref, dg_ref, scr_ref):
        c = pl.program_id(0)
        b = pl.program_id(2)
        for p in range(N_PATTERNS):
            _unpermute_in(lambda r0, n, p=p: cur_ref[p, pl.ds(r0, n), :].astype(F32), scr_ref.at[p], p, S)
        dy = scr_ref[0] + scr_ref[1] + scr_ref[2]
        xv = u_ref[...]
        gain = g_ref[...]
        ms = _seg_mean(xv * xv, e_ref, HEAD_DIM)
        r = lax.rsqrt(ms + EPS)
        xhat = xv * r
        dxh = dy * gain
        dx = r * (dxh - xhat * _seg_mean(dxh * xhat, e_ref, HEAD_DIM))
        du_ref[...] = jnp.where(c < 2, dx, dy).astype(BF16)

        @pl.when((b == 0))
        def _():
            dg_ref[...] = jnp.zeros_like(dg_ref)

        dg_ref[...] += jnp.sum(dy * xhat, axis=0, keepdims=True)

    cur = cur.reshape(3, N_PATTERNS, B, S, D_ATTN)
    ncc = D_ATTN // LANES
    return _call(
        body, grid=(3, ncc, B),
        in_specs=[pl.BlockSpec((None, N_PATTERNS, None, S, LANES), lambda c, cc, b: (c, 0, b, 0, cc)),
                  pl.BlockSpec((S, LANES), lambda c, cc, b: (b, c * ncc + cc)),
                  pl.BlockSpec((None, 1, LANES), lambda c, cc, b: (c, 0, cc)),
                  pl.BlockSpec((LANES, LANES), lambda c, cc, b: (0, 0))],
        out_specs=[pl.BlockSpec((S, LANES), lambda c, cc, b: (b, c * ncc + cc)),
                   pl.BlockSpec((None, 1, LANES), lambda c, cc, b: (c, 0, cc))],
        out_shape=[jax.ShapeDtypeStruct((B * S, 3 * D_ATTN), BF16), jax.ShapeDtypeStruct((3, 1, D_ATTN), F32)],
        scratch_shapes=[pltpu.VMEM((N_PATTERNS, S, LANES), F32)],
        sem=("arbitrary", "arbitrary", "arbitrary"), name=name, args=(cur, u, gains, emat), phase=phase)


HALO = 32
SUB = 64
SUBLANES = 8


def _shifted_copies(src_ref, sh_ref, tc):
    sh_ref[0] = src_ref[...]
    for r in range(1, SUBLANES):
        sh_ref[r, 0:tc + HALO - SUBLANES, :] = src_ref[pl.ds(r, tc + HALO - SUBLANES), :]


def _shifted(sh_ref, start, size):
    return sh_ref[start % SUBLANES, pl.ds(start - start % SUBLANES, size), :]


def _conv_fwd(u, cw, cb, lg, lb, B, S, *, tc, name):
    nchunk = S // tc
    hb = tc // HALO

    def body(ca_ref, cap_ref, cg_ref, cgp_ref, w_ref, cb_ref, lg_ref, lb_ref, cv_ref, glu_ref, y_ref, pad_ref, sh_ref):
        i = pl.program_id(1)
        glu = ca_ref[...] * _sigmoid(cg_ref[...])
        glu_ref[...] = glu
        prev = cap_ref[...] * _sigmoid(cgp_ref[...])
        pad_ref[0:HALO, :] = jnp.where(i > 0, prev, 0.0)
        pad_ref[HALO:, :] = glu
        _shifted_copies(pad_ref, sh_ref, tc)
        for sub in range(tc // SUB):
            acc = jnp.zeros((SUB, D_CONV), F32) + cb_ref[...]
            for k in range(CONV_K):
                acc = acc + _shifted(sh_ref, sub * SUB + HALO - (CONV_K - 1) + k, SUB) * w_ref[pl.ds(k, 1), :]
            y_ref[sub * SUB:(sub + 1) * SUB, :] = acc
        y = y_ref[...]
        mu = jnp.mean(y, axis=-1, keepdims=True)
        yc = y - mu
        var = jnp.mean(yc * yc, axis=-1, keepdims=True)
        z = yc * lax.rsqrt(var + EPS) * lg_ref[...] + lb_ref[...]
        cv_ref[...] = (z * _sigmoid(z)).astype(BF16)

    def cur(col):
        return pl.BlockSpec((tc, D_CONV), lambda b, i: (b * nchunk + i, col))

    def halo(col):
        return pl.BlockSpec((HALO, D_CONV), lambda b, i: (jnp.maximum((b * nchunk + i) * hb - 1, 0), col))

    vec = pl.BlockSpec((1, D_CONV), lambda b, i: (0, 0))
    out = pl.BlockSpec((tc, D_CONV), lambda b, i: (b * nchunk + i, 0))
    return pl.pallas_call(
        body, grid=(B, nchunk),
        in_specs=[cur(3), halo(3), cur(4), halo(4), pl.BlockSpec((CONV_K, D_CONV), lambda b, i: (0, 0)), vec, vec, vec],
        out_specs=[out, out, out],
        out_shape=[jax.ShapeDtypeStruct((B * S, D_CONV), BF16), jax.ShapeDtypeStruct((B * S, D_CONV), F32),
                   jax.ShapeDtypeStruct((B * S, D_CONV), F32)],
        scratch_shapes=[pltpu.VMEM((tc + HALO, D_CONV), F32), pltpu.VMEM((SUBLANES, tc + HALO, D_CONV), F32)],
        compiler_params=_params("arbitrary", "arbitrary"), name=name)(u, u, u, u, cw, cb, lg, lb)


def _conv_bwd_norm(dcv, y, lg, lb, *, tc, name):
    T = y.shape[0]

    def body(dcv_ref, y_ref, lg_ref, lb_ref, dy_ref, part_ref):
        yv = y_ref[...]
        mu = jnp.mean(yv, axis=-1, keepdims=True)
        yc = yv - mu
        var = jnp.mean(yc * yc, axis=-1, keepdims=True)
        rstd = lax.rsqrt(var + EPS)
        xhat = yc * rstd
        z = xhat * lg_ref[...] + lb_ref[...]
        sig = _sigmoid(z)
        dz = dcv_ref[...] * (sig * (1.0 + z * (1.0 - sig)))
        dxh = dz * lg_ref[...]
        dy = rstd * (dxh - jnp.mean(dxh, axis=-1, keepdims=True)
                     - xhat * jnp.mean(dxh * xhat, axis=-1, keepdims=True))
        dy_ref[...] = dy

        @pl.when(pl.program_id(0) == 0)
        def _():
            part_ref[...] = jnp.zeros_like(part_ref)

        part_ref[0:1, :] += jnp.sum(dz * xhat, axis=0, keepdims=True)
        part_ref[1:2, :] += jnp.sum(dz, axis=0, keepdims=True)
        part_ref[2:3, :] += jnp.sum(dy, axis=0, keepdims=True)

    tok = pl.BlockSpec((tc, D_CONV), lambda i: (i, 0))
    vec = pl.BlockSpec((1, D_CONV), lambda i: (0, 0))
    return pl.pallas_call(
        body, grid=(T // tc,), in_specs=[tok, tok, vec, vec],
        out_specs=[tok, pl.BlockSpec((8, D_CONV), lambda i: (0, 0))],
        out_shape=[jax.ShapeDtypeStruct((T, D_CONV), F32), jax.ShapeDtypeStruct((8, D_CONV), F32)],
        compiler_params=_params("arbitrary"), name=name)(dcv, y, lg, lb)


def _conv_bwd_taps(dy, glu, u, cw, B, S, *, tc, name, phase=None):
    nchunk = S // tc
    hb = tc // HALO
    last_hb = B * S // HALO - 1

    def body(dy_ref, dyn_ref, glu_ref, glup_ref, ca_ref, cg_ref, w_ref, dca_ref, dcg_ref, dw_ref,
             dyp_ref, glp_ref, acc_ref, shd_ref, shg_ref):
        b = pl.program_id(0)
        i = pl.program_id(1)
        dy = dy_ref[...]
        dyp_ref[0:tc, :] = dy
        dyp_ref[tc:, :] = jnp.where(i < nchunk - 1, dyn_ref[...], 0.0)
        glp_ref[0:HALO, :] = jnp.where(i > 0, glup_ref[...], 0.0)
        glp_ref[HALO:, :] = glu_ref[...]
        _shifted_copies(dyp_ref, shd_ref, tc)
        _shifted_copies(glp_ref, shg_ref, tc)

        @pl.when((b == 0) & (i == 0))
        def _():
            dw_ref[...] = jnp.zeros_like(dw_ref)

        for sub in range(tc // SUB):
            acc = jnp.zeros((SUB, D_CONV), F32)
            for k in range(CONV_K):
                acc = acc + _shifted(shd_ref, sub * SUB + (CONV_K - 1) - k, SUB) * w_ref[pl.ds(k, 1), :]
            acc_ref[sub * SUB:(sub + 1) * SUB, :] = acc
        for k in range(CONV_K):
            dw_ref[k:k + 1, :] += jnp.sum(dy * _shifted(shg_ref, HALO - (CONV_K - 1) + k, tc), axis=0, keepdims=True)
        dglu = acc_ref[...]
        ca = ca_ref[...]
        sig = _sigmoid(cg_ref[...])
        dca_ref[...] = (dglu * sig).astype(BF16)
        dcg_ref[...] = (dglu * ca * sig * (1.0 - sig)).astype(BF16)

    tok = pl.BlockSpec((tc, D_CONV), lambda b, i: (b * nchunk + i, 0))
    nxt = pl.BlockSpec((HALO, D_CONV), lambda b, i: (jnp.minimum((b * nchunk + i + 1) * hb, last_hb), 0))
    prv = pl.BlockSpec((HALO, D_CONV), lambda b, i: (jnp.maximum((b * nchunk + i) * hb - 1, 0), 0))
    return _call(
        body, grid=(B, nchunk),
        in_specs=[tok, nxt, tok, prv,
                  pl.BlockSpec((tc, D_CONV), lambda b, i: (b * nchunk + i, 3)),
                  pl.BlockSpec((tc, D_CONV), lambda b, i: (b * nchunk + i, 4)),
                  pl.BlockSpec((CONV_K, D_CONV), lambda b, i: (0, 0))],
        out_specs=[tok, tok, pl.BlockSpec((32, D_CONV), lambda b, i: (0, 0))],
        out_shape=[jax.ShapeDtypeStruct((B * S, D_CONV), BF16), jax.ShapeDtypeStruct((B * S, D_CONV), BF16),
                   jax.ShapeDtypeStruct((32, D_CONV), F32)],
        scratch_shapes=[pltpu.VMEM((tc + HALO, D_CONV), F32), pltpu.VMEM((tc + HALO, D_CONV), F32),
                        pltpu.VMEM((tc, D_CONV), F32), pltpu.VMEM((SUBLANES, tc + HALO, D_CONV), F32),
                        pltpu.VMEM((SUBLANES, tc + HALO, D_CONV), F32)],
        sem=("arbitrary", "arbitrary"), name=name, args=(dy, dy, glu, glu, u, u, cw), phase=phase)


def _outproj_fwd(h, attn, cv, wout, *, tm, name):
    T, D = h.shape

    def body(h_ref, a_ref, c_ref, w_ref, o_ref):
        o_ref[...] = (h_ref[...] + _dot(a_ref[...].astype(BF16), w_ref[0:D_ATTN, :])
                      + _dot(c_ref[...], w_ref[D_ATTN:, :]))

    tok = pl.BlockSpec((tm, D), lambda i: (i, 0))
    half = pl.BlockSpec((tm, D_ATTN), lambda i: (i, 0))
    return pl.pallas_call(
        body, grid=(T // tm,), in_specs=[tok, half, half, pl.BlockSpec(wout.shape, lambda i: (0, 0))],
        out_specs=tok, out_shape=jax.ShapeDtypeStruct((T, D), F32),
        compiler_params=_params("arbitrary"), name=name)(h, attn, cv, wout)


def _outproj_bwd(dh, attn, cv, wout, *, tm, name):
    T, D = dh.shape

    def body(dh_ref, a_ref, c_ref, w_ref, da_ref, dc_ref, dw_ref):
        @pl.when(pl.program_id(0) == 0)
        def _():
            dw_ref[...] = jnp.zeros_like(dw_ref)

        dhb = dh_ref[...].astype(BF16)
        da_ref[...] = _dot_nt(dhb, w_ref[0:D_ATTN, :])
        dc_ref[...] = _dot_nt(dhb, w_ref[D_ATTN:, :])
        dw_ref[0:D_ATTN, :] += _dot_tn(a_ref[...].astype(BF16), dhb)
        dw_ref[D_ATTN:, :] += _dot_tn(c_ref[...], dhb)

    tok = pl.BlockSpec((tm, D), lambda i: (i, 0))
    half = pl.BlockSpec((tm, D_ATTN), lambda i: (i, 0))
    wspec = pl.BlockSpec(wout.shape, lambda i: (0, 0))
    return pl.pallas_call(
        body, grid=(T // tm,), in_specs=[tok, half, half, wspec], out_specs=[half, half, wspec],
        out_shape=[jax.ShapeDtypeStruct((T, D_ATTN), F32), jax.ShapeDtypeStruct((T, D_ATTN), F32),
                   jax.ShapeDtypeStruct(wout.shape, F32)],
        compiler_params=_params("arbitrary"), name=name)(dh, attn, cv, wout)


ADAM_BLOCK_BYTES = 3 * 512 * 1024


def _adamw(w, g, m, v, *, name):
    R, C = w.shape
    tr = R
    for cand in (512, 352, 256, 176, 128, 64, 32, 16, 8):
        if R % cand == 0 and cand * C * 4 <= ADAM_BLOCK_BYTES:
            tr = cand
            break
    c1 = 1.0 - ADAM_B1 ** ADAM_STEP
    c2 = 1.0 - ADAM_B2 ** ADAM_STEP

    def body(w_ref, g_ref, m_ref, v_ref, d_ref, nm_ref, nv_ref):
        gv = g_ref[...]
        nm = ADAM_B1 * m_ref[...] + (1.0 - ADAM_B1) * gv
        nv = ADAM_B2 * v_ref[...] + (1.0 - ADAM_B2) * (gv * gv)
        d_ref[...] = -ADAM_LR * ((nm / c1) / (jnp.sqrt(nv / c2) + ADAM_EPS) + ADAM_WD * w_ref[...])
        nm_ref[...] = nm
        nv_ref[...] = nv

    blk = pl.BlockSpec((tr, C), lambda i: (i, 0))
    return pl.pallas_call(
        body, grid=(R // tr,), in_specs=[blk] * 4, out_specs=[blk] * 3,
        out_shape=[jax.ShapeDtypeStruct((R, C), F32)] * 3,
        compiler_params=_params("arbitrary"), name=name)(w, g, m, v)


TM = 512
TM_FFN_FWD = 1024
TK = 1024
TC = 256


def _local_step(x, tgt, w, overlap=None):
    B, S, D = x.shape
    T = B * S
    x2 = x.reshape(T, D)
    t2 = tgt.reshape(T, D)
    ones = jnp.ones((1, D_ATTN), F32)
    scale = HEAD_DIM ** -0.5
    gains = jnp.stack([jnp.tile(w["q_norm"], (1, HEADS)) * scale, jnp.tile(w["k_norm"], (1, HEADS)), ones])
    g = {}

    def hosting(point, build):
        phase = overlap.phase(point, w, g) if overlap is not None else None
        if phase is None:
            return build(None)
        outs, extra = build(phase)
        overlap.done(point, extra, w, g)
        return outs

    h1, n1, G1, U1 = hosting("ffn1_fwd", lambda ph: _ffn_fwd(
        x2, w["ffn1_norm"], w["wg1"], w["wu1"], w["wd1"], None, tm=TM_FFN_FWD, name="ffn1_fwd", phase=ph))
    u, n2 = hosting("inproj_fwd", lambda ph: _inproj_fwd(h1, w["mix_norm"], w["win"], tm=TM, name="inproj_fwd", phase=ph))
    qkv = _qkv_prep(u, gains, B, S, name="qkv_prep")
    qkv = qkv.reshape(3, N_PATTERNS, T // QBLK, QBLK, D_ATTN)
    o3, lse3 = _attn_fwd(qkv, name="attn_fwd")
    attn, lse = _attn_combine(o3, lse3, B, S, name="attn_combine")
    cv, glu, yconv = _conv_fwd(u, w["conv_w"], w["conv_b"], w["conv_ln_g"], w["conv_ln_b"], B, S, tc=TC, name="conv_fwd")
    h2 = _outproj_fwd(h1, attn, cv, w["wout"], tm=TM, name="outproj_fwd")
    dh3, n3, G2, U2, loss = _ffn_fwd(h2, w["ffn2_norm"], w["wg2"], w["wu2"], w["wd2"], t2, tm=TM_FFN_FWD, name="ffn2_fwd")

    dG, dU, A, dy, dh2, g["ffn2_norm"] = _ffn_bwd_act(dh3, h2, w["ffn2_norm"], G2, U2, w["wg2"], w["wu2"], w["wd2"],
                                                    tm=TM, name="ffn2_bwd_act")
    g["wg2"], g["wu2"], g["wd2"] = _ffn_bwd_w(n3, dy, dG, dU, A, tk=TK, name="ffn2_bwd_w")
    dattn, dcv, g["wout"] = _outproj_bwd(dh2, attn, cv, w["wout"], tm=TM, name="outproj_bwd")
    dyc, cpart = _conv_bwd_norm(dcv, yconv, w["conv_ln_g"], w["conv_ln_b"], tc=TC, name="conv_bwd_norm")
    dca, dcg, dcw = hosting("conv_bwd_taps", lambda ph: _conv_bwd_taps(
        dyc, glu, u, w["conv_w"], B, S, tc=TC, name="conv_bwd_taps", phase=ph))
    do3, st3 = _attn_bwd_prep(dattn, attn, lse, B, S, name="attn_bwd_prep")
    nb = T // QBLK
    (cur,) = hosting("attn_bwd", lambda ph: _attn_bwd(
        qkv, do3.reshape(N_PATTERNS, nb, QBLK, D_ATTN), st3.reshape(N_PATTERNS, nb, QBLK, LANES),
        name="attn_bwd", phase=ph))
    du_qkv, dgains = hosting("attn_grad_combine", lambda ph: _attn_grad_combine(
        cur, u, gains, B, S, name="attn_grad_combine", phase=ph))
    du = jnp.concatenate([du_qkv, dca, dcg], axis=1)
    dh1, g["mix_norm"] = hosting("inproj_bwd_act", lambda ph: _inproj_bwd_act(
        du, dh2, h1, w["mix_norm"], w["win"], tm=TM, name="inproj_bwd_act", phase=ph))
    g["win"] = _inproj_bwd_w(n2, du, w["win"].shape[0], tk=TK, name="inproj_bwd_w")
    dG, dU, A, dy, dx, g["ffn1_norm"] = _ffn_bwd_act(dh1, x2, w["ffn1_norm"], G1, U1, w["wg1"], w["wu1"], w["wd1"],
                                                   tm=TM, name="ffn1_bwd_act")
    g["wg1"], g["wu1"], g["wd1"] = _ffn_bwd_w(n1, dy, dG, dU, A, tk=TK, name="ffn1_bwd_w")

    g["q_norm"] = dgains[0].reshape(HEADS, HEAD_DIM).sum(axis=0, keepdims=True) * scale
    g["k_norm"] = dgains[1].reshape(HEADS, HEAD_DIM).sum(axis=0, keepdims=True)
    g["conv_ln_g"] = cpart[0:1]
    g["conv_ln_b"] = cpart[1:2]
    g["conv_b"] = cpart[2:3]
    g["conv_w"] = dcw[:CONV_K]
    return loss, dx.reshape(B, S, D), g


N_CHIPS = 4
N_DEV = 8
VMEM_SPEC = pl.BlockSpec(memory_space=pltpu.VMEM)


def _remote(src, dst, send_sem, recv_sem, device):
    return pltpu.make_async_remote_copy(src_ref=src, dst_ref=dst, send_sem=send_sem, recv_sem=recv_sem,
                                        device_id=device, device_id_type=MESH)


def _stage_shards(shards, dtypes, *, name):
    n = len(shards)
    halves = [s.reshape(2, s.shape[0] // 2, s.shape[1]) for s in shards]

    def body(*refs):
        ins, outs, vms, loc_sems = refs[:n], refs[n:2 * n], refs[2 * n:3 * n], refs[3 * n]
        me = 2 * lax.axis_index("x") + lax.axis_index("y")
        copies = []
        for a in range(n):
            vms[a][...] = ins[a][...].astype(dtypes[a])
            cp = pltpu.make_async_copy(vms[a], outs[a].at[me], loc_sems.at[a])
            cp.start()
            copies.append(cp)
        for cp in copies:
            cp.wait()

    return pl.pallas_call(
        body, in_specs=[VMEM_SPEC] * n, out_specs=[ANY] * n,
        out_shape=[jax.ShapeDtypeStruct((N_CHIPS,) + h.shape, dt) for h, dt in zip(halves, dtypes)],
        scratch_shapes=[pltpu.VMEM(h.shape, dt) for h, dt in zip(halves, dtypes)] + [DMA_SEMS((n,))],
        compiler_params=pltpu.CompilerParams(vmem_limit_bytes=VMEM_LIMIT), name=name)(*halves)


def _like(arrays):
    return [jax.ShapeDtypeStruct(a.shape, a.dtype) for a in arrays]


def _axes():
    x, y, c = lax.axis_index("x"), lax.axis_index("y"), lax.axis_index("c")
    first = (x + (1 - c) * (1 - 2 * x), y + c * (1 - 2 * y))
    second = (x + c * (1 - 2 * x), y + (1 - c) * (1 - 2 * y))
    slots = tuple(2 * px + py for px, py in ((x, y), first, second, (1 - x, 1 - y)))
    return (x, y, c), (*first, c), (*second, c), slots


def _gather_ici_phase(bufs):
    n = len(bufs)

    def stage1(ins, outs, sems):
        (x, y, c), peer1, peer2, (own, s1, s2, both) = _axes()
        starts, arrivals = [], []
        for a in range(n):
            mine, land = outs[a].at[own, c], outs[a].at[s2, c]
            starts.append(_remote(mine, mine, *sems(a), peer2))
            arrivals.append(_remote(land, land, *sems(a), peer2))
        return starts, arrivals

    def stage2(ins, outs, sems):
        (x, y, c), peer1, peer2, (own, s1, s2, both) = _axes()
        starts, arrivals = [], []
        for a in range(n):
            for k, (src, dst) in enumerate(((own, s1), (s2, both))):
                mine, land = outs[a].at[src, c], outs[a].at[dst, c]
                starts.append(_remote(mine, mine, *sems(2 * a + k), peer1))
                arrivals.append(_remote(land, land, *sems(2 * a + k), peer1))
        return starts, arrivals

    same = {a: a for a in range(n)}
    return _Phase(bufs, _like(bufs), same, n, stage1).then(_Phase(bufs, _like(bufs), same, 2 * n, stage2))


def _gather_d2d_phase(bufs):
    n = len(bufs)

    def copies(ins, outs, sems):
        (x, y, c), peer1, peer2, (own, s1, s2, both) = _axes()
        starts, arrivals = [], []
        for a in range(n):
            for j, s in enumerate((s1, s2, both)):
                got, land = outs[a].at[s, c], outs[a].at[s, 1 - c]
                starts.append(_remote(got, got, *sems(3 * a + j), (x, y, 1 - c)))
                arrivals.append(_remote(land, land, *sems(3 * a + j), (x, y, 1 - c)))
        return starts, arrivals

    return _Phase(bufs, _like(bufs), {a: a for a in range(n)}, 3 * n, copies)


def _exchange_phase(views):
    n = len(views)

    def copies(ins, outs, sems):
        x, y, c = lax.axis_index("x"), lax.axis_index("y"), lax.axis_index("c")
        starts = [_remote(ins[a].at[pl.ds(0, ins[a].shape[0]), 1 - c], outs[a], *sems(a), (x, y, 1 - c))
                  for a in range(n)]
        return starts, starts

    outs = [jax.ShapeDtypeStruct((v.shape[0],) + v.shape[2:], F32) for v in views]
    return _Phase(views, outs, {}, n, copies)


def _row_block(rows):
    for cand in (256, 176, 128, 64, 32, 16, 8):
        if rows % cand == 0:
            return cand
    return rows


def _add_own_half(g, r, sel, *, name):
    ns, _, rh, cdim = g.shape
    tr = _row_block(rh)

    def body(s_ref, gk_ref, rk_ref, gs_ref, rs_ref, keep_ref, send_ref):
        keep_ref[...] = gk_ref[...] + rk_ref[...]
        send_ref[...] = (gs_ref[...] + rs_ref[...]).astype(BF16)

    def g_spec(off):
        return pl.BlockSpec((None, None, tr, cdim), lambda k, i, s: (s[1 + off + k], s[0], i, 0))

    def r_spec(off):
        return pl.BlockSpec((None, tr, cdim), lambda k, i, s: (s[1 + off + k], i, 0))

    out = pl.BlockSpec((None, tr, cdim), lambda k, i, s: (k, i, 0))
    return pl.pallas_call(
        body,
        grid_spec=pltpu.PrefetchScalarGridSpec(
            num_scalar_prefetch=1, grid=(2, rh // tr),
            in_specs=[g_spec(0), r_spec(0), g_spec(2), r_spec(2)], out_specs=[out, out]),
        out_shape=[jax.ShapeDtypeStruct((2, rh, cdim), F32), jax.ShapeDtypeStruct((2, rh, cdim), BF16)],
        compiler_params=_params("arbitrary", "arbitrary"), name=name)(sel, g, r, g, r)


def _swap_phase(arrays, stage):
    n = len(arrays)

    def copies(ins, outs, sems):
        peer = _axes()[stage]
        starts = [_remote(ins[a], outs[a], *sems(a), peer) for a in range(n)]
        return starts, starts

    return _Phase(arrays, _like(arrays), {}, n, copies)


def _add_stage1(keep, got, *, name):
    _, rh, cdim = keep.shape
    tr = _row_block(rh)

    def body(k_ref, g_ref, keep_ref, send_ref):
        keep_ref[...] = k_ref[0] + g_ref[0].astype(F32)
        send_ref[...] = (k_ref[1] + g_ref[1].astype(F32)).astype(BF16)

    blk2 = pl.BlockSpec((2, tr, cdim), lambda i: (0, i, 0))
    blk = pl.BlockSpec((tr, cdim), lambda i: (i, 0))
    return pl.pallas_call(
        body, grid=(rh // tr,), in_specs=[blk2, blk2], out_specs=[blk, blk],
        out_shape=[jax.ShapeDtypeStruct((rh, cdim), F32), jax.ShapeDtypeStruct((rh, cdim), BF16)],
        compiler_params=_params("arbitrary"), name=name)(keep, got)


def _add_stage2(keep, got, sel, *, name):
    rh, cdim = keep.shape
    tr = _row_block(rh)

    def body(s_ref, k_ref, g_ref, o_ref):
        o_ref[...] = k_ref[...] + g_ref[...].astype(F32)

    blk = pl.BlockSpec((tr, cdim), lambda i, s: (i, 0))
    return pl.pallas_call(
        body,
        grid_spec=pltpu.PrefetchScalarGridSpec(
            num_scalar_prefetch=1, grid=(rh // tr,), in_specs=[blk, blk],
            out_specs=pl.BlockSpec((None, tr, cdim), lambda i, s: (s[0], i, 0))),
        out_shape=jax.ShapeDtypeStruct((2, rh, cdim), F32),
        compiler_params=_params("arbitrary"), name=name)(sel, keep, got)


def _join_phase(halves):
    n = len(halves)

    def copies(ins, outs, sems):
        x, y, c = lax.axis_index("x"), lax.axis_index("y"), lax.axis_index("c")
        starts, arrivals = [], []
        for a in range(n):
            mine, land = outs[a].at[c], outs[a].at[1 - c]
            starts.append(_remote(mine, mine, *sems(a), (x, y, 1 - c)))
            arrivals.append(_remote(land, land, *sems(a), (x, y, 1 - c)))
        return starts, arrivals

    return _Phase(halves, _like(halves), {a: a for a in range(n)}, n, copies)


def _slot_order():
    x, y, c = lax.axis_index("x"), lax.axis_index("y"), lax.axis_index("c")
    own, flip_x, flip_y, both = 2 * x + y, 2 * (1 - x) + y, 2 * x + 1 - y, 2 * (1 - x) + 1 - y
    first = jnp.where(c == 0, flip_x, flip_y)
    second = jnp.where(c == 0, flip_y, flip_x)
    return jnp.stack([c, own, second, first, both]).astype(jnp.int32)


def _reduce_scatter(grads):
    sel = _slot_order()
    views = [_half_view(g) for g in grads]
    got = _run_phase(_exchange_phase(views), name="rs_exchange_halves")
    keep, send = _add_halves(views, got, sel, "late")
    got = _run_phase(_swap_phase(send, 1), name="rs_swap_first_axis")
    keep, send = _add_first(keep, got, "late")
    got = _run_phase(_swap_phase(send, 2), name="rs_swap_second_axis")
    halves = _add_second(keep, got, sel, "late")
    full = _run_phase(_join_phase(halves), name="rs_join_halves")
    return [f.reshape(g.shape[1], g.shape[2]) for f, g in zip(full, grads)]


def _half_view(g):
    return g.reshape(N_CHIPS, 2, g.shape[1] // 2, g.shape[2])


def _add_halves(views, got, sel, tag):
    keep, send = zip(*[_add_own_half(v, r, sel, name=f"rs_add_half_{tag}{a}") for a, (v, r) in enumerate(zip(views, got))])
    return list(keep), list(send)


def _add_first(keep, got, tag):
    keep, send = zip(*[_add_stage1(k, r, name=f"rs_add_first_{tag}{a}") for a, (k, r) in enumerate(zip(keep, got))])
    return list(keep), list(send)


def _add_second(keep, got, sel, tag):
    return [_add_stage2(k, r, sel, name=f"rs_add_second_{tag}{a}") for a, (k, r) in enumerate(zip(keep, got))]


EARLY_GRADS = ("wg2", "wu2", "wd2", "wout")


class _Overlap:
    def __init__(self, staged):
        self.staged = staged
        self.sel = _slot_order()
        self.reduced = {}

    def phase(self, point, w, g):
        if point == "ffn1_fwd":
            return _gather_ici_phase(self.staged)
        if point == "inproj_fwd":
            return _gather_d2d_phase(self.ffn2)
        if point == "conv_bwd_taps":
            self.shapes = [g[k].shape for k in EARLY_GRADS]
            self.views = [_half_view(g[k].reshape(N_CHIPS, -1, g[k].shape[-1])) for k in EARLY_GRADS]
            return _exchange_phase(self.views)
        if point == "attn_bwd":
            return _swap_phase(self.send, 1)
        if point == "attn_grad_combine":
            return _swap_phase(self.send, 2)
        if point == "inproj_bwd_act":
            return _join_phase(self.halves)
        return None

    def done(self, point, outs, w, g):
        if point == "ffn1_fwd":
            win, wout, taps = [_whole(b) for b in _run_phase(_gather_d2d_phase(outs[:3]), name="gather_mix_d2d")]
            w["win"] = win
            w["wout"] = wout.reshape(-1, wout.shape[-1])
            w["conv_w"] = taps.transpose(1, 0, 2).reshape(CONV_K + 1, D_CONV)[:CONV_K]
            self.ffn2 = list(outs[3:])
        elif point == "inproj_fwd":
            w["wg2"], w["wu2"], w["wd2"] = [_whole(b) for b in outs]
        elif point == "conv_bwd_taps":
            self.keep, self.send = _add_halves(self.views, outs, self.sel, "early")
        elif point == "attn_bwd":
            self.keep, self.send = _add_first(self.keep, outs, "early")
        elif point == "attn_grad_combine":
            self.halves = _add_second(self.keep, outs, self.sel, "early")
        elif point == "inproj_bwd_act":
            for k, shp, f in zip(EARLY_GRADS, self.shapes, outs):
                self.reduced[k] = f.reshape(-1, shp[-1])


def _whole(buf):
    return buf.reshape(buf.shape[0], 2 * buf.shape[2], buf.shape[3])


def _allreduce_small(pack, *, name):
    rows = pack.shape[0]

    def body(p_ref, o_ref, buf_ref, send_sems, recv_sems):
        x, y, c = lax.axis_index("x"), lax.axis_index("y"), lax.axis_index("c")
        me = 4 * x + 2 * y + c
        buf_ref[me] = p_ref[...]
        cps = []
        for k in range(1, N_DEV):
            peer = tuple(1 - v if (k >> s) & 1 else v for v, s in ((x, 2), (y, 1), (c, 0)))
            cp = _remote(p_ref, buf_ref.at[me], send_sems.at[k - 1], recv_sems.at[k - 1], peer)
            cp.start()
            cps.append(cp)
        for k in range(1, N_DEV):
            src = 4 * (x ^ ((k >> 2) & 1)) + 2 * (y ^ ((k >> 1) & 1)) + (c ^ (k & 1))
            land = buf_ref.at[src]
            _remote(land, land, send_sems.at[k - 1], recv_sems.at[k - 1], (x, y, c)).wait_recv()
        acc = buf_ref[0]
        for d in range(1, N_DEV):
            acc = acc + buf_ref[d]
        o_ref[...] = acc
        for cp in cps:
            cp.wait_send()

    return pl.pallas_call(
        body, in_specs=[VMEM_SPEC], out_specs=VMEM_SPEC, out_shape=jax.ShapeDtypeStruct(pack.shape, F32),
        scratch_shapes=[pltpu.VMEM((N_DEV, rows, LANES), F32), pltpu.SemaphoreType.DMA((N_DEV - 1,)),
                        pltpu.SemaphoreType.DMA((N_DEV - 1,))], name=name)(pack)


SMALL = ("ffn1_norm", "mix_norm", "q_norm", "k_norm", "conv_b", "conv_ln_g", "conv_ln_b", "ffn2_norm", "conv_w")
BIG = ("ffn1_w_gate", "ffn1_w_up", "ffn1_w_down", "w_in", "w_out", "ffn2_w_gate", "ffn2_w_up", "ffn2_w_down")
TRANSPOSED = ("ffn1_w_gate", "ffn1_w_up", "ffn2_w_gate", "ffn2_w_up")
WEIGHTS = ("ffn1_norm", "ffn1_w_gate", "ffn1_w_up", "ffn1_w_down", "mix_norm", "w_in", "q_norm", "k_norm",
           "conv_w", "conv_b", "conv_ln_g", "conv_ln_b", "w_out", "ffn2_norm", "ffn2_w_gate", "ffn2_w_up",
           "ffn2_w_down")


def _pack(parts):
    rows = []
    for p in parts:
        flat = p.reshape(-1)
        tile = SUBLANES * LANES
        padded = -(-flat.shape[0] // tile) * tile
        rows.append(jnp.pad(flat, (0, padded - flat.shape[0])).reshape(-1, LANES))
    return jnp.concatenate(rows, axis=0)


def _unpack(pack, shapes):
    out, row = [], 0
    for shp in shapes:
        size = shp[0] * shp[1]
        tile = SUBLANES * LANES
        nrows = -(-size // tile) * SUBLANES
        out.append(pack[row:row + nrows].reshape(-1)[:size].reshape(shp))
        row += nrows
    return out


def kernel(x, ffn1_norm, ffn1_w_gate, ffn1_w_up, ffn1_w_down, mix_norm, w_in, q_norm, k_norm, conv_w, conv_b, conv_ln_g, conv_ln_b, w_out, ffn2_norm, ffn2_w_gate, ffn2_w_up, ffn2_w_down, loss_target, m_ffn1_norm, m_ffn1_w_gate, m_ffn1_w_up, m_ffn1_w_down, m_mix_norm, m_w_in, m_q_norm, m_k_norm, m_conv_w, m_conv_b, m_conv_ln_g, m_conv_ln_b, m_w_out, m_ffn2_norm, m_ffn2_w_gate, m_ffn2_w_up, m_ffn2_w_down, v_ffn1_norm, v_ffn1_w_gate, v_ffn1_w_up, v_ffn1_w_down, v_mix_norm, v_w_in, v_q_norm, v_k_norm, v_conv_w, v_conv_b, v_conv_ln_g, v_conv_ln_b, v_w_out, v_ffn2_norm, v_ffn2_w_gate, v_ffn2_w_up, v_ffn2_w_down):
    wts = dict(ffn1_norm=ffn1_norm, ffn1_w_gate=ffn1_w_gate[0], ffn1_w_up=ffn1_w_up[0], ffn1_w_down=ffn1_w_down[0],
               mix_norm=mix_norm, w_in=w_in[0], q_norm=q_norm, k_norm=k_norm, conv_w=conv_w[0], conv_b=conv_b,
               conv_ln_g=conv_ln_g, conv_ln_b=conv_ln_b, w_out=w_out[0], ffn2_norm=ffn2_norm,
               ffn2_w_gate=ffn2_w_gate[0], ffn2_w_up=ffn2_w_up[0], ffn2_w_down=ffn2_w_down[0])
    mom = dict(ffn1_norm=m_ffn1_norm, ffn1_w_gate=m_ffn1_w_gate[0], ffn1_w_up=m_ffn1_w_up[0], ffn1_w_down=m_ffn1_w_down[0],
               mix_norm=m_mix_norm, w_in=m_w_in[0], q_norm=m_q_norm, k_norm=m_k_norm, conv_w=m_conv_w[0], conv_b=m_conv_b,
               conv_ln_g=m_conv_ln_g, conv_ln_b=m_conv_ln_b, w_out=m_w_out[0], ffn2_norm=m_ffn2_norm,
               ffn2_w_gate=m_ffn2_w_gate[0], ffn2_w_up=m_ffn2_w_up[0], ffn2_w_down=m_ffn2_w_down[0])
    var = dict(ffn1_norm=v_ffn1_norm, ffn1_w_gate=v_ffn1_w_gate[0], ffn1_w_up=v_ffn1_w_up[0], ffn1_w_down=v_ffn1_w_down[0],
               mix_norm=v_mix_norm, w_in=v_w_in[0], q_norm=v_q_norm, k_norm=v_k_norm, conv_w=v_conv_w[0], conv_b=v_conv_b,
               conv_ln_g=v_conv_ln_g, conv_ln_b=v_conv_ln_b, w_out=v_w_out[0], ffn2_norm=v_ffn2_norm,
               ffn2_w_gate=v_ffn2_w_gate[0], ffn2_w_up=v_ffn2_w_up[0], ffn2_w_down=v_ffn2_w_down[0])
    chip = 2 * lax.axis_index("x") + lax.axis_index("y")
    for src in (wts, mom, var):
        for n in TRANSPOSED:
            src[n] = src[n].T

    taps = jnp.pad(wts["conv_w"], ((0, 1), (0, 0)))
    staged = _stage_shards([wts["ffn1_w_gate"], wts["ffn1_w_up"], wts["ffn1_w_down"], wts["w_in"], wts["w_out"], taps,
                            wts["ffn2_w_gate"], wts["ffn2_w_up"], wts["ffn2_w_down"]],
                           [BF16, BF16, BF16, BF16, BF16, F32, BF16, BF16, BF16], name="stage_shards")
    first = _run_phase(_gather_ici_phase(staged[:3]).then(_gather_d2d_phase(staged[:3])), name="gather_ffn1")
    wg1, wu1, wd1 = [_whole(b) for b in first]
    w = dict(ffn1_norm=ffn1_norm, mix_norm=mix_norm, ffn2_norm=ffn2_norm, q_norm=q_norm, k_norm=k_norm,
             conv_b=conv_b, conv_ln_g=conv_ln_g, conv_ln_b=conv_ln_b, wg1=wg1, wu1=wu1, wd1=wd1)
    overlap = _Overlap(staged[3:])
    loss_part, grad_x, g = _local_step(x, loss_target, w, overlap)

    late = _reduce_scatter([g["wg1"], g["wu1"], g["wd1"], g["win"]])
    big_grads = dict(zip(("ffn1_w_gate", "ffn1_w_up", "ffn1_w_down", "w_in"), late))
    big_grads.update(ffn2_w_gate=overlap.reduced["wg2"], ffn2_w_up=overlap.reduced["wu2"],
                     ffn2_w_down=overlap.reduced["wd2"], w_out=overlap.reduced["wout"])

    small_shapes = [g[n].shape for n in SMALL] + [(SUBLANES, LANES)]
    red = _allreduce_small(_pack([g[n] for n in SMALL] + [loss_part]), name="allreduce_small")
    small = dict(zip(SMALL + ("loss",), _unpack(red, small_shapes)))
    loss = small["loss"][0, 0]
    small["conv_w"] = lax.dynamic_slice_in_dim(small["conv_w"], chip * LANES, LANES, axis=1)

    grads, delta, new_m, new_v = {}, {}, {}, {}
    for n in BIG:
        grads[n] = big_grads[n]
        delta[n], new_m[n], new_v[n] = _adamw(wts[n], grads[n], mom[n], var[n], name=f"adamw_{n}")
    shapes = [wts[n].shape for n in SMALL]
    packs = [_pack([src[n] for n in SMALL]) for src in (wts, small, mom, var)]
    outs = _adamw(*packs, name="adamw_small")
    for dst, pk in zip((delta, new_m, new_v), outs):
        dst.update(zip(SMALL, _unpack(pk, shapes)))
    for n in SMALL:
        grads[n] = small[n]

    def shaped(d, n):
        v = d[n].T if n in TRANSPOSED else d[n]
        return v.reshape((1,) + v.shape) if n in BIG or n == "conv_w" else v

    return (loss, grad_x, *[shaped(grads, n) for n in WEIGHTS], *[shaped(delta, n) for n in WEIGHTS],
            *[shaped(new_m, n) for n in WEIGHTS], *[shaped(new_v, n) for n in WEIGHTS])
```

```python
import functools

import jax
import jax.numpy as jnp
from jax import lax
from jax.experimental import pallas as pl
from jax.experimental.pallas import tpu as pltpu

F32 = jnp.float32
BF16 = jnp.bfloat16

EPS = 1e-6
HEADS = 8
HEAD_DIM = 64
D_ATTN = HEADS * HEAD_DIM
D_CONV = 512
CONV_K = 31
QBLK = 128
N_PATTERNS = 3
DILATIONS = (1, 4, 16)
LANES = 128
NEG = -1e30

ADAM_LR = 0.001
ADAM_B1 = 0.9
ADAM_B2 = 0.999
ADAM_EPS = 1e-08
ADAM_WD = 0.01
ADAM_STEP = 10

VMEM_LIMIT = 56 * 1024 * 1024
MESH = pl.DeviceIdType.MESH

NT_DIMS = (((1,), (1,)), ((), ()))
TN_DIMS = (((0,), (0,)), ((), ()))


def _params(*sem):
    return pltpu.CompilerParams(dimension_semantics=sem, vmem_limit_bytes=VMEM_LIMIT)


def _dot(a, b):
    return jnp.dot(a, b, preferred_element_type=F32)


def _dot_nt(a, b):
    return lax.dot_general(a, b, NT_DIMS, preferred_element_type=F32)


def _dot_tn(a, b):
    return lax.dot_general(a, b, TN_DIMS, preferred_element_type=F32)


def _sigmoid(x):
    return 1.0 / (1.0 + jnp.exp(-x))


def _seg_mean(v, e_ref, width):
    hi = v.astype(BF16)
    lo = (v - hi.astype(F32)).astype(BF16)
    e = e_ref[...]
    return (_dot(hi, e) + _dot(lo, e)) * (1.0 / width)


def _seg_matrix(n):
    i = jnp.arange(n)
    return (i[:, None] // HEAD_DIM == i[None, :] // HEAD_DIM).astype(BF16)


ANY = pl.BlockSpec(memory_space=pl.ANY)
DMA_SEMS = pltpu.SemaphoreType.DMA


class _Phase:
    def __init__(self, ins, outs, aliases, nsem, copies):
        self.ins, self.outs, self.aliases = list(ins), list(outs), dict(aliases)
        self.stages = [(nsem, copies)]

    def then(self, other):
        self.stages = self.stages + other.stages
        return self

    @property
    def nsem(self):
        return sum(n for n, _ in self.stages)

    def _copies(self, k, in_refs, out_refs, send_sems, recv_sems):
        base = sum(n for n, _ in self.stages[:k])
        return self.stages[k][1](in_refs, out_refs, lambda i: (send_sems.at[base + i], recv_sems.at[base + i]))

    def start(self, k, *refs):
        for cp in self._copies(k, *refs)[0]:
            cp.start()

    def finish(self, k, *refs):
        starts, arrivals = self._copies(k, *refs)
        for cp in arrivals:
            cp.wait_recv()
        for cp in starts:
            cp.wait_send()


def _run_phase(phase, *, name):
    n_in, n_out = len(phase.ins), len(phase.outs)

    def body(*refs):
        ins, outs = refs[:n_in], refs[n_in:n_in + n_out]
        send_sems, recv_sems = refs[n_in + n_out:]
        for k in range(len(phase.stages)):
            phase.start(k, ins, outs, send_sems, recv_sems)
            phase.finish(k, ins, outs, send_sems, recv_sems)

    return pl.pallas_call(
        body, in_specs=[ANY] * n_in, out_specs=[ANY] * n_out, out_shape=phase.outs,
        input_output_aliases=phase.aliases,
        scratch_shapes=[DMA_SEMS((phase.nsem,)), DMA_SEMS((phase.nsem,))], name=name)(*phase.ins)


def _call(body, *, grid, in_specs, out_specs, out_shape, scratch_shapes=(), sem, name, args, phase=None):
    in_specs, out_specs, out_shape = list(in_specs), list(out_specs), list(out_shape)
    scratch_shapes = list(scratch_shapes)
    if phase is None:
        return pl.pallas_call(body, grid=grid, in_specs=in_specs, out_specs=out_specs, out_shape=out_shape,
                              scratch_shapes=scratch_shapes, compiler_params=_params(*sem), name=name)(*args)
    n_in, n_out, n_scr = len(in_specs), len(out_specs), len(scratch_shapes)
    p_in, p_out = len(phase.ins), len(phase.outs)

    def hosted(*refs):
        ins, pins = refs[:n_in], refs[n_in:n_in + p_in]
        o0 = n_in + p_in
        outs, pouts = refs[o0:o0 + n_out], refs[o0 + n_out:o0 + n_out + p_out]
        s0 = o0 + n_out + p_out
        scr = refs[s0:s0 + n_scr]
        send_sems, recv_sems = refs[s0 + n_scr:]
        step = 0
        for d, n in enumerate(grid):
            step = step * n + pl.program_id(d)
        nsteps = functools.reduce(lambda a, b: a * b, grid)
        nstages = len(phase.stages)
        comm_refs = (pins, pouts, send_sems, recv_sems)

        for k in range(nstages):
            @pl.when(step == (k * nsteps) // nstages)
            def _(k=k):
                if k > 0:
                    phase.finish(k - 1, *comm_refs)
                phase.start(k, *comm_refs)

        body(*ins, *outs, *scr)

        @pl.when(step == nsteps - 1)
        def _():
            phase.finish(nstages - 1, *comm_refs)

    res = pl.pallas_call(
        hosted, grid=grid, in_specs=in_specs + [ANY] * p_in, out_specs=out_specs + [ANY] * p_out,
        out_shape=out_shape + phase.outs,
        input_output_aliases={n_in + i: n_out + o for i, o in phase.aliases.items()},
        scratch_shapes=scratch_shapes + [DMA_SEMS((phase.nsem,)), DMA_SEMS((phase.nsem,))],
        compiler_params=_params(*sem), name=name)(*args, *phase.ins)
    return res[:n_out], res[n_out:]


ROW_CHUNK = 256


def _ffn_fwd(x, gain, wg, wu, wd, tgt, *, tm, name, phase=None):
    T, D = x.shape
    NS, Fs, _ = wg.shape
    with_loss = tgt is not None

    def body(*refs):
        if with_loss:
            x_ref, g_ref, wg_ref, wu_ref, wd_ref, t_ref, h_ref, n_ref, G_ref, U_ref, loss_ref, acc_ref = refs
        else:
            x_ref, g_ref, wg_ref, wu_ref, wd_ref, h_ref, n_ref, G_ref, U_ref, acc_ref = refs
        i = pl.program_id(0)
        j = pl.program_id(1)

        @pl.when(j == 0)
        def _():
            xv = x_ref[...]
            r = lax.rsqrt(jnp.mean(xv * xv, axis=-1, keepdims=True) + EPS)
            n_ref[...] = (xv * r * g_ref[...]).astype(BF16)
            acc_ref[...] = jnp.zeros_like(acc_ref)

        n = n_ref[...]
        G = _dot_nt(n, wg_ref[...])
        U = _dot_nt(n, wu_ref[...])
        G_ref[...] = G.astype(BF16)
        U_ref[...] = U.astype(BF16)
        A = (G * _sigmoid(G) * U).astype(BF16)
        acc_ref[...] += _dot(A, wd_ref[...])

        @pl.when(j == NS - 1)
        def _():
            h = x_ref[...] + 0.5 * acc_ref[...]
            if with_loss:
                e = h - t_ref[...]
                h_ref[...] = e * (1.0 / D)

                @pl.when(i == 0)
                def _():
                    loss_ref[...] = jnp.zeros_like(loss_ref)

                loss_ref[...] += jnp.sum(e * e) * (0.5 / D)
            else:
                h_ref[...] = h

    tok = pl.BlockSpec((tm, D), lambda i, j: (i, 0))
    in_specs = [tok, pl.BlockSpec((1, D), lambda i, j: (0, 0)),
                pl.BlockSpec((None, Fs, D), lambda i, j: (j, 0, 0)),
                pl.BlockSpec((None, Fs, D), lambda i, j: (j, 0, 0)),
                pl.BlockSpec((None, Fs, D), lambda i, j: (j, 0, 0))]
    args = [x, gain, wg, wu, wd]
    act = pl.BlockSpec((None, tm, Fs), lambda i, j: (j, i, 0))
    out_shape = [jax.ShapeDtypeStruct((T, D), F32), jax.ShapeDtypeStruct((T, D), BF16),
                 jax.ShapeDtypeStruct((NS, T, Fs), BF16), jax.ShapeDtypeStruct((NS, T, Fs), BF16)]
    out_specs = [tok, tok, act, act]
    if with_loss:
        in_specs.append(tok)
        args.append(tgt)
        out_shape.append(jax.ShapeDtypeStruct((8, LANES), F32))
        out_specs.append(pl.BlockSpec((8, LANES), lambda i, j: (0, 0)))
    return _call(body, grid=(T // tm, NS), in_specs=in_specs, out_specs=out_specs, out_shape=out_shape,
                 scratch_shapes=[pltpu.VMEM((tm, D), F32)], sem=("arbitrary", "arbitrary"), name=name,
                 args=args, phase=phase)


def _rms_bwd(xv, gain, dn):
    r = lax.rsqrt(jnp.mean(xv * xv, axis=-1, keepdims=True) + EPS)
    xhat = xv * r
    dxh = dn * gain
    dx = r * (dxh - xhat * jnp.mean(dxh * xhat, axis=-1, keepdims=True))
    dg = jnp.sum(dn * xhat, axis=0, keepdims=True)
    return dx, dg


def _ffn_bwd_act(dh, x, gain, G, U, wg, wu, wd, *, tm, name):
    T, D = x.shape
    NS, Fs, _ = wg.shape

    def body(dh_ref, x_ref, g_ref, G_ref, U_ref, wg_ref, wu_ref, wd_ref,
             dG_ref, dU_ref, A_ref, dy_ref, dx_ref, dg_ref, acc_ref):
        i = pl.program_id(0)
        j = pl.program_id(1)

        @pl.when(j == 0)
        def _():
            dy_ref[...] = (0.5 * dh_ref[...]).astype(BF16)
            acc_ref[...] = jnp.zeros_like(acc_ref)

        @pl.when((i == 0) & (j == 0))
        def _():
            dg_ref[...] = jnp.zeros_like(dg_ref)

        for r0 in range(0, tm, ROW_CHUNK):
            rows = slice(r0, r0 + ROW_CHUNK)
            Gv = G_ref[rows, :].astype(F32)
            Uv = U_ref[rows, :].astype(F32)
            sig = _sigmoid(Gv)
            s = Gv * sig
            dA = _dot_nt(dy_ref[rows, :], wd_ref[...])
            dG = (dA * Uv * (sig * (1.0 + Gv * (1.0 - sig)))).astype(BF16)
            dU = (dA * s).astype(BF16)
            dG_ref[rows, :] = dG
            dU_ref[rows, :] = dU
            A_ref[rows, :] = (s * Uv).astype(BF16)
            acc_ref[rows, :] += _dot(dG, wg_ref[...]) + _dot(dU, wu_ref[...])

        @pl.when(j == NS - 1)
        def _():
            dx, dg = _rms_bwd(x_ref[...], g_ref[...], acc_ref[...])
            dx_ref[...] = dh_ref[...] + dx
            dg_ref[...] += dg

    tok = pl.BlockSpec((tm, D), lambda i, j: (i, 0))
    act = pl.BlockSpec((None, tm, Fs), lambda i, j: (j, i, 0))
    vec = pl.BlockSpec((1, D), lambda i, j: (0, 0))
    return pl.pallas_call(
        body, grid=(T // tm, NS),
        in_specs=[tok, tok, vec, act, act,
                  pl.BlockSpec((None, Fs, D), lambda i, j: (j, 0, 0)),
                  pl.BlockSpec((None, Fs, D), lambda i, j: (j, 0, 0)),
                  pl.BlockSpec((None, Fs, D), lambda i, j: (j, 0, 0))],
        out_specs=[act, act, act, tok, tok, vec],
        out_shape=[jax.ShapeDtypeStruct((NS, T, Fs), BF16)] * 3
        + [jax.ShapeDtypeStruct((T, D), BF16), jax.ShapeDtypeStruct((T, D), F32),
           jax.ShapeDtypeStruct((1, D), F32)],
        scratch_shapes=[pltpu.VMEM((tm, D), F32)],
        compiler_params=_params("arbitrary", "arbitrary"), name=name)(dh, x, gain, G, U, wg, wu, wd)


def _ffn_bwd_w(n, dy, dG, dU, A, *, tk, name):
    T, D = n.shape
    NS, _, Fs = dG.shape

    def body(n_ref, dy_ref, dG_ref, dU_ref, A_ref, wg_ref, wu_ref, wd_ref):
        @pl.when(pl.program_id(1) == 0)
        def _():
            wg_ref[...] = jnp.zeros_like(wg_ref)
            wu_ref[...] = jnp.zeros_like(wu_ref)
            wd_ref[...] = jnp.zeros_like(wd_ref)

        nv = n_ref[...]
        wg_ref[...] += _dot_tn(dG_ref[...], nv)
        wu_ref[...] += _dot_tn(dU_ref[...], nv)
        wd_ref[...] += _dot_tn(A_ref[...], dy_ref[...])

    tok = pl.BlockSpec((tk, D), lambda j, k: (k, 0))
    act = pl.BlockSpec((None, tk, Fs), lambda j, k: (j, k, 0))
    return pl.pallas_call(
        body, grid=(NS, T // tk), in_specs=[tok, tok, act, act, act],
        out_specs=[pl.BlockSpec((None, Fs, D), lambda j, k: (j, 0, 0))] * 3,
        out_shape=[jax.ShapeDtypeStruct((NS, Fs, D), F32)] * 3,
        compiler_params=_params("arbitrary", "arbitrary"), name=name)(n, dy, dG, dU, A)


def _inproj_fwd(h, gain, win, *, tm, name, phase=None):
    T, D = h.shape
    NS, _, Cs = win.shape

    def body(h_ref, g_ref, w_ref, u_ref, n_ref):
        @pl.when(pl.program_id(1) == 0)
        def _():
            xv = h_ref[...]
            r = lax.rsqrt(jnp.mean(xv * xv, axis=-1, keepdims=True) + EPS)
            n_ref[...] = (xv * r * g_ref[...]).astype(BF16)

        u_ref[...] = _dot(n_ref[...], w_ref[...])

    tok = pl.BlockSpec((tm, D), lambda i, j: (i, 0))
    return _call(
        body, grid=(T // tm, NS),
        in_specs=[tok, pl.BlockSpec((1, D), lambda i, j: (0, 0)),
                  pl.BlockSpec((None, D, Cs), lambda i, j: (j, 0, 0))],
        out_specs=[pl.BlockSpec((tm, Cs), lambda i, j: (i, j)), tok],
        out_shape=[jax.ShapeDtypeStruct((T, NS * Cs), F32), jax.ShapeDtypeStruct((T, D), BF16)],
        sem=("arbitrary", "arbitrary"), name=name, args=(h, gain, win), phase=phase)


def _inproj_bwd_act(du, dh, h, gain, win, *, tm, name, phase=None):
    T, D = h.shape
    NS, _, Cs = win.shape

    def body(du_ref, dh_ref, h_ref, g_ref, w_ref, dx_ref, dg_ref, acc_ref):
        i = pl.program_id(0)
        j = pl.program_id(1)

        @pl.when(j == 0)
        def _():
            acc_ref[...] = jnp.zeros_like(acc_ref)

        @pl.when((i == 0) & (j == 0))
        def _():
            dg_ref[...] = jnp.zeros_like(dg_ref)

        acc_ref[...] += _dot_nt(du_ref[...], w_ref[...])

        @pl.when(j == NS - 1)
        def _():
            dx, dg = _rms_bwd(h_ref[...], g_ref[...], acc_ref[...])
            dx_ref[...] = dh_ref[...] + dx
            dg_ref[...] += dg

    tok = pl.BlockSpec((tm, D), lambda i, j: (i, 0))
    vec = pl.BlockSpec((1, D), lambda i, j: (0, 0))
    return _call(
        body, grid=(T // tm, NS),
        in_specs=[pl.BlockSpec((tm, Cs), lambda i, j: (i, j)), tok, tok, vec,
                  pl.BlockSpec((None, D, Cs), lambda i, j: (j, 0, 0))],
        out_specs=[tok, vec],
        out_shape=[jax.ShapeDtypeStruct((T, D), F32), jax.ShapeDtypeStruct((1, D), F32)],
        scratch_shapes=[pltpu.VMEM((tm, D), F32)],
        sem=("arbitrary", "arbitrary"), name=name, args=(du, dh, h, gain, win), phase=phase)


def _inproj_bwd_w(n, du, ns, *, tk, name):
    T, D = n.shape
    Cs = du.shape[1] // ns

    def body(n_ref, du_ref, w_ref):
        @pl.when(pl.program_id(1) == 0)
        def _():
            w_ref[...] = jnp.zeros_like(w_ref)

        w_ref[...] += _dot_tn(n_ref[...], du_ref[...])

    return pl.pallas_call(
        body, grid=(ns, T // tk),
        in_specs=[pl.BlockSpec((tk, D), lambda j, k: (k, 0)), pl.BlockSpec((tk, Cs), lambda j, k: (k, j))],
        out_specs=pl.BlockSpec((None, D, Cs), lambda j, k: (j, 0, 0)),
        out_shape=jax.ShapeDtypeStruct((ns, D, Cs), F32),
        compiler_params=_params("arbitrary", "arbitrary"), name=name)(n, du)


STRIDE = 4


def _permute(src_ref, tmp_ref, put):
    S = src_ref.shape[0]
    L4, L16 = S // STRIDE, S // (STRIDE * STRIDE)
    put(0, 0, src_ref[...])
    for r0 in range(STRIDE):
        v = src_ref[pl.ds(r0, L4, stride=STRIDE), :]
        put(1, r0 * L4, v)
        tmp_ref[r0 * L4:(r0 + 1) * L4, :] = v
    for r0 in range(STRIDE):
        for r1 in range(STRIDE):
            put(2, (r1 * STRIDE + r0) * L16, tmp_ref[pl.ds(r0 * L4 + r1, L16, stride=STRIDE), :])


def _permute_out(src_ref, tmp_ref, out_ref, cast):
    for cc in range(src_ref.shape[0]):
        cols = slice(cc * LANES, (cc + 1) * LANES)

        def put(p, row0, v, cols=cols):
            out_ref[p, row0:row0 + v.shape[0], cols] = v.astype(cast)

        _permute(src_ref.at[cc], tmp_ref, put)


def _unpermute_in(get_block, dst_ref, tmp_ref, p, S):
    L4, L16 = S // STRIDE, S // (STRIDE * STRIDE)
    if p == 0:
        dst_ref[...] = get_block(0, S)
        return
    if p == 1:
        for r0 in range(STRIDE):
            dst_ref[pl.ds(r0, L4, stride=STRIDE), :] = get_block(r0 * L4, L4)
        return
    for r0 in range(STRIDE):
        for r1 in range(STRIDE):
            tmp_ref[pl.ds(r0 * L4 + r1, L16, stride=STRIDE), :] = get_block((r1 * STRIDE + r0) * L16, L16)
    for r0 in range(STRIDE):
        dst_ref[pl.ds(r0, L4, stride=STRIDE), :] = tmp_ref[r0 * L4:(r0 + 1) * L4, :]


def _qkv_prep(u, gains, B, S, *, name):
    emat = _seg_matrix(D_ATTN)

    def body(u_ref, g_ref, e_ref, out_ref, scr_ref, tmp_ref):
        c = pl.program_id(1)
        xv = u_ref[...]
        ms = _seg_mean(xv * xv, e_ref, HEAD_DIM)
        r = jnp.where(c < 2, lax.rsqrt(ms + EPS), 1.0)
        yv = xv * r * g_ref[...]
        for cc in range(4):
            scr_ref[cc] = yv[:, cc * LANES:(cc + 1) * LANES]
        _permute_out(scr_ref, tmp_ref, out_ref, BF16)

    return pl.pallas_call(
        body, grid=(B, 3),
        in_specs=[pl.BlockSpec((S, D_ATTN), lambda b, c: (b, c)),
                  pl.BlockSpec((None, 1, D_ATTN), lambda b, c: (c, 0, 0)),
                  pl.BlockSpec((D_ATTN, D_ATTN), lambda b, c: (0, 0))],
        out_specs=pl.BlockSpec((None, N_PATTERNS, None, S, D_ATTN), lambda b, c: (c, 0, b, 0, 0)),
        out_shape=jax.ShapeDtypeStruct((3, N_PATTERNS, B, S, D_ATTN), BF16),
        scratch_shapes=[pltpu.VMEM((4, S, LANES), F32), pltpu.VMEM((S, LANES), F32)],
        compiler_params=_params("arbitrary", "arbitrary"), name=name)(u, gains, emat)


def _band_mask(p, b):
    nblk = jnp.right_shift(16, 2 * p)
    has_prev = jnp.bitwise_and(b, nblk - 1) != 0
    qi = lax.broadcasted_iota(jnp.int32, (QBLK, 2 * QBLK), 0)
    ci = lax.broadcasted_iota(jnp.int32, (QBLK, 2 * QBLK), 1)
    dist = QBLK + qi - ci
    return (dist >= 0) & (dist <= QBLK) & (has_prev | (ci >= QBLK))


def _first_head(rows):
    return lax.broadcasted_iota(jnp.int32, (rows, LANES), 1) < HEAD_DIM


def _split_heads(pair):
    first = _first_head(pair.shape[0])
    zero = jnp.zeros_like(pair)
    return jnp.concatenate([jnp.where(first, pair, zero), jnp.where(first, zero, pair)], axis=0)


def _merge_heads(col_a, col_b):
    rows = col_a.shape[0]
    return jnp.where(_first_head(rows), jnp.broadcast_to(col_a, (rows, LANES)), jnp.broadcast_to(col_b, (rows, LANES)))


def _attn_specs(nb):
    blk = (None, None, None, QBLK, D_ATTN)
    q_spec = pl.BlockSpec(blk, lambda p, b: (0, p, b, 0, 0))
    kp_spec = pl.BlockSpec(blk, lambda p, b: (1, p, jnp.maximum(b - 1, 0), 0, 0))
    kc_spec = pl.BlockSpec(blk, lambda p, b: (1, p, b, 0, 0))
    vp_spec = pl.BlockSpec(blk, lambda p, b: (2, p, jnp.maximum(b - 1, 0), 0, 0))
    vc_spec = pl.BlockSpec(blk, lambda p, b: (2, p, b, 0, 0))
    return [q_spec, kp_spec, kc_spec, vp_spec, vc_spec]


def _attn_fwd(qkv, *, name):
    nb = qkv.shape[2]

    def body(q_ref, kp_ref, kc_ref, vp_ref, vc_ref, o_ref, lse_ref):
        mask = _band_mask(pl.program_id(0), pl.program_id(1))
        mask2 = jnp.concatenate([mask, mask], axis=0)
        kk = jnp.concatenate([kp_ref[...], kc_ref[...]], axis=0)
        vv = jnp.concatenate([vp_ref[...], vc_ref[...]], axis=0)
        for hp in range(HEADS // 2):
            cols = slice(hp * LANES, (hp + 1) * LANES)
            s = _dot_nt(_split_heads(q_ref[:, cols]), kk[:, cols])
            s = jnp.where(mask2, s, NEG)
            m = jnp.max(s, axis=-1, keepdims=True)
            e = jnp.exp(s - m)
            l = jnp.sum(e, axis=-1, keepdims=True)
            pr = (e * (1.0 / l)).astype(BF16)
            o_ref[:, cols] = _dot(jnp.concatenate([pr[:QBLK], pr[QBLK:]], axis=1), _split_heads(vv[:, cols]))
            lse = m + jnp.log(l)
            lse_ref[:, cols] = _merge_heads(lse[:QBLK], lse[QBLK:])

    out = pl.BlockSpec((None, None, QBLK, D_ATTN), lambda p, b: (p, b, 0, 0))
    return pl.pallas_call(
        body, grid=(N_PATTERNS, nb), in_specs=_attn_specs(nb), out_specs=[out, out],
        out_shape=[jax.ShapeDtypeStruct((N_PATTERNS, nb, QBLK, D_ATTN), F32)] * 2,
        compiler_params=_params("arbitrary", "arbitrary"), name=name)(qkv, qkv, qkv, qkv, qkv)


def _attn_combine(o3, lse3, B, S, *, name):
    def body(o_ref, l_ref, a_ref, lt_ref, so_ref, sl_ref, tmp_ref):
        for p in range(N_PATTERNS):
            _unpermute_in(lambda r0, n, p=p: o_ref[p, pl.ds(r0, n), :], so_ref.at[p], tmp_ref, p, S)
            _unpermute_in(lambda r0, n, p=p: l_ref[p, pl.ds(r0, n), :], sl_ref.at[p], tmp_ref, p, S)
        l0, l1, l2 = sl_ref[0], sl_ref[1], sl_ref[2]
        m = jnp.maximum(jnp.maximum(l0, l1), l2)
        w0, w1, w2 = jnp.exp(l0 - m), jnp.exp(l1 - m), jnp.exp(l2 - m)
        tot = w0 + w1 + w2
        a_ref[...] = (w0 * so_ref[0] + w1 * so_ref[1] + w2 * so_ref[2]) / tot
        lt_ref[...] = m + jnp.log(tot)

    o3 = o3.reshape(N_PATTERNS, B, S, D_ATTN)
    lse3 = lse3.reshape(N_PATTERNS, B, S, D_ATTN)
    inp = pl.BlockSpec((N_PATTERNS, None, S, LANES), lambda b, c: (0, b, 0, c))
    out = pl.BlockSpec((S, LANES), lambda b, c: (b, c))
    return pl.pallas_call(
        body, grid=(B, D_ATTN // LANES), in_specs=[inp, inp], out_specs=[out, out],
        out_shape=[jax.ShapeDtypeStruct((B * S, D_ATTN), F32)] * 2,
        scratch_shapes=[pltpu.VMEM((N_PATTERNS, S, LANES), F32)] * 2 + [pltpu.VMEM((S, LANES), F32)],
        compiler_params=_params("arbitrary", "arbitrary"), name=name)(o3, lse3)


STAT_D = 8


def _attn_bwd_prep(dattn, attn, lse, B, S, *, name):
    emat = _seg_matrix(LANES)
    ncc = D_ATTN // LANES

    def body(da_ref, a_ref, l_ref, e_ref, do_ref, st_ref, scr_ref, nat_ref, tmp_ref):
        cc = pl.program_id(1)
        da = da_ref[...]
        dsum = _seg_mean(da * a_ref[...], e_ref, 1.0)
        scr_ref[...] = da

        def put_do(p, row0, v):
            do_ref[p, row0:row0 + v.shape[0], :] = v.astype(BF16)

        _permute(scr_ref, tmp_ref, put_do)

        lane = lax.broadcasted_iota(jnp.int32, (S, LANES), 1)
        h0 = 2 * cc
        vals = ((h0, l_ref[:, 0:1]), (h0 + 1, l_ref[:, HEAD_DIM:HEAD_DIM + 1]),
                (STAT_D + h0, dsum[:, 0:1]), (STAT_D + h0 + 1, dsum[:, HEAD_DIM:HEAD_DIM + 1]))
        tile = jnp.where(cc == 0, 0.0, nat_ref[...])
        for at, col in vals:
            tile = jnp.where(lane == at, col, tile)
        nat_ref[...] = tile

        @pl.when(cc == ncc - 1)
        def _():
            def put_st(p, row0, v):
                st_ref[p, row0:row0 + v.shape[0], :] = v

            _permute(nat_ref, tmp_ref, put_st)

    inp = pl.BlockSpec((S, LANES), lambda b, c: (b, c))
    return pl.pallas_call(
        body, grid=(B, ncc),
        in_specs=[inp, inp, inp, pl.BlockSpec((LANES, LANES), lambda b, c: (0, 0))],
        out_specs=[pl.BlockSpec((N_PATTERNS, None, S, LANES), lambda b, c: (0, b, 0, c)),
                   pl.BlockSpec((N_PATTERNS, None, S, LANES), lambda b, c: (0, b, 0, 0))],
        out_shape=[jax.ShapeDtypeStruct((N_PATTERNS, B, S, D_ATTN), BF16),
                   jax.ShapeDtypeStruct((N_PATTERNS, B, S, LANES), F32)],
        scratch_shapes=[pltpu.VMEM((S, LANES), F32)] * 3,
        compiler_params=_params("arbitrary", "arbitrary"), name=name)(dattn, attn, lse, emat)


def _attn_bwd(qkv, do3, st3, *, name, phase=None):
    nb = qkv.shape[2]

    def body(q_ref, kp_ref, kc_ref, vp_ref, vc_ref, do_ref, st_ref, out_ref, carry_ref):
        p = pl.program_id(0)
        b = pl.program_id(1)

        @pl.when((p == 0) & (b == 0))
        def _():
            carry_ref[...] = jnp.zeros_like(carry_ref)

        mask = _band_mask(p, b) & (b < nb)
        mask2 = jnp.concatenate([mask, mask], axis=0)
        kk = jnp.concatenate([kp_ref[...], kc_ref[...]], axis=0)
        vv = jnp.concatenate([vp_ref[...], vc_ref[...]], axis=0)
        for hp in range(HEADS // 2):
            cols = slice(hp * LANES, (hp + 1) * LANES)
            h0, h1 = 2 * hp, 2 * hp + 1
            kh, vh = kk[:, cols], vv[:, cols]
            q2 = _split_heads(q_ref[:, cols])
            do2 = _split_heads(do_ref[:, cols])
            lse = jnp.concatenate([st_ref[:, h0:h0 + 1], st_ref[:, h1:h1 + 1]], axis=0)
            dsum = jnp.concatenate([st_ref[:, STAT_D + h0:STAT_D + h0 + 1], st_ref[:, STAT_D + h1:STAT_D + h1 + 1]], axis=0)
            s = _dot_nt(q2, kh)
            pr = jnp.where(mask2, jnp.exp(s - lse), 0.0)
            dp = _dot_nt(do2, vh)
            ds = (pr * (dp - dsum)).astype(BF16)
            prb = pr.astype(BF16)
            dq = _dot(jnp.concatenate([ds[:QBLK], ds[QBLK:]], axis=1), _split_heads(kh))
            dk = _dot_tn(ds, q2)
            dv = _dot_tn(prb, do2)
            out_ref[0, :, cols] = carry_ref[0, :, cols].astype(BF16)
            out_ref[1, :, cols] = (carry_ref[1, :, cols] + dk[:QBLK]).astype(BF16)
            out_ref[2, :, cols] = (carry_ref[2, :, cols] + dv[:QBLK]).astype(BF16)
            carry_ref[0, :, cols] = dq
            carry_ref[1, :, cols] = dk[QBLK:]
            carry_ref[2, :, cols] = dv[QBLK:]

    def at(which, back):
        return lambda p, b: (which, p, jnp.maximum(jnp.minimum(b, nb - 1) - back, 0), 0, 0)

    blk = (None, None, None, QBLK, D_ATTN)
    aux = lambda lanes: pl.BlockSpec((None, None, QBLK, lanes), lambda p, b: (p, jnp.minimum(b, nb - 1), 0, 0))
    return _call(
        body, grid=(N_PATTERNS, nb + 1),
        in_specs=[pl.BlockSpec(blk, at(0, 0)), pl.BlockSpec(blk, at(1, 1)), pl.BlockSpec(blk, at(1, 0)),
                  pl.BlockSpec(blk, at(2, 1)), pl.BlockSpec(blk, at(2, 0)), aux(D_ATTN), aux(LANES)],
        out_specs=[pl.BlockSpec((3, None, None, QBLK, D_ATTN), lambda p, b: (0, p, jnp.maximum(b - 1, 0), 0, 0))],
        out_shape=[jax.ShapeDtypeStruct((3, N_PATTERNS, nb, QBLK, D_ATTN), BF16)],
        scratch_shapes=[pltpu.VMEM((3, QBLK, D_ATTN), F32)],
        sem=("arbitrary", "arbitrary"), name=name, args=(qkv, qkv, qkv, qkv, qkv, do3, st3), phase=phase)


def _attn_grad_combine(cur, u, gains, B, S, *, name, phase=None):
    emat = _seg_matrix(LANES)

    def body(cur_ref, u_ref, g_ref, e_ref, du_ref, dg_ref, scr_ref, tmp_ref):
        c = pl.program_id(0)
        b = pl.program_id(2)
        for p in range(N_PATTERNS):
            _unpermute_in(lambda r0, n, p=p: cur_ref[p, pl.ds(r0, n), :].astype(F32), scr_ref.at[p], tmp_ref, p, S)
        dy = scr_ref[0] + scr_ref[1] + scr_ref[2]
        xv = u_ref[...]
        gain = g_ref[...]
        ms = _seg_mean(xv * xv, e_ref, HEAD_DIM)
        r = lax.rsqrt(ms + EPS)
        xhat = xv * r
        dxh = dy * gain
        dx = r * (dxh - xhat * _seg_mean(dxh * xhat, e_ref, HEAD_DIM))
        du_ref[...] = jnp.where(c < 2, dx, dy).astype(BF16)

        @pl.when((b == 0))
        def _():
            dg_ref[...] = jnp.zeros_like(dg_ref)

        dg_ref[...] += jnp.sum(dy * xhat, axis=0, keepdims=True)

    cur = cur.reshape(3, N_PATTERNS, B, S, D_ATTN)
    ncc = D_ATTN // LANES
    return _call(
        body, grid=(3, ncc, B),
        in_specs=[pl.BlockSpec((None, N_PATTERNS, None, S, LANES), lambda c, cc, b: (c, 0, b, 0, cc)),
                  pl.BlockSpec((S, LANES), lambda c, cc, b: (b, c * ncc + cc)),
                  pl.BlockSpec((None, 1, LANES), lambda c, cc, b: (c, 0, cc)),
                  pl.BlockSpec((LANES, LANES), lambda c, cc, b: (0, 0))],
        out_specs=[pl.BlockSpec((S, LANES), lambda c, cc, b: (b, c * ncc + cc)),
                   pl.BlockSpec((None, 1, LANES), lambda c, cc, b: (c, 0, cc))],
        out_shape=[jax.ShapeDtypeStruct((B * S, 3 * D_ATTN), BF16), jax.ShapeDtypeStruct((3, 1, D_ATTN), F32)],
        scratch_shapes=[pltpu.VMEM((N_PATTERNS, S, LANES), F32), pltpu.VMEM((S, LANES), F32)],
        sem=("arbitrary", "arbitrary", "arbitrary"), name=name, args=(cur, u, gains, emat), phase=phase)


HALO = 32
SUB = 64
SUBLANES = 8


def _shifted_copies(src_ref, sh_ref, tc):
    sh_ref[0] = src_ref[...]
    for r in range(1, SUBLANES):
        sh_ref[r, 0:tc + HALO - SUBLANES, :] = src_ref[pl.ds(r, tc + HALO - SUBLANES), :]


def _shifted(sh_ref, start, size):
    return sh_ref[start % SUBLANES, pl.ds(start - start % SUBLANES, size), :]


def _conv_fwd(u, cw, cb, lg, lb, B, S, *, tc, name):
    nchunk = S // tc
    hb = tc // HALO

    def body(ca_ref, cap_ref, cg_ref, cgp_ref, w_ref, cb_ref, lg_ref, lb_ref, cv_ref, glu_ref, y_ref, pad_ref, sh_ref):
        i = pl.program_id(1)
        glu = ca_ref[...] * _sigmoid(cg_ref[...])
        glu_ref[...] = glu
        prev = cap_ref[...] * _sigmoid(cgp_ref[...])
        pad_ref[0:HALO, :] = jnp.where(i > 0, prev, 0.0)
        pad_ref[HALO:, :] = glu
        _shifted_copies(pad_ref, sh_ref, tc)
        for sub in range(tc // SUB):
            acc = jnp.zeros((SUB, D_CONV), F32) + cb_ref[...]
            for k in range(CONV_K):
                acc = acc + _shifted(sh_ref, sub * SUB + HALO - (CONV_K - 1) + k, SUB) * w_ref[pl.ds(k, 1), :]
            y_ref[sub * SUB:(sub + 1) * SUB, :] = acc
        y = y_ref[...]
        mu = jnp.mean(y, axis=-1, keepdims=True)
        yc = y - mu
        var = jnp.mean(yc * yc, axis=-1, keepdims=True)
        z = yc * lax.rsqrt(var + EPS) * lg_ref[...] + lb_ref[...]
        cv_ref[...] = (z * _sigmoid(z)).astype(BF16)

    def cur(col):
        return pl.BlockSpec((tc, D_CONV), lambda b, i: (b * nchunk + i, col))

    def halo(col):
        return pl.BlockSpec((HALO, D_CONV), lambda b, i: (jnp.maximum((b * nchunk + i) * hb - 1, 0), col))

    vec = pl.BlockSpec((1, D_CONV), lambda b, i: (0, 0))
    out = pl.BlockSpec((tc, D_CONV), lambda b, i: (b * nchunk + i, 0))
    return pl.pallas_call(
        body, grid=(B, nchunk),
        in_specs=[cur(3), halo(3), cur(4), halo(4), pl.BlockSpec((CONV_K, D_CONV), lambda b, i: (0, 0)), vec, vec, vec],
        out_specs=[out, out, out],
        out_shape=[jax.ShapeDtypeStruct((B * S, D_CONV), BF16), jax.ShapeDtypeStruct((B * S, D_CONV), F32),
                   jax.ShapeDtypeStruct((B * S, D_CONV), F32)],
        scratch_shapes=[pltpu.VMEM((tc + HALO, D_CONV), F32), pltpu.VMEM((SUBLANES, tc + HALO, D_CONV), F32)],
        compiler_params=_params("arbitrary", "arbitrary"), name=name)(u, u, u, u, cw, cb, lg, lb)


def _conv_bwd_norm(dcv, y, lg, lb, *, tc, name):
    T = y.shape[0]

    def body(dcv_ref, y_ref, lg_ref, lb_ref, dy_ref, part_ref):
        yv = y_ref[...]
        mu = jnp.mean(yv, axis=-1, keepdims=True)
        yc = yv - mu
        var = jnp.mean(yc * yc, axis=-1, keepdims=True)
        rstd = lax.rsqrt(var + EPS)
        xhat = yc * rstd
        z = xhat * lg_ref[...] + lb_ref[...]
        sig = _sigmoid(z)
        dz = dcv_ref[...] * (sig * (1.0 + z * (1.0 - sig)))
        dxh = dz * lg_ref[...]
        dy = rstd * (dxh - jnp.mean(dxh, axis=-1, keepdims=True)
                     - xhat * jnp.mean(dxh * xhat, axis=-1, keepdims=True))
        dy_ref[...] = dy

        @pl.when(pl.program_id(0) == 0)
        def _():
            part_ref[...] = jnp.zeros_like(part_ref)

        part_ref[0:1, :] += jnp.sum(dz * xhat, axis=0, keepdims=True)
        part_ref[1:2, :] += jnp.sum(dz, axis=0, keepdims=True)
        part_ref[2:3, :] += jnp.sum(dy, axis=0, keepdims=True)

    tok = pl.BlockSpec((tc, D_CONV), lambda i: (i, 0))
    vec = pl.BlockSpec((1, D_CONV), lambda i: (0, 0))
    return pl.pallas_call(
        body, grid=(T // tc,), in_specs=[tok, tok, vec, vec],
        out_specs=[tok, pl.BlockSpec((8, D_CONV), lambda i: (0, 0))],
        out_shape=[jax.ShapeDtypeStruct((T, D_CONV), F32), jax.ShapeDtypeStruct((8, D_CONV), F32)],
        compiler_params=_params("arbitrary"), name=name)(dcv, y, lg, lb)


def _conv_bwd_taps(dy, glu, u, cw, B, S, *, tc, name, phase=None):
    nchunk = S // tc
    hb = tc // HALO
    last_hb = B * S // HALO - 1

    def body(dy_ref, dyn_ref, glu_ref, glup_ref, ca_ref, cg_ref, w_ref, dca_ref, dcg_ref, dw_ref,
             dyp_ref, glp_ref, acc_ref, shd_ref, shg_ref):
        b = pl.program_id(0)
        i = pl.program_id(1)
        dy = dy_ref[...]
        dyp_ref[0:tc, :] = dy
        dyp_ref[tc:, :] = jnp.where(i < nchunk - 1, dyn_ref[...], 0.0)
        glp_ref[0:HALO, :] = jnp.where(i > 0, glup_ref[...], 0.0)
        glp_ref[HALO:, :] = glu_ref[...]
        _shifted_copies(dyp_ref, shd_ref, tc)
        _shifted_copies(glp_ref, shg_ref, tc)

        @pl.when((b == 0) & (i == 0))
        def _():
            dw_ref[...] = jnp.zeros_like(dw_ref)

        for sub in range(tc // SUB):
            acc = jnp.zeros((SUB, D_CONV), F32)
            for k in range(CONV_K):
                acc = acc + _shifted(shd_ref, sub * SUB + (CONV_K - 1) - k, SUB) * w_ref[pl.ds(k, 1), :]
            acc_ref[sub * SUB:(sub + 1) * SUB, :] = acc
        for k in range(CONV_K):
            dw_ref[k:k + 1, :] += jnp.sum(dy * _shifted(shg_ref, HALO - (CONV_K - 1) + k, tc), axis=0, keepdims=True)
        dglu = acc_ref[...]
        ca = ca_ref[...]
        sig = _sigmoid(cg_ref[...])
        dca_ref[...] = (dglu * sig).astype(BF16)
        dcg_ref[...] = (dglu * ca * sig * (1.0 - sig)).astype(BF16)

    tok = pl.BlockSpec((tc, D_CONV), lambda b, i: (b * nchunk + i, 0))
    nxt = pl.BlockSpec((HALO, D_CONV), lambda b, i: (jnp.minimum((b * nchunk + i + 1) * hb, last_hb), 0))
    prv = pl.BlockSpec((HALO, D_CONV), lambda b, i: (jnp.maximum((b * nchunk + i) * hb - 1, 0), 0))
    return _call(
        body, grid=(B, nchunk),
        in_specs=[tok, nxt, tok, prv,
                  pl.BlockSpec((tc, D_CONV), lambda b, i: (b * nchunk + i, 3)),
                  pl.BlockSpec((tc, D_CONV), lambda b, i: (b * nchunk + i, 4)),
                  pl.BlockSpec((CONV_K, D_CONV), lambda b, i: (0, 0))],
        out_specs=[tok, tok, pl.BlockSpec((32, D_CONV), lambda b, i: (0, 0))],
        out_shape=[jax.ShapeDtypeStruct((B * S, D_CONV), BF16), jax.ShapeDtypeStruct((B * S, D_CONV), BF16),
                   jax.ShapeDtypeStruct((32, D_CONV), F32)],
        scratch_shapes=[pltpu.VMEM((tc + HALO, D_CONV), F32), pltpu.VMEM((tc + HALO, D_CONV), F32),
                        pltpu.VMEM((tc, D_CONV), F32), pltpu.VMEM((SUBLANES, tc + HALO, D_CONV), F32),
                        pltpu.VMEM((SUBLANES, tc + HALO, D_CONV), F32)],
        sem=("arbitrary", "arbitrary"), name=name, args=(dy, dy, glu, glu, u, u, cw), phase=phase)


def _outproj_fwd(h, attn, cv, wout, *, tm, name):
    T, D = h.shape

    def body(h_ref, a_ref, c_ref, w_ref, o_ref):
        o_ref[...] = (h_ref[...] + _dot(a_ref[...].astype(BF16), w_ref[0:D_ATTN, :])
                      + _dot(c_ref[...], w_ref[D_ATTN:, :]))

    tok = pl.BlockSpec((tm, D), lambda i: (i, 0))
    half = pl.BlockSpec((tm, D_ATTN), lambda i: (i, 0))
    return pl.pallas_call(
        body, grid=(T // tm,), in_specs=[tok, half, half, pl.BlockSpec(wout.shape, lambda i: (0, 0))],
        out_specs=tok, out_shape=jax.ShapeDtypeStruct((T, D), F32),
        compiler_params=_params("arbitrary"), name=name)(h, attn, cv, wout)


def _outproj_bwd(dh, attn, cv, wout, *, tm, name):
    T, D = dh.shape

    def body(dh_ref, a_ref, c_ref, w_ref, da_ref, dc_ref, dw_ref):
        @pl.when(pl.program_id(0) == 0)
        def _():
            dw_ref[...] = jnp.zeros_like(dw_ref)

        dhb = dh_ref[...].astype(BF16)
        da_ref[...] = _dot_nt(dhb, w_ref[0:D_ATTN, :])
        dc_ref[...] = _dot_nt(dhb, w_ref[D_ATTN:, :])
        dw_ref[0:D_ATTN, :] += _dot_tn(a_ref[...].astype(BF16), dhb)
        dw_ref[D_ATTN:, :] += _dot_tn(c_ref[...], dhb)

    tok = pl.BlockSpec((tm, D), lambda i: (i, 0))
    half = pl.BlockSpec((tm, D_ATTN), lambda i: (i, 0))
    wspec = pl.BlockSpec(wout.shape, lambda i: (0, 0))
    return pl.pallas_call(
        body, grid=(T // tm,), in_specs=[tok, half, half, wspec], out_specs=[half, half, wspec],
        out_shape=[jax.ShapeDtypeStruct((T, D_ATTN), F32), jax.ShapeDtypeStruct((T, D_ATTN), F32),
                   jax.ShapeDtypeStruct(wout.shape, F32)],
        compiler_params=_params("arbitrary"), name=name)(dh, attn, cv, wout)


ADAM_BLOCK_BYTES = 3 * 512 * 1024


def _adamw(w, g, m, v, *, name):
    R, C = w.shape
    tr = R
    for cand in (512, 352, 256, 176, 128, 64, 32, 16, 8):
        if R % cand == 0 and cand * C * 4 <= ADAM_BLOCK_BYTES:
            tr = cand
            break
    c1 = 1.0 - ADAM_B1 ** ADAM_STEP
    c2 = 1.0 - ADAM_B2 ** ADAM_STEP

    def body(w_ref, g_ref, m_ref, v_ref, d_ref, nm_ref, nv_ref):
        gv = g_ref[...]
        nm = ADAM_B1 * m_ref[...] + (1.0 - ADAM_B1) * gv
        nv = ADAM_B2 * v_ref[...] + (1.0 - ADAM_B2) * (gv * gv)
        d_ref[...] = -ADAM_LR * ((nm / c1) / (jnp.sqrt(nv / c2) + ADAM_EPS) + ADAM_WD * w_ref[...])
        nm_ref[...] = nm
        nv_ref[...] = nv

    blk = pl.BlockSpec((tr, C), lambda i: (i, 0))
    return pl.pallas_call(
        body, grid=(R // tr,), in_specs=[blk] * 4, out_specs=[blk] * 3,
        out_shape=[jax.ShapeDtypeStruct((R, C), F32)] * 3,
        compiler_params=_params("arbitrary"), name=name)(w, g, m, v)


TM = 512
TM_WIDE = 1024
TK = 1024
TC = 256


def _local_step(x, tgt, w, overlap=None):
    B, S, D = x.shape
    T = B * S
    x2 = x.reshape(T, D)
    t2 = tgt.reshape(T, D)
    ones = jnp.ones((1, D_ATTN), F32)
    scale = HEAD_DIM ** -0.5
    gains = jnp.stack([jnp.tile(w["q_norm"], (1, HEADS)) * scale, jnp.tile(w["k_norm"], (1, HEADS)), ones])
    g = {}

    def hosting(point, build):
        phase = overlap.phase(point, w, g) if overlap is not None else None
        if phase is None:
            return build(None)
        outs, extra = build(phase)
        overlap.done(point, extra, w, g)
        return outs

    h1, n1, G1, U1 = hosting("ffn1_fwd", lambda ph: _ffn_fwd(
        x2, w["ffn1_norm"], w["wg1"], w["wu1"], w["wd1"], None, tm=TM_WIDE, name="ffn1_fwd", phase=ph))
    u, n2 = hosting("inproj_fwd", lambda ph: _inproj_fwd(h1, w["mix_norm"], w["win"], tm=TM_WIDE, name="inproj_fwd", phase=ph))
    qkv = _qkv_prep(u, gains, B, S, name="qkv_prep")
    qkv = qkv.reshape(3, N_PATTERNS, T // QBLK, QBLK, D_ATTN)
    o3, lse3 = _attn_fwd(qkv, name="attn_fwd")
    attn, lse = _attn_combine(o3, lse3, B, S, name="attn_combine")
    cv, glu, yconv = _conv_fwd(u, w["conv_w"], w["conv_b"], w["conv_ln_g"], w["conv_ln_b"], B, S, tc=TC, name="conv_fwd")
    h2 = _outproj_fwd(h1, attn, cv, w["wout"], tm=TM, name="outproj_fwd")
    dh3, n3, G2, U2, loss = _ffn_fwd(h2, w["ffn2_norm"], w["wg2"], w["wu2"], w["wd2"], t2, tm=TM_WIDE, name="ffn2_fwd")

    dG, dU, A, dy, dh2, g["ffn2_norm"] = _ffn_bwd_act(dh3, h2, w["ffn2_norm"], G2, U2, w["wg2"], w["wu2"], w["wd2"],
                                                    tm=TM, name="ffn2_bwd_act")
    g["wg2"], g["wu2"], g["wd2"] = _ffn_bwd_w(n3, dy, dG, dU, A, tk=TK, name="ffn2_bwd_w")
    dattn, dcv, g["wout"] = _outproj_bwd(dh2, attn, cv, w["wout"], tm=TM, name="outproj_bwd")
    dyc, cpart = _conv_bwd_norm(dcv, yconv, w["conv_ln_g"], w["conv_ln_b"], tc=TC, name="conv_bwd_norm")
    dca, dcg, dcw = hosting("conv_bwd_taps", lambda ph: _conv_bwd_taps(
        dyc, glu, u, w["conv_w"], B, S, tc=TC, name="conv_bwd_taps", phase=ph))
    do3, st3 = _attn_bwd_prep(dattn, attn, lse, B, S, name="attn_bwd_prep")
    nb = T // QBLK
    (cur,) = hosting("attn_bwd", lambda ph: _attn_bwd(
        qkv, do3.reshape(N_PATTERNS, nb, QBLK, D_ATTN), st3.reshape(N_PATTERNS, nb, QBLK, LANES),
        name="attn_bwd", phase=ph))
    du_qkv, dgains = hosting("attn_grad_combine", lambda ph: _attn_grad_combine(
        cur, u, gains, B, S, name="attn_grad_combine", phase=ph))
    du = jnp.concatenate([du_qkv, dca, dcg], axis=1)
    dh1, g["mix_norm"] = hosting("inproj_bwd_act", lambda ph: _inproj_bwd_act(
        du, dh2, h1, w["mix_norm"], w["win"], tm=TM_WIDE, name="inproj_bwd_act", phase=ph))
    g["win"] = _inproj_bwd_w(n2, du, w["win"].shape[0], tk=TK, name="inproj_bwd_w")
    dG, dU, A, dy, dx, g["ffn1_norm"] = _ffn_bwd_act(dh1, x2, w["ffn1_norm"], G1, U1, w["wg1"], w["wu1"], w["wd1"],
                                                   tm=TM, name="ffn1_bwd_act")
    g["wg1"], g["wu1"], g["wd1"] = _ffn_bwd_w(n1, dy, dG, dU, A, tk=TK, name="ffn1_bwd_w")

    g["q_norm"] = dgains[0].reshape(HEADS, HEAD_DIM).sum(axis=0, keepdims=True) * scale
    g["k_norm"] = dgains[1].reshape(HEADS, HEAD_DIM).sum(axis=0, keepdims=True)
    g["conv_ln_g"] = cpart[0:1]
    g["conv_ln_b"] = cpart[1:2]
    g["conv_b"] = cpart[2:3]
    g["conv_w"] = dcw[:CONV_K]
    return loss, dx.reshape(B, S, D), g


N_CHIPS = 4
N_DEV = 8
VMEM_SPEC = pl.BlockSpec(memory_space=pltpu.VMEM)


def _remote(src, dst, send_sem, recv_sem, device):
    return pltpu.make_async_remote_copy(src_ref=src, dst_ref=dst, send_sem=send_sem, recv_sem=recv_sem,
                                        device_id=device, device_id_type=MESH)


def _stage_shards(shards, dtypes, *, name):
    n = len(shards)
    halves = [s.reshape(2, s.shape[0] // 2, s.shape[1]) for s in shards]

    def body(*refs):
        ins, outs, vms, loc_sems = refs[:n], refs[n:2 * n], refs[2 * n:3 * n], refs[3 * n]
        me = 2 * lax.axis_index("x") + lax.axis_index("y")
        copies = []
        for a in range(n):
            vms[a][...] = ins[a][...].astype(dtypes[a])
            cp = pltpu.make_async_copy(vms[a], outs[a].at[me], loc_sems.at[a])
            cp.start()
            copies.append(cp)
        for cp in copies:
            cp.wait()

    return pl.pallas_call(
        body, in_specs=[VMEM_SPEC] * n, out_specs=[ANY] * n,
        out_shape=[jax.ShapeDtypeStruct((N_CHIPS,) + h.shape, dt) for h, dt in zip(halves, dtypes)],
        scratch_shapes=[pltpu.VMEM(h.shape, dt) for h, dt in zip(halves, dtypes)] + [DMA_SEMS((n,))],
        compiler_params=pltpu.CompilerParams(vmem_limit_bytes=VMEM_LIMIT), name=name)(*halves)


def _like(arrays):
    return [jax.ShapeDtypeStruct(a.shape, a.dtype) for a in arrays]


def _axes():
    x, y, c = lax.axis_index("x"), lax.axis_index("y"), lax.axis_index("c")
    first = (x + (1 - c) * (1 - 2 * x), y + c * (1 - 2 * y))
    second = (x + c * (1 - 2 * x), y + (1 - c) * (1 - 2 * y))
    slots = tuple(2 * px + py for px, py in ((x, y), first, second, (1 - x, 1 - y)))
    return (x, y, c), (*first, c), (*second, c), slots


def _gather_ici_phase(bufs):
    n = len(bufs)

    def stage1(ins, outs, sems):
        (x, y, c), peer1, peer2, (own, s1, s2, both) = _axes()
        starts, arrivals = [], []
        for a in range(n):
            mine, land = outs[a].at[own, c], outs[a].at[s2, c]
            starts.append(_remote(mine, mine, *sems(a), peer2))
            arrivals.append(_remote(land, land, *sems(a), peer2))
        return starts, arrivals

    def stage2(ins, outs, sems):
        (x, y, c), peer1, peer2, (own, s1, s2, both) = _axes()
        starts, arrivals = [], []
        for a in range(n):
            for k, (src, dst) in enumerate(((own, s1), (s2, both))):
                mine, land = outs[a].at[src, c], outs[a].at[dst, c]
                starts.append(_remote(mine, mine, *sems(2 * a + k), peer1))
                arrivals.append(_remote(land, land, *sems(2 * a + k), peer1))
        return starts, arrivals

    same = {a: a for a in range(n)}
    return _Phase(bufs, _like(bufs), same, n, stage1).then(_Phase(bufs, _like(bufs), same, 2 * n, stage2))


def _gather_d2d_phase(bufs):
    n = len(bufs)

    def copies(ins, outs, sems):
        (x, y, c), peer1, peer2, (own, s1, s2, both) = _axes()
        starts, arrivals = [], []
        for a in range(n):
            for j, s in enumerate((s1, s2, both)):
                got, land = outs[a].at[s, c], outs[a].at[s, 1 - c]
                starts.append(_remote(got, got, *sems(3 * a + j), (x, y, 1 - c)))
                arrivals.append(_remote(land, land, *sems(3 * a + j), (x, y, 1 - c)))
        return starts, arrivals

    return _Phase(bufs, _like(bufs), {a: a for a in range(n)}, 3 * n, copies)


def _exchange_phase(views):
    n = len(views)

    def copies(ins, outs, sems):
        x, y, c = lax.axis_index("x"), lax.axis_index("y"), lax.axis_index("c")
        starts = [_remote(ins[a].at[pl.ds(0, ins[a].shape[0]), 1 - c], outs[a], *sems(a), (x, y, 1 - c))
                  for a in range(n)]
        return starts, starts

    outs = [jax.ShapeDtypeStruct((v.shape[0],) + v.shape[2:], F32) for v in views]
    return _Phase(views, outs, {}, n, copies)


def _row_block(rows):
    for cand in (256, 176, 128, 64, 32, 16, 8):
        if rows % cand == 0:
            return cand
    return rows


def _add_own_half(g, r, sel, *, name):
    ns, _, rh, cdim = g.shape
    tr = _row_block(rh)

    def body(s_ref, gk_ref, rk_ref, gs_ref, rs_ref, keep_ref, send_ref):
        keep_ref[...] = gk_ref[...] + rk_ref[...]
        send_ref[...] = (gs_ref[...] + rs_ref[...]).astype(BF16)

    def g_spec(off):
        return pl.BlockSpec((None, None, tr, cdim), lambda k, i, s: (s[1 + off + k], s[0], i, 0))

    def r_spec(off):
        return pl.BlockSpec((None, tr, cdim), lambda k, i, s: (s[1 + off + k], i, 0))

    out = pl.BlockSpec((None, tr, cdim), lambda k, i, s: (k, i, 0))
    return pl.pallas_call(
        body,
        grid_spec=pltpu.PrefetchScalarGridSpec(
            num_scalar_prefetch=1, grid=(2, rh // tr),
            in_specs=[g_spec(0), r_spec(0), g_spec(2), r_spec(2)], out_specs=[out, out]),
        out_shape=[jax.ShapeDtypeStruct((2, rh, cdim), F32), jax.ShapeDtypeStruct((2, rh, cdim), BF16)],
        compiler_params=_params("arbitrary", "arbitrary"), name=name)(sel, g, r, g, r)


def _swap_phase(arrays, stage):
    n = len(arrays)

    def copies(ins, outs, sems):
        peer = _axes()[stage]
        starts = [_remote(ins[a], outs[a], *sems(a), peer) for a in range(n)]
        return starts, starts

    return _Phase(arrays, _like(arrays), {}, n, copies)


def _add_stage1(keep, got, *, name):
    _, rh, cdim = keep.shape
    tr = _row_block(rh)

    def body(k_ref, g_ref, keep_ref, send_ref):
        keep_ref[...] = k_ref[0] + g_ref[0].astype(F32)
        send_ref[...] = (k_ref[1] + g_ref[1].astype(F32)).astype(BF16)

    blk2 = pl.BlockSpec((2, tr, cdim), lambda i: (0, i, 0))
    blk = pl.BlockSpec((tr, cdim), lambda i: (i, 0))
    return pl.pallas_call(
        body, grid=(rh // tr,), in_specs=[blk2, blk2], out_specs=[blk, blk],
        out_shape=[jax.ShapeDtypeStruct((rh, cdim), F32), jax.ShapeDtypeStruct((rh, cdim), BF16)],
        compiler_params=_params("arbitrary"), name=name)(keep, got)


def _add_stage2(keep, got, sel, *, name):
    rh, cdim = keep.shape
    tr = _row_block(rh)

    def body(s_ref, k_ref, g_ref, o_ref):
        o_ref[...] = k_ref[...] + g_ref[...].astype(F32)

    blk = pl.BlockSpec((tr, cdim), lambda i, s: (i, 0))
    return pl.pallas_call(
        body,
        grid_spec=pltpu.PrefetchScalarGridSpec(
            num_scalar_prefetch=1, grid=(rh // tr,), in_specs=[blk, blk],
            out_specs=pl.BlockSpec((None, tr, cdim), lambda i, s: (s[0], i, 0))),
        out_shape=jax.ShapeDtypeStruct((2, rh, cdim), F32),
        compiler_params=_params("arbitrary"), name=name)(sel, keep, got)


def _join_phase(halves):
    n = len(halves)

    def copies(ins, outs, sems):
        x, y, c = lax.axis_index("x"), lax.axis_index("y"), lax.axis_index("c")
        starts, arrivals = [], []
        for a in range(n):
            mine, land = outs[a].at[c], outs[a].at[1 - c]
            starts.append(_remote(mine, mine, *sems(a), (x, y, 1 - c)))
            arrivals.append(_remote(land, land, *sems(a), (x, y, 1 - c)))
        return starts, arrivals

    return _Phase(halves, _like(halves), {a: a for a in range(n)}, n, copies)


def _slot_order():
    x, y, c = lax.axis_index("x"), lax.axis_index("y"), lax.axis_index("c")
    own, flip_x, flip_y, both = 2 * x + y, 2 * (1 - x) + y, 2 * x + 1 - y, 2 * (1 - x) + 1 - y
    first = jnp.where(c == 0, flip_x, flip_y)
    second = jnp.where(c == 0, flip_y, flip_x)
    return jnp.stack([c, own, second, first, both]).astype(jnp.int32)


def _reduce_scatter(grads):
    sel = _slot_order()
    views = [_half_view(g) for g in grads]
    got = _run_phase(_exchange_phase(views), name="rs_exchange_halves")
    keep, send = _add_halves(views, got, sel, "late")
    got = _run_phase(_swap_phase(send, 1), name="rs_swap_first_axis")
    keep, send = _add_first(keep, got, "late")
    got = _run_phase(_swap_phase(send, 2), name="rs_swap_second_axis")
    halves = _add_second(keep, got, sel, "late")
    full = _run_phase(_join_phase(halves), name="rs_join_halves")
    return [f.reshape(g.shape[1], g.shape[2]) for f, g in zip(full, grads)]


def _half_view(g):
    return g.reshape(N_CHIPS, 2, g.shape[1] // 2, g.shape[2])


def _add_halves(views, got, sel, tag):
    keep, send = zip(*[_add_own_half(v, r, sel, name=f"rs_add_half_{tag}{a}") for a, (v, r) in enumerate(zip(views, got))])
    return list(keep), list(send)


def _add_first(keep, got, tag):
    keep, send = zip(*[_add_stage1(k, r, name=f"rs_add_first_{tag}{a}") for a, (k, r) in enumerate(zip(keep, got))])
    return list(keep), list(send)


def _add_second(keep, got, sel, tag):
    return [_add_stage2(k, r, sel, name=f"rs_add_second_{tag}{a}") for a, (k, r) in enumerate(zip(keep, got))]


EARLY_GRADS = ("wg2", "wu2", "wd2", "wout")


class _Overlap:
    def __init__(self, staged):
        self.staged = staged
        self.sel = _slot_order()
        self.reduced = {}

    def phase(self, point, w, g):
        if point == "ffn1_fwd":
            return _gather_ici_phase(self.staged)
        if point == "inproj_fwd":
            return _gather_d2d_phase(self.ffn2)
        if point == "conv_bwd_taps":
            self.shapes = [g[k].shape for k in EARLY_GRADS]
            self.views = [_half_view(g[k].reshape(N_CHIPS, -1, g[k].shape[-1])) for k in EARLY_GRADS]
            return _exchange_phase(self.views)
        if point == "attn_bwd":
            return _swap_phase(self.send, 1)
        if point == "attn_grad_combine":
            return _swap_phase(self.send, 2)
        if point == "inproj_bwd_act":
            return _join_phase(self.halves)
        return None

    def done(self, point, outs, w, g):
        if point == "ffn1_fwd":
            win, wout, taps = [_whole(b) for b in _run_phase(_gather_d2d_phase(outs[:3]), name="gather_mix_d2d")]
            w["win"] = win
            w["wout"] = wout.reshape(-1, wout.shape[-1])
            w["conv_w"] = taps.transpose(1, 0, 2).reshape(CONV_K + 1, D_CONV)[:CONV_K]
            self.ffn2 = list(outs[3:])
        elif point == "inproj_fwd":
            w["wg2"], w["wu2"], w["wd2"] = [_whole(b) for b in outs]
        elif point == "conv_bwd_taps":
            self.keep, self.send = _add_halves(self.views, outs, self.sel, "early")
        elif point == "attn_bwd":
            self.keep, self.send = _add_first(self.keep, outs, "early")
        elif point == "attn_grad_combine":
            self.halves = _add_second(self.keep, outs, self.sel, "early")
        elif point == "inproj_bwd_act":
            for k, shp, f in zip(EARLY_GRADS, self.shapes, outs):
                self.reduced[k] = f.reshape(-1, shp[-1])


def _whole(buf):
    return buf.reshape(buf.shape[0], 2 * buf.shape[2], buf.shape[3])


def _allreduce_small(pack, *, name):
    rows = pack.shape[0]

    def body(p_ref, o_ref, buf_ref, send_sems, recv_sems):
        x, y, c = lax.axis_index("x"), lax.axis_index("y"), lax.axis_index("c")
        me = 4 * x + 2 * y + c
        buf_ref[me] = p_ref[...]
        cps = []
        for k in range(1, N_DEV):
            peer = tuple(1 - v if (k >> s) & 1 else v for v, s in ((x, 2), (y, 1), (c, 0)))
            cp = _remote(p_ref, buf_ref.at[me], send_sems.at[k - 1], recv_sems.at[k - 1], peer)
            cp.start()
            cps.append(cp)
        for k in range(1, N_DEV):
            src = 4 * (x ^ ((k >> 2) & 1)) + 2 * (y ^ ((k >> 1) & 1)) + (c ^ (k & 1))
            land = buf_ref.at[src]
            _remote(land, land, send_sems.at[k - 1], recv_sems.at[k - 1], (x, y, c)).wait_recv()
        acc = buf_ref[0]
        for d in range(1, N_DEV):
            acc = acc + buf_ref[d]
        o_ref[...] = acc
        for cp in cps:
            cp.wait_send()

    return pl.pallas_call(
        body, in_specs=[VMEM_SPEC], out_specs=VMEM_SPEC, out_shape=jax.ShapeDtypeStruct(pack.shape, F32),
        scratch_shapes=[pltpu.VMEM((N_DEV, rows, LANES), F32), pltpu.SemaphoreType.DMA((N_DEV - 1,)),
                        pltpu.SemaphoreType.DMA((N_DEV - 1,))], name=name)(pack)


SMALL = ("ffn1_norm", "mix_norm", "q_norm", "k_norm", "conv_b", "conv_ln_g", "conv_ln_b", "ffn2_norm", "conv_w")
BIG = ("ffn1_w_gate", "ffn1_w_up", "ffn1_w_down", "w_in", "w_out", "ffn2_w_gate", "ffn2_w_up", "ffn2_w_down")
TRANSPOSED = ("ffn1_w_gate", "ffn1_w_up", "ffn2_w_gate", "ffn2_w_up")
WEIGHTS = ("ffn1_norm", "ffn1_w_gate", "ffn1_w_up", "ffn1_w_down", "mix_norm", "w_in", "q_norm", "k_norm",
           "conv_w", "conv_b", "conv_ln_g", "conv_ln_b", "w_out", "ffn2_norm", "ffn2_w_gate", "ffn2_w_up",
           "ffn2_w_down")


def _pack(parts):
    rows = []
    for p in parts:
        flat = p.reshape(-1)
        tile = SUBLANES * LANES
        padded = -(-flat.shape[0] // tile) * tile
        rows.append(jnp.pad(flat, (0, padded - flat.shape[0])).reshape(-1, LANES))
    return jnp.concatenate(rows, axis=0)


def _unpack(pack, shapes):
    out, row = [], 0
    for shp in shapes:
        size = shp[0] * shp[1]
        tile = SUBLANES * LANES
        nrows = -(-size // tile) * SUBLANES
        out.append(pack[row:row + nrows].reshape(-1)[:size].reshape(shp))
        row += nrows
    return out


def kernel(x, ffn1_norm, ffn1_w_gate, ffn1_w_up, ffn1_w_down, mix_norm, w_in, q_norm, k_norm, conv_w, conv_b, conv_ln_g, conv_ln_b, w_out, ffn2_norm, ffn2_w_gate, ffn2_w_up, ffn2_w_down, loss_target, m_ffn1_norm, m_ffn1_w_gate, m_ffn1_w_up, m_ffn1_w_down, m_mix_norm, m_w_in, m_q_norm, m_k_norm, m_conv_w, m_conv_b, m_conv_ln_g, m_conv_ln_b, m_w_out, m_ffn2_norm, m_ffn2_w_gate, m_ffn2_w_up, m_ffn2_w_down, v_ffn1_norm, v_ffn1_w_gate, v_ffn1_w_up, v_ffn1_w_down, v_mix_norm, v_w_in, v_q_norm, v_k_norm, v_conv_w, v_conv_b, v_conv_ln_g, v_conv_ln_b, v_w_out, v_ffn2_norm, v_ffn2_w_gate, v_ffn2_w_up, v_ffn2_w_down):
    wts = dict(ffn1_norm=ffn1_norm, ffn1_w_gate=ffn1_w_gate[0], ffn1_w_up=ffn1_w_up[0], ffn1_w_down=ffn1_w_down[0],
               mix_norm=mix_norm, w_in=w_in[0], q_norm=q_norm, k_norm=k_norm, conv_w=conv_w[0], conv_b=conv_b,
               conv_ln_g=conv_ln_g, conv_ln_b=conv_ln_b, w_out=w_out[0], ffn2_norm=ffn2_norm,
               ffn2_w_gate=ffn2_w_gate[0], ffn2_w_up=ffn2_w_up[0], ffn2_w_down=ffn2_w_down[0])
    mom = dict(ffn1_norm=m_ffn1_norm, ffn1_w_gate=m_ffn1_w_gate[0], ffn1_w_up=m_ffn1_w_up[0], ffn1_w_down=m_ffn1_w_down[0],
               mix_norm=m_mix_norm, w_in=m_w_in[0], q_norm=m_q_norm, k_norm=m_k_norm, conv_w=m_conv_w[0], conv_b=m_conv_b,
               conv_ln_g=m_conv_ln_g, conv_ln_b=m_conv_ln_b, w_out=m_w_out[0], ffn2_norm=m_ffn2_norm,
               ffn2_w_gate=m_ffn2_w_gate[0], ffn2_w_up=m_ffn2_w_up[0], ffn2_w_down=m_ffn2_w_down[0])
    var = dict(ffn1_norm=v_ffn1_norm, ffn1_w_gate=v_ffn1_w_gate[0], ffn1_w_up=v_ffn1_w_up[0], ffn1_w_down=v_ffn1_w_down[0],
               mix_norm=v_mix_norm, w_in=v_w_in[0], q_norm=v_q_norm, k_norm=v_k_norm, conv_w=v_conv_w[0], conv_b=v_conv_b,
               conv_ln_g=v_conv_ln_g, conv_ln_b=v_conv_ln_b, w_out=v_w_out[0], ffn2_norm=v_ffn2_norm,
               ffn2_w_gate=v_ffn2_w_gate[0], ffn2_w_up=v_ffn2_w_up[0], ffn2_w_down=v_ffn2_w_down[0])
    chip = 2 * lax.axis_index("x") + lax.axis_index("y")
    for src in (wts, mom, var):
        for n in TRANSPOSED:
            src[n] = src[n].T

    taps = jnp.pad(wts["conv_w"], ((0, 1), (0, 0)))
    staged = _stage_shards([wts["ffn1_w_gate"], wts["ffn1_w_up"], wts["ffn1_w_down"], wts["w_in"], wts["w_out"], taps,
                            wts["ffn2_w_gate"], wts["ffn2_w_up"], wts["ffn2_w_down"]],
                           [BF16, BF16, BF16, BF16, BF16, F32, BF16, BF16, BF16], name="stage_shards")
    first = _run_phase(_gather_ici_phase(staged[:3]).then(_gather_d2d_phase(staged[:3])), name="gather_ffn1")
    wg1, wu1, wd1 = [_whole(b) for b in first]
    w = dict(ffn1_norm=ffn1_norm, mix_norm=mix_norm, ffn2_norm=ffn2_norm, q_norm=q_norm, k_norm=k_norm,
             conv_b=conv_b, conv_ln_g=conv_ln_g, conv_ln_b=conv_ln_b, wg1=wg1, wu1=wu1, wd1=wd1)
    overlap = _Overlap(staged[3:])
    loss_part, grad_x, g = _local_step(x, loss_target, w, overlap)

    late = _reduce_scatter([g["wg1"], g["wu1"], g["wd1"], g["win"]])
    big_grads = dict(zip(("ffn1_w_gate", "ffn1_w_up", "ffn1_w_down", "w_in"), late))
    big_grads.update(ffn2_w_gate=overlap.reduced["wg2"], ffn2_w_up=overlap.reduced["wu2"],
                     ffn2_w_down=overlap.reduced["wd2"], w_out=overlap.reduced["wout"])

    small_shapes = [g[n].shape for n in SMALL] + [(SUBLANES, LANES)]
    red = _allreduce_small(_pack([g[n] for n in SMALL] + [loss_part]), name="allreduce_small")
    small = dict(zip(SMALL + ("loss",), _unpack(red, small_shapes)))
    loss = small["loss"][0, 0]
    small["conv_w"] = lax.dynamic_slice_in_dim(small["conv_w"], chip * LANES, LANES, axis=1)

    grads, delta, new_m, new_v = {}, {}, {}, {}
    for n in BIG:
        grads[n] = big_grads[n]
        delta[n], new_m[n], new_v[n] = _adamw(wts[n], grads[n], mom[n], var[n], name=f"adamw_{n}")
    shapes = [wts[n].shape for n in SMALL]
    packs = [_pack([src[n] for n in SMALL]) for src in (wts, small, mom, var)]
    outs = _adamw(*packs, name="adamw_small")
    for dst, pk in zip((delta, new_m, new_v), outs):
        dst.update(zip(SMALL, _unpack(pk, shapes)))
    for n in SMALL:
        grads[n] = small[n]

    def shaped(d, n):
        v = d[n].T if n in TRANSPOSED else d[n]
        return v.reshape((1,) + v.shape) if n in BIG or n == "conv_w" else v

    return (loss, grad_x, *[shaped(grads, n) for n in WEIGHTS], *[shaped(delta, n) for n in WEIGHTS],
            *[shaped(new_m, n) for n in WEIGHTS], *[shaped(new_v, n) for n in WEIGHTS])
```

```python
import functools

import jax
import jax.numpy as jnp
from jax import lax
from jax.experimental import pallas as pl
from jax.experimental.pallas import tpu as pltpu

F32 = jnp.float32
BF16 = jnp.bfloat16

EPS = 1e-6
HEADS = 8
HEAD_DIM = 64
D_ATTN = HEADS * HEAD_DIM
D_CONV = 512
CONV_K = 31
QBLK = 128
N_PATTERNS = 3
DILATIONS = (1, 4, 16)
LANES = 128
NEG = -1e30

ADAM_LR = 0.001
ADAM_B1 = 0.9
ADAM_B2 = 0.999
ADAM_EPS = 1e-08
ADAM_WD = 0.01
ADAM_STEP = 10

VMEM_LIMIT = 56 * 1024 * 1024
MESH = pl.DeviceIdType.MESH

NT_DIMS = (((1,), (1,)), ((), ()))
TN_DIMS = (((0,), (0,)), ((), ()))


def _params(*sem):
    return pltpu.CompilerParams(dimension_semantics=sem, vmem_limit_bytes=VMEM_LIMIT)


def _dot(a, b):
    return jnp.dot(a, b, preferred_element_type=F32)


def _dot_nt(a, b):
    return lax.dot_general(a, b, NT_DIMS, preferred_element_type=F32)


def _dot_tn(a, b):
    return lax.dot_general(a, b, TN_DIMS, preferred_element_type=F32)


def _sigmoid(x):
    return 1.0 / (1.0 + jnp.exp(-x))


def _seg_mean(v, e_ref, width):
    hi = v.astype(BF16)
    lo = (v - hi.astype(F32)).astype(BF16)
    e = e_ref[...]
    return (_dot(hi, e) + _dot(lo, e)) * (1.0 / width)


def _seg_matrix(n):
    i = jnp.arange(n)
    return (i[:, None] // HEAD_DIM == i[None, :] // HEAD_DIM).astype(BF16)


ANY = pl.BlockSpec(memory_space=pl.ANY)
DMA_SEMS = pltpu.SemaphoreType.DMA


class _Phase:
    def __init__(self, ins, outs, aliases, nsem, copies):
        self.ins, self.outs, self.aliases = list(ins), list(outs), dict(aliases)
        self.stages = [(nsem, copies)]

    def then(self, other):
        self.stages = self.stages + other.stages
        return self

    @property
    def nsem(self):
        return sum(n for n, _ in self.stages)

    def _copies(self, k, in_refs, out_refs, send_sems, recv_sems):
        base = sum(n for n, _ in self.stages[:k])
        return self.stages[k][1](in_refs, out_refs, lambda i: (send_sems.at[base + i], recv_sems.at[base + i]))

    def start(self, k, *refs):
        for cp in self._copies(k, *refs)[0]:
            cp.start()

    def finish(self, k, *refs):
        starts, arrivals = self._copies(k, *refs)
        for cp in arrivals:
            cp.wait_recv()
        for cp in starts:
            cp.wait_send()


def _run_phase(phase, *, name):
    n_in, n_out = len(phase.ins), len(phase.outs)

    def body(*refs):
        ins, outs = refs[:n_in], refs[n_in:n_in + n_out]
        send_sems, recv_sems = refs[n_in + n_out:]
        for k in range(len(phase.stages)):
            phase.start(k, ins, outs, send_sems, recv_sems)
            phase.finish(k, ins, outs, send_sems, recv_sems)

    return pl.pallas_call(
        body, in_specs=[ANY] * n_in, out_specs=[ANY] * n_out, out_shape=phase.outs,
        input_output_aliases=phase.aliases,
        scratch_shapes=[DMA_SEMS((phase.nsem,)), DMA_SEMS((phase.nsem,))], name=name)(*phase.ins)


def _call(body, *, grid, in_specs, out_specs, out_shape, scratch_shapes=(), sem, name, args, phase=None):
    in_specs, out_specs, out_shape = list(in_specs), list(out_specs), list(out_shape)
    scratch_shapes = list(scratch_shapes)
    if phase is None:
        return pl.pallas_call(body, grid=grid, in_specs=in_specs, out_specs=out_specs, out_shape=out_shape,
                              scratch_shapes=scratch_shapes, compiler_params=_params(*sem), name=name)(*args)
    n_in, n_out, n_scr = len(in_specs), len(out_specs), len(scratch_shapes)
    p_in, p_out = len(phase.ins), len(phase.outs)

    def hosted(*refs):
        ins, pins = refs[:n_in], refs[n_in:n_in + p_in]
        o0 = n_in + p_in
        outs, pouts = refs[o0:o0 + n_out], refs[o0 + n_out:o0 + n_out + p_out]
        s0 = o0 + n_out + p_out
        scr = refs[s0:s0 + n_scr]
        send_sems, recv_sems = refs[s0 + n_scr:]
        step = 0
        for d, n in enumerate(grid):
            step = step * n + pl.program_id(d)
        nsteps = functools.reduce(lambda a, b: a * b, grid)
        nstages = len(phase.stages)
        comm_refs = (pins, pouts, send_sems, recv_sems)

        for k in range(nstages):
            @pl.when(step == (k * nsteps) // nstages)
            def _(k=k):
                if k > 0:
                    phase.finish(k - 1, *comm_refs)
                phase.start(k, *comm_refs)

        body(*ins, *outs, *scr)

        @pl.when(step == nsteps - 1)
        def _():
            phase.finish(nstages - 1, *comm_refs)

    res = pl.pallas_call(
        hosted, grid=grid, in_specs=in_specs + [ANY] * p_in, out_specs=out_specs + [ANY] * p_out,
        out_shape=out_shape + phase.outs,
        input_output_aliases={n_in + i: n_out + o for i, o in phase.aliases.items()},
        scratch_shapes=scratch_shapes + [DMA_SEMS((phase.nsem,)), DMA_SEMS((phase.nsem,))],
        compiler_params=_params(*sem), name=name)(*args, *phase.ins)
    return res[:n_out], res[n_out:]


ROW_CHUNK = 256


def _ffn_fwd(x, gain, wg, wu, wd, tgt, *, tm, name, phase=None):
    T, D = x.shape
    NS, Fs, _ = wg.shape
    with_loss = tgt is not None

    def body(*refs):
        if with_loss:
            x_ref, g_ref, wg_ref, wu_ref, wd_ref, t_ref, h_ref, n_ref, G_ref, U_ref, loss_ref, acc_ref = refs
        else:
            x_ref, g_ref, wg_ref, wu_ref, wd_ref, h_ref, n_ref, G_ref, U_ref, acc_ref = refs
        i = pl.program_id(0)
        j = pl.program_id(1)

        @pl.when(j == 0)
        def _():
            xv = x_ref[...]
            r = lax.rsqrt(jnp.mean(xv * xv, axis=-1, keepdims=True) + EPS)
            n_ref[...] = (xv * r * g_ref[...]).astype(BF16)
            acc_ref[...] = jnp.zeros_like(acc_ref)

        n = n_ref[...]
        G = _dot_nt(n, wg_ref[...])
        U = _dot_nt(n, wu_ref[...])
        G_ref[...] = G.astype(BF16)
        U_ref[...] = U.astype(BF16)
        A = (G * _sigmoid(G) * U).astype(BF16)
        acc_ref[...] += _dot(A, wd_ref[...])

        @pl.when(j == NS - 1)
        def _():
            h = x_ref[...] + 0.5 * acc_ref[...]
            if with_loss:
                e = h - t_ref[...]
                h_ref[...] = e * (1.0 / D)

                @pl.when(i == 0)
                def _():
                    loss_ref[...] = jnp.zeros_like(loss_ref)

                loss_ref[...] += jnp.sum(e * e) * (0.5 / D)
            else:
                h_ref[...] = h

    tok = pl.BlockSpec((tm, D), lambda i, j: (i, 0))
    in_specs = [tok, pl.BlockSpec((1, D), lambda i, j: (0, 0)),
                pl.BlockSpec((None, Fs, D), lambda i, j: (j, 0, 0)),
                pl.BlockSpec((None, Fs, D), lambda i, j: (j, 0, 0)),
                pl.BlockSpec((None, Fs, D), lambda i, j: (j, 0, 0))]
    args = [x, gain, wg, wu, wd]
    act = pl.BlockSpec((None, tm, Fs), lambda i, j: (j, i, 0))
    out_shape = [jax.ShapeDtypeStruct((T, D), F32), jax.ShapeDtypeStruct((T, D), BF16),
                 jax.ShapeDtypeStruct((NS, T, Fs), BF16), jax.ShapeDtypeStruct((NS, T, Fs), BF16)]
    out_specs = [tok, tok, act, act]
    if with_loss:
        in_specs.append(tok)
        args.append(tgt)
        out_shape.append(jax.ShapeDtypeStruct((8, LANES), F32))
        out_specs.append(pl.BlockSpec((8, LANES), lambda i, j: (0, 0)))
    return _call(body, grid=(T // tm, NS), in_specs=in_specs, out_specs=out_specs, out_shape=out_shape,
                 scratch_shapes=[pltpu.VMEM((tm, D), F32)], sem=("arbitrary", "arbitrary"), name=name,
                 args=args, phase=phase)


def _rms_bwd(xv, gain, dn):
    r = lax.rsqrt(jnp.mean(xv * xv, axis=-1, keepdims=True) + EPS)
    xhat = xv * r
    dxh = dn * gain
    dx = r * (dxh - xhat * jnp.mean(dxh * xhat, axis=-1, keepdims=True))
    dg = jnp.sum(dn * xhat, axis=0, keepdims=True)
    return dx, dg


def _ffn_bwd_act(dh, x, gain, G, U, wg, wu, wd, *, tm, name):
    T, D = x.shape
    NS, Fs, _ = wg.shape

    def body(dh_ref, x_ref, g_ref, G_ref, U_ref, wg_ref, wu_ref, wd_ref,
             dG_ref, dU_ref, A_ref, dy_ref, dx_ref, dg_ref, acc_ref):
        i = pl.program_id(0)
        j = pl.program_id(1)

        @pl.when(j == 0)
        def _():
            dy_ref[...] = (0.5 * dh_ref[...]).astype(BF16)
            acc_ref[...] = jnp.zeros_like(acc_ref)

        @pl.when((i == 0) & (j == 0))
        def _():
            dg_ref[...] = jnp.zeros_like(dg_ref)

        for r0 in range(0, tm, ROW_CHUNK):
            rows = slice(r0, r0 + ROW_CHUNK)
            Gv = G_ref[rows, :].astype(F32)
            Uv = U_ref[rows, :].astype(F32)
            sig = _sigmoid(Gv)
            s = Gv * sig
            dA = _dot_nt(dy_ref[rows, :], wd_ref[...])
            dG = (dA * Uv * (sig * (1.0 + Gv * (1.0 - sig)))).astype(BF16)
            dU = (dA * s).astype(BF16)
            dG_ref[rows, :] = dG
            dU_ref[rows, :] = dU
            A_ref[rows, :] = (s * Uv).astype(BF16)
            acc_ref[rows, :] += _dot(dG, wg_ref[...]) + _dot(dU, wu_ref[...])

        @pl.when(j == NS - 1)
        def _():
            dx, dg = _rms_bwd(x_ref[...], g_ref[...], acc_ref[...])
            dx_ref[...] = dh_ref[...] + dx
            dg_ref[...] += dg

    tok = pl.BlockSpec((tm, D), lambda i, j: (i, 0))
    act = pl.BlockSpec((None, tm, Fs), lambda i, j: (j, i, 0))
    vec = pl.BlockSpec((1, D), lambda i, j: (0, 0))
    return pl.pallas_call(
        body, grid=(T // tm, NS),
        in_specs=[tok, tok, vec, act, act,
                  pl.BlockSpec((None, Fs, D), lambda i, j: (j, 0, 0)),
                  pl.BlockSpec((None, Fs, D), lambda i, j: (j, 0, 0)),
                  pl.BlockSpec((None, Fs, D), lambda i, j: (j, 0, 0))],
        out_specs=[act, act, act, tok, tok, vec],
        out_shape=[jax.ShapeDtypeStruct((NS, T, Fs), BF16)] * 3
        + [jax.ShapeDtypeStruct((T, D), BF16), jax.ShapeDtypeStruct((T, D), F32),
           jax.ShapeDtypeStruct((1, D), F32)],
        scratch_shapes=[pltpu.VMEM((tm, D), F32)],
        compiler_params=_params("arbitrary", "arbitrary"), name=name)(dh, x, gain, G, U, wg, wu, wd)


def _ffn_bwd_w(n, dy, dG, dU, A, *, tk, name):
    T, D = n.shape
    NS, _, Fs = dG.shape

    def body(n_ref, dy_ref, dG_ref, dU_ref, A_ref, wg_ref, wu_ref, wd_ref):
        @pl.when(pl.program_id(1) == 0)
        def _():
            wg_ref[...] = jnp.zeros_like(wg_ref)
            wu_ref[...] = jnp.zeros_like(wu_ref)
            wd_ref[...] = jnp.zeros_like(wd_ref)

        nv = n_ref[...]
        wg_ref[...] += _dot_tn(dG_ref[...], nv)
        wu_ref[...] += _dot_tn(dU_ref[...], nv)
        wd_ref[...] += _dot_tn(A_ref[...], dy_ref[...])

    tok = pl.BlockSpec((tk, D), lambda j, k: (k, 0))
    act = pl.BlockSpec((None, tk, Fs), lambda j, k: (j, k, 0))
    return pl.pallas_call(
        body, grid=(NS, T // tk), in_specs=[tok, tok, act, act, act],
        out_specs=[pl.BlockSpec((None, Fs, D), lambda j, k: (j, 0, 0))] * 3,
        out_shape=[jax.ShapeDtypeStruct((NS, Fs, D), F32)] * 3,
        compiler_params=_params("arbitrary", "arbitrary"), name=name)(n, dy, dG, dU, A)


def _inproj_fwd(h, gain, win, *, tm, name, phase=None):
    T, D = h.shape
    NS, _, Cs = win.shape

    def body(h_ref, g_ref, w_ref, u_ref, n_ref):
        @pl.when(pl.program_id(1) == 0)
        def _():
            xv = h_ref[...]
            r = lax.rsqrt(jnp.mean(xv * xv, axis=-1, keepdims=True) + EPS)
            n_ref[...] = (xv * r * g_ref[...]).astype(BF16)

        u_ref[...] = _dot(n_ref[...], w_ref[...])

    tok = pl.BlockSpec((tm, D), lambda i, j: (i, 0))
    return _call(
        body, grid=(T // tm, NS),
        in_specs=[tok, pl.BlockSpec((1, D), lambda i, j: (0, 0)),
                  pl.BlockSpec((None, D, Cs), lambda i, j: (j, 0, 0))],
        out_specs=[pl.BlockSpec((tm, Cs), lambda i, j: (i, j)), tok],
        out_shape=[jax.ShapeDtypeStruct((T, NS * Cs), F32), jax.ShapeDtypeStruct((T, D), BF16)],
        sem=("arbitrary", "arbitrary"), name=name, args=(h, gain, win), phase=phase)


def _inproj_bwd_act(du, dh, h, gain, win, *, tm, name, phase=None):
    T, D = h.shape
    NS, _, Cs = win.shape

    def body(du_ref, dh_ref, h_ref, g_ref, w_ref, dx_ref, dg_ref, acc_ref):
        i = pl.program_id(0)
        j = pl.program_id(1)

        @pl.when(j == 0)
        def _():
            acc_ref[...] = jnp.zeros_like(acc_ref)

        @pl.when((i == 0) & (j == 0))
        def _():
            dg_ref[...] = jnp.zeros_like(dg_ref)

        acc_ref[...] += _dot_nt(du_ref[...], w_ref[...])

        @pl.when(j == NS - 1)
        def _():
            dx, dg = _rms_bwd(h_ref[...], g_ref[...], acc_ref[...])
            dx_ref[...] = dh_ref[...] + dx
            dg_ref[...] += dg

    tok = pl.BlockSpec((tm, D), lambda i, j: (i, 0))
    vec = pl.BlockSpec((1, D), lambda i, j: (0, 0))
    return _call(
        body, grid=(T // tm, NS),
        in_specs=[pl.BlockSpec((tm, Cs), lambda i, j: (i, j)), tok, tok, vec,
                  pl.BlockSpec((None, D, Cs), lambda i, j: (j, 0, 0))],
        out_specs=[tok, vec],
        out_shape=[jax.ShapeDtypeStruct((T, D), F32), jax.ShapeDtypeStruct((1, D), F32)],
        scratch_shapes=[pltpu.VMEM((tm, D), F32)],
        sem=("arbitrary", "arbitrary"), name=name, args=(du, dh, h, gain, win), phase=phase)


def _inproj_bwd_w(n, du, ns, *, tk, name):
    T, D = n.shape
    Cs = du.shape[1] // ns

    def body(n_ref, du_ref, w_ref):
        @pl.when(pl.program_id(1) == 0)
        def _():
            w_ref[...] = jnp.zeros_like(w_ref)

        w_ref[...] += _dot_tn(n_ref[...], du_ref[...])

    return pl.pallas_call(
        body, grid=(ns, T // tk),
        in_specs=[pl.BlockSpec((tk, D), lambda j, k: (k, 0)), pl.BlockSpec((tk, Cs), lambda j, k: (k, j))],
        out_specs=pl.BlockSpec((None, D, Cs), lambda j, k: (j, 0, 0)),
        out_shape=jax.ShapeDtypeStruct((ns, D, Cs), F32),
        compiler_params=_params("arbitrary", "arbitrary"), name=name)(n, du)


STRIDE = 4


def _permute(src_ref, tmp_ref, put):
    S = src_ref.shape[0]
    L4, L16 = S // STRIDE, S // (STRIDE * STRIDE)
    put(0, 0, src_ref[...])
    for r0 in range(STRIDE):
        v = src_ref[pl.ds(r0, L4, stride=STRIDE), :]
        put(1, r0 * L4, v)
        tmp_ref[r0 * L4:(r0 + 1) * L4, :] = v
    for r0 in range(STRIDE):
        for r1 in range(STRIDE):
            put(2, (r1 * STRIDE + r0) * L16, tmp_ref[pl.ds(r0 * L4 + r1, L16, stride=STRIDE), :])


def _permute_out(src_ref, tmp_ref, out_ref, cast):
    for cc in range(src_ref.shape[0]):
        cols = slice(cc * LANES, (cc + 1) * LANES)

        def put(p, row0, v, cols=cols):
            out_ref[p, row0:row0 + v.shape[0], cols] = v.astype(cast)

        _permute(src_ref.at[cc], tmp_ref, put)


def _unpermute_in(get_block, dst_ref, tmp_ref, p, S):
    L4, L16 = S // STRIDE, S // (STRIDE * STRIDE)
    if p == 0:
        dst_ref[...] = get_block(0, S)
        return
    if p == 1:
        for r0 in range(STRIDE):
            dst_ref[pl.ds(r0, L4, stride=STRIDE), :] = get_block(r0 * L4, L4)
        return
    for r0 in range(STRIDE):
        for r1 in range(STRIDE):
            tmp_ref[pl.ds(r0 * L4 + r1, L16, stride=STRIDE), :] = get_block((r1 * STRIDE + r0) * L16, L16)
    for r0 in range(STRIDE):
        dst_ref[pl.ds(r0, L4, stride=STRIDE), :] = tmp_ref[r0 * L4:(r0 + 1) * L4, :]


def _qkv_prep(u, gains, B, S, *, name):
    emat = _seg_matrix(D_ATTN)

    def body(u_ref, g_ref, e_ref, out_ref, scr_ref, tmp_ref):
        c = pl.program_id(1)
        xv = u_ref[...]
        ms = _seg_mean(xv * xv, e_ref, HEAD_DIM)
        r = jnp.where(c < 2, lax.rsqrt(ms + EPS), 1.0)
        yv = xv * r * g_ref[...]
        for cc in range(4):
            scr_ref[cc] = yv[:, cc * LANES:(cc + 1) * LANES]
        _permute_out(scr_ref, tmp_ref, out_ref, BF16)

    return pl.pallas_call(
        body, grid=(B, 3),
        in_specs=[pl.BlockSpec((S, D_ATTN), lambda b, c: (b, c)),
                  pl.BlockSpec((None, 1, D_ATTN), lambda b, c: (c, 0, 0)),
                  pl.BlockSpec((D_ATTN, D_ATTN), lambda b, c: (0, 0))],
        out_specs=pl.BlockSpec((None, N_PATTERNS, None, S, D_ATTN), lambda b, c: (c, 0, b, 0, 0)),
        out_shape=jax.ShapeDtypeStruct((3, N_PATTERNS, B, S, D_ATTN), BF16),
        scratch_shapes=[pltpu.VMEM((4, S, LANES), F32), pltpu.VMEM((S, LANES), F32)],
        compiler_params=_params("arbitrary", "arbitrary"), name=name)(u, gains, emat)


def _band_mask(p, b):
    nblk = jnp.right_shift(16, 2 * p)
    has_prev = jnp.bitwise_and(b, nblk - 1) != 0
    qi = lax.broadcasted_iota(jnp.int32, (QBLK, 2 * QBLK), 0)
    ci = lax.broadcasted_iota(jnp.int32, (QBLK, 2 * QBLK), 1)
    dist = QBLK + qi - ci
    return (dist >= 0) & (dist <= QBLK) & (has_prev | (ci >= QBLK))


def _first_head(rows):
    return lax.broadcasted_iota(jnp.int32, (rows, LANES), 1) < HEAD_DIM


def _split_heads(pair):
    first = _first_head(pair.shape[0])
    zero = jnp.zeros_like(pair)
    return jnp.concatenate([jnp.where(first, pair, zero), jnp.where(first, zero, pair)], axis=0)


def _merge_heads(col_a, col_b):
    rows = col_a.shape[0]
    return jnp.where(_first_head(rows), jnp.broadcast_to(col_a, (rows, LANES)), jnp.broadcast_to(col_b, (rows, LANES)))


QB = 4


def _attn_fwd(qkv, *, name):
    nb = qkv.shape[2]

    def body(q_ref, kp_ref, kc_ref, vp_ref, vc_ref, o_ref, lse_ref):
        kall = jnp.concatenate([kp_ref[...]] + [kc_ref[t] for t in range(QB)], axis=0)
        vall = jnp.concatenate([vp_ref[...]] + [vc_ref[t] for t in range(QB)], axis=0)
        for t in range(QB):
            mask = _band_mask(pl.program_id(0), QB * pl.program_id(1) + t)
            mask2 = jnp.concatenate([mask, mask], axis=0)
            kk, vv = kall[t * QBLK:(t + 2) * QBLK], vall[t * QBLK:(t + 2) * QBLK]
            for hp in range(HEADS // 2):
                cols = slice(hp * LANES, (hp + 1) * LANES)
                s = _dot_nt(_split_heads(q_ref[t, :, cols]), kk[:, cols])
                s = jnp.where(mask2, s, NEG)
                m = jnp.max(s, axis=-1, keepdims=True)
                e = jnp.exp(s - m)
                l = jnp.sum(e, axis=-1, keepdims=True)
                pr = (e * (1.0 / l)).astype(BF16)
                o_ref[t, :, cols] = _dot(jnp.concatenate([pr[:QBLK], pr[QBLK:]], axis=1), _split_heads(vv[:, cols]))
                lse = m + jnp.log(l)
                lse_ref[t, :, cols] = _merge_heads(lse[:QBLK], lse[QBLK:])

    cur = lambda which: pl.BlockSpec((None, None, QB, QBLK, D_ATTN), lambda p, i: (which, p, i, 0, 0))
    prev = lambda which: pl.BlockSpec((None, None, None, QBLK, D_ATTN),
                                      lambda p, i: (which, p, jnp.maximum(QB * i - 1, 0), 0, 0))
    out = pl.BlockSpec((None, QB, QBLK, D_ATTN), lambda p, i: (p, i, 0, 0))
    return pl.pallas_call(
        body, grid=(N_PATTERNS, nb // QB), in_specs=[cur(0), prev(1), cur(1), prev(2), cur(2)], out_specs=[out, out],
        out_shape=[jax.ShapeDtypeStruct((N_PATTERNS, nb, QBLK, D_ATTN), F32)] * 2,
        compiler_params=_params("arbitrary", "arbitrary"), name=name)(qkv, qkv, qkv, qkv, qkv)


def _attn_combine(o3, lse3, B, S, *, name):
    def body(o_ref, l_ref, a_ref, lt_ref, so_ref, sl_ref, tmp_ref):
        for p in range(N_PATTERNS):
            _unpermute_in(lambda r0, n, p=p: o_ref[p, pl.ds(r0, n), :], so_ref.at[p], tmp_ref, p, S)
            _unpermute_in(lambda r0, n, p=p: l_ref[p, pl.ds(r0, n), :], sl_ref.at[p], tmp_ref, p, S)
        l0, l1, l2 = sl_ref[0], sl_ref[1], sl_ref[2]
        m = jnp.maximum(jnp.maximum(l0, l1), l2)
        w0, w1, w2 = jnp.exp(l0 - m), jnp.exp(l1 - m), jnp.exp(l2 - m)
        tot = w0 + w1 + w2
        a_ref[...] = (w0 * so_ref[0] + w1 * so_ref[1] + w2 * so_ref[2]) / tot
        lt_ref[...] = m + jnp.log(tot)

    o3 = o3.reshape(N_PATTERNS, B, S, D_ATTN)
    lse3 = lse3.reshape(N_PATTERNS, B, S, D_ATTN)
    inp = pl.BlockSpec((N_PATTERNS, None, S, LANES), lambda b, c: (0, b, 0, c))
    out = pl.BlockSpec((S, LANES), lambda b, c: (b, c))
    return pl.pallas_call(
        body, grid=(B, D_ATTN // LANES), in_specs=[inp, inp], out_specs=[out, out],
        out_shape=[jax.ShapeDtypeStruct((B * S, D_ATTN), F32)] * 2,
        scratch_shapes=[pltpu.VMEM((N_PATTERNS, S, LANES), F32)] * 2 + [pltpu.VMEM((S, LANES), F32)],
        compiler_params=_params("arbitrary", "arbitrary"), name=name)(o3, lse3)


STAT_D = 8


def _attn_bwd_prep(dattn, attn, lse, B, S, *, name):
    emat = _seg_matrix(LANES)
    ncc = D_ATTN // LANES

    def body(da_ref, a_ref, l_ref, e_ref, do_ref, st_ref, scr_ref, nat_ref, tmp_ref):
        cc = pl.program_id(1)
        da = da_ref[...]
        dsum = _seg_mean(da * a_ref[...], e_ref, 1.0)
        scr_ref[...] = da

        def put_do(p, row0, v):
            do_ref[p, row0:row0 + v.shape[0], :] = v.astype(BF16)

        _permute(scr_ref, tmp_ref, put_do)

        lane = lax.broadcasted_iota(jnp.int32, (S, LANES), 1)
        h0 = 2 * cc
        vals = ((h0, l_ref[:, 0:1]), (h0 + 1, l_ref[:, HEAD_DIM:HEAD_DIM + 1]),
                (STAT_D + h0, dsum[:, 0:1]), (STAT_D + h0 + 1, dsum[:, HEAD_DIM:HEAD_DIM + 1]))
        tile = jnp.where(cc == 0, 0.0, nat_ref[...])
        for at, col in vals:
            tile = jnp.where(lane == at, col, tile)
        nat_ref[...] = tile

        @pl.when(cc == ncc - 1)
        def _():
            def put_st(p, row0, v):
                st_ref[p, row0:row0 + v.shape[0], :] = v

            _permute(nat_ref, tmp_ref, put_st)

    inp = pl.BlockSpec((S, LANES), lambda b, c: (b, c))
    return pl.pallas_call(
        body, grid=(B, ncc),
        in_specs=[inp, inp, inp, pl.BlockSpec((LANES, LANES), lambda b, c: (0, 0))],
        out_specs=[pl.BlockSpec((N_PATTERNS, None, S, LANES), lambda b, c: (0, b, 0, c)),
                   pl.BlockSpec((N_PATTERNS, None, S, LANES), lambda b, c: (0, b, 0, 0))],
        out_shape=[jax.ShapeDtypeStruct((N_PATTERNS, B, S, D_ATTN), BF16),
                   jax.ShapeDtypeStruct((N_PATTERNS, B, S, LANES), F32)],
        scratch_shapes=[pltpu.VMEM((S, LANES), F32)] * 3,
        compiler_params=_params("arbitrary", "arbitrary"), name=name)(dattn, attn, lse, emat)


def _attn_bwd(qkv, do3, st3, *, name, phase=None):
    nb = qkv.shape[2]
    ngroups = nb // QB

    def body(q_ref, kp_ref, kc_ref, vp_ref, vc_ref, do_ref, st_ref, out_ref, carry_ref):
        p = pl.program_id(0)
        i = pl.program_id(1)

        @pl.when((p == 0) & (i == 0))
        def _():
            carry_ref[...] = jnp.zeros_like(carry_ref)

        kall = jnp.concatenate([kp_ref[...]] + [kc_ref[t] for t in range(QB)], axis=0)
        vall = jnp.concatenate([vp_ref[...]] + [vc_ref[t] for t in range(QB)], axis=0)

        def block_grads(t, hp):
            cols = slice(hp * LANES, (hp + 1) * LANES)
            h0, h1 = 2 * hp, 2 * hp + 1
            mask = _band_mask(p, QB * i + t) & (i < ngroups)
            mask2 = jnp.concatenate([mask, mask], axis=0)
            kh, vh = kall[t * QBLK:(t + 2) * QBLK, cols], vall[t * QBLK:(t + 2) * QBLK, cols]
            q2 = _split_heads(q_ref[t, :, cols])
            do2 = _split_heads(do_ref[t, :, cols])
            lse = jnp.concatenate([st_ref[t, :, h0:h0 + 1], st_ref[t, :, h1:h1 + 1]], axis=0)
            dsum = jnp.concatenate([st_ref[t, :, STAT_D + h0:STAT_D + h0 + 1],
                                    st_ref[t, :, STAT_D + h1:STAT_D + h1 + 1]], axis=0)
            s = _dot_nt(q2, kh)
            pr = jnp.where(mask2, jnp.exp(s - lse), 0.0)
            dp = _dot_nt(do2, vh)
            ds = (pr * (dp - dsum)).astype(BF16)
            dq = _dot(jnp.concatenate([ds[:QBLK], ds[QBLK:]], axis=1), _split_heads(kh))
            return dq, _dot_tn(ds, q2), _dot_tn(pr.astype(BF16), do2)

        for hp in range(HEADS // 2):
            cols = slice(hp * LANES, (hp + 1) * LANES)
            dq, dk, dv = block_grads(0, hp)
            for c in range(3):
                for t in range(QB):
                    v = carry_ref[c, t, :, cols]
                    if t == QB - 1 and c > 0:
                        v = v + (dk if c == 1 else dv)[:QBLK]
                    out_ref[c, t, :, cols] = v.astype(BF16)
            for t in range(QB):
                if t > 0:
                    dq, dk, dv = block_grads(t, hp)
                    carry_ref[1, t - 1, :, cols] += dk[:QBLK]
                    carry_ref[2, t - 1, :, cols] += dv[:QBLK]
                carry_ref[0, t, :, cols] = dq
                carry_ref[1, t, :, cols] = dk[QBLK:]
                carry_ref[2, t, :, cols] = dv[QBLK:]

    group = lambda i: jnp.minimum(i, ngroups - 1)
    cur = lambda which: pl.BlockSpec((None, None, QB, QBLK, D_ATTN), lambda p, i: (which, p, group(i), 0, 0))
    prev = lambda which: pl.BlockSpec((None, None, None, QBLK, D_ATTN),
                                      lambda p, i: (which, p, jnp.maximum(QB * group(i) - 1, 0), 0, 0))
    aux = lambda lanes: pl.BlockSpec((None, QB, QBLK, lanes), lambda p, i: (p, group(i), 0, 0))
    return _call(
        body, grid=(N_PATTERNS, ngroups + 1),
        in_specs=[cur(0), prev(1), cur(1), prev(2), cur(2), aux(D_ATTN), aux(LANES)],
        out_specs=[pl.BlockSpec((3, None, QB, QBLK, D_ATTN), lambda p, i: (0, p, jnp.maximum(i - 1, 0), 0, 0))],
        out_shape=[jax.ShapeDtypeStruct((3, N_PATTERNS, nb, QBLK, D_ATTN), BF16)],
        scratch_shapes=[pltpu.VMEM((3, QB, QBLK, D_ATTN), F32)],
        sem=("arbitrary", "arbitrary"), name=name, args=(qkv, qkv, qkv, qkv, qkv, do3, st3), phase=phase)


def _attn_grad_combine(cur, u, gains, B, S, *, name, phase=None):
    emat = _seg_matrix(LANES)

    def body(cur_ref, u_ref, g_ref, e_ref, du_ref, dg_ref, scr_ref, tmp_ref):
        c = pl.program_id(0)
        b = pl.program_id(2)
        for p in range(N_PATTERNS):
            _unpermute_in(lambda r0, n, p=p: cur_ref[p, pl.ds(r0, n), :].astype(F32), scr_ref.at[p], tmp_ref, p, S)
        dy = scr_ref[0] + scr_ref[1] + scr_ref[2]
        xv = u_ref[...]
        gain = g_ref[...]
        ms = _seg_mean(xv * xv, e_ref, HEAD_DIM)
        r = lax.rsqrt(ms + EPS)
        xhat = xv * r
        dxh = dy * gain
        dx = r * (dxh - xhat * _seg_mean(dxh * xhat, e_ref, HEAD_DIM))
        du_ref[...] = jnp.where(c < 2, dx, dy).astype(BF16)

        @pl.when((b == 0))
        def _():
            dg_ref[...] = jnp.zeros_like(dg_ref)

        dg_ref[...] += jnp.sum(dy * xhat, axis=0, keepdims=True)

    cur = cur.reshape(3, N_PATTERNS, B, S, D_ATTN)
    ncc = D_ATTN // LANES
    return _call(
        body, grid=(3, ncc, B),
        in_specs=[pl.BlockSpec((None, N_PATTERNS, None, S, LANES), lambda c, cc, b: (c, 0, b, 0, cc)),
                  pl.BlockSpec((S, LANES), lambda c, cc, b: (b, c * ncc + cc)),
                  pl.BlockSpec((None, 1, LANES), lambda c, cc, b: (c, 0, cc)),
                  pl.BlockSpec((LANES, LANES), lambda c, cc, b: (0, 0))],
        out_specs=[pl.BlockSpec((S, LANES), lambda c, cc, b: (b, c * ncc + cc)),
                   pl.BlockSpec((None, 1, LANES), lambda c, cc, b: (c, 0, cc))],
        out_shape=[jax.ShapeDtypeStruct((B * S, 3 * D_ATTN), BF16), jax.ShapeDtypeStruct((3, 1, D_ATTN), F32)],
        scratch_shapes=[pltpu.VMEM((N_PATTERNS, S, LANES), F32), pltpu.VMEM((S, LANES), F32)],
        sem=("arbitrary", "arbitrary", "arbitrary"), name=name, args=(cur, u, gains, emat), phase=phase)


HALO = 32
SUB = 64
SUBLANES = 8


def _shifted_copies(src_ref, sh_ref, tc):
    sh_ref[0] = src_ref[...]
    for r in range(1, SUBLANES):
        sh_ref[r, 0:tc + HALO - SUBLANES, :] = src_ref[pl.ds(r, tc + HALO - SUBLANES), :]


def _shifted(sh_ref, start, size):
    return sh_ref[start % SUBLANES, pl.ds(start - start % SUBLANES, size), :]


def _conv_fwd(u, cw, cb, lg, lb, B, S, *, tc, name):
    nchunk = S // tc
    hb = tc // HALO

    def body(ca_ref, cap_ref, cg_ref, cgp_ref, w_ref, cb_ref, lg_ref, lb_ref, cv_ref, glu_ref, y_ref, pad_ref, sh_ref):
        i = pl.program_id(1)
        glu = ca_ref[...] * _sigmoid(cg_ref[...])
        glu_ref[...] = glu
        prev = cap_ref[...] * _sigmoid(cgp_ref[...])
        pad_ref[0:HALO, :] = jnp.where(i > 0, prev, 0.0)
        pad_ref[HALO:, :] = glu
        _shifted_copies(pad_ref, sh_ref, tc)
        for sub in range(tc // SUB):
            acc = jnp.zeros((SUB, D_CONV), F32) + cb_ref[...]
            for k in range(CONV_K):
                acc = acc + _shifted(sh_ref, sub * SUB + HALO - (CONV_K - 1) + k, SUB) * w_ref[pl.ds(k, 1), :]
            y_ref[sub * SUB:(sub + 1) * SUB, :] = acc
        y = y_ref[...]
        mu = jnp.mean(y, axis=-1, keepdims=True)
        yc = y - mu
        var = jnp.mean(yc * yc, axis=-1, keepdims=True)
        z = yc * lax.rsqrt(var + EPS) * lg_ref[...] + lb_ref[...]
        cv_ref[...] = (z * _sigmoid(z)).astype(BF16)

    def cur(col):
        return pl.BlockSpec((tc, D_CONV), lambda b, i: (b * nchunk + i, col))

    def halo(col):
        return pl.BlockSpec((HALO, D_CONV), lambda b, i: (jnp.maximum((b * nchunk + i) * hb - 1, 0), col))

    vec = pl.BlockSpec((1, D_CONV), lambda b, i: (0, 0))
    out = pl.BlockSpec((tc, D_CONV), lambda b, i: (b * nchunk + i, 0))
    return pl.pallas_call(
        body, grid=(B, nchunk),
        in_specs=[cur(3), halo(3), cur(4), halo(4), pl.BlockSpec((CONV_K, D_CONV), lambda b, i: (0, 0)), vec, vec, vec],
        out_specs=[out, out, out],
        out_shape=[jax.ShapeDtypeStruct((B * S, D_CONV), BF16), jax.ShapeDtypeStruct((B * S, D_CONV), F32),
                   jax.ShapeDtypeStruct((B * S, D_CONV), F32)],
        scratch_shapes=[pltpu.VMEM((tc + HALO, D_CONV), F32), pltpu.VMEM((SUBLANES, tc + HALO, D_CONV), F32)],
        compiler_params=_params("arbitrary", "arbitrary"), name=name)(u, u, u, u, cw, cb, lg, lb)


def _conv_bwd_norm(dcv, y, lg, lb, *, tc, name):
    T = y.shape[0]

    def body(dcv_ref, y_ref, lg_ref, lb_ref, dy_ref, part_ref):
        yv = y_ref[...]
        mu = jnp.mean(yv, axis=-1, keepdims=True)
        yc = yv - mu
        var = jnp.mean(yc * yc, axis=-1, keepdims=True)
        rstd = lax.rsqrt(var + EPS)
        xhat = yc * rstd
        z = xhat * lg_ref[...] + lb_ref[...]
        sig = _sigmoid(z)
        dz = dcv_ref[...] * (sig * (1.0 + z * (1.0 - sig)))
        dxh = dz * lg_ref[...]
        dy = rstd * (dxh - jnp.mean(dxh, axis=-1, keepdims=True)
                     - xhat * jnp.mean(dxh * xhat, axis=-1, keepdims=True))
        dy_ref[...] = dy

        @pl.when(pl.program_id(0) == 0)
        def _():
            part_ref[...] = jnp.zeros_like(part_ref)

        part_ref[0:1, :] += jnp.sum(dz * xhat, axis=0, keepdims=True)
        part_ref[1:2, :] += jnp.sum(dz, axis=0, keepdims=True)
        part_ref[2:3, :] += jnp.sum(dy, axis=0, keepdims=True)

    tok = pl.BlockSpec((tc, D_CONV), lambda i: (i, 0))
    vec = pl.BlockSpec((1, D_CONV), lambda i: (0, 0))
    return pl.pallas_call(
        body, grid=(T // tc,), in_specs=[tok, tok, vec, vec],
        out_specs=[tok, pl.BlockSpec((8, D_CONV), lambda i: (0, 0))],
        out_shape=[jax.ShapeDtypeStruct((T, D_CONV), F32), jax.ShapeDtypeStruct((8, D_CONV), F32)],
        compiler_params=_params("arbitrary"), name=name)(dcv, y, lg, lb)


def _conv_bwd_taps(dy, glu, u, cw, B, S, *, tc, name, phase=None):
    nchunk = S // tc
    hb = tc // HALO
    last_hb = B * S // HALO - 1

    def body(dy_ref, dyn_ref, glu_ref, glup_ref, ca_ref, cg_ref, w_ref, dca_ref, dcg_ref, dw_ref,
             dyp_ref, glp_ref, acc_ref, shd_ref, shg_ref):
        b = pl.program_id(0)
        i = pl.program_id(1)
        dy = dy_ref[...]
        dyp_ref[0:tc, :] = dy
        dyp_ref[tc:, :] = jnp.where(i < nchunk - 1, dyn_ref[...], 0.0)
        glp_ref[0:HALO, :] = jnp.where(i > 0, glup_ref[...], 0.0)
        glp_ref[HALO:, :] = glu_ref[...]
        _shifted_copies(dyp_ref, shd_ref, tc)
        _shifted_copies(glp_ref, shg_ref, tc)

        @pl.when((b == 0) & (i == 0))
        def _():
            dw_ref[...] = jnp.zeros_like(dw_ref)

        for sub in range(tc // SUB):
            acc = jnp.zeros((SUB, D_CONV), F32)
            for k in range(CONV_K):
                acc = acc + _shifted(shd_ref, sub * SUB + (CONV_K - 1) - k, SUB) * w_ref[pl.ds(k, 1), :]
            acc_ref[sub * SUB:(sub + 1) * SUB, :] = acc
        for k in range(CONV_K):
            dw_ref[k:k + 1, :] += jnp.sum(dy * _shifted(shg_ref, HALO - (CONV_K - 1) + k, tc), axis=0, keepdims=True)
        dglu = acc_ref[...]
        ca = ca_ref[...]
        sig = _sigmoid(cg_ref[...])
        dca_ref[...] = (dglu * sig).astype(BF16)
        dcg_ref[...] = (dglu * ca * sig * (1.0 - sig)).astype(BF16)

    tok = pl.BlockSpec((tc, D_CONV), lambda b, i: (b * nchunk + i, 0))
    nxt = pl.BlockSpec((HALO, D_CONV), lambda b, i: (jnp.minimum((b * nchunk + i + 1) * hb, last_hb), 0))
    prv = pl.BlockSpec((HALO, D_CONV), lambda b, i: (jnp.maximum((b * nchunk + i) * hb - 1, 0), 0))
    return _call(
        body, grid=(B, nchunk),
        in_specs=[tok, nxt, tok, prv,
                  pl.BlockSpec((tc, D_CONV), lambda b, i: (b * nchunk + i, 3)),
                  pl.BlockSpec((tc, D_CONV), lambda b, i: (b * nchunk + i, 4)),
                  pl.BlockSpec((CONV_K, D_CONV), lambda b, i: (0, 0))],
        out_specs=[tok, tok, pl.BlockSpec((32, D_CONV), lambda b, i: (0, 0))],
        out_shape=[jax.ShapeDtypeStruct((B * S, D_CONV), BF16), jax.ShapeDtypeStruct((B * S, D_CONV), BF16),
                   jax.ShapeDtypeStruct((32, D_CONV), F32)],
        scratch_shapes=[pltpu.VMEM((tc + HALO, D_CONV), F32), pltpu.VMEM((tc + HALO, D_CONV), F32),
                        pltpu.VMEM((tc, D_CONV), F32), pltpu.VMEM((SUBLANES, tc + HALO, D_CONV), F32),
                        pltpu.VMEM((SUBLANES, tc + HALO, D_CONV), F32)],
        sem=("arbitrary", "arbitrary"), name=name, args=(dy, dy, glu, glu, u, u, cw), phase=phase)


def _outproj_fwd(h, attn, cv, wout, *, tm, name):
    T, D = h.shape

    def body(h_ref, a_ref, c_ref, w_ref, o_ref):
        o_ref[...] = (h_ref[...] + _dot(a_ref[...].astype(BF16), w_ref[0:D_ATTN, :])
                      + _dot(c_ref[...], w_ref[D_ATTN:, :]))

    tok = pl.BlockSpec((tm, D), lambda i: (i, 0))
    half = pl.BlockSpec((tm, D_ATTN), lambda i: (i, 0))
    return pl.pallas_call(
        body, grid=(T // tm,), in_specs=[tok, half, half, pl.BlockSpec(wout.shape, lambda i: (0, 0))],
        out_specs=tok, out_shape=jax.ShapeDtypeStruct((T, D), F32),
        compiler_params=_params("arbitrary"), name=name)(h, attn, cv, wout)


def _outproj_bwd(dh, attn, cv, wout, *, tm, name):
    T, D = dh.shape

    def body(dh_ref, a_ref, c_ref, w_ref, da_ref, dc_ref, dw_ref):
        @pl.when(pl.program_id(0) == 0)
        def _():
            dw_ref[...] = jnp.zeros_like(dw_ref)

        dhb = dh_ref[...].astype(BF16)
        da_ref[...] = _dot_nt(dhb, w_ref[0:D_ATTN, :])
        dc_ref[...] = _dot_nt(dhb, w_ref[D_ATTN:, :])
        dw_ref[0:D_ATTN, :] += _dot_tn(a_ref[...].astype(BF16), dhb)
        dw_ref[D_ATTN:, :] += _dot_tn(c_ref[...], dhb)

    tok = pl.BlockSpec((tm, D), lambda i: (i, 0))
    half = pl.BlockSpec((tm, D_ATTN), lambda i: (i, 0))
    wspec = pl.BlockSpec(wout.shape, lambda i: (0, 0))
    return pl.pallas_call(
        body, grid=(T // tm,), in_specs=[tok, half, half, wspec], out_specs=[half, half, wspec],
        out_shape=[jax.ShapeDtypeStruct((T, D_ATTN), F32), jax.ShapeDtypeStruct((T, D_ATTN), F32),
                   jax.ShapeDtypeStruct(wout.shape, F32)],
        compiler_params=_params("arbitrary"), name=name)(dh, attn, cv, wout)


ADAM_BLOCK_BYTES = 3 * 512 * 1024


def _adamw(w, g, m, v, *, name):
    R, C = w.shape
    tr = R
    for cand in (512, 352, 256, 176, 128, 64, 32, 16, 8):
        if R % cand == 0 and cand * C * 4 <= ADAM_BLOCK_BYTES:
            tr = cand
            break
    c1 = 1.0 - ADAM_B1 ** ADAM_STEP
    c2 = 1.0 - ADAM_B2 ** ADAM_STEP

    def body(w_ref, g_ref, m_ref, v_ref, d_ref, nm_ref, nv_ref):
        gv = g_ref[...]
        nm = ADAM_B1 * m_ref[...] + (1.0 - ADAM_B1) * gv
        nv = ADAM_B2 * v_ref[...] + (1.0 - ADAM_B2) * (gv * gv)
        d_ref[...] = -ADAM_LR * ((nm / c1) / (jnp.sqrt(nv / c2) + ADAM_EPS) + ADAM_WD * w_ref[...])
        nm_ref[...] = nm
        nv_ref[...] = nv

    blk = pl.BlockSpec((tr, C), lambda i: (i, 0))
    return pl.pallas_call(
        body, grid=(R // tr,), in_specs=[blk] * 4, out_specs=[blk] * 3,
        out_shape=[jax.ShapeDtypeStruct((R, C), F32)] * 3,
        compiler_params=_params("arbitrary"), name=name)(w, g, m, v)


TM = 512
TM_WIDE = 1024
TK = 1024
TC = 256


def _local_step(x, tgt, w, overlap=None):
    B, S, D = x.shape
    T = B * S
    x2 = x.reshape(T, D)
    t2 = tgt.reshape(T, D)
    ones = jnp.ones((1, D_ATTN), F32)
    scale = HEAD_DIM ** -0.5
    gains = jnp.stack([jnp.tile(w["q_norm"], (1, HEADS)) * scale, jnp.tile(w["k_norm"], (1, HEADS)), ones])
    g = {}

    def hosting(point, build):
        phase = overlap.phase(point, w, g) if overlap is not None else None
        if phase is None:
            return build(None)
        outs, extra = build(phase)
        overlap.done(point, extra, w, g)
        return outs

    h1, n1, G1, U1 = hosting("ffn1_fwd", lambda ph: _ffn_fwd(
        x2, w["ffn1_norm"], w["wg1"], w["wu1"], w["wd1"], None, tm=TM_WIDE, name="ffn1_fwd", phase=ph))
    u, n2 = hosting("inproj_fwd", lambda ph: _inproj_fwd(h1, w["mix_norm"], w["win"], tm=TM_WIDE, name="inproj_fwd", phase=ph))
    qkv = _qkv_prep(u, gains, B, S, name="qkv_prep")
    qkv = qkv.reshape(3, N_PATTERNS, T // QBLK, QBLK, D_ATTN)
    o3, lse3 = _attn_fwd(qkv, name="attn_fwd")
    attn, lse = _attn_combine(o3, lse3, B, S, name="attn_combine")
    cv, glu, yconv = _conv_fwd(u, w["conv_w"], w["conv_b"], w["conv_ln_g"], w["conv_ln_b"], B, S, tc=TC, name="conv_fwd")
    h2 = _outproj_fwd(h1, attn, cv, w["wout"], tm=TM, name="outproj_fwd")
    dh3, n3, G2, U2, loss = _ffn_fwd(h2, w["ffn2_norm"], w["wg2"], w["wu2"], w["wd2"], t2, tm=TM_WIDE, name="ffn2_fwd")

    dG, dU, A, dy, dh2, g["ffn2_norm"] = _ffn_bwd_act(dh3, h2, w["ffn2_norm"], G2, U2, w["wg2"], w["wu2"], w["wd2"],
                                                    tm=TM, name="ffn2_bwd_act")
    g["wg2"], g["wu2"], g["wd2"] = _ffn_bwd_w(n3, dy, dG, dU, A, tk=TK, name="ffn2_bwd_w")
    dattn, dcv, g["wout"] = _outproj_bwd(dh2, attn, cv, w["wout"], tm=TM, name="outproj_bwd")
    dyc, cpart = _conv_bwd_norm(dcv, yconv, w["conv_ln_g"], w["conv_ln_b"], tc=TC, name="conv_bwd_norm")
    dca, dcg, dcw = hosting("conv_bwd_taps", lambda ph: _conv_bwd_taps(
        dyc, glu, u, w["conv_w"], B, S, tc=TC, name="conv_bwd_taps", phase=ph))
    do3, st3 = _attn_bwd_prep(dattn, attn, lse, B, S, name="attn_bwd_prep")
    nb = T // QBLK
    (cur,) = hosting("attn_bwd", lambda ph: _attn_bwd(
        qkv, do3.reshape(N_PATTERNS, nb, QBLK, D_ATTN), st3.reshape(N_PATTERNS, nb, QBLK, LANES),
        name="attn_bwd", phase=ph))
    du_qkv, dgains = hosting("attn_grad_combine", lambda ph: _attn_grad_combine(
        cur, u, gains, B, S, name="attn_grad_combine", phase=ph))
    du = jnp.concatenate([du_qkv, dca, dcg], axis=1)
    dh1, g["mix_norm"] = hosting("inproj_bwd_act", lambda ph: _inproj_bwd_act(
        du, dh2, h1, w["mix_norm"], w["win"], tm=TM_WIDE, name="inproj_bwd_act", phase=ph))
    g["win"] = _inproj_bwd_w(n2, du, w["win"].shape[0], tk=TK, name="inproj_bwd_w")
    dG, dU, A, dy, dx, g["ffn1_norm"] = _ffn_bwd_act(dh1, x2, w["ffn1_norm"], G1, U1, w["wg1"], w["wu1"], w["wd1"],
                                                   tm=TM, name="ffn1_bwd_act")
    g["wg1"], g["wu1"], g["wd1"] = _ffn_bwd_w(n1, dy, dG, dU, A, tk=TK, name="ffn1_bwd_w")

    g["q_norm"] = dgains[0].reshape(HEADS, HEAD_DIM).sum(axis=0, keepdims=True) * scale
    g["k_norm"] = dgains[1].reshape(HEADS, HEAD_DIM).sum(axis=0, keepdims=True)
    g["conv_ln_g"] = cpart[0:1]
    g["conv_ln_b"] = cpart[1:2]
    g["conv_b"] = cpart[2:3]
    g["conv_w"] = dcw[:CONV_K]
    return loss, dx.reshape(B, S, D), g


N_CHIPS = 4
N_DEV = 8
VMEM_SPEC = pl.BlockSpec(memory_space=pltpu.VMEM)


def _remote(src, dst, send_sem, recv_sem, device):
    return pltpu.make_async_remote_copy(src_ref=src, dst_ref=dst, send_sem=send_sem, recv_sem=recv_sem,
                                        device_id=device, device_id_type=MESH)


def _stage_shards(shards, dtypes, *, name):
    n = len(shards)
    halves = [s.reshape(2, s.shape[0] // 2, s.shape[1]) for s in shards]

    def body(*refs):
        ins, outs, vms, loc_sems = refs[:n], refs[n:2 * n], refs[2 * n:3 * n], refs[3 * n]
        me = 2 * lax.axis_index("x") + lax.axis_index("y")
        copies = []
        for a in range(n):
            vms[a][...] = ins[a][...].astype(dtypes[a])
            cp = pltpu.make_async_copy(vms[a], outs[a].at[me], loc_sems.at[a])
            cp.start()
            copies.append(cp)
        for cp in copies:
            cp.wait()

    return pl.pallas_call(
        body, in_specs=[VMEM_SPEC] * n, out_specs=[ANY] * n,
        out_shape=[jax.ShapeDtypeStruct((N_CHIPS,) + h.shape, dt) for h, dt in zip(halves, dtypes)],
        scratch_shapes=[pltpu.VMEM(h.shape, dt) for h, dt in zip(halves, dtypes)] + [DMA_SEMS((n,))],
        compiler_params=pltpu.CompilerParams(vmem_limit_bytes=VMEM_LIMIT), name=name)(*halves)


def _like(arrays):
    return [jax.ShapeDtypeStruct(a.shape, a.dtype) for a in arrays]


def _axes():
    x, y, c = lax.axis_index("x"), lax.axis_index("y"), lax.axis_index("c")
    first = (x + (1 - c) * (1 - 2 * x), y + c * (1 - 2 * y))
    second = (x + c * (1 - 2 * x), y + (1 - c) * (1 - 2 * y))
    slots = tuple(2 * px + py for px, py in ((x, y), first, second, (1 - x, 1 - y)))
    return (x, y, c), (*first, c), (*second, c), slots


def _gather_ici_phase(bufs):
    n = len(bufs)

    def stage1(ins, outs, sems):
        (x, y, c), peer1, peer2, (own, s1, s2, both) = _axes()
        starts, arrivals = [], []
        for a in range(n):
            mine, land = outs[a].at[own, c], outs[a].at[s2, c]
            starts.append(_remote(mine, mine, *sems(a), peer2))
            arrivals.append(_remote(land, land, *sems(a), peer2))
        return starts, arrivals

    def stage2(ins, outs, sems):
        (x, y, c), peer1, peer2, (own, s1, s2, both) = _axes()
        starts, arrivals = [], []
        for a in range(n):
            for k, (src, dst) in enumerate(((own, s1), (s2, both))):
                mine, land = outs[a].at[src, c], outs[a].at[dst, c]
                starts.append(_remote(mine, mine, *sems(2 * a + k), peer1))
                arrivals.append(_remote(land, land, *sems(2 * a + k), peer1))
        return starts, arrivals

    same = {a: a for a in range(n)}
    return _Phase(bufs, _like(bufs), same, n, stage1).then(_Phase(bufs, _like(bufs), same, 2 * n, stage2))


def _gather_d2d_phase(bufs):
    n = len(bufs)

    def copies(ins, outs, sems):
        (x, y, c), peer1, peer2, (own, s1, s2, both) = _axes()
        starts, arrivals = [], []
        for a in range(n):
            for j, s in enumerate((s1, s2, both)):
                got, land = outs[a].at[s, c], outs[a].at[s, 1 - c]
                starts.append(_remote(got, got, *sems(3 * a + j), (x, y, 1 - c)))
                arrivals.append(_remote(land, land, *sems(3 * a + j), (x, y, 1 - c)))
        return starts, arrivals

    return _Phase(bufs, _like(bufs), {a: a for a in range(n)}, 3 * n, copies)


def _exchange_phase(views):
    n = len(views)

    def copies(ins, outs, sems):
        x, y, c = lax.axis_index("x"), lax.axis_index("y"), lax.axis_index("c")
        starts = [_remote(ins[a].at[pl.ds(0, ins[a].shape[0]), 1 - c], outs[a], *sems(a), (x, y, 1 - c))
                  for a in range(n)]
        return starts, starts

    outs = [jax.ShapeDtypeStruct((v.shape[0],) + v.shape[2:], F32) for v in views]
    return _Phase(views, outs, {}, n, copies)


def _row_block(rows):
    for cand in (256, 176, 128, 64, 32, 16, 8):
        if rows % cand == 0:
            return cand
    return rows


def _add_own_half(g, r, sel, *, name):
    ns, _, rh, cdim = g.shape
    tr = _row_block(rh)

    def body(s_ref, gk_ref, rk_ref, gs_ref, rs_ref, keep_ref, send_ref):
        keep_ref[...] = gk_ref[...] + rk_ref[...]
        send_ref[...] = (gs_ref[...] + rs_ref[...]).astype(BF16)

    def g_spec(off):
        return pl.BlockSpec((None, None, tr, cdim), lambda k, i, s: (s[1 + off + k], s[0], i, 0))

    def r_spec(off):
        return pl.BlockSpec((None, tr, cdim), lambda k, i, s: (s[1 + off + k], i, 0))

    out = pl.BlockSpec((None, tr, cdim), lambda k, i, s: (k, i, 0))
    return pl.pallas_call(
        body,
        grid_spec=pltpu.PrefetchScalarGridSpec(
            num_scalar_prefetch=1, grid=(2, rh // tr),
            in_specs=[g_spec(0), r_spec(0), g_spec(2), r_spec(2)], out_specs=[out, out]),
        out_shape=[jax.ShapeDtypeStruct((2, rh, cdim), F32), jax.ShapeDtypeStruct((2, rh, cdim), BF16)],
        compiler_params=_params("arbitrary", "arbitrary"), name=name)(sel, g, r, g, r)


def _swap_phase(arrays, stage):
    n = len(arrays)

    def copies(ins, outs, sems):
        peer = _axes()[stage]
        starts = [_remote(ins[a], outs[a], *sems(a), peer) for a in range(n)]
        return starts, starts

    return _Phase(arrays, _like(arrays), {}, n, copies)


def _add_stage1(keep, got, *, name):
    _, rh, cdim = keep.shape
    tr = _row_block(rh)

    def body(k_ref, g_ref, keep_ref, send_ref):
        keep_ref[...] = k_ref[0] + g_ref[0].astype(F32)
        send_ref[...] = (k_ref[1] + g_ref[1].astype(F32)).astype(BF16)

    blk2 = pl.BlockSpec((2, tr, cdim), lambda i: (0, i, 0))
    blk = pl.BlockSpec((tr, cdim), lambda i: (i, 0))
    return pl.pallas_call(
        body, grid=(rh // tr,), in_specs=[blk2, blk2], out_specs=[blk, blk],
        out_shape=[jax.ShapeDtypeStruct((rh, cdim), F32), jax.ShapeDtypeStruct((rh, cdim), BF16)],
        compiler_params=_params("arbitrary"), name=name)(keep, got)


def _add_stage2(keep, got, sel, *, name):
    rh, cdim = keep.shape
    tr = _row_block(rh)

    def body(s_ref, k_ref, g_ref, o_ref):
        o_ref[...] = k_ref[...] + g_ref[...].astype(F32)

    blk = pl.BlockSpec((tr, cdim), lambda i, s: (i, 0))
    return pl.pallas_call(
        body,
        grid_spec=pltpu.PrefetchScalarGridSpec(
            num_scalar_prefetch=1, grid=(rh // tr,), in_specs=[blk, blk],
            out_specs=pl.BlockSpec((None, tr, cdim), lambda i, s: (s[0], i, 0))),
        out_shape=jax.ShapeDtypeStruct((2, rh, cdim), F32),
        compiler_params=_params("arbitrary"), name=name)(sel, keep, got)


def _join_phase(halves):
    n = len(halves)

    def copies(ins, outs, sems):
        x, y, c = lax.axis_index("x"), lax.axis_index("y"), lax.axis_index("c")
        starts, arrivals = [], []
        for a in range(n):
            mine, land = outs[a].at[c], outs[a].at[1 - c]
            starts.append(_remote(mine, mine, *sems(a), (x, y, 1 - c)))
            arrivals.append(_remote(land, land, *sems(a), (x, y, 1 - c)))
        return starts, arrivals

    return _Phase(halves, _like(halves), {a: a for a in range(n)}, n, copies)


def _slot_order():
    x, y, c = lax.axis_index("x"), lax.axis_index("y"), lax.axis_index("c")
    own, flip_x, flip_y, both = 2 * x + y, 2 * (1 - x) + y, 2 * x + 1 - y, 2 * (1 - x) + 1 - y
    first = jnp.where(c == 0, flip_x, flip_y)
    second = jnp.where(c == 0, flip_y, flip_x)
    return jnp.stack([c, own, second, first, both]).astype(jnp.int32)


def _reduce_scatter(grads):
    sel = _slot_order()
    views = [_half_view(g) for g in grads]
    got = _run_phase(_exchange_phase(views), name="rs_exchange_halves")
    keep, send = _add_halves(views, got, sel, "late")
    got = _run_phase(_swap_phase(send, 1), name="rs_swap_first_axis")
    keep, send = _add_first(keep, got, "late")
    got = _run_phase(_swap_phase(send, 2), name="rs_swap_second_axis")
    halves = _add_second(keep, got, sel, "late")
    full = _run_phase(_join_phase(halves), name="rs_join_halves")
    return [f.reshape(g.shape[1], g.shape[2]) for f, g in zip(full, grads)]


def _half_view(g):
    return g.reshape(N_CHIPS, 2, g.shape[1] // 2, g.shape[2])


def _add_halves(views, got, sel, tag):
    keep, send = zip(*[_add_own_half(v, r, sel, name=f"rs_add_half_{tag}{a}") for a, (v, r) in enumerate(zip(views, got))])
    return list(keep), list(send)


def _add_first(keep, got, tag):
    keep, send = zip(*[_add_stage1(k, r, name=f"rs_add_first_{tag}{a}") for a, (k, r) in enumerate(zip(keep, got))])
    return list(keep), list(send)


def _add_second(keep, got, sel, tag):
    return [_add_stage2(k, r, sel, name=f"rs_add_second_{tag}{a}") for a, (k, r) in enumerate(zip(keep, got))]


EARLY_GRADS = ("wg2", "wu2", "wd2", "wout")


class _Overlap:
    def __init__(self, staged):
        self.staged = staged
        self.sel = _slot_order()
        self.reduced = {}

    def phase(self, point, w, g):
        if point == "ffn1_fwd":
            return _gather_ici_phase(self.staged)
        if point == "inproj_fwd":
            return _gather_d2d_phase(self.ffn2)
        if point == "conv_bwd_taps":
            self.shapes = [g[k].shape for k in EARLY_GRADS]
            self.views = [_half_view(g[k].reshape(N_CHIPS, -1, g[k].shape[-1])) for k in EARLY_GRADS]
            return _exchange_phase(self.views)
        if point == "attn_bwd":
            return _swap_phase(self.send, 1)
        if point == "attn_grad_combine":
            return _swap_phase(self.send, 2)
        if point == "inproj_bwd_act":
            return _join_phase(self.halves)
        return None

    def done(self, point, outs, w, g):
        if point == "ffn1_fwd":
            win, wout, taps = [_whole(b) for b in _run_phase(_gather_d2d_phase(outs[:3]), name="gather_mix_d2d")]
            w["win"] = win
            w["wout"] = wout.reshape(-1, wout.shape[-1])
            w["conv_w"] = taps.transpose(1, 0, 2).reshape(CONV_K + 1, D_CONV)[:CONV_K]
            self.ffn2 = list(outs[3:])
        elif point == "inproj_fwd":
            w["wg2"], w["wu2"], w["wd2"] = [_whole(b) for b in outs]
        elif point == "conv_bwd_taps":
            self.keep, self.send = _add_halves(self.views, outs, self.sel, "early")
        elif point == "attn_bwd":
            self.keep, self.send = _add_first(self.keep, outs, "early")
        elif point == "attn_grad_combine":
            self.halves = _add_second(self.keep, outs, self.sel, "early")
        elif point == "inproj_bwd_act":
            for k, shp, f in zip(EARLY_GRADS, self.shapes, outs):
                self.reduced[k] = f.reshape(-1, shp[-1])


def _whole(buf):
    return buf.reshape(buf.shape[0], 2 * buf.shape[2], buf.shape[3])


def _allreduce_small(pack, *, name):
    rows = pack.shape[0]

    def body(p_ref, o_ref, buf_ref, send_sems, recv_sems):
        x, y, c = lax.axis_index("x"), lax.axis_index("y"), lax.axis_index("c")
        me = 4 * x + 2 * y + c
        buf_ref[me] = p_ref[...]
        cps = []
        for k in range(1, N_DEV):
            peer = tuple(1 - v if (k >> s) & 1 else v for v, s in ((x, 2), (y, 1), (c, 0)))
            cp = _remote(p_ref, buf_ref.at[me], send_sems.at[k - 1], recv_sems.at[k - 1], peer)
            cp.start()
            cps.append(cp)
        for k in range(1, N_DEV):
            src = 4 * (x ^ ((k >> 2) & 1)) + 2 * (y ^ ((k >> 1) & 1)) + (c ^ (k & 1))
            land = buf_ref.at[src]
            _remote(land, land, send_sems.at[k - 1], recv_sems.at[k - 1], (x, y, c)).wait_recv()
        acc = buf_ref[0]
        for d in range(1, N_DEV):
            acc = acc + buf_ref[d]
        o_ref[...] = acc
        for cp in cps:
            cp.wait_send()

    return pl.pallas_call(
        body, in_specs=[VMEM_SPEC], out_specs=VMEM_SPEC, out_shape=jax.ShapeDtypeStruct(pack.shape, F32),
        scratch_shapes=[pltpu.VMEM((N_DEV, rows, LANES), F32), pltpu.SemaphoreType.DMA((N_DEV - 1,)),
                        pltpu.SemaphoreType.DMA((N_DEV - 1,))], name=name)(pack)


SMALL = ("ffn1_norm", "mix_norm", "q_norm", "k_norm", "conv_b", "conv_ln_g", "conv_ln_b", "ffn2_norm", "conv_w")
BIG = ("ffn1_w_gate", "ffn1_w_up", "ffn1_w_down", "w_in", "w_out", "ffn2_w_gate", "ffn2_w_up", "ffn2_w_down")
TRANSPOSED = ("ffn1_w_gate", "ffn1_w_up", "ffn2_w_gate", "ffn2_w_up")
WEIGHTS = ("ffn1_norm", "ffn1_w_gate", "ffn1_w_up", "ffn1_w_down", "mix_norm", "w_in", "q_norm", "k_norm",
           "conv_w", "conv_b", "conv_ln_g", "conv_ln_b", "w_out", "ffn2_norm", "ffn2_w_gate", "ffn2_w_up",
           "ffn2_w_down")


def _pack(parts):
    rows = []
    for p in parts:
        flat = p.reshape(-1)
        tile = SUBLANES * LANES
        padded = -(-flat.shape[0] // tile) * tile
        rows.append(jnp.pad(flat, (0, padded - flat.shape[0])).reshape(-1, LANES))
    return jnp.concatenate(rows, axis=0)


def _unpack(pack, shapes):
    out, row = [], 0
    for shp in shapes:
        size = shp[0] * shp[1]
        tile = SUBLANES * LANES
        nrows = -(-size // tile) * SUBLANES
        out.append(pack[row:row + nrows].reshape(-1)[:size].reshape(shp))
        row += nrows
    return out


def kernel(x, ffn1_norm, ffn1_w_gate, ffn1_w_up, ffn1_w_down, mix_norm, w_in, q_norm, k_norm, conv_w, conv_b, conv_ln_g, conv_ln_b, w_out, ffn2_norm, ffn2_w_gate, ffn2_w_up, ffn2_w_down, loss_target, m_ffn1_norm, m_ffn1_w_gate, m_ffn1_w_up, m_ffn1_w_down, m_mix_norm, m_w_in, m_q_norm, m_k_norm, m_conv_w, m_conv_b, m_conv_ln_g, m_conv_ln_b, m_w_out, m_ffn2_norm, m_ffn2_w_gate, m_ffn2_w_up, m_ffn2_w_down, v_ffn1_norm, v_ffn1_w_gate, v_ffn1_w_up, v_ffn1_w_down, v_mix_norm, v_w_in, v_q_norm, v_k_norm, v_conv_w, v_conv_b, v_conv_ln_g, v_conv_ln_b, v_w_out, v_ffn2_norm, v_ffn2_w_gate, v_ffn2_w_up, v_ffn2_w_down):
    wts = dict(ffn1_norm=ffn1_norm, ffn1_w_gate=ffn1_w_gate[0], ffn1_w_up=ffn1_w_up[0], ffn1_w_down=ffn1_w_down[0],
               mix_norm=mix_norm, w_in=w_in[0], q_norm=q_norm, k_norm=k_norm, conv_w=conv_w[0], conv_b=conv_b,
               conv_ln_g=conv_ln_g, conv_ln_b=conv_ln_b, w_out=w_out[0], ffn2_norm=ffn2_norm,
               ffn2_w_gate=ffn2_w_gate[0], ffn2_w_up=ffn2_w_up[0], ffn2_w_down=ffn2_w_down[0])
    mom = dict(ffn1_norm=m_ffn1_norm, ffn1_w_gate=m_ffn1_w_gate[0], ffn1_w_up=m_ffn1_w_up[0], ffn1_w_down=m_ffn1_w_down[0],
               mix_norm=m_mix_norm, w_in=m_w_in[0], q_norm=m_q_norm, k_norm=m_k_norm, conv_w=m_conv_w[0], conv_b=m_conv_b,
               conv_ln_g=m_conv_ln_g, conv_ln_b=m_conv_ln_b, w_out=m_w_out[0], ffn2_norm=m_ffn2_norm,
               ffn2_w_gate=m_ffn2_w_gate[0], ffn2_w_up=m_ffn2_w_up[0], ffn2_w_down=m_ffn2_w_down[0])
    var = dict(ffn1_norm=v_ffn1_norm, ffn1_w_gate=v_ffn1_w_gate[0], ffn1_w_up=v_ffn1_w_up[0], ffn1_w_down=v_ffn1_w_down[0],
               mix_norm=v_mix_norm, w_in=v_w_in[0], q_norm=v_q_norm, k_norm=v_k_norm, conv_w=v_conv_w[0], conv_b=v_conv_b,
               conv_ln_g=v_conv_ln_g, conv_ln_b=v_conv_ln_b, w_out=v_w_out[0], ffn2_norm=v_ffn2_norm,
               ffn2_w_gate=v_ffn2_w_gate[0], ffn2_w_up=v_ffn2_w_up[0], ffn2_w_down=v_ffn2_w_down[0])
    chip = 2 * lax.axis_index("x") + lax.axis_index("y")
    for src in (wts, mom, var):
        for n in TRANSPOSED:
            src[n] = src[n].T

    taps = jnp.pad(wts["conv_w"], ((0, 1), (0, 0)))
    staged = _stage_shards([wts["ffn1_w_gate"], wts["ffn1_w_up"], wts["ffn1_w_down"], wts["w_in"], wts["w_out"], taps,
                            wts["ffn2_w_gate"], wts["ffn2_w_up"], wts["ffn2_w_down"]],
                           [BF16, BF16, BF16, BF16, BF16, F32, BF16, BF16, BF16], name="stage_shards")
    first = _run_phase(_gather_ici_phase(staged[:3]).then(_gather_d2d_phase(staged[:3])), name="gather_ffn1")
    wg1, wu1, wd1 = [_whole(b) for b in first]
    w = dict(ffn1_norm=ffn1_norm, mix_norm=mix_norm, ffn2_norm=ffn2_norm, q_norm=q_norm, k_norm=k_norm,
             conv_b=conv_b, conv_ln_g=conv_ln_g, conv_ln_b=conv_ln_b, wg1=wg1, wu1=wu1, wd1=wd1)
    overlap = _Overlap(staged[3:])
    loss_part, grad_x, g = _local_step(x, loss_target, w, overlap)

    late = _reduce_scatter([g["wg1"], g["wu1"], g["wd1"], g["win"]])
    big_grads = dict(zip(("ffn1_w_gate", "ffn1_w_up", "ffn1_w_down", "w_in"), late))
    big_grads.update(ffn2_w_gate=overlap.reduced["wg2"], ffn2_w_up=overlap.reduced["wu2"],
                     ffn2_w_down=overlap.reduced["wd2"], w_out=overlap.reduced["wout"])

    small_shapes = [g[n].shape for n in SMALL] + [(SUBLANES, LANES)]
    red = _allreduce_small(_pack([g[n] for n in SMALL] + [loss_part]), name="allreduce_small")
    small = dict(zip(SMALL + ("loss",), _unpack(red, small_shapes)))
    loss = small["loss"][0, 0]
    small["conv_w"] = lax.dynamic_slice_in_dim(small["conv_w"], chip * LANES, LANES, axis=1)

    grads, delta, new_m, new_v = {}, {}, {}, {}
    for n in BIG:
        grads[n] = big_grads[n]
        delta[n], new_m[n], new_v[n] = _adamw(wts[n], grads[n], mom[n], var[n], name=f"adamw_{n}")
    shapes = [wts[n].shape for n in SMALL]
    packs = [_pack([src[n] for n in SMALL]) for src in (wts, small, mom, var)]
    outs = _adamw(*packs, name="adamw_small")
    for dst, pk in zip((delta, new_m, new_v), outs):
        dst.update(zip(SMALL, _unpack(pk, shapes)))
    for n in SMALL:
        grads[n] = small[n]

    def shaped(d, n):
        v = d[n].T if n in TRANSPOSED else d[n]
        return v.reshape((1,) + v.shape) if n in BIG or n == "conv_w" else v

    return (loss, grad_x, *[shaped(grads, n) for n in WEIGHTS], *[shaped(delta, n) for n in WEIGHTS],
            *[shaped(new_m, n) for n in WEIGHTS], *[shaped(new_v, n) for n in WEIGHTS])
```

```python
import functools

import jax
import jax.numpy as jnp
from jax import lax
from jax.experimental import pallas as pl
from jax.experimental.pallas import tpu as pltpu

F32 = jnp.float32
BF16 = jnp.bfloat16

EPS = 1e-6
HEADS = 8
HEAD_DIM = 64
D_ATTN = HEADS * HEAD_DIM
D_CONV = 512
CONV_K = 31
QBLK = 128
N_PATTERNS = 3
DILATIONS = (1, 4, 16)
LANES = 128
NEG = -1e30

ADAM_LR = 0.001
ADAM_B1 = 0.9
ADAM_B2 = 0.999
ADAM_EPS = 1e-08
ADAM_WD = 0.01
ADAM_STEP = 10

VMEM_LIMIT = 56 * 1024 * 1024
MESH = pl.DeviceIdType.MESH

NT_DIMS = (((1,), (1,)), ((), ()))
TN_DIMS = (((0,), (0,)), ((), ()))


def _params(*sem):
    return pltpu.CompilerParams(dimension_semantics=sem, vmem_limit_bytes=VMEM_LIMIT)


def _dot(a, b):
    return jnp.dot(a, b, preferred_element_type=F32)


def _dot_nt(a, b):
    return lax.dot_general(a, b, NT_DIMS, preferred_element_type=F32)


def _dot_tn(a, b):
    return lax.dot_general(a, b, TN_DIMS, preferred_element_type=F32)


def _sigmoid(x):
    return 1.0 / (1.0 + jnp.exp(-x))


def _seg_mean(v, e_ref, width):
    hi = v.astype(BF16)
    lo = (v - hi.astype(F32)).astype(BF16)
    e = e_ref[...]
    return (_dot(hi, e) + _dot(lo, e)) * (1.0 / width)


def _seg_matrix(n):
    i = jnp.arange(n)
    return (i[:, None] // HEAD_DIM == i[None, :] // HEAD_DIM).astype(BF16)


ANY = pl.BlockSpec(memory_space=pl.ANY)
DMA_SEMS = pltpu.SemaphoreType.DMA


class _Phase:
    def __init__(self, ins, outs, aliases, nsem, copies):
        self.ins, self.outs, self.aliases = list(ins), list(outs), dict(aliases)
        self.stages = [(nsem, copies)]

    def then(self, other):
        self.stages = self.stages + other.stages
        return self

    @property
    def nsem(self):
        return sum(n for n, _ in self.stages)

    def _copies(self, k, in_refs, out_refs, send_sems, recv_sems):
        base = sum(n for n, _ in self.stages[:k])
        return self.stages[k][1](in_refs, out_refs, lambda i: (send_sems.at[base + i], recv_sems.at[base + i]))

    def start(self, k, *refs):
        for cp in self._copies(k, *refs)[0]:
            cp.start()

    def finish(self, k, *refs):
        starts, arrivals = self._copies(k, *refs)
        for cp in arrivals:
            cp.wait_recv()
        for cp in starts:
            cp.wait_send()


def _run_phase(phase, *, name):
    n_in, n_out = len(phase.ins), len(phase.outs)

    def body(*refs):
        ins, outs = refs[:n_in], refs[n_in:n_in + n_out]
        send_sems, recv_sems = refs[n_in + n_out:]
        for k in range(len(phase.stages)):
            phase.start(k, ins, outs, send_sems, recv_sems)
            phase.finish(k, ins, outs, send_sems, recv_sems)

    return pl.pallas_call(
        body, in_specs=[ANY] * n_in, out_specs=[ANY] * n_out, out_shape=phase.outs,
        input_output_aliases=phase.aliases,
        scratch_shapes=[DMA_SEMS((phase.nsem,)), DMA_SEMS((phase.nsem,))], name=name)(*phase.ins)


def _call(body, *, grid, in_specs, out_specs, out_shape, scratch_shapes=(), sem, name, args, phase=None):
    in_specs, out_specs, out_shape = list(in_specs), list(out_specs), list(out_shape)
    scratch_shapes = list(scratch_shapes)
    if phase is None:
        return pl.pallas_call(body, grid=grid, in_specs=in_specs, out_specs=out_specs, out_shape=out_shape,
                              scratch_shapes=scratch_shapes, compiler_params=_params(*sem), name=name)(*args)
    n_in, n_out, n_scr = len(in_specs), len(out_specs), len(scratch_shapes)
    p_in, p_out = len(phase.ins), len(phase.outs)

    def hosted(*refs):
        ins, pins = refs[:n_in], refs[n_in:n_in + p_in]
        o0 = n_in + p_in
        outs, pouts = refs[o0:o0 + n_out], refs[o0 + n_out:o0 + n_out + p_out]
        s0 = o0 + n_out + p_out
        scr = refs[s0:s0 + n_scr]
        send_sems, recv_sems = refs[s0 + n_scr:]
        step = 0
        for d, n in enumerate(grid):
            step = step * n + pl.program_id(d)
        nsteps = functools.reduce(lambda a, b: a * b, grid)
        nstages = len(phase.stages)
        comm_refs = (pins, pouts, send_sems, recv_sems)

        for k in range(nstages):
            @pl.when(step == (k * nsteps) // nstages)
            def _(k=k):
                if k > 0:
                    phase.finish(k - 1, *comm_refs)
                phase.start(k, *comm_refs)

        body(*ins, *outs, *scr)

        @pl.when(step == nsteps - 1)
        def _():
            phase.finish(nstages - 1, *comm_refs)

    res = pl.pallas_call(
        hosted, grid=grid, in_specs=in_specs + [ANY] * p_in, out_specs=out_specs + [ANY] * p_out,
        out_shape=out_shape + phase.outs,
        input_output_aliases={n_in + i: n_out + o for i, o in phase.aliases.items()},
        scratch_shapes=scratch_shapes + [DMA_SEMS((phase.nsem,)), DMA_SEMS((phase.nsem,))],
        compiler_params=_params(*sem), name=name)(*args, *phase.ins)
    return res[:n_out], res[n_out:]


ROW_CHUNK = 256


def _ffn_fwd(x, gain, wg, wu, wd, tgt, *, tm, name, phase=None):
    T, D = x.shape
    NS, Fs, _ = wg.shape
    with_loss = tgt is not None

    def body(*refs):
        if with_loss:
            x_ref, g_ref, wg_ref, wu_ref, wd_ref, t_ref, h_ref, n_ref, G_ref, U_ref, loss_ref, acc_ref = refs
        else:
            x_ref, g_ref, wg_ref, wu_ref, wd_ref, h_ref, n_ref, G_ref, U_ref, acc_ref = refs
        i = pl.program_id(0)
        j = pl.program_id(1)

        @pl.when(j == 0)
        def _():
            xv = x_ref[...]
            r = lax.rsqrt(jnp.mean(xv * xv, axis=-1, keepdims=True) + EPS)
            n_ref[...] = (xv * r * g_ref[...]).astype(BF16)
            acc_ref[...] = jnp.zeros_like(acc_ref)

        n = n_ref[...]
        G = _dot_nt(n, wg_ref[...])
        U = _dot_nt(n, wu_ref[...])
        G_ref[...] = G.astype(BF16)
        U_ref[...] = U.astype(BF16)
        A = (G * _sigmoid(G) * U).astype(BF16)
        acc_ref[...] += _dot(A, wd_ref[...])

        @pl.when(j == NS - 1)
        def _():
            h = x_ref[...] + 0.5 * acc_ref[...]
            if with_loss:
                e = h - t_ref[...]
                h_ref[...] = e * (1.0 / D)

                @pl.when(i == 0)
                def _():
                    loss_ref[...] = jnp.zeros_like(loss_ref)

                loss_ref[...] += jnp.sum(e * e) * (0.5 / D)
            else:
                h_ref[...] = h

    tok = pl.BlockSpec((tm, D), lambda i, j: (i, 0))
    in_specs = [tok, pl.BlockSpec((1, D), lambda i, j: (0, 0)),
                pl.BlockSpec((None, Fs, D), lambda i, j: (j, 0, 0)),
                pl.BlockSpec((None, Fs, D), lambda i, j: (j, 0, 0)),
                pl.BlockSpec((None, Fs, D), lambda i, j: (j, 0, 0))]
    args = [x, gain, wg, wu, wd]
    act = pl.BlockSpec((None, tm, Fs), lambda i, j: (j, i, 0))
    out_shape = [jax.ShapeDtypeStruct((T, D), F32), jax.ShapeDtypeStruct((T, D), BF16),
                 jax.ShapeDtypeStruct((NS, T, Fs), BF16), jax.ShapeDtypeStruct((NS, T, Fs), BF16)]
    out_specs = [tok, tok, act, act]
    if with_loss:
        in_specs.append(tok)
        args.append(tgt)
        out_shape.append(jax.ShapeDtypeStruct((8, LANES), F32))
        out_specs.append(pl.BlockSpec((8, LANES), lambda i, j: (0, 0)))
    return _call(body, grid=(T // tm, NS), in_specs=in_specs, out_specs=out_specs, out_shape=out_shape,
                 scratch_shapes=[pltpu.VMEM((tm, D), F32)], sem=("arbitrary", "arbitrary"), name=name,
                 args=args, phase=phase)


def _rms_bwd(xv, gain, dn):
    r = lax.rsqrt(jnp.mean(xv * xv, axis=-1, keepdims=True) + EPS)
    xhat = xv * r
    dxh = dn * gain
    dx = r * (dxh - xhat * jnp.mean(dxh * xhat, axis=-1, keepdims=True))
    dg = jnp.sum(dn * xhat, axis=0, keepdims=True)
    return dx, dg


def _ffn_bwd_act(dh, x, gain, G, U, wg, wu, wd, *, tm, name, phase=None):
    T, D = x.shape
    NS, Fs, _ = wg.shape

    def body(dh_ref, x_ref, g_ref, G_ref, U_ref, wg_ref, wu_ref, wd_ref,
             dG_ref, dU_ref, A_ref, dy_ref, dx_ref, dg_ref, acc_ref):
        i = pl.program_id(0)
        j = pl.program_id(1)

        @pl.when(j == 0)
        def _():
            dy_ref[...] = (0.5 * dh_ref[...]).astype(BF16)
            acc_ref[...] = jnp.zeros_like(acc_ref)

        @pl.when((i == 0) & (j == 0))
        def _():
            dg_ref[...] = jnp.zeros_like(dg_ref)

        for r0 in range(0, tm, ROW_CHUNK):
            rows = slice(r0, r0 + ROW_CHUNK)
            Gv = G_ref[rows, :].astype(F32)
            Uv = U_ref[rows, :].astype(F32)
            sig = _sigmoid(Gv)
            s = Gv * sig
            dA = _dot_nt(dy_ref[rows, :], wd_ref[...])
            dG = (dA * Uv * (sig * (1.0 + Gv * (1.0 - sig)))).astype(BF16)
            dU = (dA * s).astype(BF16)
            dG_ref[rows, :] = dG
            dU_ref[rows, :] = dU
            A_ref[rows, :] = (s * Uv).astype(BF16)
            acc_ref[rows, :] += _dot(dG, wg_ref[...]) + _dot(dU, wu_ref[...])

        @pl.when(j == NS - 1)
        def _():
            dx, dg = _rms_bwd(x_ref[...], g_ref[...], acc_ref[...])
            dx_ref[...] = dh_ref[...] + dx
            dg_ref[...] += dg

    tok = pl.BlockSpec((tm, D), lambda i, j: (i, 0))
    act = pl.BlockSpec((None, tm, Fs), lambda i, j: (j, i, 0))
    vec = pl.BlockSpec((1, D), lambda i, j: (0, 0))
    return _call(
        body, grid=(T // tm, NS),
        in_specs=[tok, tok, vec, act, act,
                  pl.BlockSpec((None, Fs, D), lambda i, j: (j, 0, 0)),
                  pl.BlockSpec((None, Fs, D), lambda i, j: (j, 0, 0)),
                  pl.BlockSpec((None, Fs, D), lambda i, j: (j, 0, 0))],
        out_specs=[act, act, act, tok, tok, vec],
        out_shape=[jax.ShapeDtypeStruct((NS, T, Fs), BF16)] * 3
        + [jax.ShapeDtypeStruct((T, D), BF16), jax.ShapeDtypeStruct((T, D), F32),
           jax.ShapeDtypeStruct((1, D), F32)],
        scratch_shapes=[pltpu.VMEM((tm, D), F32)],
        sem=("arbitrary", "arbitrary"), name=name, args=(dh, x, gain, G, U, wg, wu, wd), phase=phase)


def _ffn_bwd_w(n, dy, dG, dU, A, *, tk, name, phase=None):
    T, D = n.shape
    NS, _, Fs = dG.shape

    def body(n_ref, dy_ref, dG_ref, dU_ref, A_ref, wg_ref, wu_ref, wd_ref):
        @pl.when(pl.program_id(1) == 0)
        def _():
            wg_ref[...] = jnp.zeros_like(wg_ref)
            wu_ref[...] = jnp.zeros_like(wu_ref)
            wd_ref[...] = jnp.zeros_like(wd_ref)

        nv = n_ref[...]
        wg_ref[...] += _dot_tn(dG_ref[...], nv)
        wu_ref[...] += _dot_tn(dU_ref[...], nv)
        wd_ref[...] += _dot_tn(A_ref[...], dy_ref[...])

    tok = pl.BlockSpec((tk, D), lambda j, k: (k, 0))
    act = pl.BlockSpec((None, tk, Fs), lambda j, k: (j, k, 0))
    return _call(
        body, grid=(NS, T // tk), in_specs=[tok, tok, act, act, act],
        out_specs=[pl.BlockSpec((None, Fs, D), lambda j, k: (j, 0, 0))] * 3,
        out_shape=[jax.ShapeDtypeStruct((NS, Fs, D), F32)] * 3,
        sem=("arbitrary", "arbitrary"), name=name, args=(n, dy, dG, dU, A), phase=phase)


def _inproj_fwd(h, gain, win, *, tm, name, phase=None):
    T, D = h.shape
    NS, _, Cs = win.shape

    def body(h_ref, g_ref, w_ref, u_ref, n_ref):
        @pl.when(pl.program_id(1) == 0)
        def _():
            xv = h_ref[...]
            r = lax.rsqrt(jnp.mean(xv * xv, axis=-1, keepdims=True) + EPS)
            n_ref[...] = (xv * r * g_ref[...]).astype(BF16)

        u_ref[...] = _dot(n_ref[...], w_ref[...])

    tok = pl.BlockSpec((tm, D), lambda i, j: (i, 0))
    return _call(
        body, grid=(T // tm, NS),
        in_specs=[tok, pl.BlockSpec((1, D), lambda i, j: (0, 0)),
                  pl.BlockSpec((None, D, Cs), lambda i, j: (j, 0, 0))],
        out_specs=[pl.BlockSpec((tm, Cs), lambda i, j: (i, j)), tok],
        out_shape=[jax.ShapeDtypeStruct((T, NS * Cs), F32), jax.ShapeDtypeStruct((T, D), BF16)],
        sem=("arbitrary", "arbitrary"), name=name, args=(h, gain, win), phase=phase)


def _inproj_bwd_act(du, dh, h, gain, win, *, tm, name, phase=None):
    T, D = h.shape
    NS, _, Cs = win.shape

    def body(du_ref, dh_ref, h_ref, g_ref, w_ref, dx_ref, dg_ref, acc_ref):
        i = pl.program_id(0)
        j = pl.program_id(1)

        @pl.when(j == 0)
        def _():
            acc_ref[...] = jnp.zeros_like(acc_ref)

        @pl.when((i == 0) & (j == 0))
        def _():
            dg_ref[...] = jnp.zeros_like(dg_ref)

        acc_ref[...] += _dot_nt(du_ref[...], w_ref[...])

        @pl.when(j == NS - 1)
        def _():
            dx, dg = _rms_bwd(h_ref[...], g_ref[...], acc_ref[...])
            dx_ref[...] = dh_ref[...] + dx
            dg_ref[...] += dg

    tok = pl.BlockSpec((tm, D), lambda i, j: (i, 0))
    vec = pl.BlockSpec((1, D), lambda i, j: (0, 0))
    return _call(
        body, grid=(T // tm, NS),
        in_specs=[pl.BlockSpec((tm, Cs), lambda i, j: (i, j)), tok, tok, vec,
                  pl.BlockSpec((None, D, Cs), lambda i, j: (j, 0, 0))],
        out_specs=[tok, vec],
        out_shape=[jax.ShapeDtypeStruct((T, D), F32), jax.ShapeDtypeStruct((1, D), F32)],
        scratch_shapes=[pltpu.VMEM((tm, D), F32)],
        sem=("arbitrary", "arbitrary"), name=name, args=(du, dh, h, gain, win), phase=phase)


def _inproj_bwd_w(n, du, ns, *, tk, name, phase=None):
    T, D = n.shape
    Cs = du.shape[1] // ns

    def body(n_ref, du_ref, w_ref):
        @pl.when(pl.program_id(1) == 0)
        def _():
            w_ref[...] = jnp.zeros_like(w_ref)

        w_ref[...] += _dot_tn(n_ref[...], du_ref[...])

    return _call(
        body, grid=(ns, T // tk),
        in_specs=[pl.BlockSpec((tk, D), lambda j, k: (k, 0)), pl.BlockSpec((tk, Cs), lambda j, k: (k, j))],
        out_specs=[pl.BlockSpec((None, D, Cs), lambda j, k: (j, 0, 0))],
        out_shape=[jax.ShapeDtypeStruct((ns, D, Cs), F32)],
        sem=("arbitrary", "arbitrary"), name=name, args=(n, du), phase=phase)


STRIDE = 4


def _permute(src_ref, tmp_ref, put):
    S = src_ref.shape[0]
    L4, L16 = S // STRIDE, S // (STRIDE * STRIDE)
    put(0, 0, src_ref[...])
    for r0 in range(STRIDE):
        v = src_ref[pl.ds(r0, L4, stride=STRIDE), :]
        put(1, r0 * L4, v)
        tmp_ref[r0 * L4:(r0 + 1) * L4, :] = v
    for r0 in range(STRIDE):
        for r1 in range(STRIDE):
            put(2, (r1 * STRIDE + r0) * L16, tmp_ref[pl.ds(r0 * L4 + r1, L16, stride=STRIDE), :])


def _permute_out(src_ref, tmp_ref, out_ref, cast):
    for cc in range(src_ref.shape[0]):
        cols = slice(cc * LANES, (cc + 1) * LANES)

        def put(p, row0, v, cols=cols):
            out_ref[p, row0:row0 + v.shape[0], cols] = v.astype(cast)

        _permute(src_ref.at[cc], tmp_ref, put)


def _unpermute_in(get_block, dst_ref, tmp_ref, p, S):
    L4, L16 = S // STRIDE, S // (STRIDE * STRIDE)
    if p == 0:
        dst_ref[...] = get_block(0, S)
        return
    if p == 1:
        for r0 in range(STRIDE):
            dst_ref[pl.ds(r0, L4, stride=STRIDE), :] = get_block(r0 * L4, L4)
        return
    for r0 in range(STRIDE):
        for r1 in range(STRIDE):
            tmp_ref[pl.ds(r0 * L4 + r1, L16, stride=STRIDE), :] = get_block((r1 * STRIDE + r0) * L16, L16)
    for r0 in range(STRIDE):
        dst_ref[pl.ds(r0, L4, stride=STRIDE), :] = tmp_ref[r0 * L4:(r0 + 1) * L4, :]


def _qkv_prep(u, gains, B, S, *, name):
    emat = _seg_matrix(D_ATTN)

    def body(u_ref, g_ref, e_ref, out_ref, scr_ref, tmp_ref):
        c = pl.program_id(1)
        xv = u_ref[...]
        ms = _seg_mean(xv * xv, e_ref, HEAD_DIM)
        r = jnp.where(c < 2, lax.rsqrt(ms + EPS), 1.0)
        yv = xv * r * g_ref[...]
        for cc in range(4):
            scr_ref[cc] = yv[:, cc * LANES:(cc + 1) * LANES]
        _permute_out(scr_ref, tmp_ref, out_ref, BF16)

    return pl.pallas_call(
        body, grid=(B, 3),
        in_specs=[pl.BlockSpec((S, D_ATTN), lambda b, c: (b, c)),
                  pl.BlockSpec((None, 1, D_ATTN), lambda b, c: (c, 0, 0)),
                  pl.BlockSpec((D_ATTN, D_ATTN), lambda b, c: (0, 0))],
        out_specs=pl.BlockSpec((None, N_PATTERNS, None, S, D_ATTN), lambda b, c: (c, 0, b, 0, 0)),
        out_shape=jax.ShapeDtypeStruct((3, N_PATTERNS, B, S, D_ATTN), BF16),
        scratch_shapes=[pltpu.VMEM((4, S, LANES), F32), pltpu.VMEM((S, LANES), F32)],
        compiler_params=_params("arbitrary", "arbitrary"), name=name)(u, gains, emat)


def _band_mask(p, b):
    nblk = jnp.right_shift(16, 2 * p)
    has_prev = jnp.bitwise_and(b, nblk - 1) != 0
    qi = lax.broadcasted_iota(jnp.int32, (QBLK, 2 * QBLK), 0)
    ci = lax.broadcasted_iota(jnp.int32, (QBLK, 2 * QBLK), 1)
    dist = QBLK + qi - ci
    return (dist >= 0) & (dist <= QBLK) & (has_prev | (ci >= QBLK))


def _first_head(rows):
    return lax.broadcasted_iota(jnp.int32, (rows, LANES), 1) < HEAD_DIM


def _split_heads(pair):
    first = _first_head(pair.shape[0])
    zero = jnp.zeros_like(pair)
    return jnp.concatenate([jnp.where(first, pair, zero), jnp.where(first, zero, pair)], axis=0)


def _merge_heads(col_a, col_b):
    rows = col_a.shape[0]
    return jnp.where(_first_head(rows), jnp.broadcast_to(col_a, (rows, LANES)), jnp.broadcast_to(col_b, (rows, LANES)))


QB = 8


def _attn_fwd(qkv, *, name):
    nb = qkv.shape[2]

    def body(q_ref, kp_ref, kc_ref, vp_ref, vc_ref, o_ref, lse_ref):
        kall = jnp.concatenate([kp_ref[...]] + [kc_ref[t] for t in range(QB)], axis=0)
        vall = jnp.concatenate([vp_ref[...]] + [vc_ref[t] for t in range(QB)], axis=0)
        for t in range(QB):
            mask = _band_mask(pl.program_id(0), QB * pl.program_id(1) + t)
            mask2 = jnp.concatenate([mask, mask], axis=0)
            kk, vv = kall[t * QBLK:(t + 2) * QBLK], vall[t * QBLK:(t + 2) * QBLK]
            for hp in range(HEADS // 2):
                cols = slice(hp * LANES, (hp + 1) * LANES)
                s = _dot_nt(_split_heads(q_ref[t, :, cols]), kk[:, cols])
                s = jnp.where(mask2, s, NEG)
                m = jnp.max(s, axis=-1, keepdims=True)
                e = jnp.exp(s - m)
                l = jnp.sum(e, axis=-1, keepdims=True)
                pr = (e * (1.0 / l)).astype(BF16)
                o_ref[t, :, cols] = _dot(jnp.concatenate([pr[:QBLK], pr[QBLK:]], axis=1), _split_heads(vv[:, cols]))
                lse = m + jnp.log(l)
                lse_ref[t, :, cols] = _merge_heads(lse[:QBLK], lse[QBLK:])

    cur = lambda which: pl.BlockSpec((None, None, QB, QBLK, D_ATTN), lambda p, i: (which, p, i, 0, 0))
    prev = lambda which: pl.BlockSpec((None, None, None, QBLK, D_ATTN),
                                      lambda p, i: (which, p, jnp.maximum(QB * i - 1, 0), 0, 0))
    out = pl.BlockSpec((None, QB, QBLK, D_ATTN), lambda p, i: (p, i, 0, 0))
    return pl.pallas_call(
        body, grid=(N_PATTERNS, nb // QB), in_specs=[cur(0), prev(1), cur(1), prev(2), cur(2)], out_specs=[out, out],
        out_shape=[jax.ShapeDtypeStruct((N_PATTERNS, nb, QBLK, D_ATTN), F32)] * 2,
        compiler_params=_params("arbitrary", "arbitrary"), name=name)(qkv, qkv, qkv, qkv, qkv)


def _attn_combine(o3, lse3, B, S, *, name):
    def body(o_ref, l_ref, a_ref, lt_ref, so_ref, sl_ref, tmp_ref):
        for p in range(N_PATTERNS):
            _unpermute_in(lambda r0, n, p=p: o_ref[p, pl.ds(r0, n), :], so_ref.at[p], tmp_ref, p, S)
            _unpermute_in(lambda r0, n, p=p: l_ref[p, pl.ds(r0, n), :], sl_ref.at[p], tmp_ref, p, S)
        l0, l1, l2 = sl_ref[0], sl_ref[1], sl_ref[2]
        m = jnp.maximum(jnp.maximum(l0, l1), l2)
        w0, w1, w2 = jnp.exp(l0 - m), jnp.exp(l1 - m), jnp.exp(l2 - m)
        tot = w0 + w1 + w2
        a_ref[...] = (w0 * so_ref[0] + w1 * so_ref[1] + w2 * so_ref[2]) / tot
        lt_ref[...] = m + jnp.log(tot)

    o3 = o3.reshape(N_PATTERNS, B, S, D_ATTN)
    lse3 = lse3.reshape(N_PATTERNS, B, S, D_ATTN)
    inp = pl.BlockSpec((N_PATTERNS, None, S, LANES), lambda b, c: (0, b, 0, c))
    out = pl.BlockSpec((S, LANES), lambda b, c: (b, c))
    return pl.pallas_call(
        body, grid=(B, D_ATTN // LANES), in_specs=[inp, inp], out_specs=[out, out],
        out_shape=[jax.ShapeDtypeStruct((B * S, D_ATTN), F32)] * 2,
        scratch_shapes=[pltpu.VMEM((N_PATTERNS, S, LANES), F32)] * 2 + [pltpu.VMEM((S, LANES), F32)],
        compiler_params=_params("arbitrary", "arbitrary"), name=name)(o3, lse3)


STAT_D = 8


def _attn_bwd_prep(dattn, attn, lse, B, S, *, name):
    emat = _seg_matrix(LANES)
    ncc = D_ATTN // LANES

    def body(da_ref, a_ref, l_ref, e_ref, do_ref, st_ref, scr_ref, nat_ref, tmp_ref):
        cc = pl.program_id(1)
        da = da_ref[...]
        dsum = _seg_mean(da * a_ref[...], e_ref, 1.0)
        scr_ref[...] = da

        def put_do(p, row0, v):
            do_ref[p, row0:row0 + v.shape[0], :] = v.astype(BF16)

        _permute(scr_ref, tmp_ref, put_do)

        lane = lax.broadcasted_iota(jnp.int32, (S, LANES), 1)
        h0 = 2 * cc
        vals = ((h0, l_ref[:, 0:1]), (h0 + 1, l_ref[:, HEAD_DIM:HEAD_DIM + 1]),
                (STAT_D + h0, dsum[:, 0:1]), (STAT_D + h0 + 1, dsum[:, HEAD_DIM:HEAD_DIM + 1]))
        tile = jnp.where(cc == 0, 0.0, nat_ref[...])
        for at, col in vals:
            tile = jnp.where(lane == at, col, tile)
        nat_ref[...] = tile

        @pl.when(cc == ncc - 1)
        def _():
            def put_st(p, row0, v):
                st_ref[p, row0:row0 + v.shape[0], :] = v

            _permute(nat_ref, tmp_ref, put_st)

    inp = pl.BlockSpec((S, LANES), lambda b, c: (b, c))
    return pl.pallas_call(
        body, grid=(B, ncc),
        in_specs=[inp, inp, inp, pl.BlockSpec((LANES, LANES), lambda b, c: (0, 0))],
        out_specs=[pl.BlockSpec((N_PATTERNS, None, S, LANES), lambda b, c: (0, b, 0, c)),
                   pl.BlockSpec((N_PATTERNS, None, S, LANES), lambda b, c: (0, b, 0, 0))],
        out_shape=[jax.ShapeDtypeStruct((N_PATTERNS, B, S, D_ATTN), BF16),
                   jax.ShapeDtypeStruct((N_PATTERNS, B, S, LANES), F32)],
        scratch_shapes=[pltpu.VMEM((S, LANES), F32)] * 3,
        compiler_params=_params("arbitrary", "arbitrary"), name=name)(dattn, attn, lse, emat)


def _attn_bwd(qkv, do3, st3, *, name, phase=None):
    nb = qkv.shape[2]
    ngroups = nb // QB

    def body(q_ref, kp_ref, kc_ref, vp_ref, vc_ref, do_ref, st_ref, out_ref, carry_ref):
        p = pl.program_id(0)
        i = pl.program_id(1)

        @pl.when((p == 0) & (i == 0))
        def _():
            carry_ref[...] = jnp.zeros_like(carry_ref)

        kall = jnp.concatenate([kp_ref[...]] + [kc_ref[t] for t in range(QB)], axis=0)
        vall = jnp.concatenate([vp_ref[...]] + [vc_ref[t] for t in range(QB)], axis=0)

        def block_grads(t, hp):
            cols = slice(hp * LANES, (hp + 1) * LANES)
            h0, h1 = 2 * hp, 2 * hp + 1
            mask = _band_mask(p, QB * i + t) & (i < ngroups)
            mask2 = jnp.concatenate([mask, mask], axis=0)
            kh, vh = kall[t * QBLK:(t + 2) * QBLK, cols], vall[t * QBLK:(t + 2) * QBLK, cols]
            q2 = _split_heads(q_ref[t, :, cols])
            do2 = _split_heads(do_ref[t, :, cols])
            lse = jnp.concatenate([st_ref[t, :, h0:h0 + 1], st_ref[t, :, h1:h1 + 1]], axis=0)
            dsum = jnp.concatenate([st_ref[t, :, STAT_D + h0:STAT_D + h0 + 1],
                                    st_ref[t, :, STAT_D + h1:STAT_D + h1 + 1]], axis=0)
            s = _dot_nt(q2, kh)
            pr = jnp.where(mask2, jnp.exp(s - lse), 0.0)
            dp = _dot_nt(do2, vh)
            ds = (pr * (dp - dsum)).astype(BF16)
            dq = _dot(jnp.concatenate([ds[:QBLK], ds[QBLK:]], axis=1), _split_heads(kh))
            return dq, _dot_tn(ds, q2), _dot_tn(pr.astype(BF16), do2)

        for hp in range(HEADS // 2):
            cols = slice(hp * LANES, (hp + 1) * LANES)
            dq, dk, dv = block_grads(0, hp)
            for c in range(3):
                for t in range(QB):
                    v = carry_ref[c, t, :, cols]
                    if t == QB - 1 and c > 0:
                        v = v + (dk if c == 1 else dv)[:QBLK]
                    out_ref[c, t, :, cols] = v.astype(BF16)
            for t in range(QB):
                if t > 0:
                    dq, dk, dv = block_grads(t, hp)
                    carry_ref[1, t - 1, :, cols] += dk[:QBLK]
                    carry_ref[2, t - 1, :, cols] += dv[:QBLK]
                carry_ref[0, t, :, cols] = dq
                carry_ref[1, t, :, cols] = dk[QBLK:]
                carry_ref[2, t, :, cols] = dv[QBLK:]

    group = lambda i: jnp.minimum(i, ngroups - 1)
    cur = lambda which: pl.BlockSpec((None, None, QB, QBLK, D_ATTN), lambda p, i: (which, p, group(i), 0, 0))
    prev = lambda which: pl.BlockSpec((None, None, None, QBLK, D_ATTN),
                                      lambda p, i: (which, p, jnp.maximum(QB * group(i) - 1, 0), 0, 0))
    aux = lambda lanes: pl.BlockSpec((None, QB, QBLK, lanes), lambda p, i: (p, group(i), 0, 0))
    return _call(
        body, grid=(N_PATTERNS, ngroups + 1),
        in_specs=[cur(0), prev(1), cur(1), prev(2), cur(2), aux(D_ATTN), aux(LANES)],
        out_specs=[pl.BlockSpec((3, None, QB, QBLK, D_ATTN), lambda p, i: (0, p, jnp.maximum(i - 1, 0), 0, 0))],
        out_shape=[jax.ShapeDtypeStruct((3, N_PATTERNS, nb, QBLK, D_ATTN), BF16)],
        scratch_shapes=[pltpu.VMEM((3, QB, QBLK, D_ATTN), F32)],
        sem=("arbitrary", "arbitrary"), name=name, args=(qkv, qkv, qkv, qkv, qkv, do3, st3), phase=phase)


def _attn_grad_combine(cur, u, gains, B, S, *, name, phase=None):
    emat = _seg_matrix(LANES)

    def body(cur_ref, u_ref, g_ref, e_ref, du_ref, dg_ref, scr_ref, tmp_ref):
        c = pl.program_id(0)
        b = pl.program_id(2)
        for p in range(N_PATTERNS):
            _unpermute_in(lambda r0, n, p=p: cur_ref[p, pl.ds(r0, n), :].astype(F32), scr_ref.at[p], tmp_ref, p, S)
        dy = scr_ref[0] + scr_ref[1] + scr_ref[2]
        xv = u_ref[...]
        gain = g_ref[...]
        ms = _seg_mean(xv * xv, e_ref, HEAD_DIM)
        r = lax.rsqrt(ms + EPS)
        xhat = xv * r
        dxh = dy * gain
        dx = r * (dxh - xhat * _seg_mean(dxh * xhat, e_ref, HEAD_DIM))
        du_ref[...] = jnp.where(c < 2, dx, dy).astype(BF16)

        @pl.when((b == 0))
        def _():
            dg_ref[...] = jnp.zeros_like(dg_ref)

        dg_ref[...] += jnp.sum(dy * xhat, axis=0, keepdims=True)

    cur = cur.reshape(3, N_PATTERNS, B, S, D_ATTN)
    ncc = D_ATTN // LANES
    return _call(
        body, grid=(3, ncc, B),
        in_specs=[pl.BlockSpec((None, N_PATTERNS, None, S, LANES), lambda c, cc, b: (c, 0, b, 0, cc)),
                  pl.BlockSpec((S, LANES), lambda c, cc, b: (b, c * ncc + cc)),
                  pl.BlockSpec((None, 1, LANES), lambda c, cc, b: (c, 0, cc)),
                  pl.BlockSpec((LANES, LANES), lambda c, cc, b: (0, 0))],
        out_specs=[pl.BlockSpec((S, LANES), lambda c, cc, b: (b, c * ncc + cc)),
                   pl.BlockSpec((None, 1, LANES), lambda c, cc, b: (c, 0, cc))],
        out_shape=[jax.ShapeDtypeStruct((B * S, 3 * D_ATTN), BF16), jax.ShapeDtypeStruct((3, 1, D_ATTN), F32)],
        scratch_shapes=[pltpu.VMEM((N_PATTERNS, S, LANES), F32), pltpu.VMEM((S, LANES), F32)],
        sem=("arbitrary", "arbitrary", "arbitrary"), name=name, args=(cur, u, gains, emat), phase=phase)


HALO = 32
SUB = 64
SUBLANES = 8


def _shifted_copies(src_ref, sh_ref, tc):
    sh_ref[0] = src_ref[...]
    for r in range(1, SUBLANES):
        sh_ref[r, 0:tc + HALO - SUBLANES, :] = src_ref[pl.ds(r, tc + HALO - SUBLANES), :]


def _shifted(sh_ref, start, size):
    return sh_ref[start % SUBLANES, pl.ds(start - start % SUBLANES, size), :]


def _conv_fwd(u, cw, cb, lg, lb, B, S, *, tc, name):
    nchunk = S // tc
    hb = tc // HALO

    def body(ca_ref, cap_ref, cg_ref, cgp_ref, w_ref, cb_ref, lg_ref, lb_ref, cv_ref, glu_ref, y_ref, pad_ref, sh_ref):
        i = pl.program_id(1)
        glu = ca_ref[...] * _sigmoid(cg_ref[...])
        glu_ref[...] = glu
        prev = cap_ref[...] * _sigmoid(cgp_ref[...])
        pad_ref[0:HALO, :] = jnp.where(i > 0, prev, 0.0)
        pad_ref[HALO:, :] = glu
        _shifted_copies(pad_ref, sh_ref, tc)
        for sub in range(tc // SUB):
            acc = jnp.zeros((SUB, D_CONV), F32) + cb_ref[...]
            for k in range(CONV_K):
                acc = acc + _shifted(sh_ref, sub * SUB + HALO - (CONV_K - 1) + k, SUB) * w_ref[pl.ds(k, 1), :]
            y_ref[sub * SUB:(sub + 1) * SUB, :] = acc
        y = y_ref[...]
        mu = jnp.mean(y, axis=-1, keepdims=True)
        yc = y - mu
        var = jnp.mean(yc * yc, axis=-1, keepdims=True)
        z = yc * lax.rsqrt(var + EPS) * lg_ref[...] + lb_ref[...]
        cv_ref[...] = (z * _sigmoid(z)).astype(BF16)

    def cur(col):
        return pl.BlockSpec((tc, D_CONV), lambda b, i: (b * nchunk + i, col))

    def halo(col):
        return pl.BlockSpec((HALO, D_CONV), lambda b, i: (jnp.maximum((b * nchunk + i) * hb - 1, 0), col))

    vec = pl.BlockSpec((1, D_CONV), lambda b, i: (0, 0))
    out = pl.BlockSpec((tc, D_CONV), lambda b, i: (b * nchunk + i, 0))
    return pl.pallas_call(
        body, grid=(B, nchunk),
        in_specs=[cur(3), halo(3), cur(4), halo(4), pl.BlockSpec((CONV_K, D_CONV), lambda b, i: (0, 0)), vec, vec, vec],
        out_specs=[out, out, out],
        out_shape=[jax.ShapeDtypeStruct((B * S, D_CONV), BF16), jax.ShapeDtypeStruct((B * S, D_CONV), F32),
                   jax.ShapeDtypeStruct((B * S, D_CONV), F32)],
        scratch_shapes=[pltpu.VMEM((tc + HALO, D_CONV), F32), pltpu.VMEM((SUBLANES, tc + HALO, D_CONV), F32)],
        compiler_params=_params("arbitrary", "arbitrary"), name=name)(u, u, u, u, cw, cb, lg, lb)


def _conv_bwd_norm(dcv, y, lg, lb, *, tc, name):
    T = y.shape[0]

    def body(dcv_ref, y_ref, lg_ref, lb_ref, dy_ref, part_ref):
        yv = y_ref[...]
        mu = jnp.mean(yv, axis=-1, keepdims=True)
        yc = yv - mu
        var = jnp.mean(yc * yc, axis=-1, keepdims=True)
        rstd = lax.rsqrt(var + EPS)
        xhat = yc * rstd
        z = xhat * lg_ref[...] + lb_ref[...]
        sig = _sigmoid(z)
        dz = dcv_ref[...] * (sig * (1.0 + z * (1.0 - sig)))
        dxh = dz * lg_ref[...]
        dy = rstd * (dxh - jnp.mean(dxh, axis=-1, keepdims=True)
                     - xhat * jnp.mean(dxh * xhat, axis=-1, keepdims=True))
        dy_ref[...] = dy

        @pl.when(pl.program_id(0) == 0)
        def _():
            part_ref[...] = jnp.zeros_like(part_ref)

        part_ref[0:1, :] += jnp.sum(dz * xhat, axis=0, keepdims=True)
        part_ref[1:2, :] += jnp.sum(dz, axis=0, keepdims=True)
        part_ref[2:3, :] += jnp.sum(dy, axis=0, keepdims=True)

    tok = pl.BlockSpec((tc, D_CONV), lambda i: (i, 0))
    vec = pl.BlockSpec((1, D_CONV), lambda i: (0, 0))
    return pl.pallas_call(
        body, grid=(T // tc,), in_specs=[tok, tok, vec, vec],
        out_specs=[tok, pl.BlockSpec((8, D_CONV), lambda i: (0, 0))],
        out_shape=[jax.ShapeDtypeStruct((T, D_CONV), F32), jax.ShapeDtypeStruct((8, D_CONV), F32)],
        compiler_params=_params("arbitrary"), name=name)(dcv, y, lg, lb)


def _conv_bwd_taps(dy, glu, u, cw, B, S, *, tc, name, phase=None):
    nchunk = S // tc
    hb = tc // HALO
    last_hb = B * S // HALO - 1

    def body(dy_ref, dyn_ref, glu_ref, glup_ref, ca_ref, cg_ref, w_ref, dca_ref, dcg_ref, dw_ref,
             dyp_ref, glp_ref, acc_ref, shd_ref, shg_ref):
        b = pl.program_id(0)
        i = pl.program_id(1)
        dy = dy_ref[...]
        dyp_ref[0:tc, :] = dy
        dyp_ref[tc:, :] = jnp.where(i < nchunk - 1, dyn_ref[...], 0.0)
        glp_ref[0:HALO, :] = jnp.where(i > 0, glup_ref[...], 0.0)
        glp_ref[HALO:, :] = glu_ref[...]
        _shifted_copies(dyp_ref, shd_ref, tc)
        _shifted_copies(glp_ref, shg_ref, tc)

        @pl.when((b == 0) & (i == 0))
        def _():
            dw_ref[...] = jnp.zeros_like(dw_ref)

        for sub in range(tc // SUB):
            acc = jnp.zeros((SUB, D_CONV), F32)
            for k in range(CONV_K):
                acc = acc + _shifted(shd_ref, sub * SUB + (CONV_K - 1) - k, SUB) * w_ref[pl.ds(k, 1), :]
            acc_ref[sub * SUB:(sub + 1) * SUB, :] = acc
        for k in range(CONV_K):
            dw_ref[k:k + 1, :] += jnp.sum(dy * _shifted(shg_ref, HALO - (CONV_K - 1) + k, tc), axis=0, keepdims=True)
        dglu = acc_ref[...]
        ca = ca_ref[...]
        sig = _sigmoid(cg_ref[...])
        dca_ref[...] = (dglu * sig).astype(BF16)
        dcg_ref[...] = (dglu * ca * sig * (1.0 - sig)).astype(BF16)

    tok = pl.BlockSpec((tc, D_CONV), lambda b, i: (b * nchunk + i, 0))
    nxt = pl.BlockSpec((HALO, D_CONV), lambda b, i: (jnp.minimum((b * nchunk + i + 1) * hb, last_hb), 0))
    prv = pl.BlockSpec((HALO, D_CONV), lambda b, i: (jnp.maximum((b * nchunk + i) * hb - 1, 0), 0))
    return _call(
        body, grid=(B, nchunk),
        in_specs=[tok, nxt, tok, prv,
                  pl.BlockSpec((tc, D_CONV), lambda b, i: (b * nchunk + i, 3)),
                  pl.BlockSpec((tc, D_CONV), lambda b, i: (b * nchunk + i, 4)),
                  pl.BlockSpec((CONV_K, D_CONV), lambda b, i: (0, 0))],
        out_specs=[tok, tok, pl.BlockSpec((32, D_CONV), lambda b, i: (0, 0))],
        out_shape=[jax.ShapeDtypeStruct((B * S, D_CONV), BF16), jax.ShapeDtypeStruct((B * S, D_CONV), BF16),
                   jax.ShapeDtypeStruct((32, D_CONV), F32)],
        scratch_shapes=[pltpu.VMEM((tc + HALO, D_CONV), F32), pltpu.VMEM((tc + HALO, D_CONV), F32),
                        pltpu.VMEM((tc, D_CONV), F32), pltpu.VMEM((SUBLANES, tc + HALO, D_CONV), F32),
                        pltpu.VMEM((SUBLANES, tc + HALO, D_CONV), F32)],
        sem=("arbitrary", "arbitrary"), name=name, args=(dy, dy, glu, glu, u, u, cw), phase=phase)


def _outproj_fwd(h, attn, cv, wout, *, tm, name):
    T, D = h.shape

    def body(h_ref, a_ref, c_ref, w_ref, o_ref):
        o_ref[...] = (h_ref[...] + _dot(a_ref[...].astype(BF16), w_ref[0:D_ATTN, :])
                      + _dot(c_ref[...], w_ref[D_ATTN:, :]))

    tok = pl.BlockSpec((tm, D), lambda i: (i, 0))
    half = pl.BlockSpec((tm, D_ATTN), lambda i: (i, 0))
    return pl.pallas_call(
        body, grid=(T // tm,), in_specs=[tok, half, half, pl.BlockSpec(wout.shape, lambda i: (0, 0))],
        out_specs=tok, out_shape=jax.ShapeDtypeStruct((T, D), F32),
        compiler_params=_params("arbitrary"), name=name)(h, attn, cv, wout)


def _outproj_bwd(dh, attn, cv, wout, *, tm, name):
    T, D = dh.shape

    def body(dh_ref, a_ref, c_ref, w_ref, da_ref, dc_ref, dw_ref):
        @pl.when(pl.program_id(0) == 0)
        def _():
            dw_ref[...] = jnp.zeros_like(dw_ref)

        dhb = dh_ref[...].astype(BF16)
        da_ref[...] = _dot_nt(dhb, w_ref[0:D_ATTN, :])
        dc_ref[...] = _dot_nt(dhb, w_ref[D_ATTN:, :])
        dw_ref[0:D_ATTN, :] += _dot_tn(a_ref[...].astype(BF16), dhb)
        dw_ref[D_ATTN:, :] += _dot_tn(c_ref[...], dhb)

    tok = pl.BlockSpec((tm, D), lambda i: (i, 0))
    half = pl.BlockSpec((tm, D_ATTN), lambda i: (i, 0))
    wspec = pl.BlockSpec(wout.shape, lambda i: (0, 0))
    return pl.pallas_call(
        body, grid=(T // tm,), in_specs=[tok, half, half, wspec], out_specs=[half, half, wspec],
        out_shape=[jax.ShapeDtypeStruct((T, D_ATTN), F32), jax.ShapeDtypeStruct((T, D_ATTN), F32),
                   jax.ShapeDtypeStruct(wout.shape, F32)],
        compiler_params=_params("arbitrary"), name=name)(dh, attn, cv, wout)


ADAM_BLOCK_BYTES = 3 * 512 * 1024


def _adamw(w, g, m, v, *, name):
    R, C = w.shape
    tr = R
    for cand in (512, 352, 256, 176, 128, 64, 32, 16, 8):
        if R % cand == 0 and cand * C * 4 <= ADAM_BLOCK_BYTES:
            tr = cand
            break
    c1 = 1.0 - ADAM_B1 ** ADAM_STEP
    c2 = 1.0 - ADAM_B2 ** ADAM_STEP

    def body(w_ref, g_ref, m_ref, v_ref, d_ref, nm_ref, nv_ref):
        gv = g_ref[...]
        nm = ADAM_B1 * m_ref[...] + (1.0 - ADAM_B1) * gv
        nv = ADAM_B2 * v_ref[...] + (1.0 - ADAM_B2) * (gv * gv)
        d_ref[...] = -ADAM_LR * ((nm / c1) / (jnp.sqrt(nv / c2) + ADAM_EPS) + ADAM_WD * w_ref[...])
        nm_ref[...] = nm
        nv_ref[...] = nv

    blk = pl.BlockSpec((tr, C), lambda i: (i, 0))
    return pl.pallas_call(
        body, grid=(R // tr,), in_specs=[blk] * 4, out_specs=[blk] * 3,
        out_shape=[jax.ShapeDtypeStruct((R, C), F32)] * 3,
        compiler_params=_params("arbitrary"), name=name)(w, g, m, v)


TM = 512
TM_WIDE = 1024
TK = 1024
TC = 256


def _local_step(x, tgt, w, overlap=None):
    B, S, D = x.shape
    T = B * S
    x2 = x.reshape(T, D)
    t2 = tgt.reshape(T, D)
    ones = jnp.ones((1, D_ATTN), F32)
    scale = HEAD_DIM ** -0.5
    gains = jnp.stack([jnp.tile(w["q_norm"], (1, HEADS)) * scale, jnp.tile(w["k_norm"], (1, HEADS)), ones])
    g = {}

    def hosting(point, build):
        phase = overlap.phase(point, w, g) if overlap is not None else None
        if phase is None:
            return build(None)
        outs, extra = build(phase)
        overlap.done(point, extra, w, g)
        return outs

    h1, n1, G1, U1 = hosting("ffn1_fwd", lambda ph: _ffn_fwd(
        x2, w["ffn1_norm"], w["wg1"], w["wu1"], w["wd1"], None, tm=TM_WIDE, name="ffn1_fwd", phase=ph))
    u, n2 = hosting("inproj_fwd", lambda ph: _inproj_fwd(h1, w["mix_norm"], w["win"], tm=TM_WIDE, name="inproj_fwd", phase=ph))
    qkv = _qkv_prep(u, gains, B, S, name="qkv_prep")
    qkv = qkv.reshape(3, N_PATTERNS, T // QBLK, QBLK, D_ATTN)
    o3, lse3 = _attn_fwd(qkv, name="attn_fwd")
    attn, lse = _attn_combine(o3, lse3, B, S, name="attn_combine")
    cv, glu, yconv = _conv_fwd(u, w["conv_w"], w["conv_b"], w["conv_ln_g"], w["conv_ln_b"], B, S, tc=TC, name="conv_fwd")
    h2 = _outproj_fwd(h1, attn, cv, w["wout"], tm=TM, name="outproj_fwd")
    dh3, n3, G2, U2, loss = _ffn_fwd(h2, w["ffn2_norm"], w["wg2"], w["wu2"], w["wd2"], t2, tm=TM_WIDE, name="ffn2_fwd")

    dG, dU, A, dy, dh2, g["ffn2_norm"] = _ffn_bwd_act(dh3, h2, w["ffn2_norm"], G2, U2, w["wg2"], w["wu2"], w["wd2"],
                                                    tm=TM, name="ffn2_bwd_act")
    g["wg2"], g["wu2"], g["wd2"] = _ffn_bwd_w(n3, dy, dG, dU, A, tk=TK, name="ffn2_bwd_w")
    dattn, dcv, g["wout"] = _outproj_bwd(dh2, attn, cv, w["wout"], tm=TM, name="outproj_bwd")
    dyc, cpart = _conv_bwd_norm(dcv, yconv, w["conv_ln_g"], w["conv_ln_b"], tc=TC, name="conv_bwd_norm")
    dca, dcg, dcw = hosting("conv_bwd_taps", lambda ph: _conv_bwd_taps(
        dyc, glu, u, w["conv_w"], B, S, tc=TC, name="conv_bwd_taps", phase=ph))
    do3, st3 = _attn_bwd_prep(dattn, attn, lse, B, S, name="attn_bwd_prep")
    nb = T // QBLK
    (cur,) = hosting("attn_bwd", lambda ph: _attn_bwd(
        qkv, do3.reshape(N_PATTERNS, nb, QBLK, D_ATTN), st3.reshape(N_PATTERNS, nb, QBLK, LANES),
        name="attn_bwd", phase=ph))
    du_qkv, dgains = hosting("attn_grad_combine", lambda ph: _attn_grad_combine(
        cur, u, gains, B, S, name="attn_grad_combine", phase=ph))
    du = jnp.concatenate([du_qkv, dca, dcg], axis=1)
    (g["win"],) = hosting("inproj_bwd_w", lambda ph: _inproj_bwd_w(
        n2, du, w["win"].shape[0], tk=TK, name="inproj_bwd_w", phase=ph))
    dh1, g["mix_norm"] = hosting("inproj_bwd_act", lambda ph: _inproj_bwd_act(
        du, dh2, h1, w["mix_norm"], w["win"], tm=TM_WIDE, name="inproj_bwd_act", phase=ph))
    dG, dU, A, dy, dx, g["ffn1_norm"] = hosting("ffn1_bwd_act", lambda ph: _ffn_bwd_act(
        dh1, x2, w["ffn1_norm"], G1, U1, w["wg1"], w["wu1"], w["wd1"], tm=TM, name="ffn1_bwd_act", phase=ph))
    g["wg1"], g["wu1"], g["wd1"] = hosting("ffn1_bwd_w", lambda ph: _ffn_bwd_w(
        n1, dy, dG, dU, A, tk=TK, name="ffn1_bwd_w", phase=ph))

    g["q_norm"] = dgains[0].reshape(HEADS, HEAD_DIM).sum(axis=0, keepdims=True) * scale
    g["k_norm"] = dgains[1].reshape(HEADS, HEAD_DIM).sum(axis=0, keepdims=True)
    g["conv_ln_g"] = cpart[0:1]
    g["conv_ln_b"] = cpart[1:2]
    g["conv_b"] = cpart[2:3]
    g["conv_w"] = dcw[:CONV_K]
    return loss, dx.reshape(B, S, D), g


N_CHIPS = 4
N_DEV = 8
VMEM_SPEC = pl.BlockSpec(memory_space=pltpu.VMEM)


def _remote(src, dst, send_sem, recv_sem, device):
    return pltpu.make_async_remote_copy(src_ref=src, dst_ref=dst, send_sem=send_sem, recv_sem=recv_sem,
                                        device_id=device, device_id_type=MESH)


def _stage_shards(shards, dtypes, *, name):
    n = len(shards)
    halves = [s.reshape(2, s.shape[0] // 2, s.shape[1]) for s in shards]

    def body(*refs):
        ins, outs, vms, loc_sems = refs[:n], refs[n:2 * n], refs[2 * n:3 * n], refs[3 * n]
        me = 2 * lax.axis_index("x") + lax.axis_index("y")
        copies = []
        for a in range(n):
            vms[a][...] = ins[a][...].astype(dtypes[a])
            cp = pltpu.make_async_copy(vms[a], outs[a].at[me], loc_sems.at[a])
            cp.start()
            copies.append(cp)
        for cp in copies:
            cp.wait()

    return pl.pallas_call(
        body, in_specs=[VMEM_SPEC] * n, out_specs=[ANY] * n,
        out_shape=[jax.ShapeDtypeStruct((N_CHIPS,) + h.shape, dt) for h, dt in zip(halves, dtypes)],
        scratch_shapes=[pltpu.VMEM(h.shape, dt) for h, dt in zip(halves, dtypes)] + [DMA_SEMS((n,))],
        compiler_params=pltpu.CompilerParams(vmem_limit_bytes=VMEM_LIMIT), name=name)(*halves)


def _like(arrays):
    return [jax.ShapeDtypeStruct(a.shape, a.dtype) for a in arrays]


def _axes():
    x, y, c = lax.axis_index("x"), lax.axis_index("y"), lax.axis_index("c")
    first = (x + (1 - c) * (1 - 2 * x), y + c * (1 - 2 * y))
    second = (x + c * (1 - 2 * x), y + (1 - c) * (1 - 2 * y))
    slots = tuple(2 * px + py for px, py in ((x, y), first, second, (1 - x, 1 - y)))
    return (x, y, c), (*first, c), (*second, c), slots


def _gather_ici_phase(bufs):
    n = len(bufs)

    def stage1(ins, outs, sems):
        (x, y, c), peer1, peer2, (own, s1, s2, both) = _axes()
        starts, arrivals = [], []
        for a in range(n):
            mine, land = outs[a].at[own, c], outs[a].at[s2, c]
            starts.append(_remote(mine, mine, *sems(a), peer2))
            arrivals.append(_remote(land, land, *sems(a), peer2))
        return starts, arrivals

    def stage2(ins, outs, sems):
        (x, y, c), peer1, peer2, (own, s1, s2, both) = _axes()
        starts, arrivals = [], []
        for a in range(n):
            for k, (src, dst) in enumerate(((own, s1), (s2, both))):
                mine, land = outs[a].at[src, c], outs[a].at[dst, c]
                starts.append(_remote(mine, mine, *sems(2 * a + k), peer1))
                arrivals.append(_remote(land, land, *sems(2 * a + k), peer1))
        return starts, arrivals

    same = {a: a for a in range(n)}
    return _Phase(bufs, _like(bufs), same, n, stage1).then(_Phase(bufs, _like(bufs), same, 2 * n, stage2))


def _gather_d2d_phase(bufs):
    n = len(bufs)

    def copies(ins, outs, sems):
        (x, y, c), peer1, peer2, (own, s1, s2, both) = _axes()
        starts, arrivals = [], []
        for a in range(n):
            for j, s in enumerate((s1, s2, both)):
                got, land = outs[a].at[s, c], outs[a].at[s, 1 - c]
                starts.append(_remote(got, got, *sems(3 * a + j), (x, y, 1 - c)))
                arrivals.append(_remote(land, land, *sems(3 * a + j), (x, y, 1 - c)))
        return starts, arrivals

    return _Phase(bufs, _like(bufs), {a: a for a in range(n)}, 3 * n, copies)


def _exchange_phase(views):
    n = len(views)

    def copies(ins, outs, sems):
        x, y, c = lax.axis_index("x"), lax.axis_index("y"), lax.axis_index("c")
        starts = [_remote(ins[a].at[pl.ds(0, ins[a].shape[0]), 1 - c], outs[a], *sems(a), (x, y, 1 - c))
                  for a in range(n)]
        return starts, starts

    outs = [jax.ShapeDtypeStruct((v.shape[0],) + v.shape[2:], F32) for v in views]
    return _Phase(views, outs, {}, n, copies)


def _row_block(rows):
    for cand in (256, 176, 128, 64, 32, 16, 8):
        if rows % cand == 0:
            return cand
    return rows


def _add_own_half(g, r, sel, *, name):
    ns, _, rh, cdim = g.shape
    tr = _row_block(rh)

    def body(s_ref, gk_ref, rk_ref, gs_ref, rs_ref, keep_ref, send_ref):
        keep_ref[...] = gk_ref[...] + rk_ref[...]
        send_ref[...] = (gs_ref[...] + rs_ref[...]).astype(BF16)

    def g_spec(off):
        return pl.BlockSpec((None, None, tr, cdim), lambda k, i, s: (s[1 + off + k], s[0], i, 0))

    def r_spec(off):
        return pl.BlockSpec((None, tr, cdim), lambda k, i, s: (s[1 + off + k], i, 0))

    out = pl.BlockSpec((None, tr, cdim), lambda k, i, s: (k, i, 0))
    return pl.pallas_call(
        body,
        grid_spec=pltpu.PrefetchScalarGridSpec(
            num_scalar_prefetch=1, grid=(2, rh // tr),
            in_specs=[g_spec(0), r_spec(0), g_spec(2), r_spec(2)], out_specs=[out, out]),
        out_shape=[jax.ShapeDtypeStruct((2, rh, cdim), F32), jax.ShapeDtypeStruct((2, rh, cdim), BF16)],
        compiler_params=_params("arbitrary", "arbitrary"), name=name)(sel, g, r, g, r)


def _swap_phase(arrays, stage):
    n = len(arrays)

    def copies(ins, outs, sems):
        peer = _axes()[stage]
        starts = [_remote(ins[a], outs[a], *sems(a), peer) for a in range(n)]
        return starts, starts

    return _Phase(arrays, _like(arrays), {}, n, copies)


def _add_stage1(keep, got, *, name):
    _, rh, cdim = keep.shape
    tr = _row_block(rh)

    def body(k_ref, g_ref, keep_ref, send_ref):
        keep_ref[...] = k_ref[0] + g_ref[0].astype(F32)
        send_ref[...] = (k_ref[1] + g_ref[1].astype(F32)).astype(BF16)

    blk2 = pl.BlockSpec((2, tr, cdim), lambda i: (0, i, 0))
    blk = pl.BlockSpec((tr, cdim), lambda i: (i, 0))
    return pl.pallas_call(
        body, grid=(rh // tr,), in_specs=[blk2, blk2], out_specs=[blk, blk],
        out_shape=[jax.ShapeDtypeStruct((rh, cdim), F32), jax.ShapeDtypeStruct((rh, cdim), BF16)],
        compiler_params=_params("arbitrary"), name=name)(keep, got)


def _add_stage2(keep, got, sel, *, name):
    rh, cdim = keep.shape
    tr = _row_block(rh)

    def body(s_ref, k_ref, g_ref, o_ref):
        o_ref[...] = k_ref[...] + g_ref[...].astype(F32)

    blk = pl.BlockSpec((tr, cdim), lambda i, s: (i, 0))
    return pl.pallas_call(
        body,
        grid_spec=pltpu.PrefetchScalarGridSpec(
            num_scalar_prefetch=1, grid=(rh // tr,), in_specs=[blk, blk],
            out_specs=pl.BlockSpec((None, tr, cdim), lambda i, s: (s[0], i, 0))),
        out_shape=jax.ShapeDtypeStruct((2, rh, cdim), F32),
        compiler_params=_params("arbitrary"), name=name)(sel, keep, got)


def _join_phase(halves):
    n = len(halves)

    def copies(ins, outs, sems):
        x, y, c = lax.axis_index("x"), lax.axis_index("y"), lax.axis_index("c")
        starts, arrivals = [], []
        for a in range(n):
            mine, land = outs[a].at[c], outs[a].at[1 - c]
            starts.append(_remote(mine, mine, *sems(a), (x, y, 1 - c)))
            arrivals.append(_remote(land, land, *sems(a), (x, y, 1 - c)))
        return starts, arrivals

    return _Phase(halves, _like(halves), {a: a for a in range(n)}, n, copies)


def _slot_order():
    x, y, c = lax.axis_index("x"), lax.axis_index("y"), lax.axis_index("c")
    own, flip_x, flip_y, both = 2 * x + y, 2 * (1 - x) + y, 2 * x + 1 - y, 2 * (1 - x) + 1 - y
    first = jnp.where(c == 0, flip_x, flip_y)
    second = jnp.where(c == 0, flip_y, flip_x)
    return jnp.stack([c, own, second, first, both]).astype(jnp.int32)


def _reduce_scatter(grads, join_also=()):
    sel = _slot_order()
    views = [_half_view(g) for g in grads]
    got = _run_phase(_exchange_phase(views), name="rs_exchange_halves")
    keep, send = _add_halves(views, got, sel, "late")
    got = _run_phase(_swap_phase(send, 1), name="rs_swap_first_axis")
    keep, send = _add_first(keep, got, "late")
    got = _run_phase(_swap_phase(send, 2), name="rs_swap_second_axis")
    halves = _add_second(keep, got, sel, "late")
    full = _run_phase(_join_phase(halves + list(join_also)), name="rs_join_halves")
    return [f.reshape(-1, f.shape[-1]) for f in full]


def _half_view(g):
    return g.reshape(N_CHIPS, 2, g.shape[1] // 2, g.shape[2])


def _add_halves(views, got, sel, tag):
    keep, send = zip(*[_add_own_half(v, r, sel, name=f"rs_add_half_{tag}{a}") for a, (v, r) in enumerate(zip(views, got))])
    return list(keep), list(send)


def _add_first(keep, got, tag):
    keep, send = zip(*[_add_stage1(k, r, name=f"rs_add_first_{tag}{a}") for a, (k, r) in enumerate(zip(keep, got))])
    return list(keep), list(send)


def _add_second(keep, got, sel, tag):
    return [_add_stage2(k, r, sel, name=f"rs_add_second_{tag}{a}") for a, (k, r) in enumerate(zip(keep, got))]


EARLY_GRADS = ("wg2", "wu2", "wd2", "wout")
MIDDLE_GRADS = ("win",)


class _Overlap:
    EARLY_AT = ("conv_bwd_taps", "attn_bwd", "attn_grad_combine", "inproj_bwd_w")
    MIDDLE_AT = ("inproj_bwd_act", "ffn1_bwd_act", "ffn1_bwd_w", None)

    def __init__(self, staged):
        self.staged = staged
        sel = _slot_order()
        self.early = _Reduction(EARLY_GRADS, "early", sel)
        self.middle = _Reduction(MIDDLE_GRADS, "middle", sel)

    def phase(self, point, w, g):
        if point == "ffn1_fwd":
            return _gather_ici_phase(self.staged)
        if point == "inproj_fwd":
            return _gather_d2d_phase(self.ffn2)
        for red, at in ((self.early, self.EARLY_AT), (self.middle, self.MIDDLE_AT)):
            if point in at:
                return red.phase(at.index(point), g)
        return None

    def done(self, point, outs, w, g):
        if point == "ffn1_fwd":
            win, wout, taps = [_whole(b) for b in _run_phase(_gather_d2d_phase(outs[:3]), name="gather_mix_d2d")]
            w["win"] = win
            w["wout"] = wout.reshape(-1, wout.shape[-1])
            w["conv_w"] = taps.transpose(1, 0, 2).reshape(CONV_K + 1, D_CONV)[:CONV_K]
            self.ffn2 = list(outs[3:])
        elif point == "inproj_fwd":
            w["wg2"], w["wu2"], w["wd2"] = [_whole(b) for b in outs]
        for red, at in ((self.early, self.EARLY_AT), (self.middle, self.MIDDLE_AT)):
            if point in at:
                red.done(at.index(point), outs)


class _Reduction:
    def __init__(self, names, tag, sel):
        self.names, self.tag, self.sel = names, tag, sel
        self.reduced = {}

    def phase(self, stage, g):
        if stage == 0:
            self.cols = [g[k].shape[-1] for k in self.names]
            self.views = [_half_view(g[k].reshape(N_CHIPS, -1, g[k].shape[-1])) for k in self.names]
            return _exchange_phase(self.views)
        if stage in (1, 2):
            return _swap_phase(self.send, stage)
        return _join_phase(self.halves)

    def done(self, stage, outs):
        if stage == 0:
            self.keep, self.send = _add_halves(self.views, outs, self.sel, self.tag)
        elif stage == 1:
            self.keep, self.send = _add_first(self.keep, outs, self.tag)
        elif stage == 2:
            self.halves = _add_second(self.keep, outs, self.sel, self.tag)
        else:
            for k, c, f in zip(self.names, self.cols, outs):
                self.reduced[k] = f.reshape(-1, c)


def _whole(buf):
    return buf.reshape(buf.shape[0], 2 * buf.shape[2], buf.shape[3])


def _allreduce_small(pack, *, name):
    rows = pack.shape[0]

    def body(p_ref, o_ref, buf_ref, send_sems, recv_sems):
        x, y, c = lax.axis_index("x"), lax.axis_index("y"), lax.axis_index("c")
        me = 4 * x + 2 * y + c
        buf_ref[me] = p_ref[...]
        cps = []
        for k in range(1, N_DEV):
            peer = tuple(1 - v if (k >> s) & 1 else v for v, s in ((x, 2), (y, 1), (c, 0)))
            cp = _remote(p_ref, buf_ref.at[me], send_sems.at[k - 1], recv_sems.at[k - 1], peer)
            cp.start()
            cps.append(cp)
        for k in range(1, N_DEV):
            src = 4 * (x ^ ((k >> 2) & 1)) + 2 * (y ^ ((k >> 1) & 1)) + (c ^ (k & 1))
            land = buf_ref.at[src]
            _remote(land, land, send_sems.at[k - 1], recv_sems.at[k - 1], (x, y, c)).wait_recv()
        acc = buf_ref[0]
        for d in range(1, N_DEV):
            acc = acc + buf_ref[d]
        o_ref[...] = acc
        for cp in cps:
            cp.wait_send()

    return pl.pallas_call(
        body, in_specs=[VMEM_SPEC], out_specs=VMEM_SPEC, out_shape=jax.ShapeDtypeStruct(pack.shape, F32),
        scratch_shapes=[pltpu.VMEM((N_DEV, rows, LANES), F32), pltpu.SemaphoreType.DMA((N_DEV - 1,)),
                        pltpu.SemaphoreType.DMA((N_DEV - 1,))], name=name)(pack)


SMALL = ("ffn1_norm", "mix_norm", "q_norm", "k_norm", "conv_b", "conv_ln_g", "conv_ln_b", "ffn2_norm", "conv_w")
BIG = ("ffn1_w_gate", "ffn1_w_up", "ffn1_w_down", "w_in", "w_out", "ffn2_w_gate", "ffn2_w_up", "ffn2_w_down")
TRANSPOSED = ("ffn1_w_gate", "ffn1_w_up", "ffn2_w_gate", "ffn2_w_up")
WEIGHTS = ("ffn1_norm", "ffn1_w_gate", "ffn1_w_up", "ffn1_w_down", "mix_norm", "w_in", "q_norm", "k_norm",
           "conv_w", "conv_b", "conv_ln_g", "conv_ln_b", "w_out", "ffn2_norm", "ffn2_w_gate", "ffn2_w_up",
           "ffn2_w_down")


def _pack(parts):
    rows = []
    for p in parts:
        flat = p.reshape(-1)
        tile = SUBLANES * LANES
        padded = -(-flat.shape[0] // tile) * tile
        rows.append(jnp.pad(flat, (0, padded - flat.shape[0])).reshape(-1, LANES))
    return jnp.concatenate(rows, axis=0)


def _unpack(pack, shapes):
    out, row = [], 0
    for shp in shapes:
        size = shp[0] * shp[1]
        tile = SUBLANES * LANES
        nrows = -(-size // tile) * SUBLANES
        out.append(pack[row:row + nrows].reshape(-1)[:size].reshape(shp))
        row += nrows
    return out


def kernel(x, ffn1_norm, ffn1_w_gate, ffn1_w_up, ffn1_w_down, mix_norm, w_in, q_norm, k_norm, conv_w, conv_b, conv_ln_g, conv_ln_b, w_out, ffn2_norm, ffn2_w_gate, ffn2_w_up, ffn2_w_down, loss_target, m_ffn1_norm, m_ffn1_w_gate, m_ffn1_w_up, m_ffn1_w_down, m_mix_norm, m_w_in, m_q_norm, m_k_norm, m_conv_w, m_conv_b, m_conv_ln_g, m_conv_ln_b, m_w_out, m_ffn2_norm, m_ffn2_w_gate, m_ffn2_w_up, m_ffn2_w_down, v_ffn1_norm, v_ffn1_w_gate, v_ffn1_w_up, v_ffn1_w_down, v_mix_norm, v_w_in, v_q_norm, v_k_norm, v_conv_w, v_conv_b, v_conv_ln_g, v_conv_ln_b, v_w_out, v_ffn2_norm, v_ffn2_w_gate, v_ffn2_w_up, v_ffn2_w_down):
    wts = dict(ffn1_norm=ffn1_norm, ffn1_w_gate=ffn1_w_gate[0], ffn1_w_up=ffn1_w_up[0], ffn1_w_down=ffn1_w_down[0],
               mix_norm=mix_norm, w_in=w_in[0], q_norm=q_norm, k_norm=k_norm, conv_w=conv_w[0], conv_b=conv_b,
               conv_ln_g=conv_ln_g, conv_ln_b=conv_ln_b, w_out=w_out[0], ffn2_norm=ffn2_norm,
               ffn2_w_gate=ffn2_w_gate[0], ffn2_w_up=ffn2_w_up[0], ffn2_w_down=ffn2_w_down[0])
    mom = dict(ffn1_norm=m_ffn1_norm, ffn1_w_gate=m_ffn1_w_gate[0], ffn1_w_up=m_ffn1_w_up[0], ffn1_w_down=m_ffn1_w_down[0],
               mix_norm=m_mix_norm, w_in=m_w_in[0], q_norm=m_q_norm, k_norm=m_k_norm, conv_w=m_conv_w[0], conv_b=m_conv_b,
               conv_ln_g=m_conv_ln_g, conv_ln_b=m_conv_ln_b, w_out=m_w_out[0], ffn2_norm=m_ffn2_norm,
               ffn2_w_gate=m_ffn2_w_gate[0], ffn2_w_up=m_ffn2_w_up[0], ffn2_w_down=m_ffn2_w_down[0])
    var = dict(ffn1_norm=v_ffn1_norm, ffn1_w_gate=v_ffn1_w_gate[0], ffn1_w_up=v_ffn1_w_up[0], ffn1_w_down=v_ffn1_w_down[0],
               mix_norm=v_mix_norm, w_in=v_w_in[0], q_norm=v_q_norm, k_norm=v_k_norm, conv_w=v_conv_w[0], conv_b=v_conv_b,
               conv_ln_g=v_conv_ln_g, conv_ln_b=v_conv_ln_b, w_out=v_w_out[0], ffn2_norm=v_ffn2_norm,
               ffn2_w_gate=v_ffn2_w_gate[0], ffn2_w_up=v_ffn2_w_up[0], ffn2_w_down=v_ffn2_w_down[0])
    chip = 2 * lax.axis_index("x") + lax.axis_index("y")
    for src in (wts, mom, var):
        for n in TRANSPOSED:
            src[n] = src[n].T

    taps = jnp.pad(wts["conv_w"], ((0, 1), (0, 0)))
    staged = _stage_shards([wts["ffn1_w_gate"], wts["ffn1_w_up"], wts["ffn1_w_down"], wts["w_in"], wts["w_out"], taps,
                            wts["ffn2_w_gate"], wts["ffn2_w_up"], wts["ffn2_w_down"]],
                           [BF16, BF16, BF16, BF16, BF16, F32, BF16, BF16, BF16], name="stage_shards")
    first = _run_phase(_gather_ici_phase(staged[:3]).then(_gather_d2d_phase(staged[:3])), name="gather_ffn1")
    wg1, wu1, wd1 = [_whole(b) for b in first]
    w = dict(ffn1_norm=ffn1_norm, mix_norm=mix_norm, ffn2_norm=ffn2_norm, q_norm=q_norm, k_norm=k_norm,
             conv_b=conv_b, conv_ln_g=conv_ln_g, conv_ln_b=conv_ln_b, wg1=wg1, wu1=wu1, wd1=wd1)
    overlap = _Overlap(staged[3:])
    loss_part, grad_x, g = _local_step(x, loss_target, w, overlap)

    late = _reduce_scatter([g["wg1"], g["wu1"], g["wd1"]], join_also=overlap.middle.halves)
    big_grads = dict(zip(("ffn1_w_gate", "ffn1_w_up", "ffn1_w_down", "w_in"), late))
    early = overlap.early.reduced
    big_grads.update(ffn2_w_gate=early["wg2"], ffn2_w_up=early["wu2"], ffn2_w_down=early["wd2"], w_out=early["wout"])

    small_shapes = [g[n].shape for n in SMALL] + [(SUBLANES, LANES)]
    red = _allreduce_small(_pack([g[n] for n in SMALL] + [loss_part]), name="allreduce_small")
    small = dict(zip(SMALL + ("loss",), _unpack(red, small_shapes)))
    loss = small["loss"][0, 0]
    small["conv_w"] = lax.dynamic_slice_in_dim(small["conv_w"], chip * LANES, LANES, axis=1)

    grads, delta, new_m, new_v = {}, {}, {}, {}
    for n in BIG:
        grads[n] = big_grads[n]
        delta[n], new_m[n], new_v[n] = _adamw(wts[n], grads[n], mom[n], var[n], name=f"adamw_{n}")
    shapes = [wts[n].shape for n in SMALL]
    packs = [_pack([src[n] for n in SMALL]) for src in (wts, small, mom, var)]
    outs = _adamw(*packs, name="adamw_small")
    for dst, pk in zip((delta, new_m, new_v), outs):
        dst.update(zip(SMALL, _unpack(pk, shapes)))
    for n in SMALL:
        grads[n] = small[n]

    def shaped(d, n):
        v = d[n].T if n in TRANSPOSED else d[n]
        return v.reshape((1,) + v.shape) if n in BIG or n == "conv_w" else v

    return (loss, grad_x, *[shaped(grads, n) for n in WEIGHTS], *[shaped(delta, n) for n in WEIGHTS],
            *[shaped(new_m, n) for n in WEIGHTS], *[shaped(new_v, n) for n in WEIGHTS])
```

```python
import functools

import jax
import jax.numpy as jnp
from jax import lax
from jax.experimental import pallas as pl
from jax.experimental.pallas import tpu as pltpu

F32 = jnp.float32
BF16 = jnp.bfloat16

EPS = 1e-6
HEADS = 8
HEAD_DIM = 64
D_ATTN = HEADS * HEAD_DIM
D_CONV = 512
CONV_K = 31
QBLK = 128
N_PATTERNS = 3
DILATIONS = (1, 4, 16)
LANES = 128
NEG = -1e30

ADAM_LR = 0.001
ADAM_B1 = 0.9
ADAM_B2 = 0.999
ADAM_EPS = 1e-08
ADAM_WD = 0.01
ADAM_STEP = 10

VMEM_LIMIT = 56 * 1024 * 1024
MESH = pl.DeviceIdType.MESH

NT_DIMS = (((1,), (1,)), ((), ()))
TN_DIMS = (((0,), (0,)), ((), ()))


def _params(*sem):
    return pltpu.CompilerParams(dimension_semantics=sem, vmem_limit_bytes=VMEM_LIMIT)


def _dot(a, b):
    return jnp.dot(a, b, preferred_element_type=F32)


def _dot_nt(a, b):
    return lax.dot_general(a, b, NT_DIMS, preferred_element_type=F32)


def _dot_tn(a, b):
    return lax.dot_general(a, b, TN_DIMS, preferred_element_type=F32)


def _sigmoid(x):
    return 1.0 / (1.0 + jnp.exp(-x))


def _seg_mean(v, e_ref, width):
    hi = v.astype(BF16)
    lo = (v - hi.astype(F32)).astype(BF16)
    e = e_ref[...]
    return (_dot(hi, e) + _dot(lo, e)) * (1.0 / width)


def _seg_matrix(n):
    i = jnp.arange(n)
    return (i[:, None] // HEAD_DIM == i[None, :] // HEAD_DIM).astype(BF16)


ANY = pl.BlockSpec(memory_space=pl.ANY)
DMA_SEMS = pltpu.SemaphoreType.DMA


class _Phase:
    def __init__(self, ins, outs, aliases, nsem, copies):
        self.ins, self.outs, self.aliases = list(ins), list(outs), dict(aliases)
        self.stages = [(nsem, copies)]

    def then(self, other):
        self.stages = self.stages + other.stages
        return self

    @property
    def nsem(self):
        return sum(n for n, _ in self.stages)

    def _copies(self, k, in_refs, out_refs, send_sems, recv_sems):
        base = sum(n for n, _ in self.stages[:k])
        return self.stages[k][1](in_refs, out_refs, lambda i: (send_sems.at[base + i], recv_sems.at[base + i]))

    def start(self, k, *refs):
        for cp in self._copies(k, *refs)[0]:
            cp.start()

    def finish(self, k, *refs):
        starts, arrivals = self._copies(k, *refs)
        for cp in arrivals:
            cp.wait_recv()
        for cp in starts:
            cp.wait_send()


def _run_phase(phase, *, name):
    n_in, n_out = len(phase.ins), len(phase.outs)

    def body(*refs):
        ins, outs = refs[:n_in], refs[n_in:n_in + n_out]
        send_sems, recv_sems = refs[n_in + n_out:]
        for k in range(len(phase.stages)):
            phase.start(k, ins, outs, send_sems, recv_sems)
            phase.finish(k, ins, outs, send_sems, recv_sems)

    return pl.pallas_call(
        body, in_specs=[ANY] * n_in, out_specs=[ANY] * n_out, out_shape=phase.outs,
        input_output_aliases=phase.aliases,
        scratch_shapes=[DMA_SEMS((phase.nsem,)), DMA_SEMS((phase.nsem,))], name=name)(*phase.ins)


def _call(body, *, grid, in_specs, out_specs, out_shape, scratch_shapes=(), sem, name, args, phase=None):
    in_specs, out_specs, out_shape = list(in_specs), list(out_specs), list(out_shape)
    scratch_shapes = list(scratch_shapes)
    if phase is None:
        return pl.pallas_call(body, grid=grid, in_specs=in_specs, out_specs=out_specs, out_shape=out_shape,
                              scratch_shapes=scratch_shapes, compiler_params=_params(*sem), name=name)(*args)
    n_in, n_out, n_scr = len(in_specs), len(out_specs), len(scratch_shapes)
    p_in, p_out = len(phase.ins), len(phase.outs)

    def hosted(*refs):
        ins, pins = refs[:n_in], refs[n_in:n_in + p_in]
        o0 = n_in + p_in
        outs, pouts = refs[o0:o0 + n_out], refs[o0 + n_out:o0 + n_out + p_out]
        s0 = o0 + n_out + p_out
        scr = refs[s0:s0 + n_scr]
        send_sems, recv_sems = refs[s0 + n_scr:]
        step = 0
        for d, n in enumerate(grid):
            step = step * n + pl.program_id(d)
        nsteps = functools.reduce(lambda a, b: a * b, grid)
        nstages = len(phase.stages)
        comm_refs = (pins, pouts, send_sems, recv_sems)

        for k in range(nstages):
            @pl.when(step == (k * nsteps) // nstages)
            def _(k=k):
                if k > 0:
                    phase.finish(k - 1, *comm_refs)
                phase.start(k, *comm_refs)

        body(*ins, *outs, *scr)

        @pl.when(step == nsteps - 1)
        def _():
            phase.finish(nstages - 1, *comm_refs)

    res = pl.pallas_call(
        hosted, grid=grid, in_specs=in_specs + [ANY] * p_in, out_specs=out_specs + [ANY] * p_out,
        out_shape=out_shape + phase.outs,
        input_output_aliases={n_in + i: n_out + o for i, o in phase.aliases.items()},
        scratch_shapes=scratch_shapes + [DMA_SEMS((phase.nsem,)), DMA_SEMS((phase.nsem,))],
        compiler_params=_params(*sem), name=name)(*args, *phase.ins)
    return res[:n_out], res[n_out:]


ROW_CHUNK = 256


def _ffn_fwd(x, gain, wg, wu, wd, tgt, *, tm, name, phase=None):
    T, D = x.shape
    NS, Fs, _ = wg.shape
    with_loss = tgt is not None

    def body(*refs):
        if with_loss:
            x_ref, g_ref, wg_ref, wu_ref, wd_ref, t_ref, h_ref, n_ref, G_ref, U_ref, loss_ref, acc_ref = refs
        else:
            x_ref, g_ref, wg_ref, wu_ref, wd_ref, h_ref, n_ref, G_ref, U_ref, acc_ref = refs
        i = pl.program_id(0)
        j = pl.program_id(1)

        @pl.when(j == 0)
        def _():
            xv = x_ref[...]
            r = lax.rsqrt(jnp.mean(xv * xv, axis=-1, keepdims=True) + EPS)
            n_ref[...] = (xv * r * g_ref[...]).astype(BF16)
            acc_ref[...] = jnp.zeros_like(acc_ref)

        n = n_ref[...]
        G = _dot_nt(n, wg_ref[...])
        U = _dot_nt(n, wu_ref[...])
        G_ref[...] = G.astype(BF16)
        U_ref[...] = U.astype(BF16)
        A = (G * _sigmoid(G) * U).astype(BF16)
        acc_ref[...] += _dot(A, wd_ref[...])

        @pl.when(j == NS - 1)
        def _():
            h = x_ref[...] + 0.5 * acc_ref[...]
            if with_loss:
                e = h - t_ref[...]
                h_ref[...] = e * (1.0 / D)

                @pl.when(i == 0)
                def _():
                    loss_ref[...] = jnp.zeros_like(loss_ref)

                loss_ref[...] += jnp.sum(e * e) * (0.5 / D)
            else:
                h_ref[...] = h

    tok = pl.BlockSpec((tm, D), lambda i, j: (i, 0))
    in_specs = [tok, pl.BlockSpec((1, D), lambda i, j: (0, 0)),
                pl.BlockSpec((None, Fs, D), lambda i, j: (j, 0, 0)),
                pl.BlockSpec((None, Fs, D), lambda i, j: (j, 0, 0)),
                pl.BlockSpec((None, Fs, D), lambda i, j: (j, 0, 0))]
    args = [x, gain, wg, wu, wd]
    act = pl.BlockSpec((None, tm, Fs), lambda i, j: (j, i, 0))
    out_shape = [jax.ShapeDtypeStruct((T, D), F32), jax.ShapeDtypeStruct((T, D), BF16),
                 jax.ShapeDtypeStruct((NS, T, Fs), BF16), jax.ShapeDtypeStruct((NS, T, Fs), BF16)]
    out_specs = [tok, tok, act, act]
    if with_loss:
        in_specs.append(tok)
        args.append(tgt)
        out_shape.append(jax.ShapeDtypeStruct((8, LANES), F32))
        out_specs.append(pl.BlockSpec((8, LANES), lambda i, j: (0, 0)))
    return _call(body, grid=(T // tm, NS), in_specs=in_specs, out_specs=out_specs, out_shape=out_shape,
                 scratch_shapes=[pltpu.VMEM((tm, D), F32)], sem=("arbitrary", "arbitrary"), name=name,
                 args=args, phase=phase)


def _rms_bwd(xv, gain, dn):
    r = lax.rsqrt(jnp.mean(xv * xv, axis=-1, keepdims=True) + EPS)
    xhat = xv * r
    dxh = dn * gain
    dx = r * (dxh - xhat * jnp.mean(dxh * xhat, axis=-1, keepdims=True))
    dg = jnp.sum(dn * xhat, axis=0, keepdims=True)
    return dx, dg


def _ffn_bwd_act(dh, x, gain, G, U, wg, wu, wd, *, tm, name, phase=None):
    T, D = x.shape
    NS, Fs, _ = wg.shape

    def body(dh_ref, x_ref, g_ref, G_ref, U_ref, wg_ref, wu_ref, wd_ref,
             dG_ref, dU_ref, A_ref, dy_ref, dx_ref, dg_ref, acc_ref):
        i = pl.program_id(0)
        j = pl.program_id(1)

        @pl.when(j == 0)
        def _():
            dy_ref[...] = (0.5 * dh_ref[...]).astype(BF16)
            acc_ref[...] = jnp.zeros_like(acc_ref)

        @pl.when((i == 0) & (j == 0))
        def _():
            dg_ref[...] = jnp.zeros_like(dg_ref)

        nchunks = tm // ROW_CHUNK
        dA, dGU = {}, {}
        for step in range(nchunks + 2):
            if step < nchunks:
                rows = slice(step * ROW_CHUNK, (step + 1) * ROW_CHUNK)
                dA[step] = _dot_nt(dy_ref[rows, :], wd_ref[...])
            if 1 <= step <= nchunks:
                k = step - 1
                rows = slice(k * ROW_CHUNK, (k + 1) * ROW_CHUNK)
                Gv = G_ref[rows, :].astype(F32)
                Uv = U_ref[rows, :].astype(F32)
                sig = _sigmoid(Gv)
                s = Gv * sig
                dG = (dA[k] * Uv * (sig * (1.0 + Gv * (1.0 - sig)))).astype(BF16)
                dU = (dA.pop(k) * s).astype(BF16)
                dG_ref[rows, :] = dG
                dU_ref[rows, :] = dU
                A_ref[rows, :] = (s * Uv).astype(BF16)
                dGU[k] = (dG, dU)
            if 2 <= step:
                k = step - 2
                rows = slice(k * ROW_CHUNK, (k + 1) * ROW_CHUNK)
                dG, dU = dGU.pop(k)
                acc_ref[rows, :] += _dot(dG, wg_ref[...]) + _dot(dU, wu_ref[...])

        @pl.when(j == NS - 1)
        def _():
            dx, dg = _rms_bwd(x_ref[...], g_ref[...], acc_ref[...])
            dx_ref[...] = dh_ref[...] + dx
            dg_ref[...] += dg

    tok = pl.BlockSpec((tm, D), lambda i, j: (i, 0))
    act = pl.BlockSpec((None, tm, Fs), lambda i, j: (j, i, 0))
    vec = pl.BlockSpec((1, D), lambda i, j: (0, 0))
    return _call(
        body, grid=(T // tm, NS),
        in_specs=[tok, tok, vec, act, act,
                  pl.BlockSpec((None, Fs, D), lambda i, j: (j, 0, 0)),
                  pl.BlockSpec((None, Fs, D), lambda i, j: (j, 0, 0)),
                  pl.BlockSpec((None, Fs, D), lambda i, j: (j, 0, 0))],
        out_specs=[act, act, act, tok, tok, vec],
        out_shape=[jax.ShapeDtypeStruct((NS, T, Fs), BF16)] * 3
        + [jax.ShapeDtypeStruct((T, D), BF16), jax.ShapeDtypeStruct((T, D), F32),
           jax.ShapeDtypeStruct((1, D), F32)],
        scratch_shapes=[pltpu.VMEM((tm, D), F32)],
        sem=("arbitrary", "arbitrary"), name=name, args=(dh, x, gain, G, U, wg, wu, wd), phase=phase)


def _ffn_bwd_w(n, dy, dG, dU, A, *, tk, name, phase=None):
    T, D = n.shape
    NS, _, Fs = dG.shape

    def body(n_ref, dy_ref, dG_ref, dU_ref, A_ref, wg_ref, wu_ref, wd_ref):
        @pl.when(pl.program_id(1) == 0)
        def _():
            wg_ref[...] = jnp.zeros_like(wg_ref)
            wu_ref[...] = jnp.zeros_like(wu_ref)
            wd_ref[...] = jnp.zeros_like(wd_ref)

        nv = n_ref[...]
        wg_ref[...] += _dot_tn(dG_ref[...], nv)
        wu_ref[...] += _dot_tn(dU_ref[...], nv)
        wd_ref[...] += _dot_tn(A_ref[...], dy_ref[...])

    tok = pl.BlockSpec((tk, D), lambda j, k: (k, 0))
    act = pl.BlockSpec((None, tk, Fs), lambda j, k: (j, k, 0))
    return _call(
        body, grid=(NS, T // tk), in_specs=[tok, tok, act, act, act],
        out_specs=[pl.BlockSpec((None, Fs, D), lambda j, k: (j, 0, 0))] * 3,
        out_shape=[jax.ShapeDtypeStruct((NS, Fs, D), F32)] * 3,
        sem=("arbitrary", "arbitrary"), name=name, args=(n, dy, dG, dU, A), phase=phase)


def _inproj_fwd(h, gain, win, *, tm, name, phase=None):
    T, D = h.shape
    NS, _, Cs = win.shape

    def body(h_ref, g_ref, w_ref, u_ref, n_ref):
        @pl.when(pl.program_id(1) == 0)
        def _():
            xv = h_ref[...]
            r = lax.rsqrt(jnp.mean(xv * xv, axis=-1, keepdims=True) + EPS)
            n_ref[...] = (xv * r * g_ref[...]).astype(BF16)

        u_ref[...] = _dot(n_ref[...], w_ref[...])

    tok = pl.BlockSpec((tm, D), lambda i, j: (i, 0))
    return _call(
        body, grid=(T // tm, NS),
        in_specs=[tok, pl.BlockSpec((1, D), lambda i, j: (0, 0)),
                  pl.BlockSpec((None, D, Cs), lambda i, j: (j, 0, 0))],
        out_specs=[pl.BlockSpec((tm, Cs), lambda i, j: (i, j)), tok],
        out_shape=[jax.ShapeDtypeStruct((T, NS * Cs), F32), jax.ShapeDtypeStruct((T, D), BF16)],
        sem=("arbitrary", "arbitrary"), name=name, args=(h, gain, win), phase=phase)


def _inproj_bwd_act(du, dh, h, gain, win, *, tm, name, phase=None):
    T, D = h.shape
    NS, _, Cs = win.shape

    def body(du_ref, dh_ref, h_ref, g_ref, w_ref, dx_ref, dg_ref, acc_ref):
        i = pl.program_id(0)
        j = pl.program_id(1)

        @pl.when(j == 0)
        def _():
            acc_ref[...] = jnp.zeros_like(acc_ref)

        @pl.when((i == 0) & (j == 0))
        def _():
            dg_ref[...] = jnp.zeros_like(dg_ref)

        acc_ref[...] += _dot_nt(du_ref[...], w_ref[...])

        @pl.when(j == NS - 1)
        def _():
            dx, dg = _rms_bwd(h_ref[...], g_ref[...], acc_ref[...])
            dx_ref[...] = dh_ref[...] + dx
            dg_ref[...] += dg

    tok = pl.BlockSpec((tm, D), lambda i, j: (i, 0))
    vec = pl.BlockSpec((1, D), lambda i, j: (0, 0))
    return _call(
        body, grid=(T // tm, NS),
        in_specs=[pl.BlockSpec((tm, Cs), lambda i, j: (i, j)), tok, tok, vec,
                  pl.BlockSpec((None, D, Cs), lambda i, j: (j, 0, 0))],
        out_specs=[tok, vec],
        out_shape=[jax.ShapeDtypeStruct((T, D), F32), jax.ShapeDtypeStruct((1, D), F32)],
        scratch_shapes=[pltpu.VMEM((tm, D), F32)],
        sem=("arbitrary", "arbitrary"), name=name, args=(du, dh, h, gain, win), phase=phase)


def _inproj_bwd_w(n, du, ns, *, tk, name, phase=None):
    T, D = n.shape
    Cs = du.shape[1] // ns

    def body(n_ref, du_ref, w_ref):
        @pl.when(pl.program_id(1) == 0)
        def _():
            w_ref[...] = jnp.zeros_like(w_ref)

        w_ref[...] += _dot_tn(n_ref[...], du_ref[...])

    return _call(
        body, grid=(ns, T // tk),
        in_specs=[pl.BlockSpec((tk, D), lambda j, k: (k, 0)), pl.BlockSpec((tk, Cs), lambda j, k: (k, j))],
        out_specs=[pl.BlockSpec((None, D, Cs), lambda j, k: (j, 0, 0))],
        out_shape=[jax.ShapeDtypeStruct((ns, D, Cs), F32)],
        sem=("arbitrary", "arbitrary"), name=name, args=(n, du), phase=phase)


STRIDE = 4


def _permute(src_ref, tmp_ref, put):
    S = src_ref.shape[0]
    L4, L16 = S // STRIDE, S // (STRIDE * STRIDE)
    put(0, 0, src_ref[...])
    for r0 in range(STRIDE):
        v = src_ref[pl.ds(r0, L4, stride=STRIDE), :]
        put(1, r0 * L4, v)
        tmp_ref[r0 * L4:(r0 + 1) * L4, :] = v
    for r0 in range(STRIDE):
        for r1 in range(STRIDE):
            put(2, (r1 * STRIDE + r0) * L16, tmp_ref[pl.ds(r0 * L4 + r1, L16, stride=STRIDE), :])


def _permute_out(src_ref, tmp_ref, out_ref, cast):
    for cc in range(src_ref.shape[0]):
        cols = slice(cc * LANES, (cc + 1) * LANES)

        def put(p, row0, v, cols=cols):
            out_ref[p, row0:row0 + v.shape[0], cols] = v.astype(cast)

        _permute(src_ref.at[cc], tmp_ref, put)


def _unpermute_in(get_block, dst_ref, tmp_ref, p, S):
    L4, L16 = S // STRIDE, S // (STRIDE * STRIDE)
    if p == 0:
        dst_ref[...] = get_block(0, S)
        return
    if p == 1:
        for r0 in range(STRIDE):
            dst_ref[pl.ds(r0, L4, stride=STRIDE), :] = get_block(r0 * L4, L4)
        return
    for r0 in range(STRIDE):
        for r1 in range(STRIDE):
            tmp_ref[pl.ds(r0 * L4 + r1, L16, stride=STRIDE), :] = get_block((r1 * STRIDE + r0) * L16, L16)
    for r0 in range(STRIDE):
        dst_ref[pl.ds(r0, L4, stride=STRIDE), :] = tmp_ref[r0 * L4:(r0 + 1) * L4, :]


def _qkv_prep(u, gains, B, S, *, name):
    emat = _seg_matrix(D_ATTN)

    def body(u_ref, g_ref, e_ref, out_ref, scr_ref, tmp_ref):
        c = pl.program_id(1)
        xv = u_ref[...]
        ms = _seg_mean(xv * xv, e_ref, HEAD_DIM)
        r = jnp.where(c < 2, lax.rsqrt(ms + EPS), 1.0)
        yv = xv * r * g_ref[...]
        for cc in range(4):
            scr_ref[cc] = yv[:, cc * LANES:(cc + 1) * LANES]
        _permute_out(scr_ref, tmp_ref, out_ref, BF16)

    return pl.pallas_call(
        body, grid=(B, 3),
        in_specs=[pl.BlockSpec((S, D_ATTN), lambda b, c: (b, c)),
                  pl.BlockSpec((None, 1, D_ATTN), lambda b, c: (c, 0, 0)),
                  pl.BlockSpec((D_ATTN, D_ATTN), lambda b, c: (0, 0))],
        out_specs=pl.BlockSpec((None, N_PATTERNS, None, S, D_ATTN), lambda b, c: (c, 0, b, 0, 0)),
        out_shape=jax.ShapeDtypeStruct((3, N_PATTERNS, B, S, D_ATTN), BF16),
        scratch_shapes=[pltpu.VMEM((4, S, LANES), F32), pltpu.VMEM((S, LANES), F32)],
        compiler_params=_params("arbitrary", "arbitrary"), name=name)(u, gains, emat)


def _band_mask(p, b):
    nblk = jnp.right_shift(16, 2 * p)
    has_prev = jnp.bitwise_and(b, nblk - 1) != 0
    qi = lax.broadcasted_iota(jnp.int32, (QBLK, 2 * QBLK), 0)
    ci = lax.broadcasted_iota(jnp.int32, (QBLK, 2 * QBLK), 1)
    dist = QBLK + qi - ci
    return (dist >= 0) & (dist <= QBLK) & (has_prev | (ci >= QBLK))


def _first_head(rows):
    return lax.broadcasted_iota(jnp.int32, (rows, LANES), 1) < HEAD_DIM


def _split_heads(pair):
    first = _first_head(pair.shape[0])
    zero = jnp.zeros_like(pair)
    return jnp.concatenate([jnp.where(first, pair, zero), jnp.where(first, zero, pair)], axis=0)


def _merge_heads(col_a, col_b):
    rows = col_a.shape[0]
    return jnp.where(_first_head(rows), jnp.broadcast_to(col_a, (rows, LANES)), jnp.broadcast_to(col_b, (rows, LANES)))


QB_FWD = 8
QB_BWD = 4


def _attn_fwd(qkv, *, name):
    QB = QB_FWD
    nb = qkv.shape[2]

    def body(q_ref, kp_ref, kc_ref, vp_ref, vc_ref, o_ref, lse_ref):
        kall = jnp.concatenate([kp_ref[...]] + [kc_ref[t] for t in range(QB)], axis=0)
        vall = jnp.concatenate([vp_ref[...]] + [vc_ref[t] for t in range(QB)], axis=0)
        masks = []
        for t in range(QB):
            mask = _band_mask(pl.program_id(0), QB * pl.program_id(1) + t)
            masks.append(jnp.concatenate([mask, mask], axis=0))
        units = [(t, hp) for t in range(QB) for hp in range(HEADS // 2)]
        scores, probs = {}, {}
        for step in range(len(units) + 2):
            if step < len(units):
                t, hp = units[step]
                cols = slice(hp * LANES, (hp + 1) * LANES)
                scores[step] = _dot_nt(_split_heads(q_ref[t, :, cols]), kall[t * QBLK:(t + 2) * QBLK, cols])
            if 1 <= step <= len(units):
                t, hp = units[step - 1]
                cols = slice(hp * LANES, (hp + 1) * LANES)
                s = jnp.where(masks[t], scores.pop(step - 1), NEG)
                m = jnp.max(s, axis=-1, keepdims=True)
                e = jnp.exp(s - m)
                l = jnp.sum(e, axis=-1, keepdims=True)
                probs[step - 1] = (e * (1.0 / l)).astype(BF16)
                lse = m + jnp.log(l)
                lse_ref[t, :, cols] = _merge_heads(lse[:QBLK], lse[QBLK:])
            if 2 <= step:
                t, hp = units[step - 2]
                cols = slice(hp * LANES, (hp + 1) * LANES)
                pr = probs.pop(step - 2)
                o_ref[t, :, cols] = _dot(jnp.concatenate([pr[:QBLK], pr[QBLK:]], axis=1),
                                         _split_heads(vall[t * QBLK:(t + 2) * QBLK, cols]))

    cur = lambda which: pl.BlockSpec((None, None, QB, QBLK, D_ATTN), lambda p, i: (which, p, i, 0, 0))
    prev = lambda which: pl.BlockSpec((None, None, None, QBLK, D_ATTN),
                                      lambda p, i: (which, p, jnp.maximum(QB * i - 1, 0), 0, 0))
    out = pl.BlockSpec((None, QB, QBLK, D_ATTN), lambda p, i: (p, i, 0, 0))
    return pl.pallas_call(
        body, grid=(N_PATTERNS, nb // QB), in_specs=[cur(0), prev(1), cur(1), prev(2), cur(2)], out_specs=[out, out],
        out_shape=[jax.ShapeDtypeStruct((N_PATTERNS, nb, QBLK, D_ATTN), F32)] * 2,
        compiler_params=_params("arbitrary", "arbitrary"), name=name)(qkv, qkv, qkv, qkv, qkv)


def _attn_combine(o3, lse3, B, S, *, name):
    def body(o_ref, l_ref, a_ref, lt_ref, so_ref, sl_ref, tmp_ref):
        for p in range(N_PATTERNS):
            _unpermute_in(lambda r0, n, p=p: o_ref[p, pl.ds(r0, n), :], so_ref.at[p], tmp_ref, p, S)
            _unpermute_in(lambda r0, n, p=p: l_ref[p, pl.ds(r0, n), :], sl_ref.at[p], tmp_ref, p, S)
        l0, l1, l2 = sl_ref[0], sl_ref[1], sl_ref[2]
        m = jnp.maximum(jnp.maximum(l0, l1), l2)
        w0, w1, w2 = jnp.exp(l0 - m), jnp.exp(l1 - m), jnp.exp(l2 - m)
        tot = w0 + w1 + w2
        a_ref[...] = (w0 * so_ref[0] + w1 * so_ref[1] + w2 * so_ref[2]) / tot
        lt_ref[...] = m + jnp.log(tot)

    o3 = o3.reshape(N_PATTERNS, B, S, D_ATTN)
    lse3 = lse3.reshape(N_PATTERNS, B, S, D_ATTN)
    inp = pl.BlockSpec((N_PATTERNS, None, S, LANES), lambda b, c: (0, b, 0, c))
    out = pl.BlockSpec((S, LANES), lambda b, c: (b, c))
    return pl.pallas_call(
        body, grid=(B, D_ATTN // LANES), in_specs=[inp, inp], out_specs=[out, out],
        out_shape=[jax.ShapeDtypeStruct((B * S, D_ATTN), F32)] * 2,
        scratch_shapes=[pltpu.VMEM((N_PATTERNS, S, LANES), F32)] * 2 + [pltpu.VMEM((S, LANES), F32)],
        compiler_params=_params("arbitrary", "arbitrary"), name=name)(o3, lse3)


STAT_D = 8


def _attn_bwd_prep(dattn, attn, lse, B, S, *, name):
    emat = _seg_matrix(LANES)
    ncc = D_ATTN // LANES

    def body(da_ref, a_ref, l_ref, e_ref, do_ref, st_ref, scr_ref, nat_ref, tmp_ref):
        cc = pl.program_id(1)
        da = da_ref[...]
        dsum = _seg_mean(da * a_ref[...], e_ref, 1.0)
        scr_ref[...] = da

        def put_do(p, row0, v):
            do_ref[p, row0:row0 + v.shape[0], :] = v.astype(BF16)

        _permute(scr_ref, tmp_ref, put_do)

        lane = lax.broadcasted_iota(jnp.int32, (S, LANES), 1)
        h0 = 2 * cc
        vals = ((h0, l_ref[:, 0:1]), (h0 + 1, l_ref[:, HEAD_DIM:HEAD_DIM + 1]),
                (STAT_D + h0, dsum[:, 0:1]), (STAT_D + h0 + 1, dsum[:, HEAD_DIM:HEAD_DIM + 1]))
        tile = jnp.where(cc == 0, 0.0, nat_ref[...])
        for at, col in vals:
            tile = jnp.where(lane == at, col, tile)
        nat_ref[...] = tile

        @pl.when(cc == ncc - 1)
        def _():
            def put_st(p, row0, v):
                st_ref[p, row0:row0 + v.shape[0], :] = v

            _permute(nat_ref, tmp_ref, put_st)

    inp = pl.BlockSpec((S, LANES), lambda b, c: (b, c))
    return pl.pallas_call(
        body, grid=(B, ncc),
        in_specs=[inp, inp, inp, pl.BlockSpec((LANES, LANES), lambda b, c: (0, 0))],
        out_specs=[pl.BlockSpec((N_PATTERNS, None, S, LANES), lambda b, c: (0, b, 0, c)),
                   pl.BlockSpec((N_PATTERNS, None, S, LANES), lambda b, c: (0, b, 0, 0))],
        out_shape=[jax.ShapeDtypeStruct((N_PATTERNS, B, S, D_ATTN), BF16),
                   jax.ShapeDtypeStruct((N_PATTERNS, B, S, LANES), F32)],
        scratch_shapes=[pltpu.VMEM((S, LANES), F32)] * 3,
        compiler_params=_params("arbitrary", "arbitrary"), name=name)(dattn, attn, lse, emat)


def _attn_bwd(qkv, do3, st3, *, name, phase=None):
    QB = QB_BWD
    nb = qkv.shape[2]
    ngroups = nb // QB

    def body(q_ref, kp_ref, kc_ref, vp_ref, vc_ref, do_ref, st_ref, out_ref, carry_ref):
        p = pl.program_id(0)
        i = pl.program_id(1)

        @pl.when((p == 0) & (i == 0))
        def _():
            carry_ref[...] = jnp.zeros_like(carry_ref)

        kall = jnp.concatenate([kp_ref[...]] + [kc_ref[t] for t in range(QB)], axis=0)
        vall = jnp.concatenate([vp_ref[...]] + [vc_ref[t] for t in range(QB)], axis=0)

        masks = []
        for t in range(QB):
            mask = _band_mask(p, QB * i + t) & (i < ngroups)
            masks.append(jnp.concatenate([mask, mask], axis=0))

        def operands(t, hp):
            cols = slice(hp * LANES, (hp + 1) * LANES)
            kh, vh = kall[t * QBLK:(t + 2) * QBLK, cols], vall[t * QBLK:(t + 2) * QBLK, cols]
            return kh, vh, _split_heads(q_ref[t, :, cols]), _split_heads(do_ref[t, :, cols])

        def stage_scores(t, hp):
            kh, vh, q2, do2 = operands(t, hp)
            return _dot_nt(q2, kh), _dot_nt(do2, vh)

        def stage_softmax(t, hp, s, dp):
            h0, h1 = 2 * hp, 2 * hp + 1
            lse = jnp.concatenate([st_ref[t, :, h0:h0 + 1], st_ref[t, :, h1:h1 + 1]], axis=0)
            dsum = jnp.concatenate([st_ref[t, :, STAT_D + h0:STAT_D + h0 + 1],
                                    st_ref[t, :, STAT_D + h1:STAT_D + h1 + 1]], axis=0)
            pr = jnp.where(masks[t], jnp.exp(s - lse), 0.0)
            return (pr * (dp - dsum)).astype(BF16), pr.astype(BF16)

        def stage_grads(t, hp, ds, prb):
            cols = slice(hp * LANES, (hp + 1) * LANES)
            kh, vh, q2, do2 = operands(t, hp)
            dq = _dot(jnp.concatenate([ds[:QBLK], ds[QBLK:]], axis=1), _split_heads(kh))
            dk, dv = _dot_tn(ds, q2), _dot_tn(prb, do2)
            if t == 0:
                for c in range(3):
                    for tt in range(QB):
                        v = carry_ref[c, tt, :, cols]
                        if tt == QB - 1 and c > 0:
                            v = v + (dk if c == 1 else dv)[:QBLK]
                        out_ref[c, tt, :, cols] = v.astype(BF16)
            else:
                carry_ref[1, t - 1, :, cols] += dk[:QBLK]
                carry_ref[2, t - 1, :, cols] += dv[:QBLK]
            carry_ref[0, t, :, cols] = dq
            carry_ref[1, t, :, cols] = dk[QBLK:]
            carry_ref[2, t, :, cols] = dv[QBLK:]

        units = [(t, hp) for hp in range(HEADS // 2) for t in range(QB)]
        scores, probs = {}, {}
        for step in range(len(units) + 2):
            if step < len(units):
                scores[step] = stage_scores(*units[step])
            if 1 <= step <= len(units):
                probs[step - 1] = stage_softmax(*units[step - 1], *scores.pop(step - 1))
            if 2 <= step:
                stage_grads(*units[step - 2], *probs.pop(step - 2))

    group = lambda i: jnp.minimum(i, ngroups - 1)
    cur = lambda which: pl.BlockSpec((None, None, QB, QBLK, D_ATTN), lambda p, i: (which, p, group(i), 0, 0))
    prev = lambda which: pl.BlockSpec((None, None, None, QBLK, D_ATTN),
                                      lambda p, i: (which, p, jnp.maximum(QB * group(i) - 1, 0), 0, 0))
    aux = lambda lanes: pl.BlockSpec((None, QB, QBLK, lanes), lambda p, i: (p, group(i), 0, 0))
    return _call(
        body, grid=(N_PATTERNS, ngroups + 1),
        in_specs=[cur(0), prev(1), cur(1), prev(2), cur(2), aux(D_ATTN), aux(LANES)],
        out_specs=[pl.BlockSpec((3, None, QB, QBLK, D_ATTN), lambda p, i: (0, p, jnp.maximum(i - 1, 0), 0, 0))],
        out_shape=[jax.ShapeDtypeStruct((3, N_PATTERNS, nb, QBLK, D_ATTN), BF16)],
        scratch_shapes=[pltpu.VMEM((3, QB, QBLK, D_ATTN), F32)],
        sem=("arbitrary", "arbitrary"), name=name, args=(qkv, qkv, qkv, qkv, qkv, do3, st3), phase=phase)


def _attn_grad_combine(cur, u, gains, B, S, *, name, phase=None):
    emat = _seg_matrix(LANES)

    def body(cur_ref, u_ref, g_ref, e_ref, du_ref, dg_ref, scr_ref, tmp_ref):
        c = pl.program_id(0)
        b = pl.program_id(2)
        for p in range(N_PATTERNS):
            _unpermute_in(lambda r0, n, p=p: cur_ref[p, pl.ds(r0, n), :].astype(F32), scr_ref.at[p], tmp_ref, p, S)
        dy = scr_ref[0] + scr_ref[1] + scr_ref[2]
        xv = u_ref[...]
        gain = g_ref[...]
        ms = _seg_mean(xv * xv, e_ref, HEAD_DIM)
        r = lax.rsqrt(ms + EPS)
        xhat = xv * r
        dxh = dy * gain
        dx = r * (dxh - xhat * _seg_mean(dxh * xhat, e_ref, HEAD_DIM))
        du_ref[...] = jnp.where(c < 2, dx, dy).astype(BF16)

        @pl.when((b == 0))
        def _():
            dg_ref[...] = jnp.zeros_like(dg_ref)

        dg_ref[...] += jnp.sum(dy * xhat, axis=0, keepdims=True)

    cur = cur.reshape(3, N_PATTERNS, B, S, D_ATTN)
    ncc = D_ATTN // LANES
    return _call(
        body, grid=(3, ncc, B),
        in_specs=[pl.BlockSpec((None, N_PATTERNS, None, S, LANES), lambda c, cc, b: (c, 0, b, 0, cc)),
                  pl.BlockSpec((S, LANES), lambda c, cc, b: (b, c * ncc + cc)),
                  pl.BlockSpec((None, 1, LANES), lambda c, cc, b: (c, 0, cc)),
                  pl.BlockSpec((LANES, LANES), lambda c, cc, b: (0, 0))],
        out_specs=[pl.BlockSpec((S, LANES), lambda c, cc, b: (b, c * ncc + cc)),
                   pl.BlockSpec((None, 1, LANES), lambda c, cc, b: (c, 0, cc))],
        out_shape=[jax.ShapeDtypeStruct((B * S, 3 * D_ATTN), BF16), jax.ShapeDtypeStruct((3, 1, D_ATTN), F32)],
        scratch_shapes=[pltpu.VMEM((N_PATTERNS, S, LANES), F32), pltpu.VMEM((S, LANES), F32)],
        sem=("arbitrary", "arbitrary", "arbitrary"), name=name, args=(cur, u, gains, emat), phase=phase)


HALO = 32
SUB = 64
SUBLANES = 8


def _shifted_copies(src_ref, sh_ref, tc):
    sh_ref[0] = src_ref[...]
    for r in range(1, SUBLANES):
        sh_ref[r, 0:tc + HALO - SUBLANES, :] = src_ref[pl.ds(r, tc + HALO - SUBLANES), :]


def _shifted(sh_ref, start, size):
    return sh_ref[start % SUBLANES, pl.ds(start - start % SUBLANES, size), :]


def _conv_fwd(u, cw, cb, lg, lb, B, S, *, tc, name):
    nchunk = S // tc
    hb = tc // HALO

    def body(ca_ref, cap_ref, cg_ref, cgp_ref, w_ref, cb_ref, lg_ref, lb_ref, cv_ref, glu_ref, y_ref, pad_ref, sh_ref):
        i = pl.program_id(1)
        glu = ca_ref[...] * _sigmoid(cg_ref[...])
        glu_ref[...] = glu
        prev = cap_ref[...] * _sigmoid(cgp_ref[...])
        pad_ref[0:HALO, :] = jnp.where(i > 0, prev, 0.0)
        pad_ref[HALO:, :] = glu
        _shifted_copies(pad_ref, sh_ref, tc)
        for sub in range(tc // SUB):
            acc = jnp.zeros((SUB, D_CONV), F32) + cb_ref[...]
            for k in range(CONV_K):
                acc = acc + _shifted(sh_ref, sub * SUB + HALO - (CONV_K - 1) + k, SUB) * w_ref[pl.ds(k, 1), :]
            y_ref[sub * SUB:(sub + 1) * SUB, :] = acc
        y = y_ref[...]
        mu = jnp.mean(y, axis=-1, keepdims=True)
        yc = y - mu
        var = jnp.mean(yc * yc, axis=-1, keepdims=True)
        z = yc * lax.rsqrt(var + EPS) * lg_ref[...] + lb_ref[...]
        cv_ref[...] = (z * _sigmoid(z)).astype(BF16)

    def cur(col):
        return pl.BlockSpec((tc, D_CONV), lambda b, i: (b * nchunk + i, col))

    def halo(col):
        return pl.BlockSpec((HALO, D_CONV), lambda b, i: (jnp.maximum((b * nchunk + i) * hb - 1, 0), col))

    vec = pl.BlockSpec((1, D_CONV), lambda b, i: (0, 0))
    out = pl.BlockSpec((tc, D_CONV), lambda b, i: (b * nchunk + i, 0))
    return pl.pallas_call(
        body, grid=(B, nchunk),
        in_specs=[cur(3), halo(3), cur(4), halo(4), pl.BlockSpec((CONV_K, D_CONV), lambda b, i: (0, 0)), vec, vec, vec],
        out_specs=[out, out, out],
        out_shape=[jax.ShapeDtypeStruct((B * S, D_CONV), BF16), jax.ShapeDtypeStruct((B * S, D_CONV), F32),
                   jax.ShapeDtypeStruct((B * S, D_CONV), F32)],
        scratch_shapes=[pltpu.VMEM((tc + HALO, D_CONV), F32), pltpu.VMEM((SUBLANES, tc + HALO, D_CONV), F32)],
        compiler_params=_params("arbitrary", "arbitrary"), name=name)(u, u, u, u, cw, cb, lg, lb)


def _conv_bwd_norm(dcv, y, lg, lb, *, tc, name):
    T = y.shape[0]

    def body(dcv_ref, y_ref, lg_ref, lb_ref, dy_ref, part_ref):
        yv = y_ref[...]
        mu = jnp.mean(yv, axis=-1, keepdims=True)
        yc = yv - mu
        var = jnp.mean(yc * yc, axis=-1, keepdims=True)
        rstd = lax.rsqrt(var + EPS)
        xhat = yc * rstd
        z = xhat * lg_ref[...] + lb_ref[...]
        sig = _sigmoid(z)
        dz = dcv_ref[...] * (sig * (1.0 + z * (1.0 - sig)))
        dxh = dz * lg_ref[...]
        dy = rstd * (dxh - jnp.mean(dxh, axis=-1, keepdims=True)
                     - xhat * jnp.mean(dxh * xhat, axis=-1, keepdims=True))
        dy_ref[...] = dy

        @pl.when(pl.program_id(0) == 0)
        def _():
            part_ref[...] = jnp.zeros_like(part_ref)

        part_ref[0:1, :] += jnp.sum(dz * xhat, axis=0, keepdims=True)
        part_ref[1:2, :] += jnp.sum(dz, axis=0, keepdims=True)
        part_ref[2:3, :] += jnp.sum(dy, axis=0, keepdims=True)

    tok = pl.BlockSpec((tc, D_CONV), lambda i: (i, 0))
    vec = pl.BlockSpec((1, D_CONV), lambda i: (0, 0))
    return pl.pallas_call(
        body, grid=(T // tc,), in_specs=[tok, tok, vec, vec],
        out_specs=[tok, pl.BlockSpec((8, D_CONV), lambda i: (0, 0))],
        out_shape=[jax.ShapeDtypeStruct((T, D_CONV), F32), jax.ShapeDtypeStruct((8, D_CONV), F32)],
        compiler_params=_params("arbitrary"), name=name)(dcv, y, lg, lb)


def _conv_bwd_taps(dy, glu, u, cw, B, S, *, tc, name, phase=None):
    nchunk = S // tc
    hb = tc // HALO
    last_hb = B * S // HALO - 1

    def body(dy_ref, dyn_ref, glu_ref, glup_ref, ca_ref, cg_ref, w_ref, dca_ref, dcg_ref, dw_ref,
             dyp_ref, glp_ref, acc_ref, shd_ref, shg_ref):
        b = pl.program_id(0)
        i = pl.program_id(1)
        dy = dy_ref[...]
        dyp_ref[0:tc, :] = dy
        dyp_ref[tc:, :] = jnp.where(i < nchunk - 1, dyn_ref[...], 0.0)
        glp_ref[0:HALO, :] = jnp.where(i > 0, glup_ref[...], 0.0)
        glp_ref[HALO:, :] = glu_ref[...]
        _shifted_copies(dyp_ref, shd_ref, tc)
        _shifted_copies(glp_ref, shg_ref, tc)

        @pl.when((b == 0) & (i == 0))
        def _():
            dw_ref[...] = jnp.zeros_like(dw_ref)

        for sub in range(tc // SUB):
            acc = jnp.zeros((SUB, D_CONV), F32)
            for k in range(CONV_K):
                acc = acc + _shifted(shd_ref, sub * SUB + (CONV_K - 1) - k, SUB) * w_ref[pl.ds(k, 1), :]
            acc_ref[sub * SUB:(sub + 1) * SUB, :] = acc
        for k in range(CONV_K):
            dw_ref[k:k + 1, :] += jnp.sum(dy * _shifted(shg_ref, HALO - (CONV_K - 1) + k, tc), axis=0, keepdims=True)
        dglu = acc_ref[...]
        ca = ca_ref[...]
        sig = _sigmoid(cg_ref[...])
        dca_ref[...] = (dglu * sig).astype(BF16)
        dcg_ref[...] = (dglu * ca * sig * (1.0 - sig)).astype(BF16)

    tok = pl.BlockSpec((tc, D_CONV), lambda b, i: (b * nchunk + i, 0))
    nxt = pl.BlockSpec((HALO, D_CONV), lambda b, i: (jnp.minimum((b * nchunk + i + 1) * hb, last_hb), 0))
    prv = pl.BlockSpec((HALO, D_CONV), lambda b, i: (jnp.maximum((b * nchunk + i) * hb - 1, 0), 0))
    return _call(
        body, grid=(B, nchunk),
        in_specs=[tok, nxt, tok, prv,
                  pl.BlockSpec((tc, D_CONV), lambda b, i: (b * nchunk + i, 3)),
                  pl.BlockSpec((tc, D_CONV), lambda b, i: (b * nchunk + i, 4)),
                  pl.BlockSpec((CONV_K, D_CONV), lambda b, i: (0, 0))],
        out_specs=[tok, tok, pl.BlockSpec((32, D_CONV), lambda b, i: (0, 0))],
        out_shape=[jax.ShapeDtypeStruct((B * S, D_CONV), BF16), jax.ShapeDtypeStruct((B * S, D_CONV), BF16),
                   jax.ShapeDtypeStruct((32, D_CONV), F32)],
        scratch_shapes=[pltpu.VMEM((tc + HALO, D_CONV), F32), pltpu.VMEM((tc + HALO, D_CONV), F32),
                        pltpu.VMEM((tc, D_CONV), F32), pltpu.VMEM((SUBLANES, tc + HALO, D_CONV), F32),
                        pltpu.VMEM((SUBLANES, tc + HALO, D_CONV), F32)],
        sem=("arbitrary", "arbitrary"), name=name, args=(dy, dy, glu, glu, u, u, cw), phase=phase)


def _outproj_fwd(h, attn, cv, wout, *, tm, name):
    T, D = h.shape

    def body(h_ref, a_ref, c_ref, w_ref, o_ref):
        o_ref[...] = (h_ref[...] + _dot(a_ref[...].astype(BF16), w_ref[0:D_ATTN, :])
                      + _dot(c_ref[...], w_ref[D_ATTN:, :]))

    tok = pl.BlockSpec((tm, D), lambda i: (i, 0))
    half = pl.BlockSpec((tm, D_ATTN), lambda i: (i, 0))
    return pl.pallas_call(
        body, grid=(T // tm,), in_specs=[tok, half, half, pl.BlockSpec(wout.shape, lambda i: (0, 0))],
        out_specs=tok, out_shape=jax.ShapeDtypeStruct((T, D), F32),
        compiler_params=_params("arbitrary"), name=name)(h, attn, cv, wout)


def _outproj_bwd(dh, attn, cv, wout, *, tm, name):
    T, D = dh.shape

    def body(dh_ref, a_ref, c_ref, w_ref, da_ref, dc_ref, dw_ref):
        @pl.when(pl.program_id(0) == 0)
        def _():
            dw_ref[...] = jnp.zeros_like(dw_ref)

        dhb = dh_ref[...].astype(BF16)
        da_ref[...] = _dot_nt(dhb, w_ref[0:D_ATTN, :])
        dc_ref[...] = _dot_nt(dhb, w_ref[D_ATTN:, :])
        dw_ref[0:D_ATTN, :] += _dot_tn(a_ref[...].astype(BF16), dhb)
        dw_ref[D_ATTN:, :] += _dot_tn(c_ref[...], dhb)

    tok = pl.BlockSpec((tm, D), lambda i: (i, 0))
    half = pl.BlockSpec((tm, D_ATTN), lambda i: (i, 0))
    wspec = pl.BlockSpec(wout.shape, lambda i: (0, 0))
    return pl.pallas_call(
        body, grid=(T // tm,), in_specs=[tok, half, half, wspec], out_specs=[half, half, wspec],
        out_shape=[jax.ShapeDtypeStruct((T, D_ATTN), F32), jax.ShapeDtypeStruct((T, D_ATTN), F32),
                   jax.ShapeDtypeStruct(wout.shape, F32)],
        compiler_params=_params("arbitrary"), name=name)(dh, attn, cv, wout)


ADAM_BLOCK_BYTES = 3 * 512 * 1024


def _adamw(w, g, m, v, *, name):
    R, C = w.shape
    tr = R
    for cand in (512, 352, 256, 176, 128, 64, 32, 16, 8):
        if R % cand == 0 and cand * C * 4 <= ADAM_BLOCK_BYTES:
            tr = cand
            break
    c1 = 1.0 - ADAM_B1 ** ADAM_STEP
    c2 = 1.0 - ADAM_B2 ** ADAM_STEP

    def body(w_ref, g_ref, m_ref, v_ref, d_ref, nm_ref, nv_ref):
        gv = g_ref[...]
        nm = ADAM_B1 * m_ref[...] + (1.0 - ADAM_B1) * gv
        nv = ADAM_B2 * v_ref[...] + (1.0 - ADAM_B2) * (gv * gv)
        d_ref[...] = -ADAM_LR * ((nm / c1) / (jnp.sqrt(nv / c2) + ADAM_EPS) + ADAM_WD * w_ref[...])
        nm_ref[...] = nm
        nv_ref[...] = nv

    blk = pl.BlockSpec((tr, C), lambda i: (i, 0))
    return pl.pallas_call(
        body, grid=(R // tr,), in_specs=[blk] * 4, out_specs=[blk] * 3,
        out_shape=[jax.ShapeDtypeStruct((R, C), F32)] * 3,
        compiler_params=_params("arbitrary"), name=name)(w, g, m, v)


TM = 512
TM_WIDE = 1024
TK = 1024
TC = 256


def _local_step(x, tgt, w, overlap=None):
    B, S, D = x.shape
    T = B * S
    x2 = x.reshape(T, D)
    t2 = tgt.reshape(T, D)
    ones = jnp.ones((1, D_ATTN), F32)
    scale = HEAD_DIM ** -0.5
    gains = jnp.stack([jnp.tile(w["q_norm"], (1, HEADS)) * scale, jnp.tile(w["k_norm"], (1, HEADS)), ones])
    g = {}

    def hosting(point, build):
        phase = overlap.phase(point, w, g) if overlap is not None else None
        if phase is None:
            return build(None)
        outs, extra = build(phase)
        overlap.done(point, extra, w, g)
        return outs

    h1, n1, G1, U1 = hosting("ffn1_fwd", lambda ph: _ffn_fwd(
        x2, w["ffn1_norm"], w["wg1"], w["wu1"], w["wd1"], None, tm=TM_WIDE, name="ffn1_fwd", phase=ph))
    u, n2 = hosting("inproj_fwd", lambda ph: _inproj_fwd(h1, w["mix_norm"], w["win"], tm=TM_WIDE, name="inproj_fwd", phase=ph))
    qkv = _qkv_prep(u, gains, B, S, name="qkv_prep")
    qkv = qkv.reshape(3, N_PATTERNS, T // QBLK, QBLK, D_ATTN)
    o3, lse3 = _attn_fwd(qkv, name="attn_fwd")
    attn, lse = _attn_combine(o3, lse3, B, S, name="attn_combine")
    cv, glu, yconv = _conv_fwd(u, w["conv_w"], w["conv_b"], w["conv_ln_g"], w["conv_ln_b"], B, S, tc=TC, name="conv_fwd")
    h2 = _outproj_fwd(h1, attn, cv, w["wout"], tm=TM, name="outproj_fwd")
    dh3, n3, G2, U2, loss = _ffn_fwd(h2, w["ffn2_norm"], w["wg2"], w["wu2"], w["wd2"], t2, tm=TM_WIDE, name="ffn2_fwd")

    dG, dU, A, dy, dh2, g["ffn2_norm"] = _ffn_bwd_act(dh3, h2, w["ffn2_norm"], G2, U2, w["wg2"], w["wu2"], w["wd2"],
                                                    tm=TM, name="ffn2_bwd_act")
    g["wg2"], g["wu2"], g["wd2"] = _ffn_bwd_w(n3, dy, dG, dU, A, tk=TK, name="ffn2_bwd_w")
    dattn, dcv, g["wout"] = _outproj_bwd(dh2, attn, cv, w["wout"], tm=TM, name="outproj_bwd")
    dyc, cpart = _conv_bwd_norm(dcv, yconv, w["conv_ln_g"], w["conv_ln_b"], tc=TC, name="conv_bwd_norm")
    dca, dcg, dcw = hosting("conv_bwd_taps", lambda ph: _conv_bwd_taps(
        dyc, glu, u, w["conv_w"], B, S, tc=TC, name="conv_bwd_taps", phase=ph))
    do3, st3 = _attn_bwd_prep(dattn, attn, lse, B, S, name="attn_bwd_prep")
    nb = T // QBLK
    (cur,) = hosting("attn_bwd", lambda ph: _attn_bwd(
        qkv, do3.reshape(N_PATTERNS, nb, QBLK, D_ATTN), st3.reshape(N_PATTERNS, nb, QBLK, LANES),
        name="attn_bwd", phase=ph))
    du_qkv, dgains = hosting("attn_grad_combine", lambda ph: _attn_grad_combine(
        cur, u, gains, B, S, name="attn_grad_combine", phase=ph))
    du = jnp.concatenate([du_qkv, dca, dcg], axis=1)
    (g["win"],) = hosting("inproj_bwd_w", lambda ph: _inproj_bwd_w(
        n2, du, w["win"].shape[0], tk=TK, name="inproj_bwd_w", phase=ph))
    dh1, g["mix_norm"] = hosting("inproj_bwd_act", lambda ph: _inproj_bwd_act(
        du, dh2, h1, w["mix_norm"], w["win"], tm=TM_WIDE, name="inproj_bwd_act", phase=ph))
    dG, dU, A, dy, dx, g["ffn1_norm"] = hosting("ffn1_bwd_act", lambda ph: _ffn_bwd_act(
        dh1, x2, w["ffn1_norm"], G1, U1, w["wg1"], w["wu1"], w["wd1"], tm=TM, name="ffn1_bwd_act", phase=ph))
    g["wg1"], g["wu1"], g["wd1"] = hosting("ffn1_bwd_w", lambda ph: _ffn_bwd_w(
        n1, dy, dG, dU, A, tk=TK, name="ffn1_bwd_w", phase=ph))

    g["q_norm"] = dgains[0].reshape(HEADS, HEAD_DIM).sum(axis=0, keepdims=True) * scale
    g["k_norm"] = dgains[1].reshape(HEADS, HEAD_DIM).sum(axis=0, keepdims=True)
    g["conv_ln_g"] = cpart[0:1]
    g["conv_ln_b"] = cpart[1:2]
    g["conv_b"] = cpart[2:3]
    g["conv_w"] = dcw[:CONV_K]
    return loss, dx.reshape(B, S, D), g


N_CHIPS = 4
N_DEV = 8
VMEM_SPEC = pl.BlockSpec(memory_space=pltpu.VMEM)


def _remote(src, dst, send_sem, recv_sem, device):
    return pltpu.make_async_remote_copy(src_ref=src, dst_ref=dst, send_sem=send_sem, recv_sem=recv_sem,
                                        device_id=device, device_id_type=MESH)


def _stage_shards(shards, dtypes, *, name):
    n = len(shards)
    halves = [s.reshape(2, s.shape[0] // 2, s.shape[1]) for s in shards]

    def body(*refs):
        ins, outs, vms, loc_sems = refs[:n], refs[n:2 * n], refs[2 * n:3 * n], refs[3 * n]
        me = 2 * lax.axis_index("x") + lax.axis_index("y")
        copies = []
        for a in range(n):
            vms[a][...] = ins[a][...].astype(dtypes[a])
            cp = pltpu.make_async_copy(vms[a], outs[a].at[me], loc_sems.at[a])
            cp.start()
            copies.append(cp)
        for cp in copies:
            cp.wait()

    return pl.pallas_call(
        body, in_specs=[VMEM_SPEC] * n, out_specs=[ANY] * n,
        out_shape=[jax.ShapeDtypeStruct((N_CHIPS,) + h.shape, dt) for h, dt in zip(halves, dtypes)],
        scratch_shapes=[pltpu.VMEM(h.shape, dt) for h, dt in zip(halves, dtypes)] + [DMA_SEMS((n,))],
        compiler_params=pltpu.CompilerParams(vmem_limit_bytes=VMEM_LIMIT), name=name)(*halves)


def _like(arrays):
    return [jax.ShapeDtypeStruct(a.shape, a.dtype) for a in arrays]


def _axes():
    x, y, c = lax.axis_index("x"), lax.axis_index("y"), lax.axis_index("c")
    first = (x + (1 - c) * (1 - 2 * x), y + c * (1 - 2 * y))
    second = (x + c * (1 - 2 * x), y + (1 - c) * (1 - 2 * y))
    slots = tuple(2 * px + py for px, py in ((x, y), first, second, (1 - x, 1 - y)))
    return (x, y, c), (*first, c), (*second, c), slots


def _gather_ici_phase(bufs):
    n = len(bufs)

    def stage1(ins, outs, sems):
        (x, y, c), peer1, peer2, (own, s1, s2, both) = _axes()
        starts, arrivals = [], []
        for a in range(n):
            mine, land = outs[a].at[own, c], outs[a].at[s2, c]
            starts.append(_remote(mine, mine, *sems(a), peer2))
            arrivals.append(_remote(land, land, *sems(a), peer2))
        return starts, arrivals

    def stage2(ins, outs, sems):
        (x, y, c), peer1, peer2, (own, s1, s2, both) = _axes()
        starts, arrivals = [], []
        for a in range(n):
            for k, (src, dst) in enumerate(((own, s1), (s2, both))):
                mine, land = outs[a].at[src, c], outs[a].at[dst, c]
                starts.append(_remote(mine, mine, *sems(2 * a + k), peer1))
                arrivals.append(_remote(land, land, *sems(2 * a + k), peer1))
        return starts, arrivals

    same = {a: a for a in range(n)}
    return _Phase(bufs, _like(bufs), same, n, stage1).then(_Phase(bufs, _like(bufs), same, 2 * n, stage2))


def _gather_d2d_phase(bufs):
    n = len(bufs)

    def copies(ins, outs, sems):
        (x, y, c), peer1, peer2, (own, s1, s2, both) = _axes()
        starts, arrivals = [], []
        for a in range(n):
            for j, s in enumerate((s1, s2, both)):
                got, land = outs[a].at[s, c], outs[a].at[s, 1 - c]
                starts.append(_remote(got, got, *sems(3 * a + j), (x, y, 1 - c)))
                arrivals.append(_remote(land, land, *sems(3 * a + j), (x, y, 1 - c)))
        return starts, arrivals

    return _Phase(bufs, _like(bufs), {a: a for a in range(n)}, 3 * n, copies)


def _exchange_phase(views):
    n = len(views)

    def copies(ins, outs, sems):
        x, y, c = lax.axis_index("x"), lax.axis_index("y"), lax.axis_index("c")
        starts = [_remote(ins[a].at[pl.ds(0, ins[a].shape[0]), 1 - c], outs[a], *sems(a), (x, y, 1 - c))
                  for a in range(n)]
        return starts, starts

    outs = [jax.ShapeDtypeStruct((v.shape[0],) + v.shape[2:], F32) for v in views]
    return _Phase(views, outs, {}, n, copies)


def _row_block(rows):
    for cand in (256, 176, 128, 64, 32, 16, 8):
        if rows % cand == 0:
            return cand
    return rows


def _add_own_half(g, r, sel, *, name):
    ns, _, rh, cdim = g.shape
    tr = _row_block(rh)

    def body(s_ref, gk_ref, rk_ref, gs_ref, rs_ref, keep_ref, send_ref):
        keep_ref[...] = gk_ref[...] + rk_ref[...]
        send_ref[...] = (gs_ref[...] + rs_ref[...]).astype(BF16)

    def g_spec(off):
        return pl.BlockSpec((None, None, tr, cdim), lambda k, i, s: (s[1 + off + k], s[0], i, 0))

    def r_spec(off):
        return pl.BlockSpec((None, tr, cdim), lambda k, i, s: (s[1 + off + k], i, 0))

    out = pl.BlockSpec((None, tr, cdim), lambda k, i, s: (k, i, 0))
    return pl.pallas_call(
        body,
        grid_spec=pltpu.PrefetchScalarGridSpec(
            num_scalar_prefetch=1, grid=(2, rh // tr),
            in_specs=[g_spec(0), r_spec(0), g_spec(2), r_spec(2)], out_specs=[out, out]),
        out_shape=[jax.ShapeDtypeStruct((2, rh, cdim), F32), jax.ShapeDtypeStruct((2, rh, cdim), BF16)],
        compiler_params=_params("arbitrary", "arbitrary"), name=name)(sel, g, r, g, r)


def _swap_phase(arrays, stage):
    n = len(arrays)

    def copies(ins, outs, sems):
        peer = _axes()[stage]
        starts = [_remote(ins[a], outs[a], *sems(a), peer) for a in range(n)]
        return starts, starts

    return _Phase(arrays, _like(arrays), {}, n, copies)


def _add_stage1(keep, got, *, name):
    _, rh, cdim = keep.shape
    tr = _row_block(rh)

    def body(k_ref, g_ref, keep_ref, send_ref):
        keep_ref[...] = k_ref[0] + g_ref[0].astype(F32)
        send_ref[...] = (k_ref[1] + g_ref[1].astype(F32)).astype(BF16)

    blk2 = pl.BlockSpec((2, tr, cdim), lambda i: (0, i, 0))
    blk = pl.BlockSpec((tr, cdim), lambda i: (i, 0))
    return pl.pallas_call(
        body, grid=(rh // tr,), in_specs=[blk2, blk2], out_specs=[blk, blk],
        out_shape=[jax.ShapeDtypeStruct((rh, cdim), F32), jax.ShapeDtypeStruct((rh, cdim), BF16)],
        compiler_params=_params("arbitrary"), name=name)(keep, got)


def _add_stage2(keep, got, sel, *, name):
    rh, cdim = keep.shape
    tr = _row_block(rh)

    def body(s_ref, k_ref, g_ref, o_ref):
        o_ref[...] = k_ref[...] + g_ref[...].astype(F32)

    blk = pl.BlockSpec((tr, cdim), lambda i, s: (i, 0))
    return pl.pallas_call(
        body,
        grid_spec=pltpu.PrefetchScalarGridSpec(
            num_scalar_prefetch=1, grid=(rh // tr,), in_specs=[blk, blk],
            out_specs=pl.BlockSpec((None, tr, cdim), lambda i, s: (s[0], i, 0))),
        out_shape=jax.ShapeDtypeStruct((2, rh, cdim), F32),
        compiler_params=_params("arbitrary"), name=name)(sel, keep, got)


def _join_phase(halves):
    n = len(halves)

    def copies(ins, outs, sems):
        x, y, c = lax.axis_index("x"), lax.axis_index("y"), lax.axis_index("c")
        starts, arrivals = [], []
        for a in range(n):
            mine, land = outs[a].at[c], outs[a].at[1 - c]
            starts.append(_remote(mine, mine, *sems(a), (x, y, 1 - c)))
            arrivals.append(_remote(land, land, *sems(a), (x, y, 1 - c)))
        return starts, arrivals

    return _Phase(halves, _like(halves), {a: a for a in range(n)}, n, copies)


def _slot_order():
    x, y, c = lax.axis_index("x"), lax.axis_index("y"), lax.axis_index("c")
    own, flip_x, flip_y, both = 2 * x + y, 2 * (1 - x) + y, 2 * x + 1 - y, 2 * (1 - x) + 1 - y
    first = jnp.where(c == 0, flip_x, flip_y)
    second = jnp.where(c == 0, flip_y, flip_x)
    return jnp.stack([c, own, second, first, both]).astype(jnp.int32)


def _reduce_scatter(grads, join_also=()):
    sel = _slot_order()
    views = [_half_view(g) for g in grads]
    got = _run_phase(_exchange_phase(views), name="rs_exchange_halves")
    keep, send = _add_halves(views, got, sel, "late")
    got = _run_phase(_swap_phase(send, 1), name="rs_swap_first_axis")
    keep, send = _add_first(keep, got, "late")
    got = _run_phase(_swap_phase(send, 2), name="rs_swap_second_axis")
    halves = _add_second(keep, got, sel, "late")
    full = _run_phase(_join_phase(halves + list(join_also)), name="rs_join_halves")
    return [f.reshape(-1, f.shape[-1]) for f in full]


def _half_view(g):
    return g.reshape(N_CHIPS, 2, g.shape[1] // 2, g.shape[2])


def _add_halves(views, got, sel, tag):
    keep, send = zip(*[_add_own_half(v, r, sel, name=f"rs_add_half_{tag}{a}") for a, (v, r) in enumerate(zip(views, got))])
    return list(keep), list(send)


def _add_first(keep, got, tag):
    keep, send = zip(*[_add_stage1(k, r, name=f"rs_add_first_{tag}{a}") for a, (k, r) in enumerate(zip(keep, got))])
    return list(keep), list(send)


def _add_second(keep, got, sel, tag):
    return [_add_stage2(k, r, sel, name=f"rs_add_second_{tag}{a}") for a, (k, r) in enumerate(zip(keep, got))]


EARLY_GRADS = ("wg2", "wu2", "wd2", "wout")
MIDDLE_GRADS = ("win",)


class _Overlap:
    EARLY_AT = ("conv_bwd_taps", "attn_bwd", "attn_grad_combine", "inproj_bwd_w")
    MIDDLE_AT = ("inproj_bwd_act", "ffn1_bwd_act", "ffn1_bwd_w", None)

    def __init__(self, staged):
        self.staged = staged
        sel = _slot_order()
        self.early = _Reduction(EARLY_GRADS, "early", sel)
        self.middle = _Reduction(MIDDLE_GRADS, "middle", sel)

    def phase(self, point, w, g):
        if point == "ffn1_fwd":
            return _gather_ici_phase(self.staged)
        if point == "inproj_fwd":
            return _gather_d2d_phase(self.ffn2)
        for red, at in ((self.early, self.EARLY_AT), (self.middle, self.MIDDLE_AT)):
            if point in at:
                return red.phase(at.index(point), g)
        return None

    def done(self, point, outs, w, g):
        if point == "ffn1_fwd":
            win, wout, taps = [_whole(b) for b in _run_phase(_gather_d2d_phase(outs[:3]), name="gather_mix_d2d")]
            w["win"] = win
            w["wout"] = wout.reshape(-1, wout.shape[-1])
            w["conv_w"] = taps.transpose(1, 0, 2).reshape(CONV_K + 1, D_CONV)[:CONV_K]
            self.ffn2 = list(outs[3:])
        elif point == "inproj_fwd":
            w["wg2"], w["wu2"], w["wd2"] = [_whole(b) for b in outs]
        for red, at in ((self.early, self.EARLY_AT), (self.middle, self.MIDDLE_AT)):
            if point in at:
                red.done(at.index(point), outs)


class _Reduction:
    def __init__(self, names, tag, sel):
        self.names, self.tag, self.sel = names, tag, sel
        self.reduced = {}

    def phase(self, stage, g):
        if stage == 0:
            self.cols = [g[k].shape[-1] for k in self.names]
            self.views = [_half_view(g[k].reshape(N_CHIPS, -1, g[k].shape[-1])) for k in self.names]
            return _exchange_phase(self.views)
        if stage in (1, 2):
            return _swap_phase(self.send, stage)
        return _join_phase(self.halves)

    def done(self, stage, outs):
        if stage == 0:
            self.keep, self.send = _add_halves(self.views, outs, self.sel, self.tag)
        elif stage == 1:
            self.keep, self.send = _add_first(self.keep, outs, self.tag)
        elif stage == 2:
            self.halves = _add_second(self.keep, outs, self.sel, self.tag)
        else:
            for k, c, f in zip(self.names, self.cols, outs):
                self.reduced[k] = f.reshape(-1, c)


def _whole(buf):
    return buf.reshape(buf.shape[0], 2 * buf.shape[2], buf.shape[3])


def _allreduce_small(pack, *, name):
    rows = pack.shape[0]

    def body(p_ref, o_ref, buf_ref, send_sems, recv_sems):
        x, y, c = lax.axis_index("x"), lax.axis_index("y"), lax.axis_index("c")
        me = 4 * x + 2 * y + c
        buf_ref[me] = p_ref[...]
        cps = []
        for k in range(1, N_DEV):
            peer = tuple(1 - v if (k >> s) & 1 else v for v, s in ((x, 2), (y, 1), (c, 0)))
            cp = _remote(p_ref, buf_ref.at[me], send_sems.at[k - 1], recv_sems.at[k - 1], peer)
            cp.start()
            cps.append(cp)
        for k in range(1, N_DEV):
            src = 4 * (x ^ ((k >> 2) & 1)) + 2 * (y ^ ((k >> 1) & 1)) + (c ^ (k & 1))
            land = buf_ref.at[src]
            _remote(land, land, send_sems.at[k - 1], recv_sems.at[k - 1], (x, y, c)).wait_recv()
        acc = buf_ref[0]
        for d in range(1, N_DEV):
            acc = acc + buf_ref[d]
        o_ref[...] = acc
        for cp in cps:
            cp.wait_send()

    return pl.pallas_call(
        body, in_specs=[VMEM_SPEC], out_specs=VMEM_SPEC, out_shape=jax.ShapeDtypeStruct(pack.shape, F32),
        scratch_shapes=[pltpu.VMEM((N_DEV, rows, LANES), F32), pltpu.SemaphoreType.DMA((N_DEV - 1,)),
                        pltpu.SemaphoreType.DMA((N_DEV - 1,))], name=name)(pack)


SMALL = ("ffn1_norm", "mix_norm", "q_norm", "k_norm", "conv_b", "conv_ln_g", "conv_ln_b", "ffn2_norm", "conv_w")
BIG = ("ffn1_w_gate", "ffn1_w_up", "ffn1_w_down", "w_in", "w_out", "ffn2_w_gate", "ffn2_w_up", "ffn2_w_down")
TRANSPOSED = ("ffn1_w_gate", "ffn1_w_up", "ffn2_w_gate", "ffn2_w_up")
WEIGHTS = ("ffn1_norm", "ffn1_w_gate", "ffn1_w_up", "ffn1_w_down", "mix_norm", "w_in", "q_norm", "k_norm",
           "conv_w", "conv_b", "conv_ln_g", "conv_ln_b", "w_out", "ffn2_norm", "ffn2_w_gate", "ffn2_w_up",
           "ffn2_w_down")


def _pack(parts):
    rows = []
    for p in parts:
        flat = p.reshape(-1)
        tile = SUBLANES * LANES
        padded = -(-flat.shape[0] // tile) * tile
        rows.append(jnp.pad(flat, (0, padded - flat.shape[0])).reshape(-1, LANES))
    return jnp.concatenate(rows, axis=0)


def _unpack(pack, shapes):
    out, row = [], 0
    for shp in shapes:
        size = shp[0] * shp[1]
        tile = SUBLANES * LANES
        nrows = -(-size // tile) * SUBLANES
        out.append(pack[row:row + nrows].reshape(-1)[:size].reshape(shp))
        row += nrows
    return out


def kernel(x, ffn1_norm, ffn1_w_gate, ffn1_w_up, ffn1_w_down, mix_norm, w_in, q_norm, k_norm, conv_w, conv_b, conv_ln_g, conv_ln_b, w_out, ffn2_norm, ffn2_w_gate, ffn2_w_up, ffn2_w_down, loss_target, m_ffn1_norm, m_ffn1_w_gate, m_ffn1_w_up, m_ffn1_w_down, m_mix_norm, m_w_in, m_q_norm, m_k_norm, m_conv_w, m_conv_b, m_conv_ln_g, m_conv_ln_b, m_w_out, m_ffn2_norm, m_ffn2_w_gate, m_ffn2_w_up, m_ffn2_w_down, v_ffn1_norm, v_ffn1_w_gate, v_ffn1_w_up, v_ffn1_w_down, v_mix_norm, v_w_in, v_q_norm, v_k_norm, v_conv_w, v_conv_b, v_conv_ln_g, v_conv_ln_b, v_w_out, v_ffn2_norm, v_ffn2_w_gate, v_ffn2_w_up, v_ffn2_w_down):
    wts = dict(ffn1_norm=ffn1_norm, ffn1_w_gate=ffn1_w_gate[0], ffn1_w_up=ffn1_w_up[0], ffn1_w_down=ffn1_w_down[0],
               mix_norm=mix_norm, w_in=w_in[0], q_norm=q_norm, k_norm=k_norm, conv_w=conv_w[0], conv_b=conv_b,
               conv_ln_g=conv_ln_g, conv_ln_b=conv_ln_b, w_out=w_out[0], ffn2_norm=ffn2_norm,
               ffn2_w_gate=ffn2_w_gate[0], ffn2_w_up=ffn2_w_up[0], ffn2_w_down=ffn2_w_down[0])
    mom = dict(ffn1_norm=m_ffn1_norm, ffn1_w_gate=m_ffn1_w_gate[0], ffn1_w_up=m_ffn1_w_up[0], ffn1_w_down=m_ffn1_w_down[0],
               mix_norm=m_mix_norm, w_in=m_w_in[0], q_norm=m_q_norm, k_norm=m_k_norm, conv_w=m_conv_w[0], conv_b=m_conv_b,
               conv_ln_g=m_conv_ln_g, conv_ln_b=m_conv_ln_b, w_out=m_w_out[0], ffn2_norm=m_ffn2_norm,
               ffn2_w_gate=m_ffn2_w_gate[0], ffn2_w_up=m_ffn2_w_up[0], ffn2_w_down=m_ffn2_w_down[0])
    var = dict(ffn1_norm=v_ffn1_norm, ffn1_w_gate=v_ffn1_w_gate[0], ffn1_w_up=v_ffn1_w_up[0], ffn1_w_down=v_ffn1_w_down[0],
               mix_norm=v_mix_norm, w_in=v_w_in[0], q_norm=v_q_norm, k_norm=v_k_norm, conv_w=v_conv_w[0], conv_b=v_conv_b,
               conv_ln_g=v_conv_ln_g, conv_ln_b=v_conv_ln_b, w_out=v_w_out[0], ffn2_norm=v_ffn2_norm,
               ffn2_w_gate=v_ffn2_w_gate[0], ffn2_w_up=v_ffn2_w_up[0], ffn2_w_down=v_ffn2_w_down[0])
    chip = 2 * lax.axis_index("x") + lax.axis_index("y")
    for src in (wts, mom, var):
        for n in TRANSPOSED:
            src[n] = src[n].T

    taps = jnp.pad(wts["conv_w"], ((0, 1), (0, 0)))
    staged = _stage_shards([wts["ffn1_w_gate"], wts["ffn1_w_up"], wts["ffn1_w_down"], wts["w_in"], wts["w_out"], taps,
                            wts["ffn2_w_gate"], wts["ffn2_w_up"], wts["ffn2_w_down"]],
                           [BF16, BF16, BF16, BF16, BF16, F32, BF16, BF16, BF16], name="stage_shards")
    first = _run_phase(_gather_ici_phase(staged[:3]).then(_gather_d2d_phase(staged[:3])), name="gather_ffn1")
    wg1, wu1, wd1 = [_whole(b) for b in first]
    w = dict(ffn1_norm=ffn1_norm, mix_norm=mix_norm, ffn2_norm=ffn2_norm, q_norm=q_norm, k_norm=k_norm,
             conv_b=conv_b, conv_ln_g=conv_ln_g, conv_ln_b=conv_ln_b, wg1=wg1, wu1=wu1, wd1=wd1)
    overlap = _Overlap(staged[3:])
    loss_part, grad_x, g = _local_step(x, loss_target, w, overlap)

    late = _reduce_scatter([g["wg1"], g["wu1"], g["wd1"]], join_also=overlap.middle.halves)
    big_grads = dict(zip(("ffn1_w_gate", "ffn1_w_up", "ffn1_w_down", "w_in"), late))
    early = overlap.early.reduced
    big_grads.update(ffn2_w_gate=early["wg2"], ffn2_w_up=early["wu2"], ffn2_w_down=early["wd2"], w_out=early["wout"])

    small_shapes = [g[n].shape for n in SMALL] + [(SUBLANES, LANES)]
    red = _allreduce_small(_pack([g[n] for n in SMALL] + [loss_part]), name="allreduce_small")
    small = dict(zip(SMALL + ("loss",), _unpack(red, small_shapes)))
    loss = small["loss"][0, 0]
    small["conv_w"] = lax.dynamic_slice_in_dim(small["conv_w"], chip * LANES, LANES, axis=1)

    grads, delta, new_m, new_v = {}, {}, {}, {}
    for n in BIG:
        grads[n] = big_grads[n]
        delta[n], new_m[n], new_v[n] = _adamw(wts[n], grads[n], mom[n], var[n], name=f"adamw_{n}")
    shapes = [wts[n].shape for n in SMALL]
    packs = [_pack([src[n] for n in SMALL]) for src in (wts, small, mom, var)]
    outs = _adamw(*packs, name="adamw_small")
    for dst, pk in zip((delta, new_m, new_v), outs):
        dst.update(zip(SMALL, _unpack(pk, shapes)))
    for n in SMALL:
        grads[n] = small[n]

    def shaped(d, n):
        v = d[n].T if n in TRANSPOSED else d[n]
        return v.reshape((1,) + v.shape) if n in BIG or n == "conv_w" else v

    return (loss, grad_x, *[shaped(grads, n) for n in WEIGHTS], *[shaped(delta, n) for n in WEIGHTS],
            *[shaped(new_m, n) for n in WEIGHTS], *[shaped(new_v, n) for n in WEIGHTS])
```

```python
import functools

import jax
import jax.numpy as jnp
from jax import lax
from jax.experimental import pallas as pl
from jax.experimental.pallas import tpu as pltpu

F32 = jnp.float32
BF16 = jnp.bfloat16

EPS = 1e-6
HEADS = 8
HEAD_DIM = 64
D_ATTN = HEADS * HEAD_DIM
D_CONV = 512
CONV_K = 31
QBLK = 128
N_PATTERNS = 3
DILATIONS = (1, 4, 16)
LANES = 128
NEG = -1e30

ADAM_LR = 0.001
ADAM_B1 = 0.9
ADAM_B2 = 0.999
ADAM_EPS = 1e-08
ADAM_WD = 0.01
ADAM_STEP = 10

VMEM_LIMIT = 56 * 1024 * 1024
MESH = pl.DeviceIdType.MESH

NT_DIMS = (((1,), (1,)), ((), ()))
TN_DIMS = (((0,), (0,)), ((), ()))


def _params(*sem):
    return pltpu.CompilerParams(dimension_semantics=sem, vmem_limit_bytes=VMEM_LIMIT)


def _dot(a, b):
    return jnp.dot(a, b, preferred_element_type=F32)


def _dot_nt(a, b):
    return lax.dot_general(a, b, NT_DIMS, preferred_element_type=F32)


def _dot_tn(a, b):
    return lax.dot_general(a, b, TN_DIMS, preferred_element_type=F32)


def _sigmoid(x):
    return 1.0 / (1.0 + jnp.exp(-x))


def _seg_mean(v, e_ref, width):
    hi = v.astype(BF16)
    lo = (v - hi.astype(F32)).astype(BF16)
    e = e_ref[...]
    return (_dot(hi, e) + _dot(lo, e)) * (1.0 / width)


def _seg_matrix(n):
    i = jnp.arange(n)
    return (i[:, None] // HEAD_DIM == i[None, :] // HEAD_DIM).astype(BF16)


ANY = pl.BlockSpec(memory_space=pl.ANY)
DMA_SEMS = pltpu.SemaphoreType.DMA


class _Phase:
    def __init__(self, ins, outs, aliases, nsem, copies):
        self.ins, self.outs, self.aliases = list(ins), list(outs), dict(aliases)
        self.stages = [(nsem, copies)]

    def then(self, other):
        self.stages = self.stages + other.stages
        return self

    @property
    def nsem(self):
        return sum(n for n, _ in self.stages)

    def _copies(self, k, in_refs, out_refs, send_sems, recv_sems):
        base = sum(n for n, _ in self.stages[:k])
        return self.stages[k][1](in_refs, out_refs, lambda i: (send_sems.at[base + i], recv_sems.at[base + i]))

    def start(self, k, *refs):
        for cp in self._copies(k, *refs)[0]:
            cp.start()

    def finish(self, k, *refs):
        starts, arrivals = self._copies(k, *refs)
        for cp in arrivals:
            cp.wait_recv()
        for cp in starts:
            cp.wait_send()


def _run_phase(phase, *, name):
    n_in, n_out = len(phase.ins), len(phase.outs)

    def body(*refs):
        ins, outs = refs[:n_in], refs[n_in:n_in + n_out]
        send_sems, recv_sems = refs[n_in + n_out:]
        for k in range(len(phase.stages)):
            phase.start(k, ins, outs, send_sems, recv_sems)
            phase.finish(k, ins, outs, send_sems, recv_sems)

    return pl.pallas_call(
        body, in_specs=[ANY] * n_in, out_specs=[ANY] * n_out, out_shape=phase.outs,
        input_output_aliases=phase.aliases,
        scratch_shapes=[DMA_SEMS((phase.nsem,)), DMA_SEMS((phase.nsem,))], name=name)(*phase.ins)


def _call(body, *, grid, in_specs, out_specs, out_shape, scratch_shapes=(), sem, name, args, phase=None):
    in_specs, out_specs, out_shape = list(in_specs), list(out_specs), list(out_shape)
    scratch_shapes = list(scratch_shapes)
    if phase is None:
        return pl.pallas_call(body, grid=grid, in_specs=in_specs, out_specs=out_specs, out_shape=out_shape,
                              scratch_shapes=scratch_shapes, compiler_params=_params(*sem), name=name)(*args)
    n_in, n_out, n_scr = len(in_specs), len(out_specs), len(scratch_shapes)
    p_in, p_out = len(phase.ins), len(phase.outs)

    def hosted(*refs):
        ins, pins = refs[:n_in], refs[n_in:n_in + p_in]
        o0 = n_in + p_in
        outs, pouts = refs[o0:o0 + n_out], refs[o0 + n_out:o0 + n_out + p_out]
        s0 = o0 + n_out + p_out
        scr = refs[s0:s0 + n_scr]
        send_sems, recv_sems = refs[s0 + n_scr:]
        step = 0
        for d, n in enumerate(grid):
            step = step * n + pl.program_id(d)
        nsteps = functools.reduce(lambda a, b: a * b, grid)
        nstages = len(phase.stages)
        comm_refs = (pins, pouts, send_sems, recv_sems)

        for k in range(nstages):
            @pl.when(step == (k * nsteps) // nstages)
            def _(k=k):
                if k > 0:
                    phase.finish(k - 1, *comm_refs)
                phase.start(k, *comm_refs)

        body(*ins, *outs, *scr)

        @pl.when(step == nsteps - 1)
        def _():
            phase.finish(nstages - 1, *comm_refs)

    res = pl.pallas_call(
        hosted, grid=grid, in_specs=in_specs + [ANY] * p_in, out_specs=out_specs + [ANY] * p_out,
        out_shape=out_shape + phase.outs,
        input_output_aliases={n_in + i: n_out + o for i, o in phase.aliases.items()},
        scratch_shapes=scratch_shapes + [DMA_SEMS((phase.nsem,)), DMA_SEMS((phase.nsem,))],
        compiler_params=_params(*sem), name=name)(*args, *phase.ins)
    return res[:n_out], res[n_out:]


ROW_CHUNK = 256


def _ffn_fwd(x, gain, wg, wu, wd, tgt, *, tm, name, phase=None):
    T, D = x.shape
    NS, Fs, _ = wg.shape
    with_loss = tgt is not None

    def body(*refs):
        if with_loss:
            x_ref, g_ref, wg_ref, wu_ref, wd_ref, t_ref, h_ref, n_ref, G_ref, U_ref, loss_ref, acc_ref = refs
        else:
            x_ref, g_ref, wg_ref, wu_ref, wd_ref, h_ref, n_ref, G_ref, U_ref, acc_ref = refs
        i = pl.program_id(0)
        j = pl.program_id(1)

        @pl.when(j == 0)
        def _():
            xv = x_ref[...]
            r = lax.rsqrt(jnp.mean(xv * xv, axis=-1, keepdims=True) + EPS)
            n_ref[...] = (xv * r * g_ref[...]).astype(BF16)
            acc_ref[...] = jnp.zeros_like(acc_ref)

        n = n_ref[...]
        G = _dot_nt(n, wg_ref[...])
        U = _dot_nt(n, wu_ref[...])
        G_ref[...] = G.astype(BF16)
        U_ref[...] = U.astype(BF16)
        A = (G * _sigmoid(G) * U).astype(BF16)
        acc_ref[...] += _dot(A, wd_ref[...])

        @pl.when(j == NS - 1)
        def _():
            h = x_ref[...] + 0.5 * acc_ref[...]
            if with_loss:
                e = h - t_ref[...]
                h_ref[...] = e * (1.0 / D)

                @pl.when(i == 0)
                def _():
                    loss_ref[...] = jnp.zeros_like(loss_ref)

                loss_ref[...] += jnp.sum(e * e) * (0.5 / D)
            else:
                h_ref[...] = h

    tok = pl.BlockSpec((tm, D), lambda i, j: (i, 0))
    in_specs = [tok, pl.BlockSpec((1, D), lambda i, j: (0, 0)),
                pl.BlockSpec((None, Fs, D), lambda i, j: (j, 0, 0)),
                pl.BlockSpec((None, Fs, D), lambda i, j: (j, 0, 0)),
                pl.BlockSpec((None, Fs, D), lambda i, j: (j, 0, 0))]
    args = [x, gain, wg, wu, wd]
    act = pl.BlockSpec((None, tm, Fs), lambda i, j: (j, i, 0))
    out_shape = [jax.ShapeDtypeStruct((T, D), F32), jax.ShapeDtypeStruct((T, D), BF16),
                 jax.ShapeDtypeStruct((NS, T, Fs), BF16), jax.ShapeDtypeStruct((NS, T, Fs), BF16)]
    out_specs = [tok, tok, act, act]
    if with_loss:
        in_specs.append(tok)
        args.append(tgt)
        out_shape.append(jax.ShapeDtypeStruct((8, LANES), F32))
        out_specs.append(pl.BlockSpec((8, LANES), lambda i, j: (0, 0)))
    return _call(body, grid=(T // tm, NS), in_specs=in_specs, out_specs=out_specs, out_shape=out_shape,
                 scratch_shapes=[pltpu.VMEM((tm, D), F32)], sem=("arbitrary", "arbitrary"), name=name,
                 args=args, phase=phase)


def _rms_bwd(xv, gain, dn):
    r = lax.rsqrt(jnp.mean(xv * xv, axis=-1, keepdims=True) + EPS)
    xhat = xv * r
    dxh = dn * gain
    dx = r * (dxh - xhat * jnp.mean(dxh * xhat, axis=-1, keepdims=True))
    dg = jnp.sum(dn * xhat, axis=0, keepdims=True)
    return dx, dg


def _ffn_bwd_act(dh, x, gain, G, U, wg, wu, wd, *, tm, name, phase=None):
    T, D = x.shape
    NS, Fs, _ = wg.shape

    def body(dh_ref, x_ref, g_ref, G_ref, U_ref, wg_ref, wu_ref, wd_ref,
             dG_ref, dU_ref, A_ref, dy_ref, dx_ref, dg_ref, acc_ref):
        i = pl.program_id(0)
        j = pl.program_id(1)

        @pl.when(j == 0)
        def _():
            dy_ref[...] = (0.5 * dh_ref[...]).astype(BF16)
            acc_ref[...] = jnp.zeros_like(acc_ref)

        @pl.when((i == 0) & (j == 0))
        def _():
            dg_ref[...] = jnp.zeros_like(dg_ref)

        nchunks = tm // ROW_CHUNK
        dA, dGU = {}, {}
        for step in range(nchunks + 2):
            if step < nchunks:
                rows = slice(step * ROW_CHUNK, (step + 1) * ROW_CHUNK)
                dA[step] = _dot_nt(dy_ref[rows, :], wd_ref[...])
            if 1 <= step <= nchunks:
                k = step - 1
                rows = slice(k * ROW_CHUNK, (k + 1) * ROW_CHUNK)
                Gv = G_ref[rows, :].astype(F32)
                Uv = U_ref[rows, :].astype(F32)
                sig = _sigmoid(Gv)
                s = Gv * sig
                dG = (dA[k] * Uv * (sig * (1.0 + Gv * (1.0 - sig)))).astype(BF16)
                dU = (dA.pop(k) * s).astype(BF16)
                dG_ref[rows, :] = dG
                dU_ref[rows, :] = dU
                A_ref[rows, :] = (s * Uv).astype(BF16)
                dGU[k] = (dG, dU)
            if 2 <= step:
                k = step - 2
                rows = slice(k * ROW_CHUNK, (k + 1) * ROW_CHUNK)
                dG, dU = dGU.pop(k)
                acc_ref[rows, :] += _dot(dG, wg_ref[...]) + _dot(dU, wu_ref[...])

        @pl.when(j == NS - 1)
        def _():
            dx, dg = _rms_bwd(x_ref[...], g_ref[...], acc_ref[...])
            dx_ref[...] = dh_ref[...] + dx
            dg_ref[...] += dg

    tok = pl.BlockSpec((tm, D), lambda i, j: (i, 0))
    act = pl.BlockSpec((None, tm, Fs), lambda i, j: (j, i, 0))
    vec = pl.BlockSpec((1, D), lambda i, j: (0, 0))
    return _call(
        body, grid=(T // tm, NS),
        in_specs=[tok, tok, vec, act, act,
                  pl.BlockSpec((None, Fs, D), lambda i, j: (j, 0, 0)),
                  pl.BlockSpec((None, Fs, D), lambda i, j: (j, 0, 0)),
                  pl.BlockSpec((None, Fs, D), lambda i, j: (j, 0, 0))],
        out_specs=[act, act, act, tok, tok, vec],
        out_shape=[jax.ShapeDtypeStruct((NS, T, Fs), BF16)] * 3
        + [jax.ShapeDtypeStruct((T, D), BF16), jax.ShapeDtypeStruct((T, D), F32),
           jax.ShapeDtypeStruct((1, D), F32)],
        scratch_shapes=[pltpu.VMEM((tm, D), F32)],
        sem=("arbitrary", "arbitrary"), name=name, args=(dh, x, gain, G, U, wg, wu, wd), phase=phase)


def _ffn_bwd_w(n, dy, dG, dU, A, *, tk, name, phase=None):
    T, D = n.shape
    NS, _, Fs = dG.shape

    def body(n_ref, dy_ref, dG_ref, dU_ref, A_ref, wg_ref, wu_ref, wd_ref):
        @pl.when(pl.program_id(1) == 0)
        def _():
            wg_ref[...] = jnp.zeros_like(wg_ref)
            wu_ref[...] = jnp.zeros_like(wu_ref)
            wd_ref[...] = jnp.zeros_like(wd_ref)

        nv = n_ref[...]
        wg_ref[...] += _dot_tn(dG_ref[...], nv)
        wu_ref[...] += _dot_tn(dU_ref[...], nv)
        wd_ref[...] += _dot_tn(A_ref[...], dy_ref[...])

    tok = pl.BlockSpec((tk, D), lambda j, k: (k, 0))
    act = pl.BlockSpec((None, tk, Fs), lambda j, k: (j, k, 0))
    return _call(
        body, grid=(NS, T // tk), in_specs=[tok, tok, act, act, act],
        out_specs=[pl.BlockSpec((None, Fs, D), lambda j, k: (j, 0, 0))] * 3,
        out_shape=[jax.ShapeDtypeStruct((NS, Fs, D), F32)] * 3,
        sem=("arbitrary", "arbitrary"), name=name, args=(n, dy, dG, dU, A), phase=phase)


def _inproj_fwd(h, gain, win, *, tm, name, phase=None):
    T, D = h.shape
    NS, _, Cs = win.shape

    def body(h_ref, g_ref, w_ref, u_ref, n_ref):
        @pl.when(pl.program_id(1) == 0)
        def _():
            xv = h_ref[...]
            r = lax.rsqrt(jnp.mean(xv * xv, axis=-1, keepdims=True) + EPS)
            n_ref[...] = (xv * r * g_ref[...]).astype(BF16)

        u_ref[...] = _dot(n_ref[...], w_ref[...])

    tok = pl.BlockSpec((tm, D), lambda i, j: (i, 0))
    return _call(
        body, grid=(T // tm, NS),
        in_specs=[tok, pl.BlockSpec((1, D), lambda i, j: (0, 0)),
                  pl.BlockSpec((None, D, Cs), lambda i, j: (j, 0, 0))],
        out_specs=[pl.BlockSpec((tm, Cs), lambda i, j: (i, j)), tok],
        out_shape=[jax.ShapeDtypeStruct((T, NS * Cs), F32), jax.ShapeDtypeStruct((T, D), BF16)],
        sem=("arbitrary", "arbitrary"), name=name, args=(h, gain, win), phase=phase)


def _inproj_bwd_act(du, dh, h, gain, win, *, tm, name, phase=None):
    T, D = h.shape
    NS, _, Cs = win.shape

    def body(du_ref, dh_ref, h_ref, g_ref, w_ref, dx_ref, dg_ref, acc_ref):
        i = pl.program_id(0)
        j = pl.program_id(1)

        @pl.when(j == 0)
        def _():
            acc_ref[...] = jnp.zeros_like(acc_ref)

        @pl.when((i == 0) & (j == 0))
        def _():
            dg_ref[...] = jnp.zeros_like(dg_ref)

        acc_ref[...] += _dot_nt(du_ref[...], w_ref[...])

        @pl.when(j == NS - 1)
        def _():
            dx, dg = _rms_bwd(h_ref[...], g_ref[...], acc_ref[...])
            dx_ref[...] = dh_ref[...] + dx
            dg_ref[...] += dg

    tok = pl.BlockSpec((tm, D), lambda i, j: (i, 0))
    vec = pl.BlockSpec((1, D), lambda i, j: (0, 0))
    return _call(
        body, grid=(T // tm, NS),
        in_specs=[pl.BlockSpec((tm, Cs), lambda i, j: (i, j)), tok, tok, vec,
                  pl.BlockSpec((None, D, Cs), lambda i, j: (j, 0, 0))],
        out_specs=[tok, vec],
        out_shape=[jax.ShapeDtypeStruct((T, D), F32), jax.ShapeDtypeStruct((1, D), F32)],
        scratch_shapes=[pltpu.VMEM((tm, D), F32)],
        sem=("arbitrary", "arbitrary"), name=name, args=(du, dh, h, gain, win), phase=phase)


def _inproj_bwd_w(n, du, ns, *, tk, name, phase=None):
    T, D = n.shape
    Cs = du.shape[1] // ns

    def body(n_ref, du_ref, w_ref):
        @pl.when(pl.program_id(1) == 0)
        def _():
            w_ref[...] = jnp.zeros_like(w_ref)

        w_ref[...] += _dot_tn(n_ref[...], du_ref[...])

    return _call(
        body, grid=(ns, T // tk),
        in_specs=[pl.BlockSpec((tk, D), lambda j, k: (k, 0)), pl.BlockSpec((tk, Cs), lambda j, k: (k, j))],
        out_specs=[pl.BlockSpec((None, D, Cs), lambda j, k: (j, 0, 0))],
        out_shape=[jax.ShapeDtypeStruct((ns, D, Cs), F32)],
        sem=("arbitrary", "arbitrary"), name=name, args=(n, du), phase=phase)


STRIDE = 4


def _permute(src_ref, tmp_ref, put):
    S = src_ref.shape[0]
    L4, L16 = S // STRIDE, S // (STRIDE * STRIDE)
    put(0, 0, src_ref[...])
    for r0 in range(STRIDE):
        v = src_ref[pl.ds(r0, L4, stride=STRIDE), :]
        put(1, r0 * L4, v)
        tmp_ref[r0 * L4:(r0 + 1) * L4, :] = v
    for r0 in range(STRIDE):
        for r1 in range(STRIDE):
            put(2, (r1 * STRIDE + r0) * L16, tmp_ref[pl.ds(r0 * L4 + r1, L16, stride=STRIDE), :])


def _permute_out(src_ref, tmp_ref, out_ref, cast):
    for cc in range(src_ref.shape[0]):
        cols = slice(cc * LANES, (cc + 1) * LANES)

        def put(p, row0, v, cols=cols):
            out_ref[p, row0:row0 + v.shape[0], cols] = v.astype(cast)

        _permute(src_ref.at[cc], tmp_ref, put)


def _unpermute_in(get_block, dst_ref, tmp_ref, p, S):
    L4, L16 = S // STRIDE, S // (STRIDE * STRIDE)
    if p == 0:
        dst_ref[...] = get_block(0, S)
        return
    if p == 1:
        for r0 in range(STRIDE):
            dst_ref[pl.ds(r0, L4, stride=STRIDE), :] = get_block(r0 * L4, L4)
        return
    for r0 in range(STRIDE):
        for r1 in range(STRIDE):
            tmp_ref[pl.ds(r0 * L4 + r1, L16, stride=STRIDE), :] = get_block((r1 * STRIDE + r0) * L16, L16)
    for r0 in range(STRIDE):
        dst_ref[pl.ds(r0, L4, stride=STRIDE), :] = tmp_ref[r0 * L4:(r0 + 1) * L4, :]


def _qkv_prep(u, gains, B, S, *, name):
    emat = _seg_matrix(D_ATTN)

    def body(u_ref, g_ref, e_ref, out_ref, scr_ref, tmp_ref):
        c = pl.program_id(1)
        xv = u_ref[...]
        ms = _seg_mean(xv * xv, e_ref, HEAD_DIM)
        r = jnp.where(c < 2, lax.rsqrt(ms + EPS), 1.0)
        yv = xv * r * g_ref[...]
        for cc in range(4):
            scr_ref[cc] = yv[:, cc * LANES:(cc + 1) * LANES]
        _permute_out(scr_ref, tmp_ref, out_ref, BF16)

    return pl.pallas_call(
        body, grid=(B, 3),
        in_specs=[pl.BlockSpec((S, D_ATTN), lambda b, c: (b, c)),
                  pl.BlockSpec((None, 1, D_ATTN), lambda b, c: (c, 0, 0)),
                  pl.BlockSpec((D_ATTN, D_ATTN), lambda b, c: (0, 0))],
        out_specs=pl.BlockSpec((None, N_PATTERNS, None, S, D_ATTN), lambda b, c: (c, 0, b, 0, 0)),
        out_shape=jax.ShapeDtypeStruct((3, N_PATTERNS, B, S, D_ATTN), BF16),
        scratch_shapes=[pltpu.VMEM((4, S, LANES), F32), pltpu.VMEM((S, LANES), F32)],
        compiler_params=_params("arbitrary", "arbitrary"), name=name)(u, gains, emat)


def _band_mask(p, b):
    nblk = jnp.right_shift(16, 2 * p)
    has_prev = jnp.bitwise_and(b, nblk - 1) != 0
    qi = lax.broadcasted_iota(jnp.int32, (QBLK, 2 * QBLK), 0)
    ci = lax.broadcasted_iota(jnp.int32, (QBLK, 2 * QBLK), 1)
    dist = QBLK + qi - ci
    return (dist >= 0) & (dist <= QBLK) & (has_prev | (ci >= QBLK))


def _first_head(rows):
    return lax.broadcasted_iota(jnp.int32, (rows, LANES), 1) < HEAD_DIM


def _split_heads(pair):
    first = _first_head(pair.shape[0])
    zero = jnp.zeros_like(pair)
    return jnp.concatenate([jnp.where(first, pair, zero), jnp.where(first, zero, pair)], axis=0)


def _merge_heads(col_a, col_b):
    rows = col_a.shape[0]
    return jnp.where(_first_head(rows), jnp.broadcast_to(col_a, (rows, LANES)), jnp.broadcast_to(col_b, (rows, LANES)))


QB_FWD = 8
QB_BWD = 4


def _attn_fwd(qkv, *, name):
    QB = QB_FWD
    nb = qkv.shape[2]

    def body(q_ref, kp_ref, kc_ref, vp_ref, vc_ref, o_ref, lse_ref):
        kall = jnp.concatenate([kp_ref[...]] + [kc_ref[t] for t in range(QB)], axis=0)
        vall = jnp.concatenate([vp_ref[...]] + [vc_ref[t] for t in range(QB)], axis=0)
        masks = []
        for t in range(QB):
            mask = _band_mask(pl.program_id(0), QB * pl.program_id(1) + t)
            masks.append(jnp.concatenate([mask, mask], axis=0))
        units = [(t, hp) for t in range(QB) for hp in range(HEADS // 2)]
        scores, probs = {}, {}
        for step in range(len(units) + 2):
            if step < len(units):
                t, hp = units[step]
                cols = slice(hp * LANES, (hp + 1) * LANES)
                scores[step] = _dot_nt(_split_heads(q_ref[t, :, cols]), kall[t * QBLK:(t + 2) * QBLK, cols])
            if 1 <= step <= len(units):
                t, hp = units[step - 1]
                cols = slice(hp * LANES, (hp + 1) * LANES)
                s = jnp.where(masks[t], scores.pop(step - 1), NEG)
                m = jnp.max(s, axis=-1, keepdims=True)
                e = jnp.exp(s - m)
                l = jnp.sum(e, axis=-1, keepdims=True)
                probs[step - 1] = (e * (1.0 / l)).astype(BF16)
                lse = m + jnp.log(l)
                lse_ref[t, :, cols] = _merge_heads(lse[:QBLK], lse[QBLK:])
            if 2 <= step:
                t, hp = units[step - 2]
                cols = slice(hp * LANES, (hp + 1) * LANES)
                pr = probs.pop(step - 2)
                o_ref[t, :, cols] = _dot(jnp.concatenate([pr[:QBLK], pr[QBLK:]], axis=1),
                                         _split_heads(vall[t * QBLK:(t + 2) * QBLK, cols]))

    cur = lambda which: pl.BlockSpec((None, None, QB, QBLK, D_ATTN), lambda p, i: (which, p, i, 0, 0))
    prev = lambda which: pl.BlockSpec((None, None, None, QBLK, D_ATTN),
                                      lambda p, i: (which, p, jnp.maximum(QB * i - 1, 0), 0, 0))
    out = pl.BlockSpec((None, QB, QBLK, D_ATTN), lambda p, i: (p, i, 0, 0))
    return pl.pallas_call(
        body, grid=(N_PATTERNS, nb // QB), in_specs=[cur(0), prev(1), cur(1), prev(2), cur(2)], out_specs=[out, out],
        out_shape=[jax.ShapeDtypeStruct((N_PATTERNS, nb, QBLK, D_ATTN), F32)] * 2,
        compiler_params=_params("arbitrary", "arbitrary"), name=name)(qkv, qkv, qkv, qkv, qkv)


def _attn_combine(o3, lse3, B, S, *, name):
    def body(o_ref, l_ref, a_ref, lt_ref, so_ref, sl_ref, tmp_ref):
        for p in range(N_PATTERNS):
            _unpermute_in(lambda r0, n, p=p: o_ref[p, pl.ds(r0, n), :], so_ref.at[p], tmp_ref, p, S)
            _unpermute_in(lambda r0, n, p=p: l_ref[p, pl.ds(r0, n), :], sl_ref.at[p], tmp_ref, p, S)
        l0, l1, l2 = sl_ref[0], sl_ref[1], sl_ref[2]
        m = jnp.maximum(jnp.maximum(l0, l1), l2)
        w0, w1, w2 = jnp.exp(l0 - m), jnp.exp(l1 - m), jnp.exp(l2 - m)
        tot = w0 + w1 + w2
        a_ref[...] = (w0 * so_ref[0] + w1 * so_ref[1] + w2 * so_ref[2]) / tot
        lt_ref[...] = m + jnp.log(tot)

    o3 = o3.reshape(N_PATTERNS, B, S, D_ATTN)
    lse3 = lse3.reshape(N_PATTERNS, B, S, D_ATTN)
    inp = pl.BlockSpec((N_PATTERNS, None, S, LANES), lambda b, c: (0, b, 0, c))
    out = pl.BlockSpec((S, LANES), lambda b, c: (b, c))
    return pl.pallas_call(
        body, grid=(B, D_ATTN // LANES), in_specs=[inp, inp], out_specs=[out, out],
        out_shape=[jax.ShapeDtypeStruct((B * S, D_ATTN), F32)] * 2,
        scratch_shapes=[pltpu.VMEM((N_PATTERNS, S, LANES), F32)] * 2 + [pltpu.VMEM((S, LANES), F32)],
        compiler_params=_params("arbitrary", "arbitrary"), name=name)(o3, lse3)


STAT_D = 8


def _attn_bwd_prep(dattn, attn, lse, B, S, *, name):
    emat = _seg_matrix(LANES)
    ncc = D_ATTN // LANES

    def body(da_ref, a_ref, l_ref, e_ref, do_ref, st_ref, scr_ref, nat_ref, tmp_ref):
        cc = pl.program_id(1)
        da = da_ref[...]
        dsum = _seg_mean(da * a_ref[...], e_ref, 1.0)
        scr_ref[...] = da

        def put_do(p, row0, v):
            do_ref[p, row0:row0 + v.shape[0], :] = v.astype(BF16)

        _permute(scr_ref, tmp_ref, put_do)

        lane = lax.broadcasted_iota(jnp.int32, (S, LANES), 1)
        h0 = 2 * cc
        vals = ((h0, l_ref[:, 0:1]), (h0 + 1, l_ref[:, HEAD_DIM:HEAD_DIM + 1]),
                (STAT_D + h0, dsum[:, 0:1]), (STAT_D + h0 + 1, dsum[:, HEAD_DIM:HEAD_DIM + 1]))
        tile = jnp.where(cc == 0, 0.0, nat_ref[...])
        for at, col in vals:
            tile = jnp.where(lane == at, col, tile)
        nat_ref[...] = tile

        @pl.when(cc == ncc - 1)
        def _():
            def put_st(p, row0, v):
                st_ref[p, row0:row0 + v.shape[0], :] = v

            _permute(nat_ref, tmp_ref, put_st)

    inp = pl.BlockSpec((S, LANES), lambda b, c: (b, c))
    return pl.pallas_call(
        body, grid=(B, ncc),
        in_specs=[inp, inp, inp, pl.BlockSpec((LANES, LANES), lambda b, c: (0, 0))],
        out_specs=[pl.BlockSpec((N_PATTERNS, None, S, LANES), lambda b, c: (0, b, 0, c)),
                   pl.BlockSpec((N_PATTERNS, None, S, LANES), lambda b, c: (0, b, 0, 0))],
        out_shape=[jax.ShapeDtypeStruct((N_PATTERNS, B, S, D_ATTN), BF16),
                   jax.ShapeDtypeStruct((N_PATTERNS, B, S, LANES), F32)],
        scratch_shapes=[pltpu.VMEM((S, LANES), F32)] * 3,
        compiler_params=_params("arbitrary", "arbitrary"), name=name)(dattn, attn, lse, emat)


def _attn_bwd(qkv, do3, st3, *, name, phase=None):
    QB = QB_BWD
    nb = qkv.shape[2]
    ngroups = nb // QB

    def body(q_ref, kp_ref, kc_ref, vp_ref, vc_ref, do_ref, st_ref, out_ref, carry_ref):
        p = pl.program_id(0)
        i = pl.program_id(1)

        @pl.when((p == 0) & (i == 0))
        def _():
            carry_ref[...] = jnp.zeros_like(carry_ref)

        kall = jnp.concatenate([kp_ref[...]] + [kc_ref[t] for t in range(QB)], axis=0)
        vall = jnp.concatenate([vp_ref[...]] + [vc_ref[t] for t in range(QB)], axis=0)

        masks = []
        for t in range(QB):
            mask = _band_mask(p, QB * i + t) & (i < ngroups)
            masks.append(jnp.concatenate([mask, mask], axis=0))

        def operands(t, hp):
            cols = slice(hp * LANES, (hp + 1) * LANES)
            kh, vh = kall[t * QBLK:(t + 2) * QBLK, cols], vall[t * QBLK:(t + 2) * QBLK, cols]
            return kh, vh, _split_heads(q_ref[t, :, cols]), _split_heads(do_ref[t, :, cols])

        def stage_scores(t, hp):
            kh, vh, q2, do2 = operands(t, hp)
            return _dot_nt(q2, kh), _dot_nt(do2, vh)

        def stage_softmax(t, hp, s, dp):
            h0, h1 = 2 * hp, 2 * hp + 1
            lse = jnp.concatenate([st_ref[t, :, h0:h0 + 1], st_ref[t, :, h1:h1 + 1]], axis=0)
            dsum = jnp.concatenate([st_ref[t, :, STAT_D + h0:STAT_D + h0 + 1],
                                    st_ref[t, :, STAT_D + h1:STAT_D + h1 + 1]], axis=0)
            pr = jnp.where(masks[t], jnp.exp(s - lse), 0.0)
            return (pr * (dp - dsum)).astype(BF16), pr.astype(BF16)

        def stage_grads(t, hp, ds, prb):
            cols = slice(hp * LANES, (hp + 1) * LANES)
            kh, vh, q2, do2 = operands(t, hp)
            dq = _dot(jnp.concatenate([ds[:QBLK], ds[QBLK:]], axis=1), _split_heads(kh))
            dk, dv = _dot_tn(ds, q2), _dot_tn(prb, do2)
            if t == 0:
                for c in range(3):
                    for tt in range(QB):
                        v = carry_ref[c, tt, :, cols]
                        if tt == QB - 1 and c > 0:
                            v = v + (dk if c == 1 else dv)[:QBLK]
                        out_ref[c, tt, :, cols] = v.astype(BF16)
            else:
                carry_ref[1, t - 1, :, cols] += dk[:QBLK]
                carry_ref[2, t - 1, :, cols] += dv[:QBLK]
            carry_ref[0, t, :, cols] = dq
            carry_ref[1, t, :, cols] = dk[QBLK:]
            carry_ref[2, t, :, cols] = dv[QBLK:]

        units = [(t, hp) for hp in range(HEADS // 2) for t in range(QB)]
        scores, probs = {}, {}
        for step in range(len(units) + 2):
            if step < len(units):
                scores[step] = stage_scores(*units[step])
            if 1 <= step <= len(units):
                probs[step - 1] = stage_softmax(*units[step - 1], *scores.pop(step - 1))
            if 2 <= step:
                stage_grads(*units[step - 2], *probs.pop(step - 2))

    group = lambda i: jnp.minimum(i, ngroups - 1)
    cur = lambda which: pl.BlockSpec((None, None, QB, QBLK, D_ATTN), lambda p, i: (which, p, group(i), 0, 0))
    prev = lambda which: pl.BlockSpec((None, None, None, QBLK, D_ATTN),
                                      lambda p, i: (which, p, jnp.maximum(QB * group(i) - 1, 0), 0, 0))
    aux = lambda lanes: pl.BlockSpec((None, QB, QBLK, lanes), lambda p, i: (p, group(i), 0, 0))
    return _call(
        body, grid=(N_PATTERNS, ngroups + 1),
        in_specs=[cur(0), prev(1), cur(1), prev(2), cur(2), aux(D_ATTN), aux(LANES)],
        out_specs=[pl.BlockSpec((3, None, QB, QBLK, D_ATTN), lambda p, i: (0, p, jnp.maximum(i - 1, 0), 0, 0))],
        out_shape=[jax.ShapeDtypeStruct((3, N_PATTERNS, nb, QBLK, D_ATTN), BF16)],
        scratch_shapes=[pltpu.VMEM((3, QB, QBLK, D_ATTN), F32)],
        sem=("arbitrary", "arbitrary"), name=name, args=(qkv, qkv, qkv, qkv, qkv, do3, st3), phase=phase)


def _attn_grad_combine(cur, u, gains, B, S, *, name, phase=None):
    emat = _seg_matrix(LANES)

    def body(cur_ref, u_ref, g_ref, e_ref, du_ref, dg_ref, scr_ref, tmp_ref):
        c = pl.program_id(0)
        b = pl.program_id(2)
        for p in range(N_PATTERNS):
            _unpermute_in(lambda r0, n, p=p: cur_ref[p, pl.ds(r0, n), :].astype(F32), scr_ref.at[p], tmp_ref, p, S)
        dy = scr_ref[0] + scr_ref[1] + scr_ref[2]
        xv = u_ref[...]
        gain = g_ref[...]
        ms = _seg_mean(xv * xv, e_ref, HEAD_DIM)
        r = lax.rsqrt(ms + EPS)
        xhat = xv * r
        dxh = dy * gain
        dx = r * (dxh - xhat * _seg_mean(dxh * xhat, e_ref, HEAD_DIM))
        du_ref[...] = jnp.where(c < 2, dx, dy).astype(BF16)

        @pl.when((b == 0))
        def _():
            dg_ref[...] = jnp.zeros_like(dg_ref)

        dg_ref[...] += jnp.sum(dy * xhat, axis=0, keepdims=True)

    cur = cur.reshape(3, N_PATTERNS, B, S, D_ATTN)
    ncc = D_ATTN // LANES
    return _call(
        body, grid=(3, ncc, B),
        in_specs=[pl.BlockSpec((None, N_PATTERNS, None, S, LANES), lambda c, cc, b: (c, 0, b, 0, cc)),
                  pl.BlockSpec((S, LANES), lambda c, cc, b: (b, c * ncc + cc)),
                  pl.BlockSpec((None, 1, LANES), lambda c, cc, b: (c, 0, cc)),
                  pl.BlockSpec((LANES, LANES), lambda c, cc, b: (0, 0))],
        out_specs=[pl.BlockSpec((S, LANES), lambda c, cc, b: (b, c * ncc + cc)),
                   pl.BlockSpec((None, 1, LANES), lambda c, cc, b: (c, 0, cc))],
        out_shape=[jax.ShapeDtypeStruct((B * S, 3 * D_ATTN), BF16), jax.ShapeDtypeStruct((3, 1, D_ATTN), F32)],
        scratch_shapes=[pltpu.VMEM((N_PATTERNS, S, LANES), F32), pltpu.VMEM((S, LANES), F32)],
        sem=("arbitrary", "arbitrary", "arbitrary"), name=name, args=(cur, u, gains, emat), phase=phase)


HALO = 32
SUB = 64
SUBLANES = 8


def _shifted_copies(src_ref, sh_ref, tc):
    sh_ref[0] = src_ref[...]
    for r in range(1, SUBLANES):
        sh_ref[r, 0:tc + HALO - SUBLANES, :] = src_ref[pl.ds(r, tc + HALO - SUBLANES), :]


def _shifted(sh_ref, start, size):
    return sh_ref[start % SUBLANES, pl.ds(start - start % SUBLANES, size), :]


def _conv_fwd(u, cw, cb, lg, lb, B, S, *, tc, name):
    nchunk = S // tc
    hb = tc // HALO

    def body(ca_ref, cap_ref, cg_ref, cgp_ref, w_ref, cb_ref, lg_ref, lb_ref, cv_ref, glu_ref, y_ref, pad_ref, sh_ref):
        i = pl.program_id(1)
        glu = ca_ref[...] * _sigmoid(cg_ref[...])
        glu_ref[...] = glu
        prev = cap_ref[...] * _sigmoid(cgp_ref[...])
        pad_ref[0:HALO, :] = jnp.where(i > 0, prev, 0.0)
        pad_ref[HALO:, :] = glu
        _shifted_copies(pad_ref, sh_ref, tc)
        for sub in range(tc // SUB):
            acc = jnp.zeros((SUB, D_CONV), F32) + cb_ref[...]
            for k in range(CONV_K):
                acc = acc + _shifted(sh_ref, sub * SUB + HALO - (CONV_K - 1) + k, SUB) * w_ref[pl.ds(k, 1), :]
            y_ref[sub * SUB:(sub + 1) * SUB, :] = acc
        y = y_ref[...]
        mu = jnp.mean(y, axis=-1, keepdims=True)
        yc = y - mu
        var = jnp.mean(yc * yc, axis=-1, keepdims=True)
        z = yc * lax.rsqrt(var + EPS) * lg_ref[...] + lb_ref[...]
        cv_ref[...] = (z * _sigmoid(z)).astype(BF16)

    def cur(col):
        return pl.BlockSpec((tc, D_CONV), lambda b, i: (b * nchunk + i, col))

    def halo(col):
        return pl.BlockSpec((HALO, D_CONV), lambda b, i: (jnp.maximum((b * nchunk + i) * hb - 1, 0), col))

    vec = pl.BlockSpec((1, D_CONV), lambda b, i: (0, 0))
    out = pl.BlockSpec((tc, D_CONV), lambda b, i: (b * nchunk + i, 0))
    return pl.pallas_call(
        body, grid=(B, nchunk),
        in_specs=[cur(3), halo(3), cur(4), halo(4), pl.BlockSpec((CONV_K, D_CONV), lambda b, i: (0, 0)), vec, vec, vec],
        out_specs=[out, out, out],
        out_shape=[jax.ShapeDtypeStruct((B * S, D_CONV), BF16), jax.ShapeDtypeStruct((B * S, D_CONV), F32),
                   jax.ShapeDtypeStruct((B * S, D_CONV), F32)],
        scratch_shapes=[pltpu.VMEM((tc + HALO, D_CONV), F32), pltpu.VMEM((SUBLANES, tc + HALO, D_CONV), F32)],
        compiler_params=_params("arbitrary", "arbitrary"), name=name)(u, u, u, u, cw, cb, lg, lb)


def _conv_bwd_norm(dcv, y, lg, lb, *, tc, name):
    T = y.shape[0]

    def body(dcv_ref, y_ref, lg_ref, lb_ref, dy_ref, part_ref):
        yv = y_ref[...]
        mu = jnp.mean(yv, axis=-1, keepdims=True)
        yc = yv - mu
        var = jnp.mean(yc * yc, axis=-1, keepdims=True)
        rstd = lax.rsqrt(var + EPS)
        xhat = yc * rstd
        z = xhat * lg_ref[...] + lb_ref[...]
        sig = _sigmoid(z)
        dz = dcv_ref[...] * (sig * (1.0 + z * (1.0 - sig)))
        dxh = dz * lg_ref[...]
        dy = rstd * (dxh - jnp.mean(dxh, axis=-1, keepdims=True)
                     - xhat * jnp.mean(dxh * xhat, axis=-1, keepdims=True))
        dy_ref[...] = dy

        @pl.when(pl.program_id(0) == 0)
        def _():
            part_ref[...] = jnp.zeros_like(part_ref)

        part_ref[0:1, :] += jnp.sum(dz * xhat, axis=0, keepdims=True)
        part_ref[1:2, :] += jnp.sum(dz, axis=0, keepdims=True)
        part_ref[2:3, :] += jnp.sum(dy, axis=0, keepdims=True)

    tok = pl.BlockSpec((tc, D_CONV), lambda i: (i, 0))
    vec = pl.BlockSpec((1, D_CONV), lambda i: (0, 0))
    return pl.pallas_call(
        body, grid=(T // tc,), in_specs=[tok, tok, vec, vec],
        out_specs=[tok, pl.BlockSpec((8, D_CONV), lambda i: (0, 0))],
        out_shape=[jax.ShapeDtypeStruct((T, D_CONV), F32), jax.ShapeDtypeStruct((8, D_CONV), F32)],
        compiler_params=_params("arbitrary"), name=name)(dcv, y, lg, lb)


def _conv_bwd_taps(dy, glu, u, cw, B, S, *, tc, name, phase=None):
    nchunk = S // tc
    hb = tc // HALO
    last_hb = B * S // HALO - 1

    def body(dy_ref, dyn_ref, glu_ref, glup_ref, ca_ref, cg_ref, w_ref, dca_ref, dcg_ref, dw_ref,
             dyp_ref, glp_ref, acc_ref, shd_ref, shg_ref):
        b = pl.program_id(0)
        i = pl.program_id(1)
        dy = dy_ref[...]
        dyp_ref[0:tc, :] = dy
        dyp_ref[tc:, :] = jnp.where(i < nchunk - 1, dyn_ref[...], 0.0)
        glp_ref[0:HALO, :] = jnp.where(i > 0, glup_ref[...], 0.0)
        glp_ref[HALO:, :] = glu_ref[...]
        _shifted_copies(dyp_ref, shd_ref, tc)
        _shifted_copies(glp_ref, shg_ref, tc)

        @pl.when((b == 0) & (i == 0))
        def _():
            dw_ref[...] = jnp.zeros_like(dw_ref)

        for sub in range(tc // SUB):
            acc = jnp.zeros((SUB, D_CONV), F32)
            for k in range(CONV_K):
                acc = acc + _shifted(shd_ref, sub * SUB + (CONV_K - 1) - k, SUB) * w_ref[pl.ds(k, 1), :]
            acc_ref[sub * SUB:(sub + 1) * SUB, :] = acc
        for k in range(CONV_K):
            dw_ref[k:k + 1, :] += jnp.sum(dy * _shifted(shg_ref, HALO - (CONV_K - 1) + k, tc), axis=0, keepdims=True)
        dglu = acc_ref[...]
        ca = ca_ref[...]
        sig = _sigmoid(cg_ref[...])
        dca_ref[...] = (dglu * sig).astype(BF16)
        dcg_ref[...] = (dglu * ca * sig * (1.0 - sig)).astype(BF16)

    tok = pl.BlockSpec((tc, D_CONV), lambda b, i: (b * nchunk + i, 0))
    nxt = pl.BlockSpec((HALO, D_CONV), lambda b, i: (jnp.minimum((b * nchunk + i + 1) * hb, last_hb), 0))
    prv = pl.BlockSpec((HALO, D_CONV), lambda b, i: (jnp.maximum((b * nchunk + i) * hb - 1, 0), 0))
    return _call(
        body, grid=(B, nchunk),
        in_specs=[tok, nxt, tok, prv,
                  pl.BlockSpec((tc, D_CONV), lambda b, i: (b * nchunk + i, 3)),
                  pl.BlockSpec((tc, D_CONV), lambda b, i: (b * nchunk + i, 4)),
                  pl.BlockSpec((CONV_K, D_CONV), lambda b, i: (0, 0))],
        out_specs=[tok, tok, pl.BlockSpec((32, D_CONV), lambda b, i: (0, 0))],
        out_shape=[jax.ShapeDtypeStruct((B * S, D_CONV), BF16), jax.ShapeDtypeStruct((B * S, D_CONV), BF16),
                   jax.ShapeDtypeStruct((32, D_CONV), F32)],
        scratch_shapes=[pltpu.VMEM((tc + HALO, D_CONV), F32), pltpu.VMEM((tc + HALO, D_CONV), F32),
                        pltpu.VMEM((tc, D_CONV), F32), pltpu.VMEM((SUBLANES, tc + HALO, D_CONV), F32),
                        pltpu.VMEM((SUBLANES, tc + HALO, D_CONV), F32)],
        sem=("arbitrary", "arbitrary"), name=name, args=(dy, dy, glu, glu, u, u, cw), phase=phase)


def _outproj_fwd(h, attn, cv, wout, *, tm, name):
    T, D = h.shape

    def body(h_ref, a_ref, c_ref, w_ref, o_ref):
        o_ref[...] = (h_ref[...] + _dot(a_ref[...].astype(BF16), w_ref[0:D_ATTN, :])
                      + _dot(c_ref[...], w_ref[D_ATTN:, :]))

    tok = pl.BlockSpec((tm, D), lambda i: (i, 0))
    half = pl.BlockSpec((tm, D_ATTN), lambda i: (i, 0))
    return pl.pallas_call(
        body, grid=(T // tm,), in_specs=[tok, half, half, pl.BlockSpec(wout.shape, lambda i: (0, 0))],
        out_specs=tok, out_shape=jax.ShapeDtypeStruct((T, D), F32),
        compiler_params=_params("arbitrary"), name=name)(h, attn, cv, wout)


def _outproj_bwd(dh, attn, cv, wout, *, tm, name):
    T, D = dh.shape

    def body(dh_ref, a_ref, c_ref, w_ref, da_ref, dc_ref, dw_ref):
        @pl.when(pl.program_id(0) == 0)
        def _():
            dw_ref[...] = jnp.zeros_like(dw_ref)

        dhb = dh_ref[...].astype(BF16)
        da_ref[...] = _dot_nt(dhb, w_ref[0:D_ATTN, :])
        dc_ref[...] = _dot_nt(dhb, w_ref[D_ATTN:, :])
        dw_ref[0:D_ATTN, :] += _dot_tn(a_ref[...].astype(BF16), dhb)
        dw_ref[D_ATTN:, :] += _dot_tn(c_ref[...], dhb)

    tok = pl.BlockSpec((tm, D), lambda i: (i, 0))
    half = pl.BlockSpec((tm, D_ATTN), lambda i: (i, 0))
    wspec = pl.BlockSpec(wout.shape, lambda i: (0, 0))
    return pl.pallas_call(
        body, grid=(T // tm,), in_specs=[tok, half, half, wspec], out_specs=[half, half, wspec],
        out_shape=[jax.ShapeDtypeStruct((T, D_ATTN), F32), jax.ShapeDtypeStruct((T, D_ATTN), F32),
                   jax.ShapeDtypeStruct(wout.shape, F32)],
        compiler_params=_params("arbitrary"), name=name)(dh, attn, cv, wout)


ADAM_BLOCK_BYTES = 3 * 512 * 1024


def _adamw(w, g, m, v, *, name):
    R, C = w.shape
    tr = R
    for cand in (512, 352, 256, 176, 128, 64, 32, 16, 8):
        if R % cand == 0 and cand * C * 4 <= ADAM_BLOCK_BYTES:
            tr = cand
            break
    c1 = 1.0 - ADAM_B1 ** ADAM_STEP
    c2 = 1.0 - ADAM_B2 ** ADAM_STEP

    def body(w_ref, g_ref, m_ref, v_ref, d_ref, nm_ref, nv_ref):
        gv = g_ref[...]
        nm = ADAM_B1 * m_ref[...] + (1.0 - ADAM_B1) * gv
        nv = ADAM_B2 * v_ref[...] + (1.0 - ADAM_B2) * (gv * gv)
        d_ref[...] = -ADAM_LR * ((nm / c1) / (jnp.sqrt(nv / c2) + ADAM_EPS) + ADAM_WD * w_ref[...])
        nm_ref[...] = nm
        nv_ref[...] = nv

    blk = pl.BlockSpec((tr, C), lambda i: (i, 0))
    return pl.pallas_call(
        body, grid=(R // tr,), in_specs=[blk] * 4, out_specs=[blk] * 3,
        out_shape=[jax.ShapeDtypeStruct((R, C), F32)] * 3,
        compiler_params=_params("arbitrary"), name=name)(w, g, m, v)


ADAM_SPLIT = 4


def _adamw_many(ws, gs, ms, vs, *, name, phase=None):
    n = len(ws)
    c1 = 1.0 - ADAM_B1 ** ADAM_STEP
    c2 = 1.0 - ADAM_B2 ** ADAM_STEP

    def body(*refs):
        ins, outs = refs[:4 * n], refs[4 * n:]
        for a in range(n):
            w_ref, g_ref, m_ref, v_ref = ins[4 * a:4 * a + 4]
            gv = g_ref[...]
            nm = ADAM_B1 * m_ref[...] + (1.0 - ADAM_B1) * gv
            nv = ADAM_B2 * v_ref[...] + (1.0 - ADAM_B2) * (gv * gv)
            outs[3 * a][...] = -ADAM_LR * ((nm / c1) / (jnp.sqrt(nv / c2) + ADAM_EPS) + ADAM_WD * w_ref[...])
            outs[3 * a + 1][...] = nm
            outs[3 * a + 2][...] = nv

    in_specs, out_specs, out_shape, args = [], [], [], []
    for w, g, m, v in zip(ws, gs, ms, vs):
        R, C = w.shape
        blk = pl.BlockSpec((R // ADAM_SPLIT, C), lambda i: (i, 0))
        in_specs += [blk] * 4
        out_specs += [blk] * 3
        out_shape += [jax.ShapeDtypeStruct((R, C), F32)] * 3
        args += [w, g, m, v]
    res = _call(body, grid=(ADAM_SPLIT,), in_specs=in_specs, out_specs=out_specs, out_shape=out_shape,
                sem=("arbitrary",), name=name, args=args, phase=phase)
    outs, extra = res if phase is not None else (res, None)
    return list(outs[0::3]), list(outs[1::3]), list(outs[2::3]), extra


TM = 512
TM_WIDE = 1024
TK = 1024
TC = 256


def _local_step(x, tgt, w, overlap=None):
    B, S, D = x.shape
    T = B * S
    x2 = x.reshape(T, D)
    t2 = tgt.reshape(T, D)
    ones = jnp.ones((1, D_ATTN), F32)
    scale = HEAD_DIM ** -0.5
    gains = jnp.stack([jnp.tile(w["q_norm"], (1, HEADS)) * scale, jnp.tile(w["k_norm"], (1, HEADS)), ones])
    g = {}

    def hosting(point, build):
        phase = overlap.phase(point, w, g) if overlap is not None else None
        if phase is None:
            return build(None)
        outs, extra = build(phase)
        overlap.done(point, extra, w, g)
        return outs

    h1, n1, G1, U1 = hosting("ffn1_fwd", lambda ph: _ffn_fwd(
        x2, w["ffn1_norm"], w["wg1"], w["wu1"], w["wd1"], None, tm=TM_WIDE, name="ffn1_fwd", phase=ph))
    u, n2 = hosting("inproj_fwd", lambda ph: _inproj_fwd(h1, w["mix_norm"], w["win"], tm=TM_WIDE, name="inproj_fwd", phase=ph))
    qkv = _qkv_prep(u, gains, B, S, name="qkv_prep")
    qkv = qkv.reshape(3, N_PATTERNS, T // QBLK, QBLK, D_ATTN)
    o3, lse3 = _attn_fwd(qkv, name="attn_fwd")
    attn, lse = _attn_combine(o3, lse3, B, S, name="attn_combine")
    cv, glu, yconv = _conv_fwd(u, w["conv_w"], w["conv_b"], w["conv_ln_g"], w["conv_ln_b"], B, S, tc=TC, name="conv_fwd")
    h2 = _outproj_fwd(h1, attn, cv, w["wout"], tm=TM, name="outproj_fwd")
    dh3, n3, G2, U2, loss = _ffn_fwd(h2, w["ffn2_norm"], w["wg2"], w["wu2"], w["wd2"], t2, tm=TM_WIDE, name="ffn2_fwd")

    dG, dU, A, dy, dh2, g["ffn2_norm"] = _ffn_bwd_act(dh3, h2, w["ffn2_norm"], G2, U2, w["wg2"], w["wu2"], w["wd2"],
                                                    tm=TM, name="ffn2_bwd_act")
    g["wg2"], g["wu2"], g["wd2"] = _ffn_bwd_w(n3, dy, dG, dU, A, tk=TK, name="ffn2_bwd_w")
    dattn, dcv, g["wout"] = _outproj_bwd(dh2, attn, cv, w["wout"], tm=TM, name="outproj_bwd")
    dyc, cpart = _conv_bwd_norm(dcv, yconv, w["conv_ln_g"], w["conv_ln_b"], tc=TC, name="conv_bwd_norm")
    dca, dcg, dcw = hosting("conv_bwd_taps", lambda ph: _conv_bwd_taps(
        dyc, glu, u, w["conv_w"], B, S, tc=TC, name="conv_bwd_taps", phase=ph))
    do3, st3 = _attn_bwd_prep(dattn, attn, lse, B, S, name="attn_bwd_prep")
    nb = T // QBLK
    (cur,) = hosting("attn_bwd", lambda ph: _attn_bwd(
        qkv, do3.reshape(N_PATTERNS, nb, QBLK, D_ATTN), st3.reshape(N_PATTERNS, nb, QBLK, LANES),
        name="attn_bwd", phase=ph))
    du_qkv, dgains = hosting("attn_grad_combine", lambda ph: _attn_grad_combine(
        cur, u, gains, B, S, name="attn_grad_combine", phase=ph))
    du = jnp.concatenate([du_qkv, dca, dcg], axis=1)
    (g["win"],) = hosting("inproj_bwd_w", lambda ph: _inproj_bwd_w(
        n2, du, w["win"].shape[0], tk=TK, name="inproj_bwd_w", phase=ph))
    dh1, g["mix_norm"] = hosting("inproj_bwd_act", lambda ph: _inproj_bwd_act(
        du, dh2, h1, w["mix_norm"], w["win"], tm=TM_WIDE, name="inproj_bwd_act", phase=ph))
    dG, dU, A, dy, dx, g["ffn1_norm"] = hosting("ffn1_bwd_act", lambda ph: _ffn_bwd_act(
        dh1, x2, w["ffn1_norm"], G1, U1, w["wg1"], w["wu1"], w["wd1"], tm=TM, name="ffn1_bwd_act", phase=ph))
    g["wg1"], g["wu1"], g["wd1"] = hosting("ffn1_bwd_w", lambda ph: _ffn_bwd_w(
        n1, dy, dG, dU, A, tk=TK, name="ffn1_bwd_w", phase=ph))

    g["q_norm"] = dgains[0].reshape(HEADS, HEAD_DIM).sum(axis=0, keepdims=True) * scale
    g["k_norm"] = dgains[1].reshape(HEADS, HEAD_DIM).sum(axis=0, keepdims=True)
    g["conv_ln_g"] = cpart[0:1]
    g["conv_ln_b"] = cpart[1:2]
    g["conv_b"] = cpart[2:3]
    g["conv_w"] = dcw[:CONV_K]
    return loss, dx.reshape(B, S, D), g


N_CHIPS = 4
N_DEV = 8
VMEM_SPEC = pl.BlockSpec(memory_space=pltpu.VMEM)


def _remote(src, dst, send_sem, recv_sem, device):
    return pltpu.make_async_remote_copy(src_ref=src, dst_ref=dst, send_sem=send_sem, recv_sem=recv_sem,
                                        device_id=device, device_id_type=MESH)


def _stage_shards(shards, dtypes, *, name):
    n = len(shards)
    halves = [s.reshape(2, s.shape[0] // 2, s.shape[1]) for s in shards]

    def body(*refs):
        ins, outs, vms, loc_sems = refs[:n], refs[n:2 * n], refs[2 * n:3 * n], refs[3 * n]
        me = 2 * lax.axis_index("x") + lax.axis_index("y")
        copies = []
        for a in range(n):
            vms[a][...] = ins[a][...].astype(dtypes[a])
            cp = pltpu.make_async_copy(vms[a], outs[a].at[me], loc_sems.at[a])
            cp.start()
            copies.append(cp)
        for cp in copies:
            cp.wait()

    return pl.pallas_call(
        body, in_specs=[VMEM_SPEC] * n, out_specs=[ANY] * n,
        out_shape=[jax.ShapeDtypeStruct((N_CHIPS,) + h.shape, dt) for h, dt in zip(halves, dtypes)],
        scratch_shapes=[pltpu.VMEM(h.shape, dt) for h, dt in zip(halves, dtypes)] + [DMA_SEMS((n,))],
        compiler_params=pltpu.CompilerParams(vmem_limit_bytes=VMEM_LIMIT), name=name)(*halves)


def _like(arrays):
    return [jax.ShapeDtypeStruct(a.shape, a.dtype) for a in arrays]


def _axes():
    x, y, c = lax.axis_index("x"), lax.axis_index("y"), lax.axis_index("c")
    first = (x + (1 - c) * (1 - 2 * x), y + c * (1 - 2 * y))
    second = (x + c * (1 - 2 * x), y + (1 - c) * (1 - 2 * y))
    slots = tuple(2 * px + py for px, py in ((x, y), first, second, (1 - x, 1 - y)))
    return (x, y, c), (*first, c), (*second, c), slots


def _gather_ici_phase(bufs):
    n = len(bufs)

    def stage1(ins, outs, sems):
        (x, y, c), peer1, peer2, (own, s1, s2, both) = _axes()
        starts, arrivals = [], []
        for a in range(n):
            mine, land = outs[a].at[own, c], outs[a].at[s2, c]
            starts.append(_remote(mine, mine, *sems(a), peer2))
            arrivals.append(_remote(land, land, *sems(a), peer2))
        return starts, arrivals

    def stage2(ins, outs, sems):
        (x, y, c), peer1, peer2, (own, s1, s2, both) = _axes()
        starts, arrivals = [], []
        for a in range(n):
            for k, (src, dst) in enumerate(((own, s1), (s2, both))):
                mine, land = outs[a].at[src, c], outs[a].at[dst, c]
                starts.append(_remote(mine, mine, *sems(2 * a + k), peer1))
                arrivals.append(_remote(land, land, *sems(2 * a + k), peer1))
        return starts, arrivals

    same = {a: a for a in range(n)}
    return _Phase(bufs, _like(bufs), same, n, stage1).then(_Phase(bufs, _like(bufs), same, 2 * n, stage2))


def _gather_d2d_phase(bufs):
    n = len(bufs)

    def copies(ins, outs, sems):
        (x, y, c), peer1, peer2, (own, s1, s2, both) = _axes()
        starts, arrivals = [], []
        for a in range(n):
            for j, s in enumerate((s1, s2, both)):
                got, land = outs[a].at[s, c], outs[a].at[s, 1 - c]
                starts.append(_remote(got, got, *sems(3 * a + j), (x, y, 1 - c)))
                arrivals.append(_remote(land, land, *sems(3 * a + j), (x, y, 1 - c)))
        return starts, arrivals

    return _Phase(bufs, _like(bufs), {a: a for a in range(n)}, 3 * n, copies)


def _exchange_phase(views):
    n = len(views)

    def copies(ins, outs, sems):
        x, y, c = lax.axis_index("x"), lax.axis_index("y"), lax.axis_index("c")
        starts = [_remote(ins[a].at[pl.ds(0, ins[a].shape[0]), 1 - c], outs[a], *sems(a), (x, y, 1 - c))
                  for a in range(n)]
        return starts, starts

    outs = [jax.ShapeDtypeStruct((v.shape[0],) + v.shape[2:], F32) for v in views]
    return _Phase(views, outs, {}, n, copies)


ADD_SPLIT = 2


def _add_halves(views, got, sel, tag):
    n = len(views)

    def body(s_ref, *refs):
        ins, outs = refs[:4 * n], refs[4 * n:]
        for a in range(n):
            gk, rk, gs, rs = ins[4 * a:4 * a + 4]
            outs[2 * a][...] = gk[...] + rk[...]
            outs[2 * a + 1][...] = (gs[...] + rs[...]).astype(BF16)

    in_specs, out_specs, out_shape, args = [], [], [], []
    for g, r in zip(views, got):
        _, _, rh, cdim = g.shape
        tr = rh // ADD_SPLIT
        for off in (0, 2):
            in_specs.append(pl.BlockSpec((None, None, tr, cdim), lambda k, i, s, off=off: (s[1 + off + k], s[0], i, 0)))
            in_specs.append(pl.BlockSpec((None, tr, cdim), lambda k, i, s, off=off: (s[1 + off + k], i, 0)))
            args += [g, r]
        out_specs += [pl.BlockSpec((None, tr, cdim), lambda k, i, s: (k, i, 0))] * 2
        out_shape += [jax.ShapeDtypeStruct((2, rh, cdim), F32), jax.ShapeDtypeStruct((2, rh, cdim), BF16)]
    res = pl.pallas_call(
        body,
        grid_spec=pltpu.PrefetchScalarGridSpec(num_scalar_prefetch=1, grid=(2, ADD_SPLIT), in_specs=in_specs,
                                               out_specs=out_specs),
        out_shape=out_shape, compiler_params=_params("arbitrary", "arbitrary"), name=f"rs_add_half_{tag}")(sel, *args)
    return list(res[0::2]), list(res[1::2])


def _swap_phase(arrays, stage):
    n = len(arrays)

    def copies(ins, outs, sems):
        peer = _axes()[stage]
        starts = [_remote(ins[a], outs[a], *sems(a), peer) for a in range(n)]
        return starts, starts

    return _Phase(arrays, _like(arrays), {}, n, copies)


def _add_first(keep, got, tag):
    n = len(keep)

    def body(*refs):
        ins, outs = refs[:2 * n], refs[2 * n:]
        for a in range(n):
            k_ref, g_ref = ins[2 * a], ins[2 * a + 1]
            outs[2 * a][...] = k_ref[0] + g_ref[0].astype(F32)
            outs[2 * a + 1][...] = (k_ref[1] + g_ref[1].astype(F32)).astype(BF16)

    in_specs, out_specs, out_shape, args = [], [], [], []
    for k, g in zip(keep, got):
        _, rh, cdim = k.shape
        tr = rh // ADD_SPLIT
        in_specs += [pl.BlockSpec((2, tr, cdim), lambda i: (0, i, 0))] * 2
        out_specs += [pl.BlockSpec((tr, cdim), lambda i: (i, 0))] * 2
        out_shape += [jax.ShapeDtypeStruct((rh, cdim), F32), jax.ShapeDtypeStruct((rh, cdim), BF16)]
        args += [k, g]
    res = pl.pallas_call(body, grid=(ADD_SPLIT,), in_specs=in_specs, out_specs=out_specs, out_shape=out_shape,
                         compiler_params=_params("arbitrary"), name=f"rs_add_first_{tag}")(*args)
    return list(res[0::2]), list(res[1::2])


def _add_second(keep, got, sel, tag):
    n = len(keep)

    def body(s_ref, *refs):
        ins, outs = refs[:2 * n], refs[2 * n:]
        for a in range(n):
            outs[a][...] = ins[2 * a][...] + ins[2 * a + 1][...].astype(F32)

    in_specs, out_specs, out_shape, args = [], [], [], []
    for k, g in zip(keep, got):
        rh, cdim = k.shape
        tr = rh // ADD_SPLIT
        in_specs += [pl.BlockSpec((tr, cdim), lambda i, s: (i, 0))] * 2
        out_specs.append(pl.BlockSpec((None, tr, cdim), lambda i, s: (s[0], i, 0)))
        out_shape.append(jax.ShapeDtypeStruct((2, rh, cdim), F32))
        args += [k, g]
    res = pl.pallas_call(
        body,
        grid_spec=pltpu.PrefetchScalarGridSpec(num_scalar_prefetch=1, grid=(ADD_SPLIT,), in_specs=in_specs,
                                               out_specs=out_specs),
        out_shape=out_shape, compiler_params=_params("arbitrary"), name=f"rs_add_second_{tag}")(sel, *args)
    return list(res)


def _join_phase(halves):
    n = len(halves)

    def copies(ins, outs, sems):
        x, y, c = lax.axis_index("x"), lax.axis_index("y"), lax.axis_index("c")
        starts, arrivals = [], []
        for a in range(n):
            mine, land = outs[a].at[c], outs[a].at[1 - c]
            starts.append(_remote(mine, mine, *sems(a), (x, y, 1 - c)))
            arrivals.append(_remote(land, land, *sems(a), (x, y, 1 - c)))
        return starts, arrivals

    return _Phase(halves, _like(halves), {a: a for a in range(n)}, n, copies)


def _slot_order():
    x, y, c = lax.axis_index("x"), lax.axis_index("y"), lax.axis_index("c")
    own, flip_x, flip_y, both = 2 * x + y, 2 * (1 - x) + y, 2 * x + 1 - y, 2 * (1 - x) + 1 - y
    first = jnp.where(c == 0, flip_x, flip_y)
    second = jnp.where(c == 0, flip_y, flip_x)
    return jnp.stack([c, own, second, first, both]).astype(jnp.int32)


def _reduce_scatter(views, got, join_also=()):
    sel = _slot_order()
    keep, send = _add_halves(views, got, sel, "late")
    got = _run_phase(_swap_phase(send, 1), name="rs_swap_first_axis")
    keep, send = _add_first(keep, got, "late")
    got = _run_phase(_swap_phase(send, 2), name="rs_swap_second_axis")
    halves = _add_second(keep, got, sel, "late")
    full = _run_phase(_join_phase(halves + list(join_also)), name="rs_join_halves")
    return [f.reshape(-1, f.shape[-1]) for f in full]


def _half_view(g):
    return g.reshape(N_CHIPS, 2, g.shape[1] // 2, g.shape[2])


EARLY_GRADS = ("wg2", "wu2", "wd2", "wout")
MIDDLE_GRADS = ("win",)


class _Overlap:
    EARLY_AT = ("conv_bwd_taps", "attn_bwd", "attn_grad_combine", "inproj_bwd_w")
    MIDDLE_AT = ("inproj_bwd_act", "ffn1_bwd_act", "ffn1_bwd_w", None)

    def __init__(self, staged):
        self.staged = staged
        sel = _slot_order()
        self.early = _Reduction(EARLY_GRADS, "early", sel)
        self.middle = _Reduction(MIDDLE_GRADS, "middle", sel)

    def phase(self, point, w, g):
        if point == "ffn1_fwd":
            return _gather_ici_phase(self.staged)
        if point == "inproj_fwd":
            return _gather_d2d_phase(self.ffn2)
        for red, at in ((self.early, self.EARLY_AT), (self.middle, self.MIDDLE_AT)):
            if point in at:
                return red.phase(at.index(point), g)
        return None

    def done(self, point, outs, w, g):
        if point == "ffn1_fwd":
            win, wout, taps = [_whole(b) for b in _run_phase(_gather_d2d_phase(outs[:3]), name="gather_mix_d2d")]
            w["win"] = win
            w["wout"] = wout.reshape(-1, wout.shape[-1])
            w["conv_w"] = taps.transpose(1, 0, 2).reshape(CONV_K + 1, D_CONV)[:CONV_K]
            self.ffn2 = list(outs[3:])
        elif point == "inproj_fwd":
            w["wg2"], w["wu2"], w["wd2"] = [_whole(b) for b in outs]
        for red, at in ((self.early, self.EARLY_AT), (self.middle, self.MIDDLE_AT)):
            if point in at:
                red.done(at.index(point), outs)


class _Reduction:
    def __init__(self, names, tag, sel):
        self.names, self.tag, self.sel = names, tag, sel
        self.reduced = {}

    def phase(self, stage, g):
        if stage == 0:
            self.cols = [g[k].shape[-1] for k in self.names]
            self.views = [_half_view(g[k].reshape(N_CHIPS, -1, g[k].shape[-1])) for k in self.names]
            return _exchange_phase(self.views)
        if stage in (1, 2):
            return _swap_phase(self.send, stage)
        return _join_phase(self.halves)

    def done(self, stage, outs):
        if stage == 0:
            self.keep, self.send = _add_halves(self.views, outs, self.sel, self.tag)
        elif stage == 1:
            self.keep, self.send = _add_first(self.keep, outs, self.tag)
        elif stage == 2:
            self.halves = _add_second(self.keep, outs, self.sel, self.tag)
        else:
            for k, c, f in zip(self.names, self.cols, outs):
                self.reduced[k] = f.reshape(-1, c)


def _whole(buf):
    return buf.reshape(buf.shape[0], 2 * buf.shape[2], buf.shape[3])


def _allreduce_small(pack, *, name):
    rows = pack.shape[0]

    def body(p_ref, o_ref, buf_ref, send_sems, recv_sems):
        x, y, c = lax.axis_index("x"), lax.axis_index("y"), lax.axis_index("c")
        me = 4 * x + 2 * y + c
        buf_ref[me] = p_ref[...]
        cps = []
        for k in range(1, N_DEV):
            peer = tuple(1 - v if (k >> s) & 1 else v for v, s in ((x, 2), (y, 1), (c, 0)))
            cp = _remote(p_ref, buf_ref.at[me], send_sems.at[k - 1], recv_sems.at[k - 1], peer)
            cp.start()
            cps.append(cp)
        for k in range(1, N_DEV):
            src = 4 * (x ^ ((k >> 2) & 1)) + 2 * (y ^ ((k >> 1) & 1)) + (c ^ (k & 1))
            land = buf_ref.at[src]
            _remote(land, land, send_sems.at[k - 1], recv_sems.at[k - 1], (x, y, c)).wait_recv()
        acc = buf_ref[0]
        for d in range(1, N_DEV):
            acc = acc + buf_ref[d]
        o_ref[...] = acc
        for cp in cps:
            cp.wait_send()

    return pl.pallas_call(
        body, in_specs=[VMEM_SPEC], out_specs=VMEM_SPEC, out_shape=jax.ShapeDtypeStruct(pack.shape, F32),
        scratch_shapes=[pltpu.VMEM((N_DEV, rows, LANES), F32), pltpu.SemaphoreType.DMA((N_DEV - 1,)),
                        pltpu.SemaphoreType.DMA((N_DEV - 1,))], name=name)(pack)


SMALL = ("ffn1_norm", "mix_norm", "q_norm", "k_norm", "conv_b", "conv_ln_g", "conv_ln_b", "ffn2_norm", "conv_w")
BIG = ("ffn1_w_gate", "ffn1_w_up", "ffn1_w_down", "w_in", "w_out", "ffn2_w_gate", "ffn2_w_up", "ffn2_w_down")
TRANSPOSED = ("ffn1_w_gate", "ffn1_w_up", "ffn2_w_gate", "ffn2_w_up")
WEIGHTS = ("ffn1_norm", "ffn1_w_gate", "ffn1_w_up", "ffn1_w_down", "mix_norm", "w_in", "q_norm", "k_norm",
           "conv_w", "conv_b", "conv_ln_g", "conv_ln_b", "w_out", "ffn2_norm", "ffn2_w_gate", "ffn2_w_up",
           "ffn2_w_down")


def _pack(parts):
    rows = []
    for p in parts:
        flat = p.reshape(-1)
        tile = SUBLANES * LANES
        padded = -(-flat.shape[0] // tile) * tile
        rows.append(jnp.pad(flat, (0, padded - flat.shape[0])).reshape(-1, LANES))
    return jnp.concatenate(rows, axis=0)


def _unpack(pack, shapes):
    out, row = [], 0
    for shp in shapes:
        size = shp[0] * shp[1]
        tile = SUBLANES * LANES
        nrows = -(-size // tile) * SUBLANES
        out.append(pack[row:row + nrows].reshape(-1)[:size].reshape(shp))
        row += nrows
    return out


def kernel(x, ffn1_norm, ffn1_w_gate, ffn1_w_up, ffn1_w_down, mix_norm, w_in, q_norm, k_norm, conv_w, conv_b, conv_ln_g, conv_ln_b, w_out, ffn2_norm, ffn2_w_gate, ffn2_w_up, ffn2_w_down, loss_target, m_ffn1_norm, m_ffn1_w_gate, m_ffn1_w_up, m_ffn1_w_down, m_mix_norm, m_w_in, m_q_norm, m_k_norm, m_conv_w, m_conv_b, m_conv_ln_g, m_conv_ln_b, m_w_out, m_ffn2_norm, m_ffn2_w_gate, m_ffn2_w_up, m_ffn2_w_down, v_ffn1_norm, v_ffn1_w_gate, v_ffn1_w_up, v_ffn1_w_down, v_mix_norm, v_w_in, v_q_norm, v_k_norm, v_conv_w, v_conv_b, v_conv_ln_g, v_conv_ln_b, v_w_out, v_ffn2_norm, v_ffn2_w_gate, v_ffn2_w_up, v_ffn2_w_down):
    wts = dict(ffn1_norm=ffn1_norm, ffn1_w_gate=ffn1_w_gate[0], ffn1_w_up=ffn1_w_up[0], ffn1_w_down=ffn1_w_down[0],
               mix_norm=mix_norm, w_in=w_in[0], q_norm=q_norm, k_norm=k_norm, conv_w=conv_w[0], conv_b=conv_b,
               conv_ln_g=conv_ln_g, conv_ln_b=conv_ln_b, w_out=w_out[0], ffn2_norm=ffn2_norm,
               ffn2_w_gate=ffn2_w_gate[0], ffn2_w_up=ffn2_w_up[0], ffn2_w_down=ffn2_w_down[0])
    mom = dict(ffn1_norm=m_ffn1_norm, ffn1_w_gate=m_ffn1_w_gate[0], ffn1_w_up=m_ffn1_w_up[0], ffn1_w_down=m_ffn1_w_down[0],
               mix_norm=m_mix_norm, w_in=m_w_in[0], q_norm=m_q_norm, k_norm=m_k_norm, conv_w=m_conv_w[0], conv_b=m_conv_b,
               conv_ln_g=m_conv_ln_g, conv_ln_b=m_conv_ln_b, w_out=m_w_out[0], ffn2_norm=m_ffn2_norm,
               ffn2_w_gate=m_ffn2_w_gate[0], ffn2_w_up=m_ffn2_w_up[0], ffn2_w_down=m_ffn2_w_down[0])
    var = dict(ffn1_norm=v_ffn1_norm, ffn1_w_gate=v_ffn1_w_gate[0], ffn1_w_up=v_ffn1_w_up[0], ffn1_w_down=v_ffn1_w_down[0],
               mix_norm=v_mix_norm, w_in=v_w_in[0], q_norm=v_q_norm, k_norm=v_k_norm, conv_w=v_conv_w[0], conv_b=v_conv_b,
               conv_ln_g=v_conv_ln_g, conv_ln_b=v_conv_ln_b, w_out=v_w_out[0], ffn2_norm=v_ffn2_norm,
               ffn2_w_gate=v_ffn2_w_gate[0], ffn2_w_up=v_ffn2_w_up[0], ffn2_w_down=v_ffn2_w_down[0])
    chip = 2 * lax.axis_index("x") + lax.axis_index("y")
    for src in (wts, mom, var):
        for n in TRANSPOSED:
            src[n] = src[n].T

    taps = jnp.pad(wts["conv_w"], ((0, 1), (0, 0)))
    staged = _stage_shards([wts["ffn1_w_gate"], wts["ffn1_w_up"], wts["ffn1_w_down"], wts["w_in"], wts["w_out"], taps,
                            wts["ffn2_w_gate"], wts["ffn2_w_up"], wts["ffn2_w_down"]],
                           [BF16, BF16, BF16, BF16, BF16, F32, BF16, BF16, BF16], name="stage_shards")
    first = _run_phase(_gather_ici_phase(staged[:3]).then(_gather_d2d_phase(staged[:3])), name="gather_ffn1")
    wg1, wu1, wd1 = [_whole(b) for b in first]
    w = dict(ffn1_norm=ffn1_norm, mix_norm=mix_norm, ffn2_norm=ffn2_norm, q_norm=q_norm, k_norm=k_norm,
             conv_b=conv_b, conv_ln_g=conv_ln_g, conv_ln_b=conv_ln_b, wg1=wg1, wu1=wu1, wd1=wd1)
    overlap = _Overlap(staged[3:])
    loss_part, grad_x, g = _local_step(x, loss_target, w, overlap)

    grads, delta, new_m, new_v = {}, {}, {}, {}
    early = overlap.early.reduced
    grads.update(ffn2_w_gate=early["wg2"], ffn2_w_up=early["wu2"], ffn2_w_down=early["wd2"], w_out=early["wout"])
    first = ("ffn2_w_gate", "ffn2_w_up", "ffn2_w_down", "w_out")
    late_views = [_half_view(g[k]) for k in ("wg1", "wu1", "wd1")]
    d, m, v, got = _adamw_many([wts[n] for n in first], [grads[n] for n in first], [mom[n] for n in first],
                               [var[n] for n in first], name="adamw_early", phase=_exchange_phase(late_views))
    for dst, vals in ((delta, d), (new_m, m), (new_v, v)):
        dst.update(zip(first, vals))
    late = _reduce_scatter(late_views, got, join_also=overlap.middle.halves)
    grads.update(zip(("ffn1_w_gate", "ffn1_w_up", "ffn1_w_down", "w_in"), late))

    small_shapes = [g[n].shape for n in SMALL] + [(SUBLANES, LANES)]
    red = _allreduce_small(_pack([g[n] for n in SMALL] + [loss_part]), name="allreduce_small")
    small = dict(zip(SMALL + ("loss",), _unpack(red, small_shapes)))
    loss = small["loss"][0, 0]
    small["conv_w"] = lax.dynamic_slice_in_dim(small["conv_w"], chip * LANES, LANES, axis=1)

    for n in ("ffn1_w_gate", "ffn1_w_up", "ffn1_w_down", "w_in"):
        delta[n], new_m[n], new_v[n] = _adamw(wts[n], grads[n], mom[n], var[n], name=f"adamw_{n}")
    shapes = [wts[n].shape for n in SMALL]
    packs = [_pack([src[n] for n in SMALL]) for src in (wts, small, mom, var)]
    outs = _adamw(*packs, name="adamw_small")
    for dst, pk in zip((delta, new_m, new_v), outs):
        dst.update(zip(SMALL, _unpack(pk, shapes)))
    for n in SMALL:
        grads[n] = small[n]

    def shaped(d, n):
        v = d[n].T if n in TRANSPOSED else d[n]
        return v.reshape((1,) + v.shape) if n in BIG or n == "conv_w" else v

    return (loss, grad_x, *[shaped(grads, n) for n in WEIGHTS], *[shaped(delta, n) for n in WEIGHTS],
            *[shaped(new_m, n) for n in WEIGHTS], *[shaped(new_v, n) for n in WEIGHTS])
```

```python
import functools

import jax
import jax.numpy as jnp
from jax import lax
from jax.experimental import pallas as pl
from jax.experimental.pallas import tpu as pltpu

F32 = jnp.float32
BF16 = jnp.bfloat16

EPS = 1e-6
HEADS = 8
HEAD_DIM = 64
D_ATTN = HEADS * HEAD_DIM
D_CONV = 512
CONV_K = 31
QBLK = 128
N_PATTERNS = 3
DILATIONS = (1, 4, 16)
LANES = 128
NEG = -1e30

ADAM_LR = 0.001
ADAM_B1 = 0.9
ADAM_B2 = 0.999
ADAM_EPS = 1e-08
ADAM_WD = 0.01
ADAM_STEP = 10

VMEM_LIMIT = 56 * 1024 * 1024
MESH = pl.DeviceIdType.MESH

NT_DIMS = (((1,), (1,)), ((), ()))
TN_DIMS = (((0,), (0,)), ((), ()))


def _params(*sem):
    return pltpu.CompilerParams(dimension_semantics=sem, vmem_limit_bytes=VMEM_LIMIT)


def _dot(a, b):
    return jnp.dot(a, b, preferred_element_type=F32)


def _dot_nt(a, b):
    return lax.dot_general(a, b, NT_DIMS, preferred_element_type=F32)


def _dot_tn(a, b):
    return lax.dot_general(a, b, TN_DIMS, preferred_element_type=F32)


def _sigmoid(x):
    return 1.0 / (1.0 + jnp.exp(-x))


def _seg_mean(v, e_ref, width):
    hi = v.astype(BF16)
    lo = (v - hi.astype(F32)).astype(BF16)
    e = e_ref[...]
    return (_dot(hi, e) + _dot(lo, e)) * (1.0 / width)


def _seg_matrix(n):
    i = jnp.arange(n)
    return (i[:, None] // HEAD_DIM == i[None, :] // HEAD_DIM).astype(BF16)


ANY = pl.BlockSpec(memory_space=pl.ANY)
DMA_SEMS = pltpu.SemaphoreType.DMA


class _Phase:
    def __init__(self, ins, outs, aliases, nsem, copies):
        self.ins, self.outs, self.aliases = list(ins), list(outs), dict(aliases)
        self.stages = [(nsem, copies)]

    def then(self, other):
        self.stages = self.stages + other.stages
        return self

    @property
    def nsem(self):
        return sum(n for n, _ in self.stages)

    def _copies(self, k, in_refs, out_refs, send_sems, recv_sems):
        base = sum(n for n, _ in self.stages[:k])
        return self.stages[k][1](in_refs, out_refs, lambda i: (send_sems.at[base + i], recv_sems.at[base + i]))

    def start(self, k, *refs):
        for cp in self._copies(k, *refs)[0]:
            cp.start()

    def finish(self, k, *refs):
        starts, arrivals = self._copies(k, *refs)
        for cp in arrivals:
            cp.wait_recv()
        for cp in starts:
            cp.wait_send()


def _run_phase(phase, *, name):
    n_in, n_out = len(phase.ins), len(phase.outs)

    def body(*refs):
        ins, outs = refs[:n_in], refs[n_in:n_in + n_out]
        send_sems, recv_sems = refs[n_in + n_out:]
        for k in range(len(phase.stages)):
            phase.start(k, ins, outs, send_sems, recv_sems)
            phase.finish(k, ins, outs, send_sems, recv_sems)

    return pl.pallas_call(
        body, in_specs=[ANY] * n_in, out_specs=[ANY] * n_out, out_shape=phase.outs,
        input_output_aliases=phase.aliases,
        scratch_shapes=[DMA_SEMS((phase.nsem,)), DMA_SEMS((phase.nsem,))], name=name)(*phase.ins)


def _call(body, *, grid, in_specs, out_specs, out_shape, scratch_shapes=(), sem, name, args, phase=None):
    in_specs, out_specs, out_shape = list(in_specs), list(out_specs), list(out_shape)
    scratch_shapes = list(scratch_shapes)
    if phase is None:
        return pl.pallas_call(body, grid=grid, in_specs=in_specs, out_specs=out_specs, out_shape=out_shape,
                              scratch_shapes=scratch_shapes, compiler_params=_params(*sem), name=name)(*args)
    n_in, n_out, n_scr = len(in_specs), len(out_specs), len(scratch_shapes)
    p_in, p_out = len(phase.ins), len(phase.outs)

    def hosted(*refs):
        ins, pins = refs[:n_in], refs[n_in:n_in + p_in]
        o0 = n_in + p_in
        outs, pouts = refs[o0:o0 + n_out], refs[o0 + n_out:o0 + n_out + p_out]
        s0 = o0 + n_out + p_out
        scr = refs[s0:s0 + n_scr]
        send_sems, recv_sems = refs[s0 + n_scr:]
        step = 0
        for d, n in enumerate(grid):
            step = step * n + pl.program_id(d)
        nsteps = functools.reduce(lambda a, b: a * b, grid)
        nstages = len(phase.stages)
        comm_refs = (pins, pouts, send_sems, recv_sems)

        for k in range(nstages):
            @pl.when(step == (k * nsteps) // nstages)
            def _(k=k):
                if k > 0:
                    phase.finish(k - 1, *comm_refs)
                phase.start(k, *comm_refs)

        body(*ins, *outs, *scr)

        @pl.when(step == nsteps - 1)
        def _():
            phase.finish(nstages - 1, *comm_refs)

    res = pl.pallas_call(
        hosted, grid=grid, in_specs=in_specs + [ANY] * p_in, out_specs=out_specs + [ANY] * p_out,
        out_shape=out_shape + phase.outs,
        input_output_aliases={n_in + i: n_out + o for i, o in phase.aliases.items()},
        scratch_shapes=scratch_shapes + [DMA_SEMS((phase.nsem,)), DMA_SEMS((phase.nsem,))],
        compiler_params=_params(*sem), name=name)(*args, *phase.ins)
    return res[:n_out], res[n_out:]


ROW_CHUNK = 256


def _ffn_fwd(x, gain, wg, wu, wd, tgt, *, tm, name, phase=None):
    T, D = x.shape
    NS, Fs, _ = wg.shape
    with_loss = tgt is not None

    def body(*refs):
        if with_loss:
            x_ref, g_ref, wg_ref, wu_ref, wd_ref, t_ref, h_ref, n_ref, G_ref, U_ref, loss_ref, acc_ref = refs
        else:
            x_ref, g_ref, wg_ref, wu_ref, wd_ref, h_ref, n_ref, G_ref, U_ref, acc_ref = refs
        i = pl.program_id(0)
        j = pl.program_id(1)

        @pl.when(j == 0)
        def _():
            xv = x_ref[...]
            r = lax.rsqrt(jnp.mean(xv * xv, axis=-1, keepdims=True) + EPS)
            n_ref[...] = (xv * r * g_ref[...]).astype(BF16)
            acc_ref[...] = jnp.zeros_like(acc_ref)

        n = n_ref[...]
        G = _dot_nt(n, wg_ref[...])
        U = _dot_nt(n, wu_ref[...])
        G_ref[...] = G.astype(BF16)
        U_ref[...] = U.astype(BF16)
        A = (G * _sigmoid(G) * U).astype(BF16)
        acc_ref[...] += _dot(A, wd_ref[...])

        @pl.when(j == NS - 1)
        def _():
            h = x_ref[...] + 0.5 * acc_ref[...]
            if with_loss:
                e = h - t_ref[...]
                h_ref[...] = e * (1.0 / D)

                @pl.when(i == 0)
                def _():
                    loss_ref[...] = jnp.zeros_like(loss_ref)

                loss_ref[...] += jnp.sum(e * e) * (0.5 / D)
            else:
                h_ref[...] = h

    tok = pl.BlockSpec((tm, D), lambda i, j: (i, 0))
    in_specs = [tok, pl.BlockSpec((1, D), lambda i, j: (0, 0)),
                pl.BlockSpec((None, Fs, D), lambda i, j: (j, 0, 0)),
                pl.BlockSpec((None, Fs, D), lambda i, j: (j, 0, 0)),
                pl.BlockSpec((None, Fs, D), lambda i, j: (j, 0, 0))]
    args = [x, gain, wg, wu, wd]
    act = pl.BlockSpec((None, tm, Fs), lambda i, j: (j, i, 0))
    out_shape = [jax.ShapeDtypeStruct((T, D), F32), jax.ShapeDtypeStruct((T, D), BF16),
                 jax.ShapeDtypeStruct((NS, T, Fs), BF16), jax.ShapeDtypeStruct((NS, T, Fs), BF16)]
    out_specs = [tok, tok, act, act]
    if with_loss:
        in_specs.append(tok)
        args.append(tgt)
        out_shape.append(jax.ShapeDtypeStruct((8, LANES), F32))
        out_specs.append(pl.BlockSpec((8, LANES), lambda i, j: (0, 0)))
    return _call(body, grid=(T // tm, NS), in_specs=in_specs, out_specs=out_specs, out_shape=out_shape,
                 scratch_shapes=[pltpu.VMEM((tm, D), F32)], sem=("arbitrary", "arbitrary"), name=name,
                 args=args, phase=phase)


def _rms_bwd(xv, gain, dn):
    r = lax.rsqrt(jnp.mean(xv * xv, axis=-1, keepdims=True) + EPS)
    xhat = xv * r
    dxh = dn * gain
    dx = r * (dxh - xhat * jnp.mean(dxh * xhat, axis=-1, keepdims=True))
    dg = jnp.sum(dn * xhat, axis=0, keepdims=True)
    return dx, dg


def _ffn_bwd_act(dh, x, gain, G, U, wg, wu, wd, *, tm, name, phase=None):
    T, D = x.shape
    NS, Fs, _ = wg.shape

    def body(dh_ref, x_ref, g_ref, G_ref, U_ref, wg_ref, wu_ref, wd_ref,
             dG_ref, dU_ref, A_ref, dy_ref, dx_ref, dg_ref, acc_ref):
        i = pl.program_id(0)
        j = pl.program_id(1)

        @pl.when(j == 0)
        def _():
            dy_ref[...] = (0.5 * dh_ref[...]).astype(BF16)
            acc_ref[...] = jnp.zeros_like(acc_ref)

        @pl.when((i == 0) & (j == 0))
        def _():
            dg_ref[...] = jnp.zeros_like(dg_ref)

        nchunks = tm // ROW_CHUNK
        dA, dGU = {}, {}
        for step in range(nchunks + 2):
            if step < nchunks:
                rows = slice(step * ROW_CHUNK, (step + 1) * ROW_CHUNK)
                dA[step] = _dot_nt(dy_ref[rows, :], wd_ref[...])
            if 1 <= step <= nchunks:
                k = step - 1
                rows = slice(k * ROW_CHUNK, (k + 1) * ROW_CHUNK)
                Gv = G_ref[rows, :].astype(F32)
                Uv = U_ref[rows, :].astype(F32)
                sig = _sigmoid(Gv)
                s = Gv * sig
                dG = (dA[k] * Uv * (sig * (1.0 + Gv * (1.0 - sig)))).astype(BF16)
                dU = (dA.pop(k) * s).astype(BF16)
                dG_ref[rows, :] = dG
                dU_ref[rows, :] = dU
                A_ref[rows, :] = (s * Uv).astype(BF16)
                dGU[k] = (dG, dU)
            if 2 <= step:
                k = step - 2
                rows = slice(k * ROW_CHUNK, (k + 1) * ROW_CHUNK)
                dG, dU = dGU.pop(k)
                acc_ref[rows, :] += _dot(dG, wg_ref[...]) + _dot(dU, wu_ref[...])

        @pl.when(j == NS - 1)
        def _():
            dx, dg = _rms_bwd(x_ref[...], g_ref[...], acc_ref[...])
            dx_ref[...] = dh_ref[...] + dx
            dg_ref[...] += dg

    tok = pl.BlockSpec((tm, D), lambda i, j: (i, 0))
    act = pl.BlockSpec((None, tm, Fs), lambda i, j: (j, i, 0))
    vec = pl.BlockSpec((1, D), lambda i, j: (0, 0))
    return _call(
        body, grid=(T // tm, NS),
        in_specs=[tok, tok, vec, act, act,
                  pl.BlockSpec((None, Fs, D), lambda i, j: (j, 0, 0)),
                  pl.BlockSpec((None, Fs, D), lambda i, j: (j, 0, 0)),
                  pl.BlockSpec((None, Fs, D), lambda i, j: (j, 0, 0))],
        out_specs=[act, act, act, tok, tok, vec],
        out_shape=[jax.ShapeDtypeStruct((NS, T, Fs), BF16)] * 3
        + [jax.ShapeDtypeStruct((T, D), BF16), jax.ShapeDtypeStruct((T, D), F32),
           jax.ShapeDtypeStruct((1, D), F32)],
        scratch_shapes=[pltpu.VMEM((tm, D), F32)],
        sem=("arbitrary", "arbitrary"), name=name, args=(dh, x, gain, G, U, wg, wu, wd), phase=phase)


def _ffn_bwd_w(n, dy, dG, dU, A, *, tk, name, phase=None):
    T, D = n.shape
    NS, _, Fs = dG.shape

    def body(n_ref, dy_ref, dG_ref, dU_ref, A_ref, wg_ref, wu_ref, wd_ref):
        @pl.when(pl.program_id(1) == 0)
        def _():
            wg_ref[...] = jnp.zeros_like(wg_ref)
            wu_ref[...] = jnp.zeros_like(wu_ref)
            wd_ref[...] = jnp.zeros_like(wd_ref)

        nv = n_ref[...]
        wg_ref[...] += _dot_tn(dG_ref[...], nv)
        wu_ref[...] += _dot_tn(dU_ref[...], nv)
        wd_ref[...] += _dot_tn(A_ref[...], dy_ref[...])

    tok = pl.BlockSpec((tk, D), lambda j, k: (k, 0))
    act = pl.BlockSpec((None, tk, Fs), lambda j, k: (j, k, 0))
    return _call(
        body, grid=(NS, T // tk), in_specs=[tok, tok, act, act, act],
        out_specs=[pl.BlockSpec((None, Fs, D), lambda j, k: (j, 0, 0))] * 3,
        out_shape=[jax.ShapeDtypeStruct((NS, Fs, D), F32)] * 3,
        sem=("arbitrary", "arbitrary"), name=name, args=(n, dy, dG, dU, A), phase=phase)


def _inproj_fwd(h, gain, win, *, tm, name, phase=None):
    T, D = h.shape
    NS, _, Cs = win.shape

    def body(h_ref, g_ref, w_ref, u_ref, n_ref):
        @pl.when(pl.program_id(1) == 0)
        def _():
            xv = h_ref[...]
            r = lax.rsqrt(jnp.mean(xv * xv, axis=-1, keepdims=True) + EPS)
            n_ref[...] = (xv * r * g_ref[...]).astype(BF16)

        u_ref[...] = _dot(n_ref[...], w_ref[...])

    tok = pl.BlockSpec((tm, D), lambda i, j: (i, 0))
    return _call(
        body, grid=(T // tm, NS),
        in_specs=[tok, pl.BlockSpec((1, D), lambda i, j: (0, 0)),
                  pl.BlockSpec((None, D, Cs), lambda i, j: (j, 0, 0))],
        out_specs=[pl.BlockSpec((tm, Cs), lambda i, j: (i, j)), tok],
        out_shape=[jax.ShapeDtypeStruct((T, NS * Cs), F32), jax.ShapeDtypeStruct((T, D), BF16)],
        sem=("arbitrary", "arbitrary"), name=name, args=(h, gain, win), phase=phase)


def _inproj_bwd_act(du, dh, h, gain, win, *, tm, name, phase=None):
    T, D = h.shape
    NS, _, Cs = win.shape

    def body(du_ref, dh_ref, h_ref, g_ref, w_ref, dx_ref, dg_ref, acc_ref):
        i = pl.program_id(0)
        j = pl.program_id(1)

        @pl.when(j == 0)
        def _():
            acc_ref[...] = jnp.zeros_like(acc_ref)

        @pl.when((i == 0) & (j == 0))
        def _():
            dg_ref[...] = jnp.zeros_like(dg_ref)

        acc_ref[...] += _dot_nt(du_ref[...], w_ref[...])

        @pl.when(j == NS - 1)
        def _():
            dx, dg = _rms_bwd(h_ref[...], g_ref[...], acc_ref[...])
            dx_ref[...] = dh_ref[...] + dx
            dg_ref[...] += dg

    tok = pl.BlockSpec((tm, D), lambda i, j: (i, 0))
    vec = pl.BlockSpec((1, D), lambda i, j: (0, 0))
    return _call(
        body, grid=(T // tm, NS),
        in_specs=[pl.BlockSpec((tm, Cs), lambda i, j: (i, j)), tok, tok, vec,
                  pl.BlockSpec((None, D, Cs), lambda i, j: (j, 0, 0))],
        out_specs=[tok, vec],
        out_shape=[jax.ShapeDtypeStruct((T, D), F32), jax.ShapeDtypeStruct((1, D), F32)],
        scratch_shapes=[pltpu.VMEM((tm, D), F32)],
        sem=("arbitrary", "arbitrary"), name=name, args=(du, dh, h, gain, win), phase=phase)


def _inproj_bwd_w(n, du, ns, *, tk, name, phase=None):
    T, D = n.shape
    Cs = du.shape[1] // ns

    def body(n_ref, du_ref, w_ref):
        @pl.when(pl.program_id(1) == 0)
        def _():
            w_ref[...] = jnp.zeros_like(w_ref)

        w_ref[...] += _dot_tn(n_ref[...], du_ref[...])

    return _call(
        body, grid=(ns, T // tk),
        in_specs=[pl.BlockSpec((tk, D), lambda j, k: (k, 0)), pl.BlockSpec((tk, Cs), lambda j, k: (k, j))],
        out_specs=[pl.BlockSpec((None, D, Cs), lambda j, k: (j, 0, 0))],
        out_shape=[jax.ShapeDtypeStruct((ns, D, Cs), F32)],
        sem=("arbitrary", "arbitrary"), name=name, args=(n, du), phase=phase)


STRIDE = 4


def _permute(src_ref, tmp_ref, put):
    S = src_ref.shape[0]
    L4, L16 = S // STRIDE, S // (STRIDE * STRIDE)
    put(0, 0, src_ref[...])
    for r0 in range(STRIDE):
        v = src_ref[pl.ds(r0, L4, stride=STRIDE), :]
        put(1, r0 * L4, v)
        tmp_ref[r0 * L4:(r0 + 1) * L4, :] = v
    for r0 in range(STRIDE):
        for r1 in range(STRIDE):
            put(2, (r1 * STRIDE + r0) * L16, tmp_ref[pl.ds(r0 * L4 + r1, L16, stride=STRIDE), :])


def _permute_out(src_ref, tmp_ref, out_ref, cast):
    for cc in range(src_ref.shape[0]):
        cols = slice(cc * LANES, (cc + 1) * LANES)

        def put(p, row0, v, cols=cols):
            out_ref[p, row0:row0 + v.shape[0], cols] = v.astype(cast)

        _permute(src_ref.at[cc], tmp_ref, put)


def _unpermute_in(get_block, dst_ref, tmp_ref, p, S):
    L4, L16 = S // STRIDE, S // (STRIDE * STRIDE)
    if p == 0:
        dst_ref[...] = get_block(0, S)
        return
    if p == 1:
        for r0 in range(STRIDE):
            dst_ref[pl.ds(r0, L4, stride=STRIDE), :] = get_block(r0 * L4, L4)
        return
    for r0 in range(STRIDE):
        for r1 in range(STRIDE):
            tmp_ref[pl.ds(r0 * L4 + r1, L16, stride=STRIDE), :] = get_block((r1 * STRIDE + r0) * L16, L16)
    for r0 in range(STRIDE):
        dst_ref[pl.ds(r0, L4, stride=STRIDE), :] = tmp_ref[r0 * L4:(r0 + 1) * L4, :]


def _qkv_prep(u, gains, B, S, *, name):
    emat = _seg_matrix(D_ATTN)

    def body(u_ref, g_ref, e_ref, out_ref, scr_ref, tmp_ref):
        c = pl.program_id(1)
        xv = u_ref[...]
        ms = _seg_mean(xv * xv, e_ref, HEAD_DIM)
        r = jnp.where(c < 2, lax.rsqrt(ms + EPS), 1.0)
        yv = xv * r * g_ref[...]
        for cc in range(4):
            scr_ref[cc] = yv[:, cc * LANES:(cc + 1) * LANES]
        _permute_out(scr_ref, tmp_ref, out_ref, BF16)

    return pl.pallas_call(
        body, grid=(B, 3),
        in_specs=[pl.BlockSpec((S, D_ATTN), lambda b, c: (b, c)),
                  pl.BlockSpec((None, 1, D_ATTN), lambda b, c: (c, 0, 0)),
                  pl.BlockSpec((D_ATTN, D_ATTN), lambda b, c: (0, 0))],
        out_specs=pl.BlockSpec((None, N_PATTERNS, None, S, D_ATTN), lambda b, c: (c, 0, b, 0, 0)),
        out_shape=jax.ShapeDtypeStruct((3, N_PATTERNS, B, S, D_ATTN), BF16),
        scratch_shapes=[pltpu.VMEM((4, S, LANES), F32), pltpu.VMEM((S, LANES), F32)],
        compiler_params=_params("arbitrary", "arbitrary"), name=name)(u, gains, emat)


def _band_mask(p, b):
    nblk = jnp.right_shift(16, 2 * p)
    has_prev = jnp.bitwise_and(b, nblk - 1) != 0
    qi = lax.broadcasted_iota(jnp.int32, (QBLK, 2 * QBLK), 0)
    ci = lax.broadcasted_iota(jnp.int32, (QBLK, 2 * QBLK), 1)
    dist = QBLK + qi - ci
    return (dist >= 0) & (dist <= QBLK) & (has_prev | (ci >= QBLK))


def _first_head(rows):
    return lax.broadcasted_iota(jnp.int32, (rows, LANES), 1) < HEAD_DIM


def _split_heads(pair):
    first = _first_head(pair.shape[0])
    zero = jnp.zeros_like(pair)
    return jnp.concatenate([jnp.where(first, pair, zero), jnp.where(first, zero, pair)], axis=0)


def _merge_heads(col_a, col_b):
    rows = col_a.shape[0]
    return jnp.where(_first_head(rows), jnp.broadcast_to(col_a, (rows, LANES)), jnp.broadcast_to(col_b, (rows, LANES)))


QB_FWD = 8
QB_BWD = 4


def _attn_fwd(qkv, *, name, phase=None):
    QB = QB_FWD
    nb = qkv.shape[2]

    def body(q_ref, kp_ref, kc_ref, vp_ref, vc_ref, o_ref, lse_ref):
        kall = jnp.concatenate([kp_ref[...]] + [kc_ref[t] for t in range(QB)], axis=0)
        vall = jnp.concatenate([vp_ref[...]] + [vc_ref[t] for t in range(QB)], axis=0)
        masks = []
        for t in range(QB):
            mask = _band_mask(pl.program_id(0), QB * pl.program_id(1) + t)
            masks.append(jnp.concatenate([mask, mask], axis=0))
        units = [(t, hp) for t in range(QB) for hp in range(HEADS // 2)]
        scores, probs = {}, {}
        for step in range(len(units) + 2):
            if step < len(units):
                t, hp = units[step]
                cols = slice(hp * LANES, (hp + 1) * LANES)
                scores[step] = _dot_nt(_split_heads(q_ref[t, :, cols]), kall[t * QBLK:(t + 2) * QBLK, cols])
            if 1 <= step <= len(units):
                t, hp = units[step - 1]
                cols = slice(hp * LANES, (hp + 1) * LANES)
                s = jnp.where(masks[t], scores.pop(step - 1), NEG)
                m = jnp.max(s, axis=-1, keepdims=True)
                e = jnp.exp(s - m)
                l = jnp.sum(e, axis=-1, keepdims=True)
                probs[step - 1] = (e * (1.0 / l)).astype(BF16)
                lse = m + jnp.log(l)
                lse_ref[t, :, cols] = _merge_heads(lse[:QBLK], lse[QBLK:])
            if 2 <= step:
                t, hp = units[step - 2]
                cols = slice(hp * LANES, (hp + 1) * LANES)
                pr = probs.pop(step - 2)
                o_ref[t, :, cols] = _dot(jnp.concatenate([pr[:QBLK], pr[QBLK:]], axis=1),
                                         _split_heads(vall[t * QBLK:(t + 2) * QBLK, cols]))

    cur = lambda which: pl.BlockSpec((None, None, QB, QBLK, D_ATTN), lambda p, i: (which, p, i, 0, 0))
    prev = lambda which: pl.BlockSpec((None, None, None, QBLK, D_ATTN),
                                      lambda p, i: (which, p, jnp.maximum(QB * i - 1, 0), 0, 0))
    out = pl.BlockSpec((None, QB, QBLK, D_ATTN), lambda p, i: (p, i, 0, 0))
    return _call(
        body, grid=(N_PATTERNS, nb // QB), in_specs=[cur(0), prev(1), cur(1), prev(2), cur(2)], out_specs=[out, out],
        out_shape=[jax.ShapeDtypeStruct((N_PATTERNS, nb, QBLK, D_ATTN), F32)] * 2,
        sem=("arbitrary", "arbitrary"), name=name, args=(qkv, qkv, qkv, qkv, qkv), phase=phase)


def _attn_combine(o3, lse3, B, S, *, name):
    def body(o_ref, l_ref, a_ref, lt_ref, so_ref, sl_ref, tmp_ref):
        for p in range(N_PATTERNS):
            _unpermute_in(lambda r0, n, p=p: o_ref[p, pl.ds(r0, n), :], so_ref.at[p], tmp_ref, p, S)
            _unpermute_in(lambda r0, n, p=p: l_ref[p, pl.ds(r0, n), :], sl_ref.at[p], tmp_ref, p, S)
        l0, l1, l2 = sl_ref[0], sl_ref[1], sl_ref[2]
        m = jnp.maximum(jnp.maximum(l0, l1), l2)
        w0, w1, w2 = jnp.exp(l0 - m), jnp.exp(l1 - m), jnp.exp(l2 - m)
        tot = w0 + w1 + w2
        a_ref[...] = (w0 * so_ref[0] + w1 * so_ref[1] + w2 * so_ref[2]) / tot
        lt_ref[...] = m + jnp.log(tot)

    o3 = o3.reshape(N_PATTERNS, B, S, D_ATTN)
    lse3 = lse3.reshape(N_PATTERNS, B, S, D_ATTN)
    inp = pl.BlockSpec((N_PATTERNS, None, S, LANES), lambda b, c: (0, b, 0, c))
    out = pl.BlockSpec((S, LANES), lambda b, c: (b, c))
    return pl.pallas_call(
        body, grid=(B, D_ATTN // LANES), in_specs=[inp, inp], out_specs=[out, out],
        out_shape=[jax.ShapeDtypeStruct((B * S, D_ATTN), F32)] * 2,
        scratch_shapes=[pltpu.VMEM((N_PATTERNS, S, LANES), F32)] * 2 + [pltpu.VMEM((S, LANES), F32)],
        compiler_params=_params("arbitrary", "arbitrary"), name=name)(o3, lse3)


STAT_D = 8


def _attn_bwd_prep(dattn, attn, lse, B, S, *, name):
    emat = _seg_matrix(LANES)
    ncc = D_ATTN // LANES

    def body(da_ref, a_ref, l_ref, e_ref, do_ref, st_ref, scr_ref, nat_ref, tmp_ref):
        cc = pl.program_id(1)
        da = da_ref[...]
        dsum = _seg_mean(da * a_ref[...], e_ref, 1.0)
        scr_ref[...] = da

        def put_do(p, row0, v):
            do_ref[p, row0:row0 + v.shape[0], :] = v.astype(BF16)

        _permute(scr_ref, tmp_ref, put_do)

        lane = lax.broadcasted_iota(jnp.int32, (S, LANES), 1)
        h0 = 2 * cc
        vals = ((h0, l_ref[:, 0:1]), (h0 + 1, l_ref[:, HEAD_DIM:HEAD_DIM + 1]),
                (STAT_D + h0, dsum[:, 0:1]), (STAT_D + h0 + 1, dsum[:, HEAD_DIM:HEAD_DIM + 1]))
        tile = jnp.where(cc == 0, 0.0, nat_ref[...])
        for at, col in vals:
            tile = jnp.where(lane == at, col, tile)
        nat_ref[...] = tile

        @pl.when(cc == ncc - 1)
        def _():
            def put_st(p, row0, v):
                st_ref[p, row0:row0 + v.shape[0], :] = v

            _permute(nat_ref, tmp_ref, put_st)

    inp = pl.BlockSpec((S, LANES), lambda b, c: (b, c))
    return pl.pallas_call(
        body, grid=(B, ncc),
        in_specs=[inp, inp, inp, pl.BlockSpec((LANES, LANES), lambda b, c: (0, 0))],
        out_specs=[pl.BlockSpec((N_PATTERNS, None, S, LANES), lambda b, c: (0, b, 0, c)),
                   pl.BlockSpec((N_PATTERNS, None, S, LANES), lambda b, c: (0, b, 0, 0))],
        out_shape=[jax.ShapeDtypeStruct((N_PATTERNS, B, S, D_ATTN), BF16),
                   jax.ShapeDtypeStruct((N_PATTERNS, B, S, LANES), F32)],
        scratch_shapes=[pltpu.VMEM((S, LANES), F32)] * 3,
        compiler_params=_params("arbitrary", "arbitrary"), name=name)(dattn, attn, lse, emat)


def _attn_bwd(qkv, do3, st3, *, name, phase=None):
    QB = QB_BWD
    nb = qkv.shape[2]
    ngroups = nb // QB

    def body(q_ref, kp_ref, kc_ref, vp_ref, vc_ref, do_ref, st_ref, out_ref, carry_ref):
        p = pl.program_id(0)
        i = pl.program_id(1)

        @pl.when((p == 0) & (i == 0))
        def _():
            carry_ref[...] = jnp.zeros_like(carry_ref)

        kall = jnp.concatenate([kp_ref[...]] + [kc_ref[t] for t in range(QB)], axis=0)
        vall = jnp.concatenate([vp_ref[...]] + [vc_ref[t] for t in range(QB)], axis=0)

        masks = []
        for t in range(QB):
            mask = _band_mask(p, QB * i + t) & (i < ngroups)
            masks.append(jnp.concatenate([mask, mask], axis=0))

        def operands(t, hp):
            cols = slice(hp * LANES, (hp + 1) * LANES)
            kh, vh = kall[t * QBLK:(t + 2) * QBLK, cols], vall[t * QBLK:(t + 2) * QBLK, cols]
            return kh, vh, _split_heads(q_ref[t, :, cols]), _split_heads(do_ref[t, :, cols])

        def stage_scores(t, hp):
            kh, vh, q2, do2 = operands(t, hp)
            return _dot_nt(q2, kh), _dot_nt(do2, vh)

        def stage_softmax(t, hp, s, dp):
            h0, h1 = 2 * hp, 2 * hp + 1
            lse = jnp.concatenate([st_ref[t, :, h0:h0 + 1], st_ref[t, :, h1:h1 + 1]], axis=0)
            dsum = jnp.concatenate([st_ref[t, :, STAT_D + h0:STAT_D + h0 + 1],
                                    st_ref[t, :, STAT_D + h1:STAT_D + h1 + 1]], axis=0)
            pr = jnp.where(masks[t], jnp.exp(s - lse), 0.0)
            return (pr * (dp - dsum)).astype(BF16), pr.astype(BF16)

        def stage_grads(t, hp, ds, prb):
            cols = slice(hp * LANES, (hp + 1) * LANES)
            kh, vh, q2, do2 = operands(t, hp)
            dq = _dot(jnp.concatenate([ds[:QBLK], ds[QBLK:]], axis=1), _split_heads(kh))
            dk, dv = _dot_tn(ds, q2), _dot_tn(prb, do2)
            if t == 0:
                for c in range(3):
                    for tt in range(QB):
                        v = carry_ref[c, tt, :, cols]
                        if tt == QB - 1 and c > 0:
                            v = v + (dk if c == 1 else dv)[:QBLK]
                        out_ref[c, tt, :, cols] = v.astype(BF16)
            else:
                carry_ref[1, t - 1, :, cols] += dk[:QBLK]
                carry_ref[2, t - 1, :, cols] += dv[:QBLK]
            carry_ref[0, t, :, cols] = dq
            carry_ref[1, t, :, cols] = dk[QBLK:]
            carry_ref[2, t, :, cols] = dv[QBLK:]

        units = [(t, hp) for hp in range(HEADS // 2) for t in range(QB)]
        scores, probs = {}, {}
        for step in range(len(units) + 2):
            if step < len(units):
                scores[step] = stage_scores(*units[step])
            if 1 <= step <= len(units):
                probs[step - 1] = stage_softmax(*units[step - 1], *scores.pop(step - 1))
            if 2 <= step:
                stage_grads(*units[step - 2], *probs.pop(step - 2))

    group = lambda i: jnp.minimum(i, ngroups - 1)
    cur = lambda which: pl.BlockSpec((None, None, QB, QBLK, D_ATTN), lambda p, i: (which, p, group(i), 0, 0))
    prev = lambda which: pl.BlockSpec((None, None, None, QBLK, D_ATTN),
                                      lambda p, i: (which, p, jnp.maximum(QB * group(i) - 1, 0), 0, 0))
    aux = lambda lanes: pl.BlockSpec((None, QB, QBLK, lanes), lambda p, i: (p, group(i), 0, 0))
    return _call(
        body, grid=(N_PATTERNS, ngroups + 1),
        in_specs=[cur(0), prev(1), cur(1), prev(2), cur(2), aux(D_ATTN), aux(LANES)],
        out_specs=[pl.BlockSpec((3, None, QB, QBLK, D_ATTN), lambda p, i: (0, p, jnp.maximum(i - 1, 0), 0, 0))],
        out_shape=[jax.ShapeDtypeStruct((3, N_PATTERNS, nb, QBLK, D_ATTN), BF16)],
        scratch_shapes=[pltpu.VMEM((3, QB, QBLK, D_ATTN), F32)],
        sem=("arbitrary", "arbitrary"), name=name, args=(qkv, qkv, qkv, qkv, qkv, do3, st3), phase=phase)


def _attn_grad_combine(cur, u, gains, B, S, *, name, phase=None):
    emat = _seg_matrix(LANES)

    def body(cur_ref, u_ref, g_ref, e_ref, du_ref, dg_ref, scr_ref, tmp_ref):
        c = pl.program_id(0)
        b = pl.program_id(2)
        for p in range(N_PATTERNS):
            _unpermute_in(lambda r0, n, p=p: cur_ref[p, pl.ds(r0, n), :].astype(F32), scr_ref.at[p], tmp_ref, p, S)
        dy = scr_ref[0] + scr_ref[1] + scr_ref[2]
        xv = u_ref[...]
        gain = g_ref[...]
        ms = _seg_mean(xv * xv, e_ref, HEAD_DIM)
        r = lax.rsqrt(ms + EPS)
        xhat = xv * r
        dxh = dy * gain
        dx = r * (dxh - xhat * _seg_mean(dxh * xhat, e_ref, HEAD_DIM))
        du_ref[...] = jnp.where(c < 2, dx, dy).astype(BF16)

        @pl.when((b == 0))
        def _():
            dg_ref[...] = jnp.zeros_like(dg_ref)

        dg_ref[...] += jnp.sum(dy * xhat, axis=0, keepdims=True)

    cur = cur.reshape(3, N_PATTERNS, B, S, D_ATTN)
    ncc = D_ATTN // LANES
    return _call(
        body, grid=(3, ncc, B),
        in_specs=[pl.BlockSpec((None, N_PATTERNS, None, S, LANES), lambda c, cc, b: (c, 0, b, 0, cc)),
                  pl.BlockSpec((S, LANES), lambda c, cc, b: (b, c * ncc + cc)),
                  pl.BlockSpec((None, 1, LANES), lambda c, cc, b: (c, 0, cc)),
                  pl.BlockSpec((LANES, LANES), lambda c, cc, b: (0, 0))],
        out_specs=[pl.BlockSpec((S, LANES), lambda c, cc, b: (b, c * ncc + cc)),
                   pl.BlockSpec((None, 1, LANES), lambda c, cc, b: (c, 0, cc))],
        out_shape=[jax.ShapeDtypeStruct((B * S, 3 * D_ATTN), BF16), jax.ShapeDtypeStruct((3, 1, D_ATTN), F32)],
        scratch_shapes=[pltpu.VMEM((N_PATTERNS, S, LANES), F32), pltpu.VMEM((S, LANES), F32)],
        sem=("arbitrary", "arbitrary", "arbitrary"), name=name, args=(cur, u, gains, emat), phase=phase)


HALO = 32
SUB = 64
SUBLANES = 8


def _shifted_copies(src_ref, sh_ref, tc):
    sh_ref[0] = src_ref[...]
    for r in range(1, SUBLANES):
        sh_ref[r, 0:tc + HALO - SUBLANES, :] = src_ref[pl.ds(r, tc + HALO - SUBLANES), :]


def _shifted(sh_ref, start, size):
    return sh_ref[start % SUBLANES, pl.ds(start - start % SUBLANES, size), :]


def _conv_fwd(u, cw, cb, lg, lb, B, S, *, tc, name, phase=None):
    nchunk = S // tc
    hb = tc // HALO

    def body(ca_ref, cap_ref, cg_ref, cgp_ref, w_ref, cb_ref, lg_ref, lb_ref, cv_ref, glu_ref, y_ref, pad_ref, sh_ref):
        i = pl.program_id(1)
        glu = ca_ref[...] * _sigmoid(cg_ref[...])
        glu_ref[...] = glu
        prev = cap_ref[...] * _sigmoid(cgp_ref[...])
        pad_ref[0:HALO, :] = jnp.where(i > 0, prev, 0.0)
        pad_ref[HALO:, :] = glu
        _shifted_copies(pad_ref, sh_ref, tc)
        for sub in range(tc // SUB):
            acc = jnp.zeros((SUB, D_CONV), F32) + cb_ref[...]
            for k in range(CONV_K):
                acc = acc + _shifted(sh_ref, sub * SUB + HALO - (CONV_K - 1) + k, SUB) * w_ref[pl.ds(k, 1), :]
            y_ref[sub * SUB:(sub + 1) * SUB, :] = acc
        y = y_ref[...]
        mu = jnp.mean(y, axis=-1, keepdims=True)
        yc = y - mu
        var = jnp.mean(yc * yc, axis=-1, keepdims=True)
        z = yc * lax.rsqrt(var + EPS) * lg_ref[...] + lb_ref[...]
        cv_ref[...] = (z * _sigmoid(z)).astype(BF16)

    def cur(col):
        return pl.BlockSpec((tc, D_CONV), lambda b, i: (b * nchunk + i, col))

    def halo(col):
        return pl.BlockSpec((HALO, D_CONV), lambda b, i: (jnp.maximum((b * nchunk + i) * hb - 1, 0), col))

    vec = pl.BlockSpec((1, D_CONV), lambda b, i: (0, 0))
    out = pl.BlockSpec((tc, D_CONV), lambda b, i: (b * nchunk + i, 0))
    return _call(
        body, grid=(B, nchunk),
        in_specs=[cur(3), halo(3), cur(4), halo(4), pl.BlockSpec((CONV_K, D_CONV), lambda b, i: (0, 0)), vec, vec, vec],
        out_specs=[out, out, out],
        out_shape=[jax.ShapeDtypeStruct((B * S, D_CONV), BF16), jax.ShapeDtypeStruct((B * S, D_CONV), F32),
                   jax.ShapeDtypeStruct((B * S, D_CONV), F32)],
        scratch_shapes=[pltpu.VMEM((tc + HALO, D_CONV), F32), pltpu.VMEM((SUBLANES, tc + HALO, D_CONV), F32)],
        sem=("arbitrary", "arbitrary"), name=name, args=(u, u, u, u, cw, cb, lg, lb), phase=phase)


def _conv_bwd_norm(dcv, y, lg, lb, *, tc, name):
    T = y.shape[0]

    def body(dcv_ref, y_ref, lg_ref, lb_ref, dy_ref, part_ref):
        yv = y_ref[...]
        mu = jnp.mean(yv, axis=-1, keepdims=True)
        yc = yv - mu
        var = jnp.mean(yc * yc, axis=-1, keepdims=True)
        rstd = lax.rsqrt(var + EPS)
        xhat = yc * rstd
        z = xhat * lg_ref[...] + lb_ref[...]
        sig = _sigmoid(z)
        dz = dcv_ref[...] * (sig * (1.0 + z * (1.0 - sig)))
        dxh = dz * lg_ref[...]
        dy = rstd * (dxh - jnp.mean(dxh, axis=-1, keepdims=True)
                     - xhat * jnp.mean(dxh * xhat, axis=-1, keepdims=True))
        dy_ref[...] = dy

        @pl.when(pl.program_id(0) == 0)
        def _():
            part_ref[...] = jnp.zeros_like(part_ref)

        part_ref[0:1, :] += jnp.sum(dz * xhat, axis=0, keepdims=True)
        part_ref[1:2, :] += jnp.sum(dz, axis=0, keepdims=True)
        part_ref[2:3, :] += jnp.sum(dy, axis=0, keepdims=True)

    tok = pl.BlockSpec((tc, D_CONV), lambda i: (i, 0))
    vec = pl.BlockSpec((1, D_CONV), lambda i: (0, 0))
    return pl.pallas_call(
        body, grid=(T // tc,), in_specs=[tok, tok, vec, vec],
        out_specs=[tok, pl.BlockSpec((8, D_CONV), lambda i: (0, 0))],
        out_shape=[jax.ShapeDtypeStruct((T, D_CONV), F32), jax.ShapeDtypeStruct((8, D_CONV), F32)],
        compiler_params=_params("arbitrary"), name=name)(dcv, y, lg, lb)


def _conv_bwd_taps(dy, glu, u, cw, B, S, *, tc, name, phase=None):
    nchunk = S // tc
    hb = tc // HALO
    last_hb = B * S // HALO - 1

    def body(dy_ref, dyn_ref, glu_ref, glup_ref, ca_ref, cg_ref, w_ref, dca_ref, dcg_ref, dw_ref,
             dyp_ref, glp_ref, acc_ref, shd_ref, shg_ref):
        b = pl.program_id(0)
        i = pl.program_id(1)
        dy = dy_ref[...]
        dyp_ref[0:tc, :] = dy
        dyp_ref[tc:, :] = jnp.where(i < nchunk - 1, dyn_ref[...], 0.0)
        glp_ref[0:HALO, :] = jnp.where(i > 0, glup_ref[...], 0.0)
        glp_ref[HALO:, :] = glu_ref[...]
        _shifted_copies(dyp_ref, shd_ref, tc)
        _shifted_copies(glp_ref, shg_ref, tc)

        @pl.when((b == 0) & (i == 0))
        def _():
            dw_ref[...] = jnp.zeros_like(dw_ref)

        for sub in range(tc // SUB):
            acc = jnp.zeros((SUB, D_CONV), F32)
            for k in range(CONV_K):
                acc = acc + _shifted(shd_ref, sub * SUB + (CONV_K - 1) - k, SUB) * w_ref[pl.ds(k, 1), :]
            acc_ref[sub * SUB:(sub + 1) * SUB, :] = acc
        for k in range(CONV_K):
            dw_ref[k:k + 1, :] += jnp.sum(dy * _shifted(shg_ref, HALO - (CONV_K - 1) + k, tc), axis=0, keepdims=True)
        dglu = acc_ref[...]
        ca = ca_ref[...]
        sig = _sigmoid(cg_ref[...])
        dca_ref[...] = (dglu * sig).astype(BF16)
        dcg_ref[...] = (dglu * ca * sig * (1.0 - sig)).astype(BF16)

    tok = pl.BlockSpec((tc, D_CONV), lambda b, i: (b * nchunk + i, 0))
    nxt = pl.BlockSpec((HALO, D_CONV), lambda b, i: (jnp.minimum((b * nchunk + i + 1) * hb, last_hb), 0))
    prv = pl.BlockSpec((HALO, D_CONV), lambda b, i: (jnp.maximum((b * nchunk + i) * hb - 1, 0), 0))
    return _call(
        body, grid=(B, nchunk),
        in_specs=[tok, nxt, tok, prv,
                  pl.BlockSpec((tc, D_CONV), lambda b, i: (b * nchunk + i, 3)),
                  pl.BlockSpec((tc, D_CONV), lambda b, i: (b * nchunk + i, 4)),
                  pl.BlockSpec((CONV_K, D_CONV), lambda b, i: (0, 0))],
        out_specs=[tok, tok, pl.BlockSpec((32, D_CONV), lambda b, i: (0, 0))],
        out_shape=[jax.ShapeDtypeStruct((B * S, D_CONV), BF16), jax.ShapeDtypeStruct((B * S, D_CONV), BF16),
                   jax.ShapeDtypeStruct((32, D_CONV), F32)],
        scratch_shapes=[pltpu.VMEM((tc + HALO, D_CONV), F32), pltpu.VMEM((tc + HALO, D_CONV), F32),
                        pltpu.VMEM((tc, D_CONV), F32), pltpu.VMEM((SUBLANES, tc + HALO, D_CONV), F32),
                        pltpu.VMEM((SUBLANES, tc + HALO, D_CONV), F32)],
        sem=("arbitrary", "arbitrary"), name=name, args=(dy, dy, glu, glu, u, u, cw), phase=phase)


def _outproj_fwd(h, attn, cv, wout, *, tm, name):
    T, D = h.shape

    def body(h_ref, a_ref, c_ref, w_ref, o_ref):
        o_ref[...] = (h_ref[...] + _dot(a_ref[...].astype(BF16), w_ref[0:D_ATTN, :])
                      + _dot(c_ref[...], w_ref[D_ATTN:, :]))

    tok = pl.BlockSpec((tm, D), lambda i: (i, 0))
    half = pl.BlockSpec((tm, D_ATTN), lambda i: (i, 0))
    return pl.pallas_call(
        body, grid=(T // tm,), in_specs=[tok, half, half, pl.BlockSpec(wout.shape, lambda i: (0, 0))],
        out_specs=tok, out_shape=jax.ShapeDtypeStruct((T, D), F32),
        compiler_params=_params("arbitrary"), name=name)(h, attn, cv, wout)


def _outproj_bwd(dh, attn, cv, wout, *, tm, name):
    T, D = dh.shape

    def body(dh_ref, a_ref, c_ref, w_ref, da_ref, dc_ref, dw_ref):
        @pl.when(pl.program_id(0) == 0)
        def _():
            dw_ref[...] = jnp.zeros_like(dw_ref)

        dhb = dh_ref[...].astype(BF16)
        da_ref[...] = _dot_nt(dhb, w_ref[0:D_ATTN, :])
        dc_ref[...] = _dot_nt(dhb, w_ref[D_ATTN:, :])
        dw_ref[0:D_ATTN, :] += _dot_tn(a_ref[...].astype(BF16), dhb)
        dw_ref[D_ATTN:, :] += _dot_tn(c_ref[...], dhb)

    tok = pl.BlockSpec((tm, D), lambda i: (i, 0))
    half = pl.BlockSpec((tm, D_ATTN), lambda i: (i, 0))
    wspec = pl.BlockSpec(wout.shape, lambda i: (0, 0))
    return pl.pallas_call(
        body, grid=(T // tm,), in_specs=[tok, half, half, wspec], out_specs=[half, half, wspec],
        out_shape=[jax.ShapeDtypeStruct((T, D_ATTN), F32), jax.ShapeDtypeStruct((T, D_ATTN), F32),
                   jax.ShapeDtypeStruct(wout.shape, F32)],
        compiler_params=_params("arbitrary"), name=name)(dh, attn, cv, wout)


ADAM_BLOCK_BYTES = 3 * 512 * 1024


def _adamw(w, g, m, v, *, name):
    R, C = w.shape
    tr = R
    for cand in (512, 352, 256, 176, 128, 64, 32, 16, 8):
        if R % cand == 0 and cand * C * 4 <= ADAM_BLOCK_BYTES:
            tr = cand
            break
    c1 = 1.0 - ADAM_B1 ** ADAM_STEP
    c2 = 1.0 - ADAM_B2 ** ADAM_STEP

    def body(w_ref, g_ref, m_ref, v_ref, d_ref, nm_ref, nv_ref):
        gv = g_ref[...]
        nm = ADAM_B1 * m_ref[...] + (1.0 - ADAM_B1) * gv
        nv = ADAM_B2 * v_ref[...] + (1.0 - ADAM_B2) * (gv * gv)
        d_ref[...] = -ADAM_LR * ((nm / c1) / (jnp.sqrt(nv / c2) + ADAM_EPS) + ADAM_WD * w_ref[...])
        nm_ref[...] = nm
        nv_ref[...] = nv

    blk = pl.BlockSpec((tr, C), lambda i: (i, 0))
    return pl.pallas_call(
        body, grid=(R // tr,), in_specs=[blk] * 4, out_specs=[blk] * 3,
        out_shape=[jax.ShapeDtypeStruct((R, C), F32)] * 3,
        compiler_params=_params("arbitrary"), name=name)(w, g, m, v)


ADAM_SPLIT = 4


def _adamw_many(ws, gs, ms, vs, *, name, phase=None):
    n = len(ws)
    c1 = 1.0 - ADAM_B1 ** ADAM_STEP
    c2 = 1.0 - ADAM_B2 ** ADAM_STEP

    def body(*refs):
        ins, outs = refs[:4 * n], refs[4 * n:]
        for a in range(n):
            w_ref, g_ref, m_ref, v_ref = ins[4 * a:4 * a + 4]
            gv = g_ref[...]
            nm = ADAM_B1 * m_ref[...] + (1.0 - ADAM_B1) * gv
            nv = ADAM_B2 * v_ref[...] + (1.0 - ADAM_B2) * (gv * gv)
            outs[3 * a][...] = -ADAM_LR * ((nm / c1) / (jnp.sqrt(nv / c2) + ADAM_EPS) + ADAM_WD * w_ref[...])
            outs[3 * a + 1][...] = nm
            outs[3 * a + 2][...] = nv

    in_specs, out_specs, out_shape, args = [], [], [], []
    for w, g, m, v in zip(ws, gs, ms, vs):
        R, C = w.shape
        blk = pl.BlockSpec((R // ADAM_SPLIT, C), lambda i: (i, 0))
        in_specs += [blk] * 4
        out_specs += [blk] * 3
        out_shape += [jax.ShapeDtypeStruct((R, C), F32)] * 3
        args += [w, g, m, v]
    res = _call(body, grid=(ADAM_SPLIT,), in_specs=in_specs, out_specs=out_specs, out_shape=out_shape,
                sem=("arbitrary",), name=name, args=args, phase=phase)
    outs, extra = res if phase is not None else (res, None)
    return list(outs[0::3]), list(outs[1::3]), list(outs[2::3]), extra


TM = 512
TM_WIDE = 1024
TK = 1024
TC = 256


def _local_step(x, tgt, w, overlap=None):
    B, S, D = x.shape
    T = B * S
    x2 = x.reshape(T, D)
    t2 = tgt.reshape(T, D)
    ones = jnp.ones((1, D_ATTN), F32)
    scale = HEAD_DIM ** -0.5
    gains = jnp.stack([jnp.tile(w["q_norm"], (1, HEADS)) * scale, jnp.tile(w["k_norm"], (1, HEADS)), ones])
    g = {}

    def hosting(point, build):
        phase = overlap.phase(point, w, g) if overlap is not None else None
        if phase is None:
            return build(None)
        outs, extra = build(phase)
        overlap.done(point, extra, w, g)
        return outs

    h1, n1, G1, U1 = hosting("ffn1_fwd", lambda ph: _ffn_fwd(
        x2, w["ffn1_norm"], w["wg1"], w["wu1"], w["wd1"], None, tm=TM_WIDE, name="ffn1_fwd", phase=ph))
    u, n2 = hosting("inproj_fwd", lambda ph: _inproj_fwd(h1, w["mix_norm"], w["win"], tm=TM_WIDE, name="inproj_fwd", phase=ph))
    qkv = _qkv_prep(u, gains, B, S, name="qkv_prep")
    qkv = qkv.reshape(3, N_PATTERNS, T // QBLK, QBLK, D_ATTN)
    o3, lse3 = hosting("attn_fwd", lambda ph: _attn_fwd(qkv, name="attn_fwd", phase=ph))
    attn, lse = _attn_combine(o3, lse3, B, S, name="attn_combine")
    cv, glu, yconv = hosting("conv_fwd", lambda ph: _conv_fwd(
        u, w["conv_w"], w["conv_b"], w["conv_ln_g"], w["conv_ln_b"], B, S, tc=TC, name="conv_fwd", phase=ph))
    h2 = _outproj_fwd(h1, attn, cv, w["wout"], tm=TM, name="outproj_fwd")
    dh3, n3, G2, U2, loss = _ffn_fwd(h2, w["ffn2_norm"], w["wg2"], w["wu2"], w["wd2"], t2, tm=TM_WIDE, name="ffn2_fwd")

    dG, dU, A, dy, dh2, g["ffn2_norm"] = _ffn_bwd_act(dh3, h2, w["ffn2_norm"], G2, U2, w["wg2"], w["wu2"], w["wd2"],
                                                    tm=TM, name="ffn2_bwd_act")
    g["wg2"], g["wu2"], g["wd2"] = _ffn_bwd_w(n3, dy, dG, dU, A, tk=TK, name="ffn2_bwd_w")
    dattn, dcv, g["wout"] = _outproj_bwd(dh2, attn, cv, w["wout"], tm=TM, name="outproj_bwd")
    dyc, cpart = _conv_bwd_norm(dcv, yconv, w["conv_ln_g"], w["conv_ln_b"], tc=TC, name="conv_bwd_norm")
    dca, dcg, dcw = hosting("conv_bwd_taps", lambda ph: _conv_bwd_taps(
        dyc, glu, u, w["conv_w"], B, S, tc=TC, name="conv_bwd_taps", phase=ph))
    do3, st3 = _attn_bwd_prep(dattn, attn, lse, B, S, name="attn_bwd_prep")
    nb = T // QBLK
    (cur,) = hosting("attn_bwd", lambda ph: _attn_bwd(
        qkv, do3.reshape(N_PATTERNS, nb, QBLK, D_ATTN), st3.reshape(N_PATTERNS, nb, QBLK, LANES),
        name="attn_bwd", phase=ph))
    du_qkv, dgains = hosting("attn_grad_combine", lambda ph: _attn_grad_combine(
        cur, u, gains, B, S, name="attn_grad_combine", phase=ph))
    du = jnp.concatenate([du_qkv, dca, dcg], axis=1)
    (g["win"],) = hosting("inproj_bwd_w", lambda ph: _inproj_bwd_w(
        n2, du, w["win"].shape[0], tk=TK, name="inproj_bwd_w", phase=ph))
    dh1, g["mix_norm"] = hosting("inproj_bwd_act", lambda ph: _inproj_bwd_act(
        du, dh2, h1, w["mix_norm"], w["win"], tm=TM_WIDE, name="inproj_bwd_act", phase=ph))
    dG, dU, A, dy, dx, g["ffn1_norm"] = hosting("ffn1_bwd_act", lambda ph: _ffn_bwd_act(
        dh1, x2, w["ffn1_norm"], G1, U1, w["wg1"], w["wu1"], w["wd1"], tm=TM, name="ffn1_bwd_act", phase=ph))
    g["wg1"], g["wu1"], g["wd1"] = hosting("ffn1_bwd_w", lambda ph: _ffn_bwd_w(
        n1, dy, dG, dU, A, tk=TK, name="ffn1_bwd_w", phase=ph))

    g["q_norm"] = dgains[0].reshape(HEADS, HEAD_DIM).sum(axis=0, keepdims=True) * scale
    g["k_norm"] = dgains[1].reshape(HEADS, HEAD_DIM).sum(axis=0, keepdims=True)
    g["conv_ln_g"] = cpart[0:1]
    g["conv_ln_b"] = cpart[1:2]
    g["conv_b"] = cpart[2:3]
    g["conv_w"] = dcw[:CONV_K]
    return loss, dx.reshape(B, S, D), g


N_CHIPS = 4
N_DEV = 8
VMEM_SPEC = pl.BlockSpec(memory_space=pltpu.VMEM)


def _remote(src, dst, send_sem, recv_sem, device):
    return pltpu.make_async_remote_copy(src_ref=src, dst_ref=dst, send_sem=send_sem, recv_sem=recv_sem,
                                        device_id=device, device_id_type=MESH)


def _stage_shards(shards, dtypes, *, name):
    n = len(shards)
    halves = [s.reshape(2, s.shape[0] // 2, s.shape[1]) for s in shards]

    def body(*refs):
        ins, outs, vms, loc_sems = refs[:n], refs[n:2 * n], refs[2 * n:3 * n], refs[3 * n]
        me = 2 * lax.axis_index("x") + lax.axis_index("y")
        copies = []
        for a in range(n):
            vms[a][...] = ins[a][...].astype(dtypes[a])
            cp = pltpu.make_async_copy(vms[a], outs[a].at[me], loc_sems.at[a])
            cp.start()
            copies.append(cp)
        for cp in copies:
            cp.wait()

    return pl.pallas_call(
        body, in_specs=[VMEM_SPEC] * n, out_specs=[ANY] * n,
        out_shape=[jax.ShapeDtypeStruct((N_CHIPS,) + h.shape, dt) for h, dt in zip(halves, dtypes)],
        scratch_shapes=[pltpu.VMEM(h.shape, dt) for h, dt in zip(halves, dtypes)] + [DMA_SEMS((n,))],
        compiler_params=pltpu.CompilerParams(vmem_limit_bytes=VMEM_LIMIT), name=name)(*halves)


def _like(arrays):
    return [jax.ShapeDtypeStruct(a.shape, a.dtype) for a in arrays]


def _axes():
    x, y, c = lax.axis_index("x"), lax.axis_index("y"), lax.axis_index("c")
    first = (x + (1 - c) * (1 - 2 * x), y + c * (1 - 2 * y))
    second = (x + c * (1 - 2 * x), y + (1 - c) * (1 - 2 * y))
    slots = tuple(2 * px + py for px, py in ((x, y), first, second, (1 - x, 1 - y)))
    return (x, y, c), (*first, c), (*second, c), slots


def _gather_ici_phase(bufs, only=None):
    n = len(bufs)

    def stage1(ins, outs, sems):
        (x, y, c), peer1, peer2, (own, s1, s2, both) = _axes()
        starts, arrivals = [], []
        for a in range(n):
            mine, land = outs[a].at[own, c], outs[a].at[s2, c]
            starts.append(_remote(mine, mine, *sems(a), peer2))
            arrivals.append(_remote(land, land, *sems(a), peer2))
        return starts, arrivals

    def stage2(ins, outs, sems):
        (x, y, c), peer1, peer2, (own, s1, s2, both) = _axes()
        starts, arrivals = [], []
        for a in range(n):
            for k, (src, dst) in enumerate(((own, s1), (s2, both))):
                mine, land = outs[a].at[src, c], outs[a].at[dst, c]
                starts.append(_remote(mine, mine, *sems(2 * a + k), peer1))
                arrivals.append(_remote(land, land, *sems(2 * a + k), peer1))
        return starts, arrivals

    same = {a: a for a in range(n)}
    first, second = _Phase(bufs, _like(bufs), same, n, stage1), _Phase(bufs, _like(bufs), same, 2 * n, stage2)
    if only is None:
        return first.then(second)
    return first if only == 1 else second


def _gather_d2d_phase(bufs):
    n = len(bufs)

    def copies(ins, outs, sems):
        (x, y, c), peer1, peer2, (own, s1, s2, both) = _axes()
        starts, arrivals = [], []
        for a in range(n):
            for j, s in enumerate((s1, s2, both)):
                got, land = outs[a].at[s, c], outs[a].at[s, 1 - c]
                starts.append(_remote(got, got, *sems(3 * a + j), (x, y, 1 - c)))
                arrivals.append(_remote(land, land, *sems(3 * a + j), (x, y, 1 - c)))
        return starts, arrivals

    return _Phase(bufs, _like(bufs), {a: a for a in range(n)}, 3 * n, copies)


def _exchange_phase(views):
    n = len(views)

    def copies(ins, outs, sems):
        x, y, c = lax.axis_index("x"), lax.axis_index("y"), lax.axis_index("c")
        starts = [_remote(ins[a].at[pl.ds(0, ins[a].shape[0]), 1 - c], outs[a], *sems(a), (x, y, 1 - c))
                  for a in range(n)]
        return starts, starts

    outs = [jax.ShapeDtypeStruct((v.shape[0],) + v.shape[2:], F32) for v in views]
    return _Phase(views, outs, {}, n, copies)


ADD_SPLIT = 2


def _add_halves(views, got, sel, tag):
    n = len(views)

    def body(s_ref, *refs):
        ins, outs = refs[:4 * n], refs[4 * n:]
        for a in range(n):
            gk, rk, gs, rs = ins[4 * a:4 * a + 4]
            outs[2 * a][...] = gk[...] + rk[...]
            outs[2 * a + 1][...] = (gs[...] + rs[...]).astype(BF16)

    in_specs, out_specs, out_shape, args = [], [], [], []
    for g, r in zip(views, got):
        _, _, rh, cdim = g.shape
        tr = rh // ADD_SPLIT
        for off in (0, 2):
            in_specs.append(pl.BlockSpec((None, None, tr, cdim), lambda k, i, s, off=off: (s[1 + off + k], s[0], i, 0)))
            in_specs.append(pl.BlockSpec((None, tr, cdim), lambda k, i, s, off=off: (s[1 + off + k], i, 0)))
            args += [g, r]
        out_specs += [pl.BlockSpec((None, tr, cdim), lambda k, i, s: (k, i, 0))] * 2
        out_shape += [jax.ShapeDtypeStruct((2, rh, cdim), F32), jax.ShapeDtypeStruct((2, rh, cdim), BF16)]
    res = pl.pallas_call(
        body,
        grid_spec=pltpu.PrefetchScalarGridSpec(num_scalar_prefetch=1, grid=(2, ADD_SPLIT), in_specs=in_specs,
                                               out_specs=out_specs),
        out_shape=out_shape, compiler_params=_params("arbitrary", "arbitrary"), name=f"rs_add_half_{tag}")(sel, *args)
    return list(res[0::2]), list(res[1::2])


def _swap_phase(arrays, stage):
    n = len(arrays)

    def copies(ins, outs, sems):
        peer = _axes()[stage]
        starts = [_remote(ins[a], outs[a], *sems(a), peer) for a in range(n)]
        return starts, starts

    return _Phase(arrays, _like(arrays), {}, n, copies)


def _add_first(keep, got, tag):
    n = len(keep)

    def body(*refs):
        ins, outs = refs[:2 * n], refs[2 * n:]
        for a in range(n):
            k_ref, g_ref = ins[2 * a], ins[2 * a + 1]
            outs[2 * a][...] = k_ref[0] + g_ref[0].astype(F32)
            outs[2 * a + 1][...] = (k_ref[1] + g_ref[1].astype(F32)).astype(BF16)

    in_specs, out_specs, out_shape, args = [], [], [], []
    for k, g in zip(keep, got):
        _, rh, cdim = k.shape
        tr = rh // ADD_SPLIT
        in_specs += [pl.BlockSpec((2, tr, cdim), lambda i: (0, i, 0))] * 2
        out_specs += [pl.BlockSpec((tr, cdim), lambda i: (i, 0))] * 2
        out_shape += [jax.ShapeDtypeStruct((rh, cdim), F32), jax.ShapeDtypeStruct((rh, cdim), BF16)]
        args += [k, g]
    res = pl.pallas_call(body, grid=(ADD_SPLIT,), in_specs=in_specs, out_specs=out_specs, out_shape=out_shape,
                         compiler_params=_params("arbitrary"), name=f"rs_add_first_{tag}")(*args)
    return list(res[0::2]), list(res[1::2])


def _add_second(keep, got, sel, tag):
    n = len(keep)

    def body(s_ref, *refs):
        ins, outs = refs[:2 * n], refs[2 * n:]
        for a in range(n):
            outs[a][...] = ins[2 * a][...] + ins[2 * a + 1][...].astype(F32)

    in_specs, out_specs, out_shape, args = [], [], [], []
    for k, g in zip(keep, got):
        rh, cdim = k.shape
        tr = rh // ADD_SPLIT
        in_specs += [pl.BlockSpec((tr, cdim), lambda i, s: (i, 0))] * 2
        out_specs.append(pl.BlockSpec((None, tr, cdim), lambda i, s: (s[0], i, 0)))
        out_shape.append(jax.ShapeDtypeStruct((2, rh, cdim), F32))
        args += [k, g]
    res = pl.pallas_call(
        body,
        grid_spec=pltpu.PrefetchScalarGridSpec(num_scalar_prefetch=1, grid=(ADD_SPLIT,), in_specs=in_specs,
                                               out_specs=out_specs),
        out_shape=out_shape, compiler_params=_params("arbitrary"), name=f"rs_add_second_{tag}")(sel, *args)
    return list(res)


def _join_phase(halves):
    n = len(halves)

    def copies(ins, outs, sems):
        x, y, c = lax.axis_index("x"), lax.axis_index("y"), lax.axis_index("c")
        starts, arrivals = [], []
        for a in range(n):
            mine, land = outs[a].at[c], outs[a].at[1 - c]
            starts.append(_remote(mine, mine, *sems(a), (x, y, 1 - c)))
            arrivals.append(_remote(land, land, *sems(a), (x, y, 1 - c)))
        return starts, arrivals

    return _Phase(halves, _like(halves), {a: a for a in range(n)}, n, copies)


def _slot_order():
    x, y, c = lax.axis_index("x"), lax.axis_index("y"), lax.axis_index("c")
    own, flip_x, flip_y, both = 2 * x + y, 2 * (1 - x) + y, 2 * x + 1 - y, 2 * (1 - x) + 1 - y
    first = jnp.where(c == 0, flip_x, flip_y)
    second = jnp.where(c == 0, flip_y, flip_x)
    return jnp.stack([c, own, second, first, both]).astype(jnp.int32)


def _reduce_scatter(views, got, join_also=()):
    sel = _slot_order()
    keep, send = _add_halves(views, got, sel, "late")
    got = _run_phase(_swap_phase(send, 1), name="rs_swap_first_axis")
    keep, send = _add_first(keep, got, "late")
    got = _run_phase(_swap_phase(send, 2), name="rs_swap_second_axis")
    halves = _add_second(keep, got, sel, "late")
    full = _run_phase(_join_phase(halves + list(join_also)), name="rs_join_halves")
    return [f.reshape(-1, f.shape[-1]) for f in full]


def _half_view(g):
    return g.reshape(N_CHIPS, 2, g.shape[1] // 2, g.shape[2])


EARLY_GRADS = ("wg2", "wu2", "wd2", "wout")
MIDDLE_GRADS = ("win",)


class _Overlap:
    EARLY_AT = ("conv_bwd_taps", "attn_bwd", "attn_grad_combine", "inproj_bwd_w")
    MIDDLE_AT = ("inproj_bwd_act", "ffn1_bwd_act", "ffn1_bwd_w", None)

    def __init__(self, staged):
        self.staged = staged
        self.ffn2 = list(staged[3:])
        sel = _slot_order()
        self.early = _Reduction(EARLY_GRADS, "early", sel)
        self.middle = _Reduction(MIDDLE_GRADS, "middle", sel)

    def phase(self, point, w, g):
        if point == "ffn1_fwd":
            return _gather_ici_phase(self.staged[:3])
        if point == "inproj_fwd":
            return _gather_ici_phase(self.ffn2, only=1)
        if point == "attn_fwd":
            return _gather_ici_phase(self.ffn2, only=2)
        if point == "conv_fwd":
            return _gather_d2d_phase(self.ffn2)
        for red, at in ((self.early, self.EARLY_AT), (self.middle, self.MIDDLE_AT)):
            if point in at:
                return red.phase(at.index(point), g)
        return None

    def done(self, point, outs, w, g):
        if point == "ffn1_fwd":
            win, wout, taps = [_whole(b) for b in _run_phase(_gather_d2d_phase(outs), name="gather_mix_d2d")]
            w["win"] = win
            w["wout"] = wout.reshape(-1, wout.shape[-1])
            w["conv_w"] = taps.transpose(1, 0, 2).reshape(CONV_K + 1, D_CONV)[:CONV_K]
        elif point in ("inproj_fwd", "attn_fwd"):
            self.ffn2 = list(outs)
        elif point == "conv_fwd":
            w["wg2"], w["wu2"], w["wd2"] = [_whole(b) for b in outs]
        for red, at in ((self.early, self.EARLY_AT), (self.middle, self.MIDDLE_AT)):
            if point in at:
                red.done(at.index(point), outs)


class _Reduction:
    def __init__(self, names, tag, sel):
        self.names, self.tag, self.sel = names, tag, sel
        self.reduced = {}

    def phase(self, stage, g):
        if stage == 0:
            self.cols = [g[k].shape[-1] for k in self.names]
            self.views = [_half_view(g[k].reshape(N_CHIPS, -1, g[k].shape[-1])) for k in self.names]
            return _exchange_phase(self.views)
        if stage in (1, 2):
            return _swap_phase(self.send, stage)
        return _join_phase(self.halves)

    def done(self, stage, outs):
        if stage == 0:
            self.keep, self.send = _add_halves(self.views, outs, self.sel, self.tag)
        elif stage == 1:
            self.keep, self.send = _add_first(self.keep, outs, self.tag)
        elif stage == 2:
            self.halves = _add_second(self.keep, outs, self.sel, self.tag)
        else:
            for k, c, f in zip(self.names, self.cols, outs):
                self.reduced[k] = f.reshape(-1, c)


def _whole(buf):
    return buf.reshape(buf.shape[0], 2 * buf.shape[2], buf.shape[3])


def _allreduce_small(pack, *, name):
    rows = pack.shape[0]

    def body(p_ref, o_ref, buf_ref, send_sems, recv_sems):
        x, y, c = lax.axis_index("x"), lax.axis_index("y"), lax.axis_index("c")
        me = 4 * x + 2 * y + c
        buf_ref[me] = p_ref[...]
        cps = []
        for k in range(1, N_DEV):
            peer = tuple(1 - v if (k >> s) & 1 else v for v, s in ((x, 2), (y, 1), (c, 0)))
            cp = _remote(p_ref, buf_ref.at[me], send_sems.at[k - 1], recv_sems.at[k - 1], peer)
            cp.start()
            cps.append(cp)
        for k in range(1, N_DEV):
            src = 4 * (x ^ ((k >> 2) & 1)) + 2 * (y ^ ((k >> 1) & 1)) + (c ^ (k & 1))
            land = buf_ref.at[src]
            _remote(land, land, send_sems.at[k - 1], recv_sems.at[k - 1], (x, y, c)).wait_recv()
        acc = buf_ref[0]
        for d in range(1, N_DEV):
            acc = acc + buf_ref[d]
        o_ref[...] = acc
        for cp in cps:
            cp.wait_send()

    return pl.pallas_call(
        body, in_specs=[VMEM_SPEC], out_specs=VMEM_SPEC, out_shape=jax.ShapeDtypeStruct(pack.shape, F32),
        scratch_shapes=[pltpu.VMEM((N_DEV, rows, LANES), F32), pltpu.SemaphoreType.DMA((N_DEV - 1,)),
                        pltpu.SemaphoreType.DMA((N_DEV - 1,))], name=name)(pack)


SMALL = ("ffn1_norm", "mix_norm", "q_norm", "k_norm", "conv_b", "conv_ln_g", "conv_ln_b", "ffn2_norm", "conv_w")
BIG = ("ffn1_w_gate", "ffn1_w_up", "ffn1_w_down", "w_in", "w_out", "ffn2_w_gate", "ffn2_w_up", "ffn2_w_down")
TRANSPOSED = ("ffn1_w_gate", "ffn1_w_up", "ffn2_w_gate", "ffn2_w_up")
WEIGHTS = ("ffn1_norm", "ffn1_w_gate", "ffn1_w_up", "ffn1_w_down", "mix_norm", "w_in", "q_norm", "k_norm",
           "conv_w", "conv_b", "conv_ln_g", "conv_ln_b", "w_out", "ffn2_norm", "ffn2_w_gate", "ffn2_w_up",
           "ffn2_w_down")


def _pack(parts):
    rows = []
    for p in parts:
        flat = p.reshape(-1)
        tile = SUBLANES * LANES
        padded = -(-flat.shape[0] // tile) * tile
        rows.append(jnp.pad(flat, (0, padded - flat.shape[0])).reshape(-1, LANES))
    return jnp.concatenate(rows, axis=0)


def _unpack(pack, shapes):
    out, row = [], 0
    for shp in shapes:
        size = shp[0] * shp[1]
        tile = SUBLANES * LANES
        nrows = -(-size // tile) * SUBLANES
        out.append(pack[row:row + nrows].reshape(-1)[:size].reshape(shp))
        row += nrows
    return out


def kernel(x, ffn1_norm, ffn1_w_gate, ffn1_w_up, ffn1_w_down, mix_norm, w_in, q_norm, k_norm, conv_w, conv_b, conv_ln_g, conv_ln_b, w_out, ffn2_norm, ffn2_w_gate, ffn2_w_up, ffn2_w_down, loss_target, m_ffn1_norm, m_ffn1_w_gate, m_ffn1_w_up, m_ffn1_w_down, m_mix_norm, m_w_in, m_q_norm, m_k_norm, m_conv_w, m_conv_b, m_conv_ln_g, m_conv_ln_b, m_w_out, m_ffn2_norm, m_ffn2_w_gate, m_ffn2_w_up, m_ffn2_w_down, v_ffn1_norm, v_ffn1_w_gate, v_ffn1_w_up, v_ffn1_w_down, v_mix_norm, v_w_in, v_q_norm, v_k_norm, v_conv_w, v_conv_b, v_conv_ln_g, v_conv_ln_b, v_w_out, v_ffn2_norm, v_ffn2_w_gate, v_ffn2_w_up, v_ffn2_w_down):
    wts = dict(ffn1_norm=ffn1_norm, ffn1_w_gate=ffn1_w_gate[0], ffn1_w_up=ffn1_w_up[0], ffn1_w_down=ffn1_w_down[0],
               mix_norm=mix_norm, w_in=w_in[0], q_norm=q_norm, k_norm=k_norm, conv_w=conv_w[0], conv_b=conv_b,
               conv_ln_g=conv_ln_g, conv_ln_b=conv_ln_b, w_out=w_out[0], ffn2_norm=ffn2_norm,
               ffn2_w_gate=ffn2_w_gate[0], ffn2_w_up=ffn2_w_up[0], ffn2_w_down=ffn2_w_down[0])
    mom = dict(ffn1_norm=m_ffn1_norm, ffn1_w_gate=m_ffn1_w_gate[0], ffn1_w_up=m_ffn1_w_up[0], ffn1_w_down=m_ffn1_w_down[0],
               mix_norm=m_mix_norm, w_in=m_w_in[0], q_norm=m_q_norm, k_norm=m_k_norm, conv_w=m_conv_w[0], conv_b=m_conv_b,
               conv_ln_g=m_conv_ln_g, conv_ln_b=m_conv_ln_b, w_out=m_w_out[0], ffn2_norm=m_ffn2_norm,
               ffn2_w_gate=m_ffn2_w_gate[0], ffn2_w_up=m_ffn2_w_up[0], ffn2_w_down=m_ffn2_w_down[0])
    var = dict(ffn1_norm=v_ffn1_norm, ffn1_w_gate=v_ffn1_w_gate[0], ffn1_w_up=v_ffn1_w_up[0], ffn1_w_down=v_ffn1_w_down[0],
               mix_norm=v_mix_norm, w_in=v_w_in[0], q_norm=v_q_norm, k_norm=v_k_norm, conv_w=v_conv_w[0], conv_b=v_conv_b,
               conv_ln_g=v_conv_ln_g, conv_ln_b=v_conv_ln_b, w_out=v_w_out[0], ffn2_norm=v_ffn2_norm,
               ffn2_w_gate=v_ffn2_w_gate[0], ffn2_w_up=v_ffn2_w_up[0], ffn2_w_down=v_ffn2_w_down[0])
    chip = 2 * lax.axis_index("x") + lax.axis_index("y")
    for src in (wts, mom, var):
        for n in TRANSPOSED:
            src[n] = src[n].T

    taps = jnp.pad(wts["conv_w"], ((0, 1), (0, 0)))
    staged = _stage_shards([wts["ffn1_w_gate"], wts["ffn1_w_up"], wts["ffn1_w_down"], wts["w_in"], wts["w_out"], taps,
                            wts["ffn2_w_gate"], wts["ffn2_w_up"], wts["ffn2_w_down"]],
                           [BF16, BF16, BF16, BF16, BF16, F32, BF16, BF16, BF16], name="stage_shards")
    first = _run_phase(_gather_ici_phase(staged[:3]).then(_gather_d2d_phase(staged[:3])), name="gather_ffn1")
    wg1, wu1, wd1 = [_whole(b) for b in first]
    w = dict(ffn1_norm=ffn1_norm, mix_norm=mix_norm, ffn2_norm=ffn2_norm, q_norm=q_norm, k_norm=k_norm,
             conv_b=conv_b, conv_ln_g=conv_ln_g, conv_ln_b=conv_ln_b, wg1=wg1, wu1=wu1, wd1=wd1)
    overlap = _Overlap(staged[3:])
    loss_part, grad_x, g = _local_step(x, loss_target, w, overlap)

    grads, delta, new_m, new_v = {}, {}, {}, {}
    early = overlap.early.reduced
    grads.update(ffn2_w_gate=early["wg2"], ffn2_w_up=early["wu2"], ffn2_w_down=early["wd2"], w_out=early["wout"])
    late_views = [_half_view(g[k]) for k in ("wg1", "wu1", "wd1")]
    got = _run_phase(_exchange_phase(late_views), name="rs_exchange_halves")
    late = _reduce_scatter(late_views, got, join_also=overlap.middle.halves)
    grads.update(zip(("ffn1_w_gate", "ffn1_w_up", "ffn1_w_down", "w_in"), late))

    small_shapes = [g[n].shape for n in SMALL] + [(SUBLANES, LANES)]
    red = _allreduce_small(_pack([g[n] for n in SMALL] + [loss_part]), name="allreduce_small")
    small = dict(zip(SMALL + ("loss",), _unpack(red, small_shapes)))
    loss = small["loss"][0, 0]
    small["conv_w"] = lax.dynamic_slice_in_dim(small["conv_w"], chip * LANES, LANES, axis=1)

    for tag, names in (("early", ("ffn2_w_gate", "ffn2_w_up", "ffn2_w_down", "w_out")),
                       ("late", ("ffn1_w_gate", "ffn1_w_up", "ffn1_w_down", "w_in"))):
        d, m, v, _ = _adamw_many([wts[n] for n in names], [grads[n] for n in names], [mom[n] for n in names],
                                 [var[n] for n in names], name=f"adamw_{tag}")
        for dst, vals in ((delta, d), (new_m, m), (new_v, v)):
            dst.update(zip(names, vals))
    shapes = [wts[n].shape for n in SMALL]
    packs = [_pack([src[n] for n in SMALL]) for src in (wts, small, mom, var)]
    outs = _adamw(*packs, name="adamw_small")
    for dst, pk in zip((delta, new_m, new_v), outs):
        dst.update(zip(SMALL, _unpack(pk, shapes)))
    for n in SMALL:
        grads[n] = small[n]

    def shaped(d, n):
        v = d[n].T if n in TRANSPOSED else d[n]
        return v.reshape((1,) + v.shape) if n in BIG or n == "conv_w" else v

    return (loss, grad_x, *[shaped(grads, n) for n in WEIGHTS], *[shaped(delta, n) for n in WEIGHTS],
            *[shaped(new_m, n) for n in WEIGHTS], *[shaped(new_v, n) for n in WEIGHTS])
```

```python
import functools

import jax
import jax.numpy as jnp
from jax import lax
from jax.experimental import pallas as pl
from jax.experimental.pallas import tpu as pltpu

F32 = jnp.float32
BF16 = jnp.bfloat16

EPS = 1e-6
HEADS = 8
HEAD_DIM = 64
D_ATTN = HEADS * HEAD_DIM
D_CONV = 512
CONV_K = 31
QBLK = 128
N_PATTERNS = 3
DILATIONS = (1, 4, 16)
LANES = 128
NEG = -1e30

ADAM_LR = 0.001
ADAM_B1 = 0.9
ADAM_B2 = 0.999
ADAM_EPS = 1e-08
ADAM_WD = 0.01
ADAM_STEP = 10

VMEM_LIMIT = 56 * 1024 * 1024
MESH = pl.DeviceIdType.MESH

NT_DIMS = (((1,), (1,)), ((), ()))
TN_DIMS = (((0,), (0,)), ((), ()))


def _params(*sem):
    return pltpu.CompilerParams(dimension_semantics=sem, vmem_limit_bytes=VMEM_LIMIT)


def _dot(a, b):
    return jnp.dot(a, b, preferred_element_type=F32)


def _dot_nt(a, b):
    return lax.dot_general(a, b, NT_DIMS, preferred_element_type=F32)


def _dot_tn(a, b):
    return lax.dot_general(a, b, TN_DIMS, preferred_element_type=F32)


def _sigmoid(x):
    return 1.0 / (1.0 + jnp.exp(-x))


def _seg_mean(v, e_ref, width):
    hi = v.astype(BF16)
    lo = (v - hi.astype(F32)).astype(BF16)
    e = e_ref[...]
    return (_dot(hi, e) + _dot(lo, e)) * (1.0 / width)


def _seg_matrix(n):
    i = jnp.arange(n)
    return (i[:, None] // HEAD_DIM == i[None, :] // HEAD_DIM).astype(BF16)


ANY = pl.BlockSpec(memory_space=pl.ANY)
DMA_SEMS = pltpu.SemaphoreType.DMA


class _Phase:
    def __init__(self, ins, outs, aliases, nsem, copies):
        self.ins, self.outs, self.aliases = list(ins), list(outs), dict(aliases)
        self.stages = [(nsem, copies)]

    def then(self, other):
        self.stages = self.stages + other.stages
        return self

    @property
    def nsem(self):
        return sum(n for n, _ in self.stages)

    def _copies(self, k, in_refs, out_refs, send_sems, recv_sems):
        base = sum(n for n, _ in self.stages[:k])
        return self.stages[k][1](in_refs, out_refs, lambda i: (send_sems.at[base + i], recv_sems.at[base + i]))

    def start(self, k, *refs):
        for cp in self._copies(k, *refs)[0]:
            cp.start()

    def finish(self, k, *refs):
        starts, arrivals = self._copies(k, *refs)
        for cp in arrivals:
            cp.wait_recv()
        for cp in starts:
            cp.wait_send()


def _run_phase(phase, *, name):
    n_in, n_out = len(phase.ins), len(phase.outs)

    def body(*refs):
        ins, outs = refs[:n_in], refs[n_in:n_in + n_out]
        send_sems, recv_sems = refs[n_in + n_out:]
        for k in range(len(phase.stages)):
            phase.start(k, ins, outs, send_sems, recv_sems)
            phase.finish(k, ins, outs, send_sems, recv_sems)

    return pl.pallas_call(
        body, in_specs=[ANY] * n_in, out_specs=[ANY] * n_out, out_shape=phase.outs,
        input_output_aliases=phase.aliases,
        scratch_shapes=[DMA_SEMS((phase.nsem,)), DMA_SEMS((phase.nsem,))], name=name)(*phase.ins)


def _call(body, *, grid, in_specs, out_specs, out_shape, scratch_shapes=(), sem, name, args, phase=None):
    in_specs, out_specs, out_shape = list(in_specs), list(out_specs), list(out_shape)
    scratch_shapes = list(scratch_shapes)
    if phase is None:
        return pl.pallas_call(body, grid=grid, in_specs=in_specs, out_specs=out_specs, out_shape=out_shape,
                              scratch_shapes=scratch_shapes, compiler_params=_params(*sem), name=name)(*args)
    n_in, n_out, n_scr = len(in_specs), len(out_specs), len(scratch_shapes)
    p_in, p_out = len(phase.ins), len(phase.outs)

    def hosted(*refs):
        ins, pins = refs[:n_in], refs[n_in:n_in + p_in]
        o0 = n_in + p_in
        outs, pouts = refs[o0:o0 + n_out], refs[o0 + n_out:o0 + n_out + p_out]
        s0 = o0 + n_out + p_out
        scr = refs[s0:s0 + n_scr]
        send_sems, recv_sems = refs[s0 + n_scr:]
        step = 0
        for d, n in enumerate(grid):
            step = step * n + pl.program_id(d)
        nsteps = functools.reduce(lambda a, b: a * b, grid)
        nstages = len(phase.stages)
        comm_refs = (pins, pouts, send_sems, recv_sems)

        for k in range(nstages):
            @pl.when(step == (k * nsteps) // nstages)
            def _(k=k):
                if k > 0:
                    phase.finish(k - 1, *comm_refs)
                phase.start(k, *comm_refs)

        body(*ins, *outs, *scr)

        @pl.when(step == nsteps - 1)
        def _():
            phase.finish(nstages - 1, *comm_refs)

    res = pl.pallas_call(
        hosted, grid=grid, in_specs=in_specs + [ANY] * p_in, out_specs=out_specs + [ANY] * p_out,
        out_shape=out_shape + phase.outs,
        input_output_aliases={n_in + i: n_out + o for i, o in phase.aliases.items()},
        scratch_shapes=scratch_shapes + [DMA_SEMS((phase.nsem,)), DMA_SEMS((phase.nsem,))],
        compiler_params=_params(*sem), name=name)(*args, *phase.ins)
    return res[:n_out], res[n_out:]


ROW_CHUNK = 256


def _ffn_fwd(x, gain, wg, wu, wd, tgt, *, tm, name, phase=None):
    T, D = x.shape
    NS, Fs, _ = wg.shape
    with_loss = tgt is not None

    def body(*refs):
        if with_loss:
            x_ref, g_ref, wg_ref, wu_ref, wd_ref, t_ref, h_ref, n_ref, G_ref, U_ref, loss_ref, acc_ref = refs
        else:
            x_ref, g_ref, wg_ref, wu_ref, wd_ref, h_ref, n_ref, G_ref, U_ref, acc_ref = refs
        i = pl.program_id(0)
        j = pl.program_id(1)

        @pl.when(j == 0)
        def _():
            xv = x_ref[...]
            r = lax.rsqrt(jnp.mean(xv * xv, axis=-1, keepdims=True) + EPS)
            n_ref[...] = (xv * r * g_ref[...]).astype(BF16)
            acc_ref[...] = jnp.zeros_like(acc_ref)

        n = n_ref[...]
        G = _dot_nt(n, wg_ref[...])
        U = _dot_nt(n, wu_ref[...])
        G_ref[...] = G.astype(BF16)
        U_ref[...] = U.astype(BF16)
        A = (G * _sigmoid(G) * U).astype(BF16)
        acc_ref[...] += _dot(A, wd_ref[...])

        @pl.when(j == NS - 1)
        def _():
            h = x_ref[...] + 0.5 * acc_ref[...]
            if with_loss:
                e = h - t_ref[...]
                h_ref[...] = e * (1.0 / D)

                @pl.when(i == 0)
                def _():
                    loss_ref[...] = jnp.zeros_like(loss_ref)

                loss_ref[...] += jnp.sum(e * e) * (0.5 / D)
            else:
                h_ref[...] = h

    tok = pl.BlockSpec((tm, D), lambda i, j: (i, 0))
    in_specs = [tok, pl.BlockSpec((1, D), lambda i, j: (0, 0)),
                pl.BlockSpec((None, Fs, D), lambda i, j: (j, 0, 0)),
                pl.BlockSpec((None, Fs, D), lambda i, j: (j, 0, 0)),
                pl.BlockSpec((None, Fs, D), lambda i, j: (j, 0, 0))]
    args = [x, gain, wg, wu, wd]
    act = pl.BlockSpec((None, tm, Fs), lambda i, j: (j, i, 0))
    out_shape = [jax.ShapeDtypeStruct((T, D), F32), jax.ShapeDtypeStruct((T, D), BF16),
                 jax.ShapeDtypeStruct((NS, T, Fs), BF16), jax.ShapeDtypeStruct((NS, T, Fs), BF16)]
    out_specs = [tok, tok, act, act]
    if with_loss:
        in_specs.append(tok)
        args.append(tgt)
        out_shape.append(jax.ShapeDtypeStruct((8, LANES), F32))
        out_specs.append(pl.BlockSpec((8, LANES), lambda i, j: (0, 0)))
    return _call(body, grid=(T // tm, NS), in_specs=in_specs, out_specs=out_specs, out_shape=out_shape,
                 scratch_shapes=[pltpu.VMEM((tm, D), F32)], sem=("arbitrary", "arbitrary"), name=name,
                 args=args, phase=phase)


def _rms_bwd(xv, gain, dn):
    r = lax.rsqrt(jnp.mean(xv * xv, axis=-1, keepdims=True) + EPS)
    xhat = xv * r
    dxh = dn * gain
    dx = r * (dxh - xhat * jnp.mean(dxh * xhat, axis=-1, keepdims=True))
    dg = jnp.sum(dn * xhat, axis=0, keepdims=True)
    return dx, dg


def _ffn_bwd_act(dh, x, gain, G, U, wg, wu, wd, *, tm, name, phase=None):
    T, D = x.shape
    NS, Fs, _ = wg.shape

    def body(dh_ref, x_ref, g_ref, G_ref, U_ref, wg_ref, wu_ref, wd_ref,
             dG_ref, dU_ref, A_ref, dy_ref, dx_ref, dg_ref, acc_ref):
        i = pl.program_id(0)
        j = pl.program_id(1)

        @pl.when(j == 0)
        def _():
            dy_ref[...] = (0.5 * dh_ref[...]).astype(BF16)
            acc_ref[...] = jnp.zeros_like(acc_ref)

        @pl.when((i == 0) & (j == 0))
        def _():
            dg_ref[...] = jnp.zeros_like(dg_ref)

        nchunks = tm // ROW_CHUNK
        dA, dGU = {}, {}
        for step in range(nchunks + 2):
            if step < nchunks:
                rows = slice(step * ROW_CHUNK, (step + 1) * ROW_CHUNK)
                dA[step] = _dot_nt(dy_ref[rows, :], wd_ref[...])
            if 1 <= step <= nchunks:
                k = step - 1
                rows = slice(k * ROW_CHUNK, (k + 1) * ROW_CHUNK)
                Gv = G_ref[rows, :].astype(F32)
                Uv = U_ref[rows, :].astype(F32)
                sig = _sigmoid(Gv)
                s = Gv * sig
                dG = (dA[k] * Uv * (sig * (1.0 + Gv * (1.0 - sig)))).astype(BF16)
                dU = (dA.pop(k) * s).astype(BF16)
                dG_ref[rows, :] = dG
                dU_ref[rows, :] = dU
                A_ref[rows, :] = (s * Uv).astype(BF16)
                dGU[k] = (dG, dU)
            if 2 <= step:
                k = step - 2
                rows = slice(k * ROW_CHUNK, (k + 1) * ROW_CHUNK)
                dG, dU = dGU.pop(k)
                acc_ref[rows, :] += _dot(dG, wg_ref[...]) + _dot(dU, wu_ref[...])

        @pl.when(j == NS - 1)
        def _():
            dx, dg = _rms_bwd(x_ref[...], g_ref[...], acc_ref[...])
            dx_ref[...] = dh_ref[...] + dx
            dg_ref[...] += dg

    tok = pl.BlockSpec((tm, D), lambda i, j: (i, 0))
    act = pl.BlockSpec((None, tm, Fs), lambda i, j: (j, i, 0))
    vec = pl.BlockSpec((1, D), lambda i, j: (0, 0))
    return _call(
        body, grid=(T // tm, NS),
        in_specs=[tok, tok, vec, act, act,
                  pl.BlockSpec((None, Fs, D), lambda i, j: (j, 0, 0)),
                  pl.BlockSpec((None, Fs, D), lambda i, j: (j, 0, 0)),
                  pl.BlockSpec((None, Fs, D), lambda i, j: (j, 0, 0))],
        out_specs=[act, act, act, tok, tok, vec],
        out_shape=[jax.ShapeDtypeStruct((NS, T, Fs), BF16)] * 3
        + [jax.ShapeDtypeStruct((T, D), BF16), jax.ShapeDtypeStruct((T, D), F32),
           jax.ShapeDtypeStruct((1, D), F32)],
        scratch_shapes=[pltpu.VMEM((tm, D), F32)],
        sem=("arbitrary", "arbitrary"), name=name, args=(dh, x, gain, G, U, wg, wu, wd), phase=phase)


def _ffn_bwd_w(n, dy, dG, dU, A, *, tk, name, phase=None):
    T, D = n.shape
    NS, _, Fs = dG.shape

    def body(n_ref, dy_ref, dG_ref, dU_ref, A_ref, wg_ref, wu_ref, wd_ref):
        @pl.when(pl.program_id(1) == 0)
        def _():
            wg_ref[...] = jnp.zeros_like(wg_ref)
            wu_ref[...] = jnp.zeros_like(wu_ref)
            wd_ref[...] = jnp.zeros_like(wd_ref)

        nv = n_ref[...]
        wg_ref[...] += _dot_tn(dG_ref[...], nv)
        wu_ref[...] += _dot_tn(dU_ref[...], nv)
        wd_ref[...] += _dot_tn(A_ref[...], dy_ref[...])

    tok = pl.BlockSpec((tk, D), lambda j, k: (k, 0))
    act = pl.BlockSpec((None, tk, Fs), lambda j, k: (j, k, 0))
    return _call(
        body, grid=(NS, T // tk), in_specs=[tok, tok, act, act, act],
        out_specs=[pl.BlockSpec((None, Fs, D), lambda j, k: (j, 0, 0))] * 3,
        out_shape=[jax.ShapeDtypeStruct((NS, Fs, D), F32)] * 3,
        sem=("arbitrary", "arbitrary"), name=name, args=(n, dy, dG, dU, A), phase=phase)


def _inproj_fwd(h, gain, win, *, tm, name, phase=None):
    T, D = h.shape
    NS, _, Cs = win.shape

    def body(h_ref, g_ref, w_ref, u_ref, n_ref):
        @pl.when(pl.program_id(1) == 0)
        def _():
            xv = h_ref[...]
            r = lax.rsqrt(jnp.mean(xv * xv, axis=-1, keepdims=True) + EPS)
            n_ref[...] = (xv * r * g_ref[...]).astype(BF16)

        u_ref[...] = _dot(n_ref[...], w_ref[...])

    tok = pl.BlockSpec((tm, D), lambda i, j: (i, 0))
    return _call(
        body, grid=(T // tm, NS),
        in_specs=[tok, pl.BlockSpec((1, D), lambda i, j: (0, 0)),
                  pl.BlockSpec((None, D, Cs), lambda i, j: (j, 0, 0))],
        out_specs=[pl.BlockSpec((tm, Cs), lambda i, j: (i, j)), tok],
        out_shape=[jax.ShapeDtypeStruct((T, NS * Cs), F32), jax.ShapeDtypeStruct((T, D), BF16)],
        sem=("arbitrary", "arbitrary"), name=name, args=(h, gain, win), phase=phase)


def _inproj_bwd_act(du, dh, h, gain, win, *, tm, name, phase=None):
    T, D = h.shape
    NS, _, Cs = win.shape

    def body(du_ref, dh_ref, h_ref, g_ref, w_ref, dx_ref, dg_ref, acc_ref):
        i = pl.program_id(0)
        j = pl.program_id(1)

        @pl.when(j == 0)
        def _():
            acc_ref[...] = jnp.zeros_like(acc_ref)

        @pl.when((i == 0) & (j == 0))
        def _():
            dg_ref[...] = jnp.zeros_like(dg_ref)

        acc_ref[...] += _dot_nt(du_ref[...], w_ref[...])

        @pl.when(j == NS - 1)
        def _():
            dx, dg = _rms_bwd(h_ref[...], g_ref[...], acc_ref[...])
            dx_ref[...] = dh_ref[...] + dx
            dg_ref[...] += dg

    tok = pl.BlockSpec((tm, D), lambda i, j: (i, 0))
    vec = pl.BlockSpec((1, D), lambda i, j: (0, 0))
    return _call(
        body, grid=(T // tm, NS),
        in_specs=[pl.BlockSpec((tm, Cs), lambda i, j: (i, j)), tok, tok, vec,
                  pl.BlockSpec((None, D, Cs), lambda i, j: (j, 0, 0))],
        out_specs=[tok, vec],
        out_shape=[jax.ShapeDtypeStruct((T, D), F32), jax.ShapeDtypeStruct((1, D), F32)],
        scratch_shapes=[pltpu.VMEM((tm, D), F32)],
        sem=("arbitrary", "arbitrary"), name=name, args=(du, dh, h, gain, win), phase=phase)


def _inproj_bwd_w(n, du, ns, *, tk, name, phase=None):
    T, D = n.shape
    Cs = du.shape[1] // ns

    def body(n_ref, du_ref, w_ref):
        @pl.when(pl.program_id(1) == 0)
        def _():
            w_ref[...] = jnp.zeros_like(w_ref)

        w_ref[...] += _dot_tn(n_ref[...], du_ref[...])

    return _call(
        body, grid=(ns, T // tk),
        in_specs=[pl.BlockSpec((tk, D), lambda j, k: (k, 0)), pl.BlockSpec((tk, Cs), lambda j, k: (k, j))],
        out_specs=[pl.BlockSpec((None, D, Cs), lambda j, k: (j, 0, 0))],
        out_shape=[jax.ShapeDtypeStruct((ns, D, Cs), F32)],
        sem=("arbitrary", "arbitrary"), name=name, args=(n, du), phase=phase)


STRIDE = 4


def _permute(src_ref, tmp_ref, put):
    S = src_ref.shape[0]
    L4, L16 = S // STRIDE, S // (STRIDE * STRIDE)
    put(0, 0, src_ref[...])
    for r0 in range(STRIDE):
        v = src_ref[pl.ds(r0, L4, stride=STRIDE), :]
        put(1, r0 * L4, v)
        tmp_ref[r0 * L4:(r0 + 1) * L4, :] = v
    for r0 in range(STRIDE):
        for r1 in range(STRIDE):
            put(2, (r1 * STRIDE + r0) * L16, tmp_ref[pl.ds(r0 * L4 + r1, L16, stride=STRIDE), :])


def _permute_out(src_ref, tmp_ref, out_ref, cast):
    for cc in range(src_ref.shape[0]):
        cols = slice(cc * LANES, (cc + 1) * LANES)

        def put(p, row0, v, cols=cols):
            out_ref[p, row0:row0 + v.shape[0], cols] = v.astype(cast)

        _permute(src_ref.at[cc], tmp_ref, put)


def _unpermute_in(get_block, dst_ref, tmp_ref, p, S):
    L4, L16 = S // STRIDE, S // (STRIDE * STRIDE)
    if p == 0:
        dst_ref[...] = get_block(0, S)
        return
    if p == 1:
        for r0 in range(STRIDE):
            dst_ref[pl.ds(r0, L4, stride=STRIDE), :] = get_block(r0 * L4, L4)
        return
    for r0 in range(STRIDE):
        for r1 in range(STRIDE):
            tmp_ref[pl.ds(r0 * L4 + r1, L16, stride=STRIDE), :] = get_block((r1 * STRIDE + r0) * L16, L16)
    for r0 in range(STRIDE):
        dst_ref[pl.ds(r0, L4, stride=STRIDE), :] = tmp_ref[r0 * L4:(r0 + 1) * L4, :]


def _qkv_prep(u, gains, B, S, *, name):
    emat = _seg_matrix(D_ATTN)

    def body(u_ref, g_ref, e_ref, out_ref, scr_ref, tmp_ref):
        c = pl.program_id(1)
        xv = u_ref[...]
        ms = _seg_mean(xv * xv, e_ref, HEAD_DIM)
        r = jnp.where(c < 2, lax.rsqrt(ms + EPS), 1.0)
        yv = xv * r * g_ref[...]
        for cc in range(4):
            scr_ref[cc] = yv[:, cc * LANES:(cc + 1) * LANES]
        _permute_out(scr_ref, tmp_ref, out_ref, BF16)

    return pl.pallas_call(
        body, grid=(B, 3),
        in_specs=[pl.BlockSpec((S, D_ATTN), lambda b, c: (b, c)),
                  pl.BlockSpec((None, 1, D_ATTN), lambda b, c: (c, 0, 0)),
                  pl.BlockSpec((D_ATTN, D_ATTN), lambda b, c: (0, 0))],
        out_specs=pl.BlockSpec((None, N_PATTERNS, None, S, D_ATTN), lambda b, c: (c, 0, b, 0, 0)),
        out_shape=jax.ShapeDtypeStruct((3, N_PATTERNS, B, S, D_ATTN), BF16),
        scratch_shapes=[pltpu.VMEM((4, S, LANES), F32), pltpu.VMEM((S, LANES), F32)],
        compiler_params=_params("arbitrary", "arbitrary"), name=name)(u, gains, emat)


def _band_mask(p, b):
    nblk = jnp.right_shift(16, 2 * p)
    has_prev = jnp.bitwise_and(b, nblk - 1) != 0
    qi = lax.broadcasted_iota(jnp.int32, (QBLK, 2 * QBLK), 0)
    ci = lax.broadcasted_iota(jnp.int32, (QBLK, 2 * QBLK), 1)
    dist = QBLK + qi - ci
    return (dist >= 0) & (dist <= QBLK) & (has_prev | (ci >= QBLK))


def _first_head(rows):
    return lax.broadcasted_iota(jnp.int32, (rows, LANES), 1) < HEAD_DIM


def _split_heads(pair):
    first = _first_head(pair.shape[0])
    zero = jnp.zeros_like(pair)
    return jnp.concatenate([jnp.where(first, pair, zero), jnp.where(first, zero, pair)], axis=0)


def _merge_heads(col_a, col_b):
    rows = col_a.shape[0]
    return jnp.where(_first_head(rows), jnp.broadcast_to(col_a, (rows, LANES)), jnp.broadcast_to(col_b, (rows, LANES)))


QB_FWD = 8
QB_BWD = 4


def _attn_fwd(qkv, *, name, phase=None):
    QB = QB_FWD
    nb = qkv.shape[2]

    def body(q_ref, kp_ref, kc_ref, vp_ref, vc_ref, o_ref, lse_ref):
        kall = jnp.concatenate([kp_ref[...]] + [kc_ref[t] for t in range(QB)], axis=0)
        vall = jnp.concatenate([vp_ref[...]] + [vc_ref[t] for t in range(QB)], axis=0)
        masks = []
        for t in range(QB):
            mask = _band_mask(pl.program_id(0), QB * pl.program_id(1) + t)
            masks.append(jnp.concatenate([mask, mask], axis=0))
        units = [(t, hp) for t in range(QB) for hp in range(HEADS // 2)]
        scores, probs = {}, {}
        for step in range(len(units) + 2):
            if step < len(units):
                t, hp = units[step]
                cols = slice(hp * LANES, (hp + 1) * LANES)
                scores[step] = _dot_nt(_split_heads(q_ref[t, :, cols]), kall[t * QBLK:(t + 2) * QBLK, cols])
            if 1 <= step <= len(units):
                t, hp = units[step - 1]
                cols = slice(hp * LANES, (hp + 1) * LANES)
                s = jnp.where(masks[t], scores.pop(step - 1), NEG)
                m = jnp.max(s, axis=-1, keepdims=True)
                e = jnp.exp(s - m)
                l = jnp.sum(e, axis=-1, keepdims=True)
                probs[step - 1] = (e * (1.0 / l)).astype(BF16)
                lse = m + jnp.log(l)
                lse_ref[t, :, cols] = _merge_heads(lse[:QBLK], lse[QBLK:])
            if 2 <= step:
                t, hp = units[step - 2]
                cols = slice(hp * LANES, (hp + 1) * LANES)
                pr = probs.pop(step - 2)
                o_ref[t, :, cols] = _dot(jnp.concatenate([pr[:QBLK], pr[QBLK:]], axis=1),
                                         _split_heads(vall[t * QBLK:(t + 2) * QBLK, cols]))

    cur = lambda which: pl.BlockSpec((None, None, QB, QBLK, D_ATTN), lambda p, i: (which, p, i, 0, 0))
    prev = lambda which: pl.BlockSpec((None, None, None, QBLK, D_ATTN),
                                      lambda p, i: (which, p, jnp.maximum(QB * i - 1, 0), 0, 0))
    out = pl.BlockSpec((None, QB, QBLK, D_ATTN), lambda p, i: (p, i, 0, 0))
    return _call(
        body, grid=(N_PATTERNS, nb // QB), in_specs=[cur(0), prev(1), cur(1), prev(2), cur(2)], out_specs=[out, out],
        out_shape=[jax.ShapeDtypeStruct((N_PATTERNS, nb, QBLK, D_ATTN), F32)] * 2,
        sem=("arbitrary", "arbitrary"), name=name, args=(qkv, qkv, qkv, qkv, qkv), phase=phase)


def _attn_combine(o3, lse3, B, S, *, name):
    def body(o_ref, l_ref, a_ref, lt_ref, so_ref, sl_ref, tmp_ref):
        for p in range(N_PATTERNS):
            _unpermute_in(lambda r0, n, p=p: o_ref[p, pl.ds(r0, n), :], so_ref.at[p], tmp_ref, p, S)
            _unpermute_in(lambda r0, n, p=p: l_ref[p, pl.ds(r0, n), :], sl_ref.at[p], tmp_ref, p, S)
        l0, l1, l2 = sl_ref[0], sl_ref[1], sl_ref[2]
        m = jnp.maximum(jnp.maximum(l0, l1), l2)
        w0, w1, w2 = jnp.exp(l0 - m), jnp.exp(l1 - m), jnp.exp(l2 - m)
        tot = w0 + w1 + w2
        a_ref[...] = (w0 * so_ref[0] + w1 * so_ref[1] + w2 * so_ref[2]) / tot
        lt_ref[...] = m + jnp.log(tot)

    o3 = o3.reshape(N_PATTERNS, B, S, D_ATTN)
    lse3 = lse3.reshape(N_PATTERNS, B, S, D_ATTN)
    inp = pl.BlockSpec((N_PATTERNS, None, S, LANES), lambda b, c: (0, b, 0, c))
    out = pl.BlockSpec((S, LANES), lambda b, c: (b, c))
    return pl.pallas_call(
        body, grid=(B, D_ATTN // LANES), in_specs=[inp, inp], out_specs=[out, out],
        out_shape=[jax.ShapeDtypeStruct((B * S, D_ATTN), F32)] * 2,
        scratch_shapes=[pltpu.VMEM((N_PATTERNS, S, LANES), F32)] * 2 + [pltpu.VMEM((S, LANES), F32)],
        compiler_params=_params("arbitrary", "arbitrary"), name=name)(o3, lse3)


STAT_D = 8


def _attn_bwd_prep(dattn, attn, lse, B, S, *, name):
    emat = _seg_matrix(LANES)
    ncc = D_ATTN // LANES

    def body(da_ref, a_ref, l_ref, e_ref, do_ref, st_ref, scr_ref, nat_ref, tmp_ref):
        cc = pl.program_id(1)
        da = da_ref[...]
        dsum = _seg_mean(da * a_ref[...], e_ref, 1.0)
        scr_ref[...] = da

        def put_do(p, row0, v):
            do_ref[p, row0:row0 + v.shape[0], :] = v.astype(BF16)

        _permute(scr_ref, tmp_ref, put_do)

        lane = lax.broadcasted_iota(jnp.int32, (S, LANES), 1)
        h0 = 2 * cc
        vals = ((h0, l_ref[:, 0:1]), (h0 + 1, l_ref[:, HEAD_DIM:HEAD_DIM + 1]),
                (STAT_D + h0, dsum[:, 0:1]), (STAT_D + h0 + 1, dsum[:, HEAD_DIM:HEAD_DIM + 1]))
        tile = jnp.where(cc == 0, 0.0, nat_ref[...])
        for at, col in vals:
            tile = jnp.where(lane == at, col, tile)
        nat_ref[...] = tile

        @pl.when(cc == ncc - 1)
        def _():
            def put_st(p, row0, v):
                st_ref[p, row0:row0 + v.shape[0], :] = v

            _permute(nat_ref, tmp_ref, put_st)

    inp = pl.BlockSpec((S, LANES), lambda b, c: (b, c))
    return pl.pallas_call(
        body, grid=(B, ncc),
        in_specs=[inp, inp, inp, pl.BlockSpec((LANES, LANES), lambda b, c: (0, 0))],
        out_specs=[pl.BlockSpec((N_PATTERNS, None, S, LANES), lambda b, c: (0, b, 0, c)),
                   pl.BlockSpec((N_PATTERNS, None, S, LANES), lambda b, c: (0, b, 0, 0))],
        out_shape=[jax.ShapeDtypeStruct((N_PATTERNS, B, S, D_ATTN), BF16),
                   jax.ShapeDtypeStruct((N_PATTERNS, B, S, LANES), F32)],
        scratch_shapes=[pltpu.VMEM((S, LANES), F32)] * 3,
        compiler_params=_params("arbitrary", "arbitrary"), name=name)(dattn, attn, lse, emat)


def _attn_bwd(qkv, do3, st3, *, name, phase=None):
    QB = QB_BWD
    nb = qkv.shape[2]
    ngroups = nb // QB

    def body(q_ref, kp_ref, kc_ref, vp_ref, vc_ref, do_ref, st_ref, out_ref, carry_ref):
        p = pl.program_id(0)
        i = pl.program_id(1)

        @pl.when((p == 0) & (i == 0))
        def _():
            carry_ref[...] = jnp.zeros_like(carry_ref)

        kall = jnp.concatenate([kp_ref[...]] + [kc_ref[t] for t in range(QB)], axis=0)
        vall = jnp.concatenate([vp_ref[...]] + [vc_ref[t] for t in range(QB)], axis=0)

        masks = []
        for t in range(QB):
            mask = _band_mask(p, QB * i + t) & (i < ngroups)
            masks.append(jnp.concatenate([mask, mask], axis=0))

        def operands(t, hp):
            cols = slice(hp * LANES, (hp + 1) * LANES)
            kh, vh = kall[t * QBLK:(t + 2) * QBLK, cols], vall[t * QBLK:(t + 2) * QBLK, cols]
            return kh, vh, _split_heads(q_ref[t, :, cols]), _split_heads(do_ref[t, :, cols])

        def stage_scores(t, hp):
            kh, vh, q2, do2 = operands(t, hp)
            return _dot_nt(q2, kh), _dot_nt(do2, vh)

        def stage_softmax(t, hp, s, dp):
            h0, h1 = 2 * hp, 2 * hp + 1
            lse = jnp.concatenate([st_ref[t, :, h0:h0 + 1], st_ref[t, :, h1:h1 + 1]], axis=0)
            dsum = jnp.concatenate([st_ref[t, :, STAT_D + h0:STAT_D + h0 + 1],
                                    st_ref[t, :, STAT_D + h1:STAT_D + h1 + 1]], axis=0)
            pr = jnp.where(masks[t], jnp.exp(s - lse), 0.0)
            return (pr * (dp - dsum)).astype(BF16), pr.astype(BF16)

        def stage_grads(t, hp, ds, prb):
            cols = slice(hp * LANES, (hp + 1) * LANES)
            kh, vh, q2, do2 = operands(t, hp)
            dq = _dot(jnp.concatenate([ds[:QBLK], ds[QBLK:]], axis=1), _split_heads(kh))
            dk, dv = _dot_tn(ds, q2), _dot_tn(prb, do2)
            if t == 0:
                for c in range(3):
                    for tt in range(QB):
                        v = carry_ref[c, tt, :, cols]
                        if tt == QB - 1 and c > 0:
                            v = v + (dk if c == 1 else dv)[:QBLK]
                        out_ref[c, tt, :, cols] = v.astype(BF16)
            else:
                carry_ref[1, t - 1, :, cols] += dk[:QBLK]
                carry_ref[2, t - 1, :, cols] += dv[:QBLK]
            carry_ref[0, t, :, cols] = dq
            carry_ref[1, t, :, cols] = dk[QBLK:]
            carry_ref[2, t, :, cols] = dv[QBLK:]

        units = [(t, hp) for hp in range(HEADS // 2) for t in range(QB)]
        scores, probs = {}, {}
        for step in range(len(units) + 2):
            if step < len(units):
                scores[step] = stage_scores(*units[step])
            if 1 <= step <= len(units):
                probs[step - 1] = stage_softmax(*units[step - 1], *scores.pop(step - 1))
            if 2 <= step:
                stage_grads(*units[step - 2], *probs.pop(step - 2))

    group = lambda i: jnp.minimum(i, ngroups - 1)
    cur = lambda which: pl.BlockSpec((None, None, QB, QBLK, D_ATTN), lambda p, i: (which, p, group(i), 0, 0))
    prev = lambda which: pl.BlockSpec((None, None, None, QBLK, D_ATTN),
                                      lambda p, i: (which, p, jnp.maximum(QB * group(i) - 1, 0), 0, 0))
    aux = lambda lanes: pl.BlockSpec((None, QB, QBLK, lanes), lambda p, i: (p, group(i), 0, 0))
    return _call(
        body, grid=(N_PATTERNS, ngroups + 1),
        in_specs=[cur(0), prev(1), cur(1), prev(2), cur(2), aux(D_ATTN), aux(LANES)],
        out_specs=[pl.BlockSpec((3, None, QB, QBLK, D_ATTN), lambda p, i: (0, p, jnp.maximum(i - 1, 0), 0, 0))],
        out_shape=[jax.ShapeDtypeStruct((3, N_PATTERNS, nb, QBLK, D_ATTN), BF16)],
        scratch_shapes=[pltpu.VMEM((3, QB, QBLK, D_ATTN), F32)],
        sem=("arbitrary", "arbitrary"), name=name, args=(qkv, qkv, qkv, qkv, qkv, do3, st3), phase=phase)


def _attn_grad_combine(cur, u, gains, B, S, *, name, phase=None):
    emat = _seg_matrix(LANES)

    def body(cur_ref, u_ref, g_ref, e_ref, du_ref, dg_ref, scr_ref, tmp_ref):
        c = pl.program_id(0)
        b = pl.program_id(2)
        for p in range(N_PATTERNS):
            _unpermute_in(lambda r0, n, p=p: cur_ref[p, pl.ds(r0, n), :].astype(F32), scr_ref.at[p], tmp_ref, p, S)
        dy = scr_ref[0] + scr_ref[1] + scr_ref[2]
        xv = u_ref[...]
        gain = g_ref[...]
        ms = _seg_mean(xv * xv, e_ref, HEAD_DIM)
        r = lax.rsqrt(ms + EPS)
        xhat = xv * r
        dxh = dy * gain
        dx = r * (dxh - xhat * _seg_mean(dxh * xhat, e_ref, HEAD_DIM))
        du_ref[...] = jnp.where(c < 2, dx, dy).astype(BF16)

        @pl.when((b == 0))
        def _():
            dg_ref[...] = jnp.zeros_like(dg_ref)

        dg_ref[...] += jnp.sum(dy * xhat, axis=0, keepdims=True)

    cur = cur.reshape(3, N_PATTERNS, B, S, D_ATTN)
    ncc = D_ATTN // LANES
    return _call(
        body, grid=(3, ncc, B),
        in_specs=[pl.BlockSpec((None, N_PATTERNS, None, S, LANES), lambda c, cc, b: (c, 0, b, 0, cc)),
                  pl.BlockSpec((S, LANES), lambda c, cc, b: (b, c * ncc + cc)),
                  pl.BlockSpec((None, 1, LANES), lambda c, cc, b: (c, 0, cc)),
                  pl.BlockSpec((LANES, LANES), lambda c, cc, b: (0, 0))],
        out_specs=[pl.BlockSpec((S, LANES), lambda c, cc, b: (b, c * ncc + cc)),
                   pl.BlockSpec((None, 1, LANES), lambda c, cc, b: (c, 0, cc))],
        out_shape=[jax.ShapeDtypeStruct((B * S, 3 * D_ATTN), BF16), jax.ShapeDtypeStruct((3, 1, D_ATTN), F32)],
        scratch_shapes=[pltpu.VMEM((N_PATTERNS, S, LANES), F32), pltpu.VMEM((S, LANES), F32)],
        sem=("arbitrary", "arbitrary", "arbitrary"), name=name, args=(cur, u, gains, emat), phase=phase)


HALO = 32
SUB = 64
SUBLANES = 8


def _shifted_copies(src_ref, sh_ref, tc):
    sh_ref[0] = src_ref[...]
    for r in range(1, SUBLANES):
        sh_ref[r, 0:tc + HALO - SUBLANES, :] = src_ref[pl.ds(r, tc + HALO - SUBLANES), :]


def _shifted(sh_ref, start, size):
    return sh_ref[start % SUBLANES, pl.ds(start - start % SUBLANES, size), :]


def _conv_fwd(u, cw, cb, lg, lb, B, S, *, tc, name, phase=None):
    nchunk = S // tc
    hb = tc // HALO

    def body(ca_ref, cap_ref, cg_ref, cgp_ref, w_ref, cb_ref, lg_ref, lb_ref, cv_ref, glu_ref, y_ref, pad_ref, sh_ref):
        i = pl.program_id(1)
        glu = ca_ref[...] * _sigmoid(cg_ref[...])
        glu_ref[...] = glu
        prev = cap_ref[...] * _sigmoid(cgp_ref[...])
        pad_ref[0:HALO, :] = jnp.where(i > 0, prev, 0.0)
        pad_ref[HALO:, :] = glu
        _shifted_copies(pad_ref, sh_ref, tc)
        for sub in range(tc // SUB):
            acc = jnp.zeros((SUB, D_CONV), F32) + cb_ref[...]
            for k in range(CONV_K):
                acc = acc + _shifted(sh_ref, sub * SUB + HALO - (CONV_K - 1) + k, SUB) * w_ref[pl.ds(k, 1), :]
            y_ref[sub * SUB:(sub + 1) * SUB, :] = acc
        y = y_ref[...]
        mu = jnp.mean(y, axis=-1, keepdims=True)
        yc = y - mu
        var = jnp.mean(yc * yc, axis=-1, keepdims=True)
        z = yc * lax.rsqrt(var + EPS) * lg_ref[...] + lb_ref[...]
        cv_ref[...] = (z * _sigmoid(z)).astype(BF16)

    def cur(col):
        return pl.BlockSpec((tc, D_CONV), lambda b, i: (b * nchunk + i, col))

    def halo(col):
        return pl.BlockSpec((HALO, D_CONV), lambda b, i: (jnp.maximum((b * nchunk + i) * hb - 1, 0), col))

    vec = pl.BlockSpec((1, D_CONV), lambda b, i: (0, 0))
    out = pl.BlockSpec((tc, D_CONV), lambda b, i: (b * nchunk + i, 0))
    return _call(
        body, grid=(B, nchunk),
        in_specs=[cur(3), halo(3), cur(4), halo(4), pl.BlockSpec((CONV_K, D_CONV), lambda b, i: (0, 0)), vec, vec, vec],
        out_specs=[out, out, out],
        out_shape=[jax.ShapeDtypeStruct((B * S, D_CONV), BF16), jax.ShapeDtypeStruct((B * S, D_CONV), F32),
                   jax.ShapeDtypeStruct((B * S, D_CONV), F32)],
        scratch_shapes=[pltpu.VMEM((tc + HALO, D_CONV), F32), pltpu.VMEM((SUBLANES, tc + HALO, D_CONV), F32)],
        sem=("arbitrary", "arbitrary"), name=name, args=(u, u, u, u, cw, cb, lg, lb), phase=phase)


def _conv_bwd_norm(dcv, y, lg, lb, *, tc, name):
    T = y.shape[0]

    def body(dcv_ref, y_ref, lg_ref, lb_ref, dy_ref, part_ref):
        yv = y_ref[...]
        mu = jnp.mean(yv, axis=-1, keepdims=True)
        yc = yv - mu
        var = jnp.mean(yc * yc, axis=-1, keepdims=True)
        rstd = lax.rsqrt(var + EPS)
        xhat = yc * rstd
        z = xhat * lg_ref[...] + lb_ref[...]
        sig = _sigmoid(z)
        dz = dcv_ref[...] * (sig * (1.0 + z * (1.0 - sig)))
        dxh = dz * lg_ref[...]
        dy = rstd * (dxh - jnp.mean(dxh, axis=-1, keepdims=True)
                     - xhat * jnp.mean(dxh * xhat, axis=-1, keepdims=True))
        dy_ref[...] = dy

        @pl.when(pl.program_id(0) == 0)
        def _():
            part_ref[...] = jnp.zeros_like(part_ref)

        part_ref[0:1, :] += jnp.sum(dz * xhat, axis=0, keepdims=True)
        part_ref[1:2, :] += jnp.sum(dz, axis=0, keepdims=True)
        part_ref[2:3, :] += jnp.sum(dy, axis=0, keepdims=True)

    tok = pl.BlockSpec((tc, D_CONV), lambda i: (i, 0))
    vec = pl.BlockSpec((1, D_CONV), lambda i: (0, 0))
    return pl.pallas_call(
        body, grid=(T // tc,), in_specs=[tok, tok, vec, vec],
        out_specs=[tok, pl.BlockSpec((8, D_CONV), lambda i: (0, 0))],
        out_shape=[jax.ShapeDtypeStruct((T, D_CONV), F32), jax.ShapeDtypeStruct((8, D_CONV), F32)],
        compiler_params=_params("arbitrary"), name=name)(dcv, y, lg, lb)


def _conv_bwd_taps(dy, glu, u, cw, B, S, *, tc, name, phase=None):
    nchunk = S // tc
    hb = tc // HALO
    last_hb = B * S // HALO - 1

    def body(dy_ref, dyn_ref, glu_ref, glup_ref, ca_ref, cg_ref, w_ref, dca_ref, dcg_ref, dw_ref,
             dyp_ref, glp_ref, acc_ref, shd_ref, shg_ref):
        b = pl.program_id(0)
        i = pl.program_id(1)
        dy = dy_ref[...]
        dyp_ref[0:tc, :] = dy
        dyp_ref[tc:, :] = jnp.where(i < nchunk - 1, dyn_ref[...], 0.0)
        glp_ref[0:HALO, :] = jnp.where(i > 0, glup_ref[...], 0.0)
        glp_ref[HALO:, :] = glu_ref[...]
        _shifted_copies(dyp_ref, shd_ref, tc)
        _shifted_copies(glp_ref, shg_ref, tc)

        @pl.when((b == 0) & (i == 0))
        def _():
            dw_ref[...] = jnp.zeros_like(dw_ref)

        for sub in range(tc // SUB):
            acc = jnp.zeros((SUB, D_CONV), F32)
            for k in range(CONV_K):
                acc = acc + _shifted(shd_ref, sub * SUB + (CONV_K - 1) - k, SUB) * w_ref[pl.ds(k, 1), :]
            acc_ref[sub * SUB:(sub + 1) * SUB, :] = acc
        for k in range(CONV_K):
            dw_ref[k:k + 1, :] += jnp.sum(dy * _shifted(shg_ref, HALO - (CONV_K - 1) + k, tc), axis=0, keepdims=True)
        dglu = acc_ref[...]
        ca = ca_ref[...]
        sig = _sigmoid(cg_ref[...])
        dca_ref[...] = (dglu * sig).astype(BF16)
        dcg_ref[...] = (dglu * ca * sig * (1.0 - sig)).astype(BF16)

    tok = pl.BlockSpec((tc, D_CONV), lambda b, i: (b * nchunk + i, 0))
    nxt = pl.BlockSpec((HALO, D_CONV), lambda b, i: (jnp.minimum((b * nchunk + i + 1) * hb, last_hb), 0))
    prv = pl.BlockSpec((HALO, D_CONV), lambda b, i: (jnp.maximum((b * nchunk + i) * hb - 1, 0), 0))
    return _call(
        body, grid=(B, nchunk),
        in_specs=[tok, nxt, tok, prv,
                  pl.BlockSpec((tc, D_CONV), lambda b, i: (b * nchunk + i, 3)),
                  pl.BlockSpec((tc, D_CONV), lambda b, i: (b * nchunk + i, 4)),
                  pl.BlockSpec((CONV_K, D_CONV), lambda b, i: (0, 0))],
        out_specs=[tok, tok, pl.BlockSpec((32, D_CONV), lambda b, i: (0, 0))],
        out_shape=[jax.ShapeDtypeStruct((B * S, D_CONV), BF16), jax.ShapeDtypeStruct((B * S, D_CONV), BF16),
                   jax.ShapeDtypeStruct((32, D_CONV), F32)],
        scratch_shapes=[pltpu.VMEM((tc + HALO, D_CONV), F32), pltpu.VMEM((tc + HALO, D_CONV), F32),
                        pltpu.VMEM((tc, D_CONV), F32), pltpu.VMEM((SUBLANES, tc + HALO, D_CONV), F32),
                        pltpu.VMEM((SUBLANES, tc + HALO, D_CONV), F32)],
        sem=("arbitrary", "arbitrary"), name=name, args=(dy, dy, glu, glu, u, u, cw), phase=phase)


def _outproj_fwd(h, attn, cv, wout, *, tm, name):
    T, D = h.shape

    def body(h_ref, a_ref, c_ref, w_ref, o_ref):
        o_ref[...] = (h_ref[...] + _dot(a_ref[...].astype(BF16), w_ref[0:D_ATTN, :])
                      + _dot(c_ref[...], w_ref[D_ATTN:, :]))

    tok = pl.BlockSpec((tm, D), lambda i: (i, 0))
    half = pl.BlockSpec((tm, D_ATTN), lambda i: (i, 0))
    return pl.pallas_call(
        body, grid=(T // tm,), in_specs=[tok, half, half, pl.BlockSpec(wout.shape, lambda i: (0, 0))],
        out_specs=tok, out_shape=jax.ShapeDtypeStruct((T, D), F32),
        compiler_params=_params("arbitrary"), name=name)(h, attn, cv, wout)


def _outproj_bwd(dh, attn, cv, wout, *, tm, name):
    T, D = dh.shape

    def body(dh_ref, a_ref, c_ref, w_ref, da_ref, dc_ref, dw_ref):
        @pl.when(pl.program_id(0) == 0)
        def _():
            dw_ref[...] = jnp.zeros_like(dw_ref)

        dhb = dh_ref[...].astype(BF16)
        da_ref[...] = _dot_nt(dhb, w_ref[0:D_ATTN, :])
        dc_ref[...] = _dot_nt(dhb, w_ref[D_ATTN:, :])
        dw_ref[0:D_ATTN, :] += _dot_tn(a_ref[...].astype(BF16), dhb)
        dw_ref[D_ATTN:, :] += _dot_tn(c_ref[...], dhb)

    tok = pl.BlockSpec((tm, D), lambda i: (i, 0))
    half = pl.BlockSpec((tm, D_ATTN), lambda i: (i, 0))
    wspec = pl.BlockSpec(wout.shape, lambda i: (0, 0))
    return pl.pallas_call(
        body, grid=(T // tm,), in_specs=[tok, half, half, wspec], out_specs=[half, half, wspec],
        out_shape=[jax.ShapeDtypeStruct((T, D_ATTN), F32), jax.ShapeDtypeStruct((T, D_ATTN), F32),
                   jax.ShapeDtypeStruct(wout.shape, F32)],
        compiler_params=_params("arbitrary"), name=name)(dh, attn, cv, wout)


ADAM_BLOCK_BYTES = 3 * 512 * 1024


def _adamw(w, g, m, v, *, name):
    R, C = w.shape
    tr = R
    for cand in (512, 352, 256, 176, 128, 64, 32, 16, 8):
        if R % cand == 0 and cand * C * 4 <= ADAM_BLOCK_BYTES:
            tr = cand
            break
    c1 = 1.0 - ADAM_B1 ** ADAM_STEP
    c2 = 1.0 - ADAM_B2 ** ADAM_STEP

    def body(w_ref, g_ref, m_ref, v_ref, d_ref, nm_ref, nv_ref):
        gv = g_ref[...]
        nm = ADAM_B1 * m_ref[...] + (1.0 - ADAM_B1) * gv
        nv = ADAM_B2 * v_ref[...] + (1.0 - ADAM_B2) * (gv * gv)
        d_ref[...] = -ADAM_LR * ((nm / c1) / (jnp.sqrt(nv / c2) + ADAM_EPS) + ADAM_WD * w_ref[...])
        nm_ref[...] = nm
        nv_ref[...] = nv

    blk = pl.BlockSpec((tr, C), lambda i: (i, 0))
    return pl.pallas_call(
        body, grid=(R // tr,), in_specs=[blk] * 4, out_specs=[blk] * 3,
        out_shape=[jax.ShapeDtypeStruct((R, C), F32)] * 3,
        compiler_params=_params("arbitrary"), name=name)(w, g, m, v)


ADAM_SPLIT = 4


def _adamw_many(ws, gs, ms, vs, *, name, phase=None):
    n = len(ws)
    c1 = 1.0 - ADAM_B1 ** ADAM_STEP
    c2 = 1.0 - ADAM_B2 ** ADAM_STEP

    def body(*refs):
        ins, outs = refs[:4 * n], refs[4 * n:]
        for a in range(n):
            w_ref, g_ref, m_ref, v_ref = ins[4 * a:4 * a + 4]
            gv = g_ref[...]
            nm = ADAM_B1 * m_ref[...] + (1.0 - ADAM_B1) * gv
            nv = ADAM_B2 * v_ref[...] + (1.0 - ADAM_B2) * (gv * gv)
            outs[3 * a][...] = -ADAM_LR * ((nm / c1) / (jnp.sqrt(nv / c2) + ADAM_EPS) + ADAM_WD * w_ref[...])
            outs[3 * a + 1][...] = nm
            outs[3 * a + 2][...] = nv

    in_specs, out_specs, out_shape, args = [], [], [], []
    for w, g, m, v in zip(ws, gs, ms, vs):
        R, C = w.shape
        blk = pl.BlockSpec((R // ADAM_SPLIT, C), lambda i: (i, 0))
        in_specs += [blk] * 4
        out_specs += [blk] * 3
        out_shape += [jax.ShapeDtypeStruct((R, C), F32)] * 3
        args += [w, g, m, v]
    res = _call(body, grid=(ADAM_SPLIT,), in_specs=in_specs, out_specs=out_specs, out_shape=out_shape,
                sem=("arbitrary",), name=name, args=args, phase=phase)
    outs, extra = res if phase is not None else (res, None)
    return list(outs[0::3]), list(outs[1::3]), list(outs[2::3]), extra


TM = 512
TM_WIDE = 1024
TK = 1024
TC = 256


def _local_step(x, tgt, w, overlap=None):
    B, S, D = x.shape
    T = B * S
    x2 = x.reshape(T, D)
    t2 = tgt.reshape(T, D)
    ones = jnp.ones((1, D_ATTN), F32)
    scale = HEAD_DIM ** -0.5
    gains = jnp.stack([jnp.tile(w["q_norm"], (1, HEADS)) * scale, jnp.tile(w["k_norm"], (1, HEADS)), ones])
    g = {}

    def hosting(point, build):
        phase = overlap.phase(point, w, g) if overlap is not None else None
        if phase is None:
            return build(None)
        outs, extra = build(phase)
        overlap.done(point, extra, w, g)
        return outs

    h1, n1, G1, U1 = hosting("ffn1_fwd", lambda ph: _ffn_fwd(
        x2, w["ffn1_norm"], w["wg1"], w["wu1"], w["wd1"], None, tm=TM_WIDE, name="ffn1_fwd", phase=ph))
    u, n2 = hosting("inproj_fwd", lambda ph: _inproj_fwd(h1, w["mix_norm"], w["win"], tm=TM_WIDE, name="inproj_fwd", phase=ph))
    qkv = _qkv_prep(u, gains, B, S, name="qkv_prep")
    qkv = qkv.reshape(3, N_PATTERNS, T // QBLK, QBLK, D_ATTN)
    o3, lse3 = hosting("attn_fwd", lambda ph: _attn_fwd(qkv, name="attn_fwd", phase=ph))
    attn, lse = _attn_combine(o3, lse3, B, S, name="attn_combine")
    cv, glu, yconv = hosting("conv_fwd", lambda ph: _conv_fwd(
        u, w["conv_w"], w["conv_b"], w["conv_ln_g"], w["conv_ln_b"], B, S, tc=TC, name="conv_fwd", phase=ph))
    h2 = _outproj_fwd(h1, attn, cv, w["wout"], tm=TM_WIDE, name="outproj_fwd")
    dh3, n3, G2, U2, loss = _ffn_fwd(h2, w["ffn2_norm"], w["wg2"], w["wu2"], w["wd2"], t2, tm=TM_WIDE, name="ffn2_fwd")

    dG, dU, A, dy, dh2, g["ffn2_norm"] = _ffn_bwd_act(dh3, h2, w["ffn2_norm"], G2, U2, w["wg2"], w["wu2"], w["wd2"],
                                                    tm=TM, name="ffn2_bwd_act")
    g["wg2"], g["wu2"], g["wd2"] = _ffn_bwd_w(n3, dy, dG, dU, A, tk=TK, name="ffn2_bwd_w")
    dattn, dcv, g["wout"] = _outproj_bwd(dh2, attn, cv, w["wout"], tm=TM_WIDE, name="outproj_bwd")
    dyc, cpart = _conv_bwd_norm(dcv, yconv, w["conv_ln_g"], w["conv_ln_b"], tc=TC, name="conv_bwd_norm")
    dca, dcg, dcw = hosting("conv_bwd_taps", lambda ph: _conv_bwd_taps(
        dyc, glu, u, w["conv_w"], B, S, tc=TC, name="conv_bwd_taps", phase=ph))
    do3, st3 = _attn_bwd_prep(dattn, attn, lse, B, S, name="attn_bwd_prep")
    nb = T // QBLK
    (cur,) = hosting("attn_bwd", lambda ph: _attn_bwd(
        qkv, do3.reshape(N_PATTERNS, nb, QBLK, D_ATTN), st3.reshape(N_PATTERNS, nb, QBLK, LANES),
        name="attn_bwd", phase=ph))
    du_qkv, dgains = hosting("attn_grad_combine", lambda ph: _attn_grad_combine(
        cur, u, gains, B, S, name="attn_grad_combine", phase=ph))
    du = jnp.concatenate([du_qkv, dca, dcg], axis=1)
    (g["win"],) = hosting("inproj_bwd_w", lambda ph: _inproj_bwd_w(
        n2, du, w["win"].shape[0], tk=2 * TK, name="inproj_bwd_w", phase=ph))
    dh1, g["mix_norm"] = hosting("inproj_bwd_act", lambda ph: _inproj_bwd_act(
        du, dh2, h1, w["mix_norm"], w["win"], tm=TM_WIDE, name="inproj_bwd_act", phase=ph))
    dG, dU, A, dy, dx, g["ffn1_norm"] = hosting("ffn1_bwd_act", lambda ph: _ffn_bwd_act(
        dh1, x2, w["ffn1_norm"], G1, U1, w["wg1"], w["wu1"], w["wd1"], tm=TM, name="ffn1_bwd_act", phase=ph))
    g["wg1"], g["wu1"], g["wd1"] = hosting("ffn1_bwd_w", lambda ph: _ffn_bwd_w(
        n1, dy, dG, dU, A, tk=TK, name="ffn1_bwd_w", phase=ph))

    g["q_norm"] = dgains[0].reshape(HEADS, HEAD_DIM).sum(axis=0, keepdims=True) * scale
    g["k_norm"] = dgains[1].reshape(HEADS, HEAD_DIM).sum(axis=0, keepdims=True)
    g["conv_ln_g"] = cpart[0:1]
    g["conv_ln_b"] = cpart[1:2]
    g["conv_b"] = cpart[2:3]
    g["conv_w"] = dcw[:CONV_K]
    return loss, dx.reshape(B, S, D), g


N_CHIPS = 4
N_DEV = 8
VMEM_SPEC = pl.BlockSpec(memory_space=pltpu.VMEM)


def _remote(src, dst, send_sem, recv_sem, device):
    return pltpu.make_async_remote_copy(src_ref=src, dst_ref=dst, send_sem=send_sem, recv_sem=recv_sem,
                                        device_id=device, device_id_type=MESH)


def _stage_shards(shards, dtypes, *, name):
    n = len(shards)
    halves = [s.reshape(2, s.shape[0] // 2, s.shape[1]) for s in shards]

    def body(*refs):
        ins, outs, vms, loc_sems = refs[:n], refs[n:2 * n], refs[2 * n:3 * n], refs[3 * n]
        me = 2 * lax.axis_index("x") + lax.axis_index("y")
        copies = []
        for a in range(n):
            vms[a][...] = ins[a][...].astype(dtypes[a])
            cp = pltpu.make_async_copy(vms[a], outs[a].at[me], loc_sems.at[a])
            cp.start()
            copies.append(cp)
        for cp in copies:
            cp.wait()

    return pl.pallas_call(
        body, in_specs=[VMEM_SPEC] * n, out_specs=[ANY] * n,
        out_shape=[jax.ShapeDtypeStruct((N_CHIPS,) + h.shape, dt) for h, dt in zip(halves, dtypes)],
        scratch_shapes=[pltpu.VMEM(h.shape, dt) for h, dt in zip(halves, dtypes)] + [DMA_SEMS((n,))],
        compiler_params=pltpu.CompilerParams(vmem_limit_bytes=VMEM_LIMIT), name=name)(*halves)


def _like(arrays):
    return [jax.ShapeDtypeStruct(a.shape, a.dtype) for a in arrays]


def _axes():
    x, y, c = lax.axis_index("x"), lax.axis_index("y"), lax.axis_index("c")
    first = (x + (1 - c) * (1 - 2 * x), y + c * (1 - 2 * y))
    second = (x + c * (1 - 2 * x), y + (1 - c) * (1 - 2 * y))
    slots = tuple(2 * px + py for px, py in ((x, y), first, second, (1 - x, 1 - y)))
    return (x, y, c), (*first, c), (*second, c), slots


def _gather_ici_phase(bufs, only=None):
    n = len(bufs)

    def stage1(ins, outs, sems):
        (x, y, c), peer1, peer2, (own, s1, s2, both) = _axes()
        starts, arrivals = [], []
        for a in range(n):
            mine, land = outs[a].at[own, c], outs[a].at[s2, c]
            starts.append(_remote(mine, mine, *sems(a), peer2))
            arrivals.append(_remote(land, land, *sems(a), peer2))
        return starts, arrivals

    def stage2(ins, outs, sems):
        (x, y, c), peer1, peer2, (own, s1, s2, both) = _axes()
        starts, arrivals = [], []
        for a in range(n):
            for k, (src, dst) in enumerate(((own, s1), (s2, both))):
                mine, land = outs[a].at[src, c], outs[a].at[dst, c]
                starts.append(_remote(mine, mine, *sems(2 * a + k), peer1))
                arrivals.append(_remote(land, land, *sems(2 * a + k), peer1))
        return starts, arrivals

    same = {a: a for a in range(n)}
    first, second = _Phase(bufs, _like(bufs), same, n, stage1), _Phase(bufs, _like(bufs), same, 2 * n, stage2)
    if only is None:
        return first.then(second)
    return first if only == 1 else second


def _gather_d2d_phase(bufs):
    n = len(bufs)

    def copies(ins, outs, sems):
        (x, y, c), peer1, peer2, (own, s1, s2, both) = _axes()
        starts, arrivals = [], []
        for a in range(n):
            for j, s in enumerate((s1, s2, both)):
                got, land = outs[a].at[s, c], outs[a].at[s, 1 - c]
                starts.append(_remote(got, got, *sems(3 * a + j), (x, y, 1 - c)))
                arrivals.append(_remote(land, land, *sems(3 * a + j), (x, y, 1 - c)))
        return starts, arrivals

    return _Phase(bufs, _like(bufs), {a: a for a in range(n)}, 3 * n, copies)


def _exchange_phase(views):
    n = len(views)

    def copies(ins, outs, sems):
        x, y, c = lax.axis_index("x"), lax.axis_index("y"), lax.axis_index("c")
        starts = [_remote(ins[a].at[pl.ds(0, ins[a].shape[0]), 1 - c], outs[a], *sems(a), (x, y, 1 - c))
                  for a in range(n)]
        return starts, starts

    outs = [jax.ShapeDtypeStruct((v.shape[0],) + v.shape[2:], F32) for v in views]
    return _Phase(views, outs, {}, n, copies)


ADD_SPLIT = 2


def _add_halves(views, got, sel, tag):
    n = len(views)

    def body(s_ref, *refs):
        ins, outs = refs[:4 * n], refs[4 * n:]
        for a in range(n):
            gk, rk, gs, rs = ins[4 * a:4 * a + 4]
            outs[2 * a][...] = gk[...] + rk[...]
            outs[2 * a + 1][...] = (gs[...] + rs[...]).astype(BF16)

    in_specs, out_specs, out_shape, args = [], [], [], []
    for g, r in zip(views, got):
        _, _, rh, cdim = g.shape
        tr = rh // ADD_SPLIT
        for off in (0, 2):
            in_specs.append(pl.BlockSpec((None, None, tr, cdim), lambda k, i, s, off=off: (s[1 + off + k], s[0], i, 0)))
            in_specs.append(pl.BlockSpec((None, tr, cdim), lambda k, i, s, off=off: (s[1 + off + k], i, 0)))
            args += [g, r]
        out_specs += [pl.BlockSpec((None, tr, cdim), lambda k, i, s: (k, i, 0))] * 2
        out_shape += [jax.ShapeDtypeStruct((2, rh, cdim), F32), jax.ShapeDtypeStruct((2, rh, cdim), BF16)]
    res = pl.pallas_call(
        body,
        grid_spec=pltpu.PrefetchScalarGridSpec(num_scalar_prefetch=1, grid=(2, ADD_SPLIT), in_specs=in_specs,
                                               out_specs=out_specs),
        out_shape=out_shape, compiler_params=_params("arbitrary", "arbitrary"), name=f"rs_add_half_{tag}")(sel, *args)
    return list(res[0::2]), list(res[1::2])


def _swap_phase(arrays, stage):
    n = len(arrays)

    def copies(ins, outs, sems):
        peer = _axes()[stage]
        starts = [_remote(ins[a], outs[a], *sems(a), peer) for a in range(n)]
        return starts, starts

    return _Phase(arrays, _like(arrays), {}, n, copies)


def _add_first(keep, got, tag):
    n = len(keep)

    def body(*refs):
        ins, outs = refs[:2 * n], refs[2 * n:]
        for a in range(n):
            k_ref, g_ref = ins[2 * a], ins[2 * a + 1]
            outs[2 * a][...] = k_ref[0] + g_ref[0].astype(F32)
            outs[2 * a + 1][...] = (k_ref[1] + g_ref[1].astype(F32)).astype(BF16)

    in_specs, out_specs, out_shape, args = [], [], [], []
    for k, g in zip(keep, got):
        _, rh, cdim = k.shape
        tr = rh // ADD_SPLIT
        in_specs += [pl.BlockSpec((2, tr, cdim), lambda i: (0, i, 0))] * 2
        out_specs += [pl.BlockSpec((tr, cdim), lambda i: (i, 0))] * 2
        out_shape += [jax.ShapeDtypeStruct((rh, cdim), F32), jax.ShapeDtypeStruct((rh, cdim), BF16)]
        args += [k, g]
    res = pl.pallas_call(body, grid=(ADD_SPLIT,), in_specs=in_specs, out_specs=out_specs, out_shape=out_shape,
                         compiler_params=_params("arbitrary"), name=f"rs_add_first_{tag}")(*args)
    return list(res[0::2]), list(res[1::2])


def _add_second(keep, got, sel, tag):
    n = len(keep)

    def body(s_ref, *refs):
        ins, outs = refs[:2 * n], refs[2 * n:]
        for a in range(n):
            outs[a][...] = ins[2 * a][...] + ins[2 * a + 1][...].astype(F32)

    in_specs, out_specs, out_shape, args = [], [], [], []
    for k, g in zip(keep, got):
        rh, cdim = k.shape
        tr = rh // ADD_SPLIT
        in_specs += [pl.BlockSpec((tr, cdim), lambda i, s: (i, 0))] * 2
        out_specs.append(pl.BlockSpec((None, tr, cdim), lambda i, s: (s[0], i, 0)))
        out_shape.append(jax.ShapeDtypeStruct((2, rh, cdim), F32))
        args += [k, g]
    res = pl.pallas_call(
        body,
        grid_spec=pltpu.PrefetchScalarGridSpec(num_scalar_prefetch=1, grid=(ADD_SPLIT,), in_specs=in_specs,
                                               out_specs=out_specs),
        out_shape=out_shape, compiler_params=_params("arbitrary"), name=f"rs_add_second_{tag}")(sel, *args)
    return list(res)


def _join_phase(halves):
    n = len(halves)

    def copies(ins, outs, sems):
        x, y, c = lax.axis_index("x"), lax.axis_index("y"), lax.axis_index("c")
        starts, arrivals = [], []
        for a in range(n):
            mine, land = outs[a].at[c], outs[a].at[1 - c]
            starts.append(_remote(mine, mine, *sems(a), (x, y, 1 - c)))
            arrivals.append(_remote(land, land, *sems(a), (x, y, 1 - c)))
        return starts, arrivals

    return _Phase(halves, _like(halves), {a: a for a in range(n)}, n, copies)


def _slot_order():
    x, y, c = lax.axis_index("x"), lax.axis_index("y"), lax.axis_index("c")
    own, flip_x, flip_y, both = 2 * x + y, 2 * (1 - x) + y, 2 * x + 1 - y, 2 * (1 - x) + 1 - y
    first = jnp.where(c == 0, flip_x, flip_y)
    second = jnp.where(c == 0, flip_y, flip_x)
    return jnp.stack([c, own, second, first, both]).astype(jnp.int32)


def _half_view(g):
    return g.reshape(N_CHIPS, 2, g.shape[1] // 2, g.shape[2])


EARLY_GRADS = ("wg2", "wu2", "wd2", "wout")
MIDDLE_GRADS = ("win",)


class _Overlap:
    EARLY_AT = ("conv_bwd_taps", "attn_bwd", "attn_grad_combine", "inproj_bwd_w")
    MIDDLE_AT = ("inproj_bwd_act", "ffn1_bwd_act", "ffn1_bwd_w", None)

    def __init__(self, staged):
        self.staged = staged
        self.ffn2 = list(staged[3:])
        self.sel = sel = _slot_order()
        self.early = _Reduction(EARLY_GRADS, "early", sel)
        self.middle = _Reduction(MIDDLE_GRADS, "middle", sel)

    def finish_late(self, late):
        views = [_half_view(a) for a in late]
        got = _run_phase(_exchange_phase(views), name="rs_exchange_halves")
        keep, send = _add_halves(views, got, self.sel, "late")
        got = _run_phase(_swap_phase(send, 1), name="rs_swap_first_axis")
        keep, send = _add_first(keep, got, "late")
        got = _run_phase(_swap_phase(send, 2), name="rs_swap_second_axis")
        halves = _add_second(keep, got, self.sel, "late")
        full = _run_phase(_join_phase(halves + list(self.middle.halves)), name="rs_join_halves")
        return [f.reshape(-1, f.shape[-1]) for f in full]

    def phase(self, point, w, g):
        if point == "ffn1_fwd":
            return _gather_ici_phase(self.staged[:3])
        if point == "inproj_fwd":
            return _gather_ici_phase(self.ffn2, only=1)
        if point == "attn_fwd":
            return _gather_ici_phase(self.ffn2, only=2)
        if point == "conv_fwd":
            return _gather_d2d_phase(self.ffn2)
        for red, at in ((self.early, self.EARLY_AT), (self.middle, self.MIDDLE_AT)):
            if point in at:
                return red.phase(at.index(point), g)
        return None

    def done(self, point, outs, w, g):
        if point == "ffn1_fwd":
            win, wout, taps = [_whole(b) for b in _run_phase(_gather_d2d_phase(outs), name="gather_mix_d2d")]
            w["win"] = win
            w["wout"] = wout.reshape(-1, wout.shape[-1])
            w["conv_w"] = taps.transpose(1, 0, 2).reshape(CONV_K + 1, D_CONV)[:CONV_K]
        elif point in ("inproj_fwd", "attn_fwd"):
            self.ffn2 = list(outs)
        elif point == "conv_fwd":
            w["wg2"], w["wu2"], w["wd2"] = [_whole(b) for b in outs]
        for red, at in ((self.early, self.EARLY_AT), (self.middle, self.MIDDLE_AT)):
            if point in at:
                red.done(at.index(point), outs)


class _Reduction:
    def __init__(self, names, tag, sel):
        self.names, self.tag, self.sel = names, tag, sel
        self.reduced = {}

    def phase(self, stage, g):
        if stage == 0:
            self.cols = [g[k].shape[-1] for k in self.names]
            self.views = [_half_view(g[k].reshape(N_CHIPS, -1, g[k].shape[-1])) for k in self.names]
            return _exchange_phase(self.views)
        if stage in (1, 2):
            return _swap_phase(self.send, stage)
        return _join_phase(self.halves)

    def done(self, stage, outs):
        if stage == 0:
            self.keep, self.send = _add_halves(self.views, outs, self.sel, self.tag)
        elif stage == 1:
            self.keep, self.send = _add_first(self.keep, outs, self.tag)
        elif stage == 2:
            self.halves = _add_second(self.keep, outs, self.sel, self.tag)
        else:
            for k, c, f in zip(self.names, self.cols, outs):
                self.reduced[k] = f.reshape(-1, c)


def _whole(buf):
    return buf.reshape(buf.shape[0], 2 * buf.shape[2], buf.shape[3])


def _allreduce_small(pack, *, name):
    rows = pack.shape[0]

    def body(p_ref, o_ref, buf_ref, send_sems, recv_sems):
        x, y, c = lax.axis_index("x"), lax.axis_index("y"), lax.axis_index("c")
        me = 4 * x + 2 * y + c
        buf_ref[me] = p_ref[...]
        cps = []
        for k in range(1, N_DEV):
            peer = tuple(1 - v if (k >> s) & 1 else v for v, s in ((x, 2), (y, 1), (c, 0)))
            cp = _remote(p_ref, buf_ref.at[me], send_sems.at[k - 1], recv_sems.at[k - 1], peer)
            cp.start()
            cps.append(cp)
        for k in range(1, N_DEV):
            src = 4 * (x ^ ((k >> 2) & 1)) + 2 * (y ^ ((k >> 1) & 1)) + (c ^ (k & 1))
            land = buf_ref.at[src]
            _remote(land, land, send_sems.at[k - 1], recv_sems.at[k - 1], (x, y, c)).wait_recv()
        acc = buf_ref[0]
        for d in range(1, N_DEV):
            acc = acc + buf_ref[d]
        o_ref[...] = acc
        for cp in cps:
            cp.wait_send()

    return pl.pallas_call(
        body, in_specs=[VMEM_SPEC], out_specs=VMEM_SPEC, out_shape=jax.ShapeDtypeStruct(pack.shape, F32),
        scratch_shapes=[pltpu.VMEM((N_DEV, rows, LANES), F32), pltpu.SemaphoreType.DMA((N_DEV - 1,)),
                        pltpu.SemaphoreType.DMA((N_DEV - 1,))], name=name)(pack)


SMALL = ("ffn1_norm", "mix_norm", "q_norm", "k_norm", "conv_b", "conv_ln_g", "conv_ln_b", "ffn2_norm", "conv_w")
BIG = ("ffn1_w_gate", "ffn1_w_up", "ffn1_w_down", "w_in", "w_out", "ffn2_w_gate", "ffn2_w_up", "ffn2_w_down")
TRANSPOSED = ("ffn1_w_gate", "ffn1_w_up", "ffn2_w_gate", "ffn2_w_up")
WEIGHTS = ("ffn1_norm", "ffn1_w_gate", "ffn1_w_up", "ffn1_w_down", "mix_norm", "w_in", "q_norm", "k_norm",
           "conv_w", "conv_b", "conv_ln_g", "conv_ln_b", "w_out", "ffn2_norm", "ffn2_w_gate", "ffn2_w_up",
           "ffn2_w_down")


def _pack(parts):
    rows = []
    for p in parts:
        flat = p.reshape(-1)
        tile = SUBLANES * LANES
        padded = -(-flat.shape[0] // tile) * tile
        rows.append(jnp.pad(flat, (0, padded - flat.shape[0])).reshape(-1, LANES))
    return jnp.concatenate(rows, axis=0)


def _unpack(pack, shapes):
    out, row = [], 0
    for shp in shapes:
        size = shp[0] * shp[1]
        tile = SUBLANES * LANES
        nrows = -(-size // tile) * SUBLANES
        out.append(pack[row:row + nrows].reshape(-1)[:size].reshape(shp))
        row += nrows
    return out


def kernel(x, ffn1_norm, ffn1_w_gate, ffn1_w_up, ffn1_w_down, mix_norm, w_in, q_norm, k_norm, conv_w, conv_b, conv_ln_g, conv_ln_b, w_out, ffn2_norm, ffn2_w_gate, ffn2_w_up, ffn2_w_down, loss_target, m_ffn1_norm, m_ffn1_w_gate, m_ffn1_w_up, m_ffn1_w_down, m_mix_norm, m_w_in, m_q_norm, m_k_norm, m_conv_w, m_conv_b, m_conv_ln_g, m_conv_ln_b, m_w_out, m_ffn2_norm, m_ffn2_w_gate, m_ffn2_w_up, m_ffn2_w_down, v_ffn1_norm, v_ffn1_w_gate, v_ffn1_w_up, v_ffn1_w_down, v_mix_norm, v_w_in, v_q_norm, v_k_norm, v_conv_w, v_conv_b, v_conv_ln_g, v_conv_ln_b, v_w_out, v_ffn2_norm, v_ffn2_w_gate, v_ffn2_w_up, v_ffn2_w_down):
    wts = dict(ffn1_norm=ffn1_norm, ffn1_w_gate=ffn1_w_gate[0], ffn1_w_up=ffn1_w_up[0], ffn1_w_down=ffn1_w_down[0],
               mix_norm=mix_norm, w_in=w_in[0], q_norm=q_norm, k_norm=k_norm, conv_w=conv_w[0], conv_b=conv_b,
               conv_ln_g=conv_ln_g, conv_ln_b=conv_ln_b, w_out=w_out[0], ffn2_norm=ffn2_norm,
               ffn2_w_gate=ffn2_w_gate[0], ffn2_w_up=ffn2_w_up[0], ffn2_w_down=ffn2_w_down[0])
    mom = dict(ffn1_norm=m_ffn1_norm, ffn1_w_gate=m_ffn1_w_gate[0], ffn1_w_up=m_ffn1_w_up[0], ffn1_w_down=m_ffn1_w_down[0],
               mix_norm=m_mix_norm, w_in=m_w_in[0], q_norm=m_q_norm, k_norm=m_k_norm, conv_w=m_conv_w[0], conv_b=m_conv_b,
               conv_ln_g=m_conv_ln_g, conv_ln_b=m_conv_ln_b, w_out=m_w_out[0], ffn2_norm=m_ffn2_norm,
               ffn2_w_gate=m_ffn2_w_gate[0], ffn2_w_up=m_ffn2_w_up[0], ffn2_w_down=m_ffn2_w_down[0])
    var = dict(ffn1_norm=v_ffn1_norm, ffn1_w_gate=v_ffn1_w_gate[0], ffn1_w_up=v_ffn1_w_up[0], ffn1_w_down=v_ffn1_w_down[0],
               mix_norm=v_mix_norm, w_in=v_w_in[0], q_norm=v_q_norm, k_norm=v_k_norm, conv_w=v_conv_w[0], conv_b=v_conv_b,
               conv_ln_g=v_conv_ln_g, conv_ln_b=v_conv_ln_b, w_out=v_w_out[0], ffn2_norm=v_ffn2_norm,
               ffn2_w_gate=v_ffn2_w_gate[0], ffn2_w_up=v_ffn2_w_up[0], ffn2_w_down=v_ffn2_w_down[0])
    chip = 2 * lax.axis_index("x") + lax.axis_index("y")
    for src in (wts, mom, var):
        for n in TRANSPOSED:
            src[n] = src[n].T

    taps = jnp.pad(wts["conv_w"], ((0, 1), (0, 0)))
    staged = _stage_shards([wts["ffn1_w_gate"], wts["ffn1_w_up"], wts["ffn1_w_down"], wts["w_in"], wts["w_out"], taps,
                            wts["ffn2_w_gate"], wts["ffn2_w_up"], wts["ffn2_w_down"]],
                           [BF16, BF16, BF16, BF16, BF16, F32, BF16, BF16, BF16], name="stage_shards")
    first = _run_phase(_gather_ici_phase(staged[:3]).then(_gather_d2d_phase(staged[:3])), name="gather_ffn1")
    wg1, wu1, wd1 = [_whole(b) for b in first]
    w = dict(ffn1_norm=ffn1_norm, mix_norm=mix_norm, ffn2_norm=ffn2_norm, q_norm=q_norm, k_norm=k_norm,
             conv_b=conv_b, conv_ln_g=conv_ln_g, conv_ln_b=conv_ln_b, wg1=wg1, wu1=wu1, wd1=wd1)
    overlap = _Overlap(staged[3:])
    loss_part, grad_x, g = _local_step(x, loss_target, w, overlap)

    grads, delta, new_m, new_v = {}, {}, {}, {}
    early = overlap.early.reduced
    grads.update(ffn2_w_gate=early["wg2"], ffn2_w_up=early["wu2"], ffn2_w_down=early["wd2"], w_out=early["wout"])
    grads.update(zip(("ffn1_w_gate", "ffn1_w_up", "ffn1_w_down", "w_in"),
                     overlap.finish_late([g["wg1"], g["wu1"], g["wd1"]])))

    small_shapes = [g[n].shape for n in SMALL] + [(SUBLANES, LANES)]
    red = _allreduce_small(_pack([g[n] for n in SMALL] + [loss_part]), name="allreduce_small")
    small = dict(zip(SMALL + ("loss",), _unpack(red, small_shapes)))
    loss = small["loss"][0, 0]
    small["conv_w"] = lax.dynamic_slice_in_dim(small["conv_w"], chip * LANES, LANES, axis=1)

    for tag, names in (("early", ("ffn2_w_gate", "ffn2_w_up", "ffn2_w_down", "w_out")),
                       ("late", ("ffn1_w_gate", "ffn1_w_up", "ffn1_w_down", "w_in"))):
        d, m, v, _ = _adamw_many([wts[n] for n in names], [grads[n] for n in names], [mom[n] for n in names],
                                 [var[n] for n in names], name=f"adamw_{tag}")
        for dst, vals in ((delta, d), (new_m, m), (new_v, v)):
            dst.update(zip(names, vals))
    shapes = [wts[n].shape for n in SMALL]
    packs = [_pack([src[n] for n in SMALL]) for src in (wts, small, mom, var)]
    outs = _adamw(*packs, name="adamw_small")
    for dst, pk in zip((delta, new_m, new_v), outs):
        dst.update(zip(SMALL, _unpack(pk, shapes)))
    for n in SMALL:
        grads[n] = small[n]

    def shaped(d, n):
        v = d[n].T if n in TRANSPOSED else d[n]
        return v.reshape((1,) + v.shape) if n in BIG or n == "conv_w" else v

    return (loss, grad_x, *[shaped(grads, n) for n in WEIGHTS], *[shaped(delta, n) for n in WEIGHTS],
            *[shaped(new_m, n) for n in WEIGHTS], *[shaped(new_v, n) for n in WEIGHTS])
```

```python
import functools

import jax
import jax.numpy as jnp
from jax import lax
from jax.experimental import pallas as pl
from jax.experimental.pallas import tpu as pltpu

F32 = jnp.float32
BF16 = jnp.bfloat16

EPS = 1e-6
HEADS = 8
HEAD_DIM = 64
D_ATTN = HEADS * HEAD_DIM
D_CONV = 512
CONV_K = 31
QBLK = 128
N_PATTERNS = 3
DILATIONS = (1, 4, 16)
LANES = 128
NEG = -1e30

ADAM_LR = 0.001
ADAM_B1 = 0.9
ADAM_B2 = 0.999
ADAM_EPS = 1e-08
ADAM_WD = 0.01
ADAM_STEP = 10

VMEM_LIMIT = 56 * 1024 * 1024
MESH = pl.DeviceIdType.MESH

NT_DIMS = (((1,), (1,)), ((), ()))
TN_DIMS = (((0,), (0,)), ((), ()))


def _params(*sem):
    return pltpu.CompilerParams(dimension_semantics=sem, vmem_limit_bytes=VMEM_LIMIT)


def _dot(a, b):
    return jnp.dot(a, b, preferred_element_type=F32)


def _dot_nt(a, b):
    return lax.dot_general(a, b, NT_DIMS, preferred_element_type=F32)


def _dot_tn(a, b):
    return lax.dot_general(a, b, TN_DIMS, preferred_element_type=F32)


def _sigmoid(x):
    return 1.0 / (1.0 + jnp.exp(-x))


def _seg_mean(v, e_ref, width):
    hi = v.astype(BF16)
    lo = (v - hi.astype(F32)).astype(BF16)
    e = e_ref[...]
    return (_dot(hi, e) + _dot(lo, e)) * (1.0 / width)


def _seg_matrix(n):
    i = jnp.arange(n)
    return (i[:, None] // HEAD_DIM == i[None, :] // HEAD_DIM).astype(BF16)


ANY = pl.BlockSpec(memory_space=pl.ANY)
DMA_SEMS = pltpu.SemaphoreType.DMA


class _Phase:
    def __init__(self, ins, outs, aliases, nsem, copies):
        self.ins, self.outs, self.aliases = list(ins), list(outs), dict(aliases)
        self.stages = [(nsem, copies)]

    def then(self, other):
        self.stages = self.stages + other.stages
        return self

    @property
    def nsem(self):
        return sum(n for n, _ in self.stages)

    def _copies(self, k, in_refs, out_refs, send_sems, recv_sems):
        base = sum(n for n, _ in self.stages[:k])
        return self.stages[k][1](in_refs, out_refs, lambda i: (send_sems.at[base + i], recv_sems.at[base + i]))

    def start(self, k, *refs):
        for cp in self._copies(k, *refs)[0]:
            cp.start()

    def finish(self, k, *refs):
        starts, arrivals = self._copies(k, *refs)
        for cp in arrivals:
            cp.wait_recv()
        for cp in starts:
            cp.wait_send()


def _run_phase(phase, *, name):
    n_in, n_out = len(phase.ins), len(phase.outs)

    def body(*refs):
        ins, outs = refs[:n_in], refs[n_in:n_in + n_out]
        send_sems, recv_sems = refs[n_in + n_out:]
        for k in range(len(phase.stages)):
            phase.start(k, ins, outs, send_sems, recv_sems)
            phase.finish(k, ins, outs, send_sems, recv_sems)

    return pl.pallas_call(
        body, in_specs=[ANY] * n_in, out_specs=[ANY] * n_out, out_shape=phase.outs,
        input_output_aliases=phase.aliases,
        scratch_shapes=[DMA_SEMS((phase.nsem,)), DMA_SEMS((phase.nsem,))], name=name)(*phase.ins)


def _call(body, *, grid, in_specs, out_specs, out_shape, scratch_shapes=(), sem, name, args, phase=None):
    in_specs, out_specs, out_shape = list(in_specs), list(out_specs), list(out_shape)
    scratch_shapes = list(scratch_shapes)
    if phase is None:
        return pl.pallas_call(body, grid=grid, in_specs=in_specs, out_specs=out_specs, out_shape=out_shape,
                              scratch_shapes=scratch_shapes, compiler_params=_params(*sem), name=name)(*args)
    n_in, n_out, n_scr = len(in_specs), len(out_specs), len(scratch_shapes)
    p_in, p_out = len(phase.ins), len(phase.outs)

    def hosted(*refs):
        ins, pins = refs[:n_in], refs[n_in:n_in + p_in]
        o0 = n_in + p_in
        outs, pouts = refs[o0:o0 + n_out], refs[o0 + n_out:o0 + n_out + p_out]
        s0 = o0 + n_out + p_out
        scr = refs[s0:s0 + n_scr]
        send_sems, recv_sems = refs[s0 + n_scr:]
        step = 0
        for d, n in enumerate(grid):
            step = step * n + pl.program_id(d)
        nsteps = functools.reduce(lambda a, b: a * b, grid)
        nstages = len(phase.stages)
        comm_refs = (pins, pouts, send_sems, recv_sems)

        for k in range(nstages):
            @pl.when(step == (k * nsteps) // nstages)
            def _(k=k):
                if k > 0:
                    phase.finish(k - 1, *comm_refs)
                phase.start(k, *comm_refs)

        body(*ins, *outs, *scr)

        @pl.when(step == nsteps - 1)
        def _():
            phase.finish(nstages - 1, *comm_refs)

    res = pl.pallas_call(
        hosted, grid=grid, in_specs=in_specs + [ANY] * p_in, out_specs=out_specs + [ANY] * p_out,
        out_shape=out_shape + phase.outs,
        input_output_aliases={n_in + i: n_out + o for i, o in phase.aliases.items()},
        scratch_shapes=scratch_shapes + [DMA_SEMS((phase.nsem,)), DMA_SEMS((phase.nsem,))],
        compiler_params=_params(*sem), name=name)(*args, *phase.ins)
    return res[:n_out], res[n_out:]


ROW_CHUNK = 256


def _ffn_fwd(x, gain, wg, wu, wd, tgt, *, tm, name, phase=None):
    T, D = x.shape
    NS, Fs, _ = wg.shape
    with_loss = tgt is not None

    def body(*refs):
        if with_loss:
            x_ref, g_ref, wg_ref, wu_ref, wd_ref, t_ref, h_ref, n_ref, G_ref, U_ref, loss_ref, acc_ref = refs
        else:
            x_ref, g_ref, wg_ref, wu_ref, wd_ref, h_ref, n_ref, G_ref, U_ref, acc_ref = refs
        i = pl.program_id(0)
        j = pl.program_id(1)

        @pl.when(j == 0)
        def _():
            xv = x_ref[...]
            r = lax.rsqrt(jnp.mean(xv * xv, axis=-1, keepdims=True) + EPS)
            n_ref[...] = (xv * r * g_ref[...]).astype(BF16)
            acc_ref[...] = jnp.zeros_like(acc_ref)

        n = n_ref[...]
        G = _dot_nt(n, wg_ref[...])
        U = _dot_nt(n, wu_ref[...])
        G_ref[...] = G.astype(BF16)
        U_ref[...] = U.astype(BF16)
        A = (G * _sigmoid(G) * U).astype(BF16)
        acc_ref[...] += _dot(A, wd_ref[...])

        @pl.when(j == NS - 1)
        def _():
            h = x_ref[...] + 0.5 * acc_ref[...]
            if with_loss:
                e = h - t_ref[...]
                h_ref[...] = e * (1.0 / D)

                @pl.when(i == 0)
                def _():
                    loss_ref[...] = jnp.zeros_like(loss_ref)

                loss_ref[...] += jnp.sum(e * e) * (0.5 / D)
            else:
                h_ref[...] = h

    tok = pl.BlockSpec((tm, D), lambda i, j: (i, 0))
    in_specs = [tok, pl.BlockSpec((1, D), lambda i, j: (0, 0)),
                pl.BlockSpec((None, Fs, D), lambda i, j: (j, 0, 0)),
                pl.BlockSpec((None, Fs, D), lambda i, j: (j, 0, 0)),
                pl.BlockSpec((None, Fs, D), lambda i, j: (j, 0, 0))]
    args = [x, gain, wg, wu, wd]
    act = pl.BlockSpec((None, tm, Fs), lambda i, j: (j, i, 0))
    out_shape = [jax.ShapeDtypeStruct((T, D), F32), jax.ShapeDtypeStruct((T, D), BF16),
                 jax.ShapeDtypeStruct((NS, T, Fs), BF16), jax.ShapeDtypeStruct((NS, T, Fs), BF16)]
    out_specs = [tok, tok, act, act]
    if with_loss:
        in_specs.append(tok)
        args.append(tgt)
        out_shape.append(jax.ShapeDtypeStruct((8, LANES), F32))
        out_specs.append(pl.BlockSpec((8, LANES), lambda i, j: (0, 0)))
    return _call(body, grid=(T // tm, NS), in_specs=in_specs, out_specs=out_specs, out_shape=out_shape,
                 scratch_shapes=[pltpu.VMEM((tm, D), F32)], sem=("arbitrary", "arbitrary"), name=name,
                 args=args, phase=phase)


def _rms_bwd(xv, gain, dn):
    r = lax.rsqrt(jnp.mean(xv * xv, axis=-1, keepdims=True) + EPS)
    xhat = xv * r
    dxh = dn * gain
    dx = r * (dxh - xhat * jnp.mean(dxh * xhat, axis=-1, keepdims=True))
    dg = jnp.sum(dn * xhat, axis=0, keepdims=True)
    return dx, dg


def _ffn_bwd_act(dh, x, gain, G, U, wg, wu, wd, *, tm, name, phase=None):
    T, D = x.shape
    NS, Fs, _ = wg.shape

    def body(dh_ref, x_ref, g_ref, G_ref, U_ref, wg_ref, wu_ref, wd_ref,
             dG_ref, dU_ref, A_ref, dy_ref, dx_ref, dg_ref, acc_ref):
        i = pl.program_id(0)
        j = pl.program_id(1)

        @pl.when(j == 0)
        def _():
            dy_ref[...] = (0.5 * dh_ref[...]).astype(BF16)
            acc_ref[...] = jnp.zeros_like(acc_ref)

        @pl.when((i == 0) & (j == 0))
        def _():
            dg_ref[...] = jnp.zeros_like(dg_ref)

        nchunks = tm // ROW_CHUNK
        dA, dGU = {}, {}
        for step in range(nchunks + 2):
            if step < nchunks:
                rows = slice(step * ROW_CHUNK, (step + 1) * ROW_CHUNK)
                dA[step] = _dot_nt(dy_ref[rows, :], wd_ref[...])
            if 1 <= step <= nchunks:
                k = step - 1
                rows = slice(k * ROW_CHUNK, (k + 1) * ROW_CHUNK)
                Gv = G_ref[rows, :].astype(F32)
                Uv = U_ref[rows, :].astype(F32)
                sig = _sigmoid(Gv)
                s = Gv * sig
                dG = (dA[k] * Uv * (sig * (1.0 + Gv * (1.0 - sig)))).astype(BF16)
                dU = (dA.pop(k) * s).astype(BF16)
                dG_ref[rows, :] = dG
                dU_ref[rows, :] = dU
                A_ref[rows, :] = (s * Uv).astype(BF16)
                dGU[k] = (dG, dU)
            if 2 <= step:
                k = step - 2
                rows = slice(k * ROW_CHUNK, (k + 1) * ROW_CHUNK)
                dG, dU = dGU.pop(k)
                acc_ref[rows, :] += _dot(dG, wg_ref[...]) + _dot(dU, wu_ref[...])

        @pl.when(j == NS - 1)
        def _():
            dx, dg = _rms_bwd(x_ref[...], g_ref[...], acc_ref[...])
            dx_ref[...] = dh_ref[...] + dx
            dg_ref[...] += dg

    tok = pl.BlockSpec((tm, D), lambda i, j: (i, 0))
    act = pl.BlockSpec((None, tm, Fs), lambda i, j: (j, i, 0))
    vec = pl.BlockSpec((1, D), lambda i, j: (0, 0))
    return _call(
        body, grid=(T // tm, NS),
        in_specs=[tok, tok, vec, act, act,
                  pl.BlockSpec((None, Fs, D), lambda i, j: (j, 0, 0)),
                  pl.BlockSpec((None, Fs, D), lambda i, j: (j, 0, 0)),
                  pl.BlockSpec((None, Fs, D), lambda i, j: (j, 0, 0))],
        out_specs=[act, act, act, tok, tok, vec],
        out_shape=[jax.ShapeDtypeStruct((NS, T, Fs), BF16)] * 3
        + [jax.ShapeDtypeStruct((T, D), BF16), jax.ShapeDtypeStruct((T, D), F32),
           jax.ShapeDtypeStruct((1, D), F32)],
        scratch_shapes=[pltpu.VMEM((tm, D), F32)],
        sem=("arbitrary", "arbitrary"), name=name, args=(dh, x, gain, G, U, wg, wu, wd), phase=phase)


def _ffn_bwd_w(n, dy, dG, dU, A, *, tk, name, phase=None):
    T, D = n.shape
    NS, _, Fs = dG.shape

    def body(n_ref, dy_ref, dG_ref, dU_ref, A_ref, wg_ref, wu_ref, wd_ref):
        @pl.when(pl.program_id(1) == 0)
        def _():
            wg_ref[...] = jnp.zeros_like(wg_ref)
            wu_ref[...] = jnp.zeros_like(wu_ref)
            wd_ref[...] = jnp.zeros_like(wd_ref)

        nv = n_ref[...]
        wg_ref[...] += _dot_tn(dG_ref[...], nv)
        wu_ref[...] += _dot_tn(dU_ref[...], nv)
        wd_ref[...] += _dot_tn(A_ref[...], dy_ref[...])

    tok = pl.BlockSpec((tk, D), lambda j, k: (k, 0))
    act = pl.BlockSpec((None, tk, Fs), lambda j, k: (j, k, 0))
    return _call(
        body, grid=(NS, T // tk), in_specs=[tok, tok, act, act, act],
        out_specs=[pl.BlockSpec((None, Fs, D), lambda j, k: (j, 0, 0))] * 3,
        out_shape=[jax.ShapeDtypeStruct((NS, Fs, D), F32)] * 3,
        sem=("arbitrary", "arbitrary"), name=name, args=(n, dy, dG, dU, A), phase=phase)


def _inproj_fwd(h, gain, win, *, tm, name, phase=None):
    T, D = h.shape
    NS, _, Cs = win.shape

    def body(h_ref, g_ref, w_ref, u_ref, n_ref):
        @pl.when(pl.program_id(1) == 0)
        def _():
            xv = h_ref[...]
            r = lax.rsqrt(jnp.mean(xv * xv, axis=-1, keepdims=True) + EPS)
            n_ref[...] = (xv * r * g_ref[...]).astype(BF16)

        u_ref[...] = _dot(n_ref[...], w_ref[...])

    tok = pl.BlockSpec((tm, D), lambda i, j: (i, 0))
    return _call(
        body, grid=(T // tm, NS),
        in_specs=[tok, pl.BlockSpec((1, D), lambda i, j: (0, 0)),
                  pl.BlockSpec((None, D, Cs), lambda i, j: (j, 0, 0))],
        out_specs=[pl.BlockSpec((tm, Cs), lambda i, j: (i, j)), tok],
        out_shape=[jax.ShapeDtypeStruct((T, NS * Cs), F32), jax.ShapeDtypeStruct((T, D), BF16)],
        sem=("arbitrary", "arbitrary"), name=name, args=(h, gain, win), phase=phase)


def _inproj_bwd_act(du, dh, h, gain, win, *, tm, name, phase=None):
    T, D = h.shape
    NS, _, Cs = win.shape

    def body(du_ref, dh_ref, h_ref, g_ref, w_ref, dx_ref, dg_ref, acc_ref):
        i = pl.program_id(0)
        j = pl.program_id(1)

        @pl.when(j == 0)
        def _():
            acc_ref[...] = jnp.zeros_like(acc_ref)

        @pl.when((i == 0) & (j == 0))
        def _():
            dg_ref[...] = jnp.zeros_like(dg_ref)

        acc_ref[...] += _dot_nt(du_ref[...], w_ref[...])

        @pl.when(j == NS - 1)
        def _():
            dx, dg = _rms_bwd(h_ref[...], g_ref[...], acc_ref[...])
            dx_ref[...] = dh_ref[...] + dx
            dg_ref[...] += dg

    tok = pl.BlockSpec((tm, D), lambda i, j: (i, 0))
    vec = pl.BlockSpec((1, D), lambda i, j: (0, 0))
    return _call(
        body, grid=(T // tm, NS),
        in_specs=[pl.BlockSpec((tm, Cs), lambda i, j: (i, j)), tok, tok, vec,
                  pl.BlockSpec((None, D, Cs), lambda i, j: (j, 0, 0))],
        out_specs=[tok, vec],
        out_shape=[jax.ShapeDtypeStruct((T, D), F32), jax.ShapeDtypeStruct((1, D), F32)],
        scratch_shapes=[pltpu.VMEM((tm, D), F32)],
        sem=("arbitrary", "arbitrary"), name=name, args=(du, dh, h, gain, win), phase=phase)


def _inproj_bwd_w(n, du, ns, *, tk, name, phase=None):
    T, D = n.shape
    Cs = du.shape[1] // ns

    def body(n_ref, du_ref, w_ref):
        @pl.when(pl.program_id(1) == 0)
        def _():
            w_ref[...] = jnp.zeros_like(w_ref)

        w_ref[...] += _dot_tn(n_ref[...], du_ref[...])

    return _call(
        body, grid=(ns, T // tk),
        in_specs=[pl.BlockSpec((tk, D), lambda j, k: (k, 0)), pl.BlockSpec((tk, Cs), lambda j, k: (k, j))],
        out_specs=[pl.BlockSpec((None, D, Cs), lambda j, k: (j, 0, 0))],
        out_shape=[jax.ShapeDtypeStruct((ns, D, Cs), F32)],
        sem=("arbitrary", "arbitrary"), name=name, args=(n, du), phase=phase)


STRIDE = 4


def _permute(src_ref, tmp_ref, put):
    S = src_ref.shape[0]
    L4, L16 = S // STRIDE, S // (STRIDE * STRIDE)
    put(0, 0, src_ref[...])
    for r0 in range(STRIDE):
        v = src_ref[pl.ds(r0, L4, stride=STRIDE), :]
        put(1, r0 * L4, v)
        tmp_ref[r0 * L4:(r0 + 1) * L4, :] = v
    for r0 in range(STRIDE):
        for r1 in range(STRIDE):
            put(2, (r1 * STRIDE + r0) * L16, tmp_ref[pl.ds(r0 * L4 + r1, L16, stride=STRIDE), :])


def _permute_out(src_ref, tmp_ref, out_ref, cast):
    for cc in range(src_ref.shape[0]):
        cols = slice(cc * LANES, (cc + 1) * LANES)

        def put(p, row0, v, cols=cols):
            out_ref[p, row0:row0 + v.shape[0], cols] = v.astype(cast)

        _permute(src_ref.at[cc], tmp_ref, put)


def _unpermute_in(get_block, dst_ref, tmp_ref, p, S):
    L4, L16 = S // STRIDE, S // (STRIDE * STRIDE)
    if p == 0:
        dst_ref[...] = get_block(0, S)
        return
    if p == 1:
        for r0 in range(STRIDE):
            dst_ref[pl.ds(r0, L4, stride=STRIDE), :] = get_block(r0 * L4, L4)
        return
    for r0 in range(STRIDE):
        for r1 in range(STRIDE):
            tmp_ref[pl.ds(r0 * L4 + r1, L16, stride=STRIDE), :] = get_block((r1 * STRIDE + r0) * L16, L16)
    for r0 in range(STRIDE):
        dst_ref[pl.ds(r0, L4, stride=STRIDE), :] = tmp_ref[r0 * L4:(r0 + 1) * L4, :]


def _qkv_prep(u, gains, B, S, *, name):
    emat = _seg_matrix(D_ATTN)

    def body(u_ref, g_ref, e_ref, out_ref, scr_ref, tmp_ref):
        c = pl.program_id(1)
        xv = u_ref[...]
        ms = _seg_mean(xv * xv, e_ref, HEAD_DIM)
        r = jnp.where(c < 2, lax.rsqrt(ms + EPS), 1.0)
        yv = xv * r * g_ref[...]
        for cc in range(4):
            scr_ref[cc] = yv[:, cc * LANES:(cc + 1) * LANES]
        _permute_out(scr_ref, tmp_ref, out_ref, BF16)

    return pl.pallas_call(
        body, grid=(B, 3),
        in_specs=[pl.BlockSpec((S, D_ATTN), lambda b, c: (b, c)),
                  pl.BlockSpec((None, 1, D_ATTN), lambda b, c: (c, 0, 0)),
                  pl.BlockSpec((D_ATTN, D_ATTN), lambda b, c: (0, 0))],
        out_specs=pl.BlockSpec((None, N_PATTERNS, None, S, D_ATTN), lambda b, c: (c, 0, b, 0, 0)),
        out_shape=jax.ShapeDtypeStruct((3, N_PATTERNS, B, S, D_ATTN), BF16),
        scratch_shapes=[pltpu.VMEM((4, S, LANES), F32), pltpu.VMEM((S, LANES), F32)],
        compiler_params=_params("arbitrary", "arbitrary"), name=name)(u, gains, emat)


def _band_mask(p, b):
    nblk = jnp.right_shift(16, 2 * p)
    has_prev = jnp.bitwise_and(b, nblk - 1) != 0
    qi = lax.broadcasted_iota(jnp.int32, (QBLK, 2 * QBLK), 0)
    ci = lax.broadcasted_iota(jnp.int32, (QBLK, 2 * QBLK), 1)
    dist = QBLK + qi - ci
    return (dist >= 0) & (dist <= QBLK) & (has_prev | (ci >= QBLK))


def _first_head(rows):
    return lax.broadcasted_iota(jnp.int32, (rows, LANES), 1) < HEAD_DIM


def _split_heads(pair):
    first = _first_head(pair.shape[0])
    zero = jnp.zeros_like(pair)
    return jnp.concatenate([jnp.where(first, pair, zero), jnp.where(first, zero, pair)], axis=0)


def _merge_heads(col_a, col_b):
    rows = col_a.shape[0]
    return jnp.where(_first_head(rows), jnp.broadcast_to(col_a, (rows, LANES)), jnp.broadcast_to(col_b, (rows, LANES)))


QB_FWD = 8
QB_BWD = 4


def _attn_fwd(qkv, *, name, phase=None):
    QB = QB_FWD
    nb = qkv.shape[2]

    def body(q_ref, kp_ref, kc_ref, vp_ref, vc_ref, o_ref, lse_ref):
        kall = jnp.concatenate([kp_ref[...]] + [kc_ref[t] for t in range(QB)], axis=0)
        vall = jnp.concatenate([vp_ref[...]] + [vc_ref[t] for t in range(QB)], axis=0)
        masks = []
        for t in range(QB):
            mask = _band_mask(pl.program_id(0), QB * pl.program_id(1) + t)
            masks.append(jnp.concatenate([mask, mask], axis=0))
        units = [(t, hp) for t in range(QB) for hp in range(HEADS // 2)]
        scores, probs = {}, {}
        for step in range(len(units) + 2):
            if step < len(units):
                t, hp = units[step]
                cols = slice(hp * LANES, (hp + 1) * LANES)
                scores[step] = _dot_nt(_split_heads(q_ref[t, :, cols]), kall[t * QBLK:(t + 2) * QBLK, cols])
            if 1 <= step <= len(units):
                t, hp = units[step - 1]
                cols = slice(hp * LANES, (hp + 1) * LANES)
                s = jnp.where(masks[t], scores.pop(step - 1), NEG)
                m = jnp.max(s, axis=-1, keepdims=True)
                e = jnp.exp(s - m)
                l = jnp.sum(e, axis=-1, keepdims=True)
                probs[step - 1] = (e * (1.0 / l)).astype(BF16)
                lse = m + jnp.log(l)
                lse_ref[t, :, cols] = _merge_heads(lse[:QBLK], lse[QBLK:])
            if 2 <= step:
                t, hp = units[step - 2]
                cols = slice(hp * LANES, (hp + 1) * LANES)
                pr = probs.pop(step - 2)
                o_ref[t, :, cols] = _dot(jnp.concatenate([pr[:QBLK], pr[QBLK:]], axis=1),
                                         _split_heads(vall[t * QBLK:(t + 2) * QBLK, cols]))

    cur = lambda which: pl.BlockSpec((None, None, QB, QBLK, D_ATTN), lambda p, i: (which, p, i, 0, 0))
    prev = lambda which: pl.BlockSpec((None, None, None, QBLK, D_ATTN),
                                      lambda p, i: (which, p, jnp.maximum(QB * i - 1, 0), 0, 0))
    out = pl.BlockSpec((None, QB, QBLK, D_ATTN), lambda p, i: (p, i, 0, 0))
    return _call(
        body, grid=(N_PATTERNS, nb // QB), in_specs=[cur(0), prev(1), cur(1), prev(2), cur(2)], out_specs=[out, out],
        out_shape=[jax.ShapeDtypeStruct((N_PATTERNS, nb, QBLK, D_ATTN), F32)] * 2,
        sem=("arbitrary", "arbitrary"), name=name, args=(qkv, qkv, qkv, qkv, qkv), phase=phase)


def _attn_combine(o3, lse3, B, S, *, name):
    def body(o_ref, l_ref, a_ref, lt_ref, so_ref, sl_ref, tmp_ref):
        for p in range(N_PATTERNS):
            _unpermute_in(lambda r0, n, p=p: o_ref[p, pl.ds(r0, n), :], so_ref.at[p], tmp_ref, p, S)
            _unpermute_in(lambda r0, n, p=p: l_ref[p, pl.ds(r0, n), :], sl_ref.at[p], tmp_ref, p, S)
        l0, l1, l2 = sl_ref[0], sl_ref[1], sl_ref[2]
        m = jnp.maximum(jnp.maximum(l0, l1), l2)
        w0, w1, w2 = jnp.exp(l0 - m), jnp.exp(l1 - m), jnp.exp(l2 - m)
        tot = w0 + w1 + w2
        a_ref[...] = (w0 * so_ref[0] + w1 * so_ref[1] + w2 * so_ref[2]) / tot
        lt_ref[...] = m + jnp.log(tot)

    o3 = o3.reshape(N_PATTERNS, B, S, D_ATTN)
    lse3 = lse3.reshape(N_PATTERNS, B, S, D_ATTN)
    inp = pl.BlockSpec((N_PATTERNS, None, S, LANES), lambda b, c: (0, b, 0, c))
    out = pl.BlockSpec((S, LANES), lambda b, c: (b, c))
    return pl.pallas_call(
        body, grid=(B, D_ATTN // LANES), in_specs=[inp, inp], out_specs=[out, out],
        out_shape=[jax.ShapeDtypeStruct((B * S, D_ATTN), F32)] * 2,
        scratch_shapes=[pltpu.VMEM((N_PATTERNS, S, LANES), F32)] * 2 + [pltpu.VMEM((S, LANES), F32)],
        compiler_params=_params("arbitrary", "arbitrary"), name=name)(o3, lse3)


STAT_D = 8


def _attn_bwd_prep(dattn, attn, lse, B, S, *, name):
    emat = _seg_matrix(LANES)
    ncc = D_ATTN // LANES

    def body(da_ref, a_ref, l_ref, e_ref, do_ref, st_ref, scr_ref, nat_ref, tmp_ref):
        cc = pl.program_id(1)
        da = da_ref[...]
        dsum = _seg_mean(da * a_ref[...], e_ref, 1.0)
        scr_ref[...] = da

        def put_do(p, row0, v):
            do_ref[p, row0:row0 + v.shape[0], :] = v.astype(BF16)

        _permute(scr_ref, tmp_ref, put_do)

        lane = lax.broadcasted_iota(jnp.int32, (S, LANES), 1)
        h0 = 2 * cc
        vals = ((h0, l_ref[:, 0:1]), (h0 + 1, l_ref[:, HEAD_DIM:HEAD_DIM + 1]),
                (STAT_D + h0, dsum[:, 0:1]), (STAT_D + h0 + 1, dsum[:, HEAD_DIM:HEAD_DIM + 1]))
        tile = jnp.where(cc == 0, 0.0, nat_ref[...])
        for at, col in vals:
            tile = jnp.where(lane == at, col, tile)
        nat_ref[...] = tile

        @pl.when(cc == ncc - 1)
        def _():
            def put_st(p, row0, v):
                st_ref[p, row0:row0 + v.shape[0], :] = v

            _permute(nat_ref, tmp_ref, put_st)

    inp = pl.BlockSpec((S, LANES), lambda b, c: (b, c))
    return pl.pallas_call(
        body, grid=(B, ncc),
        in_specs=[inp, inp, inp, pl.BlockSpec((LANES, LANES), lambda b, c: (0, 0))],
        out_specs=[pl.BlockSpec((N_PATTERNS, None, S, LANES), lambda b, c: (0, b, 0, c)),
                   pl.BlockSpec((N_PATTERNS, None, S, LANES), lambda b, c: (0, b, 0, 0))],
        out_shape=[jax.ShapeDtypeStruct((N_PATTERNS, B, S, D_ATTN), BF16),
                   jax.ShapeDtypeStruct((N_PATTERNS, B, S, LANES), F32)],
        scratch_shapes=[pltpu.VMEM((S, LANES), F32)] * 3,
        compiler_params=_params("arbitrary", "arbitrary"), name=name)(dattn, attn, lse, emat)


def _attn_bwd(qkv, do3, st3, *, name, phase=None):
    QB = QB_BWD
    nb = qkv.shape[2]
    ngroups = nb // QB

    def body(q_ref, kp_ref, kc_ref, vp_ref, vc_ref, do_ref, st_ref, out_ref, carry_ref):
        p = pl.program_id(0)
        i = pl.program_id(1)

        @pl.when((p == 0) & (i == 0))
        def _():
            carry_ref[...] = jnp.zeros_like(carry_ref)

        kall = jnp.concatenate([kp_ref[...]] + [kc_ref[t] for t in range(QB)], axis=0)
        vall = jnp.concatenate([vp_ref[...]] + [vc_ref[t] for t in range(QB)], axis=0)

        masks = []
        for t in range(QB):
            mask = _band_mask(p, QB * i + t) & (i < ngroups)
            masks.append(jnp.concatenate([mask, mask], axis=0))

        def operands(t, hp):
            cols = slice(hp * LANES, (hp + 1) * LANES)
            kh, vh = kall[t * QBLK:(t + 2) * QBLK, cols], vall[t * QBLK:(t + 2) * QBLK, cols]
            return kh, vh, _split_heads(q_ref[t, :, cols]), _split_heads(do_ref[t, :, cols])

        def stage_scores(t, hp):
            kh, vh, q2, do2 = operands(t, hp)
            return _dot_nt(q2, kh), _dot_nt(do2, vh)

        def stage_softmax(t, hp, s, dp):
            h0, h1 = 2 * hp, 2 * hp + 1
            lse = jnp.concatenate([st_ref[t, :, h0:h0 + 1], st_ref[t, :, h1:h1 + 1]], axis=0)
            dsum = jnp.concatenate([st_ref[t, :, STAT_D + h0:STAT_D + h0 + 1],
                                    st_ref[t, :, STAT_D + h1:STAT_D + h1 + 1]], axis=0)
            pr = jnp.where(masks[t], jnp.exp(s - lse), 0.0)
            return (pr * (dp - dsum)).astype(BF16), pr.astype(BF16)

        def stage_grads(t, hp, ds, prb):
            cols = slice(hp * LANES, (hp + 1) * LANES)
            kh, vh, q2, do2 = operands(t, hp)
            dq = _dot(jnp.concatenate([ds[:QBLK], ds[QBLK:]], axis=1), _split_heads(kh))
            dk, dv = _dot_tn(ds, q2), _dot_tn(prb, do2)
            if t == 0:
                for c in range(3):
                    for tt in range(QB):
                        v = carry_ref[c, tt, :, cols]
                        if tt == QB - 1 and c > 0:
                            v = v + (dk if c == 1 else dv)[:QBLK]
                        out_ref[c, tt, :, cols] = v.astype(BF16)
            else:
                carry_ref[1, t - 1, :, cols] += dk[:QBLK]
                carry_ref[2, t - 1, :, cols] += dv[:QBLK]
            carry_ref[0, t, :, cols] = dq
            carry_ref[1, t, :, cols] = dk[QBLK:]
            carry_ref[2, t, :, cols] = dv[QBLK:]

        units = [(t, hp) for hp in range(HEADS // 2) for t in range(QB)]
        scores, probs = {}, {}
        for step in range(len(units) + 2):
            if step < len(units):
                scores[step] = stage_scores(*units[step])
            if 1 <= step <= len(units):
                probs[step - 1] = stage_softmax(*units[step - 1], *scores.pop(step - 1))
            if 2 <= step:
                stage_grads(*units[step - 2], *probs.pop(step - 2))

    group = lambda i: jnp.minimum(i, ngroups - 1)
    cur = lambda which: pl.BlockSpec((None, None, QB, QBLK, D_ATTN), lambda p, i: (which, p, group(i), 0, 0))
    prev = lambda which: pl.BlockSpec((None, None, None, QBLK, D_ATTN),
                                      lambda p, i: (which, p, jnp.maximum(QB * group(i) - 1, 0), 0, 0))
    aux = lambda lanes: pl.BlockSpec((None, QB, QBLK, lanes), lambda p, i: (p, group(i), 0, 0))
    return _call(
        body, grid=(N_PATTERNS, ngroups + 1),
        in_specs=[cur(0), prev(1), cur(1), prev(2), cur(2), aux(D_ATTN), aux(LANES)],
        out_specs=[pl.BlockSpec((3, None, QB, QBLK, D_ATTN), lambda p, i: (0, p, jnp.maximum(i - 1, 0), 0, 0))],
        out_shape=[jax.ShapeDtypeStruct((3, N_PATTERNS, nb, QBLK, D_ATTN), BF16)],
        scratch_shapes=[pltpu.VMEM((3, QB, QBLK, D_ATTN), F32)],
        sem=("arbitrary", "arbitrary"), name=name, args=(qkv, qkv, qkv, qkv, qkv, do3, st3), phase=phase)


def _attn_grad_combine(cur, u, gains, B, S, *, name, phase=None):
    emat = _seg_matrix(LANES)

    def body(cur_ref, u_ref, g_ref, e_ref, du_ref, dg_ref, scr_ref, tmp_ref):
        c = pl.program_id(0)
        b = pl.program_id(2)
        for p in range(N_PATTERNS):
            _unpermute_in(lambda r0, n, p=p: cur_ref[p, pl.ds(r0, n), :].astype(F32), scr_ref.at[p], tmp_ref, p, S)
        dy = scr_ref[0] + scr_ref[1] + scr_ref[2]
        xv = u_ref[...]
        gain = g_ref[...]
        ms = _seg_mean(xv * xv, e_ref, HEAD_DIM)
        r = lax.rsqrt(ms + EPS)
        xhat = xv * r
        dxh = dy * gain
        dx = r * (dxh - xhat * _seg_mean(dxh * xhat, e_ref, HEAD_DIM))
        du_ref[...] = jnp.where(c < 2, dx, dy).astype(BF16)

        @pl.when((b == 0))
        def _():
            dg_ref[...] = jnp.zeros_like(dg_ref)

        dg_ref[...] += jnp.sum(dy * xhat, axis=0, keepdims=True)

    cur = cur.reshape(3, N_PATTERNS, B, S, D_ATTN)
    ncc = D_ATTN // LANES
    return _call(
        body, grid=(3, ncc, B),
        in_specs=[pl.BlockSpec((None, N_PATTERNS, None, S, LANES), lambda c, cc, b: (c, 0, b, 0, cc)),
                  pl.BlockSpec((S, LANES), lambda c, cc, b: (b, c * ncc + cc)),
                  pl.BlockSpec((None, 1, LANES), lambda c, cc, b: (c, 0, cc)),
                  pl.BlockSpec((LANES, LANES), lambda c, cc, b: (0, 0))],
        out_specs=[pl.BlockSpec((S, LANES), lambda c, cc, b: (b, c * ncc + cc)),
                   pl.BlockSpec((None, 1, LANES), lambda c, cc, b: (c, 0, cc))],
        out_shape=[jax.ShapeDtypeStruct((B * S, 3 * D_ATTN), BF16), jax.ShapeDtypeStruct((3, 1, D_ATTN), F32)],
        scratch_shapes=[pltpu.VMEM((N_PATTERNS, S, LANES), F32), pltpu.VMEM((S, LANES), F32)],
        sem=("arbitrary", "arbitrary", "arbitrary"), name=name, args=(cur, u, gains, emat), phase=phase)


HALO = 32
SUB = 64
SUBLANES = 8


def _shifted_copies(src_ref, sh_ref, tc):
    sh_ref[0] = src_ref[...]
    for r in range(1, SUBLANES):
        sh_ref[r, 0:tc + HALO - SUBLANES, :] = src_ref[pl.ds(r, tc + HALO - SUBLANES), :]


def _shifted(sh_ref, start, size):
    return sh_ref[start % SUBLANES, pl.ds(start - start % SUBLANES, size), :]


def _conv_fwd(u, cw, cb, lg, lb, B, S, *, tc, name, phase=None):
    nchunk = S // tc
    hb = tc // HALO

    def body(ca_ref, cap_ref, cg_ref, cgp_ref, w_ref, cb_ref, lg_ref, lb_ref, cv_ref, glu_ref, y_ref, pad_ref, sh_ref):
        i = pl.program_id(1)
        glu = ca_ref[...] * _sigmoid(cg_ref[...])
        glu_ref[...] = glu
        prev = cap_ref[...] * _sigmoid(cgp_ref[...])
        pad_ref[0:HALO, :] = jnp.where(i > 0, prev, 0.0)
        pad_ref[HALO:, :] = glu
        _shifted_copies(pad_ref, sh_ref, tc)
        for sub in range(tc // SUB):
            acc = jnp.zeros((SUB, D_CONV), F32) + cb_ref[...]
            for k in range(CONV_K):
                acc = acc + _shifted(sh_ref, sub * SUB + HALO - (CONV_K - 1) + k, SUB) * w_ref[pl.ds(k, 1), :]
            y_ref[sub * SUB:(sub + 1) * SUB, :] = acc
        y = y_ref[...]
        mu = jnp.mean(y, axis=-1, keepdims=True)
        yc = y - mu
        var = jnp.mean(yc * yc, axis=-1, keepdims=True)
        z = yc * lax.rsqrt(var + EPS) * lg_ref[...] + lb_ref[...]
        cv_ref[...] = (z * _sigmoid(z)).astype(BF16)

    def cur(col):
        return pl.BlockSpec((tc, D_CONV), lambda b, i: (b * nchunk + i, col))

    def halo(col):
        return pl.BlockSpec((HALO, D_CONV), lambda b, i: (jnp.maximum((b * nchunk + i) * hb - 1, 0), col))

    vec = pl.BlockSpec((1, D_CONV), lambda b, i: (0, 0))
    out = pl.BlockSpec((tc, D_CONV), lambda b, i: (b * nchunk + i, 0))
    return _call(
        body, grid=(B, nchunk),
        in_specs=[cur(3), halo(3), cur(4), halo(4), pl.BlockSpec((CONV_K, D_CONV), lambda b, i: (0, 0)), vec, vec, vec],
        out_specs=[out, out, out],
        out_shape=[jax.ShapeDtypeStruct((B * S, D_CONV), BF16), jax.ShapeDtypeStruct((B * S, D_CONV), F32),
                   jax.ShapeDtypeStruct((B * S, D_CONV), F32)],
        scratch_shapes=[pltpu.VMEM((tc + HALO, D_CONV), F32), pltpu.VMEM((SUBLANES, tc + HALO, D_CONV), F32)],
        sem=("arbitrary", "arbitrary"), name=name, args=(u, u, u, u, cw, cb, lg, lb), phase=phase)


def _conv_bwd_norm(dcv, y, lg, lb, *, tc, name):
    T = y.shape[0]

    def body(dcv_ref, y_ref, lg_ref, lb_ref, dy_ref, part_ref):
        yv = y_ref[...]
        mu = jnp.mean(yv, axis=-1, keepdims=True)
        yc = yv - mu
        var = jnp.mean(yc * yc, axis=-1, keepdims=True)
        rstd = lax.rsqrt(var + EPS)
        xhat = yc * rstd
        z = xhat * lg_ref[...] + lb_ref[...]
        sig = _sigmoid(z)
        dz = dcv_ref[...] * (sig * (1.0 + z * (1.0 - sig)))
        dxh = dz * lg_ref[...]
        dy = rstd * (dxh - jnp.mean(dxh, axis=-1, keepdims=True)
                     - xhat * jnp.mean(dxh * xhat, axis=-1, keepdims=True))
        dy_ref[...] = dy

        @pl.when(pl.program_id(0) == 0)
        def _():
            part_ref[...] = jnp.zeros_like(part_ref)

        part_ref[0:1, :] += jnp.sum(dz * xhat, axis=0, keepdims=True)
        part_ref[1:2, :] += jnp.sum(dz, axis=0, keepdims=True)
        part_ref[2:3, :] += jnp.sum(dy, axis=0, keepdims=True)

    tok = pl.BlockSpec((tc, D_CONV), lambda i: (i, 0))
    vec = pl.BlockSpec((1, D_CONV), lambda i: (0, 0))
    return pl.pallas_call(
        body, grid=(T // tc,), in_specs=[tok, tok, vec, vec],
        out_specs=[tok, pl.BlockSpec((8, D_CONV), lambda i: (0, 0))],
        out_shape=[jax.ShapeDtypeStruct((T, D_CONV), F32), jax.ShapeDtypeStruct((8, D_CONV), F32)],
        compiler_params=_params("arbitrary"), name=name)(dcv, y, lg, lb)


def _conv_bwd_taps(dy, glu, u, cw, B, S, *, tc, name, phase=None):
    nchunk = S // tc
    hb = tc // HALO
    last_hb = B * S // HALO - 1

    def body(dy_ref, dyn_ref, glu_ref, glup_ref, ca_ref, cg_ref, w_ref, dca_ref, dcg_ref, dw_ref,
             dyp_ref, glp_ref, acc_ref, shd_ref, shg_ref):
        b = pl.program_id(0)
        i = pl.program_id(1)
        dy = dy_ref[...]
        dyp_ref[0:tc, :] = dy
        dyp_ref[tc:, :] = jnp.where(i < nchunk - 1, dyn_ref[...], 0.0)
        glp_ref[0:HALO, :] = jnp.where(i > 0, glup_ref[...], 0.0)
        glp_ref[HALO:, :] = glu_ref[...]
        _shifted_copies(dyp_ref, shd_ref, tc)
        _shifted_copies(glp_ref, shg_ref, tc)

        @pl.when((b == 0) & (i == 0))
        def _():
            dw_ref[...] = jnp.zeros_like(dw_ref)

        for sub in range(tc // SUB):
            acc = jnp.zeros((SUB, D_CONV), F32)
            for k in range(CONV_K):
                acc = acc + _shifted(shd_ref, sub * SUB + (CONV_K - 1) - k, SUB) * w_ref[pl.ds(k, 1), :]
            acc_ref[sub * SUB:(sub + 1) * SUB, :] = acc
        for k in range(CONV_K):
            dw_ref[k:k + 1, :] += jnp.sum(dy * _shifted(shg_ref, HALO - (CONV_K - 1) + k, tc), axis=0, keepdims=True)
        dglu = acc_ref[...]
        ca = ca_ref[...]
        sig = _sigmoid(cg_ref[...])
        dca_ref[...] = (dglu * sig).astype(BF16)
        dcg_ref[...] = (dglu * ca * sig * (1.0 - sig)).astype(BF16)

    tok = pl.BlockSpec((tc, D_CONV), lambda b, i: (b * nchunk + i, 0))
    nxt = pl.BlockSpec((HALO, D_CONV), lambda b, i: (jnp.minimum((b * nchunk + i + 1) * hb, last_hb), 0))
    prv = pl.BlockSpec((HALO, D_CONV), lambda b, i: (jnp.maximum((b * nchunk + i) * hb - 1, 0), 0))
    return _call(
        body, grid=(B, nchunk),
        in_specs=[tok, nxt, tok, prv,
                  pl.BlockSpec((tc, D_CONV), lambda b, i: (b * nchunk + i, 3)),
                  pl.BlockSpec((tc, D_CONV), lambda b, i: (b * nchunk + i, 4)),
                  pl.BlockSpec((CONV_K, D_CONV), lambda b, i: (0, 0))],
        out_specs=[tok, tok, pl.BlockSpec((32, D_CONV), lambda b, i: (0, 0))],
        out_shape=[jax.ShapeDtypeStruct((B * S, D_CONV), BF16), jax.ShapeDtypeStruct((B * S, D_CONV), BF16),
                   jax.ShapeDtypeStruct((32, D_CONV), F32)],
        scratch_shapes=[pltpu.VMEM((tc + HALO, D_CONV), F32), pltpu.VMEM((tc + HALO, D_CONV), F32),
                        pltpu.VMEM((tc, D_CONV), F32), pltpu.VMEM((SUBLANES, tc + HALO, D_CONV), F32),
                        pltpu.VMEM((SUBLANES, tc + HALO, D_CONV), F32)],
        sem=("arbitrary", "arbitrary"), name=name, args=(dy, dy, glu, glu, u, u, cw), phase=phase)


def _outproj_fwd(h, attn, cv, wout, *, tm, name):
    T, D = h.shape

    def body(h_ref, a_ref, c_ref, w_ref, o_ref):
        o_ref[...] = (h_ref[...] + _dot(a_ref[...].astype(BF16), w_ref[0:D_ATTN, :])
                      + _dot(c_ref[...], w_ref[D_ATTN:, :]))

    tok = pl.BlockSpec((tm, D), lambda i: (i, 0))
    half = pl.BlockSpec((tm, D_ATTN), lambda i: (i, 0))
    return pl.pallas_call(
        body, grid=(T // tm,), in_specs=[tok, half, half, pl.BlockSpec(wout.shape, lambda i: (0, 0))],
        out_specs=tok, out_shape=jax.ShapeDtypeStruct((T, D), F32),
        compiler_params=_params("arbitrary"), name=name)(h, attn, cv, wout)


def _outproj_bwd(dh, attn, cv, wout, *, tm, name):
    T, D = dh.shape

    def body(dh_ref, a_ref, c_ref, w_ref, da_ref, dc_ref, dw_ref):
        @pl.when(pl.program_id(0) == 0)
        def _():
            dw_ref[...] = jnp.zeros_like(dw_ref)

        dhb = dh_ref[...].astype(BF16)
        da_ref[...] = _dot_nt(dhb, w_ref[0:D_ATTN, :])
        dc_ref[...] = _dot_nt(dhb, w_ref[D_ATTN:, :])
        dw_ref[0:D_ATTN, :] += _dot_tn(a_ref[...].astype(BF16), dhb)
        dw_ref[D_ATTN:, :] += _dot_tn(c_ref[...], dhb)

    tok = pl.BlockSpec((tm, D), lambda i: (i, 0))
    half = pl.BlockSpec((tm, D_ATTN), lambda i: (i, 0))
    wspec = pl.BlockSpec(wout.shape, lambda i: (0, 0))
    return pl.pallas_call(
        body, grid=(T // tm,), in_specs=[tok, half, half, wspec], out_specs=[half, half, wspec],
        out_shape=[jax.ShapeDtypeStruct((T, D_ATTN), F32), jax.ShapeDtypeStruct((T, D_ATTN), F32),
                   jax.ShapeDtypeStruct(wout.shape, F32)],
        compiler_params=_params("arbitrary"), name=name)(dh, attn, cv, wout)


ADAM_BLOCK_BYTES = 3 * 512 * 1024


def _adamw(w, g, m, v, *, name):
    R, C = w.shape
    tr = R
    for cand in (512, 352, 256, 176, 128, 64, 32, 16, 8):
        if R % cand == 0 and cand * C * 4 <= ADAM_BLOCK_BYTES:
            tr = cand
            break
    c1 = 1.0 - ADAM_B1 ** ADAM_STEP
    c2 = 1.0 - ADAM_B2 ** ADAM_STEP

    def body(w_ref, g_ref, m_ref, v_ref, d_ref, nm_ref, nv_ref):
        gv = g_ref[...]
        nm = ADAM_B1 * m_ref[...] + (1.0 - ADAM_B1) * gv
        nv = ADAM_B2 * v_ref[...] + (1.0 - ADAM_B2) * (gv * gv)
        d_ref[...] = -ADAM_LR * ((nm / c1) / (jnp.sqrt(nv / c2) + ADAM_EPS) + ADAM_WD * w_ref[...])
        nm_ref[...] = nm
        nv_ref[...] = nv

    blk = pl.BlockSpec((tr, C), lambda i: (i, 0))
    return pl.pallas_call(
        body, grid=(R // tr,), in_specs=[blk] * 4, out_specs=[blk] * 3,
        out_shape=[jax.ShapeDtypeStruct((R, C), F32)] * 3,
        compiler_params=_params("arbitrary"), name=name)(w, g, m, v)


ADAM_SPLIT = 4


def _adamw_many(ws, gs, ms, vs, *, name, phase=None):
    n = len(ws)
    c1 = 1.0 - ADAM_B1 ** ADAM_STEP
    c2 = 1.0 - ADAM_B2 ** ADAM_STEP

    def body(*refs):
        ins, outs = refs[:4 * n], refs[4 * n:]
        for a in range(n):
            w_ref, g_ref, m_ref, v_ref = ins[4 * a:4 * a + 4]
            gv = g_ref[...]
            nm = ADAM_B1 * m_ref[...] + (1.0 - ADAM_B1) * gv
            nv = ADAM_B2 * v_ref[...] + (1.0 - ADAM_B2) * (gv * gv)
            outs[3 * a][...] = -ADAM_LR * ((nm / c1) / (jnp.sqrt(nv / c2) + ADAM_EPS) + ADAM_WD * w_ref[...])
            outs[3 * a + 1][...] = nm
            outs[3 * a + 2][...] = nv

    in_specs, out_specs, out_shape, args = [], [], [], []
    for w, g, m, v in zip(ws, gs, ms, vs):
        R, C = w.shape
        blk = pl.BlockSpec((R // ADAM_SPLIT, C), lambda i: (i, 0))
        in_specs += [blk] * 4
        out_specs += [blk] * 3
        out_shape += [jax.ShapeDtypeStruct((R, C), F32)] * 3
        args += [w, g, m, v]
    res = _call(body, grid=(ADAM_SPLIT,), in_specs=in_specs, out_specs=out_specs, out_shape=out_shape,
                sem=("arbitrary",), name=name, args=args, phase=phase)
    outs, extra = res if phase is not None else (res, None)
    return list(outs[0::3]), list(outs[1::3]), list(outs[2::3]), extra


TM = 512
TM_WIDE = 1024
TK = 1024
TC = 256


def _local_step(x, tgt, w, overlap=None):
    B, S, D = x.shape
    T = B * S
    x2 = x.reshape(T, D)
    t2 = tgt.reshape(T, D)
    ones = jnp.ones((1, D_ATTN), F32)
    scale = HEAD_DIM ** -0.5
    gains = jnp.stack([jnp.tile(w["q_norm"], (1, HEADS)) * scale, jnp.tile(w["k_norm"], (1, HEADS)), ones])
    g = {}

    def hosting(point, build):
        phase = overlap.phase(point, w, g) if overlap is not None else None
        if phase is None:
            return build(None)
        outs, extra = build(phase)
        overlap.done(point, extra, w, g)
        return outs

    h1, n1, G1, U1 = hosting("ffn1_fwd", lambda ph: _ffn_fwd(
        x2, w["ffn1_norm"], w["wg1"], w["wu1"], w["wd1"], None, tm=TM_WIDE, name="ffn1_fwd", phase=ph))
    u, n2 = hosting("inproj_fwd", lambda ph: _inproj_fwd(h1, w["mix_norm"], w["win"], tm=TM_WIDE, name="inproj_fwd", phase=ph))
    qkv = _qkv_prep(u, gains, B, S, name="qkv_prep")
    qkv = qkv.reshape(3, N_PATTERNS, T // QBLK, QBLK, D_ATTN)
    o3, lse3 = hosting("attn_fwd", lambda ph: _attn_fwd(qkv, name="attn_fwd", phase=ph))
    attn, lse = _attn_combine(o3, lse3, B, S, name="attn_combine")
    cv, glu, yconv = hosting("conv_fwd", lambda ph: _conv_fwd(
        u, w["conv_w"], w["conv_b"], w["conv_ln_g"], w["conv_ln_b"], B, S, tc=TC, name="conv_fwd", phase=ph))
    h2 = _outproj_fwd(h1, attn, cv, w["wout"], tm=TM_WIDE, name="outproj_fwd")
    dh3, n3, G2, U2, loss = _ffn_fwd(h2, w["ffn2_norm"], w["wg2"], w["wu2"], w["wd2"], t2, tm=TM_WIDE, name="ffn2_fwd")

    dG, dU, A, dy, dh2, g["ffn2_norm"] = _ffn_bwd_act(dh3, h2, w["ffn2_norm"], G2, U2, w["wg2"], w["wu2"], w["wd2"],
                                                    tm=TM, name="ffn2_bwd_act")
    g["wg2"], g["wu2"], g["wd2"] = _ffn_bwd_w(n3, dy, dG, dU, A, tk=TK, name="ffn2_bwd_w")
    dattn, dcv, g["wout"] = _outproj_bwd(dh2, attn, cv, w["wout"], tm=TM_WIDE, name="outproj_bwd")
    dyc, cpart = _conv_bwd_norm(dcv, yconv, w["conv_ln_g"], w["conv_ln_b"], tc=TC, name="conv_bwd_norm")
    dca, dcg, dcw = hosting("conv_bwd_taps", lambda ph: _conv_bwd_taps(
        dyc, glu, u, w["conv_w"], B, S, tc=TC, name="conv_bwd_taps", phase=ph))
    do3, st3 = _attn_bwd_prep(dattn, attn, lse, B, S, name="attn_bwd_prep")
    nb = T // QBLK
    (cur,) = hosting("attn_bwd", lambda ph: _attn_bwd(
        qkv, do3.reshape(N_PATTERNS, nb, QBLK, D_ATTN), st3.reshape(N_PATTERNS, nb, QBLK, LANES),
        name="attn_bwd", phase=ph))
    du_qkv, dgains = hosting("attn_grad_combine", lambda ph: _attn_grad_combine(
        cur, u, gains, B, S, name="attn_grad_combine", phase=ph))
    du = jnp.concatenate([du_qkv, dca, dcg], axis=1)
    (g["win"],) = hosting("inproj_bwd_w", lambda ph: _inproj_bwd_w(
        n2, du, w["win"].shape[0], tk=2 * TK, name="inproj_bwd_w", phase=ph))
    dh1, g["mix_norm"] = hosting("inproj_bwd_act", lambda ph: _inproj_bwd_act(
        du, dh2, h1, w["mix_norm"], w["win"], tm=TM_WIDE, name="inproj_bwd_act", phase=ph))
    dG, dU, A, dy, dx, g["ffn1_norm"] = hosting("ffn1_bwd_act", lambda ph: _ffn_bwd_act(
        dh1, x2, w["ffn1_norm"], G1, U1, w["wg1"], w["wu1"], w["wd1"], tm=TM, name="ffn1_bwd_act", phase=ph))
    g["wg1"], g["wu1"], g["wd1"] = hosting("ffn1_bwd_w", lambda ph: _ffn_bwd_w(
        n1, dy, dG, dU, A, tk=TK, name="ffn1_bwd_w", phase=ph))

    g["q_norm"] = dgains[0].reshape(HEADS, HEAD_DIM).sum(axis=0, keepdims=True) * scale
    g["k_norm"] = dgains[1].reshape(HEADS, HEAD_DIM).sum(axis=0, keepdims=True)
    g["conv_ln_g"] = cpart[0:1]
    g["conv_ln_b"] = cpart[1:2]
    g["conv_b"] = cpart[2:3]
    g["conv_w"] = dcw[:CONV_K]
    return loss, dx.reshape(B, S, D), g


N_CHIPS = 4
N_DEV = 8
VMEM_SPEC = pl.BlockSpec(memory_space=pltpu.VMEM)


def _remote(src, dst, send_sem, recv_sem, device):
    return pltpu.make_async_remote_copy(src_ref=src, dst_ref=dst, send_sem=send_sem, recv_sem=recv_sem,
                                        device_id=device, device_id_type=MESH)


def _stage_shards(shards, dtypes, *, name):
    n = len(shards)
    halves = [s.reshape(2, s.shape[0] // 2, s.shape[1]) for s in shards]

    def body(*refs):
        ins, outs, vms, loc_sems = refs[:n], refs[n:2 * n], refs[2 * n:3 * n], refs[3 * n]
        me = 2 * lax.axis_index("x") + lax.axis_index("y")
        copies = []
        for a in range(n):
            vms[a][...] = ins[a][...].astype(dtypes[a])
            cp = pltpu.make_async_copy(vms[a], outs[a].at[me], loc_sems.at[a])
            cp.start()
            copies.append(cp)
        for cp in copies:
            cp.wait()

    return pl.pallas_call(
        body, in_specs=[VMEM_SPEC] * n, out_specs=[ANY] * n,
        out_shape=[jax.ShapeDtypeStruct((N_CHIPS,) + h.shape, dt) for h, dt in zip(halves, dtypes)],
        scratch_shapes=[pltpu.VMEM(h.shape, dt) for h, dt in zip(halves, dtypes)] + [DMA_SEMS((n,))],
        compiler_params=pltpu.CompilerParams(vmem_limit_bytes=VMEM_LIMIT), name=name)(*halves)


def _like(arrays):
    return [jax.ShapeDtypeStruct(a.shape, a.dtype) for a in arrays]


def _axes():
    x, y, c = lax.axis_index("x"), lax.axis_index("y"), lax.axis_index("c")
    first = (x + (1 - c) * (1 - 2 * x), y + c * (1 - 2 * y))
    second = (x + c * (1 - 2 * x), y + (1 - c) * (1 - 2 * y))
    slots = tuple(2 * px + py for px, py in ((x, y), first, second, (1 - x, 1 - y)))
    return (x, y, c), (*first, c), (*second, c), slots


def _gather_ici_phase(bufs, only=None):
    n = len(bufs)

    def stage1(ins, outs, sems):
        (x, y, c), peer1, peer2, (own, s1, s2, both) = _axes()
        starts, arrivals = [], []
        for a in range(n):
            mine, land = outs[a].at[own, c], outs[a].at[s2, c]
            starts.append(_remote(mine, mine, *sems(a), peer2))
            arrivals.append(_remote(land, land, *sems(a), peer2))
        return starts, arrivals

    def stage2(ins, outs, sems):
        (x, y, c), peer1, peer2, (own, s1, s2, both) = _axes()
        starts, arrivals = [], []
        for a in range(n):
            for k, (src, dst) in enumerate(((own, s1), (s2, both))):
                mine, land = outs[a].at[src, c], outs[a].at[dst, c]
                starts.append(_remote(mine, mine, *sems(2 * a + k), peer1))
                arrivals.append(_remote(land, land, *sems(2 * a + k), peer1))
        return starts, arrivals

    same = {a: a for a in range(n)}
    first, second = _Phase(bufs, _like(bufs), same, n, stage1), _Phase(bufs, _like(bufs), same, 2 * n, stage2)
    if only is None:
        return first.then(second)
    return first if only == 1 else second


def _gather_d2d_phase(bufs):
    n = len(bufs)

    def copies(ins, outs, sems):
        (x, y, c), peer1, peer2, (own, s1, s2, both) = _axes()
        starts, arrivals = [], []
        for a in range(n):
            for j, s in enumerate((s1, s2, both)):
                got, land = outs[a].at[s, c], outs[a].at[s, 1 - c]
                starts.append(_remote(got, got, *sems(3 * a + j), (x, y, 1 - c)))
                arrivals.append(_remote(land, land, *sems(3 * a + j), (x, y, 1 - c)))
        return starts, arrivals

    return _Phase(bufs, _like(bufs), {a: a for a in range(n)}, 3 * n, copies)


def _gather_pipelined_phase(bufs):
    n = len(bufs)

    def piece(kind, a):
        def copies(ins, outs, sems):
            (x, y, c), peer1, peer2, (own, s1, s2, both) = _axes()
            moves = {1: [(own, s2, c, c, peer2)],
                     2: [(own, s1, c, c, peer1), (s2, both, c, c, peer1)],
                     3: [(s, s, c, 1 - c, (x, y, 1 - c)) for s in (s1, s2, both)]}[kind]
            starts, arrivals = [], []
            for k, (src, dst, h_src, h_dst, peer) in enumerate(moves):
                mine, land = outs[a].at[src, h_src], outs[a].at[dst, h_dst]
                starts.append(_remote(mine, mine, *sems(k), peer))
                arrivals.append(_remote(land, land, *sems(k), peer))
            return starts, arrivals

        return kind, copies

    def beside(pieces):
        def copies(ins, outs, sems):
            starts, arrivals, base = [], [], 0
            for count, fn in pieces:
                s, r = fn(ins, outs, lambda i, base=base: sems(base + i))
                starts, arrivals, base = starts + s, arrivals + r, base + count
            return starts, arrivals

        return sum(count for count, _ in pieces), copies

    phase = _Phase(bufs, _like(bufs), {a: a for a in range(n)}, 0, None)
    phase.stages = [beside([piece(kind, t - kind + 1) for kind in (1, 2, 3) if 0 <= t - kind + 1 < n])
                    for t in range(n + 2)]
    return phase


def _exchange_phase(views):
    n = len(views)

    def copies(ins, outs, sems):
        x, y, c = lax.axis_index("x"), lax.axis_index("y"), lax.axis_index("c")
        starts = [_remote(ins[a].at[pl.ds(0, ins[a].shape[0]), 1 - c], outs[a], *sems(a), (x, y, 1 - c))
                  for a in range(n)]
        return starts, starts

    outs = [jax.ShapeDtypeStruct((v.shape[0],) + v.shape[2:], F32) for v in views]
    return _Phase(views, outs, {}, n, copies)


ADD_SPLIT = 2


def _add_halves(views, got, sel, tag):
    n = len(views)

    def body(s_ref, *refs):
        ins, outs = refs[:4 * n], refs[4 * n:]
        for a in range(n):
            gk, rk, gs, rs = ins[4 * a:4 * a + 4]
            outs[2 * a][...] = gk[...] + rk[...]
            outs[2 * a + 1][...] = (gs[...] + rs[...]).astype(BF16)

    in_specs, out_specs, out_shape, args = [], [], [], []
    for g, r in zip(views, got):
        _, _, rh, cdim = g.shape
        tr = rh // ADD_SPLIT
        for off in (0, 2):
            in_specs.append(pl.BlockSpec((None, None, tr, cdim), lambda k, i, s, off=off: (s[1 + off + k], s[0], i, 0)))
            in_specs.append(pl.BlockSpec((None, tr, cdim), lambda k, i, s, off=off: (s[1 + off + k], i, 0)))
            args += [g, r]
        out_specs += [pl.BlockSpec((None, tr, cdim), lambda k, i, s: (k, i, 0))] * 2
        out_shape += [jax.ShapeDtypeStruct((2, rh, cdim), F32), jax.ShapeDtypeStruct((2, rh, cdim), BF16)]
    res = pl.pallas_call(
        body,
        grid_spec=pltpu.PrefetchScalarGridSpec(num_scalar_prefetch=1, grid=(2, ADD_SPLIT), in_specs=in_specs,
                                               out_specs=out_specs),
        out_shape=out_shape, compiler_params=_params("arbitrary", "arbitrary"), name=f"rs_add_half_{tag}")(sel, *args)
    return list(res[0::2]), list(res[1::2])


def _swap_phase(arrays, stage):
    n = len(arrays)

    def copies(ins, outs, sems):
        peer = _axes()[stage]
        starts = [_remote(ins[a], outs[a], *sems(a), peer) for a in range(n)]
        return starts, starts

    return _Phase(arrays, _like(arrays), {}, n, copies)


def _add_first(keep, got, tag):
    n = len(keep)

    def body(*refs):
        ins, outs = refs[:2 * n], refs[2 * n:]
        for a in range(n):
            k_ref, g_ref = ins[2 * a], ins[2 * a + 1]
            outs[2 * a][...] = k_ref[0] + g_ref[0].astype(F32)
            outs[2 * a + 1][...] = (k_ref[1] + g_ref[1].astype(F32)).astype(BF16)

    in_specs, out_specs, out_shape, args = [], [], [], []
    for k, g in zip(keep, got):
        _, rh, cdim = k.shape
        tr = rh // ADD_SPLIT
        in_specs += [pl.BlockSpec((2, tr, cdim), lambda i: (0, i, 0))] * 2
        out_specs += [pl.BlockSpec((tr, cdim), lambda i: (i, 0))] * 2
        out_shape += [jax.ShapeDtypeStruct((rh, cdim), F32), jax.ShapeDtypeStruct((rh, cdim), BF16)]
        args += [k, g]
    res = pl.pallas_call(body, grid=(ADD_SPLIT,), in_specs=in_specs, out_specs=out_specs, out_shape=out_shape,
                         compiler_params=_params("arbitrary"), name=f"rs_add_first_{tag}")(*args)
    return list(res[0::2]), list(res[1::2])


def _add_second(keep, got, sel, tag):
    n = len(keep)

    def body(s_ref, *refs):
        ins, outs = refs[:2 * n], refs[2 * n:]
        for a in range(n):
            outs[a][...] = ins[2 * a][...] + ins[2 * a + 1][...].astype(F32)

    in_specs, out_specs, out_shape, args = [], [], [], []
    for k, g in zip(keep, got):
        rh, cdim = k.shape
        tr = rh // ADD_SPLIT
        in_specs += [pl.BlockSpec((tr, cdim), lambda i, s: (i, 0))] * 2
        out_specs.append(pl.BlockSpec((None, tr, cdim), lambda i, s: (s[0], i, 0)))
        out_shape.append(jax.ShapeDtypeStruct((2, rh, cdim), F32))
        args += [k, g]
    res = pl.pallas_call(
        body,
        grid_spec=pltpu.PrefetchScalarGridSpec(num_scalar_prefetch=1, grid=(ADD_SPLIT,), in_specs=in_specs,
                                               out_specs=out_specs),
        out_shape=out_shape, compiler_params=_params("arbitrary"), name=f"rs_add_second_{tag}")(sel, *args)
    return list(res)


def _join_phase(halves):
    n = len(halves)

    def copies(ins, outs, sems):
        x, y, c = lax.axis_index("x"), lax.axis_index("y"), lax.axis_index("c")
        starts, arrivals = [], []
        for a in range(n):
            mine, land = outs[a].at[c], outs[a].at[1 - c]
            starts.append(_remote(mine, mine, *sems(a), (x, y, 1 - c)))
            arrivals.append(_remote(land, land, *sems(a), (x, y, 1 - c)))
        return starts, arrivals

    return _Phase(halves, _like(halves), {a: a for a in range(n)}, n, copies)


def _slot_order():
    x, y, c = lax.axis_index("x"), lax.axis_index("y"), lax.axis_index("c")
    own, flip_x, flip_y, both = 2 * x + y, 2 * (1 - x) + y, 2 * x + 1 - y, 2 * (1 - x) + 1 - y
    first = jnp.where(c == 0, flip_x, flip_y)
    second = jnp.where(c == 0, flip_y, flip_x)
    return jnp.stack([c, own, second, first, both]).astype(jnp.int32)


def _half_view(g):
    return g.reshape(N_CHIPS, 2, g.shape[1] // 2, g.shape[2])


EARLY_GRADS = ("wg2", "wu2", "wd2", "wout")
MIDDLE_GRADS = ("win",)


class _Overlap:
    EARLY_AT = ("conv_bwd_taps", "attn_bwd", "attn_grad_combine", "inproj_bwd_w")
    MIDDLE_AT = ("inproj_bwd_act", "ffn1_bwd_act", "ffn1_bwd_w", None)

    def __init__(self, staged):
        self.staged = staged
        self.ffn2 = list(staged[3:])
        self.sel = sel = _slot_order()
        self.early = _Reduction(EARLY_GRADS, "early", sel)
        self.middle = _Reduction(MIDDLE_GRADS, "middle", sel)

    def finish_late(self, late):
        views = [_half_view(a) for a in late]
        got = _run_phase(_exchange_phase(views), name="rs_exchange_halves")
        keep, send = _add_halves(views, got, self.sel, "late")
        got = _run_phase(_swap_phase(send, 1), name="rs_swap_first_axis")
        keep, send = _add_first(keep, got, "late")
        got = _run_phase(_swap_phase(send, 2), name="rs_swap_second_axis")
        halves = _add_second(keep, got, self.sel, "late")
        full = _run_phase(_join_phase(halves + list(self.middle.halves)), name="rs_join_halves")
        return [f.reshape(-1, f.shape[-1]) for f in full]

    def phase(self, point, w, g):
        if point == "ffn1_fwd":
            return _gather_ici_phase(self.staged[:3])
        if point == "inproj_fwd":
            return _gather_ici_phase(self.ffn2, only=1)
        if point == "attn_fwd":
            return _gather_ici_phase(self.ffn2, only=2)
        if point == "conv_fwd":
            return _gather_d2d_phase(self.ffn2)
        for red, at in ((self.early, self.EARLY_AT), (self.middle, self.MIDDLE_AT)):
            if point in at:
                return red.phase(at.index(point), g)
        return None

    def done(self, point, outs, w, g):
        if point == "ffn1_fwd":
            win, wout, taps = [_whole(b) for b in _run_phase(_gather_d2d_phase(outs), name="gather_mix_d2d")]
            w["win"] = win
            w["wout"] = wout.reshape(-1, wout.shape[-1])
            w["conv_w"] = taps.transpose(1, 0, 2).reshape(CONV_K + 1, D_CONV)[:CONV_K]
        elif point in ("inproj_fwd", "attn_fwd"):
            self.ffn2 = list(outs)
        elif point == "conv_fwd":
            w["wg2"], w["wu2"], w["wd2"] = [_whole(b) for b in outs]
        for red, at in ((self.early, self.EARLY_AT), (self.middle, self.MIDDLE_AT)):
            if point in at:
                red.done(at.index(point), outs)


class _Reduction:
    def __init__(self, names, tag, sel):
        self.names, self.tag, self.sel = names, tag, sel
        self.reduced = {}

    def phase(self, stage, g):
        if stage == 0:
            self.cols = [g[k].shape[-1] for k in self.names]
            self.views = [_half_view(g[k].reshape(N_CHIPS, -1, g[k].shape[-1])) for k in self.names]
            return _exchange_phase(self.views)
        if stage in (1, 2):
            return _swap_phase(self.send, stage)
        return _join_phase(self.halves)

    def done(self, stage, outs):
        if stage == 0:
            self.keep, self.send = _add_halves(self.views, outs, self.sel, self.tag)
        elif stage == 1:
            self.keep, self.send = _add_first(self.keep, outs, self.tag)
        elif stage == 2:
            self.halves = _add_second(self.keep, outs, self.sel, self.tag)
        else:
            for k, c, f in zip(self.names, self.cols, outs):
                self.reduced[k] = f.reshape(-1, c)


def _whole(buf):
    return buf.reshape(buf.shape[0], 2 * buf.shape[2], buf.shape[3])


def _allreduce_small(pack, *, name):
    rows = pack.shape[0]

    def body(p_ref, o_ref, buf_ref, send_sems, recv_sems):
        x, y, c = lax.axis_index("x"), lax.axis_index("y"), lax.axis_index("c")
        me = 4 * x + 2 * y + c
        buf_ref[me] = p_ref[...]
        cps = []
        for k in range(1, N_DEV):
            peer = tuple(1 - v if (k >> s) & 1 else v for v, s in ((x, 2), (y, 1), (c, 0)))
            cp = _remote(p_ref, buf_ref.at[me], send_sems.at[k - 1], recv_sems.at[k - 1], peer)
            cp.start()
            cps.append(cp)
        for k in range(1, N_DEV):
            src = 4 * (x ^ ((k >> 2) & 1)) + 2 * (y ^ ((k >> 1) & 1)) + (c ^ (k & 1))
            land = buf_ref.at[src]
            _remote(land, land, send_sems.at[k - 1], recv_sems.at[k - 1], (x, y, c)).wait_recv()
        acc = buf_ref[0]
        for d in range(1, N_DEV):
            acc = acc + buf_ref[d]
        o_ref[...] = acc
        for cp in cps:
            cp.wait_send()

    return pl.pallas_call(
        body, in_specs=[VMEM_SPEC], out_specs=VMEM_SPEC, out_shape=jax.ShapeDtypeStruct(pack.shape, F32),
        scratch_shapes=[pltpu.VMEM((N_DEV, rows, LANES), F32), pltpu.SemaphoreType.DMA((N_DEV - 1,)),
                        pltpu.SemaphoreType.DMA((N_DEV - 1,))], name=name)(pack)


SMALL = ("ffn1_norm", "mix_norm", "q_norm", "k_norm", "conv_b", "conv_ln_g", "conv_ln_b", "ffn2_norm", "conv_w")
BIG = ("ffn1_w_gate", "ffn1_w_up", "ffn1_w_down", "w_in", "w_out", "ffn2_w_gate", "ffn2_w_up", "ffn2_w_down")
TRANSPOSED = ("ffn1_w_gate", "ffn1_w_up", "ffn2_w_gate", "ffn2_w_up")
WEIGHTS = ("ffn1_norm", "ffn1_w_gate", "ffn1_w_up", "ffn1_w_down", "mix_norm", "w_in", "q_norm", "k_norm",
           "conv_w", "conv_b", "conv_ln_g", "conv_ln_b", "w_out", "ffn2_norm", "ffn2_w_gate", "ffn2_w_up",
           "ffn2_w_down")


def _pack(parts):
    rows = []
    for p in parts:
        flat = p.reshape(-1)
        tile = SUBLANES * LANES
        padded = -(-flat.shape[0] // tile) * tile
        rows.append(jnp.pad(flat, (0, padded - flat.shape[0])).reshape(-1, LANES))
    return jnp.concatenate(rows, axis=0)


def _unpack(pack, shapes):
    out, row = [], 0
    for shp in shapes:
        size = shp[0] * shp[1]
        tile = SUBLANES * LANES
        nrows = -(-size // tile) * SUBLANES
        out.append(pack[row:row + nrows].reshape(-1)[:size].reshape(shp))
        row += nrows
    return out


def kernel(x, ffn1_norm, ffn1_w_gate, ffn1_w_up, ffn1_w_down, mix_norm, w_in, q_norm, k_norm, conv_w, conv_b, conv_ln_g, conv_ln_b, w_out, ffn2_norm, ffn2_w_gate, ffn2_w_up, ffn2_w_down, loss_target, m_ffn1_norm, m_ffn1_w_gate, m_ffn1_w_up, m_ffn1_w_down, m_mix_norm, m_w_in, m_q_norm, m_k_norm, m_conv_w, m_conv_b, m_conv_ln_g, m_conv_ln_b, m_w_out, m_ffn2_norm, m_ffn2_w_gate, m_ffn2_w_up, m_ffn2_w_down, v_ffn1_norm, v_ffn1_w_gate, v_ffn1_w_up, v_ffn1_w_down, v_mix_norm, v_w_in, v_q_norm, v_k_norm, v_conv_w, v_conv_b, v_conv_ln_g, v_conv_ln_b, v_w_out, v_ffn2_norm, v_ffn2_w_gate, v_ffn2_w_up, v_ffn2_w_down):
    wts = dict(ffn1_norm=ffn1_norm, ffn1_w_gate=ffn1_w_gate[0], ffn1_w_up=ffn1_w_up[0], ffn1_w_down=ffn1_w_down[0],
               mix_norm=mix_norm, w_in=w_in[0], q_norm=q_norm, k_norm=k_norm, conv_w=conv_w[0], conv_b=conv_b,
               conv_ln_g=conv_ln_g, conv_ln_b=conv_ln_b, w_out=w_out[0], ffn2_norm=ffn2_norm,
               ffn2_w_gate=ffn2_w_gate[0], ffn2_w_up=ffn2_w_up[0], ffn2_w_down=ffn2_w_down[0])
    mom = dict(ffn1_norm=m_ffn1_norm, ffn1_w_gate=m_ffn1_w_gate[0], ffn1_w_up=m_ffn1_w_up[0], ffn1_w_down=m_ffn1_w_down[0],
               mix_norm=m_mix_norm, w_in=m_w_in[0], q_norm=m_q_norm, k_norm=m_k_norm, conv_w=m_conv_w[0], conv_b=m_conv_b,
               conv_ln_g=m_conv_ln_g, conv_ln_b=m_conv_ln_b, w_out=m_w_out[0], ffn2_norm=m_ffn2_norm,
               ffn2_w_gate=m_ffn2_w_gate[0], ffn2_w_up=m_ffn2_w_up[0], ffn2_w_down=m_ffn2_w_down[0])
    var = dict(ffn1_norm=v_ffn1_norm, ffn1_w_gate=v_ffn1_w_gate[0], ffn1_w_up=v_ffn1_w_up[0], ffn1_w_down=v_ffn1_w_down[0],
               mix_norm=v_mix_norm, w_in=v_w_in[0], q_norm=v_q_norm, k_norm=v_k_norm, conv_w=v_conv_w[0], conv_b=v_conv_b,
               conv_ln_g=v_conv_ln_g, conv_ln_b=v_conv_ln_b, w_out=v_w_out[0], ffn2_norm=v_ffn2_norm,
               ffn2_w_gate=v_ffn2_w_gate[0], ffn2_w_up=v_ffn2_w_up[0], ffn2_w_down=v_ffn2_w_down[0])
    chip = 2 * lax.axis_index("x") + lax.axis_index("y")
    for src in (wts, mom, var):
        for n in TRANSPOSED:
            src[n] = src[n].T

    taps = jnp.pad(wts["conv_w"], ((0, 1), (0, 0)))
    staged = _stage_shards([wts["ffn1_w_gate"], wts["ffn1_w_up"], wts["ffn1_w_down"], wts["w_in"], wts["w_out"], taps,
                            wts["ffn2_w_gate"], wts["ffn2_w_up"], wts["ffn2_w_down"]],
                           [BF16, BF16, BF16, BF16, BF16, F32, BF16, BF16, BF16], name="stage_shards")
    first = _run_phase(_gather_pipelined_phase(staged[:3]), name="gather_ffn1")
    wg1, wu1, wd1 = [_whole(b) for b in first]
    w = dict(ffn1_norm=ffn1_norm, mix_norm=mix_norm, ffn2_norm=ffn2_norm, q_norm=q_norm, k_norm=k_norm,
             conv_b=conv_b, conv_ln_g=conv_ln_g, conv_ln_b=conv_ln_b, wg1=wg1, wu1=wu1, wd1=wd1)
    overlap = _Overlap(staged[3:])
    loss_part, grad_x, g = _local_step(x, loss_target, w, overlap)

    grads, delta, new_m, new_v = {}, {}, {}, {}
    early = overlap.early.reduced
    grads.update(ffn2_w_gate=early["wg2"], ffn2_w_up=early["wu2"], ffn2_w_down=early["wd2"], w_out=early["wout"])
    grads.update(zip(("ffn1_w_gate", "ffn1_w_up", "ffn1_w_down", "w_in"),
                     overlap.finish_late([g["wg1"], g["wu1"], g["wd1"]])))

    small_shapes = [g[n].shape for n in SMALL] + [(SUBLANES, LANES)]
    red = _allreduce_small(_pack([g[n] for n in SMALL] + [loss_part]), name="allreduce_small")
    small = dict(zip(SMALL + ("loss",), _unpack(red, small_shapes)))
    loss = small["loss"][0, 0]
    small["conv_w"] = lax.dynamic_slice_in_dim(small["conv_w"], chip * LANES, LANES, axis=1)

    for tag, names in (("early", ("ffn2_w_gate", "ffn2_w_up", "ffn2_w_down", "w_out")),
                       ("late", ("ffn1_w_gate", "ffn1_w_up", "ffn1_w_down", "w_in"))):
        d, m, v, _ = _adamw_many([wts[n] for n in names], [grads[n] for n in names], [mom[n] for n in names],
                                 [var[n] for n in names], name=f"adamw_{tag}")
        for dst, vals in ((delta, d), (new_m, m), (new_v, v)):
            dst.update(zip(names, vals))
    shapes = [wts[n].shape for n in SMALL]
    packs = [_pack([src[n] for n in SMALL]) for src in (wts, small, mom, var)]
    outs = _adamw(*packs, name="adamw_small")
    for dst, pk in zip((delta, new_m, new_v), outs):
        dst.update(zip(SMALL, _unpack(pk, shapes)))
    for n in SMALL:
        grads[n] = small[n]

    def shaped(d, n):
        v = d[n].T if n in TRANSPOSED else d[n]
        return v.reshape((1,) + v.shape) if n in BIG or n == "conv_w" else v

    return (loss, grad_x, *[shaped(grads, n) for n in WEIGHTS], *[shaped(delta, n) for n in WEIGHTS],
            *[shaped(new_m, n) for n in WEIGHTS], *[shaped(new_v, n) for n in WEIGHTS])
```

```python
import functools

import jax
import jax.numpy as jnp
from jax import lax
from jax.experimental import pallas as pl
from jax.experimental.pallas import tpu as pltpu

F32 = jnp.float32
BF16 = jnp.bfloat16

EPS = 1e-6
HEADS = 8
HEAD_DIM = 64
D_ATTN = HEADS * HEAD_DIM
D_CONV = 512
CONV_K = 31
QBLK = 128
N_PATTERNS = 3
DILATIONS = (1, 4, 16)
LANES = 128
NEG = -1e30

ADAM_LR = 0.001
ADAM_B1 = 0.9
ADAM_B2 = 0.999
ADAM_EPS = 1e-08
ADAM_WD = 0.01
ADAM_STEP = 10

VMEM_LIMIT = 56 * 1024 * 1024
MESH = pl.DeviceIdType.MESH

NT_DIMS = (((1,), (1,)), ((), ()))
TN_DIMS = (((0,), (0,)), ((), ()))


def _params(*sem):
    return pltpu.CompilerParams(dimension_semantics=sem, vmem_limit_bytes=VMEM_LIMIT)


def _dot(a, b):
    return jnp.dot(a, b, preferred_element_type=F32)


def _dot_nt(a, b):
    return lax.dot_general(a, b, NT_DIMS, preferred_element_type=F32)


def _dot_tn(a, b):
    return lax.dot_general(a, b, TN_DIMS, preferred_element_type=F32)


def _sigmoid(x):
    return 1.0 / (1.0 + jnp.exp(-x))


def _seg_mean(v, e_ref, width):
    hi = v.astype(BF16)
    lo = (v - hi.astype(F32)).astype(BF16)
    e = e_ref[...]
    return (_dot(hi, e) + _dot(lo, e)) * (1.0 / width)


def _seg_matrix(n):
    i = jnp.arange(n)
    return (i[:, None] // HEAD_DIM == i[None, :] // HEAD_DIM).astype(BF16)


ANY = pl.BlockSpec(memory_space=pl.ANY)
DMA_SEMS = pltpu.SemaphoreType.DMA


class _Phase:
    def __init__(self, ins, outs, aliases, nsem, copies):
        self.ins, self.outs, self.aliases = list(ins), list(outs), dict(aliases)
        self.stages = [(nsem, copies)]

    def then(self, other):
        self.stages = self.stages + other.stages
        return self

    @property
    def nsem(self):
        return sum(n for n, _ in self.stages)

    def _copies(self, k, in_refs, out_refs, send_sems, recv_sems):
        base = sum(n for n, _ in self.stages[:k])
        return self.stages[k][1](in_refs, out_refs, lambda i: (send_sems.at[base + i], recv_sems.at[base + i]))

    def start(self, k, *refs):
        for cp in self._copies(k, *refs)[0]:
            cp.start()

    def finish(self, k, *refs):
        starts, arrivals = self._copies(k, *refs)
        for cp in arrivals:
            cp.wait_recv()
        for cp in starts:
            cp.wait_send()


def _run_phase(phase, *, name):
    n_in, n_out = len(phase.ins), len(phase.outs)

    def body(*refs):
        ins, outs = refs[:n_in], refs[n_in:n_in + n_out]
        send_sems, recv_sems = refs[n_in + n_out:]
        for k in range(len(phase.stages)):
            phase.start(k, ins, outs, send_sems, recv_sems)
            phase.finish(k, ins, outs, send_sems, recv_sems)

    return pl.pallas_call(
        body, in_specs=[ANY] * n_in, out_specs=[ANY] * n_out, out_shape=phase.outs,
        input_output_aliases=phase.aliases,
        scratch_shapes=[DMA_SEMS((phase.nsem,)), DMA_SEMS((phase.nsem,))], name=name)(*phase.ins)


def _call(body, *, grid, in_specs, out_specs, out_shape, scratch_shapes=(), sem, name, args, phase=None):
    in_specs, out_specs, out_shape = list(in_specs), list(out_specs), list(out_shape)
    scratch_shapes = list(scratch_shapes)
    if phase is None:
        return pl.pallas_call(body, grid=grid, in_specs=in_specs, out_specs=out_specs, out_shape=out_shape,
                              scratch_shapes=scratch_shapes, compiler_params=_params(*sem), name=name)(*args)
    n_in, n_out, n_scr = len(in_specs), len(out_specs), len(scratch_shapes)
    p_in, p_out = len(phase.ins), len(phase.outs)

    def hosted(*refs):
        ins, pins = refs[:n_in], refs[n_in:n_in + p_in]
        o0 = n_in + p_in
        outs, pouts = refs[o0:o0 + n_out], refs[o0 + n_out:o0 + n_out + p_out]
        s0 = o0 + n_out + p_out
        scr = refs[s0:s0 + n_scr]
        send_sems, recv_sems = refs[s0 + n_scr:]
        step = 0
        for d, n in enumerate(grid):
            step = step * n + pl.program_id(d)
        nsteps = functools.reduce(lambda a, b: a * b, grid)
        nstages = len(phase.stages)
        comm_refs = (pins, pouts, send_sems, recv_sems)

        for k in range(nstages):
            @pl.when(step == (k * nsteps) // nstages)
            def _(k=k):
                if k > 0:
                    phase.finish(k - 1, *comm_refs)
                phase.start(k, *comm_refs)

        body(*ins, *outs, *scr)

        @pl.when(step == nsteps - 1)
        def _():
            phase.finish(nstages - 1, *comm_refs)

    res = pl.pallas_call(
        hosted, grid=grid, in_specs=in_specs + [ANY] * p_in, out_specs=out_specs + [ANY] * p_out,
        out_shape=out_shape + phase.outs,
        input_output_aliases={n_in + i: n_out + o for i, o in phase.aliases.items()},
        scratch_shapes=scratch_shapes + [DMA_SEMS((phase.nsem,)), DMA_SEMS((phase.nsem,))],
        compiler_params=_params(*sem), name=name)(*args, *phase.ins)
    return res[:n_out], res[n_out:]


ROW_CHUNK = 256


def _ffn_fwd(x, gain, wg, wu, wd, tgt, *, tm, name, phase=None):
    T, D = x.shape
    NS, Fs, _ = wg.shape
    with_loss = tgt is not None

    def body(*refs):
        if with_loss:
            x_ref, g_ref, wg_ref, wu_ref, wd_ref, t_ref, h_ref, n_ref, G_ref, U_ref, loss_ref, acc_ref = refs
        else:
            x_ref, g_ref, wg_ref, wu_ref, wd_ref, h_ref, n_ref, G_ref, U_ref, acc_ref = refs
        i = pl.program_id(0)
        j = pl.program_id(1)

        @pl.when(j == 0)
        def _():
            xv = x_ref[...]
            r = lax.rsqrt(jnp.mean(xv * xv, axis=-1, keepdims=True) + EPS)
            n_ref[...] = (xv * r * g_ref[...]).astype(BF16)
            acc_ref[...] = jnp.zeros_like(acc_ref)

        n = n_ref[...]
        G = _dot_nt(n, wg_ref[...])
        U = _dot_nt(n, wu_ref[...])
        G_ref[...] = G.astype(BF16)
        U_ref[...] = U.astype(BF16)
        A = (G * _sigmoid(G) * U).astype(BF16)
        acc_ref[...] += _dot(A, wd_ref[...])

        @pl.when(j == NS - 1)
        def _():
            h = x_ref[...] + 0.5 * acc_ref[...]
            if with_loss:
                e = h - t_ref[...]
                h_ref[...] = e * (1.0 / D)

                @pl.when(i == 0)
                def _():
                    loss_ref[...] = jnp.zeros_like(loss_ref)

                loss_ref[...] += jnp.sum(e * e) * (0.5 / D)
            else:
                h_ref[...] = h

    tok = pl.BlockSpec((tm, D), lambda i, j: (i, 0))
    in_specs = [tok, pl.BlockSpec((1, D), lambda i, j: (0, 0)),
                pl.BlockSpec((None, Fs, D), lambda i, j: (j, 0, 0)),
                pl.BlockSpec((None, Fs, D), lambda i, j: (j, 0, 0)),
                pl.BlockSpec((None, Fs, D), lambda i, j: (j, 0, 0))]
    args = [x, gain, wg, wu, wd]
    act = pl.BlockSpec((None, tm, Fs), lambda i, j: (j, i, 0))
    out_shape = [jax.ShapeDtypeStruct((T, D), F32), jax.ShapeDtypeStruct((T, D), BF16),
                 jax.ShapeDtypeStruct((NS, T, Fs), BF16), jax.ShapeDtypeStruct((NS, T, Fs), BF16)]
    out_specs = [tok, tok, act, act]
    if with_loss:
        in_specs.append(tok)
        args.append(tgt)
        out_shape.append(jax.ShapeDtypeStruct((8, LANES), F32))
        out_specs.append(pl.BlockSpec((8, LANES), lambda i, j: (0, 0)))
    return _call(body, grid=(T // tm, NS), in_specs=in_specs, out_specs=out_specs, out_shape=out_shape,
                 scratch_shapes=[pltpu.VMEM((tm, D), F32)], sem=("arbitrary", "arbitrary"), name=name,
                 args=args, phase=phase)


def _rms_bwd(xv, gain, dn):
    r = lax.rsqrt(jnp.mean(xv * xv, axis=-1, keepdims=True) + EPS)
    xhat = xv * r
    dxh = dn * gain
    dx = r * (dxh - xhat * jnp.mean(dxh * xhat, axis=-1, keepdims=True))
    dg = jnp.sum(dn * xhat, axis=0, keepdims=True)
    return dx, dg


def _ffn_bwd_act(dh, x, gain, G, U, wg, wu, wd, *, tm, name, phase=None):
    T, D = x.shape
    NS, Fs, _ = wg.shape

    def body(dh_ref, x_ref, g_ref, G_ref, U_ref, wg_ref, wu_ref, wd_ref,
             dG_ref, dU_ref, A_ref, dy_ref, dx_ref, dg_ref, acc_ref):
        i = pl.program_id(0)
        j = pl.program_id(1)

        @pl.when(j == 0)
        def _():
            dy_ref[...] = (0.5 * dh_ref[...]).astype(BF16)
            acc_ref[...] = jnp.zeros_like(acc_ref)

        @pl.when((i == 0) & (j == 0))
        def _():
            dg_ref[...] = jnp.zeros_like(dg_ref)

        nchunks = tm // ROW_CHUNK
        dA, dGU = {}, {}
        for step in range(nchunks + 2):
            if step < nchunks:
                rows = slice(step * ROW_CHUNK, (step + 1) * ROW_CHUNK)
                dA[step] = _dot_nt(dy_ref[rows, :], wd_ref[...])
            if 1 <= step <= nchunks:
                k = step - 1
                rows = slice(k * ROW_CHUNK, (k + 1) * ROW_CHUNK)
                Gv = G_ref[rows, :].astype(F32)
                Uv = U_ref[rows, :].astype(F32)
                sig = _sigmoid(Gv)
                s = Gv * sig
                dG = (dA[k] * Uv * (sig * (1.0 + Gv * (1.0 - sig)))).astype(BF16)
                dU = (dA.pop(k) * s).astype(BF16)
                dG_ref[rows, :] = dG
                dU_ref[rows, :] = dU
                A_ref[rows, :] = (s * Uv).astype(BF16)
                dGU[k] = (dG, dU)
            if 2 <= step:
                k = step - 2
                rows = slice(k * ROW_CHUNK, (k + 1) * ROW_CHUNK)
                dG, dU = dGU.pop(k)
                acc_ref[rows, :] += _dot(dG, wg_ref[...]) + _dot(dU, wu_ref[...])

        @pl.when(j == NS - 1)
        def _():
            dx, dg = _rms_bwd(x_ref[...], g_ref[...], acc_ref[...])
            dx_ref[...] = dh_ref[...] + dx
            dg_ref[...] += dg

    tok = pl.BlockSpec((tm, D), lambda i, j: (i, 0))
    act = pl.BlockSpec((None, tm, Fs), lambda i, j: (j, i, 0))
    vec = pl.BlockSpec((1, D), lambda i, j: (0, 0))
    return _call(
        body, grid=(T // tm, NS),
        in_specs=[tok, tok, vec, act, act,
                  pl.BlockSpec((None, Fs, D), lambda i, j: (j, 0, 0)),
                  pl.BlockSpec((None, Fs, D), lambda i, j: (j, 0, 0)),
                  pl.BlockSpec((None, Fs, D), lambda i, j: (j, 0, 0))],
        out_specs=[act, act, act, tok, tok, vec],
        out_shape=[jax.ShapeDtypeStruct((NS, T, Fs), BF16)] * 3
        + [jax.ShapeDtypeStruct((T, D), BF16), jax.ShapeDtypeStruct((T, D), F32),
           jax.ShapeDtypeStruct((1, D), F32)],
        scratch_shapes=[pltpu.VMEM((tm, D), F32)],
        sem=("arbitrary", "arbitrary"), name=name, args=(dh, x, gain, G, U, wg, wu, wd), phase=phase)


def _ffn_bwd_w(n, dy, dG, dU, A, *, tk, name, phase=None):
    T, D = n.shape
    NS, _, Fs = dG.shape

    def body(n_ref, dy_ref, dG_ref, dU_ref, A_ref, wg_ref, wu_ref, wd_ref):
        @pl.when(pl.program_id(1) == 0)
        def _():
            wg_ref[...] = jnp.zeros_like(wg_ref)
            wu_ref[...] = jnp.zeros_like(wu_ref)
            wd_ref[...] = jnp.zeros_like(wd_ref)

        nv = n_ref[...]
        wg_ref[...] += _dot_tn(dG_ref[...], nv)
        wu_ref[...] += _dot_tn(dU_ref[...], nv)
        wd_ref[...] += _dot_tn(A_ref[...], dy_ref[...])

    tok = pl.BlockSpec((tk, D), lambda j, k: (k, 0))
    act = pl.BlockSpec((None, tk, Fs), lambda j, k: (j, k, 0))
    return _call(
        body, grid=(NS, T // tk), in_specs=[tok, tok, act, act, act],
        out_specs=[pl.BlockSpec((None, Fs, D), lambda j, k: (j, 0, 0))] * 3,
        out_shape=[jax.ShapeDtypeStruct((NS, Fs, D), F32)] * 3,
        sem=("arbitrary", "arbitrary"), name=name, args=(n, dy, dG, dU, A), phase=phase)


def _inproj_fwd(h, gain, win, *, tm, name, phase=None):
    T, D = h.shape
    NS, _, Cs = win.shape

    def body(h_ref, g_ref, w_ref, u_ref, n_ref):
        @pl.when(pl.program_id(1) == 0)
        def _():
            xv = h_ref[...]
            r = lax.rsqrt(jnp.mean(xv * xv, axis=-1, keepdims=True) + EPS)
            n_ref[...] = (xv * r * g_ref[...]).astype(BF16)

        u_ref[...] = _dot(n_ref[...], w_ref[...])

    tok = pl.BlockSpec((tm, D), lambda i, j: (i, 0))
    return _call(
        body, grid=(T // tm, NS),
        in_specs=[tok, pl.BlockSpec((1, D), lambda i, j: (0, 0)),
                  pl.BlockSpec((None, D, Cs), lambda i, j: (j, 0, 0))],
        out_specs=[pl.BlockSpec((tm, Cs), lambda i, j: (i, j)), tok],
        out_shape=[jax.ShapeDtypeStruct((T, NS * Cs), F32), jax.ShapeDtypeStruct((T, D), BF16)],
        sem=("arbitrary", "arbitrary"), name=name, args=(h, gain, win), phase=phase)


def _inproj_bwd_act(du, dh, h, gain, win, *, tm, name, phase=None):
    T, D = h.shape
    NS, _, Cs = win.shape

    def body(du_ref, dh_ref, h_ref, g_ref, w_ref, dx_ref, dg_ref, acc_ref):
        i = pl.program_id(0)
        j = pl.program_id(1)

        @pl.when(j == 0)
        def _():
            acc_ref[...] = jnp.zeros_like(acc_ref)

        @pl.when((i == 0) & (j == 0))
        def _():
            dg_ref[...] = jnp.zeros_like(dg_ref)

        acc_ref[...] += _dot_nt(du_ref[...], w_ref[...])

        @pl.when(j == NS - 1)
        def _():
            dx, dg = _rms_bwd(h_ref[...], g_ref[...], acc_ref[...])
            dx_ref[...] = dh_ref[...] + dx
            dg_ref[...] += dg

    tok = pl.BlockSpec((tm, D), lambda i, j: (i, 0))
    vec = pl.BlockSpec((1, D), lambda i, j: (0, 0))
    return _call(
        body, grid=(T // tm, NS),
        in_specs=[pl.BlockSpec((tm, Cs), lambda i, j: (i, j)), tok, tok, vec,
                  pl.BlockSpec((None, D, Cs), lambda i, j: (j, 0, 0))],
        out_specs=[tok, vec],
        out_shape=[jax.ShapeDtypeStruct((T, D), F32), jax.ShapeDtypeStruct((1, D), F32)],
        scratch_shapes=[pltpu.VMEM((tm, D), F32)],
        sem=("arbitrary", "arbitrary"), name=name, args=(du, dh, h, gain, win), phase=phase)


def _inproj_bwd_w(n, du, ns, *, tk, name, phase=None):
    T, D = n.shape
    Cs = du.shape[1] // ns

    def body(n_ref, du_ref, w_ref):
        @pl.when(pl.program_id(1) == 0)
        def _():
            w_ref[...] = jnp.zeros_like(w_ref)

        w_ref[...] += _dot_tn(n_ref[...], du_ref[...])

    return _call(
        body, grid=(ns, T // tk),
        in_specs=[pl.BlockSpec((tk, D), lambda j, k: (k, 0)), pl.BlockSpec((tk, Cs), lambda j, k: (k, j))],
        out_specs=[pl.BlockSpec((None, D, Cs), lambda j, k: (j, 0, 0))],
        out_shape=[jax.ShapeDtypeStruct((ns, D, Cs), F32)],
        sem=("arbitrary", "arbitrary"), name=name, args=(n, du), phase=phase)


STRIDE = 4


def _permute(src_ref, tmp_ref, put):
    S = src_ref.shape[0]
    L4, L16 = S // STRIDE, S // (STRIDE * STRIDE)
    put(0, 0, src_ref[...])
    for r0 in range(STRIDE):
        v = src_ref[pl.ds(r0, L4, stride=STRIDE), :]
        put(1, r0 * L4, v)
        tmp_ref[r0 * L4:(r0 + 1) * L4, :] = v
    for r0 in range(STRIDE):
        for r1 in range(STRIDE):
            put(2, (r1 * STRIDE + r0) * L16, tmp_ref[pl.ds(r0 * L4 + r1, L16, stride=STRIDE), :])


def _permute_out(src_ref, tmp_ref, out_ref, cast):
    for cc in range(src_ref.shape[0]):
        cols = slice(cc * LANES, (cc + 1) * LANES)

        def put(p, row0, v, cols=cols):
            out_ref[p, row0:row0 + v.shape[0], cols] = v.astype(cast)

        _permute(src_ref.at[cc], tmp_ref, put)


def _unpermute_in(get_block, dst_ref, tmp_ref, p, S):
    L4, L16 = S // STRIDE, S // (STRIDE * STRIDE)
    if p == 0:
        dst_ref[...] = get_block(0, S)
        return
    if p == 1:
        for r0 in range(STRIDE):
            dst_ref[pl.ds(r0, L4, stride=STRIDE), :] = get_block(r0 * L4, L4)
        return
    for r0 in range(STRIDE):
        for r1 in range(STRIDE):
            tmp_ref[pl.ds(r0 * L4 + r1, L16, stride=STRIDE), :] = get_block((r1 * STRIDE + r0) * L16, L16)
    for r0 in range(STRIDE):
        dst_ref[pl.ds(r0, L4, stride=STRIDE), :] = tmp_ref[r0 * L4:(r0 + 1) * L4, :]


def _qkv_prep(u, gains, B, S, *, name):
    emat = _seg_matrix(D_ATTN)

    def body(u_ref, g_ref, e_ref, out_ref, scr_ref, tmp_ref):
        c = pl.program_id(1)
        xv = u_ref[...]
        ms = _seg_mean(xv * xv, e_ref, HEAD_DIM)
        r = jnp.where(c < 2, lax.rsqrt(ms + EPS), 1.0)
        yv = xv * r * g_ref[...]
        for cc in range(4):
            scr_ref[cc] = yv[:, cc * LANES:(cc + 1) * LANES]
        _permute_out(scr_ref, tmp_ref, out_ref, BF16)

    return pl.pallas_call(
        body, grid=(B, 3),
        in_specs=[pl.BlockSpec((S, D_ATTN), lambda b, c: (b, c)),
                  pl.BlockSpec((None, 1, D_ATTN), lambda b, c: (c, 0, 0)),
                  pl.BlockSpec((D_ATTN, D_ATTN), lambda b, c: (0, 0))],
        out_specs=pl.BlockSpec((None, N_PATTERNS, None, S, D_ATTN), lambda b, c: (c, 0, b, 0, 0)),
        out_shape=jax.ShapeDtypeStruct((3, N_PATTERNS, B, S, D_ATTN), BF16),
        scratch_shapes=[pltpu.VMEM((4, S, LANES), F32), pltpu.VMEM((S, LANES), F32)],
        compiler_params=_params("arbitrary", "arbitrary"), name=name)(u, gains, emat)


def _band_mask(p, b):
    nblk = jnp.right_shift(16, 2 * p)
    has_prev = jnp.bitwise_and(b, nblk - 1) != 0
    qi = lax.broadcasted_iota(jnp.int32, (QBLK, 2 * QBLK), 0)
    ci = lax.broadcasted_iota(jnp.int32, (QBLK, 2 * QBLK), 1)
    dist = QBLK + qi - ci
    return (dist >= 0) & (dist <= QBLK) & (has_prev | (ci >= QBLK))


def _first_head(rows):
    return lax.broadcasted_iota(jnp.int32, (rows, LANES), 1) < HEAD_DIM


def _split_heads(pair):
    first = _first_head(pair.shape[0])
    zero = jnp.zeros_like(pair)
    return jnp.concatenate([jnp.where(first, pair, zero), jnp.where(first, zero, pair)], axis=0)


def _merge_heads(col_a, col_b):
    rows = col_a.shape[0]
    return jnp.where(_first_head(rows), jnp.broadcast_to(col_a, (rows, LANES)), jnp.broadcast_to(col_b, (rows, LANES)))


QB_FWD = 8
QB_BWD = 4


def _attn_fwd(qkv, *, name, phase=None):
    QB = QB_FWD
    nb = qkv.shape[2]

    def body(q_ref, kp_ref, kc_ref, vp_ref, vc_ref, o_ref, lse_ref):
        kall = jnp.concatenate([kp_ref[...]] + [kc_ref[t] for t in range(QB)], axis=0)
        vall = jnp.concatenate([vp_ref[...]] + [vc_ref[t] for t in range(QB)], axis=0)
        masks = []
        for t in range(QB):
            mask = _band_mask(pl.program_id(0), QB * pl.program_id(1) + t)
            masks.append(jnp.concatenate([mask, mask], axis=0))
        units = [(t, hp) for t in range(QB) for hp in range(HEADS // 2)]
        scores, probs = {}, {}
        for step in range(len(units) + 2):
            if step < len(units):
                t, hp = units[step]
                cols = slice(hp * LANES, (hp + 1) * LANES)
                scores[step] = _dot_nt(_split_heads(q_ref[t, :, cols]), kall[t * QBLK:(t + 2) * QBLK, cols])
            if 1 <= step <= len(units):
                t, hp = units[step - 1]
                cols = slice(hp * LANES, (hp + 1) * LANES)
                s = jnp.where(masks[t], scores.pop(step - 1), NEG)
                m = jnp.max(s, axis=-1, keepdims=True)
                e = jnp.exp(s - m)
                l = jnp.sum(e, axis=-1, keepdims=True)
                probs[step - 1] = (e * (1.0 / l)).astype(BF16)
                lse = m + jnp.log(l)
                lse_ref[t, :, cols] = _merge_heads(lse[:QBLK], lse[QBLK:])
            if 2 <= step:
                t, hp = units[step - 2]
                cols = slice(hp * LANES, (hp + 1) * LANES)
                pr = probs.pop(step - 2)
                o_ref[t, :, cols] = _dot(jnp.concatenate([pr[:QBLK], pr[QBLK:]], axis=1),
                                         _split_heads(vall[t * QBLK:(t + 2) * QBLK, cols]))

    cur = lambda which: pl.BlockSpec((None, None, QB, QBLK, D_ATTN), lambda p, i: (which, p, i, 0, 0))
    prev = lambda which: pl.BlockSpec((None, None, None, QBLK, D_ATTN),
                                      lambda p, i: (which, p, jnp.maximum(QB * i - 1, 0), 0, 0))
    out = pl.BlockSpec((None, QB, QBLK, D_ATTN), lambda p, i: (p, i, 0, 0))
    return _call(
        body, grid=(N_PATTERNS, nb // QB), in_specs=[cur(0), prev(1), cur(1), prev(2), cur(2)], out_specs=[out, out],
        out_shape=[jax.ShapeDtypeStruct((N_PATTERNS, nb, QBLK, D_ATTN), F32)] * 2,
        sem=("arbitrary", "arbitrary"), name=name, args=(qkv, qkv, qkv, qkv, qkv), phase=phase)


def _attn_combine(o3, lse3, B, S, *, name):
    def body(o_ref, l_ref, a_ref, lt_ref, so_ref, sl_ref, tmp_ref):
        for p in range(N_PATTERNS):
            _unpermute_in(lambda r0, n, p=p: o_ref[p, pl.ds(r0, n), :], so_ref.at[p], tmp_ref, p, S)
            _unpermute_in(lambda r0, n, p=p: l_ref[p, pl.ds(r0, n), :], sl_ref.at[p], tmp_ref, p, S)
        l0, l1, l2 = sl_ref[0], sl_ref[1], sl_ref[2]
        m = jnp.maximum(jnp.maximum(l0, l1), l2)
        w0, w1, w2 = jnp.exp(l0 - m), jnp.exp(l1 - m), jnp.exp(l2 - m)
        tot = w0 + w1 + w2
        a_ref[...] = (w0 * so_ref[0] + w1 * so_ref[1] + w2 * so_ref[2]) / tot
        lt_ref[...] = m + jnp.log(tot)

    o3 = o3.reshape(N_PATTERNS, B, S, D_ATTN)
    lse3 = lse3.reshape(N_PATTERNS, B, S, D_ATTN)
    inp = pl.BlockSpec((N_PATTERNS, None, S, LANES), lambda b, c: (0, b, 0, c))
    out = pl.BlockSpec((S, LANES), lambda b, c: (b, c))
    return pl.pallas_call(
        body, grid=(B, D_ATTN // LANES), in_specs=[inp, inp], out_specs=[out, out],
        out_shape=[jax.ShapeDtypeStruct((B * S, D_ATTN), F32)] * 2,
        scratch_shapes=[pltpu.VMEM((N_PATTERNS, S, LANES), F32)] * 2 + [pltpu.VMEM((S, LANES), F32)],
        compiler_params=_params("arbitrary", "arbitrary"), name=name)(o3, lse3)


STAT_D = 8


def _attn_bwd_prep(dattn, attn, lse, B, S, *, name):
    emat = _seg_matrix(LANES)
    ncc = D_ATTN // LANES

    def body(da_ref, a_ref, l_ref, e_ref, do_ref, st_ref, scr_ref, nat_ref, tmp_ref):
        cc = pl.program_id(1)
        da = da_ref[...]
        dsum = _seg_mean(da * a_ref[...], e_ref, 1.0)
        scr_ref[...] = da

        def put_do(p, row0, v):
            do_ref[p, row0:row0 + v.shape[0], :] = v.astype(BF16)

        _permute(scr_ref, tmp_ref, put_do)

        lane = lax.broadcasted_iota(jnp.int32, (S, LANES), 1)
        h0 = 2 * cc
        vals = ((h0, l_ref[:, 0:1]), (h0 + 1, l_ref[:, HEAD_DIM:HEAD_DIM + 1]),
                (STAT_D + h0, dsum[:, 0:1]), (STAT_D + h0 + 1, dsum[:, HEAD_DIM:HEAD_DIM + 1]))
        tile = jnp.where(cc == 0, 0.0, nat_ref[...])
        for at, col in vals:
            tile = jnp.where(lane == at, col, tile)
        nat_ref[...] = tile

        @pl.when(cc == ncc - 1)
        def _():
            def put_st(p, row0, v):
                st_ref[p, row0:row0 + v.shape[0], :] = v

            _permute(nat_ref, tmp_ref, put_st)

    inp = pl.BlockSpec((S, LANES), lambda b, c: (b, c))
    return pl.pallas_call(
        body, grid=(B, ncc),
        in_specs=[inp, inp, inp, pl.BlockSpec((LANES, LANES), lambda b, c: (0, 0))],
        out_specs=[pl.BlockSpec((N_PATTERNS, None, S, LANES), lambda b, c: (0, b, 0, c)),
                   pl.BlockSpec((N_PATTERNS, None, S, LANES), lambda b, c: (0, b, 0, 0))],
        out_shape=[jax.ShapeDtypeStruct((N_PATTERNS, B, S, D_ATTN), BF16),
                   jax.ShapeDtypeStruct((N_PATTERNS, B, S, LANES), F32)],
        scratch_shapes=[pltpu.VMEM((S, LANES), F32)] * 3,
        compiler_params=_params("arbitrary", "arbitrary"), name=name)(dattn, attn, lse, emat)


def _attn_bwd(qkv, do3, st3, *, name, phase=None):
    QB = QB_BWD
    nb = qkv.shape[2]
    ngroups = nb // QB

    def body(q_ref, kp_ref, kc_ref, vp_ref, vc_ref, do_ref, st_ref, out_ref, carry_ref):
        p = pl.program_id(0)
        i = pl.program_id(1)

        @pl.when((p == 0) & (i == 0))
        def _():
            carry_ref[...] = jnp.zeros_like(carry_ref)

        kall = jnp.concatenate([kp_ref[...]] + [kc_ref[t] for t in range(QB)], axis=0)
        vall = jnp.concatenate([vp_ref[...]] + [vc_ref[t] for t in range(QB)], axis=0)

        masks = []
        for t in range(QB):
            mask = _band_mask(p, QB * i + t) & (i < ngroups)
            masks.append(jnp.concatenate([mask, mask], axis=0))

        def operands(t, hp):
            cols = slice(hp * LANES, (hp + 1) * LANES)
            kh, vh = kall[t * QBLK:(t + 2) * QBLK, cols], vall[t * QBLK:(t + 2) * QBLK, cols]
            return kh, vh, _split_heads(q_ref[t, :, cols]), _split_heads(do_ref[t, :, cols])

        def stage_scores(t, hp):
            kh, vh, q2, do2 = operands(t, hp)
            return _dot_nt(q2, kh), _dot_nt(do2, vh)

        def stage_softmax(t, hp, s, dp):
            h0, h1 = 2 * hp, 2 * hp + 1
            lse = jnp.concatenate([st_ref[t, :, h0:h0 + 1], st_ref[t, :, h1:h1 + 1]], axis=0)
            dsum = jnp.concatenate([st_ref[t, :, STAT_D + h0:STAT_D + h0 + 1],
                                    st_ref[t, :, STAT_D + h1:STAT_D + h1 + 1]], axis=0)
            pr = jnp.where(masks[t], jnp.exp(s - lse), 0.0)
            return (pr * (dp - dsum)).astype(BF16), pr.astype(BF16)

        def stage_grads(t, hp, ds, prb):
            cols = slice(hp * LANES, (hp + 1) * LANES)
            kh, vh, q2, do2 = operands(t, hp)
            dq = _dot(jnp.concatenate([ds[:QBLK], ds[QBLK:]], axis=1), _split_heads(kh))
            dk, dv = _dot_tn(ds, q2), _dot_tn(prb, do2)
            if t == 0:
                for c in range(3):
                    for tt in range(QB):
                        v = carry_ref[c, tt, :, cols]
                        if tt == QB - 1 and c > 0:
                            v = v + (dk if c == 1 else dv)[:QBLK]
                        out_ref[c, tt, :, cols] = v.astype(BF16)
            else:
                carry_ref[1, t - 1, :, cols] += dk[:QBLK]
                carry_ref[2, t - 1, :, cols] += dv[:QBLK]
            carry_ref[0, t, :, cols] = dq
            carry_ref[1, t, :, cols] = dk[QBLK:]
            carry_ref[2, t, :, cols] = dv[QBLK:]

        units = [(t, hp) for hp in range(HEADS // 2) for t in range(QB)]
        scores, probs = {}, {}
        for step in range(len(units) + 2):
            if step < len(units):
                scores[step] = stage_scores(*units[step])
            if 1 <= step <= len(units):
                probs[step - 1] = stage_softmax(*units[step - 1], *scores.pop(step - 1))
            if 2 <= step:
                stage_grads(*units[step - 2], *probs.pop(step - 2))

    group = lambda i: jnp.minimum(i, ngroups - 1)
    cur = lambda which: pl.BlockSpec((None, None, QB, QBLK, D_ATTN), lambda p, i: (which, p, group(i), 0, 0))
    prev = lambda which: pl.BlockSpec((None, None, None, QBLK, D_ATTN),
                                      lambda p, i: (which, p, jnp.maximum(QB * group(i) - 1, 0), 0, 0))
    aux = lambda lanes: pl.BlockSpec((None, QB, QBLK, lanes), lambda p, i: (p, group(i), 0, 0))
    return _call(
        body, grid=(N_PATTERNS, ngroups + 1),
        in_specs=[cur(0), prev(1), cur(1), prev(2), cur(2), aux(D_ATTN), aux(LANES)],
        out_specs=[pl.BlockSpec((3, None, QB, QBLK, D_ATTN), lambda p, i: (0, p, jnp.maximum(i - 1, 0), 0, 0))],
        out_shape=[jax.ShapeDtypeStruct((3, N_PATTERNS, nb, QBLK, D_ATTN), BF16)],
        scratch_shapes=[pltpu.VMEM((3, QB, QBLK, D_ATTN), F32)],
        sem=("arbitrary", "arbitrary"), name=name, args=(qkv, qkv, qkv, qkv, qkv, do3, st3), phase=phase)


def _attn_grad_combine(cur, u, gains, B, S, *, name, phase=None):
    emat = _seg_matrix(LANES)

    def body(cur_ref, u_ref, g_ref, e_ref, du_ref, dg_ref, scr_ref, tmp_ref):
        c = pl.program_id(0)
        b = pl.program_id(2)
        for p in range(N_PATTERNS):
            _unpermute_in(lambda r0, n, p=p: cur_ref[p, pl.ds(r0, n), :].astype(F32), scr_ref.at[p], tmp_ref, p, S)
        dy = scr_ref[0] + scr_ref[1] + scr_ref[2]
        xv = u_ref[...]
        gain = g_ref[...]
        ms = _seg_mean(xv * xv, e_ref, HEAD_DIM)
        r = lax.rsqrt(ms + EPS)
        xhat = xv * r
        dxh = dy * gain
        dx = r * (dxh - xhat * _seg_mean(dxh * xhat, e_ref, HEAD_DIM))
        du_ref[...] = jnp.where(c < 2, dx, dy).astype(BF16)

        @pl.when((b == 0))
        def _():
            dg_ref[...] = jnp.zeros_like(dg_ref)

        dg_ref[...] += jnp.sum(dy * xhat, axis=0, keepdims=True)

    cur = cur.reshape(3, N_PATTERNS, B, S, D_ATTN)
    ncc = D_ATTN // LANES
    return _call(
        body, grid=(3, ncc, B),
        in_specs=[pl.BlockSpec((None, N_PATTERNS, None, S, LANES), lambda c, cc, b: (c, 0, b, 0, cc)),
                  pl.BlockSpec((S, LANES), lambda c, cc, b: (b, c * ncc + cc)),
                  pl.BlockSpec((None, 1, LANES), lambda c, cc, b: (c, 0, cc)),
                  pl.BlockSpec((LANES, LANES), lambda c, cc, b: (0, 0))],
        out_specs=[pl.BlockSpec((S, LANES), lambda c, cc, b: (b, c * ncc + cc)),
                   pl.BlockSpec((None, 1, LANES), lambda c, cc, b: (c, 0, cc))],
        out_shape=[jax.ShapeDtypeStruct((B * S, 3 * D_ATTN), BF16), jax.ShapeDtypeStruct((3, 1, D_ATTN), F32)],
        scratch_shapes=[pltpu.VMEM((N_PATTERNS, S, LANES), F32), pltpu.VMEM((S, LANES), F32)],
        sem=("arbitrary", "arbitrary", "arbitrary"), name=name, args=(cur, u, gains, emat), phase=phase)


HALO = 32
SUB = 64
SUBLANES = 8


def _shifted_copies(src_ref, sh_ref, tc):
    sh_ref[0] = src_ref[...]
    for r in range(1, SUBLANES):
        sh_ref[r, 0:tc + HALO - SUBLANES, :] = src_ref[pl.ds(r, tc + HALO - SUBLANES), :]


def _shifted(sh_ref, start, size):
    return sh_ref[start % SUBLANES, pl.ds(start - start % SUBLANES, size), :]


def _conv_fwd(u, cw, cb, lg, lb, B, S, *, tc, name, phase=None):
    nchunk = S // tc
    hb = tc // HALO

    def body(ca_ref, cap_ref, cg_ref, cgp_ref, w_ref, cb_ref, lg_ref, lb_ref, cv_ref, glu_ref, y_ref, pad_ref, sh_ref):
        i = pl.program_id(1)
        glu = ca_ref[...] * _sigmoid(cg_ref[...])
        glu_ref[...] = glu
        prev = cap_ref[...] * _sigmoid(cgp_ref[...])
        pad_ref[0:HALO, :] = jnp.where(i > 0, prev, 0.0)
        pad_ref[HALO:, :] = glu
        _shifted_copies(pad_ref, sh_ref, tc)
        for sub in range(tc // SUB):
            acc = jnp.zeros((SUB, D_CONV), F32) + cb_ref[...]
            for k in range(CONV_K):
                acc = acc + _shifted(sh_ref, sub * SUB + HALO - (CONV_K - 1) + k, SUB) * w_ref[pl.ds(k, 1), :]
            y_ref[sub * SUB:(sub + 1) * SUB, :] = acc
        y = y_ref[...]
        mu = jnp.mean(y, axis=-1, keepdims=True)
        yc = y - mu
        var = jnp.mean(yc * yc, axis=-1, keepdims=True)
        z = yc * lax.rsqrt(var + EPS) * lg_ref[...] + lb_ref[...]
        cv_ref[...] = (z * _sigmoid(z)).astype(BF16)

    def cur(col):
        return pl.BlockSpec((tc, D_CONV), lambda b, i: (b * nchunk + i, col))

    def halo(col):
        return pl.BlockSpec((HALO, D_CONV), lambda b, i: (jnp.maximum((b * nchunk + i) * hb - 1, 0), col))

    vec = pl.BlockSpec((1, D_CONV), lambda b, i: (0, 0))
    out = pl.BlockSpec((tc, D_CONV), lambda b, i: (b * nchunk + i, 0))
    return _call(
        body, grid=(B, nchunk),
        in_specs=[cur(3), halo(3), cur(4), halo(4), pl.BlockSpec((CONV_K, D_CONV), lambda b, i: (0, 0)), vec, vec, vec],
        out_specs=[out, out, out],
        out_shape=[jax.ShapeDtypeStruct((B * S, D_CONV), BF16), jax.ShapeDtypeStruct((B * S, D_CONV), F32),
                   jax.ShapeDtypeStruct((B * S, D_CONV), F32)],
        scratch_shapes=[pltpu.VMEM((tc + HALO, D_CONV), F32), pltpu.VMEM((SUBLANES, tc + HALO, D_CONV), F32)],
        sem=("arbitrary", "arbitrary"), name=name, args=(u, u, u, u, cw, cb, lg, lb), phase=phase)


def _conv_bwd_norm(dcv, y, lg, lb, *, tc, name):
    T = y.shape[0]

    def body(dcv_ref, y_ref, lg_ref, lb_ref, dy_ref, part_ref):
        yv = y_ref[...]
        mu = jnp.mean(yv, axis=-1, keepdims=True)
        yc = yv - mu
        var = jnp.mean(yc * yc, axis=-1, keepdims=True)
        rstd = lax.rsqrt(var + EPS)
        xhat = yc * rstd
        z = xhat * lg_ref[...] + lb_ref[...]
        sig = _sigmoid(z)
        dz = dcv_ref[...] * (sig * (1.0 + z * (1.0 - sig)))
        dxh = dz * lg_ref[...]
        dy = rstd * (dxh - jnp.mean(dxh, axis=-1, keepdims=True)
                     - xhat * jnp.mean(dxh * xhat, axis=-1, keepdims=True))
        dy_ref[...] = dy

        @pl.when(pl.program_id(0) == 0)
        def _():
            part_ref[...] = jnp.zeros_like(part_ref)

        part_ref[0:1, :] += jnp.sum(dz * xhat, axis=0, keepdims=True)
        part_ref[1:2, :] += jnp.sum(dz, axis=0, keepdims=True)
        part_ref[2:3, :] += jnp.sum(dy, axis=0, keepdims=True)

    tok = pl.BlockSpec((tc, D_CONV), lambda i: (i, 0))
    vec = pl.BlockSpec((1, D_CONV), lambda i: (0, 0))
    return pl.pallas_call(
        body, grid=(T // tc,), in_specs=[tok, tok, vec, vec],
        out_specs=[tok, pl.BlockSpec((8, D_CONV), lambda i: (0, 0))],
        out_shape=[jax.ShapeDtypeStruct((T, D_CONV), F32), jax.ShapeDtypeStruct((8, D_CONV), F32)],
        compiler_params=_params("arbitrary"), name=name)(dcv, y, lg, lb)


def _conv_bwd_taps(dy, glu, u, cw, B, S, *, tc, name, phase=None):
    nchunk = S // tc
    hb = tc // HALO
    last_hb = B * S // HALO - 1

    def body(dy_ref, dyn_ref, glu_ref, glup_ref, ca_ref, cg_ref, w_ref, dca_ref, dcg_ref, dw_ref,
             dyp_ref, glp_ref, acc_ref, shd_ref, shg_ref):
        b = pl.program_id(0)
        i = pl.program_id(1)
        dy = dy_ref[...]
        dyp_ref[0:tc, :] = dy
        dyp_ref[tc:, :] = jnp.where(i < nchunk - 1, dyn_ref[...], 0.0)
        glp_ref[0:HALO, :] = jnp.where(i > 0, glup_ref[...], 0.0)
        glp_ref[HALO:, :] = glu_ref[...]
        _shifted_copies(dyp_ref, shd_ref, tc)
        _shifted_copies(glp_ref, shg_ref, tc)

        @pl.when((b == 0) & (i == 0))
        def _():
            dw_ref[...] = jnp.zeros_like(dw_ref)

        for sub in range(tc // SUB):
            acc = jnp.zeros((SUB, D_CONV), F32)
            for k in range(CONV_K):
                acc = acc + _shifted(shd_ref, sub * SUB + (CONV_K - 1) - k, SUB) * w_ref[pl.ds(k, 1), :]
            acc_ref[sub * SUB:(sub + 1) * SUB, :] = acc
        for k in range(CONV_K):
            dw_ref[k:k + 1, :] += jnp.sum(dy * _shifted(shg_ref, HALO - (CONV_K - 1) + k, tc), axis=0, keepdims=True)
        dglu = acc_ref[...]
        ca = ca_ref[...]
        sig = _sigmoid(cg_ref[...])
        dca_ref[...] = (dglu * sig).astype(BF16)
        dcg_ref[...] = (dglu * ca * sig * (1.0 - sig)).astype(BF16)

    tok = pl.BlockSpec((tc, D_CONV), lambda b, i: (b * nchunk + i, 0))
    nxt = pl.BlockSpec((HALO, D_CONV), lambda b, i: (jnp.minimum((b * nchunk + i + 1) * hb, last_hb), 0))
    prv = pl.BlockSpec((HALO, D_CONV), lambda b, i: (jnp.maximum((b * nchunk + i) * hb - 1, 0), 0))
    return _call(
        body, grid=(B, nchunk),
        in_specs=[tok, nxt, tok, prv,
                  pl.BlockSpec((tc, D_CONV), lambda b, i: (b * nchunk + i, 3)),
                  pl.BlockSpec((tc, D_CONV), lambda b, i: (b * nchunk + i, 4)),
                  pl.BlockSpec((CONV_K, D_CONV), lambda b, i: (0, 0))],
        out_specs=[tok, tok, pl.BlockSpec((32, D_CONV), lambda b, i: (0, 0))],
        out_shape=[jax.ShapeDtypeStruct((B * S, D_CONV), BF16), jax.ShapeDtypeStruct((B * S, D_CONV), BF16),
                   jax.ShapeDtypeStruct((32, D_CONV), F32)],
        scratch_shapes=[pltpu.VMEM((tc + HALO, D_CONV), F32), pltpu.VMEM((tc + HALO, D_CONV), F32),
                        pltpu.VMEM((tc, D_CONV), F32), pltpu.VMEM((SUBLANES, tc + HALO, D_CONV), F32),
                        pltpu.VMEM((SUBLANES, tc + HALO, D_CONV), F32)],
        sem=("arbitrary", "arbitrary"), name=name, args=(dy, dy, glu, glu, u, u, cw), phase=phase)


def _outproj_fwd(h, attn, cv, wout, *, tm, name):
    T, D = h.shape

    def body(h_ref, a_ref, c_ref, w_ref, o_ref):
        o_ref[...] = (h_ref[...] + _dot(a_ref[...].astype(BF16), w_ref[0:D_ATTN, :])
                      + _dot(c_ref[...], w_ref[D_ATTN:, :]))

    tok = pl.BlockSpec((tm, D), lambda i: (i, 0))
    half = pl.BlockSpec((tm, D_ATTN), lambda i: (i, 0))
    return pl.pallas_call(
        body, grid=(T // tm,), in_specs=[tok, half, half, pl.BlockSpec(wout.shape, lambda i: (0, 0))],
        out_specs=tok, out_shape=jax.ShapeDtypeStruct((T, D), F32),
        compiler_params=_params("arbitrary"), name=name)(h, attn, cv, wout)


def _outproj_bwd(dh, attn, cv, wout, *, tm, name):
    T, D = dh.shape

    def body(dh_ref, a_ref, c_ref, w_ref, da_ref, dc_ref, dw_ref):
        @pl.when(pl.program_id(0) == 0)
        def _():
            dw_ref[...] = jnp.zeros_like(dw_ref)

        dhb = dh_ref[...].astype(BF16)
        da_ref[...] = _dot_nt(dhb, w_ref[0:D_ATTN, :])
        dc_ref[...] = _dot_nt(dhb, w_ref[D_ATTN:, :])
        dw_ref[0:D_ATTN, :] += _dot_tn(a_ref[...].astype(BF16), dhb)
        dw_ref[D_ATTN:, :] += _dot_tn(c_ref[...], dhb)

    tok = pl.BlockSpec((tm, D), lambda i: (i, 0))
    half = pl.BlockSpec((tm, D_ATTN), lambda i: (i, 0))
    wspec = pl.BlockSpec(wout.shape, lambda i: (0, 0))
    return pl.pallas_call(
        body, grid=(T // tm,), in_specs=[tok, half, half, wspec], out_specs=[half, half, wspec],
        out_shape=[jax.ShapeDtypeStruct((T, D_ATTN), F32), jax.ShapeDtypeStruct((T, D_ATTN), F32),
                   jax.ShapeDtypeStruct(wout.shape, F32)],
        compiler_params=_params("arbitrary"), name=name)(dh, attn, cv, wout)


ADAM_BLOCK_BYTES = 3 * 512 * 1024


def _adamw(w, g, m, v, *, name):
    R, C = w.shape
    tr = R
    for cand in (512, 352, 256, 176, 128, 64, 32, 16, 8):
        if R % cand == 0 and cand * C * 4 <= ADAM_BLOCK_BYTES:
            tr = cand
            break
    c1 = 1.0 - ADAM_B1 ** ADAM_STEP
    c2 = 1.0 - ADAM_B2 ** ADAM_STEP

    def body(w_ref, g_ref, m_ref, v_ref, d_ref, nm_ref, nv_ref):
        gv = g_ref[...]
        nm = ADAM_B1 * m_ref[...] + (1.0 - ADAM_B1) * gv
        nv = ADAM_B2 * v_ref[...] + (1.0 - ADAM_B2) * (gv * gv)
        d_ref[...] = -ADAM_LR * ((nm / c1) / (jnp.sqrt(nv / c2) + ADAM_EPS) + ADAM_WD * w_ref[...])
        nm_ref[...] = nm
        nv_ref[...] = nv

    blk = pl.BlockSpec((tr, C), lambda i: (i, 0))
    return pl.pallas_call(
        body, grid=(R // tr,), in_specs=[blk] * 4, out_specs=[blk] * 3,
        out_shape=[jax.ShapeDtypeStruct((R, C), F32)] * 3,
        compiler_params=_params("arbitrary"), name=name)(w, g, m, v)


ADAM_SPLIT = 4


def _adamw_many(ws, gs, ms, vs, *, name, phase=None):
    n = len(ws)
    c1 = 1.0 - ADAM_B1 ** ADAM_STEP
    c2 = 1.0 - ADAM_B2 ** ADAM_STEP

    def body(*refs):
        ins, outs = refs[:4 * n], refs[4 * n:]
        for a in range(n):
            w_ref, g_ref, m_ref, v_ref = ins[4 * a:4 * a + 4]
            gv = g_ref[...]
            nm = ADAM_B1 * m_ref[...] + (1.0 - ADAM_B1) * gv
            nv = ADAM_B2 * v_ref[...] + (1.0 - ADAM_B2) * (gv * gv)
            outs[3 * a][...] = -ADAM_LR * ((nm / c1) / (jnp.sqrt(nv / c2) + ADAM_EPS) + ADAM_WD * w_ref[...])
            outs[3 * a + 1][...] = nm
            outs[3 * a + 2][...] = nv

    in_specs, out_specs, out_shape, args = [], [], [], []
    for w, g, m, v in zip(ws, gs, ms, vs):
        R, C = w.shape
        blk = pl.BlockSpec((R // ADAM_SPLIT, C), lambda i: (i, 0))
        in_specs += [blk] * 4
        out_specs += [blk] * 3
        out_shape += [jax.ShapeDtypeStruct((R, C), F32)] * 3
        args += [w, g, m, v]
    res = _call(body, grid=(ADAM_SPLIT,), in_specs=in_specs, out_specs=out_specs, out_shape=out_shape,
                sem=("arbitrary",), name=name, args=args, phase=phase)
    outs, extra = res if phase is not None else (res, None)
    return list(outs[0::3]), list(outs[1::3]), list(outs[2::3]), extra


TM = 512
TM_WIDE = 1024
TK = 1024
TC = 512


def _local_step(x, tgt, w, overlap=None):
    B, S, D = x.shape
    T = B * S
    x2 = x.reshape(T, D)
    t2 = tgt.reshape(T, D)
    ones = jnp.ones((1, D_ATTN), F32)
    scale = HEAD_DIM ** -0.5
    gains = jnp.stack([jnp.tile(w["q_norm"], (1, HEADS)) * scale, jnp.tile(w["k_norm"], (1, HEADS)), ones])
    g = {}

    def hosting(point, build):
        phase = overlap.phase(point, w, g) if overlap is not None else None
        if phase is None:
            return build(None)
        outs, extra = build(phase)
        overlap.done(point, extra, w, g)
        return outs

    h1, n1, G1, U1 = hosting("ffn1_fwd", lambda ph: _ffn_fwd(
        x2, w["ffn1_norm"], w["wg1"], w["wu1"], w["wd1"], None, tm=TM_WIDE, name="ffn1_fwd", phase=ph))
    u, n2 = hosting("inproj_fwd", lambda ph: _inproj_fwd(h1, w["mix_norm"], w["win"], tm=TM_WIDE, name="inproj_fwd", phase=ph))
    qkv = _qkv_prep(u, gains, B, S, name="qkv_prep")
    qkv = qkv.reshape(3, N_PATTERNS, T // QBLK, QBLK, D_ATTN)
    o3, lse3 = hosting("attn_fwd", lambda ph: _attn_fwd(qkv, name="attn_fwd", phase=ph))
    attn, lse = _attn_combine(o3, lse3, B, S, name="attn_combine")
    cv, glu, yconv = hosting("conv_fwd", lambda ph: _conv_fwd(
        u, w["conv_w"], w["conv_b"], w["conv_ln_g"], w["conv_ln_b"], B, S, tc=TC, name="conv_fwd", phase=ph))
    h2 = _outproj_fwd(h1, attn, cv, w["wout"], tm=TM_WIDE, name="outproj_fwd")
    dh3, n3, G2, U2, loss = _ffn_fwd(h2, w["ffn2_norm"], w["wg2"], w["wu2"], w["wd2"], t2, tm=TM_WIDE, name="ffn2_fwd")

    dG, dU, A, dy, dh2, g["ffn2_norm"] = _ffn_bwd_act(dh3, h2, w["ffn2_norm"], G2, U2, w["wg2"], w["wu2"], w["wd2"],
                                                    tm=TM, name="ffn2_bwd_act")
    g["wg2"], g["wu2"], g["wd2"] = _ffn_bwd_w(n3, dy, dG, dU, A, tk=TK, name="ffn2_bwd_w")
    dattn, dcv, g["wout"] = _outproj_bwd(dh2, attn, cv, w["wout"], tm=TM_WIDE, name="outproj_bwd")
    dyc, cpart = _conv_bwd_norm(dcv, yconv, w["conv_ln_g"], w["conv_ln_b"], tc=TC, name="conv_bwd_norm")
    dca, dcg, dcw = hosting("conv_bwd_taps", lambda ph: _conv_bwd_taps(
        dyc, glu, u, w["conv_w"], B, S, tc=TC, name="conv_bwd_taps", phase=ph))
    do3, st3 = _attn_bwd_prep(dattn, attn, lse, B, S, name="attn_bwd_prep")
    nb = T // QBLK
    (cur,) = hosting("attn_bwd", lambda ph: _attn_bwd(
        qkv, do3.reshape(N_PATTERNS, nb, QBLK, D_ATTN), st3.reshape(N_PATTERNS, nb, QBLK, LANES),
        name="attn_bwd", phase=ph))
    du_qkv, dgains = hosting("attn_grad_combine", lambda ph: _attn_grad_combine(
        cur, u, gains, B, S, name="attn_grad_combine", phase=ph))
    du = jnp.concatenate([du_qkv, dca, dcg], axis=1)
    (g["win"],) = hosting("inproj_bwd_w", lambda ph: _inproj_bwd_w(
        n2, du, w["win"].shape[0], tk=2 * TK, name="inproj_bwd_w", phase=ph))
    dh1, g["mix_norm"] = hosting("inproj_bwd_act", lambda ph: _inproj_bwd_act(
        du, dh2, h1, w["mix_norm"], w["win"], tm=TM_WIDE, name="inproj_bwd_act", phase=ph))
    dG, dU, A, dy, dx, g["ffn1_norm"] = hosting("ffn1_bwd_act", lambda ph: _ffn_bwd_act(
        dh1, x2, w["ffn1_norm"], G1, U1, w["wg1"], w["wu1"], w["wd1"], tm=TM, name="ffn1_bwd_act", phase=ph))
    g["wg1"], g["wu1"], g["wd1"] = hosting("ffn1_bwd_w", lambda ph: _ffn_bwd_w(
        n1, dy, dG, dU, A, tk=TK, name="ffn1_bwd_w", phase=ph))

    g["q_norm"] = dgains[0].reshape(HEADS, HEAD_DIM).sum(axis=0, keepdims=True) * scale
    g["k_norm"] = dgains[1].reshape(HEADS, HEAD_DIM).sum(axis=0, keepdims=True)
    g["conv_ln_g"] = cpart[0:1]
    g["conv_ln_b"] = cpart[1:2]
    g["conv_b"] = cpart[2:3]
    g["conv_w"] = dcw[:CONV_K]
    return loss, dx.reshape(B, S, D), g


N_CHIPS = 4
N_DEV = 8
VMEM_SPEC = pl.BlockSpec(memory_space=pltpu.VMEM)


def _remote(src, dst, send_sem, recv_sem, device):
    return pltpu.make_async_remote_copy(src_ref=src, dst_ref=dst, send_sem=send_sem, recv_sem=recv_sem,
                                        device_id=device, device_id_type=MESH)


def _stage_shards(shards, dtypes, *, name):
    n = len(shards)
    halves = [s.reshape(2, s.shape[0] // 2, s.shape[1]) for s in shards]

    def body(*refs):
        ins, outs, vms, loc_sems = refs[:n], refs[n:2 * n], refs[2 * n:3 * n], refs[3 * n]
        me = 2 * lax.axis_index("x") + lax.axis_index("y")
        copies = []
        for a in range(n):
            vms[a][...] = ins[a][...].astype(dtypes[a])
            cp = pltpu.make_async_copy(vms[a], outs[a].at[me], loc_sems.at[a])
            cp.start()
            copies.append(cp)
        for cp in copies:
            cp.wait()

    return pl.pallas_call(
        body, in_specs=[VMEM_SPEC] * n, out_specs=[ANY] * n,
        out_shape=[jax.ShapeDtypeStruct((N_CHIPS,) + h.shape, dt) for h, dt in zip(halves, dtypes)],
        scratch_shapes=[pltpu.VMEM(h.shape, dt) for h, dt in zip(halves, dtypes)] + [DMA_SEMS((n,))],
        compiler_params=pltpu.CompilerParams(vmem_limit_bytes=VMEM_LIMIT), name=name)(*halves)


def _like(arrays):
    return [jax.ShapeDtypeStruct(a.shape, a.dtype) for a in arrays]


def _axes():
    x, y, c = lax.axis_index("x"), lax.axis_index("y"), lax.axis_index("c")
    first = (x + (1 - c) * (1 - 2 * x), y + c * (1 - 2 * y))
    second = (x + c * (1 - 2 * x), y + (1 - c) * (1 - 2 * y))
    slots = tuple(2 * px + py for px, py in ((x, y), first, second, (1 - x, 1 - y)))
    return (x, y, c), (*first, c), (*second, c), slots


def _gather_ici_phase(bufs, only=None):
    n = len(bufs)

    def stage1(ins, outs, sems):
        (x, y, c), peer1, peer2, (own, s1, s2, both) = _axes()
        starts, arrivals = [], []
        for a in range(n):
            mine, land = outs[a].at[own, c], outs[a].at[s2, c]
            starts.append(_remote(mine, mine, *sems(a), peer2))
            arrivals.append(_remote(land, land, *sems(a), peer2))
        return starts, arrivals

    def stage2(ins, outs, sems):
        (x, y, c), peer1, peer2, (own, s1, s2, both) = _axes()
        starts, arrivals = [], []
        for a in range(n):
            for k, (src, dst) in enumerate(((own, s1), (s2, both))):
                mine, land = outs[a].at[src, c], outs[a].at[dst, c]
                starts.append(_remote(mine, mine, *sems(2 * a + k), peer1))
                arrivals.append(_remote(land, land, *sems(2 * a + k), peer1))
        return starts, arrivals

    same = {a: a for a in range(n)}
    first, second = _Phase(bufs, _like(bufs), same, n, stage1), _Phase(bufs, _like(bufs), same, 2 * n, stage2)
    if only is None:
        return first.then(second)
    return first if only == 1 else second


def _gather_d2d_phase(bufs):
    n = len(bufs)

    def copies(ins, outs, sems):
        (x, y, c), peer1, peer2, (own, s1, s2, both) = _axes()
        starts, arrivals = [], []
        for a in range(n):
            for j, s in enumerate((s1, s2, both)):
                got, land = outs[a].at[s, c], outs[a].at[s, 1 - c]
                starts.append(_remote(got, got, *sems(3 * a + j), (x, y, 1 - c)))
                arrivals.append(_remote(land, land, *sems(3 * a + j), (x, y, 1 - c)))
        return starts, arrivals

    return _Phase(bufs, _like(bufs), {a: a for a in range(n)}, 3 * n, copies)


def _gather_pipelined_phase(bufs):
    n = len(bufs)

    def piece(kind, a):
        def copies(ins, outs, sems):
            (x, y, c), peer1, peer2, (own, s1, s2, both) = _axes()
            moves = {1: [(own, s2, c, c, peer2)],
                     2: [(own, s1, c, c, peer1), (s2, both, c, c, peer1)],
                     3: [(s, s, c, 1 - c, (x, y, 1 - c)) for s in (s1, s2, both)]}[kind]
            starts, arrivals = [], []
            for k, (src, dst, h_src, h_dst, peer) in enumerate(moves):
                mine, land = outs[a].at[src, h_src], outs[a].at[dst, h_dst]
                starts.append(_remote(mine, mine, *sems(k), peer))
                arrivals.append(_remote(land, land, *sems(k), peer))
            return starts, arrivals

        return kind, copies

    def beside(pieces):
        def copies(ins, outs, sems):
            starts, arrivals, base = [], [], 0
            for count, fn in pieces:
                s, r = fn(ins, outs, lambda i, base=base: sems(base + i))
                starts, arrivals, base = starts + s, arrivals + r, base + count
            return starts, arrivals

        return sum(count for count, _ in pieces), copies

    phase = _Phase(bufs, _like(bufs), {a: a for a in range(n)}, 0, None)
    phase.stages = [beside([piece(kind, t - kind + 1) for kind in (1, 2, 3) if 0 <= t - kind + 1 < n])
                    for t in range(n + 2)]
    return phase


def _exchange_phase(views):
    n = len(views)

    def copies(ins, outs, sems):
        x, y, c = lax.axis_index("x"), lax.axis_index("y"), lax.axis_index("c")
        starts = [_remote(ins[a].at[pl.ds(0, ins[a].shape[0]), 1 - c], outs[a], *sems(a), (x, y, 1 - c))
                  for a in range(n)]
        return starts, starts

    outs = [jax.ShapeDtypeStruct((v.shape[0],) + v.shape[2:], F32) for v in views]
    return _Phase(views, outs, {}, n, copies)


ADD_SPLIT = 2


def _add_halves(views, got, sel, tag):
    n = len(views)

    def body(s_ref, *refs):
        ins, outs = refs[:4 * n], refs[4 * n:]
        for a in range(n):
            gk, rk, gs, rs = ins[4 * a:4 * a + 4]
            outs[2 * a][...] = gk[...] + rk[...]
            outs[2 * a + 1][...] = (gs[...] + rs[...]).astype(BF16)

    in_specs, out_specs, out_shape, args = [], [], [], []
    for g, r in zip(views, got):
        _, _, rh, cdim = g.shape
        tr = rh // ADD_SPLIT
        for off in (0, 2):
            in_specs.append(pl.BlockSpec((None, None, tr, cdim), lambda k, i, s, off=off: (s[1 + off + k], s[0], i, 0)))
            in_specs.append(pl.BlockSpec((None, tr, cdim), lambda k, i, s, off=off: (s[1 + off + k], i, 0)))
            args += [g, r]
        out_specs += [pl.BlockSpec((None, tr, cdim), lambda k, i, s: (k, i, 0))] * 2
        out_shape += [jax.ShapeDtypeStruct((2, rh, cdim), F32), jax.ShapeDtypeStruct((2, rh, cdim), BF16)]
    res = pl.pallas_call(
        body,
        grid_spec=pltpu.PrefetchScalarGridSpec(num_scalar_prefetch=1, grid=(2, ADD_SPLIT), in_specs=in_specs,
                                               out_specs=out_specs),
        out_shape=out_shape, compiler_params=_params("arbitrary", "arbitrary"), name=f"rs_add_half_{tag}")(sel, *args)
    return list(res[0::2]), list(res[1::2])


def _swap_phase(arrays, stage):
    n = len(arrays)

    def copies(ins, outs, sems):
        peer = _axes()[stage]
        starts = [_remote(ins[a], outs[a], *sems(a), peer) for a in range(n)]
        return starts, starts

    return _Phase(arrays, _like(arrays), {}, n, copies)


def _add_first(keep, got, tag):
    n = len(keep)

    def body(*refs):
        ins, outs = refs[:2 * n], refs[2 * n:]
        for a in range(n):
            k_ref, g_ref = ins[2 * a], ins[2 * a + 1]
            outs[2 * a][...] = k_ref[0] + g_ref[0].astype(F32)
            outs[2 * a + 1][...] = (k_ref[1] + g_ref[1].astype(F32)).astype(BF16)

    in_specs, out_specs, out_shape, args = [], [], [], []
    for k, g in zip(keep, got):
        _, rh, cdim = k.shape
        tr = rh // ADD_SPLIT
        in_specs += [pl.BlockSpec((2, tr, cdim), lambda i: (0, i, 0))] * 2
        out_specs += [pl.BlockSpec((tr, cdim), lambda i: (i, 0))] * 2
        out_shape += [jax.ShapeDtypeStruct((rh, cdim), F32), jax.ShapeDtypeStruct((rh, cdim), BF16)]
        args += [k, g]
    res = pl.pallas_call(body, grid=(ADD_SPLIT,), in_specs=in_specs, out_specs=out_specs, out_shape=out_shape,
                         compiler_params=_params("arbitrary"), name=f"rs_add_first_{tag}")(*args)
    return list(res[0::2]), list(res[1::2])


def _add_second(keep, got, sel, tag):
    n = len(keep)

    def body(s_ref, *refs):
        ins, outs = refs[:2 * n], refs[2 * n:]
        for a in range(n):
            outs[a][...] = ins[2 * a][...] + ins[2 * a + 1][...].astype(F32)

    in_specs, out_specs, out_shape, args = [], [], [], []
    for k, g in zip(keep, got):
        rh, cdim = k.shape
        tr = rh // ADD_SPLIT
        in_specs += [pl.BlockSpec((tr, cdim), lambda i, s: (i, 0))] * 2
        out_specs.append(pl.BlockSpec((None, tr, cdim), lambda i, s: (s[0], i, 0)))
        out_shape.append(jax.ShapeDtypeStruct((2, rh, cdim), F32))
        args += [k, g]
    res = pl.pallas_call(
        body,
        grid_spec=pltpu.PrefetchScalarGridSpec(num_scalar_prefetch=1, grid=(ADD_SPLIT,), in_specs=in_specs,
                                               out_specs=out_specs),
        out_shape=out_shape, compiler_params=_params("arbitrary"), name=f"rs_add_second_{tag}")(sel, *args)
    return list(res)


def _join_phase(halves):
    n = len(halves)

    def copies(ins, outs, sems):
        x, y, c = lax.axis_index("x"), lax.axis_index("y"), lax.axis_index("c")
        starts, arrivals = [], []
        for a in range(n):
            mine, land = outs[a].at[c], outs[a].at[1 - c]
            starts.append(_remote(mine, mine, *sems(a), (x, y, 1 - c)))
            arrivals.append(_remote(land, land, *sems(a), (x, y, 1 - c)))
        return starts, arrivals

    return _Phase(halves, _like(halves), {a: a for a in range(n)}, n, copies)


def _slot_order():
    x, y, c = lax.axis_index("x"), lax.axis_index("y"), lax.axis_index("c")
    own, flip_x, flip_y, both = 2 * x + y, 2 * (1 - x) + y, 2 * x + 1 - y, 2 * (1 - x) + 1 - y
    first = jnp.where(c == 0, flip_x, flip_y)
    second = jnp.where(c == 0, flip_y, flip_x)
    return jnp.stack([c, own, second, first, both]).astype(jnp.int32)


def _half_view(g):
    return g.reshape(N_CHIPS, 2, g.shape[1] // 2, g.shape[2])


EARLY_GRADS = ("wg2", "wu2", "wd2", "wout")
MIDDLE_GRADS = ("win",)


class _Overlap:
    EARLY_AT = ("conv_bwd_taps", "attn_bwd", "attn_grad_combine", "inproj_bwd_w")
    MIDDLE_AT = ("inproj_bwd_act", "ffn1_bwd_act", "ffn1_bwd_w", None)

    def __init__(self, staged):
        self.staged = staged
        self.ffn2 = list(staged[3:])
        self.sel = sel = _slot_order()
        self.early = _Reduction(EARLY_GRADS, "early", sel)
        self.middle = _Reduction(MIDDLE_GRADS, "middle", sel)

    def finish_late(self, late):
        views = [_half_view(a) for a in late]
        got = _run_phase(_exchange_phase(views), name="rs_exchange_halves")
        keep, send = _add_halves(views, got, self.sel, "late")
        got = _run_phase(_swap_phase(send, 1), name="rs_swap_first_axis")
        keep, send = _add_first(keep, got, "late")
        got = _run_phase(_swap_phase(send, 2), name="rs_swap_second_axis")
        halves = _add_second(keep, got, self.sel, "late")
        full = _run_phase(_join_phase(halves + list(self.middle.halves)), name="rs_join_halves")
        return [f.reshape(-1, f.shape[-1]) for f in full]

    def phase(self, point, w, g):
        if point == "ffn1_fwd":
            return _gather_ici_phase(self.staged[:3])
        if point == "inproj_fwd":
            return _gather_ici_phase(self.ffn2, only=1)
        if point == "attn_fwd":
            return _gather_ici_phase(self.ffn2, only=2)
        if point == "conv_fwd":
            return _gather_d2d_phase(self.ffn2)
        for red, at in ((self.early, self.EARLY_AT), (self.middle, self.MIDDLE_AT)):
            if point in at:
                return red.phase(at.index(point), g)
        return None

    def done(self, point, outs, w, g):
        if point == "ffn1_fwd":
            win, wout, taps = [_whole(b) for b in _run_phase(_gather_d2d_phase(outs), name="gather_mix_d2d")]
            w["win"] = win
            w["wout"] = wout.reshape(-1, wout.shape[-1])
            w["conv_w"] = taps.transpose(1, 0, 2).reshape(CONV_K + 1, D_CONV)[:CONV_K]
        elif point in ("inproj_fwd", "attn_fwd"):
            self.ffn2 = list(outs)
        elif point == "conv_fwd":
            w["wg2"], w["wu2"], w["wd2"] = [_whole(b) for b in outs]
        for red, at in ((self.early, self.EARLY_AT), (self.middle, self.MIDDLE_AT)):
            if point in at:
                red.done(at.index(point), outs)


class _Reduction:
    def __init__(self, names, tag, sel):
        self.names, self.tag, self.sel = names, tag, sel
        self.reduced = {}

    def phase(self, stage, g):
        if stage == 0:
            self.cols = [g[k].shape[-1] for k in self.names]
            self.views = [_half_view(g[k].reshape(N_CHIPS, -1, g[k].shape[-1])) for k in self.names]
            return _exchange_phase(self.views)
        if stage in (1, 2):
            return _swap_phase(self.send, stage)
        return _join_phase(self.halves)

    def done(self, stage, outs):
        if stage == 0:
            self.keep, self.send = _add_halves(self.views, outs, self.sel, self.tag)
        elif stage == 1:
            self.keep, self.send = _add_first(self.keep, outs, self.tag)
        elif stage == 2:
            self.halves = _add_second(self.keep, outs, self.sel, self.tag)
        else:
            for k, c, f in zip(self.names, self.cols, outs):
                self.reduced[k] = f.reshape(-1, c)


def _whole(buf):
    return buf.reshape(buf.shape[0], 2 * buf.shape[2], buf.shape[3])


def _allreduce_small(pack, *, name):
    rows = pack.shape[0]

    def body(p_ref, o_ref, buf_ref, send_sems, recv_sems):
        x, y, c = lax.axis_index("x"), lax.axis_index("y"), lax.axis_index("c")
        me = 4 * x + 2 * y + c
        buf_ref[me] = p_ref[...]
        cps = []
        for k in range(1, N_DEV):
            peer = tuple(1 - v if (k >> s) & 1 else v for v, s in ((x, 2), (y, 1), (c, 0)))
            cp = _remote(p_ref, buf_ref.at[me], send_sems.at[k - 1], recv_sems.at[k - 1], peer)
            cp.start()
            cps.append(cp)
        for k in range(1, N_DEV):
            src = 4 * (x ^ ((k >> 2) & 1)) + 2 * (y ^ ((k >> 1) & 1)) + (c ^ (k & 1))
            land = buf_ref.at[src]
            _remote(land, land, send_sems.at[k - 1], recv_sems.at[k - 1], (x, y, c)).wait_recv()
        acc = buf_ref[0]
        for d in range(1, N_DEV):
            acc = acc + buf_ref[d]
        o_ref[...] = acc
        for cp in cps:
            cp.wait_send()

    return pl.pallas_call(
        body, in_specs=[VMEM_SPEC], out_specs=VMEM_SPEC, out_shape=jax.ShapeDtypeStruct(pack.shape, F32),
        scratch_shapes=[pltpu.VMEM((N_DEV, rows, LANES), F32), pltpu.SemaphoreType.DMA((N_DEV - 1,)),
                        pltpu.SemaphoreType.DMA((N_DEV - 1,))], name=name)(pack)


SMALL = ("ffn1_norm", "mix_norm", "q_norm", "k_norm", "conv_b", "conv_ln_g", "conv_ln_b", "ffn2_norm", "conv_w")
BIG = ("ffn1_w_gate", "ffn1_w_up", "ffn1_w_down", "w_in", "w_out", "ffn2_w_gate", "ffn2_w_up", "ffn2_w_down")
TRANSPOSED = ("ffn1_w_gate", "ffn1_w_up", "ffn2_w_gate", "ffn2_w_up")
WEIGHTS = ("ffn1_norm", "ffn1_w_gate", "ffn1_w_up", "ffn1_w_down", "mix_norm", "w_in", "q_norm", "k_norm",
           "conv_w", "conv_b", "conv_ln_g", "conv_ln_b", "w_out", "ffn2_norm", "ffn2_w_gate", "ffn2_w_up",
           "ffn2_w_down")


def _pack(parts):
    rows = []
    for p in parts:
        flat = p.reshape(-1)
        tile = SUBLANES * LANES
        padded = -(-flat.shape[0] // tile) * tile
        rows.append(jnp.pad(flat, (0, padded - flat.shape[0])).reshape(-1, LANES))
    return jnp.concatenate(rows, axis=0)


def _unpack(pack, shapes):
    out, row = [], 0
    for shp in shapes:
        size = shp[0] * shp[1]
        tile = SUBLANES * LANES
        nrows = -(-size // tile) * SUBLANES
        out.append(pack[row:row + nrows].reshape(-1)[:size].reshape(shp))
        row += nrows
    return out


def kernel(x, ffn1_norm, ffn1_w_gate, ffn1_w_up, ffn1_w_down, mix_norm, w_in, q_norm, k_norm, conv_w, conv_b, conv_ln_g, conv_ln_b, w_out, ffn2_norm, ffn2_w_gate, ffn2_w_up, ffn2_w_down, loss_target, m_ffn1_norm, m_ffn1_w_gate, m_ffn1_w_up, m_ffn1_w_down, m_mix_norm, m_w_in, m_q_norm, m_k_norm, m_conv_w, m_conv_b, m_conv_ln_g, m_conv_ln_b, m_w_out, m_ffn2_norm, m_ffn2_w_gate, m_ffn2_w_up, m_ffn2_w_down, v_ffn1_norm, v_ffn1_w_gate, v_ffn1_w_up, v_ffn1_w_down, v_mix_norm, v_w_in, v_q_norm, v_k_norm, v_conv_w, v_conv_b, v_conv_ln_g, v_conv_ln_b, v_w_out, v_ffn2_norm, v_ffn2_w_gate, v_ffn2_w_up, v_ffn2_w_down):
    wts = dict(ffn1_norm=ffn1_norm, ffn1_w_gate=ffn1_w_gate[0], ffn1_w_up=ffn1_w_up[0], ffn1_w_down=ffn1_w_down[0],
               mix_norm=mix_norm, w_in=w_in[0], q_norm=q_norm, k_norm=k_norm, conv_w=conv_w[0], conv_b=conv_b,
               conv_ln_g=conv_ln_g, conv_ln_b=conv_ln_b, w_out=w_out[0], ffn2_norm=ffn2_norm,
               ffn2_w_gate=ffn2_w_gate[0], ffn2_w_up=ffn2_w_up[0], ffn2_w_down=ffn2_w_down[0])
    mom = dict(ffn1_norm=m_ffn1_norm, ffn1_w_gate=m_ffn1_w_gate[0], ffn1_w_up=m_ffn1_w_up[0], ffn1_w_down=m_ffn1_w_down[0],
               mix_norm=m_mix_norm, w_in=m_w_in[0], q_norm=m_q_norm, k_norm=m_k_norm, conv_w=m_conv_w[0], conv_b=m_conv_b,
               conv_ln_g=m_conv_ln_g, conv_ln_b=m_conv_ln_b, w_out=m_w_out[0], ffn2_norm=m_ffn2_norm,
               ffn2_w_gate=m_ffn2_w_gate[0], ffn2_w_up=m_ffn2_w_up[0], ffn2_w_down=m_ffn2_w_down[0])
    var = dict(ffn1_norm=v_ffn1_norm, ffn1_w_gate=v_ffn1_w_gate[0], ffn1_w_up=v_ffn1_w_up[0], ffn1_w_down=v_ffn1_w_down[0],
               mix_norm=v_mix_norm, w_in=v_w_in[0], q_norm=v_q_norm, k_norm=v_k_norm, conv_w=v_conv_w[0], conv_b=v_conv_b,
               conv_ln_g=v_conv_ln_g, conv_ln_b=v_conv_ln_b, w_out=v_w_out[0], ffn2_norm=v_ffn2_norm,
               ffn2_w_gate=v_ffn2_w_gate[0], ffn2_w_up=v_ffn2_w_up[0], ffn2_w_down=v_ffn2_w_down[0])
    chip = 2 * lax.axis_index("x") + lax.axis_index("y")
    for src in (wts, mom, var):
        for n in TRANSPOSED:
            src[n] = src[n].T

    taps = jnp.pad(wts["conv_w"], ((0, 1), (0, 0)))
    staged = _stage_shards([wts["ffn1_w_gate"], wts["ffn1_w_up"], wts["ffn1_w_down"], wts["w_in"], wts["w_out"], taps,
                            wts["ffn2_w_gate"], wts["ffn2_w_up"], wts["ffn2_w_down"]],
                           [BF16, BF16, BF16, BF16, BF16, F32, BF16, BF16, BF16], name="stage_shards")
    first = _run_phase(_gather_pipelined_phase(staged[:3]), name="gather_ffn1")
    wg1, wu1, wd1 = [_whole(b) for b in first]
    w = dict(ffn1_norm=ffn1_norm, mix_norm=mix_norm, ffn2_norm=ffn2_norm, q_norm=q_norm, k_norm=k_norm,
             conv_b=conv_b, conv_ln_g=conv_ln_g, conv_ln_b=conv_ln_b, wg1=wg1, wu1=wu1, wd1=wd1)
    overlap = _Overlap(staged[3:])
    loss_part, grad_x, g = _local_step(x, loss_target, w, overlap)

    grads, delta, new_m, new_v = {}, {}, {}, {}
    early = overlap.early.reduced
    grads.update(ffn2_w_gate=early["wg2"], ffn2_w_up=early["wu2"], ffn2_w_down=early["wd2"], w_out=early["wout"])
    grads.update(zip(("ffn1_w_gate", "ffn1_w_up", "ffn1_w_down", "w_in"),
                     overlap.finish_late([g["wg1"], g["wu1"], g["wd1"]])))

    small_shapes = [g[n].shape for n in SMALL] + [(SUBLANES, LANES)]
    red = _allreduce_small(_pack([g[n] for n in SMALL] + [loss_part]), name="allreduce_small")
    small = dict(zip(SMALL + ("loss",), _unpack(red, small_shapes)))
    loss = small["loss"][0, 0]
    small["conv_w"] = lax.dynamic_slice_in_dim(small["conv_w"], chip * LANES, LANES, axis=1)

    for tag, names in (("early", ("ffn2_w_gate", "ffn2_w_up", "ffn2_w_down", "w_out")),
                       ("late", ("ffn1_w_gate", "ffn1_w_up", "ffn1_w_down", "w_in"))):
        d, m, v, _ = _adamw_many([wts[n] for n in names], [grads[n] for n in names], [mom[n] for n in names],
                                 [var[n] for n in names], name=f"adamw_{tag}")
        for dst, vals in ((delta, d), (new_m, m), (new_v, v)):
            dst.update(zip(names, vals))
    shapes = [wts[n].shape for n in SMALL]
    packs = [_pack([src[n] for n in SMALL]) for src in (wts, small, mom, var)]
    outs = _adamw(*packs, name="adamw_small")
    for dst, pk in zip((delta, new_m, new_v), outs):
        dst.update(zip(SMALL, _unpack(pk, shapes)))
    for n in SMALL:
        grads[n] = small[n]

    def shaped(d, n):
        v = d[n].T if n in TRANSPOSED else d[n]
        return v.reshape((1,) + v.shape) if n in BIG or n == "conv_w" else v

    return (loss, grad_x, *[shaped(grads, n) for n in WEIGHTS], *[shaped(delta, n) for n in WEIGHTS],
            *[shaped(new_m, n) for n in WEIGHTS], *[shaped(new_v, n) for n in WEIGHTS])
```

```python
import functools

import jax
import jax.numpy as jnp
from jax import lax
from jax.experimental import pallas as pl
from jax.experimental.pallas import tpu as pltpu

F32 = jnp.float32
BF16 = jnp.bfloat16

EPS = 1e-6
HEADS = 8
HEAD_DIM = 64
D_ATTN = HEADS * HEAD_DIM
D_CONV = 512
CONV_K = 31
QBLK = 128
N_PATTERNS = 3
DILATIONS = (1, 4, 16)
LANES = 128
NEG = -1e30

ADAM_LR = 0.001
ADAM_B1 = 0.9
ADAM_B2 = 0.999
ADAM_EPS = 1e-08
ADAM_WD = 0.01
ADAM_STEP = 10

VMEM_LIMIT = 56 * 1024 * 1024
MESH = pl.DeviceIdType.MESH

NT_DIMS = (((1,), (1,)), ((), ()))
TN_DIMS = (((0,), (0,)), ((), ()))


def _params(*sem):
    return pltpu.CompilerParams(dimension_semantics=sem, vmem_limit_bytes=VMEM_LIMIT)


def _dot(a, b):
    return jnp.dot(a, b, preferred_element_type=F32)


def _dot_nt(a, b):
    return lax.dot_general(a, b, NT_DIMS, preferred_element_type=F32)


def _dot_tn(a, b):
    return lax.dot_general(a, b, TN_DIMS, preferred_element_type=F32)


def _sigmoid(x):
    return 1.0 / (1.0 + jnp.exp(-x))


def _seg_mean(v, e_ref, width):
    hi = v.astype(BF16)
    lo = (v - hi.astype(F32)).astype(BF16)
    e = e_ref[...]
    return (_dot(hi, e) + _dot(lo, e)) * (1.0 / width)


def _seg_matrix(n):
    i = jnp.arange(n)
    return (i[:, None] // HEAD_DIM == i[None, :] // HEAD_DIM).astype(BF16)


ANY = pl.BlockSpec(memory_space=pl.ANY)
DMA_SEMS = pltpu.SemaphoreType.DMA


class _Phase:
    def __init__(self, ins, outs, aliases, nsem, copies):
        self.ins, self.outs, self.aliases = list(ins), list(outs), dict(aliases)
        self.stages = [(nsem, copies)]

    def then(self, other):
        self.stages = self.stages + other.stages
        return self

    @property
    def nsem(self):
        return sum(n for n, _ in self.stages)

    def _copies(self, k, in_refs, out_refs, send_sems, recv_sems):
        base = sum(n for n, _ in self.stages[:k])
        return self.stages[k][1](in_refs, out_refs, lambda i: (send_sems.at[base + i], recv_sems.at[base + i]))

    def start(self, k, *refs):
        for cp in self._copies(k, *refs)[0]:
            cp.start()

    def finish(self, k, *refs):
        starts, arrivals = self._copies(k, *refs)
        for cp in arrivals:
            cp.wait_recv()
        for cp in starts:
            cp.wait_send()


def _run_phase(phase, *, name):
    n_in, n_out = len(phase.ins), len(phase.outs)

    def body(*refs):
        ins, outs = refs[:n_in], refs[n_in:n_in + n_out]
        send_sems, recv_sems = refs[n_in + n_out:]
        for k in range(len(phase.stages)):
            phase.start(k, ins, outs, send_sems, recv_sems)
            phase.finish(k, ins, outs, send_sems, recv_sems)

    return pl.pallas_call(
        body, in_specs=[ANY] * n_in, out_specs=[ANY] * n_out, out_shape=phase.outs,
        input_output_aliases=phase.aliases,
        scratch_shapes=[DMA_SEMS((phase.nsem,)), DMA_SEMS((phase.nsem,))], name=name)(*phase.ins)


def _call(body, *, grid, in_specs, out_specs, out_shape, scratch_shapes=(), sem, name, args, phase=None):
    in_specs, out_specs, out_shape = list(in_specs), list(out_specs), list(out_shape)
    scratch_shapes = list(scratch_shapes)
    if phase is None:
        return pl.pallas_call(body, grid=grid, in_specs=in_specs, out_specs=out_specs, out_shape=out_shape,
                              scratch_shapes=scratch_shapes, compiler_params=_params(*sem), name=name)(*args)
    n_in, n_out, n_scr = len(in_specs), len(out_specs), len(scratch_shapes)
    p_in, p_out = len(phase.ins), len(phase.outs)

    def hosted(*refs):
        ins, pins = refs[:n_in], refs[n_in:n_in + p_in]
        o0 = n_in + p_in
        outs, pouts = refs[o0:o0 + n_out], refs[o0 + n_out:o0 + n_out + p_out]
        s0 = o0 + n_out + p_out
        scr = refs[s0:s0 + n_scr]
        send_sems, recv_sems = refs[s0 + n_scr:]
        step = 0
        for d, n in enumerate(grid):
            step = step * n + pl.program_id(d)
        nsteps = functools.reduce(lambda a, b: a * b, grid)
        nstages = len(phase.stages)
        comm_refs = (pins, pouts, send_sems, recv_sems)

        for k in range(nstages):
            @pl.when(step == (k * nsteps) // nstages)
            def _(k=k):
                if k > 0:
                    phase.finish(k - 1, *comm_refs)
                phase.start(k, *comm_refs)

        body(*ins, *outs, *scr)

        @pl.when(step == nsteps - 1)
        def _():
            phase.finish(nstages - 1, *comm_refs)

    res = pl.pallas_call(
        hosted, grid=grid, in_specs=in_specs + [ANY] * p_in, out_specs=out_specs + [ANY] * p_out,
        out_shape=out_shape + phase.outs,
        input_output_aliases={n_in + i: n_out + o for i, o in phase.aliases.items()},
        scratch_shapes=scratch_shapes + [DMA_SEMS((phase.nsem,)), DMA_SEMS((phase.nsem,))],
        compiler_params=_params(*sem), name=name)(*args, *phase.ins)
    return res[:n_out], res[n_out:]


ROW_CHUNK = 256


def _ffn_fwd(x, gain, wg, wu, wd, tgt, *, tm, name, phase=None):
    T, D = x.shape
    NS, Fs, _ = wg.shape
    with_loss = tgt is not None

    def body(*refs):
        if with_loss:
            x_ref, g_ref, wg_ref, wu_ref, wd_ref, t_ref, h_ref, n_ref, G_ref, U_ref, loss_ref, acc_ref = refs
        else:
            x_ref, g_ref, wg_ref, wu_ref, wd_ref, h_ref, n_ref, G_ref, U_ref, acc_ref = refs
        i = pl.program_id(0)
        j = pl.program_id(1)

        @pl.when(j == 0)
        def _():
            xv = x_ref[...]
            r = lax.rsqrt(jnp.mean(xv * xv, axis=-1, keepdims=True) + EPS)
            n_ref[...] = (xv * r * g_ref[...]).astype(BF16)
            acc_ref[...] = jnp.zeros_like(acc_ref)

        n = n_ref[...]
        G = _dot_nt(n, wg_ref[...])
        U = _dot_nt(n, wu_ref[...])
        G_ref[...] = G.astype(BF16)
        U_ref[...] = U.astype(BF16)
        A = (G * _sigmoid(G) * U).astype(BF16)
        acc_ref[...] += _dot(A, wd_ref[...])

        @pl.when(j == NS - 1)
        def _():
            h = x_ref[...] + 0.5 * acc_ref[...]
            if with_loss:
                e = h - t_ref[...]
                h_ref[...] = e * (1.0 / D)

                @pl.when(i == 0)
                def _():
                    loss_ref[...] = jnp.zeros_like(loss_ref)

                loss_ref[...] += jnp.sum(e * e) * (0.5 / D)
            else:
                h_ref[...] = h

    tok = pl.BlockSpec((tm, D), lambda i, j: (i, 0))
    in_specs = [tok, pl.BlockSpec((1, D), lambda i, j: (0, 0)),
                pl.BlockSpec((None, Fs, D), lambda i, j: (j, 0, 0)),
                pl.BlockSpec((None, Fs, D), lambda i, j: (j, 0, 0)),
                pl.BlockSpec((None, Fs, D), lambda i, j: (j, 0, 0))]
    args = [x, gain, wg, wu, wd]
    act = pl.BlockSpec((None, tm, Fs), lambda i, j: (j, i, 0))
    out_shape = [jax.ShapeDtypeStruct((T, D), F32), jax.ShapeDtypeStruct((T, D), BF16),
                 jax.ShapeDtypeStruct((NS, T, Fs), BF16), jax.ShapeDtypeStruct((NS, T, Fs), BF16)]
    out_specs = [tok, tok, act, act]
    if with_loss:
        in_specs.append(tok)
        args.append(tgt)
        out_shape.append(jax.ShapeDtypeStruct((8, LANES), F32))
        out_specs.append(pl.BlockSpec((8, LANES), lambda i, j: (0, 0)))
    return _call(body, grid=(T // tm, NS), in_specs=in_specs, out_specs=out_specs, out_shape=out_shape,
                 scratch_shapes=[pltpu.VMEM((tm, D), F32)], sem=("arbitrary", "arbitrary"), name=name,
                 args=args, phase=phase)


def _rms_bwd(xv, gain, dn):
    r = lax.rsqrt(jnp.mean(xv * xv, axis=-1, keepdims=True) + EPS)
    xhat = xv * r
    dxh = dn * gain
    dx = r * (dxh - xhat * jnp.mean(dxh * xhat, axis=-1, keepdims=True))
    dg = jnp.sum(dn * xhat, axis=0, keepdims=True)
    return dx, dg


def _ffn_bwd_act(dh, x, gain, G, U, wg, wu, wd, *, tm, name, phase=None):
    T, D = x.shape
    NS, Fs, _ = wg.shape

    def body(dh_ref, x_ref, g_ref, G_ref, U_ref, wg_ref, wu_ref, wd_ref,
             dG_ref, dU_ref, A_ref, dy_ref, dx_ref, dg_ref, acc_ref):
        i = pl.program_id(0)
        j = pl.program_id(1)

        @pl.when(j == 0)
        def _():
            dy_ref[...] = (0.5 * dh_ref[...]).astype(BF16)
            acc_ref[...] = jnp.zeros_like(acc_ref)

        @pl.when((i == 0) & (j == 0))
        def _():
            dg_ref[...] = jnp.zeros_like(dg_ref)

        nchunks = tm // ROW_CHUNK
        dA, dGU = {}, {}
        for step in range(nchunks + 2):
            if step < nchunks:
                rows = slice(step * ROW_CHUNK, (step + 1) * ROW_CHUNK)
                dA[step] = _dot_nt(dy_ref[rows, :], wd_ref[...])
            if 1 <= step <= nchunks:
                k = step - 1
                rows = slice(k * ROW_CHUNK, (k + 1) * ROW_CHUNK)
                Gv = G_ref[rows, :].astype(F32)
                Uv = U_ref[rows, :].astype(F32)
                sig = _sigmoid(Gv)
                s = Gv * sig
                dG = (dA[k] * Uv * (sig * (1.0 + Gv * (1.0 - sig)))).astype(BF16)
                dU = (dA.pop(k) * s).astype(BF16)
                dG_ref[rows, :] = dG
                dU_ref[rows, :] = dU
                A_ref[rows, :] = (s * Uv).astype(BF16)
                dGU[k] = (dG, dU)
            if 2 <= step:
                k = step - 2
                rows = slice(k * ROW_CHUNK, (k + 1) * ROW_CHUNK)
                dG, dU = dGU.pop(k)
                acc_ref[rows, :] += _dot(dG, wg_ref[...]) + _dot(dU, wu_ref[...])

        @pl.when(j == NS - 1)
        def _():
            dx, dg = _rms_bwd(x_ref[...], g_ref[...], acc_ref[...])
            dx_ref[...] = dh_ref[...] + dx
            dg_ref[...] += dg

    tok = pl.BlockSpec((tm, D), lambda i, j: (i, 0))
    act = pl.BlockSpec((None, tm, Fs), lambda i, j: (j, i, 0))
    vec = pl.BlockSpec((1, D), lambda i, j: (0, 0))
    return _call(
        body, grid=(T // tm, NS),
        in_specs=[tok, tok, vec, act, act,
                  pl.BlockSpec((None, Fs, D), lambda i, j: (j, 0, 0)),
                  pl.BlockSpec((None, Fs, D), lambda i, j: (j, 0, 0)),
                  pl.BlockSpec((None, Fs, D), lambda i, j: (j, 0, 0))],
        out_specs=[act, act, act, tok, tok, vec],
        out_shape=[jax.ShapeDtypeStruct((NS, T, Fs), BF16)] * 3
        + [jax.ShapeDtypeStruct((T, D), BF16), jax.ShapeDtypeStruct((T, D), F32),
           jax.ShapeDtypeStruct((1, D), F32)],
        scratch_shapes=[pltpu.VMEM((tm, D), F32)],
        sem=("arbitrary", "arbitrary"), name=name, args=(dh, x, gain, G, U, wg, wu, wd), phase=phase)


def _ffn_bwd_w(n, dy, dG, dU, A, *, tk, name, phase=None):
    T, D = n.shape
    NS, _, Fs = dG.shape

    def body(n_ref, dy_ref, dG_ref, dU_ref, A_ref, wg_ref, wu_ref, wd_ref):
        @pl.when(pl.program_id(1) == 0)
        def _():
            wg_ref[...] = jnp.zeros_like(wg_ref)
            wu_ref[...] = jnp.zeros_like(wu_ref)
            wd_ref[...] = jnp.zeros_like(wd_ref)

        nv = n_ref[...]
        wg_ref[...] += _dot_tn(dG_ref[...], nv)
        wu_ref[...] += _dot_tn(dU_ref[...], nv)
        wd_ref[...] += _dot_tn(A_ref[...], dy_ref[...])

    tok = pl.BlockSpec((tk, D), lambda j, k: (k, 0))
    act = pl.BlockSpec((None, tk, Fs), lambda j, k: (j, k, 0))
    return _call(
        body, grid=(NS, T // tk), in_specs=[tok, tok, act, act, act],
        out_specs=[pl.BlockSpec((None, Fs, D), lambda j, k: (j, 0, 0))] * 3,
        out_shape=[jax.ShapeDtypeStruct((NS, Fs, D), F32)] * 3,
        sem=("arbitrary", "arbitrary"), name=name, args=(n, dy, dG, dU, A), phase=phase)


def _inproj_fwd(h, gain, win, *, tm, name, phase=None):
    T, D = h.shape
    NS, _, Cs = win.shape

    def body(h_ref, g_ref, w_ref, u_ref, n_ref):
        @pl.when(pl.program_id(1) == 0)
        def _():
            xv = h_ref[...]
            r = lax.rsqrt(jnp.mean(xv * xv, axis=-1, keepdims=True) + EPS)
            n_ref[...] = (xv * r * g_ref[...]).astype(BF16)

        u_ref[...] = _dot(n_ref[...], w_ref[...])

    tok = pl.BlockSpec((tm, D), lambda i, j: (i, 0))
    return _call(
        body, grid=(T // tm, NS),
        in_specs=[tok, pl.BlockSpec((1, D), lambda i, j: (0, 0)),
                  pl.BlockSpec((None, D, Cs), lambda i, j: (j, 0, 0))],
        out_specs=[pl.BlockSpec((tm, Cs), lambda i, j: (i, j)), tok],
        out_shape=[jax.ShapeDtypeStruct((T, NS * Cs), F32), jax.ShapeDtypeStruct((T, D), BF16)],
        sem=("arbitrary", "arbitrary"), name=name, args=(h, gain, win), phase=phase)


def _inproj_bwd_act(du, dh, h, gain, win, *, tm, name, phase=None):
    T, D = h.shape
    NS, _, Cs = win.shape

    def body(du_ref, dh_ref, h_ref, g_ref, w_ref, dx_ref, dg_ref, acc_ref):
        i = pl.program_id(0)
        j = pl.program_id(1)

        @pl.when(j == 0)
        def _():
            acc_ref[...] = jnp.zeros_like(acc_ref)

        @pl.when((i == 0) & (j == 0))
        def _():
            dg_ref[...] = jnp.zeros_like(dg_ref)

        acc_ref[...] += _dot_nt(du_ref[...], w_ref[...])

        @pl.when(j == NS - 1)
        def _():
            dx, dg = _rms_bwd(h_ref[...], g_ref[...], acc_ref[...])
            dx_ref[...] = dh_ref[...] + dx
            dg_ref[...] += dg

    tok = pl.BlockSpec((tm, D), lambda i, j: (i, 0))
    vec = pl.BlockSpec((1, D), lambda i, j: (0, 0))
    return _call(
        body, grid=(T // tm, NS),
        in_specs=[pl.BlockSpec((tm, Cs), lambda i, j: (i, j)), tok, tok, vec,
                  pl.BlockSpec((None, D, Cs), lambda i, j: (j, 0, 0))],
        out_specs=[tok, vec],
        out_shape=[jax.ShapeDtypeStruct((T, D), F32), jax.ShapeDtypeStruct((1, D), F32)],
        scratch_shapes=[pltpu.VMEM((tm, D), F32)],
        sem=("arbitrary", "arbitrary"), name=name, args=(du, dh, h, gain, win), phase=phase)


def _inproj_bwd_w(n, du, ns, *, tk, name, phase=None):
    T, D = n.shape
    Cs = du.shape[1] // ns

    def body(n_ref, du_ref, w_ref):
        @pl.when(pl.program_id(1) == 0)
        def _():
            w_ref[...] = jnp.zeros_like(w_ref)

        w_ref[...] += _dot_tn(n_ref[...], du_ref[...])

    return _call(
        body, grid=(ns, T // tk),
        in_specs=[pl.BlockSpec((tk, D), lambda j, k: (k, 0)), pl.BlockSpec((tk, Cs), lambda j, k: (k, j))],
        out_specs=[pl.BlockSpec((None, D, Cs), lambda j, k: (j, 0, 0))],
        out_shape=[jax.ShapeDtypeStruct((ns, D, Cs), F32)],
        sem=("arbitrary", "arbitrary"), name=name, args=(n, du), phase=phase)


STRIDE = 4


def _permute(src_ref, tmp_ref, put):
    S = src_ref.shape[0]
    L4, L16 = S // STRIDE, S // (STRIDE * STRIDE)
    put(0, 0, src_ref[...])
    for r0 in range(STRIDE):
        v = src_ref[pl.ds(r0, L4, stride=STRIDE), :]
        put(1, r0 * L4, v)
        tmp_ref[r0 * L4:(r0 + 1) * L4, :] = v
    for r0 in range(STRIDE):
        for r1 in range(STRIDE):
            put(2, (r1 * STRIDE + r0) * L16, tmp_ref[pl.ds(r0 * L4 + r1, L16, stride=STRIDE), :])


def _permute_out(src_ref, tmp_ref, out_ref, cast):
    for cc in range(src_ref.shape[0]):
        cols = slice(cc * LANES, (cc + 1) * LANES)

        def put(p, row0, v, cols=cols):
            out_ref[p, row0:row0 + v.shape[0], cols] = v.astype(cast)

        _permute(src_ref.at[cc], tmp_ref, put)


def _unpermute_in(get_block, dst_ref, tmp_ref, p, S):
    L4, L16 = S // STRIDE, S // (STRIDE * STRIDE)
    if p == 0:
        dst_ref[...] = get_block(0, S)
        return
    if p == 1:
        for r0 in range(STRIDE):
            dst_ref[pl.ds(r0, L4, stride=STRIDE), :] = get_block(r0 * L4, L4)
        return
    for r0 in range(STRIDE):
        for r1 in range(STRIDE):
            tmp_ref[pl.ds(r0 * L4 + r1, L16, stride=STRIDE), :] = get_block((r1 * STRIDE + r0) * L16, L16)
    for r0 in range(STRIDE):
        dst_ref[pl.ds(r0, L4, stride=STRIDE), :] = tmp_ref[r0 * L4:(r0 + 1) * L4, :]


def _qkv_prep(u, gains, B, S, *, name):
    emat = _seg_matrix(D_ATTN)

    def body(u_ref, g_ref, e_ref, out_ref, scr_ref, tmp_ref):
        c = pl.program_id(1)
        xv = u_ref[...]
        ms = _seg_mean(xv * xv, e_ref, HEAD_DIM)
        r = jnp.where(c < 2, lax.rsqrt(ms + EPS), 1.0)
        yv = xv * r * g_ref[...]
        for cc in range(4):
            scr_ref[cc] = yv[:, cc * LANES:(cc + 1) * LANES]
        _permute_out(scr_ref, tmp_ref, out_ref, BF16)

    return pl.pallas_call(
        body, grid=(B, 3),
        in_specs=[pl.BlockSpec((S, D_ATTN), lambda b, c: (b, c)),
                  pl.BlockSpec((None, 1, D_ATTN), lambda b, c: (c, 0, 0)),
                  pl.BlockSpec((D_ATTN, D_ATTN), lambda b, c: (0, 0))],
        out_specs=pl.BlockSpec((None, N_PATTERNS, None, S, D_ATTN), lambda b, c: (c, 0, b, 0, 0)),
        out_shape=jax.ShapeDtypeStruct((3, N_PATTERNS, B, S, D_ATTN), BF16),
        scratch_shapes=[pltpu.VMEM((4, S, LANES), F32), pltpu.VMEM((S, LANES), F32)],
        compiler_params=_params("arbitrary", "arbitrary"), name=name)(u, gains, emat)


def _band_mask(p, b):
    nblk = jnp.right_shift(16, 2 * p)
    has_prev = jnp.bitwise_and(b, nblk - 1) != 0
    qi = lax.broadcasted_iota(jnp.int32, (QBLK, 2 * QBLK), 0)
    ci = lax.broadcasted_iota(jnp.int32, (QBLK, 2 * QBLK), 1)
    dist = QBLK + qi - ci
    return (dist >= 0) & (dist <= QBLK) & (has_prev | (ci >= QBLK))


def _first_head(rows):
    return lax.broadcasted_iota(jnp.int32, (rows, LANES), 1) < HEAD_DIM


def _split_heads(pair):
    first = _first_head(pair.shape[0])
    zero = jnp.zeros_like(pair)
    return jnp.concatenate([jnp.where(first, pair, zero), jnp.where(first, zero, pair)], axis=0)


def _merge_heads(col_a, col_b):
    rows = col_a.shape[0]
    return jnp.where(_first_head(rows), jnp.broadcast_to(col_a, (rows, LANES)), jnp.broadcast_to(col_b, (rows, LANES)))


QB_FWD = 8
QB_BWD = 4


def _attn_fwd(qkv, *, name, phase=None):
    QB = QB_FWD
    nb = qkv.shape[2]

    def body(q_ref, kp_ref, kc_ref, vp_ref, vc_ref, o_ref, lse_ref):
        kall = jnp.concatenate([kp_ref[...]] + [kc_ref[t] for t in range(QB)], axis=0)
        vall = jnp.concatenate([vp_ref[...]] + [vc_ref[t] for t in range(QB)], axis=0)
        masks = []
        for t in range(QB):
            mask = _band_mask(pl.program_id(0), QB * pl.program_id(1) + t)
            masks.append(jnp.concatenate([mask, mask], axis=0))
        units = [(t, hp) for t in range(QB) for hp in range(HEADS // 2)]
        scores, probs = {}, {}
        for step in range(len(units) + 2):
            if step < len(units):
                t, hp = units[step]
                cols = slice(hp * LANES, (hp + 1) * LANES)
                scores[step] = _dot_nt(_split_heads(q_ref[t, :, cols]), kall[t * QBLK:(t + 2) * QBLK, cols])
            if 1 <= step <= len(units):
                t, hp = units[step - 1]
                cols = slice(hp * LANES, (hp + 1) * LANES)
                s = jnp.where(masks[t], scores.pop(step - 1), NEG)
                m = jnp.max(s, axis=-1, keepdims=True)
                e = jnp.exp(s - m)
                l = jnp.sum(e, axis=-1, keepdims=True)
                probs[step - 1] = (e * (1.0 / l)).astype(BF16)
                lse = m + jnp.log(l)
                lse_ref[t, :, cols] = _merge_heads(lse[:QBLK], lse[QBLK:])
            if 2 <= step:
                t, hp = units[step - 2]
                cols = slice(hp * LANES, (hp + 1) * LANES)
                pr = probs.pop(step - 2)
                o_ref[t, :, cols] = _dot(jnp.concatenate([pr[:QBLK], pr[QBLK:]], axis=1),
                                         _split_heads(vall[t * QBLK:(t + 2) * QBLK, cols]))

    cur = lambda which: pl.BlockSpec((None, None, QB, QBLK, D_ATTN), lambda p, i: (which, p, i, 0, 0))
    prev = lambda which: pl.BlockSpec((None, None, None, QBLK, D_ATTN),
                                      lambda p, i: (which, p, jnp.maximum(QB * i - 1, 0), 0, 0))
    out = pl.BlockSpec((None, QB, QBLK, D_ATTN), lambda p, i: (p, i, 0, 0))
    return _call(
        body, grid=(N_PATTERNS, nb // QB), in_specs=[cur(0), prev(1), cur(1), prev(2), cur(2)], out_specs=[out, out],
        out_shape=[jax.ShapeDtypeStruct((N_PATTERNS, nb, QBLK, D_ATTN), F32)] * 2,
        sem=("arbitrary", "arbitrary"), name=name, args=(qkv, qkv, qkv, qkv, qkv), phase=phase)


def _attn_combine(o3, lse3, B, S, *, name):
    def body(o_ref, l_ref, a_ref, lt_ref, so_ref, sl_ref, tmp_ref):
        for p in range(N_PATTERNS):
            _unpermute_in(lambda r0, n, p=p: o_ref[p, pl.ds(r0, n), :], so_ref.at[p], tmp_ref, p, S)
            _unpermute_in(lambda r0, n, p=p: l_ref[p, pl.ds(r0, n), :], sl_ref.at[p], tmp_ref, p, S)
        l0, l1, l2 = sl_ref[0], sl_ref[1], sl_ref[2]
        m = jnp.maximum(jnp.maximum(l0, l1), l2)
        w0, w1, w2 = jnp.exp(l0 - m), jnp.exp(l1 - m), jnp.exp(l2 - m)
        tot = w0 + w1 + w2
        a_ref[...] = (w0 * so_ref[0] + w1 * so_ref[1] + w2 * so_ref[2]) / tot
        lt_ref[...] = m + jnp.log(tot)

    o3 = o3.reshape(N_PATTERNS, B, S, D_ATTN)
    lse3 = lse3.reshape(N_PATTERNS, B, S, D_ATTN)
    inp = pl.BlockSpec((N_PATTERNS, None, S, LANES), lambda b, c: (0, b, 0, c))
    out = pl.BlockSpec((S, LANES), lambda b, c: (b, c))
    return pl.pallas_call(
        body, grid=(B, D_ATTN // LANES), in_specs=[inp, inp], out_specs=[out, out],
        out_shape=[jax.ShapeDtypeStruct((B * S, D_ATTN), F32)] * 2,
        scratch_shapes=[pltpu.VMEM((N_PATTERNS, S, LANES), F32)] * 2 + [pltpu.VMEM((S, LANES), F32)],
        compiler_params=_params("arbitrary", "arbitrary"), name=name)(o3, lse3)


STAT_D = 8


def _attn_bwd_prep(dattn, attn, lse, B, S, *, name):
    emat = _seg_matrix(LANES)
    ncc = D_ATTN // LANES

    def body(da_ref, a_ref, l_ref, e_ref, do_ref, st_ref, scr_ref, nat_ref, tmp_ref):
        cc = pl.program_id(1)
        da = da_ref[...]
        dsum = _seg_mean(da * a_ref[...], e_ref, 1.0)
        scr_ref[...] = da

        def put_do(p, row0, v):
            do_ref[p, row0:row0 + v.shape[0], :] = v.astype(BF16)

        _permute(scr_ref, tmp_ref, put_do)

        lane = lax.broadcasted_iota(jnp.int32, (S, LANES), 1)
        h0 = 2 * cc
        vals = ((h0, l_ref[:, 0:1]), (h0 + 1, l_ref[:, HEAD_DIM:HEAD_DIM + 1]),
                (STAT_D + h0, dsum[:, 0:1]), (STAT_D + h0 + 1, dsum[:, HEAD_DIM:HEAD_DIM + 1]))
        tile = jnp.where(cc == 0, 0.0, nat_ref[...])
        for at, col in vals:
            tile = jnp.where(lane == at, col, tile)
        nat_ref[...] = tile

        @pl.when(cc == ncc - 1)
        def _():
            def put_st(p, row0, v):
                st_ref[p, row0:row0 + v.shape[0], :] = v

            _permute(nat_ref, tmp_ref, put_st)

    inp = pl.BlockSpec((S, LANES), lambda b, c: (b, c))
    return pl.pallas_call(
        body, grid=(B, ncc),
        in_specs=[inp, inp, inp, pl.BlockSpec((LANES, LANES), lambda b, c: (0, 0))],
        out_specs=[pl.BlockSpec((N_PATTERNS, None, S, LANES), lambda b, c: (0, b, 0, c)),
                   pl.BlockSpec((N_PATTERNS, None, S, LANES), lambda b, c: (0, b, 0, 0))],
        out_shape=[jax.ShapeDtypeStruct((N_PATTERNS, B, S, D_ATTN), BF16),
                   jax.ShapeDtypeStruct((N_PATTERNS, B, S, LANES), F32)],
        scratch_shapes=[pltpu.VMEM((S, LANES), F32)] * 3,
        compiler_params=_params("arbitrary", "arbitrary"), name=name)(dattn, attn, lse, emat)


def _attn_bwd(qkv, do3, st3, *, name, phase=None):
    QB = QB_BWD
    nb = qkv.shape[2]
    ngroups = nb // QB

    def body(q_ref, kp_ref, kc_ref, vp_ref, vc_ref, do_ref, st_ref, out_ref, carry_ref):
        p = pl.program_id(0)
        i = pl.program_id(1)

        @pl.when((p == 0) & (i == 0))
        def _():
            carry_ref[...] = jnp.zeros_like(carry_ref)

        kall = jnp.concatenate([kp_ref[...]] + [kc_ref[t] for t in range(QB)], axis=0)
        vall = jnp.concatenate([vp_ref[...]] + [vc_ref[t] for t in range(QB)], axis=0)

        masks = []
        for t in range(QB):
            mask = _band_mask(p, QB * i + t) & (i < ngroups)
            masks.append(jnp.concatenate([mask, mask], axis=0))

        def operands(t, hp):
            cols = slice(hp * LANES, (hp + 1) * LANES)
            kh, vh = kall[t * QBLK:(t + 2) * QBLK, cols], vall[t * QBLK:(t + 2) * QBLK, cols]
            return kh, vh, _split_heads(q_ref[t, :, cols]), _split_heads(do_ref[t, :, cols])

        def stage_scores(t, hp):
            kh, vh, q2, do2 = operands(t, hp)
            return _dot_nt(q2, kh), _dot_nt(do2, vh)

        def stage_softmax(t, hp, s, dp):
            h0, h1 = 2 * hp, 2 * hp + 1
            lse = jnp.concatenate([st_ref[t, :, h0:h0 + 1], st_ref[t, :, h1:h1 + 1]], axis=0)
            dsum = jnp.concatenate([st_ref[t, :, STAT_D + h0:STAT_D + h0 + 1],
                                    st_ref[t, :, STAT_D + h1:STAT_D + h1 + 1]], axis=0)
            pr = jnp.where(masks[t], jnp.exp(s - lse), 0.0)
            return (pr * (dp - dsum)).astype(BF16), pr.astype(BF16)

        def stage_grads(t, hp, ds, prb):
            cols = slice(hp * LANES, (hp + 1) * LANES)
            kh, vh, q2, do2 = operands(t, hp)
            dq = _dot(jnp.concatenate([ds[:QBLK], ds[QBLK:]], axis=1), _split_heads(kh))
            dk, dv = _dot_tn(ds, q2), _dot_tn(prb, do2)
            if t == 0:
                for c in range(3):
                    for tt in range(QB):
                        v = carry_ref[c, tt, :, cols]
                        if tt == QB - 1 and c > 0:
                            v = v + (dk if c == 1 else dv)[:QBLK]
                        out_ref[c, tt, :, cols] = v.astype(BF16)
            else:
                carry_ref[1, t - 1, :, cols] += dk[:QBLK]
                carry_ref[2, t - 1, :, cols] += dv[:QBLK]
            carry_ref[0, t, :, cols] = dq
            carry_ref[1, t, :, cols] = dk[QBLK:]
            carry_ref[2, t, :, cols] = dv[QBLK:]

        units = [(t, hp) for hp in range(HEADS // 2) for t in range(QB)]
        scores, probs = {}, {}
        for step in range(len(units) + 2):
            if step < len(units):
                scores[step] = stage_scores(*units[step])
            if 1 <= step <= len(units):
                probs[step - 1] = stage_softmax(*units[step - 1], *scores.pop(step - 1))
            if 2 <= step:
                stage_grads(*units[step - 2], *probs.pop(step - 2))

    group = lambda i: jnp.minimum(i, ngroups - 1)
    cur = lambda which: pl.BlockSpec((None, None, QB, QBLK, D_ATTN), lambda p, i: (which, p, group(i), 0, 0))
    prev = lambda which: pl.BlockSpec((None, None, None, QBLK, D_ATTN),
                                      lambda p, i: (which, p, jnp.maximum(QB * group(i) - 1, 0), 0, 0))
    aux = lambda lanes: pl.BlockSpec((None, QB, QBLK, lanes), lambda p, i: (p, group(i), 0, 0))
    return _call(
        body, grid=(N_PATTERNS, ngroups + 1),
        in_specs=[cur(0), prev(1), cur(1), prev(2), cur(2), aux(D_ATTN), aux(LANES)],
        out_specs=[pl.BlockSpec((3, None, QB, QBLK, D_ATTN), lambda p, i: (0, p, jnp.maximum(i - 1, 0), 0, 0))],
        out_shape=[jax.ShapeDtypeStruct((3, N_PATTERNS, nb, QBLK, D_ATTN), BF16)],
        scratch_shapes=[pltpu.VMEM((3, QB, QBLK, D_ATTN), F32)],
        sem=("arbitrary", "arbitrary"), name=name, args=(qkv, qkv, qkv, qkv, qkv, do3, st3), phase=phase)


def _attn_grad_combine(cur, u, gains, B, S, *, name, phase=None):
    emat = _seg_matrix(LANES)

    def body(cur_ref, u_ref, g_ref, e_ref, du_ref, dg_ref, scr_ref, tmp_ref):
        c = pl.program_id(0)
        b = pl.program_id(2)
        for p in range(N_PATTERNS):
            _unpermute_in(lambda r0, n, p=p: cur_ref[p, pl.ds(r0, n), :].astype(F32), scr_ref.at[p], tmp_ref, p, S)
        dy = scr_ref[0] + scr_ref[1] + scr_ref[2]
        xv = u_ref[...]
        gain = g_ref[...]
        ms = _seg_mean(xv * xv, e_ref, HEAD_DIM)
        r = lax.rsqrt(ms + EPS)
        xhat = xv * r
        dxh = dy * gain
        dx = r * (dxh - xhat * _seg_mean(dxh * xhat, e_ref, HEAD_DIM))
        du_ref[...] = jnp.where(c < 2, dx, dy).astype(BF16)

        @pl.when((b == 0))
        def _():
            dg_ref[...] = jnp.zeros_like(dg_ref)

        dg_ref[...] += jnp.sum(dy * xhat, axis=0, keepdims=True)

    cur = cur.reshape(3, N_PATTERNS, B, S, D_ATTN)
    ncc = D_ATTN // LANES
    return _call(
        body, grid=(3, ncc, B),
        in_specs=[pl.BlockSpec((None, N_PATTERNS, None, S, LANES), lambda c, cc, b: (c, 0, b, 0, cc)),
                  pl.BlockSpec((S, LANES), lambda c, cc, b: (b, c * ncc + cc)),
                  pl.BlockSpec((None, 1, LANES), lambda c, cc, b: (c, 0, cc)),
                  pl.BlockSpec((LANES, LANES), lambda c, cc, b: (0, 0))],
        out_specs=[pl.BlockSpec((S, LANES), lambda c, cc, b: (b, c * ncc + cc)),
                   pl.BlockSpec((None, 1, LANES), lambda c, cc, b: (c, 0, cc))],
        out_shape=[jax.ShapeDtypeStruct((B * S, 3 * D_ATTN), BF16), jax.ShapeDtypeStruct((3, 1, D_ATTN), F32)],
        scratch_shapes=[pltpu.VMEM((N_PATTERNS, S, LANES), F32), pltpu.VMEM((S, LANES), F32)],
        sem=("arbitrary", "arbitrary", "arbitrary"), name=name, args=(cur, u, gains, emat), phase=phase)


HALO = 32
SUB = 64
SUBLANES = 8


def _shifted_copies(src_ref, sh_ref, tc):
    sh_ref[0] = src_ref[...]
    for r in range(1, SUBLANES):
        sh_ref[r, 0:tc + HALO - SUBLANES, :] = src_ref[pl.ds(r, tc + HALO - SUBLANES), :]


def _shifted(sh_ref, start, size):
    return sh_ref[start % SUBLANES, pl.ds(start - start % SUBLANES, size), :]


def _conv_fwd(u, cw, cb, lg, lb, B, S, *, tc, name, phase=None):
    nchunk = S // tc
    hb = tc // HALO

    def body(ca_ref, cap_ref, cg_ref, cgp_ref, w_ref, cb_ref, lg_ref, lb_ref, cv_ref, glu_ref, y_ref, pad_ref, sh_ref):
        i = pl.program_id(1)
        glu = ca_ref[...] * _sigmoid(cg_ref[...])
        glu_ref[...] = glu
        prev = cap_ref[...] * _sigmoid(cgp_ref[...])
        pad_ref[0:HALO, :] = jnp.where(i > 0, prev, 0.0)
        pad_ref[HALO:, :] = glu
        _shifted_copies(pad_ref, sh_ref, tc)
        for sub in range(tc // SUB):
            acc = jnp.zeros((SUB, D_CONV), F32) + cb_ref[...]
            for k in range(CONV_K):
                acc = acc + _shifted(sh_ref, sub * SUB + HALO - (CONV_K - 1) + k, SUB) * w_ref[pl.ds(k, 1), :]
            y_ref[sub * SUB:(sub + 1) * SUB, :] = acc
        y = y_ref[...]
        mu = jnp.mean(y, axis=-1, keepdims=True)
        yc = y - mu
        var = jnp.mean(yc * yc, axis=-1, keepdims=True)
        z = yc * lax.rsqrt(var + EPS) * lg_ref[...] + lb_ref[...]
        cv_ref[...] = (z * _sigmoid(z)).astype(BF16)

    def cur(col):
        return pl.BlockSpec((tc, D_CONV), lambda b, i: (b * nchunk + i, col))

    def halo(col):
        return pl.BlockSpec((HALO, D_CONV), lambda b, i: (jnp.maximum((b * nchunk + i) * hb - 1, 0), col))

    vec = pl.BlockSpec((1, D_CONV), lambda b, i: (0, 0))
    out = pl.BlockSpec((tc, D_CONV), lambda b, i: (b * nchunk + i, 0))
    return _call(
        body, grid=(B, nchunk),
        in_specs=[cur(3), halo(3), cur(4), halo(4), pl.BlockSpec((CONV_K, D_CONV), lambda b, i: (0, 0)), vec, vec, vec],
        out_specs=[out, out, out],
        out_shape=[jax.ShapeDtypeStruct((B * S, D_CONV), BF16), jax.ShapeDtypeStruct((B * S, D_CONV), F32),
                   jax.ShapeDtypeStruct((B * S, D_CONV), F32)],
        scratch_shapes=[pltpu.VMEM((tc + HALO, D_CONV), F32), pltpu.VMEM((SUBLANES, tc + HALO, D_CONV), F32)],
        sem=("arbitrary", "arbitrary"), name=name, args=(u, u, u, u, cw, cb, lg, lb), phase=phase)


def _conv_bwd_norm(dcv, y, lg, lb, *, tc, name):
    T = y.shape[0]

    def body(dcv_ref, y_ref, lg_ref, lb_ref, dy_ref, part_ref):
        yv = y_ref[...]
        mu = jnp.mean(yv, axis=-1, keepdims=True)
        yc = yv - mu
        var = jnp.mean(yc * yc, axis=-1, keepdims=True)
        rstd = lax.rsqrt(var + EPS)
        xhat = yc * rstd
        z = xhat * lg_ref[...] + lb_ref[...]
        sig = _sigmoid(z)
        dz = dcv_ref[...] * (sig * (1.0 + z * (1.0 - sig)))
        dxh = dz * lg_ref[...]
        dy = rstd * (dxh - jnp.mean(dxh, axis=-1, keepdims=True)
                     - xhat * jnp.mean(dxh * xhat, axis=-1, keepdims=True))
        dy_ref[...] = dy

        @pl.when(pl.program_id(0) == 0)
        def _():
            part_ref[...] = jnp.zeros_like(part_ref)

        part_ref[0:1, :] += jnp.sum(dz * xhat, axis=0, keepdims=True)
        part_ref[1:2, :] += jnp.sum(dz, axis=0, keepdims=True)
        part_ref[2:3, :] += jnp.sum(dy, axis=0, keepdims=True)

    tok = pl.BlockSpec((tc, D_CONV), lambda i: (i, 0))
    vec = pl.BlockSpec((1, D_CONV), lambda i: (0, 0))
    return pl.pallas_call(
        body, grid=(T // tc,), in_specs=[tok, tok, vec, vec],
        out_specs=[tok, pl.BlockSpec((8, D_CONV), lambda i: (0, 0))],
        out_shape=[jax.ShapeDtypeStruct((T, D_CONV), F32), jax.ShapeDtypeStruct((8, D_CONV), F32)],
        compiler_params=_params("arbitrary"), name=name)(dcv, y, lg, lb)


def _conv_bwd_taps(dy, glu, u, cw, B, S, *, tc, name, phase=None):
    nchunk = S // tc
    hb = tc // HALO
    last_hb = B * S // HALO - 1

    def body(dy_ref, dyn_ref, glu_ref, glup_ref, ca_ref, cg_ref, w_ref, dca_ref, dcg_ref, dw_ref,
             dyp_ref, glp_ref, acc_ref, shd_ref, shg_ref):
        b = pl.program_id(0)
        i = pl.program_id(1)
        dy = dy_ref[...]
        dyp_ref[0:tc, :] = dy
        dyp_ref[tc:, :] = jnp.where(i < nchunk - 1, dyn_ref[...], 0.0)
        glp_ref[0:HALO, :] = jnp.where(i > 0, glup_ref[...], 0.0)
        glp_ref[HALO:, :] = glu_ref[...]
        _shifted_copies(dyp_ref, shd_ref, tc)
        _shifted_copies(glp_ref, shg_ref, tc)

        @pl.when((b == 0) & (i == 0))
        def _():
            dw_ref[...] = jnp.zeros_like(dw_ref)

        for sub in range(tc // SUB):
            acc = jnp.zeros((SUB, D_CONV), F32)
            for k in range(CONV_K):
                acc = acc + _shifted(shd_ref, sub * SUB + (CONV_K - 1) - k, SUB) * w_ref[pl.ds(k, 1), :]
            acc_ref[sub * SUB:(sub + 1) * SUB, :] = acc
        for k in range(CONV_K):
            dw_ref[k:k + 1, :] += jnp.sum(dy * _shifted(shg_ref, HALO - (CONV_K - 1) + k, tc), axis=0, keepdims=True)
        dglu = acc_ref[...]
        ca = ca_ref[...]
        sig = _sigmoid(cg_ref[...])
        dca_ref[...] = (dglu * sig).astype(BF16)
        dcg_ref[...] = (dglu * ca * sig * (1.0 - sig)).astype(BF16)

    tok = pl.BlockSpec((tc, D_CONV), lambda b, i: (b * nchunk + i, 0))
    nxt = pl.BlockSpec((HALO, D_CONV), lambda b, i: (jnp.minimum((b * nchunk + i + 1) * hb, last_hb), 0))
    prv = pl.BlockSpec((HALO, D_CONV), lambda b, i: (jnp.maximum((b * nchunk + i) * hb - 1, 0), 0))
    return _call(
        body, grid=(B, nchunk),
        in_specs=[tok, nxt, tok, prv,
                  pl.BlockSpec((tc, D_CONV), lambda b, i: (b * nchunk + i, 3)),
                  pl.BlockSpec((tc, D_CONV), lambda b, i: (b * nchunk + i, 4)),
                  pl.BlockSpec((CONV_K, D_CONV), lambda b, i: (0, 0))],
        out_specs=[tok, tok, pl.BlockSpec((32, D_CONV), lambda b, i: (0, 0))],
        out_shape=[jax.ShapeDtypeStruct((B * S, D_CONV), BF16), jax.ShapeDtypeStruct((B * S, D_CONV), BF16),
                   jax.ShapeDtypeStruct((32, D_CONV), F32)],
        scratch_shapes=[pltpu.VMEM((tc + HALO, D_CONV), F32), pltpu.VMEM((tc + HALO, D_CONV), F32),
                        pltpu.VMEM((tc, D_CONV), F32), pltpu.VMEM((SUBLANES, tc + HALO, D_CONV), F32),
                        pltpu.VMEM((SUBLANES, tc + HALO, D_CONV), F32)],
        sem=("arbitrary", "arbitrary"), name=name, args=(dy, dy, glu, glu, u, u, cw), phase=phase)


def _outproj_fwd(h, attn, cv, wout, *, tm, name):
    T, D = h.shape

    def body(h_ref, a_ref, c_ref, w_ref, o_ref):
        o_ref[...] = (h_ref[...] + _dot(a_ref[...].astype(BF16), w_ref[0:D_ATTN, :])
                      + _dot(c_ref[...], w_ref[D_ATTN:, :]))

    tok = pl.BlockSpec((tm, D), lambda i: (i, 0))
    half = pl.BlockSpec((tm, D_ATTN), lambda i: (i, 0))
    return pl.pallas_call(
        body, grid=(T // tm,), in_specs=[tok, half, half, pl.BlockSpec(wout.shape, lambda i: (0, 0))],
        out_specs=tok, out_shape=jax.ShapeDtypeStruct((T, D), F32),
        compiler_params=_params("arbitrary"), name=name)(h, attn, cv, wout)


def _outproj_bwd(dh, attn, cv, wout, *, tm, name):
    T, D = dh.shape

    def body(dh_ref, a_ref, c_ref, w_ref, da_ref, dc_ref, dw_ref):
        @pl.when(pl.program_id(0) == 0)
        def _():
            dw_ref[...] = jnp.zeros_like(dw_ref)

        dhb = dh_ref[...].astype(BF16)
        da_ref[...] = _dot_nt(dhb, w_ref[0:D_ATTN, :])
        dc_ref[...] = _dot_nt(dhb, w_ref[D_ATTN:, :])
        dw_ref[0:D_ATTN, :] += _dot_tn(a_ref[...].astype(BF16), dhb)
        dw_ref[D_ATTN:, :] += _dot_tn(c_ref[...], dhb)

    tok = pl.BlockSpec((tm, D), lambda i: (i, 0))
    half = pl.BlockSpec((tm, D_ATTN), lambda i: (i, 0))
    wspec = pl.BlockSpec(wout.shape, lambda i: (0, 0))
    return pl.pallas_call(
        body, grid=(T // tm,), in_specs=[tok, half, half, wspec], out_specs=[half, half, wspec],
        out_shape=[jax.ShapeDtypeStruct((T, D_ATTN), F32), jax.ShapeDtypeStruct((T, D_ATTN), F32),
                   jax.ShapeDtypeStruct(wout.shape, F32)],
        compiler_params=_params("arbitrary"), name=name)(dh, attn, cv, wout)


ADAM_BLOCK_BYTES = 3 * 512 * 1024


def _adamw(w, g, m, v, *, name):
    R, C = w.shape
    tr = R
    for cand in (512, 352, 256, 176, 128, 64, 32, 16, 8):
        if R % cand == 0 and cand * C * 4 <= ADAM_BLOCK_BYTES:
            tr = cand
            break
    c1 = 1.0 - ADAM_B1 ** ADAM_STEP
    c2 = 1.0 - ADAM_B2 ** ADAM_STEP

    def body(w_ref, g_ref, m_ref, v_ref, d_ref, nm_ref, nv_ref):
        gv = g_ref[...]
        nm = ADAM_B1 * m_ref[...] + (1.0 - ADAM_B1) * gv
        nv = ADAM_B2 * v_ref[...] + (1.0 - ADAM_B2) * (gv * gv)
        d_ref[...] = -ADAM_LR * ((nm / c1) / (jnp.sqrt(nv / c2) + ADAM_EPS) + ADAM_WD * w_ref[...])
        nm_ref[...] = nm
        nv_ref[...] = nv

    blk = pl.BlockSpec((tr, C), lambda i: (i, 0))
    return pl.pallas_call(
        body, grid=(R // tr,), in_specs=[blk] * 4, out_specs=[blk] * 3,
        out_shape=[jax.ShapeDtypeStruct((R, C), F32)] * 3,
        compiler_params=_params("arbitrary"), name=name)(w, g, m, v)


ADAM_SPLIT = 4


def _adamw_many(ws, gs, ms, vs, *, name, phase=None):
    n = len(ws)
    c1 = 1.0 - ADAM_B1 ** ADAM_STEP
    c2 = 1.0 - ADAM_B2 ** ADAM_STEP

    def body(*refs):
        ins, outs = refs[:4 * n], refs[4 * n:]
        for a in range(n):
            w_ref, g_ref, m_ref, v_ref = ins[4 * a:4 * a + 4]
            gv = g_ref[...]
            nm = ADAM_B1 * m_ref[...] + (1.0 - ADAM_B1) * gv
            nv = ADAM_B2 * v_ref[...] + (1.0 - ADAM_B2) * (gv * gv)
            outs[3 * a][...] = -ADAM_LR * ((nm / c1) / (jnp.sqrt(nv / c2) + ADAM_EPS) + ADAM_WD * w_ref[...])
            outs[3 * a + 1][...] = nm
            outs[3 * a + 2][...] = nv

    in_specs, out_specs, out_shape, args = [], [], [], []
    for w, g, m, v in zip(ws, gs, ms, vs):
        R, C = w.shape
        blk = pl.BlockSpec((R // ADAM_SPLIT, C), lambda i: (i, 0))
        in_specs += [blk] * 4
        out_specs += [blk] * 3
        out_shape += [jax.ShapeDtypeStruct((R, C), F32)] * 3
        args += [w, g, m, v]
    res = _call(body, grid=(ADAM_SPLIT,), in_specs=in_specs, out_specs=out_specs, out_shape=out_shape,
                sem=("arbitrary",), name=name, args=args, phase=phase)
    outs, extra = res if phase is not None else (res, None)
    return list(outs[0::3]), list(outs[1::3]), list(outs[2::3]), extra


TM = 512
TM_WIDE = 1024
TK = 1024
TC = 512


def _local_step(x, tgt, w, overlap=None):
    B, S, D = x.shape
    T = B * S
    x2 = x.reshape(T, D)
    t2 = tgt.reshape(T, D)
    ones = jnp.ones((1, D_ATTN), F32)
    scale = HEAD_DIM ** -0.5
    gains = jnp.stack([jnp.tile(w["q_norm"], (1, HEADS)) * scale, jnp.tile(w["k_norm"], (1, HEADS)), ones])
    g = {}

    def hosting(point, build):
        phase = overlap.phase(point, w, g) if overlap is not None else None
        if phase is None:
            return build(None)
        outs, extra = build(phase)
        overlap.done(point, extra, w, g)
        return outs

    h1, n1, G1, U1 = hosting("ffn1_fwd", lambda ph: _ffn_fwd(
        x2, w["ffn1_norm"], w["wg1"], w["wu1"], w["wd1"], None, tm=TM_WIDE, name="ffn1_fwd", phase=ph))
    u, n2 = hosting("inproj_fwd", lambda ph: _inproj_fwd(h1, w["mix_norm"], w["win"], tm=TM_WIDE, name="inproj_fwd", phase=ph))
    qkv = _qkv_prep(u, gains, B, S, name="qkv_prep")
    qkv = qkv.reshape(3, N_PATTERNS, T // QBLK, QBLK, D_ATTN)
    o3, lse3 = hosting("attn_fwd", lambda ph: _attn_fwd(qkv, name="attn_fwd", phase=ph))
    attn, lse = _attn_combine(o3, lse3, B, S, name="attn_combine")
    cv, glu, yconv = hosting("conv_fwd", lambda ph: _conv_fwd(
        u, w["conv_w"], w["conv_b"], w["conv_ln_g"], w["conv_ln_b"], B, S, tc=TC, name="conv_fwd", phase=ph))
    h2 = _outproj_fwd(h1, attn, cv, w["wout"], tm=TM_WIDE, name="outproj_fwd")
    dh3, n3, G2, U2, loss = _ffn_fwd(h2, w["ffn2_norm"], w["wg2"], w["wu2"], w["wd2"], t2, tm=TM_WIDE, name="ffn2_fwd")

    dG, dU, A, dy, dh2, g["ffn2_norm"] = _ffn_bwd_act(dh3, h2, w["ffn2_norm"], G2, U2, w["wg2"], w["wu2"], w["wd2"],
                                                    tm=TM, name="ffn2_bwd_act")
    g["wg2"], g["wu2"], g["wd2"] = _ffn_bwd_w(n3, dy, dG, dU, A, tk=2 * TK, name="ffn2_bwd_w")
    dattn, dcv, g["wout"] = _outproj_bwd(dh2, attn, cv, w["wout"], tm=TM_WIDE, name="outproj_bwd")
    dyc, cpart = _conv_bwd_norm(dcv, yconv, w["conv_ln_g"], w["conv_ln_b"], tc=TC, name="conv_bwd_norm")
    dca, dcg, dcw = hosting("conv_bwd_taps", lambda ph: _conv_bwd_taps(
        dyc, glu, u, w["conv_w"], B, S, tc=TC, name="conv_bwd_taps", phase=ph))
    do3, st3 = _attn_bwd_prep(dattn, attn, lse, B, S, name="attn_bwd_prep")
    nb = T // QBLK
    (cur,) = hosting("attn_bwd", lambda ph: _attn_bwd(
        qkv, do3.reshape(N_PATTERNS, nb, QBLK, D_ATTN), st3.reshape(N_PATTERNS, nb, QBLK, LANES),
        name="attn_bwd", phase=ph))
    du_qkv, dgains = hosting("attn_grad_combine", lambda ph: _attn_grad_combine(
        cur, u, gains, B, S, name="attn_grad_combine", phase=ph))
    du = jnp.concatenate([du_qkv, dca, dcg], axis=1)
    (g["win"],) = hosting("inproj_bwd_w", lambda ph: _inproj_bwd_w(
        n2, du, w["win"].shape[0], tk=2 * TK, name="inproj_bwd_w", phase=ph))
    dh1, g["mix_norm"] = hosting("inproj_bwd_act", lambda ph: _inproj_bwd_act(
        du, dh2, h1, w["mix_norm"], w["win"], tm=TM_WIDE, name="inproj_bwd_act", phase=ph))
    dG, dU, A, dy, dx, g["ffn1_norm"] = hosting("ffn1_bwd_act", lambda ph: _ffn_bwd_act(
        dh1, x2, w["ffn1_norm"], G1, U1, w["wg1"], w["wu1"], w["wd1"], tm=TM, name="ffn1_bwd_act", phase=ph))
    g["wg1"], g["wu1"], g["wd1"] = hosting("ffn1_bwd_w", lambda ph: _ffn_bwd_w(
        n1, dy, dG, dU, A, tk=2 * TK, name="ffn1_bwd_w", phase=ph))

    g["q_norm"] = dgains[0].reshape(HEADS, HEAD_DIM).sum(axis=0, keepdims=True) * scale
    g["k_norm"] = dgains[1].reshape(HEADS, HEAD_DIM).sum(axis=0, keepdims=True)
    g["conv_ln_g"] = cpart[0:1]
    g["conv_ln_b"] = cpart[1:2]
    g["conv_b"] = cpart[2:3]
    g["conv_w"] = dcw[:CONV_K]
    return loss, dx.reshape(B, S, D), g


N_CHIPS = 4
N_DEV = 8
VMEM_SPEC = pl.BlockSpec(memory_space=pltpu.VMEM)


def _remote(src, dst, send_sem, recv_sem, device):
    return pltpu.make_async_remote_copy(src_ref=src, dst_ref=dst, send_sem=send_sem, recv_sem=recv_sem,
                                        device_id=device, device_id_type=MESH)


def _stage_shards(shards, dtypes, *, name):
    n = len(shards)
    halves = [s.reshape(2, s.shape[0] // 2, s.shape[1]) for s in shards]

    def body(*refs):
        ins, outs, vms, loc_sems = refs[:n], refs[n:2 * n], refs[2 * n:3 * n], refs[3 * n]
        me = 2 * lax.axis_index("x") + lax.axis_index("y")
        copies = []
        for a in range(n):
            vms[a][...] = ins[a][...].astype(dtypes[a])
            cp = pltpu.make_async_copy(vms[a], outs[a].at[me], loc_sems.at[a])
            cp.start()
            copies.append(cp)
        for cp in copies:
            cp.wait()

    return pl.pallas_call(
        body, in_specs=[VMEM_SPEC] * n, out_specs=[ANY] * n,
        out_shape=[jax.ShapeDtypeStruct((N_CHIPS,) + h.shape, dt) for h, dt in zip(halves, dtypes)],
        scratch_shapes=[pltpu.VMEM(h.shape, dt) for h, dt in zip(halves, dtypes)] + [DMA_SEMS((n,))],
        compiler_params=pltpu.CompilerParams(vmem_limit_bytes=VMEM_LIMIT), name=name)(*halves)


def _like(arrays):
    return [jax.ShapeDtypeStruct(a.shape, a.dtype) for a in arrays]


def _axes():
    x, y, c = lax.axis_index("x"), lax.axis_index("y"), lax.axis_index("c")
    first = (x + (1 - c) * (1 - 2 * x), y + c * (1 - 2 * y))
    second = (x + c * (1 - 2 * x), y + (1 - c) * (1 - 2 * y))
    slots = tuple(2 * px + py for px, py in ((x, y), first, second, (1 - x, 1 - y)))
    return (x, y, c), (*first, c), (*second, c), slots


def _gather_ici_phase(bufs, only=None):
    n = len(bufs)

    def stage1(ins, outs, sems):
        (x, y, c), peer1, peer2, (own, s1, s2, both) = _axes()
        starts, arrivals = [], []
        for a in range(n):
            mine, land = outs[a].at[own, c], outs[a].at[s2, c]
            starts.append(_remote(mine, mine, *sems(a), peer2))
            arrivals.append(_remote(land, land, *sems(a), peer2))
        return starts, arrivals

    def stage2(ins, outs, sems):
        (x, y, c), peer1, peer2, (own, s1, s2, both) = _axes()
        starts, arrivals = [], []
        for a in range(n):
            for k, (src, dst) in enumerate(((own, s1), (s2, both))):
                mine, land = outs[a].at[src, c], outs[a].at[dst, c]
                starts.append(_remote(mine, mine, *sems(2 * a + k), peer1))
                arrivals.append(_remote(land, land, *sems(2 * a + k), peer1))
        return starts, arrivals

    same = {a: a for a in range(n)}
    first, second = _Phase(bufs, _like(bufs), same, n, stage1), _Phase(bufs, _like(bufs), same, 2 * n, stage2)
    if only is None:
        return first.then(second)
    return first if only == 1 else second


def _gather_d2d_phase(bufs):
    n = len(bufs)

    def copies(ins, outs, sems):
        (x, y, c), peer1, peer2, (own, s1, s2, both) = _axes()
        starts, arrivals = [], []
        for a in range(n):
            for j, s in enumerate((s1, s2, both)):
                got, land = outs[a].at[s, c], outs[a].at[s, 1 - c]
                starts.append(_remote(got, got, *sems(3 * a + j), (x, y, 1 - c)))
                arrivals.append(_remote(land, land, *sems(3 * a + j), (x, y, 1 - c)))
        return starts, arrivals

    return _Phase(bufs, _like(bufs), {a: a for a in range(n)}, 3 * n, copies)


def _gather_pipelined_phase(bufs):
    n = len(bufs)

    def piece(kind, a):
        def copies(ins, outs, sems):
            (x, y, c), peer1, peer2, (own, s1, s2, both) = _axes()
            moves = {1: [(own, s2, c, c, peer2)],
                     2: [(own, s1, c, c, peer1), (s2, both, c, c, peer1)],
                     3: [(s, s, c, 1 - c, (x, y, 1 - c)) for s in (s1, s2, both)]}[kind]
            starts, arrivals = [], []
            for k, (src, dst, h_src, h_dst, peer) in enumerate(moves):
                mine, land = outs[a].at[src, h_src], outs[a].at[dst, h_dst]
                starts.append(_remote(mine, mine, *sems(k), peer))
                arrivals.append(_remote(land, land, *sems(k), peer))
            return starts, arrivals

        return kind, copies

    def beside(pieces):
        def copies(ins, outs, sems):
            starts, arrivals, base = [], [], 0
            for count, fn in pieces:
                s, r = fn(ins, outs, lambda i, base=base: sems(base + i))
                starts, arrivals, base = starts + s, arrivals + r, base + count
            return starts, arrivals

        return sum(count for count, _ in pieces), copies

    phase = _Phase(bufs, _like(bufs), {a: a for a in range(n)}, 0, None)
    phase.stages = [beside([piece(kind, t - kind + 1) for kind in (1, 2, 3) if 0 <= t - kind + 1 < n])
                    for t in range(n + 2)]
    return phase


def _exchange_phase(views):
    n = len(views)

    def copies(ins, outs, sems):
        x, y, c = lax.axis_index("x"), lax.axis_index("y"), lax.axis_index("c")
        starts = [_remote(ins[a].at[pl.ds(0, ins[a].shape[0]), 1 - c], outs[a], *sems(a), (x, y, 1 - c))
                  for a in range(n)]
        return starts, starts

    outs = [jax.ShapeDtypeStruct((v.shape[0],) + v.shape[2:], F32) for v in views]
    return _Phase(views, outs, {}, n, copies)


ADD_SPLIT = 2


def _add_halves(views, got, sel, tag):
    n = len(views)

    def body(s_ref, *refs):
        ins, outs = refs[:4 * n], refs[4 * n:]
        for a in range(n):
            gk, rk, gs, rs = ins[4 * a:4 * a + 4]
            outs[2 * a][...] = gk[...] + rk[...]
            outs[2 * a + 1][...] = (gs[...] + rs[...]).astype(BF16)

    in_specs, out_specs, out_shape, args = [], [], [], []
    for g, r in zip(views, got):
        _, _, rh, cdim = g.shape
        tr = rh // ADD_SPLIT
        for off in (0, 2):
            in_specs.append(pl.BlockSpec((None, None, tr, cdim), lambda k, i, s, off=off: (s[1 + off + k], s[0], i, 0)))
            in_specs.append(pl.BlockSpec((None, tr, cdim), lambda k, i, s, off=off: (s[1 + off + k], i, 0)))
            args += [g, r]
        out_specs += [pl.BlockSpec((None, tr, cdim), lambda k, i, s: (k, i, 0))] * 2
        out_shape += [jax.ShapeDtypeStruct((2, rh, cdim), F32), jax.ShapeDtypeStruct((2, rh, cdim), BF16)]
    res = pl.pallas_call(
        body,
        grid_spec=pltpu.PrefetchScalarGridSpec(num_scalar_prefetch=1, grid=(2, ADD_SPLIT), in_specs=in_specs,
                                               out_specs=out_specs),
        out_shape=out_shape, compiler_params=_params("arbitrary", "arbitrary"), name=f"rs_add_half_{tag}")(sel, *args)
    return list(res[0::2]), list(res[1::2])


def _swap_phase(arrays, stage):
    n = len(arrays)

    def copies(ins, outs, sems):
        peer = _axes()[stage]
        starts = [_remote(ins[a], outs[a], *sems(a), peer) for a in range(n)]
        return starts, starts

    return _Phase(arrays, _like(arrays), {}, n, copies)


def _add_first(keep, got, tag):
    n = len(keep)

    def body(*refs):
        ins, outs = refs[:2 * n], refs[2 * n:]
        for a in range(n):
            k_ref, g_ref = ins[2 * a], ins[2 * a + 1]
            outs[2 * a][...] = k_ref[0] + g_ref[0].astype(F32)
            outs[2 * a + 1][...] = (k_ref[1] + g_ref[1].astype(F32)).astype(BF16)

    in_specs, out_specs, out_shape, args = [], [], [], []
    for k, g in zip(keep, got):
        _, rh, cdim = k.shape
        tr = rh // ADD_SPLIT
        in_specs += [pl.BlockSpec((2, tr, cdim), lambda i: (0, i, 0))] * 2
        out_specs += [pl.BlockSpec((tr, cdim), lambda i: (i, 0))] * 2
        out_shape += [jax.ShapeDtypeStruct((rh, cdim), F32), jax.ShapeDtypeStruct((rh, cdim), BF16)]
        args += [k, g]
    res = pl.pallas_call(body, grid=(ADD_SPLIT,), in_specs=in_specs, out_specs=out_specs, out_shape=out_shape,
                         compiler_params=_params("arbitrary"), name=f"rs_add_first_{tag}")(*args)
    return list(res[0::2]), list(res[1::2])


def _add_second(keep, got, sel, tag):
    n = len(keep)

    def body(s_ref, *refs):
        ins, outs = refs[:2 * n], refs[2 * n:]
        for a in range(n):
            outs[a][...] = ins[2 * a][...] + ins[2 * a + 1][...].astype(F32)

    in_specs, out_specs, out_shape, args = [], [], [], []
    for k, g in zip(keep, got):
        rh, cdim = k.shape
        tr = rh // ADD_SPLIT
        in_specs += [pl.BlockSpec((tr, cdim), lambda i, s: (i, 0))] * 2
        out_specs.append(pl.BlockSpec((None, tr, cdim), lambda i, s: (s[0], i, 0)))
        out_shape.append(jax.ShapeDtypeStruct((2, rh, cdim), F32))
        args += [k, g]
    res = pl.pallas_call(
        body,
        grid_spec=pltpu.PrefetchScalarGridSpec(num_scalar_prefetch=1, grid=(ADD_SPLIT,), in_specs=in_specs,
                                               out_specs=out_specs),
        out_shape=out_shape, compiler_params=_params("arbitrary"), name=f"rs_add_second_{tag}")(sel, *args)
    return list(res)


def _join_phase(halves):
    n = len(halves)

    def copies(ins, outs, sems):
        x, y, c = lax.axis_index("x"), lax.axis_index("y"), lax.axis_index("c")
        starts, arrivals = [], []
        for a in range(n):
            mine, land = outs[a].at[c], outs[a].at[1 - c]
            starts.append(_remote(mine, mine, *sems(a), (x, y, 1 - c)))
            arrivals.append(_remote(land, land, *sems(a), (x, y, 1 - c)))
        return starts, arrivals

    return _Phase(halves, _like(halves), {a: a for a in range(n)}, n, copies)


def _slot_order():
    x, y, c = lax.axis_index("x"), lax.axis_index("y"), lax.axis_index("c")
    own, flip_x, flip_y, both = 2 * x + y, 2 * (1 - x) + y, 2 * x + 1 - y, 2 * (1 - x) + 1 - y
    first = jnp.where(c == 0, flip_x, flip_y)
    second = jnp.where(c == 0, flip_y, flip_x)
    return jnp.stack([c, own, second, first, both]).astype(jnp.int32)


def _half_view(g):
    return g.reshape(N_CHIPS, 2, g.shape[1] // 2, g.shape[2])


EARLY_GRADS = ("wg2", "wu2", "wd2", "wout")
MIDDLE_GRADS = ("win",)


class _Overlap:
    EARLY_AT = ("conv_bwd_taps", "attn_bwd", "attn_grad_combine", "inproj_bwd_w")
    MIDDLE_AT = ("inproj_bwd_act", "ffn1_bwd_act", "ffn1_bwd_w", None)

    def __init__(self, staged):
        self.staged = staged
        self.ffn2 = list(staged[3:])
        self.sel = sel = _slot_order()
        self.early = _Reduction(EARLY_GRADS, "early", sel)
        self.middle = _Reduction(MIDDLE_GRADS, "middle", sel)

    def finish_late(self, late):
        views = [_half_view(a) for a in late]
        got = _run_phase(_exchange_phase(views), name="rs_exchange_halves")
        keep, send = _add_halves(views, got, self.sel, "late")
        got = _run_phase(_swap_phase(send, 1), name="rs_swap_first_axis")
        keep, send = _add_first(keep, got, "late")
        got = _run_phase(_swap_phase(send, 2), name="rs_swap_second_axis")
        halves = _add_second(keep, got, self.sel, "late")
        full = _run_phase(_join_phase(halves + list(self.middle.halves)), name="rs_join_halves")
        return [f.reshape(-1, f.shape[-1]) for f in full]

    def phase(self, point, w, g):
        if point == "ffn1_fwd":
            return _gather_ici_phase(self.staged[:3])
        if point == "inproj_fwd":
            return _gather_ici_phase(self.ffn2, only=1)
        if point == "attn_fwd":
            return _gather_ici_phase(self.ffn2, only=2)
        if point == "conv_fwd":
            return _gather_d2d_phase(self.ffn2)
        for red, at in ((self.early, self.EARLY_AT), (self.middle, self.MIDDLE_AT)):
            if point in at:
                return red.phase(at.index(point), g)
        return None

    def done(self, point, outs, w, g):
        if point == "ffn1_fwd":
            win, wout, taps = [_whole(b) for b in _run_phase(_gather_d2d_phase(outs), name="gather_mix_d2d")]
            w["win"] = win
            w["wout"] = wout.reshape(-1, wout.shape[-1])
            w["conv_w"] = taps.transpose(1, 0, 2).reshape(CONV_K + 1, D_CONV)[:CONV_K]
        elif point in ("inproj_fwd", "attn_fwd"):
            self.ffn2 = list(outs)
        elif point == "conv_fwd":
            w["wg2"], w["wu2"], w["wd2"] = [_whole(b) for b in outs]
        for red, at in ((self.early, self.EARLY_AT), (self.middle, self.MIDDLE_AT)):
            if point in at:
                red.done(at.index(point), outs)


class _Reduction:
    def __init__(self, names, tag, sel):
        self.names, self.tag, self.sel = names, tag, sel
        self.reduced = {}

    def phase(self, stage, g):
        if stage == 0:
            self.cols = [g[k].shape[-1] for k in self.names]
            self.views = [_half_view(g[k].reshape(N_CHIPS, -1, g[k].shape[-1])) for k in self.names]
            return _exchange_phase(self.views)
        if stage in (1, 2):
            return _swap_phase(self.send, stage)
        return _join_phase(self.halves)

    def done(self, stage, outs):
        if stage == 0:
            self.keep, self.send = _add_halves(self.views, outs, self.sel, self.tag)
        elif stage == 1:
            self.keep, self.send = _add_first(self.keep, outs, self.tag)
        elif stage == 2:
            self.halves = _add_second(self.keep, outs, self.sel, self.tag)
        else:
            for k, c, f in zip(self.names, self.cols, outs):
                self.reduced[k] = f.reshape(-1, c)


def _whole(buf):
    return buf.reshape(buf.shape[0], 2 * buf.shape[2], buf.shape[3])


def _allreduce_small(pack, *, name):
    rows = pack.shape[0]

    def body(p_ref, o_ref, buf_ref, send_sems, recv_sems):
        x, y, c = lax.axis_index("x"), lax.axis_index("y"), lax.axis_index("c")
        me = 4 * x + 2 * y + c
        buf_ref[me] = p_ref[...]
        cps = []
        for k in range(1, N_DEV):
            peer = tuple(1 - v if (k >> s) & 1 else v for v, s in ((x, 2), (y, 1), (c, 0)))
            cp = _remote(p_ref, buf_ref.at[me], send_sems.at[k - 1], recv_sems.at[k - 1], peer)
            cp.start()
            cps.append(cp)
        for k in range(1, N_DEV):
            src = 4 * (x ^ ((k >> 2) & 1)) + 2 * (y ^ ((k >> 1) & 1)) + (c ^ (k & 1))
            land = buf_ref.at[src]
            _remote(land, land, send_sems.at[k - 1], recv_sems.at[k - 1], (x, y, c)).wait_recv()
        acc = buf_ref[0]
        for d in range(1, N_DEV):
            acc = acc + buf_ref[d]
        o_ref[...] = acc
        for cp in cps:
            cp.wait_send()

    return pl.pallas_call(
        body, in_specs=[VMEM_SPEC], out_specs=VMEM_SPEC, out_shape=jax.ShapeDtypeStruct(pack.shape, F32),
        scratch_shapes=[pltpu.VMEM((N_DEV, rows, LANES), F32), pltpu.SemaphoreType.DMA((N_DEV - 1,)),
                        pltpu.SemaphoreType.DMA((N_DEV - 1,))], name=name)(pack)


SMALL = ("ffn1_norm", "mix_norm", "q_norm", "k_norm", "conv_b", "conv_ln_g", "conv_ln_b", "ffn2_norm", "conv_w")
BIG = ("ffn1_w_gate", "ffn1_w_up", "ffn1_w_down", "w_in", "w_out", "ffn2_w_gate", "ffn2_w_up", "ffn2_w_down")
TRANSPOSED = ("ffn1_w_gate", "ffn1_w_up", "ffn2_w_gate", "ffn2_w_up")
WEIGHTS = ("ffn1_norm", "ffn1_w_gate", "ffn1_w_up", "ffn1_w_down", "mix_norm", "w_in", "q_norm", "k_norm",
           "conv_w", "conv_b", "conv_ln_g", "conv_ln_b", "w_out", "ffn2_norm", "ffn2_w_gate", "ffn2_w_up",
           "ffn2_w_down")


def _pack(parts):
    rows = []
    for p in parts:
        flat = p.reshape(-1)
        tile = SUBLANES * LANES
        padded = -(-flat.shape[0] // tile) * tile
        rows.append(jnp.pad(flat, (0, padded - flat.shape[0])).reshape(-1, LANES))
    return jnp.concatenate(rows, axis=0)


def _unpack(pack, shapes):
    out, row = [], 0
    for shp in shapes:
        size = shp[0] * shp[1]
        tile = SUBLANES * LANES
        nrows = -(-size // tile) * SUBLANES
        out.append(pack[row:row + nrows].reshape(-1)[:size].reshape(shp))
        row += nrows
    return out


def kernel(x, ffn1_norm, ffn1_w_gate, ffn1_w_up, ffn1_w_down, mix_norm, w_in, q_norm, k_norm, conv_w, conv_b, conv_ln_g, conv_ln_b, w_out, ffn2_norm, ffn2_w_gate, ffn2_w_up, ffn2_w_down, loss_target, m_ffn1_norm, m_ffn1_w_gate, m_ffn1_w_up, m_ffn1_w_down, m_mix_norm, m_w_in, m_q_norm, m_k_norm, m_conv_w, m_conv_b, m_conv_ln_g, m_conv_ln_b, m_w_out, m_ffn2_norm, m_ffn2_w_gate, m_ffn2_w_up, m_ffn2_w_down, v_ffn1_norm, v_ffn1_w_gate, v_ffn1_w_up, v_ffn1_w_down, v_mix_norm, v_w_in, v_q_norm, v_k_norm, v_conv_w, v_conv_b, v_conv_ln_g, v_conv_ln_b, v_w_out, v_ffn2_norm, v_ffn2_w_gate, v_ffn2_w_up, v_ffn2_w_down):
    wts = dict(ffn1_norm=ffn1_norm, ffn1_w_gate=ffn1_w_gate[0], ffn1_w_up=ffn1_w_up[0], ffn1_w_down=ffn1_w_down[0],
               mix_norm=mix_norm, w_in=w_in[0], q_norm=q_norm, k_norm=k_norm, conv_w=conv_w[0], conv_b=conv_b,
               conv_ln_g=conv_ln_g, conv_ln_b=conv_ln_b, w_out=w_out[0], ffn2_norm=ffn2_norm,
               ffn2_w_gate=ffn2_w_gate[0], ffn2_w_up=ffn2_w_up[0], ffn2_w_down=ffn2_w_down[0])
    mom = dict(ffn1_norm=m_ffn1_norm, ffn1_w_gate=m_ffn1_w_gate[0], ffn1_w_up=m_ffn1_w_up[0], ffn1_w_down=m_ffn1_w_down[0],
               mix_norm=m_mix_norm, w_in=m_w_in[0], q_norm=m_q_norm, k_norm=m_k_norm, conv_w=m_conv_w[0], conv_b=m_conv_b,
               conv_ln_g=m_conv_ln_g, conv_ln_b=m_conv_ln_b, w_out=m_w_out[0], ffn2_norm=m_ffn2_norm,
               ffn2_w_gate=m_ffn2_w_gate[0], ffn2_w_up=m_ffn2_w_up[0], ffn2_w_down=m_ffn2_w_down[0])
    var = dict(ffn1_norm=v_ffn1_norm, ffn1_w_gate=v_ffn1_w_gate[0], ffn1_w_up=v_ffn1_w_up[0], ffn1_w_down=v_ffn1_w_down[0],
               mix_norm=v_mix_norm, w_in=v_w_in[0], q_norm=v_q_norm, k_norm=v_k_norm, conv_w=v_conv_w[0], conv_b=v_conv_b,
               conv_ln_g=v_conv_ln_g, conv_ln_b=v_conv_ln_b, w_out=v_w_out[0], ffn2_norm=v_ffn2_norm,
               ffn2_w_gate=v_ffn2_w_gate[0], ffn2_w_up=v_ffn2_w_up[0], ffn2_w_down=v_ffn2_w_down[0])
    chip = 2 * lax.axis_index("x") + lax.axis_index("y")
    for src in (wts, mom, var):
        for n in TRANSPOSED:
            src[n] = src[n].T

    taps = jnp.pad(wts["conv_w"], ((0, 1), (0, 0)))
    staged = _stage_shards([wts["ffn1_w_gate"], wts["ffn1_w_up"], wts["ffn1_w_down"], wts["w_in"], wts["w_out"], taps,
                            wts["ffn2_w_gate"], wts["ffn2_w_up"], wts["ffn2_w_down"]],
                           [BF16, BF16, BF16, BF16, BF16, F32, BF16, BF16, BF16], name="stage_shards")
    first = _run_phase(_gather_pipelined_phase(staged[:3]), name="gather_ffn1")
    wg1, wu1, wd1 = [_whole(b) for b in first]
    w = dict(ffn1_norm=ffn1_norm, mix_norm=mix_norm, ffn2_norm=ffn2_norm, q_norm=q_norm, k_norm=k_norm,
             conv_b=conv_b, conv_ln_g=conv_ln_g, conv_ln_b=conv_ln_b, wg1=wg1, wu1=wu1, wd1=wd1)
    overlap = _Overlap(staged[3:])
    loss_part, grad_x, g = _local_step(x, loss_target, w, overlap)

    grads, delta, new_m, new_v = {}, {}, {}, {}
    early = overlap.early.reduced
    grads.update(ffn2_w_gate=early["wg2"], ffn2_w_up=early["wu2"], ffn2_w_down=early["wd2"], w_out=early["wout"])
    grads.update(zip(("ffn1_w_gate", "ffn1_w_up", "ffn1_w_down", "w_in"),
                     overlap.finish_late([g["wg1"], g["wu1"], g["wd1"]])))

    small_shapes = [g[n].shape for n in SMALL] + [(SUBLANES, LANES)]
    red = _allreduce_small(_pack([g[n] for n in SMALL] + [loss_part]), name="allreduce_small")
    small = dict(zip(SMALL + ("loss",), _unpack(red, small_shapes)))
    loss = small["loss"][0, 0]
    small["conv_w"] = lax.dynamic_slice_in_dim(small["conv_w"], chip * LANES, LANES, axis=1)

    for tag, names in (("early", ("ffn2_w_gate", "ffn2_w_up", "ffn2_w_down", "w_out")),
                       ("late", ("ffn1_w_gate", "ffn1_w_up", "ffn1_w_down", "w_in"))):
        d, m, v, _ = _adamw_many([wts[n] for n in names], [grads[n] for n in names], [mom[n] for n in names],
                                 [var[n] for n in names], name=f"adamw_{tag}")
        for dst, vals in ((delta, d), (new_m, m), (new_v, v)):
            dst.update(zip(names, vals))
    shapes = [wts[n].shape for n in SMALL]
    packs = [_pack([src[n] for n in SMALL]) for src in (wts, small, mom, var)]
    outs = _adamw(*packs, name="adamw_small")
    for dst, pk in zip((delta, new_m, new_v), outs):
        dst.update(zip(SMALL, _unpack(pk, shapes)))
    for n in SMALL:
        grads[n] = small[n]

    def shaped(d, n):
        v = d[n].T if n in TRANSPOSED else d[n]
        return v.reshape((1,) + v.shape) if n in BIG or n == "conv_w" else v

    return (loss, grad_x, *[shaped(grads, n) for n in WEIGHTS], *[shaped(delta, n) for n in WEIGHTS],
            *[shaped(new_m, n) for n in WEIGHTS], *[shaped(new_v, n) for n in WEIGHTS])
```

```python
import functools

import jax
import jax.numpy as jnp
from jax import lax
from jax.experimental import pallas as pl
from jax.experimental.pallas import tpu as pltpu

F32 = jnp.float32
BF16 = jnp.bfloat16

EPS = 1e-6
HEADS = 8
HEAD_DIM = 64
D_ATTN = HEADS * HEAD_DIM
D_CONV = 512
CONV_K = 31
QBLK = 128
N_PATTERNS = 3
DILATIONS = (1, 4, 16)
LANES = 128
NEG = -1e30

ADAM_LR = 0.001
ADAM_B1 = 0.9
ADAM_B2 = 0.999
ADAM_EPS = 1e-08
ADAM_WD = 0.01
ADAM_STEP = 10

VMEM_LIMIT = 56 * 1024 * 1024
MESH = pl.DeviceIdType.MESH

NT_DIMS = (((1,), (1,)), ((), ()))
TN_DIMS = (((0,), (0,)), ((), ()))


def _params(*sem):
    return pltpu.CompilerParams(dimension_semantics=sem, vmem_limit_bytes=VMEM_LIMIT)


def _dot(a, b):
    return jnp.dot(a, b, preferred_element_type=F32)


def _dot_nt(a, b):
    return lax.dot_general(a, b, NT_DIMS, preferred_element_type=F32)


def _dot_tn(a, b):
    return lax.dot_general(a, b, TN_DIMS, preferred_element_type=F32)


def _sigmoid(x):
    return 1.0 / (1.0 + jnp.exp(-x))


def _seg_mean(v, e_ref, width):
    hi = v.astype(BF16)
    lo = (v - hi.astype(F32)).astype(BF16)
    e = e_ref[...]
    return (_dot(hi, e) + _dot(lo, e)) * (1.0 / width)


def _seg_matrix(n):
    i = jnp.arange(n)
    return (i[:, None] // HEAD_DIM == i[None, :] // HEAD_DIM).astype(BF16)


ANY = pl.BlockSpec(memory_space=pl.ANY)
DMA_SEMS = pltpu.SemaphoreType.DMA


class _Phase:
    def __init__(self, ins, outs, aliases, nsem, copies):
        self.ins, self.outs, self.aliases = list(ins), list(outs), dict(aliases)
        self.stages = [(nsem, copies)]

    def then(self, other):
        self.stages = self.stages + other.stages
        return self

    @property
    def nsem(self):
        return sum(n for n, _ in self.stages)

    def _copies(self, k, in_refs, out_refs, send_sems, recv_sems):
        base = sum(n for n, _ in self.stages[:k])
        return self.stages[k][1](in_refs, out_refs, lambda i: (send_sems.at[base + i], recv_sems.at[base + i]))

    def start(self, k, *refs):
        for cp in self._copies(k, *refs)[0]:
            cp.start()

    def finish(self, k, *refs):
        starts, arrivals = self._copies(k, *refs)
        for cp in arrivals:
            cp.wait_recv()
        for cp in starts:
            cp.wait_send()


def _run_phase(phase, *, name):
    n_in, n_out = len(phase.ins), len(phase.outs)

    def body(*refs):
        ins, outs = refs[:n_in], refs[n_in:n_in + n_out]
        send_sems, recv_sems = refs[n_in + n_out:]
        for k in range(len(phase.stages)):
            phase.start(k, ins, outs, send_sems, recv_sems)
            phase.finish(k, ins, outs, send_sems, recv_sems)

    return pl.pallas_call(
        body, in_specs=[ANY] * n_in, out_specs=[ANY] * n_out, out_shape=phase.outs,
        input_output_aliases=phase.aliases,
        scratch_shapes=[DMA_SEMS((phase.nsem,)), DMA_SEMS((phase.nsem,))], name=name)(*phase.ins)


def _call(body, *, grid, in_specs, out_specs, out_shape, scratch_shapes=(), sem, name, args, phase=None):
    in_specs, out_specs, out_shape = list(in_specs), list(out_specs), list(out_shape)
    scratch_shapes = list(scratch_shapes)
    if phase is None:
        return pl.pallas_call(body, grid=grid, in_specs=in_specs, out_specs=out_specs, out_shape=out_shape,
                              scratch_shapes=scratch_shapes, compiler_params=_params(*sem), name=name)(*args)
    n_in, n_out, n_scr = len(in_specs), len(out_specs), len(scratch_shapes)
    p_in, p_out = len(phase.ins), len(phase.outs)

    def hosted(*refs):
        ins, pins = refs[:n_in], refs[n_in:n_in + p_in]
        o0 = n_in + p_in
        outs, pouts = refs[o0:o0 + n_out], refs[o0 + n_out:o0 + n_out + p_out]
        s0 = o0 + n_out + p_out
        scr = refs[s0:s0 + n_scr]
        send_sems, recv_sems = refs[s0 + n_scr:]
        step = 0
        for d, n in enumerate(grid):
            step = step * n + pl.program_id(d)
        nsteps = functools.reduce(lambda a, b: a * b, grid)
        nstages = len(phase.stages)
        comm_refs = (pins, pouts, send_sems, recv_sems)

        for k in range(nstages):
            @pl.when(step == (k * nsteps) // nstages)
            def _(k=k):
                if k > 0:
                    phase.finish(k - 1, *comm_refs)
                phase.start(k, *comm_refs)

        body(*ins, *outs, *scr)

        @pl.when(step == nsteps - 1)
        def _():
            phase.finish(nstages - 1, *comm_refs)

    res = pl.pallas_call(
        hosted, grid=grid, in_specs=in_specs + [ANY] * p_in, out_specs=out_specs + [ANY] * p_out,
        out_shape=out_shape + phase.outs,
        input_output_aliases={n_in + i: n_out + o for i, o in phase.aliases.items()},
        scratch_shapes=scratch_shapes + [DMA_SEMS((phase.nsem,)), DMA_SEMS((phase.nsem,))],
        compiler_params=_params(*sem), name=name)(*args, *phase.ins)
    return res[:n_out], res[n_out:]


ROW_CHUNK = 256


def _ffn_fwd(x, gain, wg, wu, wd, tgt, *, tm, name, phase=None):
    T, D = x.shape
    NS, Fs, _ = wg.shape
    with_loss = tgt is not None

    def body(*refs):
        if with_loss:
            x_ref, g_ref, wg_ref, wu_ref, wd_ref, t_ref, h_ref, n_ref, G_ref, U_ref, loss_ref, acc_ref = refs
        else:
            x_ref, g_ref, wg_ref, wu_ref, wd_ref, h_ref, n_ref, G_ref, U_ref, acc_ref = refs
        i = pl.program_id(0)
        j = pl.program_id(1)

        @pl.when(j == 0)
        def _():
            xv = x_ref[...]
            r = lax.rsqrt(jnp.mean(xv * xv, axis=-1, keepdims=True) + EPS)
            n_ref[...] = (xv * r * g_ref[...]).astype(BF16)
            acc_ref[...] = jnp.zeros_like(acc_ref)

        n = n_ref[...]
        G = _dot_nt(n, wg_ref[...])
        U = _dot_nt(n, wu_ref[...])
        G_ref[...] = G.astype(BF16)
        U_ref[...] = U.astype(BF16)
        A = (G * _sigmoid(G) * U).astype(BF16)
        acc_ref[...] += _dot(A, wd_ref[...])

        @pl.when(j == NS - 1)
        def _():
            h = x_ref[...] + 0.5 * acc_ref[...]
            if with_loss:
                e = h - t_ref[...]
                h_ref[...] = e * (1.0 / D)

                @pl.when(i == 0)
                def _():
                    loss_ref[...] = jnp.zeros_like(loss_ref)

                loss_ref[...] += jnp.sum(e * e) * (0.5 / D)
            else:
                h_ref[...] = h

    tok = pl.BlockSpec((tm, D), lambda i, j: (i, 0))
    in_specs = [tok, pl.BlockSpec((1, D), lambda i, j: (0, 0)),
                pl.BlockSpec((None, Fs, D), lambda i, j: (j, 0, 0)),
                pl.BlockSpec((None, Fs, D), lambda i, j: (j, 0, 0)),
                pl.BlockSpec((None, Fs, D), lambda i, j: (j, 0, 0))]
    args = [x, gain, wg, wu, wd]
    act = pl.BlockSpec((None, tm, Fs), lambda i, j: (j, i, 0))
    out_shape = [jax.ShapeDtypeStruct((T, D), F32), jax.ShapeDtypeStruct((T, D), BF16),
                 jax.ShapeDtypeStruct((NS, T, Fs), BF16), jax.ShapeDtypeStruct((NS, T, Fs), BF16)]
    out_specs = [tok, tok, act, act]
    if with_loss:
        in_specs.append(tok)
        args.append(tgt)
        out_shape.append(jax.ShapeDtypeStruct((8, LANES), F32))
        out_specs.append(pl.BlockSpec((8, LANES), lambda i, j: (0, 0)))
    return _call(body, grid=(T // tm, NS), in_specs=in_specs, out_specs=out_specs, out_shape=out_shape,
                 scratch_shapes=[pltpu.VMEM((tm, D), F32)], sem=("arbitrary", "arbitrary"), name=name,
                 args=args, phase=phase)


def _rms_bwd(xv, gain, dn):
    r = lax.rsqrt(jnp.mean(xv * xv, axis=-1, keepdims=True) + EPS)
    xhat = xv * r
    dxh = dn * gain
    dx = r * (dxh - xhat * jnp.mean(dxh * xhat, axis=-1, keepdims=True))
    dg = jnp.sum(dn * xhat, axis=0, keepdims=True)
    return dx, dg


def _ffn_bwd_act(dh, x, gain, G, U, wg, wu, wd, *, tm, name, phase=None):
    T, D = x.shape
    NS, Fs, _ = wg.shape

    def body(dh_ref, x_ref, g_ref, G_ref, U_ref, wg_ref, wu_ref, wd_ref,
             dG_ref, dU_ref, A_ref, dy_ref, dx_ref, dg_ref, acc_ref):
        i = pl.program_id(0)
        j = pl.program_id(1)

        @pl.when(j == 0)
        def _():
            dy_ref[...] = (0.5 * dh_ref[...]).astype(BF16)
            acc_ref[...] = jnp.zeros_like(acc_ref)

        @pl.when((i == 0) & (j == 0))
        def _():
            dg_ref[...] = jnp.zeros_like(dg_ref)

        nchunks = tm // ROW_CHUNK
        dA, dGU = {}, {}
        for step in range(nchunks + 2):
            if step < nchunks:
                rows = slice(step * ROW_CHUNK, (step + 1) * ROW_CHUNK)
                dA[step] = _dot_nt(dy_ref[rows, :], wd_ref[...])
            if 1 <= step <= nchunks:
                k = step - 1
                rows = slice(k * ROW_CHUNK, (k + 1) * ROW_CHUNK)
                Gv = G_ref[rows, :].astype(F32)
                Uv = U_ref[rows, :].astype(F32)
                sig = _sigmoid(Gv)
                s = Gv * sig
                dG = (dA[k] * Uv * (sig * (1.0 + Gv * (1.0 - sig)))).astype(BF16)
                dU = (dA.pop(k) * s).astype(BF16)
                dG_ref[rows, :] = dG
                dU_ref[rows, :] = dU
                A_ref[rows, :] = (s * Uv).astype(BF16)
                dGU[k] = (dG, dU)
            if 2 <= step:
                k = step - 2
                rows = slice(k * ROW_CHUNK, (k + 1) * ROW_CHUNK)
                dG, dU = dGU.pop(k)
                acc_ref[rows, :] += _dot(dG, wg_ref[...]) + _dot(dU, wu_ref[...])

        @pl.when(j == NS - 1)
        def _():
            dx, dg = _rms_bwd(x_ref[...], g_ref[...], acc_ref[...])
            dx_ref[...] = dh_ref[...] + dx
            dg_ref[...] += dg

    tok = pl.BlockSpec((tm, D), lambda i, j: (i, 0))
    act = pl.BlockSpec((None, tm, Fs), lambda i, j: (j, i, 0))
    vec = pl.BlockSpec((1, D), lambda i, j: (0, 0))
    return _call(
        body, grid=(T // tm, NS),
        in_specs=[tok, tok, vec, act, act,
                  pl.BlockSpec((None, Fs, D), lambda i, j: (j, 0, 0)),
                  pl.BlockSpec((None, Fs, D), lambda i, j: (j, 0, 0)),
                  pl.BlockSpec((None, Fs, D), lambda i, j: (j, 0, 0))],
        out_specs=[act, act, act, tok, tok, vec],
        out_shape=[jax.ShapeDtypeStruct((NS, T, Fs), BF16)] * 3
        + [jax.ShapeDtypeStruct((T, D), BF16), jax.ShapeDtypeStruct((T, D), F32),
           jax.ShapeDtypeStruct((1, D), F32)],
        scratch_shapes=[pltpu.VMEM((tm, D), F32)],
        sem=("arbitrary", "arbitrary"), name=name, args=(dh, x, gain, G, U, wg, wu, wd), phase=phase)


def _ffn_bwd_w(n, dy, dG, dU, A, *, tk, name, phase=None):
    T, D = n.shape
    NS, _, Fs = dG.shape

    def body(n_ref, dy_ref, dG_ref, dU_ref, A_ref, wg_ref, wu_ref, wd_ref):
        @pl.when(pl.program_id(1) == 0)
        def _():
            wg_ref[...] = jnp.zeros_like(wg_ref)
            wu_ref[...] = jnp.zeros_like(wu_ref)
            wd_ref[...] = jnp.zeros_like(wd_ref)

        nv = n_ref[...]
        wg_ref[...] += _dot_tn(dG_ref[...], nv)
        wu_ref[...] += _dot_tn(dU_ref[...], nv)
        wd_ref[...] += _dot_tn(A_ref[...], dy_ref[...])

    tok = pl.BlockSpec((tk, D), lambda j, k: (k, 0))
    act = pl.BlockSpec((None, tk, Fs), lambda j, k: (j, k, 0))
    return _call(
        body, grid=(NS, T // tk), in_specs=[tok, tok, act, act, act],
        out_specs=[pl.BlockSpec((None, Fs, D), lambda j, k: (j, 0, 0))] * 3,
        out_shape=[jax.ShapeDtypeStruct((NS, Fs, D), F32)] * 3,
        sem=("arbitrary", "arbitrary"), name=name, args=(n, dy, dG, dU, A), phase=phase)


def _inproj_fwd(h, gain, win, *, tm, name, phase=None):
    T, D = h.shape
    NS, _, Cs = win.shape

    def body(h_ref, g_ref, w_ref, u_ref, n_ref):
        @pl.when(pl.program_id(1) == 0)
        def _():
            xv = h_ref[...]
            r = lax.rsqrt(jnp.mean(xv * xv, axis=-1, keepdims=True) + EPS)
            n_ref[...] = (xv * r * g_ref[...]).astype(BF16)

        u_ref[...] = _dot(n_ref[...], w_ref[...])

    tok = pl.BlockSpec((tm, D), lambda i, j: (i, 0))
    return _call(
        body, grid=(T // tm, NS),
        in_specs=[tok, pl.BlockSpec((1, D), lambda i, j: (0, 0)),
                  pl.BlockSpec((None, D, Cs), lambda i, j: (j, 0, 0))],
        out_specs=[pl.BlockSpec((tm, Cs), lambda i, j: (i, j)), tok],
        out_shape=[jax.ShapeDtypeStruct((T, NS * Cs), F32), jax.ShapeDtypeStruct((T, D), BF16)],
        sem=("arbitrary", "arbitrary"), name=name, args=(h, gain, win), phase=phase)


def _inproj_bwd_act(du, dh, h, gain, win, *, tm, name, phase=None):
    T, D = h.shape
    NS, _, Cs = win.shape

    def body(du_ref, dh_ref, h_ref, g_ref, w_ref, dx_ref, dg_ref, acc_ref):
        i = pl.program_id(0)
        j = pl.program_id(1)

        @pl.when(j == 0)
        def _():
            acc_ref[...] = jnp.zeros_like(acc_ref)

        @pl.when((i == 0) & (j == 0))
        def _():
            dg_ref[...] = jnp.zeros_like(dg_ref)

        acc_ref[...] += _dot_nt(du_ref[...], w_ref[...])

        @pl.when(j == NS - 1)
        def _():
            dx, dg = _rms_bwd(h_ref[...], g_ref[...], acc_ref[...])
            dx_ref[...] = dh_ref[...] + dx
            dg_ref[...] += dg

    tok = pl.BlockSpec((tm, D), lambda i, j: (i, 0))
    vec = pl.BlockSpec((1, D), lambda i, j: (0, 0))
    return _call(
        body, grid=(T // tm, NS),
        in_specs=[pl.BlockSpec((tm, Cs), lambda i, j: (i, j)), tok, tok, vec,
                  pl.BlockSpec((None, D, Cs), lambda i, j: (j, 0, 0))],
        out_specs=[tok, vec],
        out_shape=[jax.ShapeDtypeStruct((T, D), F32), jax.ShapeDtypeStruct((1, D), F32)],
        scratch_shapes=[pltpu.VMEM((tm, D), F32)],
        sem=("arbitrary", "arbitrary"), name=name, args=(du, dh, h, gain, win), phase=phase)


def _inproj_bwd_w(n, du, ns, *, tk, name, phase=None):
    T, D = n.shape
    Cs = du.shape[1] // ns

    def body(n_ref, du_ref, w_ref):
        @pl.when(pl.program_id(1) == 0)
        def _():
            w_ref[...] = jnp.zeros_like(w_ref)

        w_ref[...] += _dot_tn(n_ref[...], du_ref[...])

    return _call(
        body, grid=(ns, T // tk),
        in_specs=[pl.BlockSpec((tk, D), lambda j, k: (k, 0)), pl.BlockSpec((tk, Cs), lambda j, k: (k, j))],
        out_specs=[pl.BlockSpec((None, D, Cs), lambda j, k: (j, 0, 0))],
        out_shape=[jax.ShapeDtypeStruct((ns, D, Cs), F32)],
        sem=("arbitrary", "arbitrary"), name=name, args=(n, du), phase=phase)


STRIDE = 4


def _permute(src_ref, tmp_ref, put):
    S = src_ref.shape[0]
    L4, L16 = S // STRIDE, S // (STRIDE * STRIDE)
    put(0, 0, src_ref[...])
    for r0 in range(STRIDE):
        v = src_ref[pl.ds(r0, L4, stride=STRIDE), :]
        put(1, r0 * L4, v)
        tmp_ref[r0 * L4:(r0 + 1) * L4, :] = v
    for r0 in range(STRIDE):
        for r1 in range(STRIDE):
            put(2, (r1 * STRIDE + r0) * L16, tmp_ref[pl.ds(r0 * L4 + r1, L16, stride=STRIDE), :])


def _permute_out(src_ref, tmp_ref, out_ref, cast):
    for cc in range(src_ref.shape[0]):
        cols = slice(cc * LANES, (cc + 1) * LANES)

        def put(p, row0, v, cols=cols):
            out_ref[p, row0:row0 + v.shape[0], cols] = v.astype(cast)

        _permute(src_ref.at[cc], tmp_ref, put)


def _unpermute_in(get_block, dst_ref, tmp_ref, p, S):
    L4, L16 = S // STRIDE, S // (STRIDE * STRIDE)
    if p == 0:
        dst_ref[...] = get_block(0, S)
        return
    if p == 1:
        for r0 in range(STRIDE):
            dst_ref[pl.ds(r0, L4, stride=STRIDE), :] = get_block(r0 * L4, L4)
        return
    for r0 in range(STRIDE):
        for r1 in range(STRIDE):
            tmp_ref[pl.ds(r0 * L4 + r1, L16, stride=STRIDE), :] = get_block((r1 * STRIDE + r0) * L16, L16)
    for r0 in range(STRIDE):
        dst_ref[pl.ds(r0, L4, stride=STRIDE), :] = tmp_ref[r0 * L4:(r0 + 1) * L4, :]


def _qkv_prep(u, gains, B, S, *, name):
    emat = _seg_matrix(D_ATTN)

    def body(u_ref, g_ref, e_ref, out_ref, scr_ref, tmp_ref):
        c = pl.program_id(1)
        xv = u_ref[...]
        ms = _seg_mean(xv * xv, e_ref, HEAD_DIM)
        r = jnp.where(c < 2, lax.rsqrt(ms + EPS), 1.0)
        yv = xv * r * g_ref[...]
        for cc in range(4):
            scr_ref[cc] = yv[:, cc * LANES:(cc + 1) * LANES]
        _permute_out(scr_ref, tmp_ref, out_ref, BF16)

    return pl.pallas_call(
        body, grid=(B, 3),
        in_specs=[pl.BlockSpec((S, D_ATTN), lambda b, c: (b, c)),
                  pl.BlockSpec((None, 1, D_ATTN), lambda b, c: (c, 0, 0)),
                  pl.BlockSpec((D_ATTN, D_ATTN), lambda b, c: (0, 0))],
        out_specs=pl.BlockSpec((None, N_PATTERNS, None, S, D_ATTN), lambda b, c: (c, 0, b, 0, 0)),
        out_shape=jax.ShapeDtypeStruct((3, N_PATTERNS, B, S, D_ATTN), BF16),
        scratch_shapes=[pltpu.VMEM((4, S, LANES), F32), pltpu.VMEM((S, LANES), F32)],
        compiler_params=_params("arbitrary", "arbitrary"), name=name)(u, gains, emat)


def _band_mask(p, b):
    nblk = jnp.right_shift(16, 2 * p)
    has_prev = jnp.bitwise_and(b, nblk - 1) != 0
    qi = lax.broadcasted_iota(jnp.int32, (QBLK, 2 * QBLK), 0)
    ci = lax.broadcasted_iota(jnp.int32, (QBLK, 2 * QBLK), 1)
    dist = QBLK + qi - ci
    return (dist >= 0) & (dist <= QBLK) & (has_prev | (ci >= QBLK))


def _first_head(rows):
    return lax.broadcasted_iota(jnp.int32, (rows, LANES), 1) < HEAD_DIM


def _split_heads(pair):
    first = _first_head(pair.shape[0])
    zero = jnp.zeros_like(pair)
    return jnp.concatenate([jnp.where(first, pair, zero), jnp.where(first, zero, pair)], axis=0)


def _merge_heads(col_a, col_b):
    rows = col_a.shape[0]
    return jnp.where(_first_head(rows), jnp.broadcast_to(col_a, (rows, LANES)), jnp.broadcast_to(col_b, (rows, LANES)))


QB_FWD = 8
QB_BWD = 4


def _attn_fwd(qkv, *, name, phase=None):
    QB = QB_FWD
    nb = qkv.shape[2]

    def body(q_ref, kp_ref, kc_ref, vp_ref, vc_ref, o_ref, lse_ref):
        kall = jnp.concatenate([kp_ref[...]] + [kc_ref[t] for t in range(QB)], axis=0)
        vall = jnp.concatenate([vp_ref[...]] + [vc_ref[t] for t in range(QB)], axis=0)
        masks = []
        for t in range(QB):
            mask = _band_mask(pl.program_id(0), QB * pl.program_id(1) + t)
            masks.append(jnp.concatenate([mask, mask], axis=0))
        units = [(t, hp) for t in range(QB) for hp in range(HEADS // 2)]
        scores, probs = {}, {}
        for step in range(len(units) + 2):
            if step < len(units):
                t, hp = units[step]
                cols = slice(hp * LANES, (hp + 1) * LANES)
                scores[step] = _dot_nt(_split_heads(q_ref[t, :, cols]), kall[t * QBLK:(t + 2) * QBLK, cols])
            if 1 <= step <= len(units):
                t, hp = units[step - 1]
                cols = slice(hp * LANES, (hp + 1) * LANES)
                s = jnp.where(masks[t], scores.pop(step - 1), NEG)
                m = jnp.max(s, axis=-1, keepdims=True)
                e = jnp.exp(s - m)
                l = jnp.sum(e, axis=-1, keepdims=True)
                probs[step - 1] = (e * (1.0 / l)).astype(BF16)
                lse = m + jnp.log(l)
                lse_ref[t, :, cols] = _merge_heads(lse[:QBLK], lse[QBLK:])
            if 2 <= step:
                t, hp = units[step - 2]
                cols = slice(hp * LANES, (hp + 1) * LANES)
                pr = probs.pop(step - 2)
                o_ref[t, :, cols] = _dot(jnp.concatenate([pr[:QBLK], pr[QBLK:]], axis=1),
                                         _split_heads(vall[t * QBLK:(t + 2) * QBLK, cols]))

    cur = lambda which: pl.BlockSpec((None, None, QB, QBLK, D_ATTN), lambda p, i: (which, p, i, 0, 0))
    prev = lambda which: pl.BlockSpec((None, None, None, QBLK, D_ATTN),
                                      lambda p, i: (which, p, jnp.maximum(QB * i - 1, 0), 0, 0))
    out = pl.BlockSpec((None, QB, QBLK, D_ATTN), lambda p, i: (p, i, 0, 0))
    return _call(
        body, grid=(N_PATTERNS, nb // QB), in_specs=[cur(0), prev(1), cur(1), prev(2), cur(2)], out_specs=[out, out],
        out_shape=[jax.ShapeDtypeStruct((N_PATTERNS, nb, QBLK, D_ATTN), F32)] * 2,
        sem=("arbitrary", "arbitrary"), name=name, args=(qkv, qkv, qkv, qkv, qkv), phase=phase)


def _attn_combine(o3, lse3, B, S, *, name):
    def body(o_ref, l_ref, a_ref, lt_ref, so_ref, sl_ref, tmp_ref):
        for p in range(N_PATTERNS):
            _unpermute_in(lambda r0, n, p=p: o_ref[p, pl.ds(r0, n), :], so_ref.at[p], tmp_ref, p, S)
            _unpermute_in(lambda r0, n, p=p: l_ref[p, pl.ds(r0, n), :], sl_ref.at[p], tmp_ref, p, S)
        l0, l1, l2 = sl_ref[0], sl_ref[1], sl_ref[2]
        m = jnp.maximum(jnp.maximum(l0, l1), l2)
        w0, w1, w2 = jnp.exp(l0 - m), jnp.exp(l1 - m), jnp.exp(l2 - m)
        tot = w0 + w1 + w2
        a_ref[...] = (w0 * so_ref[0] + w1 * so_ref[1] + w2 * so_ref[2]) / tot
        lt_ref[...] = m + jnp.log(tot)

    o3 = o3.reshape(N_PATTERNS, B, S, D_ATTN)
    lse3 = lse3.reshape(N_PATTERNS, B, S, D_ATTN)
    inp = pl.BlockSpec((N_PATTERNS, None, S, LANES), lambda b, c: (0, b, 0, c))
    out = pl.BlockSpec((S, LANES), lambda b, c: (b, c))
    return pl.pallas_call(
        body, grid=(B, D_ATTN // LANES), in_specs=[inp, inp], out_specs=[out, out],
        out_shape=[jax.ShapeDtypeStruct((B * S, D_ATTN), F32)] * 2,
        scratch_shapes=[pltpu.VMEM((N_PATTERNS, S, LANES), F32)] * 2 + [pltpu.VMEM((S, LANES), F32)],
        compiler_params=_params("arbitrary", "arbitrary"), name=name)(o3, lse3)


STAT_D = 8


def _attn_bwd_prep(dattn, attn, lse, B, S, *, name):
    emat = _seg_matrix(LANES)
    ncc = D_ATTN // LANES

    def body(da_ref, a_ref, l_ref, e_ref, do_ref, st_ref, scr_ref, nat_ref, tmp_ref):
        cc = pl.program_id(1)
        da = da_ref[...]
        dsum = _seg_mean(da * a_ref[...], e_ref, 1.0)
        scr_ref[...] = da

        def put_do(p, row0, v):
            do_ref[p, row0:row0 + v.shape[0], :] = v.astype(BF16)

        _permute(scr_ref, tmp_ref, put_do)

        lane = lax.broadcasted_iota(jnp.int32, (S, LANES), 1)
        h0 = 2 * cc
        vals = ((h0, l_ref[:, 0:1]), (h0 + 1, l_ref[:, HEAD_DIM:HEAD_DIM + 1]),
                (STAT_D + h0, dsum[:, 0:1]), (STAT_D + h0 + 1, dsum[:, HEAD_DIM:HEAD_DIM + 1]))
        tile = jnp.where(cc == 0, 0.0, nat_ref[...])
        for at, col in vals:
            tile = jnp.where(lane == at, col, tile)
        nat_ref[...] = tile

        @pl.when(cc == ncc - 1)
        def _():
            def put_st(p, row0, v):
                st_ref[p, row0:row0 + v.shape[0], :] = v

            _permute(nat_ref, tmp_ref, put_st)

    inp = pl.BlockSpec((S, LANES), lambda b, c: (b, c))
    return pl.pallas_call(
        body, grid=(B, ncc),
        in_specs=[inp, inp, inp, pl.BlockSpec((LANES, LANES), lambda b, c: (0, 0))],
        out_specs=[pl.BlockSpec((N_PATTERNS, None, S, LANES), lambda b, c: (0, b, 0, c)),
                   pl.BlockSpec((N_PATTERNS, None, S, LANES), lambda b, c: (0, b, 0, 0))],
        out_shape=[jax.ShapeDtypeStruct((N_PATTERNS, B, S, D_ATTN), BF16),
                   jax.ShapeDtypeStruct((N_PATTERNS, B, S, LANES), F32)],
        scratch_shapes=[pltpu.VMEM((S, LANES), F32)] * 3,
        compiler_params=_params("arbitrary", "arbitrary"), name=name)(dattn, attn, lse, emat)


def _attn_bwd(qkv, do3, st3, *, name, phase=None):
    QB = QB_BWD
    nb = qkv.shape[2]
    ngroups = nb // QB

    def body(q_ref, kp_ref, kc_ref, vp_ref, vc_ref, do_ref, st_ref, out_ref, carry_ref):
        p = pl.program_id(0)
        i = pl.program_id(1)

        @pl.when((p == 0) & (i == 0))
        def _():
            carry_ref[...] = jnp.zeros_like(carry_ref)

        kall = jnp.concatenate([kp_ref[...]] + [kc_ref[t] for t in range(QB)], axis=0)
        vall = jnp.concatenate([vp_ref[...]] + [vc_ref[t] for t in range(QB)], axis=0)

        masks = []
        for t in range(QB):
            mask = _band_mask(p, QB * i + t) & (i < ngroups)
            masks.append(jnp.concatenate([mask, mask], axis=0))

        def operands(t, hp):
            cols = slice(hp * LANES, (hp + 1) * LANES)
            kh, vh = kall[t * QBLK:(t + 2) * QBLK, cols], vall[t * QBLK:(t + 2) * QBLK, cols]
            return kh, vh, _split_heads(q_ref[t, :, cols]), _split_heads(do_ref[t, :, cols])

        def stage_scores(t, hp):
            kh, vh, q2, do2 = operands(t, hp)
            return _dot_nt(q2, kh), _dot_nt(do2, vh)

        def stage_softmax(t, hp, s, dp):
            h0, h1 = 2 * hp, 2 * hp + 1
            lse = jnp.concatenate([st_ref[t, :, h0:h0 + 1], st_ref[t, :, h1:h1 + 1]], axis=0)
            dsum = jnp.concatenate([st_ref[t, :, STAT_D + h0:STAT_D + h0 + 1],
                                    st_ref[t, :, STAT_D + h1:STAT_D + h1 + 1]], axis=0)
            pr = jnp.where(masks[t], jnp.exp(s - lse), 0.0)
            return (pr * (dp - dsum)).astype(BF16), pr.astype(BF16)

        def stage_grads(t, hp, ds, prb):
            cols = slice(hp * LANES, (hp + 1) * LANES)
            kh, vh, q2, do2 = operands(t, hp)
            dq = _dot(jnp.concatenate([ds[:QBLK], ds[QBLK:]], axis=1), _split_heads(kh))
            dk, dv = _dot_tn(ds, q2), _dot_tn(prb, do2)
            if t == 0:
                for c in range(3):
                    for tt in range(QB):
                        v = carry_ref[c, tt, :, cols]
                        if tt == QB - 1 and c > 0:
                            v = v + (dk if c == 1 else dv)[:QBLK]
                        out_ref[c, tt, :, cols] = v.astype(BF16)
            else:
                carry_ref[1, t - 1, :, cols] += dk[:QBLK]
                carry_ref[2, t - 1, :, cols] += dv[:QBLK]
            carry_ref[0, t, :, cols] = dq
            carry_ref[1, t, :, cols] = dk[QBLK:]
            carry_ref[2, t, :, cols] = dv[QBLK:]

        units = [(t, hp) for hp in range(HEADS // 2) for t in range(QB)]
        scores, probs = {}, {}
        for step in range(len(units) + 2):
            if step < len(units):
                scores[step] = stage_scores(*units[step])
            if 1 <= step <= len(units):
                probs[step - 1] = stage_softmax(*units[step - 1], *scores.pop(step - 1))
            if 2 <= step:
                stage_grads(*units[step - 2], *probs.pop(step - 2))

    group = lambda i: jnp.minimum(i, ngroups - 1)
    cur = lambda which: pl.BlockSpec((None, None, QB, QBLK, D_ATTN), lambda p, i: (which, p, group(i), 0, 0))
    prev = lambda which: pl.BlockSpec((None, None, None, QBLK, D_ATTN),
                                      lambda p, i: (which, p, jnp.maximum(QB * group(i) - 1, 0), 0, 0))
    aux = lambda lanes: pl.BlockSpec((None, QB, QBLK, lanes), lambda p, i: (p, group(i), 0, 0))
    return _call(
        body, grid=(N_PATTERNS, ngroups + 1),
        in_specs=[cur(0), prev(1), cur(1), prev(2), cur(2), aux(D_ATTN), aux(LANES)],
        out_specs=[pl.BlockSpec((3, None, QB, QBLK, D_ATTN), lambda p, i: (0, p, jnp.maximum(i - 1, 0), 0, 0))],
        out_shape=[jax.ShapeDtypeStruct((3, N_PATTERNS, nb, QBLK, D_ATTN), BF16)],
        scratch_shapes=[pltpu.VMEM((3, QB, QBLK, D_ATTN), F32)],
        sem=("arbitrary", "arbitrary"), name=name, args=(qkv, qkv, qkv, qkv, qkv, do3, st3), phase=phase)


def _attn_grad_combine(cur, u, gains, B, S, *, name, phase=None):
    emat = _seg_matrix(LANES)

    def body(cur_ref, u_ref, g_ref, e_ref, du_ref, dg_ref, scr_ref, tmp_ref):
        c = pl.program_id(0)
        b = pl.program_id(2)
        for p in range(N_PATTERNS):
            _unpermute_in(lambda r0, n, p=p: cur_ref[p, pl.ds(r0, n), :].astype(F32), scr_ref.at[p], tmp_ref, p, S)
        dy = scr_ref[0] + scr_ref[1] + scr_ref[2]
        xv = u_ref[...]
        gain = g_ref[...]
        ms = _seg_mean(xv * xv, e_ref, HEAD_DIM)
        r = lax.rsqrt(ms + EPS)
        xhat = xv * r
        dxh = dy * gain
        dx = r * (dxh - xhat * _seg_mean(dxh * xhat, e_ref, HEAD_DIM))
        du_ref[...] = jnp.where(c < 2, dx, dy).astype(BF16)

        @pl.when((b == 0))
        def _():
            dg_ref[...] = jnp.zeros_like(dg_ref)

        dg_ref[...] += jnp.sum(dy * xhat, axis=0, keepdims=True)

    cur = cur.reshape(3, N_PATTERNS, B, S, D_ATTN)
    ncc = D_ATTN // LANES
    return _call(
        body, grid=(3, ncc, B),
        in_specs=[pl.BlockSpec((None, N_PATTERNS, None, S, LANES), lambda c, cc, b: (c, 0, b, 0, cc)),
                  pl.BlockSpec((S, LANES), lambda c, cc, b: (b, c * ncc + cc)),
                  pl.BlockSpec((None, 1, LANES), lambda c, cc, b: (c, 0, cc)),
                  pl.BlockSpec((LANES, LANES), lambda c, cc, b: (0, 0))],
        out_specs=[pl.BlockSpec((S, LANES), lambda c, cc, b: (b, c * ncc + cc)),
                   pl.BlockSpec((None, 1, LANES), lambda c, cc, b: (c, 0, cc))],
        out_shape=[jax.ShapeDtypeStruct((B * S, 3 * D_ATTN), BF16), jax.ShapeDtypeStruct((3, 1, D_ATTN), F32)],
        scratch_shapes=[pltpu.VMEM((N_PATTERNS, S, LANES), F32), pltpu.VMEM((S, LANES), F32)],
        sem=("arbitrary", "arbitrary", "arbitrary"), name=name, args=(cur, u, gains, emat), phase=phase)


HALO = 32
SUB = 64
SUBLANES = 8


def _shifted_copies(src_ref, sh_ref, tc):
    sh_ref[0] = src_ref[...]
    for r in range(1, SUBLANES):
        sh_ref[r, 0:tc + HALO - SUBLANES, :] = src_ref[pl.ds(r, tc + HALO - SUBLANES), :]


def _shifted(sh_ref, start, size):
    return sh_ref[start % SUBLANES, pl.ds(start - start % SUBLANES, size), :]


def _conv_fwd(u, cw, cb, lg, lb, B, S, *, tc, name, phase=None):
    nchunk = S // tc
    hb = tc // HALO

    def body(ca_ref, cap_ref, cg_ref, cgp_ref, w_ref, cb_ref, lg_ref, lb_ref, cv_ref, glu_ref, y_ref, pad_ref, sh_ref):
        i = pl.program_id(1)
        glu = ca_ref[...] * _sigmoid(cg_ref[...])
        glu_ref[...] = glu
        prev = cap_ref[...] * _sigmoid(cgp_ref[...])
        pad_ref[0:HALO, :] = jnp.where(i > 0, prev, 0.0)
        pad_ref[HALO:, :] = glu
        _shifted_copies(pad_ref, sh_ref, tc)
        for sub in range(tc // SUB):
            acc = jnp.zeros((SUB, D_CONV), F32) + cb_ref[...]
            for k in range(CONV_K):
                acc = acc + _shifted(sh_ref, sub * SUB + HALO - (CONV_K - 1) + k, SUB) * w_ref[pl.ds(k, 1), :]
            y_ref[sub * SUB:(sub + 1) * SUB, :] = acc
        y = y_ref[...]
        mu = jnp.mean(y, axis=-1, keepdims=True)
        yc = y - mu
        var = jnp.mean(yc * yc, axis=-1, keepdims=True)
        z = yc * lax.rsqrt(var + EPS) * lg_ref[...] + lb_ref[...]
        cv_ref[...] = (z * _sigmoid(z)).astype(BF16)

    def cur(col):
        return pl.BlockSpec((tc, D_CONV), lambda b, i: (b * nchunk + i, col))

    def halo(col):
        return pl.BlockSpec((HALO, D_CONV), lambda b, i: (jnp.maximum((b * nchunk + i) * hb - 1, 0), col))

    vec = pl.BlockSpec((1, D_CONV), lambda b, i: (0, 0))
    out = pl.BlockSpec((tc, D_CONV), lambda b, i: (b * nchunk + i, 0))
    return _call(
        body, grid=(B, nchunk),
        in_specs=[cur(3), halo(3), cur(4), halo(4), pl.BlockSpec((CONV_K, D_CONV), lambda b, i: (0, 0)), vec, vec, vec],
        out_specs=[out, out, out],
        out_shape=[jax.ShapeDtypeStruct((B * S, D_CONV), BF16), jax.ShapeDtypeStruct((B * S, D_CONV), F32),
                   jax.ShapeDtypeStruct((B * S, D_CONV), F32)],
        scratch_shapes=[pltpu.VMEM((tc + HALO, D_CONV), F32), pltpu.VMEM((SUBLANES, tc + HALO, D_CONV), F32)],
        sem=("arbitrary", "arbitrary"), name=name, args=(u, u, u, u, cw, cb, lg, lb), phase=phase)


def _conv_bwd_norm(dcv, y, lg, lb, *, tc, name):
    T = y.shape[0]

    def body(dcv_ref, y_ref, lg_ref, lb_ref, dy_ref, part_ref):
        yv = y_ref[...]
        mu = jnp.mean(yv, axis=-1, keepdims=True)
        yc = yv - mu
        var = jnp.mean(yc * yc, axis=-1, keepdims=True)
        rstd = lax.rsqrt(var + EPS)
        xhat = yc * rstd
        z = xhat * lg_ref[...] + lb_ref[...]
        sig = _sigmoid(z)
        dz = dcv_ref[...] * (sig * (1.0 + z * (1.0 - sig)))
        dxh = dz * lg_ref[...]
        dy = rstd * (dxh - jnp.mean(dxh, axis=-1, keepdims=True)
                     - xhat * jnp.mean(dxh * xhat, axis=-1, keepdims=True))
        dy_ref[...] = dy

        @pl.when(pl.program_id(0) == 0)
        def _():
            part_ref[...] = jnp.zeros_like(part_ref)

        part_ref[0:1, :] += jnp.sum(dz * xhat, axis=0, keepdims=True)
        part_ref[1:2, :] += jnp.sum(dz, axis=0, keepdims=True)
        part_ref[2:3, :] += jnp.sum(dy, axis=0, keepdims=True)

    tok = pl.BlockSpec((tc, D_CONV), lambda i: (i, 0))
    vec = pl.BlockSpec((1, D_CONV), lambda i: (0, 0))
    return pl.pallas_call(
        body, grid=(T // tc,), in_specs=[tok, tok, vec, vec],
        out_specs=[tok, pl.BlockSpec((8, D_CONV), lambda i: (0, 0))],
        out_shape=[jax.ShapeDtypeStruct((T, D_CONV), F32), jax.ShapeDtypeStruct((8, D_CONV), F32)],
        compiler_params=_params("arbitrary"), name=name)(dcv, y, lg, lb)


def _conv_bwd_taps(dy, glu, u, cw, B, S, *, tc, name, phase=None):
    nchunk = S // tc
    hb = tc // HALO
    last_hb = B * S // HALO - 1

    def body(dy_ref, dyn_ref, glu_ref, glup_ref, ca_ref, cg_ref, w_ref, dca_ref, dcg_ref, dw_ref,
             dyp_ref, glp_ref, acc_ref, shd_ref, shg_ref):
        b = pl.program_id(0)
        i = pl.program_id(1)
        dy = dy_ref[...]
        dyp_ref[0:tc, :] = dy
        dyp_ref[tc:, :] = jnp.where(i < nchunk - 1, dyn_ref[...], 0.0)
        glp_ref[0:HALO, :] = jnp.where(i > 0, glup_ref[...], 0.0)
        glp_ref[HALO:, :] = glu_ref[...]
        _shifted_copies(dyp_ref, shd_ref, tc)
        _shifted_copies(glp_ref, shg_ref, tc)

        @pl.when((b == 0) & (i == 0))
        def _():
            dw_ref[...] = jnp.zeros_like(dw_ref)

        for sub in range(tc // SUB):
            acc = jnp.zeros((SUB, D_CONV), F32)
            for k in range(CONV_K):
                acc = acc + _shifted(shd_ref, sub * SUB + (CONV_K - 1) - k, SUB) * w_ref[pl.ds(k, 1), :]
            acc_ref[sub * SUB:(sub + 1) * SUB, :] = acc
        for k in range(CONV_K):
            dw_ref[k:k + 1, :] += jnp.sum(dy * _shifted(shg_ref, HALO - (CONV_K - 1) + k, tc), axis=0, keepdims=True)
        dglu = acc_ref[...]
        ca = ca_ref[...]
        sig = _sigmoid(cg_ref[...])
        dca_ref[...] = (dglu * sig).astype(BF16)
        dcg_ref[...] = (dglu * ca * sig * (1.0 - sig)).astype(BF16)

    tok = pl.BlockSpec((tc, D_CONV), lambda b, i: (b * nchunk + i, 0))
    nxt = pl.BlockSpec((HALO, D_CONV), lambda b, i: (jnp.minimum((b * nchunk + i + 1) * hb, last_hb), 0))
    prv = pl.BlockSpec((HALO, D_CONV), lambda b, i: (jnp.maximum((b * nchunk + i) * hb - 1, 0), 0))
    return _call(
        body, grid=(B, nchunk),
        in_specs=[tok, nxt, tok, prv,
                  pl.BlockSpec((tc, D_CONV), lambda b, i: (b * nchunk + i, 3)),
                  pl.BlockSpec((tc, D_CONV), lambda b, i: (b * nchunk + i, 4)),
                  pl.BlockSpec((CONV_K, D_CONV), lambda b, i: (0, 0))],
        out_specs=[tok, tok, pl.BlockSpec((32, D_CONV), lambda b, i: (0, 0))],
        out_shape=[jax.ShapeDtypeStruct((B * S, D_CONV), BF16), jax.ShapeDtypeStruct((B * S, D_CONV), BF16),
                   jax.ShapeDtypeStruct((32, D_CONV), F32)],
        scratch_shapes=[pltpu.VMEM((tc + HALO, D_CONV), F32), pltpu.VMEM((tc + HALO, D_CONV), F32),
                        pltpu.VMEM((tc, D_CONV), F32), pltpu.VMEM((SUBLANES, tc + HALO, D_CONV), F32),
                        pltpu.VMEM((SUBLANES, tc + HALO, D_CONV), F32)],
        sem=("arbitrary", "arbitrary"), name=name, args=(dy, dy, glu, glu, u, u, cw), phase=phase)


def _outproj_fwd(h, attn, cv, wout, *, tm, name):
    T, D = h.shape

    def body(h_ref, a_ref, c_ref, w_ref, o_ref):
        o_ref[...] = (h_ref[...] + _dot(a_ref[...].astype(BF16), w_ref[0:D_ATTN, :])
                      + _dot(c_ref[...], w_ref[D_ATTN:, :]))

    tok = pl.BlockSpec((tm, D), lambda i: (i, 0))
    half = pl.BlockSpec((tm, D_ATTN), lambda i: (i, 0))
    return pl.pallas_call(
        body, grid=(T // tm,), in_specs=[tok, half, half, pl.BlockSpec(wout.shape, lambda i: (0, 0))],
        out_specs=tok, out_shape=jax.ShapeDtypeStruct((T, D), F32),
        compiler_params=_params("arbitrary"), name=name)(h, attn, cv, wout)


def _outproj_bwd(dh, attn, cv, wout, *, tm, name):
    T, D = dh.shape

    def body(dh_ref, a_ref, c_ref, w_ref, da_ref, dc_ref, dw_ref):
        @pl.when(pl.program_id(0) == 0)
        def _():
            dw_ref[...] = jnp.zeros_like(dw_ref)

        dhb = dh_ref[...].astype(BF16)
        da_ref[...] = _dot_nt(dhb, w_ref[0:D_ATTN, :])
        dc_ref[...] = _dot_nt(dhb, w_ref[D_ATTN:, :])
        dw_ref[0:D_ATTN, :] += _dot_tn(a_ref[...].astype(BF16), dhb)
        dw_ref[D_ATTN:, :] += _dot_tn(c_ref[...], dhb)

    tok = pl.BlockSpec((tm, D), lambda i: (i, 0))
    half = pl.BlockSpec((tm, D_ATTN), lambda i: (i, 0))
    wspec = pl.BlockSpec(wout.shape, lambda i: (0, 0))
    return pl.pallas_call(
        body, grid=(T // tm,), in_specs=[tok, half, half, wspec], out_specs=[half, half, wspec],
        out_shape=[jax.ShapeDtypeStruct((T, D_ATTN), F32), jax.ShapeDtypeStruct((T, D_ATTN), F32),
                   jax.ShapeDtypeStruct(wout.shape, F32)],
        compiler_params=_params("arbitrary"), name=name)(dh, attn, cv, wout)


ADAM_BLOCK_BYTES = 3 * 512 * 1024


def _adamw(w, g, m, v, *, name):
    R, C = w.shape
    tr = R
    for cand in (512, 352, 256, 176, 128, 64, 32, 16, 8):
        if R % cand == 0 and cand * C * 4 <= ADAM_BLOCK_BYTES:
            tr = cand
            break
    c1 = 1.0 - ADAM_B1 ** ADAM_STEP
    c2 = 1.0 - ADAM_B2 ** ADAM_STEP

    def body(w_ref, g_ref, m_ref, v_ref, d_ref, nm_ref, nv_ref):
        gv = g_ref[...]
        nm = ADAM_B1 * m_ref[...] + (1.0 - ADAM_B1) * gv
        nv = ADAM_B2 * v_ref[...] + (1.0 - ADAM_B2) * (gv * gv)
        d_ref[...] = -ADAM_LR * ((nm / c1) / (jnp.sqrt(nv / c2) + ADAM_EPS) + ADAM_WD * w_ref[...])
        nm_ref[...] = nm
        nv_ref[...] = nv

    blk = pl.BlockSpec((tr, C), lambda i: (i, 0))
    return pl.pallas_call(
        body, grid=(R // tr,), in_specs=[blk] * 4, out_specs=[blk] * 3,
        out_shape=[jax.ShapeDtypeStruct((R, C), F32)] * 3,
        compiler_params=_params("arbitrary"), name=name)(w, g, m, v)


ADAM_SPLIT = 4


def _adamw_many(ws, gs, ms, vs, *, name, phase=None):
    n = len(ws)
    c1 = 1.0 - ADAM_B1 ** ADAM_STEP
    c2 = 1.0 - ADAM_B2 ** ADAM_STEP

    def body(*refs):
        ins, outs = refs[:4 * n], refs[4 * n:]
        for a in range(n):
            w_ref, g_ref, m_ref, v_ref = ins[4 * a:4 * a + 4]
            gv = g_ref[...]
            nm = ADAM_B1 * m_ref[...] + (1.0 - ADAM_B1) * gv
            nv = ADAM_B2 * v_ref[...] + (1.0 - ADAM_B2) * (gv * gv)
            outs[3 * a][...] = -ADAM_LR * ((nm / c1) / (jnp.sqrt(nv / c2) + ADAM_EPS) + ADAM_WD * w_ref[...])
            outs[3 * a + 1][...] = nm
            outs[3 * a + 2][...] = nv

    in_specs, out_specs, out_shape, args = [], [], [], []
    for w, g, m, v in zip(ws, gs, ms, vs):
        R, C = w.shape
        blk = pl.BlockSpec((R // ADAM_SPLIT, C), lambda i: (i, 0))
        in_specs += [blk] * 4
        out_specs += [blk] * 3
        out_shape += [jax.ShapeDtypeStruct((R, C), F32)] * 3
        args += [w, g, m, v]
    res = _call(body, grid=(ADAM_SPLIT,), in_specs=in_specs, out_specs=out_specs, out_shape=out_shape,
                sem=("arbitrary",), name=name, args=args, phase=phase)
    outs, extra = res if phase is not None else (res, None)
    return list(outs[0::3]), list(outs[1::3]), list(outs[2::3]), extra


TM = 512
TM_WIDE = 1024
TK = 1024
TC = 512


def _local_step(x, tgt, w, overlap=None):
    B, S, D = x.shape
    T = B * S
    x2 = x.reshape(T, D)
    t2 = tgt.reshape(T, D)
    ones = jnp.ones((1, D_ATTN), F32)
    scale = HEAD_DIM ** -0.5
    gains = jnp.stack([jnp.tile(w["q_norm"], (1, HEADS)) * scale, jnp.tile(w["k_norm"], (1, HEADS)), ones])
    g = {}

    def hosting(point, build):
        phase = overlap.phase(point, w, g) if overlap is not None else None
        if phase is None:
            return build(None)
        outs, extra = build(phase)
        overlap.done(point, extra, w, g)
        return outs

    h1, n1, G1, U1 = hosting("ffn1_fwd", lambda ph: _ffn_fwd(
        x2, w["ffn1_norm"], w["wg1"], w["wu1"], w["wd1"], None, tm=TM_WIDE, name="ffn1_fwd", phase=ph))
    u, n2 = hosting("inproj_fwd", lambda ph: _inproj_fwd(h1, w["mix_norm"], w["win"], tm=TM_WIDE, name="inproj_fwd", phase=ph))
    qkv = _qkv_prep(u, gains, B, S, name="qkv_prep")
    qkv = qkv.reshape(3, N_PATTERNS, T // QBLK, QBLK, D_ATTN)
    o3, lse3 = hosting("attn_fwd", lambda ph: _attn_fwd(qkv, name="attn_fwd", phase=ph))
    attn, lse = _attn_combine(o3, lse3, B, S, name="attn_combine")
    cv, glu, yconv = hosting("conv_fwd", lambda ph: _conv_fwd(
        u, w["conv_w"], w["conv_b"], w["conv_ln_g"], w["conv_ln_b"], B, S, tc=TC, name="conv_fwd", phase=ph))
    h2 = _outproj_fwd(h1, attn, cv, w["wout"], tm=TM_WIDE, name="outproj_fwd")
    dh3, n3, G2, U2, loss = _ffn_fwd(h2, w["ffn2_norm"], w["wg2"], w["wu2"], w["wd2"], t2, tm=TM_WIDE, name="ffn2_fwd")

    dG, dU, A, dy, dh2, g["ffn2_norm"] = _ffn_bwd_act(dh3, h2, w["ffn2_norm"], G2, U2, w["wg2"], w["wu2"], w["wd2"],
                                                    tm=TM, name="ffn2_bwd_act")
    g["wg2"], g["wu2"], g["wd2"] = _ffn_bwd_w(n3, dy, dG, dU, A, tk=2 * TK, name="ffn2_bwd_w")
    dattn, dcv, g["wout"] = _outproj_bwd(dh2, attn, cv, w["wout"], tm=TM_WIDE, name="outproj_bwd")
    dyc, cpart = _conv_bwd_norm(dcv, yconv, w["conv_ln_g"], w["conv_ln_b"], tc=TC, name="conv_bwd_norm")
    dca, dcg, dcw = hosting("conv_bwd_taps", lambda ph: _conv_bwd_taps(
        dyc, glu, u, w["conv_w"], B, S, tc=TC, name="conv_bwd_taps", phase=ph))
    do3, st3 = _attn_bwd_prep(dattn, attn, lse, B, S, name="attn_bwd_prep")
    nb = T // QBLK
    (cur,) = hosting("attn_bwd", lambda ph: _attn_bwd(
        qkv, do3.reshape(N_PATTERNS, nb, QBLK, D_ATTN), st3.reshape(N_PATTERNS, nb, QBLK, LANES),
        name="attn_bwd", phase=ph))
    du_qkv, dgains = hosting("attn_grad_combine", lambda ph: _attn_grad_combine(
        cur, u, gains, B, S, name="attn_grad_combine", phase=ph))
    du = jnp.concatenate([du_qkv, dca, dcg], axis=1)
    (g["win"],) = hosting("inproj_bwd_w", lambda ph: _inproj_bwd_w(
        n2, du, w["win"].shape[0], tk=2 * TK, name="inproj_bwd_w", phase=ph))
    dh1, g["mix_norm"] = hosting("inproj_bwd_act", lambda ph: _inproj_bwd_act(
        du, dh2, h1, w["mix_norm"], w["win"], tm=TM_WIDE, name="inproj_bwd_act", phase=ph))
    dG, dU, A, dy, dx, g["ffn1_norm"] = hosting("ffn1_bwd_act", lambda ph: _ffn_bwd_act(
        dh1, x2, w["ffn1_norm"], G1, U1, w["wg1"], w["wu1"], w["wd1"], tm=TM, name="ffn1_bwd_act", phase=ph))
    g["wg1"], g["wu1"], g["wd1"] = hosting("ffn1_bwd_w", lambda ph: _ffn_bwd_w(
        n1, dy, dG, dU, A, tk=2 * TK, name="ffn1_bwd_w", phase=ph))

    g["q_norm"] = dgains[0].reshape(HEADS, HEAD_DIM).sum(axis=0, keepdims=True) * scale
    g["k_norm"] = dgains[1].reshape(HEADS, HEAD_DIM).sum(axis=0, keepdims=True)
    g["conv_ln_g"] = cpart[0:1]
    g["conv_ln_b"] = cpart[1:2]
    g["conv_b"] = cpart[2:3]
    g["conv_w"] = dcw[:CONV_K]
    return loss, dx.reshape(B, S, D), g


N_CHIPS = 4
N_DEV = 8
VMEM_SPEC = pl.BlockSpec(memory_space=pltpu.VMEM)


def _remote(src, dst, send_sem, recv_sem, device):
    return pltpu.make_async_remote_copy(src_ref=src, dst_ref=dst, send_sem=send_sem, recv_sem=recv_sem,
                                        device_id=device, device_id_type=MESH)


def _stage_shards(shards, dtypes, *, name):
    n = len(shards)
    halves = [s.reshape(2, s.shape[0] // 2, s.shape[1]) for s in shards]

    def body(*refs):
        ins, outs, vms, loc_sems = refs[:n], refs[n:2 * n], refs[2 * n:3 * n], refs[3 * n]
        me = 2 * lax.axis_index("x") + lax.axis_index("y")
        copies = []
        for a in range(n):
            vms[a][...] = ins[a][...].astype(dtypes[a])
            cp = pltpu.make_async_copy(vms[a], outs[a].at[me], loc_sems.at[a])
            cp.start()
            copies.append(cp)
        for cp in copies:
            cp.wait()

    return pl.pallas_call(
        body, in_specs=[VMEM_SPEC] * n, out_specs=[ANY] * n,
        out_shape=[jax.ShapeDtypeStruct((N_CHIPS,) + h.shape, dt) for h, dt in zip(halves, dtypes)],
        scratch_shapes=[pltpu.VMEM(h.shape, dt) for h, dt in zip(halves, dtypes)] + [DMA_SEMS((n,))],
        compiler_params=pltpu.CompilerParams(vmem_limit_bytes=VMEM_LIMIT), name=name)(*halves)


def _like(arrays):
    return [jax.ShapeDtypeStruct(a.shape, a.dtype) for a in arrays]


def _axes():
    x, y, c = lax.axis_index("x"), lax.axis_index("y"), lax.axis_index("c")
    first = (x + (1 - c) * (1 - 2 * x), y + c * (1 - 2 * y))
    second = (x + c * (1 - 2 * x), y + (1 - c) * (1 - 2 * y))
    slots = tuple(2 * px + py for px, py in ((x, y), first, second, (1 - x, 1 - y)))
    return (x, y, c), (*first, c), (*second, c), slots


def _gather_ici_phase(bufs, only=None):
    n = len(bufs)

    def stage1(ins, outs, sems):
        (x, y, c), peer1, peer2, (own, s1, s2, both) = _axes()
        starts, arrivals = [], []
        for a in range(n):
            mine, land = outs[a].at[own, c], outs[a].at[s2, c]
            starts.append(_remote(mine, mine, *sems(a), peer2))
            arrivals.append(_remote(land, land, *sems(a), peer2))
        return starts, arrivals

    def stage2(ins, outs, sems):
        (x, y, c), peer1, peer2, (own, s1, s2, both) = _axes()
        starts, arrivals = [], []
        for a in range(n):
            for k, (src, dst) in enumerate(((own, s1), (s2, both))):
                mine, land = outs[a].at[src, c], outs[a].at[dst, c]
                starts.append(_remote(mine, mine, *sems(2 * a + k), peer1))
                arrivals.append(_remote(land, land, *sems(2 * a + k), peer1))
        return starts, arrivals

    same = {a: a for a in range(n)}
    first, second = _Phase(bufs, _like(bufs), same, n, stage1), _Phase(bufs, _like(bufs), same, 2 * n, stage2)
    if only is None:
        return first.then(second)
    return first if only == 1 else second


def _gather_d2d_phase(bufs):
    n = len(bufs)

    def copies(ins, outs, sems):
        (x, y, c), peer1, peer2, (own, s1, s2, both) = _axes()
        starts, arrivals = [], []
        for a in range(n):
            for j, s in enumerate((s1, s2, both)):
                got, land = outs[a].at[s, c], outs[a].at[s, 1 - c]
                starts.append(_remote(got, got, *sems(3 * a + j), (x, y, 1 - c)))
                arrivals.append(_remote(land, land, *sems(3 * a + j), (x, y, 1 - c)))
        return starts, arrivals

    return _Phase(bufs, _like(bufs), {a: a for a in range(n)}, 3 * n, copies)


def _gather_pipelined_phase(bufs):
    n = len(bufs)

    def piece(kind, a):
        def copies(ins, outs, sems):
            (x, y, c), peer1, peer2, (own, s1, s2, both) = _axes()
            moves = {1: [(own, s2, c, c, peer2)],
                     2: [(own, s1, c, c, peer1), (s2, both, c, c, peer1)],
                     3: [(s, s, c, 1 - c, (x, y, 1 - c)) for s in (s1, s2, both)]}[kind]
            starts, arrivals = [], []
            for k, (src, dst, h_src, h_dst, peer) in enumerate(moves):
                mine, land = outs[a].at[src, h_src], outs[a].at[dst, h_dst]
                starts.append(_remote(mine, mine, *sems(k), peer))
                arrivals.append(_remote(land, land, *sems(k), peer))
            return starts, arrivals

        return kind, copies

    def beside(pieces):
        def copies(ins, outs, sems):
            starts, arrivals, base = [], [], 0
            for count, fn in pieces:
                s, r = fn(ins, outs, lambda i, base=base: sems(base + i))
                starts, arrivals, base = starts + s, arrivals + r, base + count
            return starts, arrivals

        return sum(count for count, _ in pieces), copies

    phase = _Phase(bufs, _like(bufs), {a: a for a in range(n)}, 0, None)
    phase.stages = [beside([piece(kind, t - kind + 1) for kind in (1, 2, 3) if 0 <= t - kind + 1 < n])
                    for t in range(n + 2)]
    return phase


def _exchange_phase(views):
    n = len(views)

    def copies(ins, outs, sems):
        x, y, c = lax.axis_index("x"), lax.axis_index("y"), lax.axis_index("c")
        starts = [_remote(ins[a].at[pl.ds(0, ins[a].shape[0]), 1 - c], outs[a], *sems(a), (x, y, 1 - c))
                  for a in range(n)]
        return starts, starts

    outs = [jax.ShapeDtypeStruct((v.shape[0],) + v.shape[2:], F32) for v in views]
    return _Phase(views, outs, {}, n, copies)


ADD_SPLIT = 2


def _add_halves(views, got, sel, tag):
    n = len(views)

    def body(s_ref, *refs):
        ins, outs = refs[:4 * n], refs[4 * n:]
        for a in range(n):
            gk, rk, gs, rs = ins[4 * a:4 * a + 4]
            outs[2 * a][...] = gk[...] + rk[...]
            outs[2 * a + 1][...] = (gs[...] + rs[...]).astype(BF16)

    in_specs, out_specs, out_shape, args = [], [], [], []
    for g, r in zip(views, got):
        _, _, rh, cdim = g.shape
        tr = rh // ADD_SPLIT
        for off in (0, 2):
            in_specs.append(pl.BlockSpec((None, None, tr, cdim), lambda k, i, s, off=off: (s[1 + off + k], s[0], i, 0)))
            in_specs.append(pl.BlockSpec((None, tr, cdim), lambda k, i, s, off=off: (s[1 + off + k], i, 0)))
            args += [g, r]
        out_specs += [pl.BlockSpec((None, tr, cdim), lambda k, i, s: (k, i, 0))] * 2
        out_shape += [jax.ShapeDtypeStruct((2, rh, cdim), F32), jax.ShapeDtypeStruct((2, rh, cdim), BF16)]
    res = pl.pallas_call(
        body,
        grid_spec=pltpu.PrefetchScalarGridSpec(num_scalar_prefetch=1, grid=(2, ADD_SPLIT), in_specs=in_specs,
                                               out_specs=out_specs),
        out_shape=out_shape, compiler_params=_params("arbitrary", "arbitrary"), name=f"rs_add_half_{tag}")(sel, *args)
    return list(res[0::2]), list(res[1::2])


def _swap_phase(arrays, stage):
    n = len(arrays)

    def copies(ins, outs, sems):
        peer = _axes()[stage]
        starts = [_remote(ins[a], outs[a], *sems(a), peer) for a in range(n)]
        return starts, starts

    return _Phase(arrays, _like(arrays), {}, n, copies)


def _add_first(keep, got, tag):
    n = len(keep)

    def body(*refs):
        ins, outs = refs[:2 * n], refs[2 * n:]
        for a in range(n):
            k_ref, g_ref = ins[2 * a], ins[2 * a + 1]
            outs[2 * a][...] = k_ref[0] + g_ref[0].astype(F32)
            outs[2 * a + 1][...] = (k_ref[1] + g_ref[1].astype(F32)).astype(BF16)

    in_specs, out_specs, out_shape, args = [], [], [], []
    for k, g in zip(keep, got):
        _, rh, cdim = k.shape
        tr = rh // ADD_SPLIT
        in_specs += [pl.BlockSpec((2, tr, cdim), lambda i: (0, i, 0))] * 2
        out_specs += [pl.BlockSpec((tr, cdim), lambda i: (i, 0))] * 2
        out_shape += [jax.ShapeDtypeStruct((rh, cdim), F32), jax.ShapeDtypeStruct((rh, cdim), BF16)]
        args += [k, g]
    res = pl.pallas_call(body, grid=(ADD_SPLIT,), in_specs=in_specs, out_specs=out_specs, out_shape=out_shape,
                         compiler_params=_params("arbitrary"), name=f"rs_add_first_{tag}")(*args)
    return list(res[0::2]), list(res[1::2])


def _add_second(keep, got, sel, tag):
    n = len(keep)

    def body(s_ref, *refs):
        ins, outs = refs[:2 * n], refs[2 * n:]
        for a in range(n):
            outs[a][...] = ins[2 * a][...] + ins[2 * a + 1][...].astype(F32)

    in_specs, out_specs, out_shape, args = [], [], [], []
    for k, g in zip(keep, got):
        rh, cdim = k.shape
        tr = rh // ADD_SPLIT
        in_specs += [pl.BlockSpec((tr, cdim), lambda i, s: (i, 0))] * 2
        out_specs.append(pl.BlockSpec((None, tr, cdim), lambda i, s: (s[0], i, 0)))
        out_shape.append(jax.ShapeDtypeStruct((2, rh, cdim), F32))
        args += [k, g]
    res = pl.pallas_call(
        body,
        grid_spec=pltpu.PrefetchScalarGridSpec(num_scalar_prefetch=1, grid=(ADD_SPLIT,), in_specs=in_specs,
                                               out_specs=out_specs),
        out_shape=out_shape, compiler_params=_params("arbitrary"), name=f"rs_add_second_{tag}")(sel, *args)
    return list(res)


def _join_phase(halves):
    n = len(halves)

    def copies(ins, outs, sems):
        x, y, c = lax.axis_index("x"), lax.axis_index("y"), lax.axis_index("c")
        starts, arrivals = [], []
        for a in range(n):
            mine, land = outs[a].at[c], outs[a].at[1 - c]
            starts.append(_remote(mine, mine, *sems(a), (x, y, 1 - c)))
            arrivals.append(_remote(land, land, *sems(a), (x, y, 1 - c)))
        return starts, arrivals

    return _Phase(halves, _like(halves), {a: a for a in range(n)}, n, copies)


def _slot_order():
    x, y, c = lax.axis_index("x"), lax.axis_index("y"), lax.axis_index("c")
    own, flip_x, flip_y, both = 2 * x + y, 2 * (1 - x) + y, 2 * x + 1 - y, 2 * (1 - x) + 1 - y
    first = jnp.where(c == 0, flip_x, flip_y)
    second = jnp.where(c == 0, flip_y, flip_x)
    return jnp.stack([c, own, second, first, both]).astype(jnp.int32)


def _half_view(g):
    return g.reshape(N_CHIPS, 2, g.shape[1] // 2, g.shape[2])


EARLY_GRADS = ("wg2", "wu2", "wd2", "wout")
MIDDLE_GRADS = ("win",)


class _Overlap:
    EARLY_AT = ("conv_bwd_taps", "attn_bwd", "attn_grad_combine", "inproj_bwd_w")
    MIDDLE_AT = ("inproj_bwd_act", "ffn1_bwd_act", "ffn1_bwd_w", None)

    def __init__(self, staged):
        self.staged = staged
        self.ffn2 = list(staged[3:])
        self.sel = sel = _slot_order()
        self.early = _Reduction(EARLY_GRADS, "early", sel)
        self.middle = _Reduction(MIDDLE_GRADS, "middle", sel)

    def finish_late(self, late):
        views = [_half_view(a) for a in late]
        got = _run_phase(_exchange_phase(views), name="rs_exchange_halves")
        keep, send = _add_halves(views, got, self.sel, "late")
        got = _run_phase(_swap_phase(send, 1), name="rs_swap_first_axis")
        keep, send = _add_first(keep, got, "late")
        got = _run_phase(_swap_phase(send, 2), name="rs_swap_second_axis")
        halves = _add_second(keep, got, self.sel, "late")
        full = _run_phase(_join_phase(halves + list(self.middle.halves)), name="rs_join_halves")
        return [f.reshape(-1, f.shape[-1]) for f in full]

    def phase(self, point, w, g):
        if point == "ffn1_fwd":
            return _gather_ici_phase(self.staged[:3]).then(_gather_d2d_phase(self.staged[:3]))
        if point == "inproj_fwd":
            return _gather_ici_phase(self.ffn2, only=1)
        if point == "attn_fwd":
            return _gather_ici_phase(self.ffn2, only=2)
        if point == "conv_fwd":
            return _gather_d2d_phase(self.ffn2)
        for red, at in ((self.early, self.EARLY_AT), (self.middle, self.MIDDLE_AT)):
            if point in at:
                return red.phase(at.index(point), g)
        return None

    def done(self, point, outs, w, g):
        if point == "ffn1_fwd":
            win, wout, taps = [_whole(b) for b in outs]
            w["win"] = win
            w["wout"] = wout.reshape(-1, wout.shape[-1])
            w["conv_w"] = taps.transpose(1, 0, 2).reshape(CONV_K + 1, D_CONV)[:CONV_K]
        elif point in ("inproj_fwd", "attn_fwd"):
            self.ffn2 = list(outs)
        elif point == "conv_fwd":
            w["wg2"], w["wu2"], w["wd2"] = [_whole(b) for b in outs]
        for red, at in ((self.early, self.EARLY_AT), (self.middle, self.MIDDLE_AT)):
            if point in at:
                red.done(at.index(point), outs)


class _Reduction:
    def __init__(self, names, tag, sel):
        self.names, self.tag, self.sel = names, tag, sel
        self.reduced = {}

    def phase(self, stage, g):
        if stage == 0:
            self.cols = [g[k].shape[-1] for k in self.names]
            self.views = [_half_view(g[k].reshape(N_CHIPS, -1, g[k].shape[-1])) for k in self.names]
            return _exchange_phase(self.views)
        if stage in (1, 2):
            return _swap_phase(self.send, stage)
        return _join_phase(self.halves)

    def done(self, stage, outs):
        if stage == 0:
            self.keep, self.send = _add_halves(self.views, outs, self.sel, self.tag)
        elif stage == 1:
            self.keep, self.send = _add_first(self.keep, outs, self.tag)
        elif stage == 2:
            self.halves = _add_second(self.keep, outs, self.sel, self.tag)
        else:
            for k, c, f in zip(self.names, self.cols, outs):
                self.reduced[k] = f.reshape(-1, c)


def _whole(buf):
    return buf.reshape(buf.shape[0], 2 * buf.shape[2], buf.shape[3])


def _allreduce_small(pack, *, name):
    rows = pack.shape[0]

    def body(p_ref, o_ref, buf_ref, send_sems, recv_sems):
        x, y, c = lax.axis_index("x"), lax.axis_index("y"), lax.axis_index("c")
        me = 4 * x + 2 * y + c
        buf_ref[me] = p_ref[...]
        cps = []
        for k in range(1, N_DEV):
            peer = tuple(1 - v if (k >> s) & 1 else v for v, s in ((x, 2), (y, 1), (c, 0)))
            cp = _remote(p_ref, buf_ref.at[me], send_sems.at[k - 1], recv_sems.at[k - 1], peer)
            cp.start()
            cps.append(cp)
        for k in range(1, N_DEV):
            src = 4 * (x ^ ((k >> 2) & 1)) + 2 * (y ^ ((k >> 1) & 1)) + (c ^ (k & 1))
            land = buf_ref.at[src]
            _remote(land, land, send_sems.at[k - 1], recv_sems.at[k - 1], (x, y, c)).wait_recv()
        acc = buf_ref[0]
        for d in range(1, N_DEV):
            acc = acc + buf_ref[d]
        o_ref[...] = acc
        for cp in cps:
            cp.wait_send()

    return pl.pallas_call(
        body, in_specs=[VMEM_SPEC], out_specs=VMEM_SPEC, out_shape=jax.ShapeDtypeStruct(pack.shape, F32),
        scratch_shapes=[pltpu.VMEM((N_DEV, rows, LANES), F32), pltpu.SemaphoreType.DMA((N_DEV - 1,)),
                        pltpu.SemaphoreType.DMA((N_DEV - 1,))], name=name)(pack)


SMALL = ("ffn1_norm", "mix_norm", "q_norm", "k_norm", "conv_b", "conv_ln_g", "conv_ln_b", "ffn2_norm", "conv_w")
BIG = ("ffn1_w_gate", "ffn1_w_up", "ffn1_w_down", "w_in", "w_out", "ffn2_w_gate", "ffn2_w_up", "ffn2_w_down")
TRANSPOSED = ("ffn1_w_gate", "ffn1_w_up", "ffn2_w_gate", "ffn2_w_up")
WEIGHTS = ("ffn1_norm", "ffn1_w_gate", "ffn1_w_up", "ffn1_w_down", "mix_norm", "w_in", "q_norm", "k_norm",
           "conv_w", "conv_b", "conv_ln_g", "conv_ln_b", "w_out", "ffn2_norm", "ffn2_w_gate", "ffn2_w_up",
           "ffn2_w_down")


def _pack(parts):
    rows = []
    for p in parts:
        flat = p.reshape(-1)
        tile = SUBLANES * LANES
        padded = -(-flat.shape[0] // tile) * tile
        rows.append(jnp.pad(flat, (0, padded - flat.shape[0])).reshape(-1, LANES))
    return jnp.concatenate(rows, axis=0)


def _unpack(pack, shapes):
    out, row = [], 0
    for shp in shapes:
        size = shp[0] * shp[1]
        tile = SUBLANES * LANES
        nrows = -(-size // tile) * SUBLANES
        out.append(pack[row:row + nrows].reshape(-1)[:size].reshape(shp))
        row += nrows
    return out


def kernel(x, ffn1_norm, ffn1_w_gate, ffn1_w_up, ffn1_w_down, mix_norm, w_in, q_norm, k_norm, conv_w, conv_b, conv_ln_g, conv_ln_b, w_out, ffn2_norm, ffn2_w_gate, ffn2_w_up, ffn2_w_down, loss_target, m_ffn1_norm, m_ffn1_w_gate, m_ffn1_w_up, m_ffn1_w_down, m_mix_norm, m_w_in, m_q_norm, m_k_norm, m_conv_w, m_conv_b, m_conv_ln_g, m_conv_ln_b, m_w_out, m_ffn2_norm, m_ffn2_w_gate, m_ffn2_w_up, m_ffn2_w_down, v_ffn1_norm, v_ffn1_w_gate, v_ffn1_w_up, v_ffn1_w_down, v_mix_norm, v_w_in, v_q_norm, v_k_norm, v_conv_w, v_conv_b, v_conv_ln_g, v_conv_ln_b, v_w_out, v_ffn2_norm, v_ffn2_w_gate, v_ffn2_w_up, v_ffn2_w_down):
    wts = dict(ffn1_norm=ffn1_norm, ffn1_w_gate=ffn1_w_gate[0], ffn1_w_up=ffn1_w_up[0], ffn1_w_down=ffn1_w_down[0],
               mix_norm=mix_norm, w_in=w_in[0], q_norm=q_norm, k_norm=k_norm, conv_w=conv_w[0], conv_b=conv_b,
               conv_ln_g=conv_ln_g, conv_ln_b=conv_ln_b, w_out=w_out[0], ffn2_norm=ffn2_norm,
               ffn2_w_gate=ffn2_w_gate[0], ffn2_w_up=ffn2_w_up[0], ffn2_w_down=ffn2_w_down[0])
    mom = dict(ffn1_norm=m_ffn1_norm, ffn1_w_gate=m_ffn1_w_gate[0], ffn1_w_up=m_ffn1_w_up[0], ffn1_w_down=m_ffn1_w_down[0],
               mix_norm=m_mix_norm, w_in=m_w_in[0], q_norm=m_q_norm, k_norm=m_k_norm, conv_w=m_conv_w[0], conv_b=m_conv_b,
               conv_ln_g=m_conv_ln_g, conv_ln_b=m_conv_ln_b, w_out=m_w_out[0], ffn2_norm=m_ffn2_norm,
               ffn2_w_gate=m_ffn2_w_gate[0], ffn2_w_up=m_ffn2_w_up[0], ffn2_w_down=m_ffn2_w_down[0])
    var = dict(ffn1_norm=v_ffn1_norm, ffn1_w_gate=v_ffn1_w_gate[0], ffn1_w_up=v_ffn1_w_up[0], ffn1_w_down=v_ffn1_w_down[0],
               mix_norm=v_mix_norm, w_in=v_w_in[0], q_norm=v_q_norm, k_norm=v_k_norm, conv_w=v_conv_w[0], conv_b=v_conv_b,
               conv_ln_g=v_conv_ln_g, conv_ln_b=v_conv_ln_b, w_out=v_w_out[0], ffn2_norm=v_ffn2_norm,
               ffn2_w_gate=v_ffn2_w_gate[0], ffn2_w_up=v_ffn2_w_up[0], ffn2_w_down=v_ffn2_w_down[0])
    chip = 2 * lax.axis_index("x") + lax.axis_index("y")
    for src in (wts, mom, var):
        for n in TRANSPOSED:
            src[n] = src[n].T

    taps = jnp.pad(wts["conv_w"], ((0, 1), (0, 0)))
    staged = _stage_shards([wts["ffn1_w_gate"], wts["ffn1_w_up"], wts["ffn1_w_down"], wts["w_in"], wts["w_out"], taps,
                            wts["ffn2_w_gate"], wts["ffn2_w_up"], wts["ffn2_w_down"]],
                           [BF16, BF16, BF16, BF16, BF16, F32, BF16, BF16, BF16], name="stage_shards")
    first = _run_phase(_gather_pipelined_phase(staged[:3]), name="gather_ffn1")
    wg1, wu1, wd1 = [_whole(b) for b in first]
    w = dict(ffn1_norm=ffn1_norm, mix_norm=mix_norm, ffn2_norm=ffn2_norm, q_norm=q_norm, k_norm=k_norm,
             conv_b=conv_b, conv_ln_g=conv_ln_g, conv_ln_b=conv_ln_b, wg1=wg1, wu1=wu1, wd1=wd1)
    overlap = _Overlap(staged[3:])
    loss_part, grad_x, g = _local_step(x, loss_target, w, overlap)

    grads, delta, new_m, new_v = {}, {}, {}, {}
    early = overlap.early.reduced
    grads.update(ffn2_w_gate=early["wg2"], ffn2_w_up=early["wu2"], ffn2_w_down=early["wd2"], w_out=early["wout"])
    grads.update(zip(("ffn1_w_gate", "ffn1_w_up", "ffn1_w_down", "w_in"),
                     overlap.finish_late([g["wg1"], g["wu1"], g["wd1"]])))

    small_shapes = [g[n].shape for n in SMALL] + [(SUBLANES, LANES)]
    red = _allreduce_small(_pack([g[n] for n in SMALL] + [loss_part]), name="allreduce_small")
    small = dict(zip(SMALL + ("loss",), _unpack(red, small_shapes)))
    loss = small["loss"][0, 0]
    small["conv_w"] = lax.dynamic_slice_in_dim(small["conv_w"], chip * LANES, LANES, axis=1)

    for tag, names in (("early", ("ffn2_w_gate", "ffn2_w_up", "ffn2_w_down", "w_out")),
                       ("late", ("ffn1_w_gate", "ffn1_w_up", "ffn1_w_down", "w_in"))):
        d, m, v, _ = _adamw_many([wts[n] for n in names], [grads[n] for n in names], [mom[n] for n in names],
                                 [var[n] for n in names], name=f"adamw_{tag}")
        for dst, vals in ((delta, d), (new_m, m), (new_v, v)):
            dst.update(zip(names, vals))
    shapes = [wts[n].shape for n in SMALL]
    packs = [_pack([src[n] for n in SMALL]) for src in (wts, small, mom, var)]
    outs = _adamw(*packs, name="adamw_small")
    for dst, pk in zip((delta, new_m, new_v), outs):
        dst.update(zip(SMALL, _unpack(pk, shapes)))
    for n in SMALL:
        grads[n] = small[n]

    def shaped(d, n):
        v = d[n].T if n in TRANSPOSED else d[n]
        return v.reshape((1,) + v.shape) if n in BIG or n == "conv_w" else v

    return (loss, grad_x, *[shaped(grads, n) for n in WEIGHTS], *[shaped(delta, n) for n in WEIGHTS],
            *[shaped(new_m, n) for n in WEIGHTS], *[shaped(new_v, n) for n in WEIGHTS])
```

```python
import functools

import jax
import jax.numpy as jnp
from jax import lax
from jax.experimental import pallas as pl
from jax.experimental.pallas import tpu as pltpu

F32 = jnp.float32
BF16 = jnp.bfloat16

EPS = 1e-6
HEADS = 8
HEAD_DIM = 64
D_ATTN = HEADS * HEAD_DIM
D_CONV = 512
CONV_K = 31
QBLK = 128
N_PATTERNS = 3
DILATIONS = (1, 4, 16)
LANES = 128
NEG = -1e30

ADAM_LR = 0.001
ADAM_B1 = 0.9
ADAM_B2 = 0.999
ADAM_EPS = 1e-08
ADAM_WD = 0.01
ADAM_STEP = 10

VMEM_LIMIT = 56 * 1024 * 1024
MESH = pl.DeviceIdType.MESH

NT_DIMS = (((1,), (1,)), ((), ()))
TN_DIMS = (((0,), (0,)), ((), ()))


def _params(*sem):
    return pltpu.CompilerParams(dimension_semantics=sem, vmem_limit_bytes=VMEM_LIMIT)


def _dot(a, b):
    return jnp.dot(a, b, preferred_element_type=F32)


def _dot_nt(a, b):
    return lax.dot_general(a, b, NT_DIMS, preferred_element_type=F32)


def _dot_tn(a, b):
    return lax.dot_general(a, b, TN_DIMS, preferred_element_type=F32)


def _sigmoid(x):
    return 1.0 / (1.0 + jnp.exp(-x))


def _seg_mean(v, e_ref, width):
    hi = v.astype(BF16)
    lo = (v - hi.astype(F32)).astype(BF16)
    e = e_ref[...]
    return (_dot(hi, e) + _dot(lo, e)) * (1.0 / width)


def _seg_matrix(n):
    i = jnp.arange(n)
    return (i[:, None] // HEAD_DIM == i[None, :] // HEAD_DIM).astype(BF16)


ANY = pl.BlockSpec(memory_space=pl.ANY)
DMA_SEMS = pltpu.SemaphoreType.DMA


class _Phase:
    def __init__(self, ins, outs, aliases, nsem, copies):
        self.ins, self.outs, self.aliases = list(ins), list(outs), dict(aliases)
        self.stages = [(nsem, copies)]

    def then(self, other):
        self.stages = self.stages + other.stages
        return self

    @property
    def nsem(self):
        return sum(n for n, _ in self.stages)

    def _copies(self, k, in_refs, out_refs, send_sems, recv_sems):
        base = sum(n for n, _ in self.stages[:k])
        return self.stages[k][1](in_refs, out_refs, lambda i: (send_sems.at[base + i], recv_sems.at[base + i]))

    def start(self, k, *refs):
        for cp in self._copies(k, *refs)[0]:
            cp.start()

    def finish(self, k, *refs):
        starts, arrivals = self._copies(k, *refs)
        for cp in arrivals:
            cp.wait_recv()
        for cp in starts:
            cp.wait_send()


def _run_phase(phase, *, name):
    n_in, n_out = len(phase.ins), len(phase.outs)

    def body(*refs):
        ins, outs = refs[:n_in], refs[n_in:n_in + n_out]
        send_sems, recv_sems = refs[n_in + n_out:]
        for k in range(len(phase.stages)):
            phase.start(k, ins, outs, send_sems, recv_sems)
            phase.finish(k, ins, outs, send_sems, recv_sems)

    return pl.pallas_call(
        body, in_specs=[ANY] * n_in, out_specs=[ANY] * n_out, out_shape=phase.outs,
        input_output_aliases=phase.aliases,
        scratch_shapes=[DMA_SEMS((phase.nsem,)), DMA_SEMS((phase.nsem,))], name=name)(*phase.ins)


def _call(body, *, grid, in_specs, out_specs, out_shape, scratch_shapes=(), sem, name, args, phase=None):
    in_specs, out_specs, out_shape = list(in_specs), list(out_specs), list(out_shape)
    scratch_shapes = list(scratch_shapes)
    if phase is None:
        return pl.pallas_call(body, grid=grid, in_specs=in_specs, out_specs=out_specs, out_shape=out_shape,
                              scratch_shapes=scratch_shapes, compiler_params=_params(*sem), name=name)(*args)
    n_in, n_out, n_scr = len(in_specs), len(out_specs), len(scratch_shapes)
    p_in, p_out = len(phase.ins), len(phase.outs)

    def hosted(*refs):
        ins, pins = refs[:n_in], refs[n_in:n_in + p_in]
        o0 = n_in + p_in
        outs, pouts = refs[o0:o0 + n_out], refs[o0 + n_out:o0 + n_out + p_out]
        s0 = o0 + n_out + p_out
        scr = refs[s0:s0 + n_scr]
        send_sems, recv_sems = refs[s0 + n_scr:]
        step = 0
        for d, n in enumerate(grid):
            step = step * n + pl.program_id(d)
        nsteps = functools.reduce(lambda a, b: a * b, grid)
        nstages = len(phase.stages)
        comm_refs = (pins, pouts, send_sems, recv_sems)

        for k in range(nstages):
            @pl.when(step == (k * nsteps) // nstages)
            def _(k=k):
                if k > 0:
                    phase.finish(k - 1, *comm_refs)
                phase.start(k, *comm_refs)

        body(*ins, *outs, *scr)

        @pl.when(step == nsteps - 1)
        def _():
            phase.finish(nstages - 1, *comm_refs)

    res = pl.pallas_call(
        hosted, grid=grid, in_specs=in_specs + [ANY] * p_in, out_specs=out_specs + [ANY] * p_out,
        out_shape=out_shape + phase.outs,
        input_output_aliases={n_in + i: n_out + o for i, o in phase.aliases.items()},
        scratch_shapes=scratch_shapes + [DMA_SEMS((phase.nsem,)), DMA_SEMS((phase.nsem,))],
        compiler_params=_params(*sem), name=name)(*args, *phase.ins)
    return res[:n_out], res[n_out:]


ROW_CHUNK = 256


def _ffn_fwd(x, gain, wg, wu, wd, tgt, *, tm, name, phase=None):
    T, D = x.shape
    NS, Fs, _ = wg.shape
    with_loss = tgt is not None

    def body(*refs):
        if with_loss:
            x_ref, g_ref, wg_ref, wu_ref, wd_ref, t_ref, h_ref, n_ref, G_ref, U_ref, loss_ref, acc_ref = refs
        else:
            x_ref, g_ref, wg_ref, wu_ref, wd_ref, h_ref, n_ref, G_ref, U_ref, acc_ref = refs
        i = pl.program_id(0)
        j = pl.program_id(1)

        @pl.when(j == 0)
        def _():
            xv = x_ref[...]
            r = lax.rsqrt(jnp.mean(xv * xv, axis=-1, keepdims=True) + EPS)
            n_ref[...] = (xv * r * g_ref[...]).astype(BF16)
            acc_ref[...] = jnp.zeros_like(acc_ref)

        n = n_ref[...]
        G = _dot_nt(n, wg_ref[...])
        U = _dot_nt(n, wu_ref[...])
        G_ref[...] = G.astype(BF16)
        U_ref[...] = U.astype(BF16)
        A = (G * _sigmoid(G) * U).astype(BF16)
        acc_ref[...] += _dot(A, wd_ref[...])

        @pl.when(j == NS - 1)
        def _():
            h = x_ref[...] + 0.5 * acc_ref[...]
            if with_loss:
                e = h - t_ref[...]
                h_ref[...] = e * (1.0 / D)

                @pl.when(i == 0)
                def _():
                    loss_ref[...] = jnp.zeros_like(loss_ref)

                loss_ref[...] += jnp.sum(e * e) * (0.5 / D)
            else:
                h_ref[...] = h

    tok = pl.BlockSpec((tm, D), lambda i, j: (i, 0))
    in_specs = [tok, pl.BlockSpec((1, D), lambda i, j: (0, 0)),
                pl.BlockSpec((None, Fs, D), lambda i, j: (j, 0, 0)),
                pl.BlockSpec((None, Fs, D), lambda i, j: (j, 0, 0)),
                pl.BlockSpec((None, Fs, D), lambda i, j: (j, 0, 0))]
    args = [x, gain, wg, wu, wd]
    act = pl.BlockSpec((None, tm, Fs), lambda i, j: (j, i, 0))
    out_shape = [jax.ShapeDtypeStruct((T, D), F32), jax.ShapeDtypeStruct((T, D), BF16),
                 jax.ShapeDtypeStruct((NS, T, Fs), BF16), jax.ShapeDtypeStruct((NS, T, Fs), BF16)]
    out_specs = [tok, tok, act, act]
    if with_loss:
        in_specs.append(tok)
        args.append(tgt)
        out_shape.append(jax.ShapeDtypeStruct((8, LANES), F32))
        out_specs.append(pl.BlockSpec((8, LANES), lambda i, j: (0, 0)))
    return _call(body, grid=(T // tm, NS), in_specs=in_specs, out_specs=out_specs, out_shape=out_shape,
                 scratch_shapes=[pltpu.VMEM((tm, D), F32)], sem=("arbitrary", "arbitrary"), name=name,
                 args=args, phase=phase)


def _rms_bwd(xv, gain, dn):
    r = lax.rsqrt(jnp.mean(xv * xv, axis=-1, keepdims=True) + EPS)
    xhat = xv * r
    dxh = dn * gain
    dx = r * (dxh - xhat * jnp.mean(dxh * xhat, axis=-1, keepdims=True))
    dg = jnp.sum(dn * xhat, axis=0, keepdims=True)
    return dx, dg


def _ffn_bwd_act(dh, x, gain, G, U, wg, wu, wd, *, tm, name, phase=None):
    T, D = x.shape
    NS, Fs, _ = wg.shape

    def body(dh_ref, x_ref, g_ref, G_ref, U_ref, wg_ref, wu_ref, wd_ref,
             dG_ref, dU_ref, A_ref, dy_ref, dx_ref, dg_ref, acc_ref):
        i = pl.program_id(0)
        j = pl.program_id(1)

        @pl.when(j == 0)
        def _():
            dy_ref[...] = (0.5 * dh_ref[...]).astype(BF16)
            acc_ref[...] = jnp.zeros_like(acc_ref)

        @pl.when((i == 0) & (j == 0))
        def _():
            dg_ref[...] = jnp.zeros_like(dg_ref)

        nchunks = tm // ROW_CHUNK
        dA, dGU = {}, {}
        for step in range(nchunks + 2):
            if step < nchunks:
                rows = slice(step * ROW_CHUNK, (step + 1) * ROW_CHUNK)
                dA[step] = _dot_nt(dy_ref[rows, :], wd_ref[...])
            if 1 <= step <= nchunks:
                k = step - 1
                rows = slice(k * ROW_CHUNK, (k + 1) * ROW_CHUNK)
                Gv = G_ref[rows, :].astype(F32)
                Uv = U_ref[rows, :].astype(F32)
                sig = _sigmoid(Gv)
                s = Gv * sig
                dG = (dA[k] * Uv * (sig * (1.0 + Gv * (1.0 - sig)))).astype(BF16)
                dU = (dA.pop(k) * s).astype(BF16)
                dG_ref[rows, :] = dG
                dU_ref[rows, :] = dU
                A_ref[rows, :] = (s * Uv).astype(BF16)
                dGU[k] = (dG, dU)
            if 2 <= step:
                k = step - 2
                rows = slice(k * ROW_CHUNK, (k + 1) * ROW_CHUNK)
                dG, dU = dGU.pop(k)
                acc_ref[rows, :] += _dot(dG, wg_ref[...]) + _dot(dU, wu_ref[...])

        @pl.when(j == NS - 1)
        def _():
            dx, dg = _rms_bwd(x_ref[...], g_ref[...], acc_ref[...])
            dx_ref[...] = dh_ref[...] + dx
            dg_ref[...] += dg

    tok = pl.BlockSpec((tm, D), lambda i, j: (i, 0))
    act = pl.BlockSpec((None, tm, Fs), lambda i, j: (j, i, 0))
    vec = pl.BlockSpec((1, D), lambda i, j: (0, 0))
    return _call(
        body, grid=(T // tm, NS),
        in_specs=[tok, tok, vec, act, act,
                  pl.BlockSpec((None, Fs, D), lambda i, j: (j, 0, 0)),
                  pl.BlockSpec((None, Fs, D), lambda i, j: (j, 0, 0)),
                  pl.BlockSpec((None, Fs, D), lambda i, j: (j, 0, 0))],
        out_specs=[act, act, act, tok, tok, vec],
        out_shape=[jax.ShapeDtypeStruct((NS, T, Fs), BF16)] * 3
        + [jax.ShapeDtypeStruct((T, D), BF16), jax.ShapeDtypeStruct((T, D), F32),
           jax.ShapeDtypeStruct((1, D), F32)],
        scratch_shapes=[pltpu.VMEM((tm, D), F32)],
        sem=("arbitrary", "arbitrary"), name=name, args=(dh, x, gain, G, U, wg, wu, wd), phase=phase)


def _ffn_bwd_w(n, dy, dG, dU, A, *, tk, name, phase=None):
    T, D = n.shape
    NS, _, Fs = dG.shape

    def body(n_ref, dy_ref, dG_ref, dU_ref, A_ref, wg_ref, wu_ref, wd_ref):
        @pl.when(pl.program_id(1) == 0)
        def _():
            wg_ref[...] = jnp.zeros_like(wg_ref)
            wu_ref[...] = jnp.zeros_like(wu_ref)
            wd_ref[...] = jnp.zeros_like(wd_ref)

        nv = n_ref[...]
        wg_ref[...] += _dot_tn(dG_ref[...], nv)
        wu_ref[...] += _dot_tn(dU_ref[...], nv)
        wd_ref[...] += _dot_tn(A_ref[...], dy_ref[...])

    tok = pl.BlockSpec((tk, D), lambda j, k: (k, 0))
    act = pl.BlockSpec((None, tk, Fs), lambda j, k: (j, k, 0))
    return _call(
        body, grid=(NS, T // tk), in_specs=[tok, tok, act, act, act],
        out_specs=[pl.BlockSpec((None, Fs, D), lambda j, k: (j, 0, 0))] * 3,
        out_shape=[jax.ShapeDtypeStruct((NS, Fs, D), F32)] * 3,
        sem=("arbitrary", "arbitrary"), name=name, args=(n, dy, dG, dU, A), phase=phase)


def _inproj_fwd(h, gain, win, *, tm, name, phase=None):
    T, D = h.shape
    NS, _, Cs = win.shape

    def body(h_ref, g_ref, w_ref, u_ref, n_ref):
        @pl.when(pl.program_id(1) == 0)
        def _():
            xv = h_ref[...]
            r = lax.rsqrt(jnp.mean(xv * xv, axis=-1, keepdims=True) + EPS)
            n_ref[...] = (xv * r * g_ref[...]).astype(BF16)

        u_ref[...] = _dot(n_ref[...], w_ref[...])

    tok = pl.BlockSpec((tm, D), lambda i, j: (i, 0))
    return _call(
        body, grid=(T // tm, NS),
        in_specs=[tok, pl.BlockSpec((1, D), lambda i, j: (0, 0)),
                  pl.BlockSpec((None, D, Cs), lambda i, j: (j, 0, 0))],
        out_specs=[pl.BlockSpec((tm, Cs), lambda i, j: (i, j)), tok],
        out_shape=[jax.ShapeDtypeStruct((T, NS * Cs), F32), jax.ShapeDtypeStruct((T, D), BF16)],
        sem=("arbitrary", "arbitrary"), name=name, args=(h, gain, win), phase=phase)


def _inproj_bwd_act(du, dh, h, gain, win, *, tm, name, phase=None):
    T, D = h.shape
    NS, _, Cs = win.shape

    def body(du_ref, dh_ref, h_ref, g_ref, w_ref, dx_ref, dg_ref, acc_ref):
        i = pl.program_id(0)
        j = pl.program_id(1)

        @pl.when(j == 0)
        def _():
            acc_ref[...] = jnp.zeros_like(acc_ref)

        @pl.when((i == 0) & (j == 0))
        def _():
            dg_ref[...] = jnp.zeros_like(dg_ref)

        acc_ref[...] += _dot_nt(du_ref[...], w_ref[...])

        @pl.when(j == NS - 1)
        def _():
            dx, dg = _rms_bwd(h_ref[...], g_ref[...], acc_ref[...])
            dx_ref[...] = dh_ref[...] + dx
            dg_ref[...] += dg

    tok = pl.BlockSpec((tm, D), lambda i, j: (i, 0))
    vec = pl.BlockSpec((1, D), lambda i, j: (0, 0))
    return _call(
        body, grid=(T // tm, NS),
        in_specs=[pl.BlockSpec((tm, Cs), lambda i, j: (i, j)), tok, tok, vec,
                  pl.BlockSpec((None, D, Cs), lambda i, j: (j, 0, 0))],
        out_specs=[tok, vec],
        out_shape=[jax.ShapeDtypeStruct((T, D), F32), jax.ShapeDtypeStruct((1, D), F32)],
        scratch_shapes=[pltpu.VMEM((tm, D), F32)],
        sem=("arbitrary", "arbitrary"), name=name, args=(du, dh, h, gain, win), phase=phase)


def _inproj_bwd_w(n, du, ns, *, tk, name, phase=None):
    T, D = n.shape
    Cs = du.shape[1] // ns

    def body(n_ref, du_ref, w_ref):
        @pl.when(pl.program_id(1) == 0)
        def _():
            w_ref[...] = jnp.zeros_like(w_ref)

        w_ref[...] += _dot_tn(n_ref[...], du_ref[...])

    return _call(
        body, grid=(ns, T // tk),
        in_specs=[pl.BlockSpec((tk, D), lambda j, k: (k, 0)), pl.BlockSpec((tk, Cs), lambda j, k: (k, j))],
        out_specs=[pl.BlockSpec((None, D, Cs), lambda j, k: (j, 0, 0))],
        out_shape=[jax.ShapeDtypeStruct((ns, D, Cs), F32)],
        sem=("arbitrary", "arbitrary"), name=name, args=(n, du), phase=phase)


STRIDE = 4


def _permute(src_ref, tmp_ref, put):
    S = src_ref.shape[0]
    L4, L16 = S // STRIDE, S // (STRIDE * STRIDE)
    put(0, 0, src_ref[...])
    for r0 in range(STRIDE):
        v = src_ref[pl.ds(r0, L4, stride=STRIDE), :]
        put(1, r0 * L4, v)
        tmp_ref[r0 * L4:(r0 + 1) * L4, :] = v
    for r0 in range(STRIDE):
        for r1 in range(STRIDE):
            put(2, (r1 * STRIDE + r0) * L16, tmp_ref[pl.ds(r0 * L4 + r1, L16, stride=STRIDE), :])


def _permute_out(src_ref, tmp_ref, out_ref, cast):
    for cc in range(src_ref.shape[0]):
        cols = slice(cc * LANES, (cc + 1) * LANES)

        def put(p, row0, v, cols=cols):
            out_ref[p, row0:row0 + v.shape[0], cols] = v.astype(cast)

        _permute(src_ref.at[cc], tmp_ref, put)


def _unpermute_in(get_block, dst_ref, tmp_ref, p, S):
    L4, L16 = S // STRIDE, S // (STRIDE * STRIDE)
    if p == 0:
        dst_ref[...] = get_block(0, S)
        return
    if p == 1:
        for r0 in range(STRIDE):
            dst_ref[pl.ds(r0, L4, stride=STRIDE), :] = get_block(r0 * L4, L4)
        return
    for r0 in range(STRIDE):
        for r1 in range(STRIDE):
            tmp_ref[pl.ds(r0 * L4 + r1, L16, stride=STRIDE), :] = get_block((r1 * STRIDE + r0) * L16, L16)
    for r0 in range(STRIDE):
        dst_ref[pl.ds(r0, L4, stride=STRIDE), :] = tmp_ref[r0 * L4:(r0 + 1) * L4, :]


def _qkv_prep(u, gains, B, S, *, name):
    emat = _seg_matrix(D_ATTN)

    def body(u_ref, g_ref, e_ref, out_ref, scr_ref, tmp_ref):
        c = pl.program_id(1)
        xv = u_ref[...]
        ms = _seg_mean(xv * xv, e_ref, HEAD_DIM)
        r = jnp.where(c < 2, lax.rsqrt(ms + EPS), 1.0)
        yv = xv * r * g_ref[...]
        for cc in range(4):
            scr_ref[cc] = yv[:, cc * LANES:(cc + 1) * LANES]
        _permute_out(scr_ref, tmp_ref, out_ref, BF16)

    return pl.pallas_call(
        body, grid=(B, 3),
        in_specs=[pl.BlockSpec((S, D_ATTN), lambda b, c: (b, c)),
                  pl.BlockSpec((None, 1, D_ATTN), lambda b, c: (c, 0, 0)),
                  pl.BlockSpec((D_ATTN, D_ATTN), lambda b, c: (0, 0))],
        out_specs=pl.BlockSpec((None, N_PATTERNS, None, S, D_ATTN), lambda b, c: (c, 0, b, 0, 0)),
        out_shape=jax.ShapeDtypeStruct((3, N_PATTERNS, B, S, D_ATTN), BF16),
        scratch_shapes=[pltpu.VMEM((4, S, LANES), F32), pltpu.VMEM((S, LANES), F32)],
        compiler_params=_params("arbitrary", "arbitrary"), name=name)(u, gains, emat)


def _band_mask(p, b):
    nblk = jnp.right_shift(16, 2 * p)
    has_prev = jnp.bitwise_and(b, nblk - 1) != 0
    qi = lax.broadcasted_iota(jnp.int32, (QBLK, 2 * QBLK), 0)
    ci = lax.broadcasted_iota(jnp.int32, (QBLK, 2 * QBLK), 1)
    dist = QBLK + qi - ci
    return (dist >= 0) & (dist <= QBLK) & (has_prev | (ci >= QBLK))


def _first_head(rows):
    return lax.broadcasted_iota(jnp.int32, (rows, LANES), 1) < HEAD_DIM


def _split_heads(pair):
    first = _first_head(pair.shape[0])
    zero = jnp.zeros_like(pair)
    return jnp.concatenate([jnp.where(first, pair, zero), jnp.where(first, zero, pair)], axis=0)


def _merge_heads(col_a, col_b):
    rows = col_a.shape[0]
    return jnp.where(_first_head(rows), jnp.broadcast_to(col_a, (rows, LANES)), jnp.broadcast_to(col_b, (rows, LANES)))


QB_FWD = 8
QB_BWD = 4


def _attn_fwd(qkv, *, name, phase=None):
    QB = QB_FWD
    nb = qkv.shape[2]

    def body(q_ref, kp_ref, kc_ref, vp_ref, vc_ref, o_ref, lse_ref):
        kall = jnp.concatenate([kp_ref[...]] + [kc_ref[t] for t in range(QB)], axis=0)
        vall = jnp.concatenate([vp_ref[...]] + [vc_ref[t] for t in range(QB)], axis=0)
        masks = []
        for t in range(QB):
            mask = _band_mask(pl.program_id(0), QB * pl.program_id(1) + t)
            masks.append(jnp.concatenate([mask, mask], axis=0))
        units = [(t, hp) for t in range(QB) for hp in range(HEADS // 2)]
        scores, probs = {}, {}
        for step in range(len(units) + 2):
            if step < len(units):
                t, hp = units[step]
                cols = slice(hp * LANES, (hp + 1) * LANES)
                scores[step] = _dot_nt(_split_heads(q_ref[t, :, cols]), kall[t * QBLK:(t + 2) * QBLK, cols])
            if 1 <= step <= len(units):
                t, hp = units[step - 1]
                cols = slice(hp * LANES, (hp + 1) * LANES)
                s = jnp.where(masks[t], scores.pop(step - 1), NEG)
                m = jnp.max(s, axis=-1, keepdims=True)
                e = jnp.exp(s - m)
                l = jnp.sum(e, axis=-1, keepdims=True)
                probs[step - 1] = (e * (1.0 / l)).astype(BF16)
                lse = m + jnp.log(l)
                lse_ref[t, :, cols] = _merge_heads(lse[:QBLK], lse[QBLK:])
            if 2 <= step:
                t, hp = units[step - 2]
                cols = slice(hp * LANES, (hp + 1) * LANES)
                pr = probs.pop(step - 2)
                o_ref[t, :, cols] = _dot(jnp.concatenate([pr[:QBLK], pr[QBLK:]], axis=1),
                                         _split_heads(vall[t * QBLK:(t + 2) * QBLK, cols]))

    cur = lambda which: pl.BlockSpec((None, None, QB, QBLK, D_ATTN), lambda p, i: (which, p, i, 0, 0))
    prev = lambda which: pl.BlockSpec((None, None, None, QBLK, D_ATTN),
                                      lambda p, i: (which, p, jnp.maximum(QB * i - 1, 0), 0, 0))
    out = pl.BlockSpec((None, QB, QBLK, D_ATTN), lambda p, i: (p, i, 0, 0))
    return _call(
        body, grid=(N_PATTERNS, nb // QB), in_specs=[cur(0), prev(1), cur(1), prev(2), cur(2)], out_specs=[out, out],
        out_shape=[jax.ShapeDtypeStruct((N_PATTERNS, nb, QBLK, D_ATTN), F32)] * 2,
        sem=("arbitrary", "arbitrary"), name=name, args=(qkv, qkv, qkv, qkv, qkv), phase=phase)


def _attn_combine(o3, lse3, B, S, *, name):
    def body(o_ref, l_ref, a_ref, lt_ref, so_ref, sl_ref, tmp_ref):
        for p in range(N_PATTERNS):
            _unpermute_in(lambda r0, n, p=p: o_ref[p, pl.ds(r0, n), :], so_ref.at[p], tmp_ref, p, S)
            _unpermute_in(lambda r0, n, p=p: l_ref[p, pl.ds(r0, n), :], sl_ref.at[p], tmp_ref, p, S)
        l0, l1, l2 = sl_ref[0], sl_ref[1], sl_ref[2]
        m = jnp.maximum(jnp.maximum(l0, l1), l2)
        w0, w1, w2 = jnp.exp(l0 - m), jnp.exp(l1 - m), jnp.exp(l2 - m)
        tot = w0 + w1 + w2
        a_ref[...] = (w0 * so_ref[0] + w1 * so_ref[1] + w2 * so_ref[2]) / tot
        lt_ref[...] = m + jnp.log(tot)

    o3 = o3.reshape(N_PATTERNS, B, S, D_ATTN)
    lse3 = lse3.reshape(N_PATTERNS, B, S, D_ATTN)
    inp = pl.BlockSpec((N_PATTERNS, None, S, LANES), lambda b, c: (0, b, 0, c))
    out = pl.BlockSpec((S, LANES), lambda b, c: (b, c))
    return pl.pallas_call(
        body, grid=(B, D_ATTN // LANES), in_specs=[inp, inp], out_specs=[out, out],
        out_shape=[jax.ShapeDtypeStruct((B * S, D_ATTN), F32)] * 2,
        scratch_shapes=[pltpu.VMEM((N_PATTERNS, S, LANES), F32)] * 2 + [pltpu.VMEM((S, LANES), F32)],
        compiler_params=_params("arbitrary", "arbitrary"), name=name)(o3, lse3)


STAT_D = 8


def _attn_bwd_prep(dattn, attn, lse, B, S, *, name):
    emat = _seg_matrix(LANES)
    ncc = D_ATTN // LANES

    def body(da_ref, a_ref, l_ref, e_ref, do_ref, st_ref, scr_ref, nat_ref, tmp_ref):
        cc = pl.program_id(1)
        da = da_ref[...]
        dsum = _seg_mean(da * a_ref[...], e_ref, 1.0)
        scr_ref[...] = da

        def put_do(p, row0, v):
            do_ref[p, row0:row0 + v.shape[0], :] = v.astype(BF16)

        _permute(scr_ref, tmp_ref, put_do)

        lane = lax.broadcasted_iota(jnp.int32, (S, LANES), 1)
        h0 = 2 * cc
        vals = ((h0, l_ref[:, 0:1]), (h0 + 1, l_ref[:, HEAD_DIM:HEAD_DIM + 1]),
                (STAT_D + h0, dsum[:, 0:1]), (STAT_D + h0 + 1, dsum[:, HEAD_DIM:HEAD_DIM + 1]))
        tile = jnp.where(cc == 0, 0.0, nat_ref[...])
        for at, col in vals:
            tile = jnp.where(lane == at, col, tile)
        nat_ref[...] = tile

        @pl.when(cc == ncc - 1)
        def _():
            def put_st(p, row0, v):
                st_ref[p, row0:row0 + v.shape[0], :] = v

            _permute(nat_ref, tmp_ref, put_st)

    inp = pl.BlockSpec((S, LANES), lambda b, c: (b, c))
    return pl.pallas_call(
        body, grid=(B, ncc),
        in_specs=[inp, inp, inp, pl.BlockSpec((LANES, LANES), lambda b, c: (0, 0))],
        out_specs=[pl.BlockSpec((N_PATTERNS, None, S, LANES), lambda b, c: (0, b, 0, c)),
                   pl.BlockSpec((N_PATTERNS, None, S, LANES), lambda b, c: (0, b, 0, 0))],
        out_shape=[jax.ShapeDtypeStruct((N_PATTERNS, B, S, D_ATTN), BF16),
                   jax.ShapeDtypeStruct((N_PATTERNS, B, S, LANES), F32)],
        scratch_shapes=[pltpu.VMEM((S, LANES), F32)] * 3,
        compiler_params=_params("arbitrary", "arbitrary"), name=name)(dattn, attn, lse, emat)


def _attn_bwd(qkv, do3, st3, *, name, phase=None):
    QB = QB_BWD
    nb = qkv.shape[2]
    ngroups = nb // QB

    def body(q_ref, kp_ref, kc_ref, vp_ref, vc_ref, do_ref, st_ref, out_ref, carry_ref):
        p = pl.program_id(0)
        i = pl.program_id(1)

        @pl.when((p == 0) & (i == 0))
        def _():
            carry_ref[...] = jnp.zeros_like(carry_ref)

        kall = jnp.concatenate([kp_ref[...]] + [kc_ref[t] for t in range(QB)], axis=0)
        vall = jnp.concatenate([vp_ref[...]] + [vc_ref[t] for t in range(QB)], axis=0)

        masks = []
        for t in range(QB):
            mask = _band_mask(p, QB * i + t) & (i < ngroups)
            masks.append(jnp.concatenate([mask, mask], axis=0))

        def operands(t, hp):
            cols = slice(hp * LANES, (hp + 1) * LANES)
            kh, vh = kall[t * QBLK:(t + 2) * QBLK, cols], vall[t * QBLK:(t + 2) * QBLK, cols]
            return kh, vh, _split_heads(q_ref[t, :, cols]), _split_heads(do_ref[t, :, cols])

        def stage_scores(t, hp):
            kh, vh, q2, do2 = operands(t, hp)
            return _dot_nt(q2, kh), _dot_nt(do2, vh)

        def stage_softmax(t, hp, s, dp):
            h0, h1 = 2 * hp, 2 * hp + 1
            lse = jnp.concatenate([st_ref[t, :, h0:h0 + 1], st_ref[t, :, h1:h1 + 1]], axis=0)
            dsum = jnp.concatenate([st_ref[t, :, STAT_D + h0:STAT_D + h0 + 1],
                                    st_ref[t, :, STAT_D + h1:STAT_D + h1 + 1]], axis=0)
            pr = jnp.where(masks[t], jnp.exp(s - lse), 0.0)
            return (pr * (dp - dsum)).astype(BF16), pr.astype(BF16)

        def stage_grads(t, hp, ds, prb):
            cols = slice(hp * LANES, (hp + 1) * LANES)
            kh, vh, q2, do2 = operands(t, hp)
            dq = _dot(jnp.concatenate([ds[:QBLK], ds[QBLK:]], axis=1), _split_heads(kh))
            dk, dv = _dot_tn(ds, q2), _dot_tn(prb, do2)
            if t == 0:
                for c in range(3):
                    for tt in range(QB):
                        v = carry_ref[c, tt, :, cols]
                        if tt == QB - 1 and c > 0:
                            v = v + (dk if c == 1 else dv)[:QBLK]
                        out_ref[c, tt, :, cols] = v.astype(BF16)
            else:
                carry_ref[1, t - 1, :, cols] += dk[:QBLK]
                carry_ref[2, t - 1, :, cols] += dv[:QBLK]
            carry_ref[0, t, :, cols] = dq
            carry_ref[1, t, :, cols] = dk[QBLK:]
            carry_ref[2, t, :, cols] = dv[QBLK:]

        units = [(t, hp) for hp in range(HEADS // 2) for t in range(QB)]
        scores, probs = {}, {}
        for step in range(len(units) + 2):
            if step < len(units):
                scores[step] = stage_scores(*units[step])
            if 1 <= step <= len(units):
                probs[step - 1] = stage_softmax(*units[step - 1], *scores.pop(step - 1))
            if 2 <= step:
                stage_grads(*units[step - 2], *probs.pop(step - 2))

    group = lambda i: jnp.minimum(i, ngroups - 1)
    cur = lambda which: pl.BlockSpec((None, None, QB, QBLK, D_ATTN), lambda p, i: (which, p, group(i), 0, 0))
    prev = lambda which: pl.BlockSpec((None, None, None, QBLK, D_ATTN),
                                      lambda p, i: (which, p, jnp.maximum(QB * group(i) - 1, 0), 0, 0))
    aux = lambda lanes: pl.BlockSpec((None, QB, QBLK, lanes), lambda p, i: (p, group(i), 0, 0))
    return _call(
        body, grid=(N_PATTERNS, ngroups + 1),
        in_specs=[cur(0), prev(1), cur(1), prev(2), cur(2), aux(D_ATTN), aux(LANES)],
        out_specs=[pl.BlockSpec((3, None, QB, QBLK, D_ATTN), lambda p, i: (0, p, jnp.maximum(i - 1, 0), 0, 0))],
        out_shape=[jax.ShapeDtypeStruct((3, N_PATTERNS, nb, QBLK, D_ATTN), BF16)],
        scratch_shapes=[pltpu.VMEM((3, QB, QBLK, D_ATTN), F32)],
        sem=("arbitrary", "arbitrary"), name=name, args=(qkv, qkv, qkv, qkv, qkv, do3, st3), phase=phase)


def _attn_grad_combine(cur, u, gains, B, S, *, name, phase=None):
    emat = _seg_matrix(LANES)

    def body(cur_ref, u_ref, g_ref, e_ref, du_ref, dg_ref, scr_ref, tmp_ref):
        c = pl.program_id(0)
        b = pl.program_id(2)
        for p in range(N_PATTERNS):
            _unpermute_in(lambda r0, n, p=p: cur_ref[p, pl.ds(r0, n), :].astype(F32), scr_ref.at[p], tmp_ref, p, S)
        dy = scr_ref[0] + scr_ref[1] + scr_ref[2]
        xv = u_ref[...]
        gain = g_ref[...]
        ms = _seg_mean(xv * xv, e_ref, HEAD_DIM)
        r = lax.rsqrt(ms + EPS)
        xhat = xv * r
        dxh = dy * gain
        dx = r * (dxh - xhat * _seg_mean(dxh * xhat, e_ref, HEAD_DIM))
        du_ref[...] = jnp.where(c < 2, dx, dy).astype(BF16)

        @pl.when((b == 0))
        def _():
            dg_ref[...] = jnp.zeros_like(dg_ref)

        dg_ref[...] += jnp.sum(dy * xhat, axis=0, keepdims=True)

    cur = cur.reshape(3, N_PATTERNS, B, S, D_ATTN)
    ncc = D_ATTN // LANES
    return _call(
        body, grid=(3, ncc, B),
        in_specs=[pl.BlockSpec((None, N_PATTERNS, None, S, LANES), lambda c, cc, b: (c, 0, b, 0, cc)),
                  pl.BlockSpec((S, LANES), lambda c, cc, b: (b, c * ncc + cc)),
                  pl.BlockSpec((None, 1, LANES), lambda c, cc, b: (c, 0, cc)),
                  pl.BlockSpec((LANES, LANES), lambda c, cc, b: (0, 0))],
        out_specs=[pl.BlockSpec((S, LANES), lambda c, cc, b: (b, c * ncc + cc)),
                   pl.BlockSpec((None, 1, LANES), lambda c, cc, b: (c, 0, cc))],
        out_shape=[jax.ShapeDtypeStruct((B * S, 3 * D_ATTN), BF16), jax.ShapeDtypeStruct((3, 1, D_ATTN), F32)],
        scratch_shapes=[pltpu.VMEM((N_PATTERNS, S, LANES), F32), pltpu.VMEM((S, LANES), F32)],
        sem=("arbitrary", "arbitrary", "arbitrary"), name=name, args=(cur, u, gains, emat), phase=phase)


HALO = 32
SUB = 64
SUBLANES = 8


def _shifted_copies(src_ref, sh_ref, tc):
    sh_ref[0] = src_ref[...]
    for r in range(1, SUBLANES):
        sh_ref[r, 0:tc + HALO - SUBLANES, :] = src_ref[pl.ds(r, tc + HALO - SUBLANES), :]


def _shifted(sh_ref, start, size):
    return sh_ref[start % SUBLANES, pl.ds(start - start % SUBLANES, size), :]


def _conv_fwd(u, cw, cb, lg, lb, B, S, *, tc, name, phase=None):
    nchunk = S // tc
    hb = tc // HALO

    def body(ca_ref, cap_ref, cg_ref, cgp_ref, w_ref, cb_ref, lg_ref, lb_ref, cv_ref, glu_ref, y_ref, pad_ref, sh_ref):
        i = pl.program_id(1)
        glu = ca_ref[...] * _sigmoid(cg_ref[...])
        glu_ref[...] = glu
        prev = cap_ref[...] * _sigmoid(cgp_ref[...])
        pad_ref[0:HALO, :] = jnp.where(i > 0, prev, 0.0)
        pad_ref[HALO:, :] = glu
        _shifted_copies(pad_ref, sh_ref, tc)
        for sub in range(tc // SUB):
            acc = jnp.zeros((SUB, D_CONV), F32) + cb_ref[...]
            for k in range(CONV_K):
                acc = acc + _shifted(sh_ref, sub * SUB + HALO - (CONV_K - 1) + k, SUB) * w_ref[pl.ds(k, 1), :]
            y_ref[sub * SUB:(sub + 1) * SUB, :] = acc
        y = y_ref[...]
        mu = jnp.mean(y, axis=-1, keepdims=True)
        yc = y - mu
        var = jnp.mean(yc * yc, axis=-1, keepdims=True)
        z = yc * lax.rsqrt(var + EPS) * lg_ref[...] + lb_ref[...]
        cv_ref[...] = (z * _sigmoid(z)).astype(BF16)

    def cur(col):
        return pl.BlockSpec((tc, D_CONV), lambda b, i: (b * nchunk + i, col))

    def halo(col):
        return pl.BlockSpec((HALO, D_CONV), lambda b, i: (jnp.maximum((b * nchunk + i) * hb - 1, 0), col))

    vec = pl.BlockSpec((1, D_CONV), lambda b, i: (0, 0))
    out = pl.BlockSpec((tc, D_CONV), lambda b, i: (b * nchunk + i, 0))
    return _call(
        body, grid=(B, nchunk),
        in_specs=[cur(3), halo(3), cur(4), halo(4), pl.BlockSpec((CONV_K, D_CONV), lambda b, i: (0, 0)), vec, vec, vec],
        out_specs=[out, out, out],
        out_shape=[jax.ShapeDtypeStruct((B * S, D_CONV), BF16), jax.ShapeDtypeStruct((B * S, D_CONV), F32),
                   jax.ShapeDtypeStruct((B * S, D_CONV), F32)],
        scratch_shapes=[pltpu.VMEM((tc + HALO, D_CONV), F32), pltpu.VMEM((SUBLANES, tc + HALO, D_CONV), F32)],
        sem=("arbitrary", "arbitrary"), name=name, args=(u, u, u, u, cw, cb, lg, lb), phase=phase)


def _conv_bwd_norm(dcv, y, lg, lb, *, tc, name):
    T = y.shape[0]

    def body(dcv_ref, y_ref, lg_ref, lb_ref, dy_ref, part_ref):
        yv = y_ref[...]
        mu = jnp.mean(yv, axis=-1, keepdims=True)
        yc = yv - mu
        var = jnp.mean(yc * yc, axis=-1, keepdims=True)
        rstd = lax.rsqrt(var + EPS)
        xhat = yc * rstd
        z = xhat * lg_ref[...] + lb_ref[...]
        sig = _sigmoid(z)
        dz = dcv_ref[...] * (sig * (1.0 + z * (1.0 - sig)))
        dxh = dz * lg_ref[...]
        dy = rstd * (dxh - jnp.mean(dxh, axis=-1, keepdims=True)
                     - xhat * jnp.mean(dxh * xhat, axis=-1, keepdims=True))
        dy_ref[...] = dy

        @pl.when(pl.program_id(0) == 0)
        def _():
            part_ref[...] = jnp.zeros_like(part_ref)

        part_ref[0:1, :] += jnp.sum(dz * xhat, axis=0, keepdims=True)
        part_ref[1:2, :] += jnp.sum(dz, axis=0, keepdims=True)
        part_ref[2:3, :] += jnp.sum(dy, axis=0, keepdims=True)

    tok = pl.BlockSpec((tc, D_CONV), lambda i: (i, 0))
    vec = pl.BlockSpec((1, D_CONV), lambda i: (0, 0))
    return pl.pallas_call(
        body, grid=(T // tc,), in_specs=[tok, tok, vec, vec],
        out_specs=[tok, pl.BlockSpec((8, D_CONV), lambda i: (0, 0))],
        out_shape=[jax.ShapeDtypeStruct((T, D_CONV), F32), jax.ShapeDtypeStruct((8, D_CONV), F32)],
        compiler_params=_params("arbitrary"), name=name)(dcv, y, lg, lb)


def _conv_bwd_taps(dy, glu, u, cw, B, S, *, tc, name, phase=None):
    nchunk = S // tc
    hb = tc // HALO
    last_hb = B * S // HALO - 1

    def body(dy_ref, dyn_ref, glu_ref, glup_ref, ca_ref, cg_ref, w_ref, dca_ref, dcg_ref, dw_ref,
             dyp_ref, glp_ref, acc_ref, shd_ref, shg_ref):
        b = pl.program_id(0)
        i = pl.program_id(1)
        dy = dy_ref[...]
        dyp_ref[0:tc, :] = dy
        dyp_ref[tc:, :] = jnp.where(i < nchunk - 1, dyn_ref[...], 0.0)
        glp_ref[0:HALO, :] = jnp.where(i > 0, glup_ref[...], 0.0)
        glp_ref[HALO:, :] = glu_ref[...]
        _shifted_copies(dyp_ref, shd_ref, tc)
        _shifted_copies(glp_ref, shg_ref, tc)

        @pl.when((b == 0) & (i == 0))
        def _():
            dw_ref[...] = jnp.zeros_like(dw_ref)

        for sub in range(tc // SUB):
            acc = jnp.zeros((SUB, D_CONV), F32)
            for k in range(CONV_K):
                acc = acc + _shifted(shd_ref, sub * SUB + (CONV_K - 1) - k, SUB) * w_ref[pl.ds(k, 1), :]
            acc_ref[sub * SUB:(sub + 1) * SUB, :] = acc
        for k in range(CONV_K):
            dw_ref[k:k + 1, :] += jnp.sum(dy * _shifted(shg_ref, HALO - (CONV_K - 1) + k, tc), axis=0, keepdims=True)
        dglu = acc_ref[...]
        ca = ca_ref[...]
        sig = _sigmoid(cg_ref[...])
        dca_ref[...] = (dglu * sig).astype(BF16)
        dcg_ref[...] = (dglu * ca * sig * (1.0 - sig)).astype(BF16)

    tok = pl.BlockSpec((tc, D_CONV), lambda b, i: (b * nchunk + i, 0))
    nxt = pl.BlockSpec((HALO, D_CONV), lambda b, i: (jnp.minimum((b * nchunk + i + 1) * hb, last_hb), 0))
    prv = pl.BlockSpec((HALO, D_CONV), lambda b, i: (jnp.maximum((b * nchunk + i) * hb - 1, 0), 0))
    return _call(
        body, grid=(B, nchunk),
        in_specs=[tok, nxt, tok, prv,
                  pl.BlockSpec((tc, D_CONV), lambda b, i: (b * nchunk + i, 3)),
                  pl.BlockSpec((tc, D_CONV), lambda b, i: (b * nchunk + i, 4)),
                  pl.BlockSpec((CONV_K, D_CONV), lambda b, i: (0, 0))],
        out_specs=[tok, tok, pl.BlockSpec((32, D_CONV), lambda b, i: (0, 0))],
        out_shape=[jax.ShapeDtypeStruct((B * S, D_CONV), BF16), jax.ShapeDtypeStruct((B * S, D_CONV), BF16),
                   jax.ShapeDtypeStruct((32, D_CONV), F32)],
        scratch_shapes=[pltpu.VMEM((tc + HALO, D_CONV), F32), pltpu.VMEM((tc + HALO, D_CONV), F32),
                        pltpu.VMEM((tc, D_CONV), F32), pltpu.VMEM((SUBLANES, tc + HALO, D_CONV), F32),
                        pltpu.VMEM((SUBLANES, tc + HALO, D_CONV), F32)],
        sem=("arbitrary", "arbitrary"), name=name, args=(dy, dy, glu, glu, u, u, cw), phase=phase)


def _outproj_fwd(h, attn, cv, wout, *, tm, name):
    T, D = h.shape

    def body(h_ref, a_ref, c_ref, w_ref, o_ref):
        o_ref[...] = (h_ref[...] + _dot(a_ref[...].astype(BF16), w_ref[0:D_ATTN, :])
                      + _dot(c_ref[...], w_ref[D_ATTN:, :]))

    tok = pl.BlockSpec((tm, D), lambda i: (i, 0))
    half = pl.BlockSpec((tm, D_ATTN), lambda i: (i, 0))
    return pl.pallas_call(
        body, grid=(T // tm,), in_specs=[tok, half, half, pl.BlockSpec(wout.shape, lambda i: (0, 0))],
        out_specs=tok, out_shape=jax.ShapeDtypeStruct((T, D), F32),
        compiler_params=_params("arbitrary"), name=name)(h, attn, cv, wout)


def _outproj_bwd(dh, attn, cv, wout, *, tm, name):
    T, D = dh.shape

    def body(dh_ref, a_ref, c_ref, w_ref, da_ref, dc_ref, dw_ref):
        @pl.when(pl.program_id(0) == 0)
        def _():
            dw_ref[...] = jnp.zeros_like(dw_ref)

        dhb = dh_ref[...].astype(BF16)
        da_ref[...] = _dot_nt(dhb, w_ref[0:D_ATTN, :])
        dc_ref[...] = _dot_nt(dhb, w_ref[D_ATTN:, :])
        dw_ref[0:D_ATTN, :] += _dot_tn(a_ref[...].astype(BF16), dhb)
        dw_ref[D_ATTN:, :] += _dot_tn(c_ref[...], dhb)

    tok = pl.BlockSpec((tm, D), lambda i: (i, 0))
    half = pl.BlockSpec((tm, D_ATTN), lambda i: (i, 0))
    wspec = pl.BlockSpec(wout.shape, lambda i: (0, 0))
    return pl.pallas_call(
        body, grid=(T // tm,), in_specs=[tok, half, half, wspec], out_specs=[half, half, wspec],
        out_shape=[jax.ShapeDtypeStruct((T, D_ATTN), F32), jax.ShapeDtypeStruct((T, D_ATTN), F32),
                   jax.ShapeDtypeStruct(wout.shape, F32)],
        compiler_params=_params("arbitrary"), name=name)(dh, attn, cv, wout)


ADAM_BLOCK_BYTES = 3 * 512 * 1024


def _adamw(w, g, m, v, *, name):
    R, C = w.shape
    tr = R
    for cand in (512, 352, 256, 176, 128, 64, 32, 16, 8):
        if R % cand == 0 and cand * C * 4 <= ADAM_BLOCK_BYTES:
            tr = cand
            break
    c1 = 1.0 - ADAM_B1 ** ADAM_STEP
    c2 = 1.0 - ADAM_B2 ** ADAM_STEP

    def body(w_ref, g_ref, m_ref, v_ref, d_ref, nm_ref, nv_ref):
        gv = g_ref[...]
        nm = ADAM_B1 * m_ref[...] + (1.0 - ADAM_B1) * gv
        nv = ADAM_B2 * v_ref[...] + (1.0 - ADAM_B2) * (gv * gv)
        d_ref[...] = -ADAM_LR * ((nm / c1) / (jnp.sqrt(nv / c2) + ADAM_EPS) + ADAM_WD * w_ref[...])
        nm_ref[...] = nm
        nv_ref[...] = nv

    blk = pl.BlockSpec((tr, C), lambda i: (i, 0))
    return pl.pallas_call(
        body, grid=(R // tr,), in_specs=[blk] * 4, out_specs=[blk] * 3,
        out_shape=[jax.ShapeDtypeStruct((R, C), F32)] * 3,
        compiler_params=_params("arbitrary"), name=name)(w, g, m, v)


ADAM_SPLIT = 4


def _adamw_many(ws, gs, ms, vs, *, name, phase=None):
    n = len(ws)
    c1 = 1.0 - ADAM_B1 ** ADAM_STEP
    c2 = 1.0 - ADAM_B2 ** ADAM_STEP

    def body(*refs):
        ins, outs = refs[:4 * n], refs[4 * n:]
        for a in range(n):
            w_ref, g_ref, m_ref, v_ref = ins[4 * a:4 * a + 4]
            gv = g_ref[...]
            nm = ADAM_B1 * m_ref[...] + (1.0 - ADAM_B1) * gv
            nv = ADAM_B2 * v_ref[...] + (1.0 - ADAM_B2) * (gv * gv)
            outs[4 * a][...] = -ADAM_LR * ((nm / c1) / (jnp.sqrt(nv / c2) + ADAM_EPS) + ADAM_WD * w_ref[...])
            outs[4 * a + 1][...] = nm
            outs[4 * a + 2][...] = nv
            outs[4 * a + 3][...] = gv

    in_specs, out_specs, out_shape, args = [], [], [], []
    for w, g, m, v in zip(ws, gs, ms, vs):
        R, C = w.shape
        blk = pl.BlockSpec((R // ADAM_SPLIT, C), lambda i: (i, 0))
        in_specs += [blk] * 4
        out_specs += [blk] * 4
        out_shape += [jax.ShapeDtypeStruct((R, C), F32)] * 4
        args += [w, g, m, v]
    res = _call(body, grid=(ADAM_SPLIT,), in_specs=in_specs, out_specs=out_specs, out_shape=out_shape,
                sem=("arbitrary",), name=name, args=args, phase=phase)
    outs, extra = res if phase is not None else (res, None)
    return list(outs[0::4]), list(outs[1::4]), list(outs[2::4]), list(outs[3::4]), extra


TM = 512
TM_WIDE = 1024
TK = 1024
TC = 512


def _local_step(x, tgt, w, overlap=None):
    B, S, D = x.shape
    T = B * S
    x2 = x.reshape(T, D)
    t2 = tgt.reshape(T, D)
    ones = jnp.ones((1, D_ATTN), F32)
    scale = HEAD_DIM ** -0.5
    gains = jnp.stack([jnp.tile(w["q_norm"], (1, HEADS)) * scale, jnp.tile(w["k_norm"], (1, HEADS)), ones])
    g = {}

    def hosting(point, build):
        phase = overlap.phase(point, w, g) if overlap is not None else None
        if phase is None:
            return build(None)
        outs, extra = build(phase)
        overlap.done(point, extra, w, g)
        return outs

    h1, n1, G1, U1 = hosting("ffn1_fwd", lambda ph: _ffn_fwd(
        x2, w["ffn1_norm"], w["wg1"], w["wu1"], w["wd1"], None, tm=TM_WIDE, name="ffn1_fwd", phase=ph))
    u, n2 = hosting("inproj_fwd", lambda ph: _inproj_fwd(h1, w["mix_norm"], w["win"], tm=TM_WIDE, name="inproj_fwd", phase=ph))
    qkv = _qkv_prep(u, gains, B, S, name="qkv_prep")
    qkv = qkv.reshape(3, N_PATTERNS, T // QBLK, QBLK, D_ATTN)
    o3, lse3 = hosting("attn_fwd", lambda ph: _attn_fwd(qkv, name="attn_fwd", phase=ph))
    attn, lse = _attn_combine(o3, lse3, B, S, name="attn_combine")
    cv, glu, yconv = hosting("conv_fwd", lambda ph: _conv_fwd(
        u, w["conv_w"], w["conv_b"], w["conv_ln_g"], w["conv_ln_b"], B, S, tc=TC, name="conv_fwd", phase=ph))
    h2 = _outproj_fwd(h1, attn, cv, w["wout"], tm=TM_WIDE, name="outproj_fwd")
    dh3, n3, G2, U2, loss = _ffn_fwd(h2, w["ffn2_norm"], w["wg2"], w["wu2"], w["wd2"], t2, tm=TM_WIDE, name="ffn2_fwd")

    dG, dU, A, dy, dh2, g["ffn2_norm"] = _ffn_bwd_act(dh3, h2, w["ffn2_norm"], G2, U2, w["wg2"], w["wu2"], w["wd2"],
                                                    tm=TM, name="ffn2_bwd_act")
    g["wg2"], g["wu2"], g["wd2"] = _ffn_bwd_w(n3, dy, dG, dU, A, tk=2 * TK, name="ffn2_bwd_w")
    dattn, dcv, g["wout"] = _outproj_bwd(dh2, attn, cv, w["wout"], tm=TM_WIDE, name="outproj_bwd")
    dyc, cpart = _conv_bwd_norm(dcv, yconv, w["conv_ln_g"], w["conv_ln_b"], tc=TC, name="conv_bwd_norm")
    dca, dcg, dcw = hosting("conv_bwd_taps", lambda ph: _conv_bwd_taps(
        dyc, glu, u, w["conv_w"], B, S, tc=TC, name="conv_bwd_taps", phase=ph))
    do3, st3 = _attn_bwd_prep(dattn, attn, lse, B, S, name="attn_bwd_prep")
    nb = T // QBLK
    (cur,) = hosting("attn_bwd", lambda ph: _attn_bwd(
        qkv, do3.reshape(N_PATTERNS, nb, QBLK, D_ATTN), st3.reshape(N_PATTERNS, nb, QBLK, LANES),
        name="attn_bwd", phase=ph))
    du_qkv, dgains = hosting("attn_grad_combine", lambda ph: _attn_grad_combine(
        cur, u, gains, B, S, name="attn_grad_combine", phase=ph))
    du = jnp.concatenate([du_qkv, dca, dcg], axis=1)
    (g["win"],) = hosting("inproj_bwd_w", lambda ph: _inproj_bwd_w(
        n2, du, w["win"].shape[0], tk=2 * TK, name="inproj_bwd_w", phase=ph))
    dh1, g["mix_norm"] = hosting("inproj_bwd_act", lambda ph: _inproj_bwd_act(
        du, dh2, h1, w["mix_norm"], w["win"], tm=TM_WIDE, name="inproj_bwd_act", phase=ph))
    dG, dU, A, dy, dx, g["ffn1_norm"] = hosting("ffn1_bwd_act", lambda ph: _ffn_bwd_act(
        dh1, x2, w["ffn1_norm"], G1, U1, w["wg1"], w["wu1"], w["wd1"], tm=TM, name="ffn1_bwd_act", phase=ph))
    g["wg1"], g["wu1"], g["wd1"] = hosting("ffn1_bwd_w", lambda ph: _ffn_bwd_w(
        n1, dy, dG, dU, A, tk=2 * TK, name="ffn1_bwd_w", phase=ph))

    g["q_norm"] = dgains[0].reshape(HEADS, HEAD_DIM).sum(axis=0, keepdims=True) * scale
    g["k_norm"] = dgains[1].reshape(HEADS, HEAD_DIM).sum(axis=0, keepdims=True)
    g["conv_ln_g"] = cpart[0:1]
    g["conv_ln_b"] = cpart[1:2]
    g["conv_b"] = cpart[2:3]
    g["conv_w"] = dcw[:CONV_K]
    return loss, dx.reshape(B, S, D), g


N_CHIPS = 4
N_DEV = 8
VMEM_SPEC = pl.BlockSpec(memory_space=pltpu.VMEM)


def _remote(src, dst, send_sem, recv_sem, device):
    return pltpu.make_async_remote_copy(src_ref=src, dst_ref=dst, send_sem=send_sem, recv_sem=recv_sem,
                                        device_id=device, device_id_type=MESH)


def _stage_shards(shards, dtypes, *, name):
    n = len(shards)
    halves = [s.reshape(2, s.shape[0] // 2, s.shape[1]) for s in shards]

    def body(*refs):
        ins, outs, vms, loc_sems = refs[:n], refs[n:2 * n], refs[2 * n:3 * n], refs[3 * n]
        me = 2 * lax.axis_index("x") + lax.axis_index("y")
        copies = []
        for a in range(n):
            vms[a][...] = ins[a][...].astype(dtypes[a])
            cp = pltpu.make_async_copy(vms[a], outs[a].at[me], loc_sems.at[a])
            cp.start()
            copies.append(cp)
        for cp in copies:
            cp.wait()

    return pl.pallas_call(
        body, in_specs=[VMEM_SPEC] * n, out_specs=[ANY] * n,
        out_shape=[jax.ShapeDtypeStruct((N_CHIPS,) + h.shape, dt) for h, dt in zip(halves, dtypes)],
        scratch_shapes=[pltpu.VMEM(h.shape, dt) for h, dt in zip(halves, dtypes)] + [DMA_SEMS((n,))],
        compiler_params=pltpu.CompilerParams(vmem_limit_bytes=VMEM_LIMIT), name=name)(*halves)


def _like(arrays):
    return [jax.ShapeDtypeStruct(a.shape, a.dtype) for a in arrays]


def _axes():
    x, y, c = lax.axis_index("x"), lax.axis_index("y"), lax.axis_index("c")
    first = (x + (1 - c) * (1 - 2 * x), y + c * (1 - 2 * y))
    second = (x + c * (1 - 2 * x), y + (1 - c) * (1 - 2 * y))
    slots = tuple(2 * px + py for px, py in ((x, y), first, second, (1 - x, 1 - y)))
    return (x, y, c), (*first, c), (*second, c), slots


def _gather_ici_phase(bufs, only=None):
    n = len(bufs)

    def stage1(ins, outs, sems):
        (x, y, c), peer1, peer2, (own, s1, s2, both) = _axes()
        starts, arrivals = [], []
        for a in range(n):
            mine, land = outs[a].at[own, c], outs[a].at[s2, c]
            starts.append(_remote(mine, mine, *sems(a), peer2))
            arrivals.append(_remote(land, land, *sems(a), peer2))
        return starts, arrivals

    def stage2(ins, outs, sems):
        (x, y, c), peer1, peer2, (own, s1, s2, both) = _axes()
        starts, arrivals = [], []
        for a in range(n):
            for k, (src, dst) in enumerate(((own, s1), (s2, both))):
                mine, land = outs[a].at[src, c], outs[a].at[dst, c]
                starts.append(_remote(mine, mine, *sems(2 * a + k), peer1))
                arrivals.append(_remote(land, land, *sems(2 * a + k), peer1))
        return starts, arrivals

    same = {a: a for a in range(n)}
    first, second = _Phase(bufs, _like(bufs), same, n, stage1), _Phase(bufs, _like(bufs), same, 2 * n, stage2)
    if only is None:
        return first.then(second)
    return first if only == 1 else second


def _gather_d2d_phase(bufs):
    n = len(bufs)

    def copies(ins, outs, sems):
        (x, y, c), peer1, peer2, (own, s1, s2, both) = _axes()
        starts, arrivals = [], []
        for a in range(n):
            for j, s in enumerate((s1, s2, both)):
                got, land = outs[a].at[s, c], outs[a].at[s, 1 - c]
                starts.append(_remote(got, got, *sems(3 * a + j), (x, y, 1 - c)))
                arrivals.append(_remote(land, land, *sems(3 * a + j), (x, y, 1 - c)))
        return starts, arrivals

    return _Phase(bufs, _like(bufs), {a: a for a in range(n)}, 3 * n, copies)


def _gather_pipelined_phase(bufs):
    n = len(bufs)

    def piece(kind, a):
        def copies(ins, outs, sems):
            (x, y, c), peer1, peer2, (own, s1, s2, both) = _axes()
            moves = {1: [(own, s2, c, c, peer2)],
                     2: [(own, s1, c, c, peer1), (s2, both, c, c, peer1)],
                     3: [(s, s, c, 1 - c, (x, y, 1 - c)) for s in (s1, s2, both)]}[kind]
            starts, arrivals = [], []
            for k, (src, dst, h_src, h_dst, peer) in enumerate(moves):
                mine, land = outs[a].at[src, h_src], outs[a].at[dst, h_dst]
                starts.append(_remote(mine, mine, *sems(k), peer))
                arrivals.append(_remote(land, land, *sems(k), peer))
            return starts, arrivals

        return kind, copies

    def beside(pieces):
        def copies(ins, outs, sems):
            starts, arrivals, base = [], [], 0
            for count, fn in pieces:
                s, r = fn(ins, outs, lambda i, base=base: sems(base + i))
                starts, arrivals, base = starts + s, arrivals + r, base + count
            return starts, arrivals

        return sum(count for count, _ in pieces), copies

    phase = _Phase(bufs, _like(bufs), {a: a for a in range(n)}, 0, None)
    phase.stages = [beside([piece(kind, t - kind + 1) for kind in (1, 2, 3) if 0 <= t - kind + 1 < n])
                    for t in range(n + 2)]
    return phase


def _exchange_phase(views):
    n = len(views)

    def copies(ins, outs, sems):
        x, y, c = lax.axis_index("x"), lax.axis_index("y"), lax.axis_index("c")
        starts = [_remote(ins[a].at[pl.ds(0, ins[a].shape[0]), 1 - c], outs[a], *sems(a), (x, y, 1 - c))
                  for a in range(n)]
        return starts, starts

    outs = [jax.ShapeDtypeStruct((v.shape[0],) + v.shape[2:], F32) for v in views]
    return _Phase(views, outs, {}, n, copies)


ADD_SPLIT = 2


def _add_halves(views, got, sel, tag):
    n = len(views)

    def body(s_ref, *refs):
        ins, outs = refs[:4 * n], refs[4 * n:]
        for a in range(n):
            gk, rk, gs, rs = ins[4 * a:4 * a + 4]
            outs[2 * a][...] = gk[...] + rk[...]
            outs[2 * a + 1][...] = (gs[...] + rs[...]).astype(BF16)

    in_specs, out_specs, out_shape, args = [], [], [], []
    for g, r in zip(views, got):
        _, _, rh, cdim = g.shape
        tr = rh // ADD_SPLIT
        for off in (0, 2):
            in_specs.append(pl.BlockSpec((None, None, tr, cdim), lambda k, i, s, off=off: (s[1 + off + k], s[0], i, 0)))
            in_specs.append(pl.BlockSpec((None, tr, cdim), lambda k, i, s, off=off: (s[1 + off + k], i, 0)))
            args += [g, r]
        out_specs += [pl.BlockSpec((None, tr, cdim), lambda k, i, s: (k, i, 0))] * 2
        out_shape += [jax.ShapeDtypeStruct((2, rh, cdim), F32), jax.ShapeDtypeStruct((2, rh, cdim), BF16)]
    res = pl.pallas_call(
        body,
        grid_spec=pltpu.PrefetchScalarGridSpec(num_scalar_prefetch=1, grid=(2, ADD_SPLIT), in_specs=in_specs,
                                               out_specs=out_specs),
        out_shape=out_shape, compiler_params=_params("arbitrary", "arbitrary"), name=f"rs_add_half_{tag}")(sel, *args)
    return list(res[0::2]), list(res[1::2])


def _swap_phase(arrays, stage):
    n = len(arrays)

    def copies(ins, outs, sems):
        peer = _axes()[stage]
        starts = [_remote(ins[a], outs[a], *sems(a), peer) for a in range(n)]
        return starts, starts

    return _Phase(arrays, _like(arrays), {}, n, copies)


def _add_first(keep, got, tag):
    n = len(keep)

    def body(*refs):
        ins, outs = refs[:2 * n], refs[2 * n:]
        for a in range(n):
            k_ref, g_ref = ins[2 * a], ins[2 * a + 1]
            outs[2 * a][...] = k_ref[0] + g_ref[0].astype(F32)
            outs[2 * a + 1][...] = (k_ref[1] + g_ref[1].astype(F32)).astype(BF16)

    in_specs, out_specs, out_shape, args = [], [], [], []
    for k, g in zip(keep, got):
        _, rh, cdim = k.shape
        tr = rh // ADD_SPLIT
        in_specs += [pl.BlockSpec((2, tr, cdim), lambda i: (0, i, 0))] * 2
        out_specs += [pl.BlockSpec((tr, cdim), lambda i: (i, 0))] * 2
        out_shape += [jax.ShapeDtypeStruct((rh, cdim), F32), jax.ShapeDtypeStruct((rh, cdim), BF16)]
        args += [k, g]
    res = pl.pallas_call(body, grid=(ADD_SPLIT,), in_specs=in_specs, out_specs=out_specs, out_shape=out_shape,
                         compiler_params=_params("arbitrary"), name=f"rs_add_first_{tag}")(*args)
    return list(res[0::2]), list(res[1::2])


def _add_second(keep, got, sel, tag):
    n = len(keep)

    def body(s_ref, *refs):
        ins, outs = refs[:2 * n], refs[2 * n:]
        for a in range(n):
            outs[a][...] = ins[2 * a][...] + ins[2 * a + 1][...].astype(F32)

    in_specs, out_specs, out_shape, args = [], [], [], []
    for k, g in zip(keep, got):
        rh, cdim = k.shape
        tr = rh // ADD_SPLIT
        in_specs += [pl.BlockSpec((tr, cdim), lambda i, s: (i, 0))] * 2
        out_specs.append(pl.BlockSpec((None, tr, cdim), lambda i, s: (s[0], i, 0)))
        out_shape.append(jax.ShapeDtypeStruct((2, rh, cdim), F32))
        args += [k, g]
    res = pl.pallas_call(
        body,
        grid_spec=pltpu.PrefetchScalarGridSpec(num_scalar_prefetch=1, grid=(ADD_SPLIT,), in_specs=in_specs,
                                               out_specs=out_specs),
        out_shape=out_shape, compiler_params=_params("arbitrary"), name=f"rs_add_second_{tag}")(sel, *args)
    return list(res)


def _join_phase(halves):
    n = len(halves)

    def copies(ins, outs, sems):
        x, y, c = lax.axis_index("x"), lax.axis_index("y"), lax.axis_index("c")
        starts, arrivals = [], []
        for a in range(n):
            mine, land = outs[a].at[c], outs[a].at[1 - c]
            starts.append(_remote(mine, mine, *sems(a), (x, y, 1 - c)))
            arrivals.append(_remote(land, land, *sems(a), (x, y, 1 - c)))
        return starts, arrivals

    return _Phase(halves, _like(halves), {a: a for a in range(n)}, n, copies)


def _slot_order():
    x, y, c = lax.axis_index("x"), lax.axis_index("y"), lax.axis_index("c")
    own, flip_x, flip_y, both = 2 * x + y, 2 * (1 - x) + y, 2 * x + 1 - y, 2 * (1 - x) + 1 - y
    first = jnp.where(c == 0, flip_x, flip_y)
    second = jnp.where(c == 0, flip_y, flip_x)
    return jnp.stack([c, own, second, first, both]).astype(jnp.int32)


def _half_view(g):
    return g.reshape(N_CHIPS, 2, g.shape[1] // 2, g.shape[2])


EARLY_GRADS = ("wg2", "wu2", "wd2", "wout")
MIDDLE_GRADS = ("win",)


class _Overlap:
    EARLY_AT = ("conv_bwd_taps", "attn_bwd", "attn_grad_combine", "inproj_bwd_w")
    MIDDLE_AT = ("inproj_bwd_act", "ffn1_bwd_act", "ffn1_bwd_w", None)

    def __init__(self, staged):
        self.staged = staged
        self.ffn2 = list(staged[3:])
        self.sel = sel = _slot_order()
        self.early = _Reduction(EARLY_GRADS, "early", sel)
        self.middle = _Reduction(MIDDLE_GRADS, "middle", sel)

    def finish_late(self, late):
        views = [_half_view(a) for a in late]
        got = _run_phase(_exchange_phase(views), name="rs_exchange_halves")
        keep, send = _add_halves(views, got, self.sel, "late")
        got = _run_phase(_swap_phase(send, 1), name="rs_swap_first_axis")
        keep, send = _add_first(keep, got, "late")
        got = _run_phase(_swap_phase(send, 2), name="rs_swap_second_axis")
        halves = _add_second(keep, got, self.sel, "late")
        full = _run_phase(_join_phase(halves + list(self.middle.halves)), name="rs_join_halves")
        return [f.reshape(-1, f.shape[-1]) for f in full]

    def phase(self, point, w, g):
        if point == "ffn1_fwd":
            return _gather_ici_phase(self.staged[:3]).then(_gather_d2d_phase(self.staged[:3]))
        if point == "inproj_fwd":
            return _gather_ici_phase(self.ffn2, only=1)
        if point == "attn_fwd":
            return _gather_ici_phase(self.ffn2, only=2)
        if point == "conv_fwd":
            return _gather_d2d_phase(self.ffn2)
        for red, at in ((self.early, self.EARLY_AT), (self.middle, self.MIDDLE_AT)):
            if point in at:
                return red.phase(at.index(point), g)
        return None

    def done(self, point, outs, w, g):
        if point == "ffn1_fwd":
            win, wout, taps = [_whole(b) for b in outs]
            w["win"] = win
            w["wout"] = wout.reshape(-1, wout.shape[-1])
            w["conv_w"] = taps.transpose(1, 0, 2).reshape(CONV_K + 1, D_CONV)[:CONV_K]
        elif point in ("inproj_fwd", "attn_fwd"):
            self.ffn2 = list(outs)
        elif point == "conv_fwd":
            w["wg2"], w["wu2"], w["wd2"] = [_whole(b) for b in outs]
        for red, at in ((self.early, self.EARLY_AT), (self.middle, self.MIDDLE_AT)):
            if point in at:
                red.done(at.index(point), outs)


class _Reduction:
    def __init__(self, names, tag, sel):
        self.names, self.tag, self.sel = names, tag, sel
        self.reduced = {}

    def phase(self, stage, g):
        if stage == 0:
            self.cols = [g[k].shape[-1] for k in self.names]
            self.views = [_half_view(g[k].reshape(N_CHIPS, -1, g[k].shape[-1])) for k in self.names]
            return _exchange_phase(self.views)
        if stage in (1, 2):
            return _swap_phase(self.send, stage)
        return _join_phase(self.halves)

    def done(self, stage, outs):
        if stage == 0:
            self.keep, self.send = _add_halves(self.views, outs, self.sel, self.tag)
        elif stage == 1:
            self.keep, self.send = _add_first(self.keep, outs, self.tag)
        elif stage == 2:
            self.halves = _add_second(self.keep, outs, self.sel, self.tag)
        else:
            for k, c, f in zip(self.names, self.cols, outs):
                self.reduced[k] = f.reshape(-1, c)


def _whole(buf):
    return buf.reshape(buf.shape[0], 2 * buf.shape[2], buf.shape[3])


def _allreduce_small(pack, *, name):
    rows = pack.shape[0]

    def body(p_ref, o_ref, buf_ref, send_sems, recv_sems):
        x, y, c = lax.axis_index("x"), lax.axis_index("y"), lax.axis_index("c")
        me = 4 * x + 2 * y + c
        buf_ref[me] = p_ref[...]
        cps = []
        for k in range(1, N_DEV):
            peer = tuple(1 - v if (k >> s) & 1 else v for v, s in ((x, 2), (y, 1), (c, 0)))
            cp = _remote(p_ref, buf_ref.at[me], send_sems.at[k - 1], recv_sems.at[k - 1], peer)
            cp.start()
            cps.append(cp)
        for k in range(1, N_DEV):
            src = 4 * (x ^ ((k >> 2) & 1)) + 2 * (y ^ ((k >> 1) & 1)) + (c ^ (k & 1))
            land = buf_ref.at[src]
            _remote(land, land, send_sems.at[k - 1], recv_sems.at[k - 1], (x, y, c)).wait_recv()
        acc = buf_ref[0]
        for d in range(1, N_DEV):
            acc = acc + buf_ref[d]
        o_ref[...] = acc
        for cp in cps:
            cp.wait_send()

    return pl.pallas_call(
        body, in_specs=[VMEM_SPEC], out_specs=VMEM_SPEC, out_shape=jax.ShapeDtypeStruct(pack.shape, F32),
        scratch_shapes=[pltpu.VMEM((N_DEV, rows, LANES), F32), pltpu.SemaphoreType.DMA((N_DEV - 1,)),
                        pltpu.SemaphoreType.DMA((N_DEV - 1,))], name=name)(pack)


SMALL = ("ffn1_norm", "mix_norm", "q_norm", "k_norm", "conv_b", "conv_ln_g", "conv_ln_b", "ffn2_norm", "conv_w")
BIG = ("ffn1_w_gate", "ffn1_w_up", "ffn1_w_down", "w_in", "w_out", "ffn2_w_gate", "ffn2_w_up", "ffn2_w_down")
TRANSPOSED = ("ffn1_w_gate", "ffn1_w_up", "ffn2_w_gate", "ffn2_w_up")
WEIGHTS = ("ffn1_norm", "ffn1_w_gate", "ffn1_w_up", "ffn1_w_down", "mix_norm", "w_in", "q_norm", "k_norm",
           "conv_w", "conv_b", "conv_ln_g", "conv_ln_b", "w_out", "ffn2_norm", "ffn2_w_gate", "ffn2_w_up",
           "ffn2_w_down")


def _pack(parts):
    rows = []
    for p in parts:
        flat = p.reshape(-1)
        tile = SUBLANES * LANES
        padded = -(-flat.shape[0] // tile) * tile
        rows.append(jnp.pad(flat, (0, padded - flat.shape[0])).reshape(-1, LANES))
    return jnp.concatenate(rows, axis=0)


def _unpack(pack, shapes):
    out, row = [], 0
    for shp in shapes:
        size = shp[0] * shp[1]
        tile = SUBLANES * LANES
        nrows = -(-size // tile) * SUBLANES
        out.append(pack[row:row + nrows].reshape(-1)[:size].reshape(shp))
        row += nrows
    return out


def kernel(x, ffn1_norm, ffn1_w_gate, ffn1_w_up, ffn1_w_down, mix_norm, w_in, q_norm, k_norm, conv_w, conv_b, conv_ln_g, conv_ln_b, w_out, ffn2_norm, ffn2_w_gate, ffn2_w_up, ffn2_w_down, loss_target, m_ffn1_norm, m_ffn1_w_gate, m_ffn1_w_up, m_ffn1_w_down, m_mix_norm, m_w_in, m_q_norm, m_k_norm, m_conv_w, m_conv_b, m_conv_ln_g, m_conv_ln_b, m_w_out, m_ffn2_norm, m_ffn2_w_gate, m_ffn2_w_up, m_ffn2_w_down, v_ffn1_norm, v_ffn1_w_gate, v_ffn1_w_up, v_ffn1_w_down, v_mix_norm, v_w_in, v_q_norm, v_k_norm, v_conv_w, v_conv_b, v_conv_ln_g, v_conv_ln_b, v_w_out, v_ffn2_norm, v_ffn2_w_gate, v_ffn2_w_up, v_ffn2_w_down):
    wts = dict(ffn1_norm=ffn1_norm, ffn1_w_gate=ffn1_w_gate[0], ffn1_w_up=ffn1_w_up[0], ffn1_w_down=ffn1_w_down[0],
               mix_norm=mix_norm, w_in=w_in[0], q_norm=q_norm, k_norm=k_norm, conv_w=conv_w[0], conv_b=conv_b,
               conv_ln_g=conv_ln_g, conv_ln_b=conv_ln_b, w_out=w_out[0], ffn2_norm=ffn2_norm,
               ffn2_w_gate=ffn2_w_gate[0], ffn2_w_up=ffn2_w_up[0], ffn2_w_down=ffn2_w_down[0])
    mom = dict(ffn1_norm=m_ffn1_norm, ffn1_w_gate=m_ffn1_w_gate[0], ffn1_w_up=m_ffn1_w_up[0], ffn1_w_down=m_ffn1_w_down[0],
               mix_norm=m_mix_norm, w_in=m_w_in[0], q_norm=m_q_norm, k_norm=m_k_norm, conv_w=m_conv_w[0], conv_b=m_conv_b,
               conv_ln_g=m_conv_ln_g, conv_ln_b=m_conv_ln_b, w_out=m_w_out[0], ffn2_norm=m_ffn2_norm,
               ffn2_w_gate=m_ffn2_w_gate[0], ffn2_w_up=m_ffn2_w_up[0], ffn2_w_down=m_ffn2_w_down[0])
    var = dict(ffn1_norm=v_ffn1_norm, ffn1_w_gate=v_ffn1_w_gate[0], ffn1_w_up=v_ffn1_w_up[0], ffn1_w_down=v_ffn1_w_down[0],
               mix_norm=v_mix_norm, w_in=v_w_in[0], q_norm=v_q_norm, k_norm=v_k_norm, conv_w=v_conv_w[0], conv_b=v_conv_b,
               conv_ln_g=v_conv_ln_g, conv_ln_b=v_conv_ln_b, w_out=v_w_out[0], ffn2_norm=v_ffn2_norm,
               ffn2_w_gate=v_ffn2_w_gate[0], ffn2_w_up=v_ffn2_w_up[0], ffn2_w_down=v_ffn2_w_down[0])
    chip = 2 * lax.axis_index("x") + lax.axis_index("y")
    for src in (wts, mom, var):
        for n in TRANSPOSED:
            src[n] = src[n].T

    taps = jnp.pad(wts["conv_w"], ((0, 1), (0, 0)))
    staged = _stage_shards([wts["ffn1_w_gate"], wts["ffn1_w_up"], wts["ffn1_w_down"], wts["w_in"], wts["w_out"], taps,
                            wts["ffn2_w_gate"], wts["ffn2_w_up"], wts["ffn2_w_down"]],
                           [BF16, BF16, BF16, BF16, BF16, F32, BF16, BF16, BF16], name="stage_shards")
    first = _run_phase(_gather_pipelined_phase(staged[:3]), name="gather_ffn1")
    wg1, wu1, wd1 = [_whole(b) for b in first]
    w = dict(ffn1_norm=ffn1_norm, mix_norm=mix_norm, ffn2_norm=ffn2_norm, q_norm=q_norm, k_norm=k_norm,
             conv_b=conv_b, conv_ln_g=conv_ln_g, conv_ln_b=conv_ln_b, wg1=wg1, wu1=wu1, wd1=wd1)
    overlap = _Overlap(staged[3:])
    loss_part, grad_x, g = _local_step(x, loss_target, w, overlap)

    grads, delta, new_m, new_v = {}, {}, {}, {}
    early = overlap.early.reduced
    grads.update(ffn2_w_gate=early["wg2"], ffn2_w_up=early["wu2"], ffn2_w_down=early["wd2"], w_out=early["wout"])
    grads.update(zip(("ffn1_w_gate", "ffn1_w_up", "ffn1_w_down", "w_in"),
                     overlap.finish_late([g["wg1"], g["wu1"], g["wd1"]])))

    small_shapes = [g[n].shape for n in SMALL] + [(SUBLANES, LANES)]
    red = _allreduce_small(_pack([g[n] for n in SMALL] + [loss_part]), name="allreduce_small")
    small = dict(zip(SMALL + ("loss",), _unpack(red, small_shapes)))
    loss = small["loss"][0, 0]
    small["conv_w"] = lax.dynamic_slice_in_dim(small["conv_w"], chip * LANES, LANES, axis=1)

    for tag, names in (("early", ("ffn2_w_gate", "ffn2_w_up", "ffn2_w_down", "w_out")),
                       ("late", ("ffn1_w_gate", "ffn1_w_up", "ffn1_w_down", "w_in"))):
        d, m, v, gr, _ = _adamw_many([wts[n] for n in names], [grads[n] for n in names], [mom[n] for n in names],
                                     [var[n] for n in names], name=f"adamw_{tag}")
        for dst, vals in ((delta, d), (new_m, m), (new_v, v), (grads, gr)):
            dst.update(zip(names, vals))
    shapes = [wts[n].shape for n in SMALL]
    packs = [_pack([src[n] for n in SMALL]) for src in (wts, small, mom, var)]
    outs = _adamw(*packs, name="adamw_small")
    for dst, pk in zip((delta, new_m, new_v), outs):
        dst.update(zip(SMALL, _unpack(pk, shapes)))
    for n in SMALL:
        grads[n] = small[n]

    def shaped(d, n):
        v = d[n].T if n in TRANSPOSED else d[n]
        return v.reshape((1,) + v.shape) if n in BIG or n == "conv_w" else v

    return (loss, grad_x, *[shaped(grads, n) for n in WEIGHTS], *[shaped(delta, n) for n in WEIGHTS],
            *[shaped(new_m, n) for n in WEIGHTS], *[shaped(new_v, n) for n in WEIGHTS])
```
